```python
import jax, jax.numpy as jnp
from jax import lax
import numpy as np

D_MODEL = 1024
BATCH = 1
SEQ = 16384
DEPTH = 1
DEC_BATCH = 16
DEC_SEQ = 32
PAST_LEN = 4096

CHUNK = 64
CONV_DIM = 512
CONV_WIDTH = 3
RWKV_HEAD = 64
RWKV_HEADS = 8
RWKV_DIM = RWKV_HEADS * RWKV_HEAD
LORA_W = 64
LORA_A = 64
LORA_G = 128
SHIFT_DIM = 3 * RWKV_DIM + LORA_W + LORA_A + LORA_G
IN_COLS = 3 * CONV_DIM + SHIFT_DIM + 2 * D_MODEL
N_GROUPS = 4
EXPERTS_PER_GROUP = 8
N_EXPERTS = N_GROUPS * EXPERTS_PER_GROUP
TOP_K = 2
D_EXPERT = 512
MOE_BLOCK = 128
RMS_EPS = 1e-6
GN_EPS = 64e-5

kernel_name = 'hybrid_conv_rwkv7_hmoe_stream'


def rms_norm(x, g):
    xf = x.astype(jnp.float32)
    y = xf * lax.rsqrt(jnp.mean(xf * xf, axis=-1, keepdims=True) + RMS_EPS)
    return (y * g.astype(jnp.float32)).astype(x.dtype)


def wkv_scan(r, decay, k, v, a, b, s0):
    def step(s, inp):
        r_t, w_t, k_t, v_t, a_t, b_t = inp
        sa = jnp.einsum('bhvk,bhk->bhv', s, a_t)
        s = s * w_t[:, :, None, :] + sa[..., None] * b_t[:, :, None, :] + v_t[..., None] * k_t[:, :, None, :]
        return s, jnp.einsum('bhvk,bhk->bhv', s, r_t)
    xs = tuple(jnp.swapaxes(t, 0, 1) for t in (r, decay, k, v, a, b))
    s_final, ys = lax.scan(step, s0, xs)
    return jnp.swapaxes(ys, 0, 1), s_final


def token_mixer(h, st_conv, st_shift, st_wkv, w_in, conv_w, mu_shift, w0, w_lora_w, a0,
                w_lora_a, w_lora_g, k_k, k_a, r_k, lnx_g, lnx_b, w_out_a, w_out_b, w_o):
    bsz, t_len, _ = h.shape
    f32 = jnp.float32
    p = h @ w_in
    p_a = p[..., :3 * CONV_DIM]
    p_b = p[..., 3 * CONV_DIM:3 * CONV_DIM + SHIFT_DIM]
    p_g = p[..., 3 * CONV_DIM + SHIFT_DIM:]

    g_in, g_out, x_c = jnp.split(p_a, 3, axis=-1)
    bx = g_in * x_c
    conv_in = jnp.concatenate([st_conv.astype(bx.dtype), bx], axis=1)
    conv = conv_w[0] * conv_in[:, 0:t_len]
    for j in range(1, CONV_WIDTH):
        conv = conv + conv_w[j] * conv_in[:, j:j + t_len]
    y_a = (g_out * conv) @ w_out_a
    new_conv = conv_in[:, t_len:]

    shift_in = jnp.concatenate([st_shift.astype(p_b.dtype), p_b], axis=1)
    prev = shift_in[:, :t_len]
    s = (p_b + (prev - p_b) * mu_shift).astype(f32)
    new_shift = shift_in[:, t_len:]
    o1, o2, o3 = RWKV_DIM, 2 * RWKV_DIM, 3 * RWKV_DIM
    o4, o5 = o3 + LORA_W, o3 + LORA_W + LORA_A
    r, k, v = s[..., :o1], s[..., o1:o2], s[..., o2:o3]
    lw, la, lg = s[..., o3:o4], s[..., o4:o5], s[..., o5:]
    w_log = -jax.nn.softplus(-(w0.astype(f32) + jnp.tanh(lw) @ w_lora_w.astype(f32))) - 0.5
    decay = jnp.exp(-jnp.exp(w_log))
    a = jax.nn.sigmoid(a0.astype(f32) + la @ w_lora_a.astype(f32))
    g = jax.nn.sigmoid(lg) @ w_lora_g.astype(f32)

    def heads(z):
        return z.reshape(bsz, t_len, RWKV_HEADS, RWKV_HEAD)

    kk = heads(k * k_k.astype(f32))
    kk = kk / jnp.maximum(jnp.sqrt(jnp.sum(kk * kk, axis=-1, keepdims=True)), 1e-12)
    k = k * (1.0 + (a - 1.0) * k_a.astype(f32))
    rh, kh, vh, ah = heads(r), heads(k), heads(v), heads(a)
    y, s_new = wkv_scan(rh, heads(decay), kh, vh, -kk, kk * ah, st_wkv.astype(f32))
    mean = jnp.mean(y, axis=-1, keepdims=True)
    var = jnp.mean(jnp.square(y - mean), axis=-1, keepdims=True)
    y = ((y - mean) * lax.rsqrt(var + GN_EPS)).reshape(bsz, t_len, RWKV_DIM)
    y = y * lnx_g.astype(f32) + lnx_b.astype(f32)
    bonus = jnp.sum(rh * kh * r_k.astype(f32), axis=-1, keepdims=True) * vh
    y = (y + bonus.reshape(bsz, t_len, RWKV_DIM)) * g
    y_b = y.astype(h.dtype) @ w_out_b

    mg_a, mg_b = jnp.split(p_g, 2, axis=-1)
    merged = jax.nn.sigmoid(mg_a) * y_a + jax.nn.sigmoid(mg_b) * y_b
    return merged @ w_o, new_conv, new_shift, s_new.astype(st_wkv.dtype)


def hier_moe(h, w_rg, b_rg, w_re, b_re, w_eg, w_eu, w_ed):
    bsz, t_len, d = h.shape
    hf = h.reshape(-1, d)
    n = hf.shape[0]
    n_assign = n * TOP_K
    p_grp = jax.nn.softmax((hf @ w_rg + b_rg).astype(jnp.float32), axis=-1)
    p_top, grp = lax.top_k(p_grp, 1)
    logits_e = (hf @ w_re + b_re).astype(jnp.float32).reshape(n, N_GROUPS, EXPERTS_PER_GROUP)
    logits_in = jnp.take_along_axis(logits_e, grp[:, :, None], axis=1)[:, 0]
    l_top, i_top = lax.top_k(logits_in, TOP_K)
    p_e = jax.nn.softmax(l_top, axis=-1)
    expert_idx = (grp * EXPERTS_PER_GROUP + i_top).reshape(-1).astype(jnp.int32)
    gate_w = (p_top * p_e).reshape(-1)
    order = jnp.argsort(expert_idx).astype(jnp.int32)
    e_sorted = expert_idx[order]
    tok_sorted = order // TOP_K
    counts = jax.ops.segment_sum(jnp.ones_like(expert_idx), expert_idx, num_segments=N_EXPERTS)
    padded = (counts + MOE_BLOCK - 1) // MOE_BLOCK * MOE_BLOCK
    pad_end = jnp.cumsum(padded)
    pad_start = pad_end - padded
    start = jnp.cumsum(counts) - counts
    dest = pad_start[e_sorted] + jnp.arange(n_assign, dtype=jnp.int32) - start[e_sorted]
    n_blocks = -(-n_assign // MOE_BLOCK) + N_EXPERTS
    rows = n_blocks * MOE_BLOCK
    buf_tok = jnp.full((rows,), n, jnp.int32).at[dest].set(tok_sorted)
    buf_w = jnp.zeros((rows,), jnp.float32).at[dest].set(gate_w[order])
    block_start = jnp.arange(n_blocks, dtype=jnp.int32) * MOE_BLOCK
    block_expert = jnp.minimum(jnp.searchsorted(pad_end, block_start, side='right'),
                               N_EXPERTS - 1).astype(jnp.int32)
    h_pad = jnp.concatenate([hf, jnp.zeros((1, d), hf.dtype)], axis=0)
    xb = h_pad[buf_tok].reshape(n_blocks, MOE_BLOCK, d)

    def expert_block(args):
        x_blk, e = args
        hid = jax.nn.silu(x_blk @ w_eg[e]) * (x_blk @ w_eu[e])
        return hid @ w_ed[e]

    yb = lax.map(expert_block, (xb, block_expert)).reshape(rows, d)
    out = jnp.zeros((n + 1, d), h.dtype).at[buf_tok].add(yb * buf_w[:, None].astype(h.dtype))
    return out[:n].reshape(bsz, t_len, d)


def encoder_trunk(x, st_conv, st_shift, st_wkv, layer_w, normf_g):
    new_c, new_s, new_w = [], [], []
    for l in range(DEPTH):
        (norm1_g, w_in, conv_w, mu_shift, w0, w_lora_w, a0, w_lora_a, w_lora_g, k_k, k_a, r_k,
         lnx_g, lnx_b, w_out_a, w_out_b, w_o, norm2_g, w_rg, b_rg, w_re, b_re,
         w_eg, w_eu, w_ed) = [w[l] for w in layer_w]
        h = rms_norm(x, norm1_g)
        mix, c_new, s_new, wkv_new = token_mixer(
            h, st_conv[l], st_shift[l], st_wkv[l], w_in, conv_w, mu_shift, w0, w_lora_w, a0,
            w_lora_a, w_lora_g, k_k, k_a, r_k, lnx_g, lnx_b, w_out_a, w_out_b, w_o)
        x = x + mix
        x = x + hier_moe(rms_norm(x, norm2_g), w_rg, b_rg, w_re, b_re, w_eg, w_eu, w_ed)
        new_c.append(c_new)
        new_s.append(s_new)
        new_w.append(wkv_new)
    return rms_norm(x, normf_g), jnp.stack(new_c), jnp.stack(new_s), jnp.stack(new_w)


def setup_inputs(seed: int = 0) -> dict:
    key = jax.random.key(seed)
    ks = jax.random.split(key, 32)
    f32 = jnp.float32

    def nrm(k, shape, scale):
        return jax.random.normal(k, shape, f32) * scale

    L, D = DEPTH, D_MODEL
    return {
        'x_prompt': nrm(ks[0], (BATCH, SEQ, D), 1.0),
        'x_sample': nrm(ks[1], (DEC_BATCH, DEC_SEQ, D), 1.0),
        'state_conv': nrm(ks[2], (L, DEC_BATCH, CONV_WIDTH - 1, CONV_DIM), 0.5),
        'state_shift': nrm(ks[3], (L, DEC_BATCH, 1, SHIFT_DIM), 1.0),
        'state_wkv': nrm(ks[4], (L, DEC_BATCH, RWKV_HEADS, RWKV_HEAD, RWKV_HEAD), 0.3),
        'norm1_g': 1.0 + nrm(ks[5], (L, D), 0.02),
        'w_in': nrm(ks[6], (L, D, IN_COLS), D ** -0.5),
        'conv_w': nrm(ks[7], (L, CONV_WIDTH, CONV_DIM), CONV_WIDTH ** -0.5),
        'mu_shift': jax.random.uniform(ks[8], (L, SHIFT_DIM), f32, 0.0, 1.0),
        'w0': jax.random.uniform(ks[9], (L, RWKV_DIM), f32, -2.5, 0.5),
        'w_lora_w': nrm(ks[10], (L, LORA_W, RWKV_DIM), 0.5 * LORA_W ** -0.5),
        'a0': nrm(ks[11], (L, RWKV_DIM), 0.1),
        'w_lora_a': nrm(ks[12], (L, LORA_A, RWKV_DIM), LORA_A ** -0.5),
        'w_lora_g': nrm(ks[13], (L, LORA_G, RWKV_DIM), LORA_G ** -0.5),
        'k_k': 0.85 + nrm(ks[14], (L, RWKV_DIM), 0.02),
        'k_a': 1.0 + nrm(ks[15], (L, RWKV_DIM), 0.02),
        'r_k': nrm(ks[16], (L, RWKV_HEADS, RWKV_HEAD), 0.1),
        'lnx_g': 1.0 + nrm(ks[17], (L, RWKV_DIM), 0.02),
        'lnx_b': nrm(ks[18], (L, RWKV_DIM), 0.02),
        'w_out_a': nrm(ks[19], (L, CONV_DIM, D), CONV_DIM ** -0.5),
        'w_out_b': nrm(ks[20], (L, RWKV_DIM, D), RWKV_DIM ** -0.5),
        'w_o': nrm(ks[21], (L, D, D), D ** -0.5),
        'norm2_g': 1.0 + nrm(ks[22], (L, D), 0.02),
        'w_router_group': nrm(ks[23], (L, D, N_GROUPS), D ** -0.5),
        'b_router_group': nrm(ks[24], (L, N_GROUPS), 0.01),
        'w_router_expert': nrm(ks[25], (L, D, N_EXPERTS), D ** -0.5),
        'b_router_expert': nrm(ks[26], (L, N_EXPERTS), 0.01),
        'w_e_gate': nrm(ks[27], (L, N_EXPERTS, D, D_EXPERT), D ** -0.5),
        'w_e_up': nrm(ks[28], (L, N_EXPERTS, D, D_EXPERT), D ** -0.5),
        'w_e_down': nrm(ks[29], (L, N_EXPERTS, D_EXPERT, D), D_EXPERT ** -0.5),
        'normf_g': 1.0 + nrm(ks[30], (D,), 0.02),
    }


def reference(x_prompt, x_sample, state_conv, state_shift, state_wkv, norm1_g, w_in, conv_w,
              mu_shift, w0, w_lora_w, a0, w_lora_a, w_lora_g, k_k, k_a, r_k, lnx_g, lnx_b,
              w_out_a, w_out_b, w_o, norm2_g, w_router_group, b_router_group, w_router_expert,
              b_router_expert, w_e_gate, w_e_up, w_e_down, normf_g):
    layer_w = (norm1_g, w_in, conv_w, mu_shift, w0, w_lora_w, a0, w_lora_a, w_lora_g, k_k, k_a,
               r_k, lnx_g, lnx_b, w_out_a, w_out_b, w_o, norm2_g, w_router_group, b_router_group,
               w_router_expert, b_router_expert, w_e_gate, w_e_up, w_e_down)
    bp = x_prompt.shape[0]
    zero_conv = jnp.zeros((DEPTH, bp, CONV_WIDTH - 1, CONV_DIM), x_prompt.dtype)
    zero_shift = jnp.zeros((DEPTH, bp, 1, SHIFT_DIM), x_prompt.dtype)
    zero_wkv = jnp.zeros((DEPTH, bp, RWKV_HEADS, RWKV_HEAD, RWKV_HEAD), x_prompt.dtype)
    y_prompt, conv_p, shift_p, wkv_p = encoder_trunk(
        x_prompt, zero_conv, zero_shift, zero_wkv, layer_w, normf_g)
    y_sample, conv_s, shift_s, wkv_s = encoder_trunk(
        x_sample, state_conv, state_shift, state_wkv, layer_w, normf_g)
    return (y_prompt, y_sample, conv_p, shift_p, wkv_p, conv_s, shift_s, wkv_s)
```

```python
import functools

import jax
import jax.numpy as jnp
from jax import lax
from jax.experimental import pallas as pl
from jax.experimental.pallas import tpu as pltpu

F32 = jnp.float32
BF16 = jnp.bfloat16
HIGHEST = lax.Precision.HIGHEST

CONV_DIM = 512
CONV_WIDTH = 3
RWKV_HEAD = 64
RWKV_HEADS = 8
RWKV_DIM = RWKV_HEADS * RWKV_HEAD
LORA_W = 64
LORA_A = 64
LORA_G = 128
SHIFT_DIM = 3 * RWKV_DIM + LORA_W + LORA_A + LORA_G
N_GROUPS = 4
EXPERTS_PER_GROUP = 8
N_EXPERTS = N_GROUPS * EXPERTS_PER_GROUP
TOP_K = 2
RMS_EPS = 1e-6
GN_EPS = 64e-5

LANES = 128
ROW_TILE = 256
SCAN_BLOCK = 256
MOE_ROWS = 256
HIST = 8
VMEM_LIMIT = 56 * 1024 * 1024


def _rms(x, g):
    return x * lax.rsqrt(jnp.mean(x * x, axis=-1, keepdims=True) + RMS_EPS) * g


def _split2_dot(x, ones_bf16):
    hi = x.astype(BF16)
    lo = (x - hi.astype(F32)).astype(BF16)
    return (jnp.dot(hi, ones_bf16, preferred_element_type=F32)
            + jnp.dot(lo, ones_bf16, preferred_element_type=F32))


def _split3_dot(x, ones_bf16):
    p1 = x.astype(BF16)
    r1 = x - p1.astype(F32)
    p2 = r1.astype(BF16)
    p3 = (r1 - p2.astype(F32)).astype(BF16)
    return (jnp.dot(p1, ones_bf16, preferred_element_type=F32)
            + jnp.dot(p2, ones_bf16, preferred_element_type=F32)
            + jnp.dot(p3, ones_bf16, preferred_element_type=F32))


def _const_spec(shape):
    nd = len(shape)
    return pl.BlockSpec(shape, lambda *_: (0,) * nd)


def _mixer_pre_kernel(x_ref, stc_ref, sts_ref, n1_ref, wa_ref, wb_ref, wg_ref, convw_ref, mu_ref,
                      w0_ref, lw_ref, a0_ref, la_ref, lgw_ref, kk_ref, ka_ref, rk_ref, woa_ref,
                      ones_ref,
                      yag_ref, sgb_ref, r_ref, w_ref, k_ref, v_ref, a_ref, b_ref, bonus_ref, g_ref,
                      ctail_ref, stail_ref,
                      cbuf, sbuf, p1buf, p2buf, spbuf, *, n_prompt_tiles, seq_len):
    i = pl.program_id(0)
    tm = x_ref.shape[0]
    seqs = tm // seq_len

    @pl.when(i == 0)
    def _():
        cbuf[0:HIST, :] = jnp.zeros((HIST, CONV_DIM), F32)
        sbuf[0:HIST, :] = jnp.zeros((HIST, SHIFT_DIM), F32)

    h = _rms(x_ref[...], n1_ref[...]).astype(BF16)

    pa = jnp.dot(h, wa_ref[...], preferred_element_type=F32)
    g_in = pa[:, 0:CONV_DIM]
    g_out = pa[:, CONV_DIM:2 * CONV_DIM]
    x_c = pa[:, 2 * CONV_DIM:3 * CONV_DIM]
    bx = g_in * x_c
    cbuf[HIST:HIST + tm, :] = bx
    p1buf[...] = cbuf[HIST - 1:HIST - 1 + tm, :]
    p2buf[...] = cbuf[HIST - 2:HIST - 2 + tm, :]

    @pl.when(i >= n_prompt_tiles)
    def _():
        for j in range(seqs):
            r0 = j * seq_len
            p1buf[r0:r0 + 1, :] = stc_ref[0, 2 * j + 1:2 * j + 2, :]
            p2buf[r0:r0 + 1, :] = stc_ref[0, 2 * j:2 * j + 1, :]
            p2buf[r0 + 1:r0 + 2, :] = stc_ref[0, 2 * j + 1:2 * j + 2, :]

    cw = convw_ref[...]
    conv = cw[0:1, :] * p2buf[...] + cw[1:2, :] * p1buf[...] + cw[2:3, :] * bx
    y_a = jnp.dot((g_out * conv).astype(BF16), woa_ref[...], preferred_element_type=F32)
    for j in range(seqs):
        r1 = HIST + (j + 1) * seq_len
        ctail_ref[0, 2 * j:2 * j + 2, :] = cbuf[r1 - 2:r1, :]
    cbuf[HIST - 2:HIST, :] = cbuf[HIST + tm - 2:HIST + tm, :]

    pg = jnp.dot(h, wg_ref[...], preferred_element_type=F32)
    d = pg.shape[1] // 2
    yag_ref[...] = jax.nn.sigmoid(pg[:, 0:d]) * y_a
    sgb_ref[...] = jax.nn.sigmoid(pg[:, d:2 * d])

    pb = jnp.dot(h, wb_ref[...], preferred_element_type=F32)
    sbuf[HIST:HIST + tm, :] = pb
    spbuf[...] = sbuf[HIST - 1:HIST - 1 + tm, :]

    @pl.when(i >= n_prompt_tiles)
    def _():
        for j in range(seqs):
            r0 = j * seq_len
            spbuf[r0:r0 + 1, :] = sts_ref[0, j:j + 1, :]

    for j in range(seqs):
        r1 = HIST + (j + 1) * seq_len
        stail_ref[0, j:j + 1, :] = sbuf[r1 - 1:r1, :]
    sbuf[HIST - 1:HIST, :] = sbuf[HIST + tm - 1:HIST + tm, :]

    s = pb + (spbuf[...] - pb) * mu_ref[...]
    o1, o2, o3 = RWKV_DIM, 2 * RWKV_DIM, 3 * RWKV_DIM
    r = s[:, 0:o1]
    k = s[:, o1:o2]
    v = s[:, o2:o3]
    s_l = s[:, o3:o3 + LORA_W + LORA_A]
    lg = s[:, o3 + LORA_W + LORA_A:]
    z = w0_ref[...] + jnp.dot(jnp.tanh(s_l), lw_ref[...], precision=HIGHEST,
                              preferred_element_type=F32)
    w_log = -jax.nn.softplus(-z) - 0.5
    decay = jnp.exp(-jnp.exp(w_log))
    a = jax.nn.sigmoid(a0_ref[...] + jnp.dot(s_l, la_ref[...], precision=HIGHEST,
                                             preferred_element_type=F32))
    g = jnp.dot(jax.nn.sigmoid(lg), lgw_ref[...], precision=HIGHEST, preferred_element_type=F32)

    ones = ones_ref[...]
    kk = k * kk_ref[...]
    kk_n = kk / jnp.maximum(jnp.sqrt(_split2_dot(kk * kk, ones)), 1e-12)
    k2 = k * (1.0 + (a - 1.0) * ka_ref[...])
    bonus = _split2_dot(r * k2 * rk_ref[...], ones) * v

    r_ref[...] = r
    w_ref[...] = decay
    k_ref[...] = k2
    v_ref[...] = v
    a_ref[...] = -kk_n
    b_ref[...] = kk_n * a
    bonus_ref[...] = bonus
    g_ref[...] = g


def _mixer_pre(x, st_conv_t, st_shift_t, p, *, n_prompt_tiles, seq_len, tm):
    n_tok, d = x.shape
    n_tiles = n_tok // tm
    seqs = tm // seq_len
    row = lambda w: pl.BlockSpec((tm, w), lambda i: (i, 0))
    st_idx = lambda i: (jnp.maximum(i - n_prompt_tiles, 0), 0, 0)
    consts = [p['norm1_g'], p['w_in_a'], p['w_in_b'], p['w_in_g'], p['conv_w'], p['mu_shift'],
              p['w0'], p['w_lora_w'], p['a0'], p['w_lora_a'], p['w_lora_g'], p['k_k'], p['k_a'],
              p['r_k'], p['w_out_a'], p['ones_bf16']]
    in_specs = [row(d),
                pl.BlockSpec((1, 2 * seqs, CONV_DIM), st_idx),
                pl.BlockSpec((1, seqs, SHIFT_DIM), st_idx)] + [_const_spec(c.shape) for c in consts]
    sds = lambda w: jax.ShapeDtypeStruct((n_tok, w), F32)
    out_shape = [sds(d), sds(d)] + [sds(RWKV_DIM)] * 8 + [
        jax.ShapeDtypeStruct((n_tiles, 2 * seqs, CONV_DIM), F32),
        jax.ShapeDtypeStruct((n_tiles, seqs, SHIFT_DIM), F32)]
    out_specs = [row(d), row(d)] + [row(RWKV_DIM)] * 8 + [
        pl.BlockSpec((1, 2 * seqs, CONV_DIM), lambda i: (i, 0, 0)),
        pl.BlockSpec((1, seqs, SHIFT_DIM), lambda i: (i, 0, 0))]
    kern = functools.partial(_mixer_pre_kernel, n_prompt_tiles=n_prompt_tiles, seq_len=seq_len)
    return pl.pallas_call(
        kern, out_shape=out_shape, grid=(n_tiles,), in_specs=in_specs, out_specs=out_specs,
        scratch_shapes=[pltpu.VMEM((tm + HIST, CONV_DIM), F32), pltpu.VMEM((tm + HIST, SHIFT_DIM), F32),
                        pltpu.VMEM((tm, CONV_DIM), F32), pltpu.VMEM((tm, CONV_DIM), F32),
                        pltpu.VMEM((tm, SHIFT_DIM), F32)],
        compiler_params=pltpu.CompilerParams(dimension_semantics=("arbitrary",),
                                             vmem_limit_bytes=VMEM_LIMIT),
        name="mixer_pre")(x, st_conv_t, st_shift_t, *consts)


def _wkv_scan_kernel(r_ref, w_ref, k_ref, v_ref, a_ref, b_ref, s0_ref, ones_ref, fold_ref,
                     y_ref, sout_ref, state, *, blocks_per_seq):
    i = pl.program_id(0)
    steps = r_ref.shape[0]

    @pl.when(i % blocks_per_seq == 0)
    def _():
        state[...] = s0_ref[0]

    ones = ones_ref[...]
    fold = fold_ref[...]

    def step(t, carry):
        row = pl.ds(t, 1)
        s_prev = state[...]
        sa = _split3_dot(s_prev * a_ref[row, :], ones)
        v_col = _split3_dot(fold * v_ref[row, :], ones)
        s_new = s_prev * w_ref[row, :] + sa * b_ref[row, :] + v_col * k_ref[row, :]
        state[...] = s_new
        y_col = _split3_dot(s_new * r_ref[row, :], ones)
        y_ref[row, :] = jnp.sum(fold * y_col, axis=0, keepdims=True)
        return carry

    lax.fori_loop(0, steps, step, 0)

    @pl.when(i % blocks_per_seq == blocks_per_seq - 1)
    def _():
        sout_ref[0] = state[...]


def _wkv_scan(rwkvab, s0, ones_bf16, fold, *, row_block0, n_blocks, steps, blocks_per_seq, s0_of_seq):
    n_tok = rwkvab[0].shape[0]
    n_seq = n_blocks // blocks_per_seq
    row = pl.BlockSpec((steps, RWKV_DIM), lambda i: (row_block0 + i, 0))
    st = (RWKV_HEAD, RWKV_DIM)
    in_specs = [row] * 6 + [pl.BlockSpec((1,) + st, lambda i: (s0_of_seq(i // blocks_per_seq), 0, 0)),
                            _const_spec(ones_bf16.shape), _const_spec(fold.shape)]
    out_shape = [jax.ShapeDtypeStruct((n_blocks * steps, RWKV_DIM), F32),
                 jax.ShapeDtypeStruct((n_seq,) + st, F32)]
    out_specs = [pl.BlockSpec((steps, RWKV_DIM), lambda i: (i, 0)),
                 pl.BlockSpec((1,) + st, lambda i: (i // blocks_per_seq, 0, 0))]
    del n_tok
    return pl.pallas_call(
        functools.partial(_wkv_scan_kernel, blocks_per_seq=blocks_per_seq),
        out_shape=out_shape, grid=(n_blocks,), in_specs=in_specs, out_specs=out_specs,
        scratch_shapes=[pltpu.VMEM(st, F32)],
        compiler_params=pltpu.CompilerParams(dimension_semantics=("arbitrary",),
                                             vmem_limit_bytes=VMEM_LIMIT),
        name="wkv_scan")(*rwkvab, s0, ones_bf16, fold)


def _mixer_post_kernel(y_ref, bonus_ref, g_ref, yag_ref, sgb_ref, x_ref, lng_ref, lnb_ref, wob_ref,
                       wo_ref, n2_ref, wr_ref, br_ref, ones_ref,
                       x1_ref, h2_ref, route_ref):
    ones = ones_ref[...]
    y = y_ref[...]
    inv_n = 1.0 / RWKV_HEAD
    mean = _split2_dot(y, ones) * inv_n
    yc = y - mean
    var = _split2_dot(yc * yc, ones) * inv_n
    yn = yc * lax.rsqrt(var + GN_EPS) * lng_ref[...] + lnb_ref[...]
    yy = (yn + bonus_ref[...]) * g_ref[...]
    y_b = jnp.dot(yy.astype(BF16), wob_ref[...], preferred_element_type=F32)
    merged = yag_ref[...] + sgb_ref[...] * y_b
    x1 = x_ref[...] + jnp.dot(merged.astype(BF16), wo_ref[...], preferred_element_type=F32)
    x1_ref[...] = x1
    h2 = _rms(x1, n2_ref[...])
    h2_ref[...] = h2

    logits = jnp.dot(h2, wr_ref[...], precision=HIGHEST, preferred_element_type=F32) + br_ref[...]
    lane = lax.broadcasted_iota(jnp.int32, logits.shape, 1)
    neg = jnp.float32(-jnp.inf)
    big = jnp.int32(LANES)
    is_g = lane < N_GROUPS
    lgp = jnp.where(is_g, logits, neg)
    m_g = jnp.max(lgp, axis=-1, keepdims=True)
    grp = jnp.min(jnp.where(lgp == m_g, lane, big), axis=-1, keepdims=True)
    p_top = 1.0 / jnp.sum(jnp.where(is_g, jnp.exp(logits - m_g), 0.0), axis=-1, keepdims=True)
    e_lane = lane - N_GROUPS
    in_grp = (e_lane >= grp * EXPERTS_PER_GROUP) & (e_lane < (grp + 1) * EXPERTS_PER_GROUP)
    le = jnp.where(in_grp, logits, neg)
    m1 = jnp.max(le, axis=-1, keepdims=True)
    i1 = jnp.min(jnp.where(le == m1, lane, big), axis=-1, keepdims=True)
    le2 = jnp.where(lane == i1, neg, le)
    m2 = jnp.max(le2, axis=-1, keepdims=True)
    i2 = jnp.min(jnp.where(le2 == m2, lane, big), axis=-1, keepdims=True)
    ex = jnp.exp(m2 - m1)
    p1 = 1.0 / (1.0 + ex)
    p2 = ex / (1.0 + ex)
    route = jnp.where(lane == 0, (i1 - N_GROUPS).astype(F32),
            jnp.where(lane == 1, (i2 - N_GROUPS).astype(F32),
            jnp.where(lane == 2, p_top * p1,
            jnp.where(lane == 3, p_top * p2, 0.0))))
    route_ref[...] = route


def _mixer_post(y, bonus, g, yag, sgb, x, p, *, tm):
    n_tok, d = x.shape
    row = lambda w: pl.BlockSpec((tm, w), lambda i: (i, 0))
    consts = [p['lnx_g'], p['lnx_b'], p['w_out_b'], p['w_o'], p['norm2_g'], p['w_router'],
              p['b_router'], p['ones_bf16']]
    in_specs = [row(RWKV_DIM)] * 3 + [row(d)] * 3 + [_const_spec(c.shape) for c in consts]
    out_shape = [jax.ShapeDtypeStruct((n_tok, d), F32), jax.ShapeDtypeStruct((n_tok, d), F32),
                 jax.ShapeDtypeStruct((n_tok, LANES), F32)]
    out_specs = [row(d), row(d), row(LANES)]
    return pl.pallas_call(
        _mixer_post_kernel, out_shape=out_shape, grid=(n_tok // tm,), in_specs=in_specs,
        out_specs=out_specs,
        compiler_params=pltpu.CompilerParams(dimension_semantics=("arbitrary",),
                                             vmem_limit_bytes=VMEM_LIMIT),
        name="mixer_post")(y, bonus, g, yag, sgb, x, *consts)


def _gather_rows(idx_ref, src_hbm, dst, sem, n_rows):
    def body(r, carry):
        pltpu.make_async_copy(src_hbm.at[pl.ds(idx_ref[r], 1)], dst.at[pl.ds(r, 1)], sem).start()
        return carry
    lax.fori_loop(0, n_rows, body, 0)


def _wait_rows(src_hbm, dst, sem, n_rows):
    pltpu.make_async_copy(src_hbm.at[pl.ds(0, n_rows)], dst, sem).wait()


def _moe_experts_kernel(bexp_ref, nused_ref, tok_cur_ref, tok_nxt_ref, h_hbm, bw_ref, wg_ref,
                        wu_ref, wd_ref, yb_ref, xbuf, sems):
    del bexp_ref
    i = pl.program_id(0)
    n_blocks = pl.num_programs(0)
    rows = xbuf.shape[1]
    slot = i % 2

    @pl.when(i == 0)
    def _():
        _gather_rows(tok_cur_ref.at[0, 0], h_hbm, xbuf.at[0], sems.at[0], rows)

    @pl.when(i + 1 < n_blocks)
    def _():
        _gather_rows(tok_nxt_ref.at[0, 0], h_hbm, xbuf.at[1 - slot], sems.at[1 - slot], rows)

    _wait_rows(h_hbm, xbuf.at[slot], sems.at[slot], rows)

    @pl.when(i < nused_ref[0])
    def _():
        xb = xbuf[slot].astype(BF16)
        hg = jnp.dot(xb, wg_ref[0], preferred_element_type=F32)
        hu = jnp.dot(xb, wu_ref[0], preferred_element_type=F32)
        hid = (hg * jax.nn.sigmoid(hg)) * hu
        yb = jnp.dot(hid.astype(BF16), wd_ref[0], preferred_element_type=F32)
        yb_ref[...] = yb * bw_ref[...]

    @pl.when(i >= nused_ref[0])
    def _():
        yb_ref[...] = jnp.zeros(yb_ref.shape, F32)


def _moe_experts(h2, buf_tok, buf_w, block_expert, n_used, w_eg, w_eu, w_ed, *, blk):
    n_tok, d = h2.shape
    n_blocks = buf_tok.shape[0] // blk
    de = w_eg.shape[2]
    tok3 = buf_tok.reshape(n_blocks, 1, blk)
    grid_spec = pltpu.PrefetchScalarGridSpec(
        num_scalar_prefetch=2, grid=(n_blocks,),
        in_specs=[
            pl.BlockSpec((1, 1, blk), lambda i, be, nu: (i, 0, 0), memory_space=pltpu.SMEM),
            pl.BlockSpec((1, 1, blk), lambda i, be, nu: (jnp.minimum(i + 1, n_blocks - 1), 0, 0),
                         memory_space=pltpu.SMEM),
            pl.BlockSpec(memory_space=pl.ANY),
            pl.BlockSpec((blk, 1), lambda i, be, nu: (i, 0)),
            pl.BlockSpec((1, d, de), lambda i, be, nu: (be[i], 0, 0)),
            pl.BlockSpec((1, d, de), lambda i, be, nu: (be[i], 0, 0)),
            pl.BlockSpec((1, de, d), lambda i, be, nu: (be[i], 0, 0)),
        ],
        out_specs=pl.BlockSpec((blk, d), lambda i, be, nu: (i, 0)),
        scratch_shapes=[pltpu.VMEM((2, blk, d), F32), pltpu.SemaphoreType.DMA((2,))])
    return pl.pallas_call(
        _moe_experts_kernel, out_shape=jax.ShapeDtypeStruct((n_blocks * blk, d), F32),
        grid_spec=grid_spec,
        compiler_params=pltpu.CompilerParams(dimension_semantics=("arbitrary",),
                                             vmem_limit_bytes=VMEM_LIMIT),
        name="moe_experts")(block_expert, n_used, tok3, tok3, h2, buf_w.reshape(-1, 1), w_eg, w_eu, w_ed)


def _moe_combine_kernel(pos_cur_ref, pos_nxt_ref, x1_ref, nf_ref, yb_hbm, out_ref, ybuf, sems):
    i = pl.program_id(0)
    n_tiles = pl.num_programs(0)
    rows = ybuf.shape[1]
    slot = i % 2

    @pl.when(i == 0)
    def _():
        _gather_rows(pos_cur_ref.at[0, 0], yb_hbm, ybuf.at[0], sems.at[0], rows)

    @pl.when(i + 1 < n_tiles)
    def _():
        _gather_rows(pos_nxt_ref.at[0, 0], yb_hbm, ybuf.at[1 - slot], sems.at[1 - slot], rows)

    _wait_rows(yb_hbm, ybuf.at[slot], sems.at[slot], rows)
    tm = x1_ref.shape[0]
    x2 = x1_ref[...] + ybuf[slot, 0:tm, :] + ybuf[slot, tm:2 * tm, :]
    out_ref[...] = _rms(x2, nf_ref[...])


def _moe_combine(pos, x1, normf_g, yb, *, tm):
    n_tok, d = x1.shape
    n_tiles = n_tok // tm
    pos3 = pos.reshape(n_tiles, tm, TOP_K).transpose(0, 2, 1).reshape(n_tiles, 1, TOP_K * tm)
    in_specs = [
        pl.BlockSpec((1, 1, TOP_K * tm), lambda i: (i, 0, 0), memory_space=pltpu.SMEM),
        pl.BlockSpec((1, 1, TOP_K * tm), lambda i: (jnp.minimum(i + 1, n_tiles - 1), 0, 0),
                     memory_space=pltpu.SMEM),
        pl.BlockSpec((tm, d), lambda i: (i, 0)),
        _const_spec(normf_g.shape),
        pl.BlockSpec(memory_space=pl.ANY),
    ]
    return pl.pallas_call(
        _moe_combine_kernel, out_shape=jax.ShapeDtypeStruct((n_tok, d), F32), grid=(n_tiles,),
        in_specs=in_specs, out_specs=pl.BlockSpec((tm, d), lambda i: (i, 0)),
        scratch_shapes=[pltpu.VMEM((2, TOP_K * tm, d), F32), pltpu.SemaphoreType.DMA((2,))],
        compiler_params=pltpu.CompilerParams(dimension_semantics=("arbitrary",),
                                             vmem_limit_bytes=VMEM_LIMIT),
        name="moe_combine")(pos3, pos3, x1, normf_g, yb)


def _dispatch(route, blk):
    n_tok = route.shape[0]
    n_assign = n_tok * TOP_K
    expert = route[:, 0:TOP_K].astype(jnp.int32).reshape(-1)
    gate_w = route[:, TOP_K:2 * TOP_K].reshape(-1)
    onehot = (expert[:, None] == jnp.arange(N_EXPERTS, dtype=jnp.int32)[None, :]).astype(jnp.int32)
    csum = jnp.cumsum(onehot, axis=0)
    counts = csum[-1]
    rank = jnp.take_along_axis(csum, expert[:, None], axis=1)[:, 0] - 1
    padded = (counts + blk - 1) // blk * blk
    pad_end = jnp.cumsum(padded)
    pad_start = pad_end - padded
    dest = pad_start[expert] + rank
    n_blocks = -(-n_assign // blk) + N_EXPERTS
    rows = n_blocks * blk
    tok = jnp.arange(n_assign, dtype=jnp.int32) // TOP_K
    buf_tok = jnp.zeros((rows,), jnp.int32).at[dest].set(tok)
    buf_w = jnp.zeros((rows,), F32).at[dest].set(gate_w)
    block_start = jnp.arange(n_blocks, dtype=jnp.int32) * blk
    block_expert = jnp.minimum(jnp.searchsorted(pad_end, block_start, side='right'),
                               N_EXPERTS - 1).astype(jnp.int32)
    n_used = (pad_end[-1] // blk).astype(jnp.int32).reshape(1)
    return buf_tok, buf_w, block_expert, n_used, dest.astype(jnp.int32)


def _state_to_kernel(s):
    b = s.shape[0]
    return s.transpose(0, 2, 1, 3).reshape(b, RWKV_HEAD, RWKV_DIM)


def _state_from_kernel(s):
    b = s.shape[0]
    return s.reshape(b, RWKV_HEAD, RWKV_HEADS, RWKV_HEAD).transpose(0, 2, 1, 3)


def kernel(x_prompt, x_sample, state_conv, state_shift, state_wkv, norm1_g, w_in, conv_w, mu_shift, w0, w_lora_w, a0, w_lora_a, w_lora_g, k_k, k_a, r_k, lnx_g, lnx_b, w_out_a, w_out_b, w_o, norm2_g, w_router_group, b_router_group, w_router_expert, b_router_expert, w_e_gate, w_e_up, w_e_down, normf_g):
    depth = norm1_g.shape[0]
    bp, seq, d = x_prompt.shape
    db, dseq, _ = x_sample.shape
    assert depth == 1 and bp == 1, "single layer, single prompt stream"
    tm = ROW_TILE
    n_p, n_s = bp * seq, db * dseq
    assert n_p % tm == 0 and n_s % tm == 0 and tm % dseq == 0
    n_prompt_tiles, n_sample_tiles = n_p // tm, n_s // tm
    seqs = tm // dseq
    n_tok = n_p + n_s

    x = jnp.concatenate([x_prompt.reshape(n_p, d), x_sample.reshape(n_s, d)], axis=0)

    l = 0
    c3 = 3 * CONV_DIM
    head_id = jnp.arange(RWKV_DIM, dtype=jnp.int32) // RWKV_HEAD
    ones_bf16 = (head_id[:, None] == head_id[None, :]).astype(BF16)
    fold = (jnp.arange(RWKV_HEAD, dtype=jnp.int32)[:, None]
            == (jnp.arange(RWKV_DIM, dtype=jnp.int32) % RWKV_HEAD)[None, :]).astype(F32)
    zpad = jnp.zeros((LORA_W, RWKV_DIM), F32)
    n_r = N_GROUPS + N_EXPERTS
    p = {
        'norm1_g': norm1_g[l].reshape(1, d),
        'w_in_a': w_in[l][:, :c3].astype(BF16),
        'w_in_b': w_in[l][:, c3:c3 + SHIFT_DIM].astype(BF16),
        'w_in_g': w_in[l][:, c3 + SHIFT_DIM:].astype(BF16),
        'conv_w': conv_w[l],
        'mu_shift': mu_shift[l].reshape(1, SHIFT_DIM),
        'w0': w0[l].reshape(1, RWKV_DIM),
        'w_lora_w': jnp.concatenate([w_lora_w[l], zpad], axis=0),
        'a0': a0[l].reshape(1, RWKV_DIM),
        'w_lora_a': jnp.concatenate([zpad, w_lora_a[l]], axis=0),
        'w_lora_g': w_lora_g[l],
        'k_k': k_k[l].reshape(1, RWKV_DIM),
        'k_a': k_a[l].reshape(1, RWKV_DIM),
        'r_k': r_k[l].reshape(1, RWKV_DIM),
        'w_out_a': w_out_a[l].astype(BF16),
        'ones_bf16': ones_bf16,
        'lnx_g': lnx_g[l].reshape(1, RWKV_DIM),
        'lnx_b': lnx_b[l].reshape(1, RWKV_DIM),
        'w_out_b': w_out_b[l].astype(BF16),
        'w_o': w_o[l].astype(BF16),
        'norm2_g': norm2_g[l].reshape(1, d),
        'w_router': jnp.pad(jnp.concatenate([w_router_group[l], w_router_expert[l]], axis=1),
                            ((0, 0), (0, LANES - n_r))),
        'b_router': jnp.pad(jnp.concatenate([b_router_group[l], b_router_expert[l]]),
                            (0, LANES - n_r)).reshape(1, LANES),
    }
    st_conv_t = state_conv[l].reshape(n_sample_tiles, seqs * (CONV_WIDTH - 1), CONV_DIM)
    st_shift_t = state_shift[l].reshape(n_sample_tiles, seqs, SHIFT_DIM)

    (yag, sgb, r, w, k, v, a, b, bonus, g, ctail, stail) = _mixer_pre(
        x, st_conv_t, st_shift_t, p, n_prompt_tiles=n_prompt_tiles, seq_len=dseq, tm=tm)

    s0_prompt = jnp.zeros((1, RWKV_HEAD, RWKV_DIM), F32)
    s0_sample = _state_to_kernel(state_wkv[l])
    rwkvab = (r, w, k, v, a, b)
    sb = min(SCAN_BLOCK, n_p)
    assert n_p % sb == 0 and n_p % dseq == 0
    y_p, s_p = _wkv_scan(rwkvab, s0_prompt, ones_bf16, fold, row_block0=0, n_blocks=n_p // sb,
                         steps=sb, blocks_per_seq=n_p // sb, s0_of_seq=lambda q: 0)
    y_s, s_s = _wkv_scan(rwkvab, s0_sample, ones_bf16, fold, row_block0=n_p // dseq, n_blocks=db,
                         steps=dseq, blocks_per_seq=1, s0_of_seq=lambda q: q)
    y = jnp.concatenate([y_p, y_s], axis=0)

    x1, h2, route = _mixer_post(y, bonus, g, yag, sgb, x, p, tm=tm)

    blk = MOE_ROWS
    buf_tok, buf_w, block_expert, n_used, dest = _dispatch(route, blk)
    yb = _moe_experts(h2, buf_tok, buf_w, block_expert, n_used, w_e_gate[l].astype(BF16),
                      w_e_up[l].astype(BF16), w_e_down[l].astype(BF16), blk=blk)
    out = _moe_combine(dest.reshape(n_tok, TOP_K), x1, normf_g.reshape(1, d), yb, tm=tm)

    y_prompt = out[:n_p].reshape(bp, seq, d)
    y_sample = out[n_p:].reshape(db, dseq, d)
    conv_p = ctail[n_prompt_tiles - 1, 2 * (seqs - 1):2 * seqs].reshape(1, bp, CONV_WIDTH - 1, CONV_DIM)
    shift_p = stail[n_prompt_tiles - 1, seqs - 1].reshape(1, bp, 1, SHIFT_DIM)
    wkv_p = _state_from_kernel(s_p).reshape(1, bp, RWKV_HEADS, RWKV_HEAD, RWKV_HEAD)
    conv_s = ctail[n_prompt_tiles:].reshape(1, db, CONV_WIDTH - 1, CONV_DIM)
    shift_s = stail[n_prompt_tiles:].reshape(1, db, 1, SHIFT_DIM)
    wkv_s = _state_from_kernel(s_s).reshape(1, db, RWKV_HEADS, RWKV_HEAD, RWKV_HEAD)
    return (y_prompt, y_sample, conv_p, shift_p, wkv_p, conv_s, shift_s, wkv_s)
```

```python
import functools

import jax
import jax.numpy as jnp
from jax import lax
from jax.experimental import pallas as pl
from jax.experimental.pallas import tpu as pltpu

F32 = jnp.float32
BF16 = jnp.bfloat16
HIGHEST = lax.Precision.HIGHEST

CONV_DIM = 512
CONV_WIDTH = 3
RWKV_HEAD = 64
RWKV_HEADS = 8
RWKV_DIM = RWKV_HEADS * RWKV_HEAD
LORA_W = 64
LORA_A = 64
LORA_G = 128
SHIFT_DIM = 3 * RWKV_DIM + LORA_W + LORA_A + LORA_G
N_GROUPS = 4
EXPERTS_PER_GROUP = 8
N_EXPERTS = N_GROUPS * EXPERTS_PER_GROUP
TOP_K = 2
RMS_EPS = 1e-6
GN_EPS = 64e-5

LANES = 128
ROW_TILE = 256
SCAN_CHUNK = 64
MOE_ROWS = 256
HIST = 8
VMEM_LIMIT = 56 * 1024 * 1024


def _rms(x, g):
    return x * lax.rsqrt(jnp.mean(x * x, axis=-1, keepdims=True) + RMS_EPS) * g


def _split2_dot(x, ones_bf16):
    hi = x.astype(BF16)
    lo = (x - hi.astype(F32)).astype(BF16)
    return (jnp.dot(hi, ones_bf16, preferred_element_type=F32)
            + jnp.dot(lo, ones_bf16, preferred_element_type=F32))


def _const_spec(shape):
    nd = len(shape)
    return pl.BlockSpec(shape, lambda *_: (0,) * nd)


def _mixer_pre_kernel(x_ref, stc_ref, sts_ref, n1_ref, wa_ref, wb_ref, wg_ref, convw_ref, mu_ref,
                      w0_ref, lw_ref, a0_ref, la_ref, lgw_ref, kk_ref, ka_ref, rk_ref, woa_ref,
                      ones_ref,
                      yag_ref, sgb_ref, r_ref, w_ref, k_ref, v_ref, a_ref, b_ref, bonus_ref, g_ref,
                      ctail_ref, stail_ref,
                      cbuf, sbuf, p1buf, p2buf, spbuf, *, n_prompt_tiles, seq_len):
    i = pl.program_id(0)
    tm = x_ref.shape[0]
    seqs = tm // seq_len

    @pl.when(i == 0)
    def _():
        cbuf[0:HIST, :] = jnp.zeros((HIST, CONV_DIM), F32)
        sbuf[0:HIST, :] = jnp.zeros((HIST, SHIFT_DIM), F32)

    h = _rms(x_ref[...], n1_ref[...]).astype(BF16)

    pa = jnp.dot(h, wa_ref[...], preferred_element_type=F32)
    g_in = pa[:, 0:CONV_DIM]
    g_out = pa[:, CONV_DIM:2 * CONV_DIM]
    x_c = pa[:, 2 * CONV_DIM:3 * CONV_DIM]
    bx = g_in * x_c
    cbuf[HIST:HIST + tm, :] = bx
    p1buf[...] = cbuf[HIST - 1:HIST - 1 + tm, :]
    p2buf[...] = cbuf[HIST - 2:HIST - 2 + tm, :]

    @pl.when(i >= n_prompt_tiles)
    def _():
        for j in range(seqs):
            r0 = j * seq_len
            p1buf[r0:r0 + 1, :] = stc_ref[0, 2 * j + 1:2 * j + 2, :]
            p2buf[r0:r0 + 1, :] = stc_ref[0, 2 * j:2 * j + 1, :]
            p2buf[r0 + 1:r0 + 2, :] = stc_ref[0, 2 * j + 1:2 * j + 2, :]

    cw = convw_ref[...]
    conv = cw[0:1, :] * p2buf[...] + cw[1:2, :] * p1buf[...] + cw[2:3, :] * bx
    y_a = jnp.dot((g_out * conv).astype(BF16), woa_ref[...], preferred_element_type=F32)
    for j in range(seqs):
        r1 = HIST + (j + 1) * seq_len
        ctail_ref[0, 2 * j:2 * j + 2, :] = cbuf[r1 - 2:r1, :]
    cbuf[HIST - 2:HIST, :] = cbuf[HIST + tm - 2:HIST + tm, :]

    pg = jnp.dot(h, wg_ref[...], preferred_element_type=F32)
    d = pg.shape[1] // 2
    yag_ref[...] = jax.nn.sigmoid(pg[:, 0:d]) * y_a
    sgb_ref[...] = jax.nn.sigmoid(pg[:, d:2 * d])

    pb = jnp.dot(h, wb_ref[...], preferred_element_type=F32)
    sbuf[HIST:HIST + tm, :] = pb
    spbuf[...] = sbuf[HIST - 1:HIST - 1 + tm, :]

    @pl.when(i >= n_prompt_tiles)
    def _():
        for j in range(seqs):
            r0 = j * seq_len
            spbuf[r0:r0 + 1, :] = sts_ref[0, j:j + 1, :]

    for j in range(seqs):
        r1 = HIST + (j + 1) * seq_len
        stail_ref[0, j:j + 1, :] = sbuf[r1 - 1:r1, :]
    sbuf[HIST - 1:HIST, :] = sbuf[HIST + tm - 1:HIST + tm, :]

    s = pb + (spbuf[...] - pb) * mu_ref[...]
    o1, o2, o3 = RWKV_DIM, 2 * RWKV_DIM, 3 * RWKV_DIM
    r = s[:, 0:o1]
    k = s[:, o1:o2]
    v = s[:, o2:o3]
    s_l = s[:, o3:o3 + LORA_W + LORA_A]
    lg = s[:, o3 + LORA_W + LORA_A:]
    z = w0_ref[...] + jnp.dot(jnp.tanh(s_l), lw_ref[...], precision=HIGHEST,
                              preferred_element_type=F32)
    w_log = -jax.nn.softplus(-z) - 0.5
    log_decay = -jnp.exp(w_log)
    a = jax.nn.sigmoid(a0_ref[...] + jnp.dot(s_l, la_ref[...], precision=HIGHEST,
                                             preferred_element_type=F32))
    g = jnp.dot(jax.nn.sigmoid(lg), lgw_ref[...], precision=HIGHEST, preferred_element_type=F32)

    ones = ones_ref[...]
    kk = k * kk_ref[...]
    kk_n = kk / jnp.maximum(jnp.sqrt(_split2_dot(kk * kk, ones)), 1e-12)
    k2 = k * (1.0 + (a - 1.0) * ka_ref[...])
    bonus = _split2_dot(r * k2 * rk_ref[...], ones) * v

    r_ref[...] = r
    w_ref[...] = log_decay
    k_ref[...] = k2
    v_ref[...] = v
    a_ref[...] = -kk_n
    b_ref[...] = kk_n * a
    bonus_ref[...] = bonus
    g_ref[...] = g


def _mixer_pre(x, st_conv_t, st_shift_t, p, *, n_prompt_tiles, seq_len, tm):
    n_tok, d = x.shape
    n_tiles = n_tok // tm
    seqs = tm // seq_len
    row = lambda w: pl.BlockSpec((tm, w), lambda i: (i, 0))
    st_idx = lambda i: (jnp.maximum(i - n_prompt_tiles, 0), 0, 0)
    consts = [p['norm1_g'], p['w_in_a'], p['w_in_b'], p['w_in_g'], p['conv_w'], p['mu_shift'],
              p['w0'], p['w_lora_w'], p['a0'], p['w_lora_a'], p['w_lora_g'], p['k_k'], p['k_a'],
              p['r_k'], p['w_out_a'], p['ones_bf16']]
    in_specs = [row(d),
                pl.BlockSpec((1, 2 * seqs, CONV_DIM), st_idx),
                pl.BlockSpec((1, seqs, SHIFT_DIM), st_idx)] + [_const_spec(c.shape) for c in consts]
    sds = lambda w: jax.ShapeDtypeStruct((n_tok, w), F32)
    out_shape = [sds(d), sds(d)] + [sds(RWKV_DIM)] * 8 + [
        jax.ShapeDtypeStruct((n_tiles, 2 * seqs, CONV_DIM), F32),
        jax.ShapeDtypeStruct((n_tiles, seqs, SHIFT_DIM), F32)]
    out_specs = [row(d), row(d)] + [row(RWKV_DIM)] * 8 + [
        pl.BlockSpec((1, 2 * seqs, CONV_DIM), lambda i: (i, 0, 0)),
        pl.BlockSpec((1, seqs, SHIFT_DIM), lambda i: (i, 0, 0))]
    kern = functools.partial(_mixer_pre_kernel, n_prompt_tiles=n_prompt_tiles, seq_len=seq_len)
    return pl.pallas_call(
        kern, out_shape=out_shape, grid=(n_tiles,), in_specs=in_specs, out_specs=out_specs,
        scratch_shapes=[pltpu.VMEM((tm + HIST, CONV_DIM), F32), pltpu.VMEM((tm + HIST, SHIFT_DIM), F32),
                        pltpu.VMEM((tm, CONV_DIM), F32), pltpu.VMEM((tm, CONV_DIM), F32),
                        pltpu.VMEM((tm, SHIFT_DIM), F32)],
        compiler_params=pltpu.CompilerParams(dimension_semantics=("arbitrary",),
                                             vmem_limit_bytes=VMEM_LIMIT),
        name="mixer_pre")(x, st_conv_t, st_shift_t, *consts)


GROUP_LANES = 256
HEADS_PER_GROUP = GROUP_LANES // RWKV_HEAD
N_HEAD_GROUPS = RWKV_DIM // GROUP_LANES
NN = (((1,), (0,)), ((), ()))
NT = (((1,), (1,)), ((), ()))


def _split2(x):
    hi = x.astype(BF16)
    lo = (x - hi.astype(F32)).astype(BF16)
    return hi, lo


def _dot3(x, y, dims=NN):
    x1, x2 = _split2(x)
    y1, y2 = _split2(y)
    d = lambda p, q: lax.dot_general(p, q, dims, preferred_element_type=F32)
    return d(x1, y1) + d(x1, y2) + d(x2, y1)


def _wkv_chunk_kernel(r_ref, lw_ref, k_ref, v_ref, a_ref, b_ref, s0_ref, y_ref, sout_ref, state,
                      *, chunks_per_seq):
    i = pl.program_id(0)
    c = r_ref.shape[0]
    hc = HEADS_PER_GROUP * c
    levels = c.bit_length() - 1

    @pl.when(i % chunks_per_seq == 0)
    def _():
        state[...] = s0_ref[0]

    iota = lambda shape, dim: lax.broadcasted_iota(jnp.int32, shape, dim)
    t_idx = iota((c, hc), 0)
    s_idx = iota((c, hc), 1) & (c - 1)
    strict = s_idx < t_idx
    incl = s_idx <= t_idx
    eye = jnp.where(s_idx == t_idx, 1.0, 0.0).astype(F32)
    bd_masks = {w: (iota((hc, w), 0) // c) == (iota((hc, w), 1) // (w // HEADS_PER_GROUP))
                for w in (GROUP_LANES, hc)}
    st_mask = ((iota((GROUP_LANES, GROUP_LANES), 0) // RWKV_HEAD)
               == (iota((GROUP_LANES, GROUP_LANES), 1) // RWKV_HEAD))
    tri = jnp.where(iota((c, c), 1) <= iota((c, c), 0), 1.0, 0.0).astype(BF16)

    def level_mask(lvl):
        half = 1 << (lvl - 1)
        return ((t_idx >> lvl) == (s_idx >> lvl)) & ((t_idx & half) != 0) & ((s_idx & half) == 0)

    def bd(m):
        return jnp.where(bd_masks[m.shape[1]], jnp.concatenate([m] * HEADS_PER_GROUP, axis=0), 0.0)

    def cumsum_rows(x):
        p1 = x.astype(BF16)
        r1 = x - p1.astype(F32)
        p2 = r1.astype(BF16)
        p3 = (r1 - p2.astype(F32)).astype(BF16)
        d = lambda q: jnp.dot(tri, q, preferred_element_type=F32)
        return d(p1) + d(p2) + d(p3)

    for grp in range(N_HEAD_GROUPS):
        sl = slice(grp * GROUP_LANES, (grp + 1) * GROUP_LANES)
        lw = lw_ref[:, sl]
        cum = cumsum_rows(lw)
        cum_last = cum[c - 1:c, :]
        e_neg = jnp.exp(-cum)
        e_end = jnp.exp(cum_last - cum)
        a_t = a_ref[:, sl] * jnp.exp(cum - lw)
        r_t = r_ref[:, sl] * jnp.exp(cum)
        b_raw = b_ref[:, sl]
        k_raw = k_ref[:, sl]
        v = v_ref[:, sl]
        s_prev = state[grp]

        ar = jnp.concatenate([a_t, r_t], axis=0)
        bk = jnp.concatenate([bd(b_raw * e_neg), bd(k_raw * e_neg)], axis=0)
        gram = _dot3(ar, bk, NT)
        a_ab = jnp.where(strict, gram[0:c, 0:hc], 0.0)
        a_ak = jnp.where(strict, gram[0:c, hc:2 * hc], 0.0)
        a_rb = jnp.where(incl, gram[c:2 * c, 0:hc], 0.0)
        a_rk = jnp.where(incl, gram[c:2 * c, hc:2 * hc], 0.0)

        inv = eye + jnp.where(level_mask(1), a_ab, 0.0)
        for lvl in range(2, levels + 1):
            off = jnp.where(level_mask(lvl), a_ab, 0.0)
            inv = inv + _dot3(inv, bd(_dot3(off, bd(inv))))

        x0 = _dot3(ar, s_prev, NT)
        u = _dot3(inv, bd(x0[0:c] + _dot3(a_ak, bd(v))))
        y = x0[c:2 * c] + _dot3(jnp.concatenate([a_rb, a_rk], axis=1),
                                jnp.concatenate([bd(u), bd(v)], axis=0))
        y_ref[:, sl] = y
        uv_t = jnp.transpose(jnp.concatenate([u, v], axis=0))
        upd = _dot3(uv_t, jnp.concatenate([b_raw * e_end, k_raw * e_end], axis=0))
        state[grp] = s_prev * jnp.exp(cum_last) + jnp.where(st_mask, upd, 0.0)

    @pl.when(i % chunks_per_seq == chunks_per_seq - 1)
    def _():
        sout_ref[0] = state[...]


def _wkv_scan(rlkvab, s0, *, row_block0, n_chunks, chunk, chunks_per_seq, s0_of_seq):
    n_seq = n_chunks // chunks_per_seq
    row = pl.BlockSpec((chunk, RWKV_DIM), lambda i: (row_block0 + i, 0))
    st = (N_HEAD_GROUPS, GROUP_LANES, GROUP_LANES)
    in_specs = [row] * 6 + [pl.BlockSpec((1,) + st, lambda i: (s0_of_seq(i // chunks_per_seq), 0, 0, 0))]
    out_shape = [jax.ShapeDtypeStruct((n_chunks * chunk, RWKV_DIM), F32),
                 jax.ShapeDtypeStruct((n_seq,) + st, F32)]
    out_specs = [pl.BlockSpec((chunk, RWKV_DIM), lambda i: (i, 0)),
                 pl.BlockSpec((1,) + st, lambda i: (i // chunks_per_seq, 0, 0, 0))]
    return pl.pallas_call(
        functools.partial(_wkv_chunk_kernel, chunks_per_seq=chunks_per_seq),
        out_shape=out_shape, grid=(n_chunks,), in_specs=in_specs, out_specs=out_specs,
        scratch_shapes=[pltpu.VMEM(st, F32)],
        compiler_params=pltpu.CompilerParams(dimension_semantics=("arbitrary",),
                                             vmem_limit_bytes=VMEM_LIMIT),
        name="wkv_scan")(*rlkvab, s0)


def _mixer_post_kernel(y_ref, bonus_ref, g_ref, yag_ref, sgb_ref, x_ref, lng_ref, lnb_ref, wob_ref,
                       wo_ref, n2_ref, wr_ref, br_ref, ones_ref,
                       x1_ref, h2_ref, route_ref):
    ones = ones_ref[...]
    y = y_ref[...]
    inv_n = 1.0 / RWKV_HEAD
    mean = _split2_dot(y, ones) * inv_n
    yc = y - mean
    var = _split2_dot(yc * yc, ones) * inv_n
    yn = yc * lax.rsqrt(var + GN_EPS) * lng_ref[...] + lnb_ref[...]
    yy = (yn + bonus_ref[...]) * g_ref[...]
    y_b = jnp.dot(yy.astype(BF16), wob_ref[...], preferred_element_type=F32)
    merged = yag_ref[...] + sgb_ref[...] * y_b
    x1 = x_ref[...] + jnp.dot(merged.astype(BF16), wo_ref[...], preferred_element_type=F32)
    x1_ref[...] = x1
    h2 = _rms(x1, n2_ref[...])
    h2_ref[...] = h2

    logits = jnp.dot(h2, wr_ref[...], precision=HIGHEST, preferred_element_type=F32) + br_ref[...]
    lane = lax.broadcasted_iota(jnp.int32, logits.shape, 1)
    neg = jnp.float32(-jnp.inf)
    big = jnp.int32(LANES)
    is_g = lane < N_GROUPS
    lgp = jnp.where(is_g, logits, neg)
    m_g = jnp.max(lgp, axis=-1, keepdims=True)
    grp = jnp.min(jnp.where(lgp == m_g, lane, big), axis=-1, keepdims=True)
    p_top = 1.0 / jnp.sum(jnp.where(is_g, jnp.exp(logits - m_g), 0.0), axis=-1, keepdims=True)
    e_lane = lane - N_GROUPS
    in_grp = (e_lane >= grp * EXPERTS_PER_GROUP) & (e_lane < (grp + 1) * EXPERTS_PER_GROUP)
    le = jnp.where(in_grp, logits, neg)
    m1 = jnp.max(le, axis=-1, keepdims=True)
    i1 = jnp.min(jnp.where(le == m1, lane, big), axis=-1, keepdims=True)
    le2 = jnp.where(lane == i1, neg, le)
    m2 = jnp.max(le2, axis=-1, keepdims=True)
    i2 = jnp.min(jnp.where(le2 == m2, lane, big), axis=-1, keepdims=True)
    ex = jnp.exp(m2 - m1)
    p1 = 1.0 / (1.0 + ex)
    p2 = ex / (1.0 + ex)
    route = jnp.where(lane == 0, (i1 - N_GROUPS).astype(F32),
            jnp.where(lane == 1, (i2 - N_GROUPS).astype(F32),
            jnp.where(lane == 2, p_top * p1,
            jnp.where(lane == 3, p_top * p2, 0.0))))
    route_ref[...] = route


def _mixer_post(y, bonus, g, yag, sgb, x, p, *, tm):
    n_tok, d = x.shape
    row = lambda w: pl.BlockSpec((tm, w), lambda i: (i, 0))
    consts = [p['lnx_g'], p['lnx_b'], p['w_out_b'], p['w_o'], p['norm2_g'], p['w_router'],
              p['b_router'], p['ones_bf16']]
    in_specs = [row(RWKV_DIM)] * 3 + [row(d)] * 3 + [_const_spec(c.shape) for c in consts]
    out_shape = [jax.ShapeDtypeStruct((n_tok, d), F32), jax.ShapeDtypeStruct((n_tok, d), F32),
                 jax.ShapeDtypeStruct((n_tok, LANES), F32)]
    out_specs = [row(d), row(d), row(LANES)]
    return pl.pallas_call(
        _mixer_post_kernel, out_shape=out_shape, grid=(n_tok // tm,), in_specs=in_specs,
        out_specs=out_specs,
        compiler_params=pltpu.CompilerParams(dimension_semantics=("arbitrary",),
                                             vmem_limit_bytes=VMEM_LIMIT),
        name="mixer_post")(y, bonus, g, yag, sgb, x, *consts)


def _gather_rows(idx_ref, src_hbm, dst, sem, n_rows):
    def body(r, carry):
        pltpu.make_async_copy(src_hbm.at[pl.ds(idx_ref[r], 1)], dst.at[pl.ds(r, 1)], sem).start()
        return carry
    lax.fori_loop(0, n_rows, body, 0)


def _wait_rows(src_hbm, dst, sem, n_rows):
    pltpu.make_async_copy(src_hbm.at[pl.ds(0, n_rows)], dst, sem).wait()


def _moe_experts_kernel(bexp_ref, nused_ref, tok_cur_ref, tok_nxt_ref, h_hbm, bw_ref, wg_ref,
                        wu_ref, wd_ref, yb_ref, xbuf, sems):
    del bexp_ref
    i = pl.program_id(0)
    n_blocks = pl.num_programs(0)
    rows = xbuf.shape[1]
    slot = i % 2

    @pl.when(i == 0)
    def _():
        _gather_rows(tok_cur_ref.at[0, 0], h_hbm, xbuf.at[0], sems.at[0], rows)

    @pl.when(i + 1 < n_blocks)
    def _():
        _gather_rows(tok_nxt_ref.at[0, 0], h_hbm, xbuf.at[1 - slot], sems.at[1 - slot], rows)

    _wait_rows(h_hbm, xbuf.at[slot], sems.at[slot], rows)

    @pl.when(i < nused_ref[0])
    def _():
        xb = xbuf[slot].astype(BF16)
        hg = jnp.dot(xb, wg_ref[0], preferred_element_type=F32)
        hu = jnp.dot(xb, wu_ref[0], preferred_element_type=F32)
        hid = (hg * jax.nn.sigmoid(hg)) * hu
        yb = jnp.dot(hid.astype(BF16), wd_ref[0], preferred_element_type=F32)
        yb_ref[...] = yb * bw_ref[...]

    @pl.when(i >= nused_ref[0])
    def _():
        yb_ref[...] = jnp.zeros(yb_ref.shape, F32)


def _moe_experts(h2, buf_tok, buf_w, block_expert, n_used, w_eg, w_eu, w_ed, *, blk):
    n_tok, d = h2.shape
    n_blocks = buf_tok.shape[0] // blk
    de = w_eg.shape[2]
    tok3 = buf_tok.reshape(n_blocks, 1, blk)
    grid_spec = pltpu.PrefetchScalarGridSpec(
        num_scalar_prefetch=2, grid=(n_blocks,),
        in_specs=[
            pl.BlockSpec((1, 1, blk), lambda i, be, nu: (i, 0, 0), memory_space=pltpu.SMEM),
            pl.BlockSpec((1, 1, blk), lambda i, be, nu: (jnp.minimum(i + 1, n_blocks - 1), 0, 0),
                         memory_space=pltpu.SMEM),
            pl.BlockSpec(memory_space=pl.ANY),
            pl.BlockSpec((blk, 1), lambda i, be, nu: (i, 0)),
            pl.BlockSpec((1, d, de), lambda i, be, nu: (be[i], 0, 0)),
            pl.BlockSpec((1, d, de), lambda i, be, nu: (be[i], 0, 0)),
            pl.BlockSpec((1, de, d), lambda i, be, nu: (be[i], 0, 0)),
        ],
        out_specs=pl.BlockSpec((blk, d), lambda i, be, nu: (i, 0)),
        scratch_shapes=[pltpu.VMEM((2, blk, d), F32), pltpu.SemaphoreType.DMA((2,))])
    return pl.pallas_call(
        _moe_experts_kernel, out_shape=jax.ShapeDtypeStruct((n_blocks * blk, d), F32),
        grid_spec=grid_spec,
        compiler_params=pltpu.CompilerParams(dimension_semantics=("arbitrary",),
                                             vmem_limit_bytes=VMEM_LIMIT),
        name="moe_experts")(block_expert, n_used, tok3, tok3, h2, buf_w.reshape(-1, 1), w_eg, w_eu, w_ed)


def _moe_combine_kernel(pos_cur_ref, pos_nxt_ref, x1_ref, nf_ref, yb_hbm, out_ref, ybuf, sems):
    i = pl.program_id(0)
    n_tiles = pl.num_programs(0)
    rows = ybuf.shape[1]
    slot = i % 2

    @pl.when(i == 0)
    def _():
        _gather_rows(pos_cur_ref.at[0, 0], yb_hbm, ybuf.at[0], sems.at[0], rows)

    @pl.when(i + 1 < n_tiles)
    def _():
        _gather_rows(pos_nxt_ref.at[0, 0], yb_hbm, ybuf.at[1 - slot], sems.at[1 - slot], rows)

    _wait_rows(yb_hbm, ybuf.at[slot], sems.at[slot], rows)
    tm = x1_ref.shape[0]
    x2 = x1_ref[...] + ybuf[slot, 0:tm, :] + ybuf[slot, tm:2 * tm, :]
    out_ref[...] = _rms(x2, nf_ref[...])


def _moe_combine(pos, x1, normf_g, yb, *, tm):
    n_tok, d = x1.shape
    n_tiles = n_tok // tm
    pos3 = pos.reshape(n_tiles, tm, TOP_K).transpose(0, 2, 1).reshape(n_tiles, 1, TOP_K * tm)
    in_specs = [
        pl.BlockSpec((1, 1, TOP_K * tm), lambda i: (i, 0, 0), memory_space=pltpu.SMEM),
        pl.BlockSpec((1, 1, TOP_K * tm), lambda i: (jnp.minimum(i + 1, n_tiles - 1), 0, 0),
                     memory_space=pltpu.SMEM),
        pl.BlockSpec((tm, d), lambda i: (i, 0)),
        _const_spec(normf_g.shape),
        pl.BlockSpec(memory_space=pl.ANY),
    ]
    return pl.pallas_call(
        _moe_combine_kernel, out_shape=jax.ShapeDtypeStruct((n_tok, d), F32), grid=(n_tiles,),
        in_specs=in_specs, out_specs=pl.BlockSpec((tm, d), lambda i: (i, 0)),
        scratch_shapes=[pltpu.VMEM((2, TOP_K * tm, d), F32), pltpu.SemaphoreType.DMA((2,))],
        compiler_params=pltpu.CompilerParams(dimension_semantics=("arbitrary",),
                                             vmem_limit_bytes=VMEM_LIMIT),
        name="moe_combine")(pos3, pos3, x1, normf_g, yb)


def _dispatch(route, blk):
    n_tok = route.shape[0]
    n_assign = n_tok * TOP_K
    expert = route[:, 0:TOP_K].astype(jnp.int32).reshape(-1)
    gate_w = route[:, TOP_K:2 * TOP_K].reshape(-1)
    onehot = (expert[:, None] == jnp.arange(N_EXPERTS, dtype=jnp.int32)[None, :]).astype(jnp.int32)
    csum = jnp.cumsum(onehot, axis=0)
    counts = csum[-1]
    rank = jnp.take_along_axis(csum, expert[:, None], axis=1)[:, 0] - 1
    padded = (counts + blk - 1) // blk * blk
    pad_end = jnp.cumsum(padded)
    pad_start = pad_end - padded
    dest = pad_start[expert] + rank
    n_blocks = -(-n_assign // blk) + N_EXPERTS
    rows = n_blocks * blk
    tok = jnp.arange(n_assign, dtype=jnp.int32) // TOP_K
    buf_tok = jnp.zeros((rows,), jnp.int32).at[dest].set(tok)
    buf_w = jnp.zeros((rows,), F32).at[dest].set(gate_w)
    block_start = jnp.arange(n_blocks, dtype=jnp.int32) * blk
    block_expert = jnp.minimum(jnp.searchsorted(pad_end, block_start, side='right'),
                               N_EXPERTS - 1).astype(jnp.int32)
    n_used = (pad_end[-1] // blk).astype(jnp.int32).reshape(1)
    return buf_tok, buf_w, block_expert, n_used, dest.astype(jnp.int32)


def _state_to_kernel(s):
    b = s.shape[0]
    s6 = s.reshape(b, N_HEAD_GROUPS, HEADS_PER_GROUP, RWKV_HEAD, 1, RWKV_HEAD)
    eye = jnp.eye(HEADS_PER_GROUP, dtype=s.dtype).reshape(1, 1, HEADS_PER_GROUP, 1, HEADS_PER_GROUP, 1)
    return (s6 * eye).reshape(b, N_HEAD_GROUPS, GROUP_LANES, GROUP_LANES)


def _state_from_kernel(s):
    b = s.shape[0]
    s6 = s.reshape(b, N_HEAD_GROUPS, HEADS_PER_GROUP, RWKV_HEAD, HEADS_PER_GROUP, RWKV_HEAD)
    diag = jnp.stack([s6[:, :, h, :, h, :] for h in range(HEADS_PER_GROUP)], axis=2)
    return diag.reshape(b, RWKV_HEADS, RWKV_HEAD, RWKV_HEAD)


def kernel(x_prompt, x_sample, state_conv, state_shift, state_wkv, norm1_g, w_in, conv_w, mu_shift, w0, w_lora_w, a0, w_lora_a, w_lora_g, k_k, k_a, r_k, lnx_g, lnx_b, w_out_a, w_out_b, w_o, norm2_g, w_router_group, b_router_group, w_router_expert, b_router_expert, w_e_gate, w_e_up, w_e_down, normf_g):
    depth = norm1_g.shape[0]
    bp, seq, d = x_prompt.shape
    db, dseq, _ = x_sample.shape
    assert depth == 1 and bp == 1, "single layer, single prompt stream"
    tm = ROW_TILE
    n_p, n_s = bp * seq, db * dseq
    assert n_p % tm == 0 and n_s % tm == 0 and tm % dseq == 0
    n_prompt_tiles, n_sample_tiles = n_p // tm, n_s // tm
    seqs = tm // dseq
    n_tok = n_p + n_s

    x = jnp.concatenate([x_prompt.reshape(n_p, d), x_sample.reshape(n_s, d)], axis=0)

    l = 0
    c3 = 3 * CONV_DIM
    head_id = jnp.arange(RWKV_DIM, dtype=jnp.int32) // RWKV_HEAD
    ones_bf16 = (head_id[:, None] == head_id[None, :]).astype(BF16)
    zpad = jnp.zeros((LORA_W, RWKV_DIM), F32)
    n_r = N_GROUPS + N_EXPERTS
    p = {
        'norm1_g': norm1_g[l].reshape(1, d),
        'w_in_a': w_in[l][:, :c3].astype(BF16),
        'w_in_b': w_in[l][:, c3:c3 + SHIFT_DIM].astype(BF16),
        'w_in_g': w_in[l][:, c3 + SHIFT_DIM:].astype(BF16),
        'conv_w': conv_w[l],
        'mu_shift': mu_shift[l].reshape(1, SHIFT_DIM),
        'w0': w0[l].reshape(1, RWKV_DIM),
        'w_lora_w': jnp.concatenate([w_lora_w[l], zpad], axis=0),
        'a0': a0[l].reshape(1, RWKV_DIM),
        'w_lora_a': jnp.concatenate([zpad, w_lora_a[l]], axis=0),
        'w_lora_g': w_lora_g[l],
        'k_k': k_k[l].reshape(1, RWKV_DIM),
        'k_a': k_a[l].reshape(1, RWKV_DIM),
        'r_k': r_k[l].reshape(1, RWKV_DIM),
        'w_out_a': w_out_a[l].astype(BF16),
        'ones_bf16': ones_bf16,
        'lnx_g': lnx_g[l].reshape(1, RWKV_DIM),
        'lnx_b': lnx_b[l].reshape(1, RWKV_DIM),
        'w_out_b': w_out_b[l].astype(BF16),
        'w_o': w_o[l].astype(BF16),
        'norm2_g': norm2_g[l].reshape(1, d),
        'w_router': jnp.pad(jnp.concatenate([w_router_group[l], w_router_expert[l]], axis=1),
                            ((0, 0), (0, LANES - n_r))),
        'b_router': jnp.pad(jnp.concatenate([b_router_group[l], b_router_expert[l]]),
                            (0, LANES - n_r)).reshape(1, LANES),
    }
    st_conv_t = state_conv[l].reshape(n_sample_tiles, seqs * (CONV_WIDTH - 1), CONV_DIM)
    st_shift_t = state_shift[l].reshape(n_sample_tiles, seqs, SHIFT_DIM)

    (yag, sgb, r, w, k, v, a, b, bonus, g, ctail, stail) = _mixer_pre(
        x, st_conv_t, st_shift_t, p, n_prompt_tiles=n_prompt_tiles, seq_len=dseq, tm=tm)

    s0_prompt = jnp.zeros((1, N_HEAD_GROUPS, GROUP_LANES, GROUP_LANES), F32)
    s0_sample = _state_to_kernel(state_wkv[l])
    rlkvab = (r, w, k, v, a, b)
    cp = min(SCAN_CHUNK, n_p)
    assert n_p % cp == 0 and n_p % dseq == 0 and cp & (cp - 1) == 0 and dseq & (dseq - 1) == 0
    y_p, s_p = _wkv_scan(rlkvab, s0_prompt, row_block0=0, n_chunks=n_p // cp, chunk=cp,
                         chunks_per_seq=n_p // cp, s0_of_seq=lambda q: 0)
    y_s, s_s = _wkv_scan(rlkvab, s0_sample, row_block0=n_p // dseq, n_chunks=db, chunk=dseq,
                         chunks_per_seq=1, s0_of_seq=lambda q: q)
    y = jnp.concatenate([y_p, y_s], axis=0)

    x1, h2, route = _mixer_post(y, bonus, g, yag, sgb, x, p, tm=tm)

    blk = MOE_ROWS
    buf_tok, buf_w, block_expert, n_used, dest = _dispatch(route, blk)
    yb = _moe_experts(h2, buf_tok, buf_w, block_expert, n_used, w_e_gate[l].astype(BF16),
                      w_e_up[l].astype(BF16), w_e_down[l].astype(BF16), blk=blk)
    out = _moe_combine(dest.reshape(n_tok, TOP_K), x1, normf_g.reshape(1, d), yb, tm=tm)

    y_prompt = out[:n_p].reshape(bp, seq, d)
    y_sample = out[n_p:].reshape(db, dseq, d)
    conv_p = ctail[n_prompt_tiles - 1, 2 * (seqs - 1):2 * seqs].reshape(1, bp, CONV_WIDTH - 1, CONV_DIM)
    shift_p = stail[n_prompt_tiles - 1, seqs - 1].reshape(1, bp, 1, SHIFT_DIM)
    wkv_p = _state_from_kernel(s_p).reshape(1, bp, RWKV_HEADS, RWKV_HEAD, RWKV_HEAD)
    conv_s = ctail[n_prompt_tiles:].reshape(1, db, CONV_WIDTH - 1, CONV_DIM)
    shift_s = stail[n_prompt_tiles:].reshape(1, db, 1, SHIFT_DIM)
    wkv_s = _state_from_kernel(s_s).reshape(1, db, RWKV_HEADS, RWKV_HEAD, RWKV_HEAD)
    return (y_prompt, y_sample, conv_p, shift_p, wkv_p, conv_s, shift_s, wkv_s)
```

```python
import functools

import numpy as np
import jax
import jax.numpy as jnp
from jax import lax
from jax.experimental import pallas as pl
from jax.experimental.pallas import tpu as pltpu

F32 = jnp.float32
BF16 = jnp.bfloat16
HIGHEST = lax.Precision.HIGHEST

CONV_DIM = 512
CONV_WIDTH = 3
RWKV_HEAD = 64
RWKV_HEADS = 8
RWKV_DIM = RWKV_HEADS * RWKV_HEAD
LORA_W = 64
LORA_A = 64
LORA_G = 128
SHIFT_DIM = 3 * RWKV_DIM + LORA_W + LORA_A + LORA_G
N_GROUPS = 4
EXPERTS_PER_GROUP = 8
N_EXPERTS = N_GROUPS * EXPERTS_PER_GROUP
TOP_K = 2
RMS_EPS = 1e-6
GN_EPS = 64e-5

LANES = 128
ROW_TILE = 256
SCAN_CHUNK = 64
SCAN_CHUNKS_PER_STEP = 2
MOE_ROWS = 256
HIST = 8
VMEM_LIMIT = 56 * 1024 * 1024


def _rms(x, g):
    return x * lax.rsqrt(jnp.mean(x * x, axis=-1, keepdims=True) + RMS_EPS) * g


def _split2_dot(x, ones_bf16):
    hi = x.astype(BF16)
    lo = (x - hi.astype(F32)).astype(BF16)
    return (jnp.dot(hi, ones_bf16, preferred_element_type=F32)
            + jnp.dot(lo, ones_bf16, preferred_element_type=F32))


def _const_spec(shape):
    nd = len(shape)
    return pl.BlockSpec(shape, lambda *_: (0,) * nd)


def _mixer_pre_kernel(x_ref, stc_ref, sts_ref, n1_ref, wa_ref, wb_ref, wg_ref, convw_ref, mu_ref,
                      w0_ref, lw_ref, a0_ref, la_ref, lgw_ref, kk_ref, ka_ref, rk_ref, woa_ref,
                      ones_ref,
                      yag_ref, sgb_ref, r_ref, w_ref, k_ref, v_ref, a_ref, b_ref, bonus_ref, g_ref,
                      ctail_ref, stail_ref,
                      cbuf, sbuf, p1buf, p2buf, spbuf, *, n_prompt_tiles, seq_len):
    i = pl.program_id(0)
    tm = x_ref.shape[0]
    seqs = tm // seq_len

    @pl.when(i == 0)
    def _():
        cbuf[0:HIST, :] = jnp.zeros((HIST, CONV_DIM), F32)
        sbuf[0:HIST, :] = jnp.zeros((HIST, SHIFT_DIM), F32)

    h = _rms(x_ref[...], n1_ref[...]).astype(BF16)

    pa = jnp.dot(h, wa_ref[...], preferred_element_type=F32)
    g_in = pa[:, 0:CONV_DIM]
    g_out = pa[:, CONV_DIM:2 * CONV_DIM]
    x_c = pa[:, 2 * CONV_DIM:3 * CONV_DIM]
    bx = g_in * x_c
    cbuf[HIST:HIST + tm, :] = bx
    p1buf[...] = cbuf[HIST - 1:HIST - 1 + tm, :]
    p2buf[...] = cbuf[HIST - 2:HIST - 2 + tm, :]

    @pl.when(i >= n_prompt_tiles)
    def _():
        for j in range(seqs):
            r0 = j * seq_len
            p1buf[r0:r0 + 1, :] = stc_ref[0, 2 * j + 1:2 * j + 2, :]
            p2buf[r0:r0 + 1, :] = stc_ref[0, 2 * j:2 * j + 1, :]
            p2buf[r0 + 1:r0 + 2, :] = stc_ref[0, 2 * j + 1:2 * j + 2, :]

    cw = convw_ref[...]
    conv = cw[0:1, :] * p2buf[...] + cw[1:2, :] * p1buf[...] + cw[2:3, :] * bx
    y_a = jnp.dot((g_out * conv).astype(BF16), woa_ref[...], preferred_element_type=F32)
    for j in range(seqs):
        r1 = HIST + (j + 1) * seq_len
        ctail_ref[0, 2 * j:2 * j + 2, :] = cbuf[r1 - 2:r1, :]
    cbuf[HIST - 2:HIST, :] = cbuf[HIST + tm - 2:HIST + tm, :]

    pg = jnp.dot(h, wg_ref[...], preferred_element_type=F32)
    d = pg.shape[1] // 2
    yag_ref[...] = jax.nn.sigmoid(pg[:, 0:d]) * y_a
    sgb_ref[...] = jax.nn.sigmoid(pg[:, d:2 * d])

    pb = jnp.dot(h, wb_ref[...], preferred_element_type=F32)
    sbuf[HIST:HIST + tm, :] = pb
    spbuf[...] = sbuf[HIST - 1:HIST - 1 + tm, :]

    @pl.when(i >= n_prompt_tiles)
    def _():
        for j in range(seqs):
            r0 = j * seq_len
            spbuf[r0:r0 + 1, :] = sts_ref[0, j:j + 1, :]

    for j in range(seqs):
        r1 = HIST + (j + 1) * seq_len
        stail_ref[0, j:j + 1, :] = sbuf[r1 - 1:r1, :]
    sbuf[HIST - 1:HIST, :] = sbuf[HIST + tm - 1:HIST + tm, :]

    s = pb + (spbuf[...] - pb) * mu_ref[...]
    o1, o2, o3 = RWKV_DIM, 2 * RWKV_DIM, 3 * RWKV_DIM
    r = s[:, 0:o1]
    k = s[:, o1:o2]
    v = s[:, o2:o3]
    s_l = s[:, o3:o3 + LORA_W + LORA_A]
    lg = s[:, o3 + LORA_W + LORA_A:]
    z = w0_ref[...] + jnp.dot(jnp.tanh(s_l), lw_ref[...], precision=HIGHEST,
                              preferred_element_type=F32)
    w_log = -jax.nn.softplus(-z) - 0.5
    log_decay = -jnp.exp(w_log)
    a = jax.nn.sigmoid(a0_ref[...] + jnp.dot(s_l, la_ref[...], precision=HIGHEST,
                                             preferred_element_type=F32))
    g = jnp.dot(jax.nn.sigmoid(lg), lgw_ref[...], precision=HIGHEST, preferred_element_type=F32)

    ones = ones_ref[...]
    kk = k * kk_ref[...]
    kk_n = kk / jnp.maximum(jnp.sqrt(_split2_dot(kk * kk, ones)), 1e-12)
    k2 = k * (1.0 + (a - 1.0) * ka_ref[...])
    bonus = _split2_dot(r * k2 * rk_ref[...], ones) * v

    r_ref[...] = r
    w_ref[...] = log_decay
    k_ref[...] = k2
    v_ref[...] = v
    a_ref[...] = -kk_n
    b_ref[...] = kk_n * a
    bonus_ref[...] = bonus
    g_ref[...] = g


def _mixer_pre(x, st_conv_t, st_shift_t, p, *, n_prompt_tiles, seq_len, tm):
    n_tok, d = x.shape
    n_tiles = n_tok // tm
    seqs = tm // seq_len
    row = lambda w: pl.BlockSpec((tm, w), lambda i: (i, 0))
    st_idx = lambda i: (jnp.maximum(i - n_prompt_tiles, 0), 0, 0)
    consts = [p['norm1_g'], p['w_in_a'], p['w_in_b'], p['w_in_g'], p['conv_w'], p['mu_shift'],
              p['w0'], p['w_lora_w'], p['a0'], p['w_lora_a'], p['w_lora_g'], p['k_k'], p['k_a'],
              p['r_k'], p['w_out_a'], p['ones_bf16']]
    in_specs = [row(d),
                pl.BlockSpec((1, 2 * seqs, CONV_DIM), st_idx),
                pl.BlockSpec((1, seqs, SHIFT_DIM), st_idx)] + [_const_spec(c.shape) for c in consts]
    sds = lambda w: jax.ShapeDtypeStruct((n_tok, w), F32)
    out_shape = [sds(d), sds(d)] + [sds(RWKV_DIM)] * 8 + [
        jax.ShapeDtypeStruct((n_tiles, 2 * seqs, CONV_DIM), F32),
        jax.ShapeDtypeStruct((n_tiles, seqs, SHIFT_DIM), F32)]
    out_specs = [row(d), row(d)] + [row(RWKV_DIM)] * 8 + [
        pl.BlockSpec((1, 2 * seqs, CONV_DIM), lambda i: (i, 0, 0)),
        pl.BlockSpec((1, seqs, SHIFT_DIM), lambda i: (i, 0, 0))]
    kern = functools.partial(_mixer_pre_kernel, n_prompt_tiles=n_prompt_tiles, seq_len=seq_len)
    return pl.pallas_call(
        kern, out_shape=out_shape, grid=(n_tiles,), in_specs=in_specs, out_specs=out_specs,
        scratch_shapes=[pltpu.VMEM((tm + HIST, CONV_DIM), F32), pltpu.VMEM((tm + HIST, SHIFT_DIM), F32),
                        pltpu.VMEM((tm, CONV_DIM), F32), pltpu.VMEM((tm, CONV_DIM), F32),
                        pltpu.VMEM((tm, SHIFT_DIM), F32)],
        compiler_params=pltpu.CompilerParams(dimension_semantics=("arbitrary",),
                                             vmem_limit_bytes=VMEM_LIMIT),
        name="mixer_pre")(x, st_conv_t, st_shift_t, *consts)


GROUP_LANES = 256
HEADS_PER_GROUP = GROUP_LANES // RWKV_HEAD
N_HEAD_GROUPS = RWKV_DIM // GROUP_LANES
NN = (((1,), (0,)), ((), ()))
NT = (((1,), (1,)), ((), ()))


def _split2(x):
    hi = x.astype(BF16)
    lo = (x - hi.astype(F32)).astype(BF16)
    return hi, lo


def _mm(xs, ys, dims=NN):
    x1, x2 = xs
    y1, y2 = ys
    d = lambda p, q: lax.dot_general(p, q, dims, preferred_element_type=F32)
    m = x1.shape[0]
    both = d(jnp.concatenate([x1, x2], axis=0), y1)
    return both[0:m] + both[m:2 * m] + d(x1, y2)


def _cat2(ps, qs, axis):
    return tuple(jnp.concatenate([p, q], axis=axis) for p, q in zip(ps, qs))


def _wkv_chunk_kernel(r_ref, lw_ref, k_ref, v_ref, a_ref, b_ref, s0_ref, mfeat_ref, mpos_ref,
                      tmask_ref, stmask_ref, tri_ref, y_ref, sout_ref, state,
                      *, chunk, steps_per_seq):
    i = pl.program_id(0)
    c = chunk
    n_chunks = r_ref.shape[0] // c
    hc = HEADS_PER_GROUP * c
    levels = c.bit_length() - 1
    chained = steps_per_seq is not None
    groups = range(N_HEAD_GROUPS)
    units = [(j, g) for j in range(n_chunks) for g in groups]

    if chained:
        @pl.when(i % steps_per_seq == 0)
        def _():
            state[...] = s0_ref[0]

    mfeat = mfeat_ref[...]
    mpos = mpos_ref[...]
    strict = tmask_ref[0]
    incl = tmask_ref[1]
    eye = incl - strict
    tri = tri_ref[...]

    def ld(ref, j, g):
        return ref[j * c:(j + 1) * c, g * GROUP_LANES:(g + 1) * GROUP_LANES]

    def bd2(m):
        mask = mpos if m.shape[1] == hc else mfeat
        return tuple(jnp.concatenate([p] * HEADS_PER_GROUP, axis=0) * mask for p in _split2(m))

    def cumsum_rows(x):
        p1 = x.astype(BF16)
        r1 = x - p1.astype(F32)
        p2 = r1.astype(BF16)
        p3 = (r1 - p2.astype(F32)).astype(BF16)
        d = lambda q: jnp.dot(tri, q, preferred_element_type=F32)
        return d(p1) + d(p2) + d(p3)

    cum = {u: cumsum_rows(ld(lw_ref, *u)) for u in units}
    ar, bk_end, p_end, a_ab, a_ak, a_rb, a_rk, v, vbd = ({} for _ in range(9))
    for u in units:
        cm = cum[u]
        cum_last = cm[c - 1:c, :]
        e_neg = jnp.exp(-cm)
        e_end = jnp.exp(cum_last - cm)
        b_raw = ld(b_ref, *u)
        k_raw = ld(k_ref, *u)
        ar[u] = _split2(jnp.concatenate([ld(a_ref, *u) * jnp.exp(cm - ld(lw_ref, *u)),
                                         ld(r_ref, *u) * jnp.exp(cm)], axis=0))
        bk_end[u] = _split2(jnp.concatenate([b_raw * e_end, k_raw * e_end], axis=0))
        p_end[u] = jnp.exp(cum_last)
        v[u] = ld(v_ref, *u)
        vbd[u] = bd2(v[u])
        gram = _mm(ar[u], _cat2(bd2(b_raw * e_neg), bd2(k_raw * e_neg), 0), NT)
        a_ab[u] = jnp.where(strict > 0, gram[0:c, 0:hc], 0.0)
        a_ak[u] = jnp.where(strict > 0, gram[0:c, hc:2 * hc], 0.0)
        a_rb[u] = jnp.where(incl > 0, gram[c:2 * c, 0:hc], 0.0)
        a_rk[u] = jnp.where(incl > 0, gram[c:2 * c, hc:2 * hc], 0.0)

    inv = {u: eye + a_ab[u] * tmask_ref[2] for u in units}
    for lvl in range(2, levels + 1):
        lm = tmask_ref[1 + lvl]
        t1 = {u: _mm(_split2(a_ab[u] * lm), bd2(inv[u])) for u in units}
        inv = {u: inv[u] + _mm(_split2(inv[u]), bd2(t1[u])) for u in units}
    akv = {u: _mm(_split2(a_ak[u]), vbd[u]) for u in units}
    inv2 = {u: _split2(inv[u]) for u in units}
    arbk = {u: _split2(jnp.concatenate([a_rb[u], a_rk[u]], axis=1)) for u in units}

    stmask = stmask_ref[...]
    s_cur = [state[g] for g in groups] if chained else None
    for j in range(n_chunks):
        s_prev = s_cur if chained else [s0_ref[j, g] for g in groups]
        x0 = [_mm(ar[j, g], _split2(s_prev[g]), NT) for g in groups]
        uu = [_mm(inv2[j, g], bd2(x0[g][0:c] + akv[j, g])) for g in groups]
        for g in groups:
            yy = x0[g][c:2 * c] + _mm(arbk[j, g], _cat2(bd2(uu[g]), vbd[j, g], 0))
            y_ref[j * c:(j + 1) * c, g * GROUP_LANES:(g + 1) * GROUP_LANES] = yy
        s_new = []
        for g in groups:
            uv_t = jnp.transpose(jnp.concatenate([uu[g], v[j, g]], axis=0))
            upd = _mm(_split2(uv_t), bk_end[j, g])
            s_new.append(s_prev[g] * p_end[j, g] + stmask * upd)
        if chained:
            s_cur = s_new
        else:
            for g in groups:
                sout_ref[j, g] = s_new[g]

    if chained:
        for g in groups:
            state[g] = s_cur[g]

        @pl.when(i % steps_per_seq == steps_per_seq - 1)
        def _():
            sout_ref[0] = state[...]


def _wkv_masks(c):
    hc = HEADS_PER_GROUP * c
    levels = c.bit_length() - 1
    t = np.arange(c)[:, None]
    s = (np.arange(hc) % c)[None, :]
    tm = [s < t, s <= t]
    for lvl in range(1, levels + 1):
        half = 1 << (lvl - 1)
        tm.append(((t >> lvl) == (s >> lvl)) & ((t & half) != 0) & ((s & half) == 0))
    row_head = (np.arange(hc) // c)[:, None]
    mfeat = row_head == (np.arange(GROUP_LANES) // RWKV_HEAD)[None, :]
    mpos = row_head == (np.arange(hc) // c)[None, :]
    lane_head = np.arange(GROUP_LANES) // RWKV_HEAD
    stmask = lane_head[:, None] == lane_head[None, :]
    tri = np.arange(c)[None, :] <= np.arange(c)[:, None]
    return (jnp.asarray(mfeat, BF16), jnp.asarray(mpos, BF16), jnp.asarray(np.stack(tm), F32),
            jnp.asarray(stmask, F32), jnp.asarray(tri, BF16))


def _wkv_scan(rlkvab, s0, *, row0, n_rows, chunk, chunks_per_step, chained):
    rows = chunk * chunks_per_step
    n_steps = n_rows // rows
    assert n_rows % rows == 0 and row0 % rows == 0
    masks = _wkv_masks(chunk)
    st = (N_HEAD_GROUPS, GROUP_LANES, GROUP_LANES)
    row = pl.BlockSpec((rows, RWKV_DIM), lambda i: (row0 // rows + i, 0))
    if chained:
        n_state = 1
        st_spec = pl.BlockSpec((1,) + st, lambda i: (0, 0, 0, 0))
    else:
        n_state = n_rows // chunk
        st_spec = pl.BlockSpec((chunks_per_step,) + st, lambda i: (i, 0, 0, 0))
    in_specs = [row] * 6 + [st_spec] + [_const_spec(m.shape) for m in masks]
    out_shape = [jax.ShapeDtypeStruct((n_rows, RWKV_DIM), F32),
                 jax.ShapeDtypeStruct((n_state,) + st, F32)]
    out_specs = [pl.BlockSpec((rows, RWKV_DIM), lambda i: (i, 0)), st_spec]
    kern = functools.partial(_wkv_chunk_kernel, chunk=chunk, steps_per_seq=n_steps if chained else None)
    return pl.pallas_call(
        kern, out_shape=out_shape, grid=(n_steps,), in_specs=in_specs, out_specs=out_specs,
        scratch_shapes=[pltpu.VMEM(st, F32)],
        compiler_params=pltpu.CompilerParams(dimension_semantics=("arbitrary",),
                                             vmem_limit_bytes=VMEM_LIMIT),
        name="wkv_scan")(*rlkvab, s0, *masks)


def _mixer_post_kernel(y_ref, bonus_ref, g_ref, yag_ref, sgb_ref, x_ref, lng_ref, lnb_ref, wob_ref,
                       wo_ref, n2_ref, wr_ref, br_ref, ones_ref,
                       x1_ref, h2_ref, route_ref):
    ones = ones_ref[...]
    y = y_ref[...]
    inv_n = 1.0 / RWKV_HEAD
    mean = _split2_dot(y, ones) * inv_n
    yc = y - mean
    var = _split2_dot(yc * yc, ones) * inv_n
    yn = yc * lax.rsqrt(var + GN_EPS) * lng_ref[...] + lnb_ref[...]
    yy = (yn + bonus_ref[...]) * g_ref[...]
    y_b = jnp.dot(yy.astype(BF16), wob_ref[...], preferred_element_type=F32)
    merged = yag_ref[...] + sgb_ref[...] * y_b
    x1 = x_ref[...] + jnp.dot(merged.astype(BF16), wo_ref[...], preferred_element_type=F32)
    x1_ref[...] = x1
    h2 = _rms(x1, n2_ref[...])
    h2_ref[...] = h2

    logits = jnp.dot(h2, wr_ref[...], precision=HIGHEST, preferred_element_type=F32) + br_ref[...]
    lane = lax.broadcasted_iota(jnp.int32, logits.shape, 1)
    neg = jnp.float32(-jnp.inf)
    big = jnp.int32(LANES)
    is_g = lane < N_GROUPS
    lgp = jnp.where(is_g, logits, neg)
    m_g = jnp.max(lgp, axis=-1, keepdims=True)
    grp = jnp.min(jnp.where(lgp == m_g, lane, big), axis=-1, keepdims=True)
    p_top = 1.0 / jnp.sum(jnp.where(is_g, jnp.exp(logits - m_g), 0.0), axis=-1, keepdims=True)
    e_lane = lane - N_GROUPS
    in_grp = (e_lane >= grp * EXPERTS_PER_GROUP) & (e_lane < (grp + 1) * EXPERTS_PER_GROUP)
    le = jnp.where(in_grp, logits, neg)
    m1 = jnp.max(le, axis=-1, keepdims=True)
    i1 = jnp.min(jnp.where(le == m1, lane, big), axis=-1, keepdims=True)
    le2 = jnp.where(lane == i1, neg, le)
    m2 = jnp.max(le2, axis=-1, keepdims=True)
    i2 = jnp.min(jnp.where(le2 == m2, lane, big), axis=-1, keepdims=True)
    ex = jnp.exp(m2 - m1)
    p1 = 1.0 / (1.0 + ex)
    p2 = ex / (1.0 + ex)
    route = jnp.where(lane == 0, (i1 - N_GROUPS).astype(F32),
            jnp.where(lane == 1, (i2 - N_GROUPS).astype(F32),
            jnp.where(lane == 2, p_top * p1,
            jnp.where(lane == 3, p_top * p2, 0.0))))
    route_ref[...] = route


def _mixer_post(y, bonus, g, yag, sgb, x, p, *, tm):
    n_tok, d = x.shape
    row = lambda w: pl.BlockSpec((tm, w), lambda i: (i, 0))
    consts = [p['lnx_g'], p['lnx_b'], p['w_out_b'], p['w_o'], p['norm2_g'], p['w_router'],
              p['b_router'], p['ones_bf16']]
    in_specs = [row(RWKV_DIM)] * 3 + [row(d)] * 3 + [_const_spec(c.shape) for c in consts]
    out_shape = [jax.ShapeDtypeStruct((n_tok, d), F32), jax.ShapeDtypeStruct((n_tok, d), F32),
                 jax.ShapeDtypeStruct((n_tok, LANES), F32)]
    out_specs = [row(d), row(d), row(LANES)]
    return pl.pallas_call(
        _mixer_post_kernel, out_shape=out_shape, grid=(n_tok // tm,), in_specs=in_specs,
        out_specs=out_specs,
        compiler_params=pltpu.CompilerParams(dimension_semantics=("arbitrary",),
                                             vmem_limit_bytes=VMEM_LIMIT),
        name="mixer_post")(y, bonus, g, yag, sgb, x, *consts)


def _gather_rows(idx_ref, src_hbm, dst, sem, n_rows, *, unrolled):
    def start(r):
        pltpu.make_async_copy(src_hbm.at[pl.ds(idx_ref[r], 1)], dst.at[pl.ds(r, 1)], sem).start()
    if unrolled:
        for r in range(n_rows):
            start(r)
    else:
        def body(r, carry):
            start(r)
            return carry
        lax.fori_loop(0, n_rows, body, 0)


def _wait_rows(src_hbm, dst, sem, n_rows):
    pltpu.make_async_copy(src_hbm.at[pl.ds(0, n_rows)], dst, sem).wait()


def _moe_experts_kernel(bexp_ref, tok_cur_ref, tok_nxt_ref, h_hbm, bw_ref, wg_ref,
                        wu_ref, wd_ref, yb_ref, xbuf, sems):
    del bexp_ref
    i = pl.program_id(0)
    rows = xbuf.shape[1]

    @pl.when(i == 0)
    def _():
        _gather_rows(tok_cur_ref.at[0, 0], h_hbm, xbuf.at[0], sems.at[0], rows, unrolled=False)

    def step(slot):
        _wait_rows(h_hbm, xbuf.at[slot], sems.at[slot], rows)
        _gather_rows(tok_nxt_ref.at[0, 0], h_hbm, xbuf.at[1 - slot], sems.at[1 - slot], rows,
                     unrolled=True)
        xb = xbuf[slot].astype(BF16)
        hg = jnp.dot(xb, wg_ref[0], preferred_element_type=F32)
        hu = jnp.dot(xb, wu_ref[0], preferred_element_type=F32)
        hid = (hg * jax.nn.sigmoid(hg)) * hu
        yb = jnp.dot(hid.astype(BF16), wd_ref[0], preferred_element_type=F32)
        yb_ref[...] = yb * bw_ref[...]

        @pl.when(i == pl.num_programs(0) - 1)
        def _():
            _wait_rows(h_hbm, xbuf.at[1 - slot], sems.at[1 - slot], rows)

    for slot in range(2):
        pl.when(i % 2 == slot)(functools.partial(step, slot))


def _with_dummy_block(idx, n_blocks, blk):
    return jnp.concatenate([idx, jnp.zeros((blk,), idx.dtype)]).reshape(n_blocks + 1, 1, blk)


def _moe_experts(h2, buf_tok, buf_w, block_expert, w_eg, w_eu, w_ed, *, blk):
    n_tok, d = h2.shape
    n_blocks = buf_tok.shape[0] // blk
    de = w_eg.shape[2]
    tok3 = _with_dummy_block(buf_tok, n_blocks, blk)
    grid_spec = pltpu.PrefetchScalarGridSpec(
        num_scalar_prefetch=1, grid=(n_blocks,),
        in_specs=[
            pl.BlockSpec((1, 1, blk), lambda i, be: (i, 0, 0), memory_space=pltpu.SMEM),
            pl.BlockSpec((1, 1, blk), lambda i, be: (i + 1, 0, 0), memory_space=pltpu.SMEM),
            pl.BlockSpec(memory_space=pl.ANY),
            pl.BlockSpec((blk, 1), lambda i, be: (i, 0)),
            pl.BlockSpec((1, d, de), lambda i, be: (be[i], 0, 0)),
            pl.BlockSpec((1, d, de), lambda i, be: (be[i], 0, 0)),
            pl.BlockSpec((1, de, d), lambda i, be: (be[i], 0, 0)),
        ],
        out_specs=pl.BlockSpec((blk, d), lambda i, be: (i, 0)),
        scratch_shapes=[pltpu.VMEM((2, blk, d), F32), pltpu.SemaphoreType.DMA((2,))])
    return pl.pallas_call(
        _moe_experts_kernel, out_shape=jax.ShapeDtypeStruct((n_blocks * blk, d), F32),
        grid_spec=grid_spec,
        compiler_params=pltpu.CompilerParams(dimension_semantics=("arbitrary",),
                                             vmem_limit_bytes=VMEM_LIMIT),
        name="moe_experts")(block_expert, tok3, tok3, h2, buf_w.reshape(-1, 1), w_eg, w_eu, w_ed)


def _moe_combine_kernel(pos_cur_ref, pos_nxt_ref, x1_ref, nf_ref, yb_hbm, out_ref, ybuf, sems):
    i = pl.program_id(0)
    rows = ybuf.shape[1]
    slot = i % 2

    @pl.when(i == 0)
    def _():
        _gather_rows(pos_cur_ref.at[0, 0], yb_hbm, ybuf.at[0], sems.at[0], rows, unrolled=False)

    _wait_rows(yb_hbm, ybuf.at[slot], sems.at[slot], rows)
    _gather_rows(pos_nxt_ref.at[0, 0], yb_hbm, ybuf.at[1 - slot], sems.at[1 - slot], rows, unrolled=True)
    tm = x1_ref.shape[0]
    x2 = x1_ref[...] + ybuf[slot, 0:tm, :] + ybuf[slot, tm:2 * tm, :]
    out_ref[...] = _rms(x2, nf_ref[...])

    @pl.when(i == pl.num_programs(0) - 1)
    def _():
        _wait_rows(yb_hbm, ybuf.at[1 - slot], sems.at[1 - slot], rows)


def _moe_combine(pos, x1, normf_g, yb, *, tm):
    n_tok, d = x1.shape
    n_tiles = n_tok // tm
    pos3 = _with_dummy_block(pos.reshape(n_tiles, tm, TOP_K).transpose(0, 2, 1).reshape(-1),
                             n_tiles, TOP_K * tm)
    in_specs = [
        pl.BlockSpec((1, 1, TOP_K * tm), lambda i: (i, 0, 0), memory_space=pltpu.SMEM),
        pl.BlockSpec((1, 1, TOP_K * tm), lambda i: (i + 1, 0, 0), memory_space=pltpu.SMEM),
        pl.BlockSpec((tm, d), lambda i: (i, 0)),
        _const_spec(normf_g.shape),
        pl.BlockSpec(memory_space=pl.ANY),
    ]
    return pl.pallas_call(
        _moe_combine_kernel, out_shape=jax.ShapeDtypeStruct((n_tok, d), F32), grid=(n_tiles,),
        in_specs=in_specs, out_specs=pl.BlockSpec((tm, d), lambda i: (i, 0)),
        scratch_shapes=[pltpu.VMEM((2, TOP_K * tm, d), F32), pltpu.SemaphoreType.DMA((2,))],
        compiler_params=pltpu.CompilerParams(dimension_semantics=("arbitrary",),
                                             vmem_limit_bytes=VMEM_LIMIT),
        name="moe_combine")(pos3, pos3, x1, normf_g, yb)


def _dispatch(route, blk):
    n_tok = route.shape[0]
    n_assign = n_tok * TOP_K
    expert = route[:, 0:TOP_K].astype(jnp.int32).reshape(-1)
    gate_w = route[:, TOP_K:2 * TOP_K].reshape(-1)
    onehot = (expert[:, None] == jnp.arange(N_EXPERTS, dtype=jnp.int32)[None, :]).astype(jnp.int32)
    csum = jnp.cumsum(onehot, axis=0)
    counts = csum[-1]
    rank = jnp.take_along_axis(csum, expert[:, None], axis=1)[:, 0] - 1
    padded = (counts + blk - 1) // blk * blk
    pad_end = jnp.cumsum(padded)
    pad_start = pad_end - padded
    dest = pad_start[expert] + rank
    n_blocks = -(-n_assign // blk) + N_EXPERTS
    rows = n_blocks * blk
    tok = jnp.arange(n_assign, dtype=jnp.int32) // TOP_K
    buf_tok = jnp.zeros((rows,), jnp.int32).at[dest].set(tok)
    buf_w = jnp.zeros((rows,), F32).at[dest].set(gate_w)
    block_start = jnp.arange(n_blocks, dtype=jnp.int32) * blk
    block_expert = jnp.minimum(jnp.searchsorted(pad_end, block_start, side='right'),
                               N_EXPERTS - 1).astype(jnp.int32)
    return buf_tok, buf_w, block_expert, dest.astype(jnp.int32)


def _state_to_kernel(s):
    b = s.shape[0]
    s6 = s.reshape(b, N_HEAD_GROUPS, HEADS_PER_GROUP, RWKV_HEAD, 1, RWKV_HEAD)
    eye = jnp.eye(HEADS_PER_GROUP, dtype=s.dtype).reshape(1, 1, HEADS_PER_GROUP, 1, HEADS_PER_GROUP, 1)
    return (s6 * eye).reshape(b, N_HEAD_GROUPS, GROUP_LANES, GROUP_LANES)


def _state_from_kernel(s):
    b = s.shape[0]
    s6 = s.reshape(b, N_HEAD_GROUPS, HEADS_PER_GROUP, RWKV_HEAD, HEADS_PER_GROUP, RWKV_HEAD)
    diag = jnp.stack([s6[:, :, h, :, h, :] for h in range(HEADS_PER_GROUP)], axis=2)
    return diag.reshape(b, RWKV_HEADS, RWKV_HEAD, RWKV_HEAD)


def kernel(x_prompt, x_sample, state_conv, state_shift, state_wkv, norm1_g, w_in, conv_w, mu_shift, w0, w_lora_w, a0, w_lora_a, w_lora_g, k_k, k_a, r_k, lnx_g, lnx_b, w_out_a, w_out_b, w_o, norm2_g, w_router_group, b_router_group, w_router_expert, b_router_expert, w_e_gate, w_e_up, w_e_down, normf_g):
    depth = norm1_g.shape[0]
    bp, seq, d = x_prompt.shape
    db, dseq, _ = x_sample.shape
    assert depth == 1 and bp == 1, "single layer, single prompt stream"
    tm = ROW_TILE
    n_p, n_s = bp * seq, db * dseq
    assert n_p % tm == 0 and n_s % tm == 0 and tm % dseq == 0
    n_prompt_tiles, n_sample_tiles = n_p // tm, n_s // tm
    seqs = tm // dseq
    n_tok = n_p + n_s

    x = jnp.concatenate([x_prompt.reshape(n_p, d), x_sample.reshape(n_s, d)], axis=0)

    l = 0
    c3 = 3 * CONV_DIM
    head_id = jnp.arange(RWKV_DIM, dtype=jnp.int32) // RWKV_HEAD
    ones_bf16 = (head_id[:, None] == head_id[None, :]).astype(BF16)
    zpad = jnp.zeros((LORA_W, RWKV_DIM), F32)
    n_r = N_GROUPS + N_EXPERTS
    p = {
        'norm1_g': norm1_g[l].reshape(1, d),
        'w_in_a': w_in[l][:, :c3].astype(BF16),
        'w_in_b': w_in[l][:, c3:c3 + SHIFT_DIM].astype(BF16),
        'w_in_g': w_in[l][:, c3 + SHIFT_DIM:].astype(BF16),
        'conv_w': conv_w[l],
        'mu_shift': mu_shift[l].reshape(1, SHIFT_DIM),
        'w0': w0[l].reshape(1, RWKV_DIM),
        'w_lora_w': jnp.concatenate([w_lora_w[l], zpad], axis=0),
        'a0': a0[l].reshape(1, RWKV_DIM),
        'w_lora_a': jnp.concatenate([zpad, w_lora_a[l]], axis=0),
        'w_lora_g': w_lora_g[l],
        'k_k': k_k[l].reshape(1, RWKV_DIM),
        'k_a': k_a[l].reshape(1, RWKV_DIM),
        'r_k': r_k[l].reshape(1, RWKV_DIM),
        'w_out_a': w_out_a[l].astype(BF16),
        'ones_bf16': ones_bf16,
        'lnx_g': lnx_g[l].reshape(1, RWKV_DIM),
        'lnx_b': lnx_b[l].reshape(1, RWKV_DIM),
        'w_out_b': w_out_b[l].astype(BF16),
        'w_o': w_o[l].astype(BF16),
        'norm2_g': norm2_g[l].reshape(1, d),
        'w_router': jnp.pad(jnp.concatenate([w_router_group[l], w_router_expert[l]], axis=1),
                            ((0, 0), (0, LANES - n_r))),
        'b_router': jnp.pad(jnp.concatenate([b_router_group[l], b_router_expert[l]]),
                            (0, LANES - n_r)).reshape(1, LANES),
    }
    st_conv_t = state_conv[l].reshape(n_sample_tiles, seqs * (CONV_WIDTH - 1), CONV_DIM)
    st_shift_t = state_shift[l].reshape(n_sample_tiles, seqs, SHIFT_DIM)

    (yag, sgb, r, w, k, v, a, b, bonus, g, ctail, stail) = _mixer_pre(
        x, st_conv_t, st_shift_t, p, n_prompt_tiles=n_prompt_tiles, seq_len=dseq, tm=tm)

    s0_prompt = jnp.zeros((1, N_HEAD_GROUPS, GROUP_LANES, GROUP_LANES), F32)
    s0_sample = _state_to_kernel(state_wkv[l])
    rlkvab = (r, w, k, v, a, b)
    cp = min(SCAN_CHUNK, n_p)
    assert cp & (cp - 1) == 0 and dseq & (dseq - 1) == 0
    y_p, s_p = _wkv_scan(rlkvab, s0_prompt, row0=0, n_rows=n_p, chunk=cp,
                         chunks_per_step=SCAN_CHUNKS_PER_STEP, chained=True)
    y_s, s_s = _wkv_scan(rlkvab, s0_sample, row0=n_p, n_rows=n_s, chunk=dseq,
                         chunks_per_step=SCAN_CHUNKS_PER_STEP, chained=False)
    y = jnp.concatenate([y_p, y_s], axis=0)

    x1, h2, route = _mixer_post(y, bonus, g, yag, sgb, x, p, tm=tm)

    blk = MOE_ROWS
    buf_tok, buf_w, block_expert, dest = _dispatch(route, blk)
    yb = _moe_experts(h2, buf_tok, buf_w, block_expert, w_e_gate[l].astype(BF16),
                      w_e_up[l].astype(BF16), w_e_down[l].astype(BF16), blk=blk)
    out = _moe_combine(dest.reshape(n_tok, TOP_K), x1, normf_g.reshape(1, d), yb, tm=tm)

    y_prompt = out[:n_p].reshape(bp, seq, d)
    y_sample = out[n_p:].reshape(db, dseq, d)
    conv_p = ctail[n_prompt_tiles - 1, 2 * (seqs - 1):2 * seqs].reshape(1, bp, CONV_WIDTH - 1, CONV_DIM)
    shift_p = stail[n_prompt_tiles - 1, seqs - 1].reshape(1, bp, 1, SHIFT_DIM)
    wkv_p = _state_from_kernel(s_p).reshape(1, bp, RWKV_HEADS, RWKV_HEAD, RWKV_HEAD)
    conv_s = ctail[n_prompt_tiles:].reshape(1, db, CONV_WIDTH - 1, CONV_DIM)
    shift_s = stail[n_prompt_tiles:].reshape(1, db, 1, SHIFT_DIM)
    wkv_s = _state_from_kernel(s_s).reshape(1, db, RWKV_HEADS, RWKV_HEAD, RWKV_HEAD)
    return (y_prompt, y_sample, conv_p, shift_p, wkv_p, conv_s, shift_s, wkv_s)
```

```python
import functools

import numpy as np
import jax
import jax.numpy as jnp
from jax import lax
from jax.experimental import pallas as pl
from jax.experimental.pallas import tpu as pltpu

F32 = jnp.float32
BF16 = jnp.bfloat16
HIGHEST = lax.Precision.HIGHEST

CONV_DIM = 512
CONV_WIDTH = 3
RWKV_HEAD = 64
RWKV_HEADS = 8
RWKV_DIM = RWKV_HEADS * RWKV_HEAD
LORA_W = 64
LORA_A = 64
LORA_G = 128
SHIFT_DIM = 3 * RWKV_DIM + LORA_W + LORA_A + LORA_G
N_GROUPS = 4
EXPERTS_PER_GROUP = 8
N_EXPERTS = N_GROUPS * EXPERTS_PER_GROUP
TOP_K = 2
RMS_EPS = 1e-6
GN_EPS = 64e-5

LANES = 128
ROW_TILE = 256
SCAN_CHUNK = 64
SCAN_CHUNKS_PER_STEP = 2
MOE_ROWS = 256
HIST = 8
VMEM_LIMIT = 56 * 1024 * 1024


def _rms(x, g):
    return x * lax.rsqrt(jnp.mean(x * x, axis=-1, keepdims=True) + RMS_EPS) * g


def _split2_dot(x, ones_bf16):
    hi = x.astype(BF16)
    lo = (x - hi.astype(F32)).astype(BF16)
    return (jnp.dot(hi, ones_bf16, preferred_element_type=F32)
            + jnp.dot(lo, ones_bf16, preferred_element_type=F32))


def _const_spec(shape):
    nd = len(shape)
    return pl.BlockSpec(shape, lambda *_: (0,) * nd)


def _mixer_pre_kernel(xp_ref, xs_ref, stc_ref, sts_ref, n1_ref, wa_ref, wb_ref, wg_ref, convw_ref, mu_ref,
                      w0_ref, lw_ref, a0_ref, la_ref, lgw_ref, kk_ref, ka_ref, rk_ref, woa_ref,
                      ones_ref,
                      yag_ref, sgb_ref, r_ref, w_ref, k_ref, v_ref, a_ref, b_ref, bonus_ref, g_ref,
                      ctail_ref, stail_ref,
                      cbuf, sbuf, p1buf, p2buf, spbuf, *, n_prompt_tiles, seq_len):
    i = pl.program_id(0)
    tm = xp_ref.shape[0]
    seqs = tm // seq_len

    @pl.when(i == 0)
    def _():
        cbuf[0:HIST, :] = jnp.zeros((HIST, CONV_DIM), F32)
        sbuf[0:HIST, :] = jnp.zeros((HIST, SHIFT_DIM), F32)

    x = jnp.where(i < n_prompt_tiles, xp_ref[...], xs_ref[...])
    h = _rms(x, n1_ref[...]).astype(BF16)

    pa = jnp.dot(h, wa_ref[...], preferred_element_type=F32)
    g_in = pa[:, 0:CONV_DIM]
    g_out = pa[:, CONV_DIM:2 * CONV_DIM]
    x_c = pa[:, 2 * CONV_DIM:3 * CONV_DIM]
    bx = g_in * x_c
    cbuf[HIST:HIST + tm, :] = bx
    p1buf[...] = cbuf[HIST - 1:HIST - 1 + tm, :]
    p2buf[...] = cbuf[HIST - 2:HIST - 2 + tm, :]

    @pl.when(i >= n_prompt_tiles)
    def _():
        for j in range(seqs):
            r0 = j * seq_len
            p1buf[r0:r0 + 1, :] = stc_ref[0, 2 * j + 1:2 * j + 2, :]
            p2buf[r0:r0 + 1, :] = stc_ref[0, 2 * j:2 * j + 1, :]
            p2buf[r0 + 1:r0 + 2, :] = stc_ref[0, 2 * j + 1:2 * j + 2, :]

    cw = convw_ref[...]
    conv = cw[0:1, :] * p2buf[...] + cw[1:2, :] * p1buf[...] + cw[2:3, :] * bx
    y_a = jnp.dot((g_out * conv).astype(BF16), woa_ref[...], preferred_element_type=F32)
    for j in range(seqs):
        r1 = HIST + (j + 1) * seq_len
        ctail_ref[0, 2 * j:2 * j + 2, :] = cbuf[r1 - 2:r1, :]
    cbuf[HIST - 2:HIST, :] = cbuf[HIST + tm - 2:HIST + tm, :]

    pg = jnp.dot(h, wg_ref[...], preferred_element_type=F32)
    d = pg.shape[1] // 2
    yag_ref[...] = jax.nn.sigmoid(pg[:, 0:d]) * y_a
    sgb_ref[...] = jax.nn.sigmoid(pg[:, d:2 * d])

    pb = jnp.dot(h, wb_ref[...], preferred_element_type=F32)
    sbuf[HIST:HIST + tm, :] = pb
    spbuf[...] = sbuf[HIST - 1:HIST - 1 + tm, :]

    @pl.when(i >= n_prompt_tiles)
    def _():
        for j in range(seqs):
            r0 = j * seq_len
            spbuf[r0:r0 + 1, :] = sts_ref[0, j:j + 1, :]

    for j in range(seqs):
        r1 = HIST + (j + 1) * seq_len
        stail_ref[0, j:j + 1, :] = sbuf[r1 - 1:r1, :]
    sbuf[HIST - 1:HIST, :] = sbuf[HIST + tm - 1:HIST + tm, :]

    s = pb + (spbuf[...] - pb) * mu_ref[...]
    o1, o2, o3 = RWKV_DIM, 2 * RWKV_DIM, 3 * RWKV_DIM
    r = s[:, 0:o1]
    k = s[:, o1:o2]
    v = s[:, o2:o3]
    s_l = s[:, o3:o3 + LORA_W + LORA_A]
    lg = s[:, o3 + LORA_W + LORA_A:]
    z = w0_ref[...] + jnp.dot(jnp.tanh(s_l), lw_ref[...], precision=HIGHEST,
                              preferred_element_type=F32)
    w_log = -jax.nn.softplus(-z) - 0.5
    log_decay = -jnp.exp(w_log)
    a = jax.nn.sigmoid(a0_ref[...] + jnp.dot(s_l, la_ref[...], precision=HIGHEST,
                                             preferred_element_type=F32))
    g = jnp.dot(jax.nn.sigmoid(lg), lgw_ref[...], precision=HIGHEST, preferred_element_type=F32)

    ones = ones_ref[...]
    kk = k * kk_ref[...]
    kk_n = kk / jnp.maximum(jnp.sqrt(_split2_dot(kk * kk, ones)), 1e-12)
    k2 = k * (1.0 + (a - 1.0) * ka_ref[...])
    bonus = _split2_dot(r * k2 * rk_ref[...], ones) * v

    r_ref[...] = r
    w_ref[...] = log_decay
    k_ref[...] = k2
    v_ref[...] = v
    a_ref[...] = -kk_n
    b_ref[...] = kk_n * a
    bonus_ref[...] = bonus
    g_ref[...] = g


def _mixer_pre(x_p, x_s, st_conv_t, st_shift_t, p, *, seq_len, tm):
    n_p, d = x_p.shape
    n_tok = n_p + x_s.shape[0]
    n_prompt_tiles = n_p // tm
    n_tiles = n_tok // tm
    seqs = tm // seq_len
    row = lambda w: pl.BlockSpec((tm, w), lambda i: (i, 0))
    st_idx = lambda i: (jnp.maximum(i - n_prompt_tiles, 0), 0, 0)
    consts = [p['norm1_g'], p['w_in_a'], p['w_in_b'], p['w_in_g'], p['conv_w'], p['mu_shift'],
              p['w0'], p['w_lora_w'], p['a0'], p['w_lora_a'], p['w_lora_g'], p['k_k'], p['k_a'],
              p['r_k'], p['w_out_a'], p['ones_bf16']]
    in_specs = [pl.BlockSpec((tm, d), lambda i: (jnp.minimum(i, n_prompt_tiles - 1), 0)),
                pl.BlockSpec((tm, d), lambda i: (jnp.maximum(i - n_prompt_tiles, 0), 0)),
                pl.BlockSpec((1, 2 * seqs, CONV_DIM), st_idx),
                pl.BlockSpec((1, seqs, SHIFT_DIM), st_idx)] + [_const_spec(c.shape) for c in consts]
    sds = lambda w: jax.ShapeDtypeStruct((n_tok, w), F32)
    out_shape = [sds(d), sds(d)] + [sds(RWKV_DIM)] * 8 + [
        jax.ShapeDtypeStruct((n_tiles, 2 * seqs, CONV_DIM), F32),
        jax.ShapeDtypeStruct((n_tiles, seqs, SHIFT_DIM), F32)]
    out_specs = [row(d), row(d)] + [row(RWKV_DIM)] * 8 + [
        pl.BlockSpec((1, 2 * seqs, CONV_DIM), lambda i: (i, 0, 0)),
        pl.BlockSpec((1, seqs, SHIFT_DIM), lambda i: (i, 0, 0))]
    kern = functools.partial(_mixer_pre_kernel, n_prompt_tiles=n_prompt_tiles, seq_len=seq_len)
    return pl.pallas_call(
        kern, out_shape=out_shape, grid=(n_tiles,), in_specs=in_specs, out_specs=out_specs,
        scratch_shapes=[pltpu.VMEM((tm + HIST, CONV_DIM), F32), pltpu.VMEM((tm + HIST, SHIFT_DIM), F32),
                        pltpu.VMEM((tm, CONV_DIM), F32), pltpu.VMEM((tm, CONV_DIM), F32),
                        pltpu.VMEM((tm, SHIFT_DIM), F32)],
        compiler_params=pltpu.CompilerParams(dimension_semantics=("arbitrary",),
                                             vmem_limit_bytes=VMEM_LIMIT),
        name="mixer_pre")(x_p, x_s, st_conv_t, st_shift_t, *consts)


GROUP_LANES = 256
HEADS_PER_GROUP = GROUP_LANES // RWKV_HEAD
N_HEAD_GROUPS = RWKV_DIM // GROUP_LANES
NN = (((1,), (0,)), ((), ()))
NT = (((1,), (1,)), ((), ()))


def _split2(x):
    hi = x.astype(BF16)
    lo = (x - hi.astype(F32)).astype(BF16)
    return hi, lo


def _mm(xs, ys, dims=NN):
    x1, x2 = xs
    y1, y2 = ys
    d = lambda p, q: lax.dot_general(p, q, dims, preferred_element_type=F32)
    m = x1.shape[0]
    both = d(jnp.concatenate([x1, x2], axis=0), y1)
    return both[0:m] + both[m:2 * m] + d(x1, y2)


def _cat2(ps, qs, axis):
    return tuple(jnp.concatenate([p, q], axis=axis) for p, q in zip(ps, qs))


def _wkv_chunk_kernel(r_ref, lw_ref, k_ref, v_ref, a_ref, b_ref, s0_ref, mfeat_ref, mpos_ref,
                      tmask_ref, stmask_ref, tri_ref, y_ref, sout_ref, state,
                      *, chunk, steps_per_seq):
    i = pl.program_id(0)
    c = chunk
    n_chunks = r_ref.shape[0] // c
    hc = HEADS_PER_GROUP * c
    levels = c.bit_length() - 1
    chained = steps_per_seq is not None
    groups = range(N_HEAD_GROUPS)
    units = [(j, g) for j in range(n_chunks) for g in groups]

    if chained:
        @pl.when(i % steps_per_seq == 0)
        def _():
            state[...] = s0_ref[0]

    mfeat = mfeat_ref[...]
    mpos = mpos_ref[...]
    strict = tmask_ref[0]
    incl = tmask_ref[1]
    eye = incl - strict
    tri = tri_ref[...]

    def ld(ref, j, g):
        return ref[j * c:(j + 1) * c, g * GROUP_LANES:(g + 1) * GROUP_LANES]

    def bd2(m):
        mask = mpos if m.shape[1] == hc else mfeat
        return tuple(jnp.concatenate([p] * HEADS_PER_GROUP, axis=0) * mask for p in _split2(m))

    def cumsum_rows(x):
        p1 = x.astype(BF16)
        r1 = x - p1.astype(F32)
        p2 = r1.astype(BF16)
        p3 = (r1 - p2.astype(F32)).astype(BF16)
        d = lambda q: jnp.dot(tri, q, preferred_element_type=F32)
        return d(p1) + d(p2) + d(p3)

    cum = {u: cumsum_rows(ld(lw_ref, *u)) for u in units}
    ar, bk_end, p_end, a_ab, a_ak, a_rb, a_rk, v, vbd = ({} for _ in range(9))
    for u in units:
        cm = cum[u]
        cum_last = cm[c - 1:c, :]
        e_neg = jnp.exp(-cm)
        e_end = jnp.exp(cum_last - cm)
        b_raw = ld(b_ref, *u)
        k_raw = ld(k_ref, *u)
        ar[u] = _split2(jnp.concatenate([ld(a_ref, *u) * jnp.exp(cm - ld(lw_ref, *u)),
                                         ld(r_ref, *u) * jnp.exp(cm)], axis=0))
        bk_end[u] = _split2(jnp.concatenate([b_raw * e_end, k_raw * e_end], axis=0))
        p_end[u] = jnp.exp(cum_last)
        v[u] = ld(v_ref, *u)
        vbd[u] = bd2(v[u])
        gram = _mm(ar[u], _cat2(bd2(b_raw * e_neg), bd2(k_raw * e_neg), 0), NT)
        a_ab[u] = jnp.where(strict > 0, gram[0:c, 0:hc], 0.0)
        a_ak[u] = jnp.where(strict > 0, gram[0:c, hc:2 * hc], 0.0)
        a_rb[u] = jnp.where(incl > 0, gram[c:2 * c, 0:hc], 0.0)
        a_rk[u] = jnp.where(incl > 0, gram[c:2 * c, hc:2 * hc], 0.0)

    inv = {u: eye + a_ab[u] * tmask_ref[2] for u in units}
    for lvl in range(2, levels + 1):
        lm = tmask_ref[1 + lvl]
        t1 = {u: _mm(_split2(a_ab[u] * lm), bd2(inv[u])) for u in units}
        inv = {u: inv[u] + _mm(_split2(inv[u]), bd2(t1[u])) for u in units}
    akv = {u: _mm(_split2(a_ak[u]), vbd[u]) for u in units}
    inv2 = {u: _split2(inv[u]) for u in units}
    arbk = {u: _split2(jnp.concatenate([a_rb[u], a_rk[u]], axis=1)) for u in units}

    stmask = stmask_ref[...]
    s_cur = [state[g] for g in groups] if chained else None
    for j in range(n_chunks):
        s_prev = s_cur if chained else [s0_ref[j, g] for g in groups]
        x0 = [_mm(ar[j, g], _split2(s_prev[g]), NT) for g in groups]
        uu = [_mm(inv2[j, g], bd2(x0[g][0:c] + akv[j, g])) for g in groups]
        for g in groups:
            yy = x0[g][c:2 * c] + _mm(arbk[j, g], _cat2(bd2(uu[g]), vbd[j, g], 0))
            y_ref[j * c:(j + 1) * c, g * GROUP_LANES:(g + 1) * GROUP_LANES] = yy
        s_new = []
        for g in groups:
            uv_t = jnp.transpose(jnp.concatenate([uu[g], v[j, g]], axis=0))
            upd = _mm(_split2(uv_t), bk_end[j, g])
            s_new.append(s_prev[g] * p_end[j, g] + stmask * upd)
        if chained:
            s_cur = s_new
        else:
            for g in groups:
                sout_ref[j, g] = s_new[g]

    if chained:
        for g in groups:
            state[g] = s_cur[g]

        @pl.when(i % steps_per_seq == steps_per_seq - 1)
        def _():
            sout_ref[0] = state[...]


def _wkv_masks(c):
    hc = HEADS_PER_GROUP * c
    levels = c.bit_length() - 1
    t = np.arange(c)[:, None]
    s = (np.arange(hc) % c)[None, :]
    tm = [s < t, s <= t]
    for lvl in range(1, levels + 1):
        half = 1 << (lvl - 1)
        tm.append(((t >> lvl) == (s >> lvl)) & ((t & half) != 0) & ((s & half) == 0))
    row_head = (np.arange(hc) // c)[:, None]
    mfeat = row_head == (np.arange(GROUP_LANES) // RWKV_HEAD)[None, :]
    mpos = row_head == (np.arange(hc) // c)[None, :]
    lane_head = np.arange(GROUP_LANES) // RWKV_HEAD
    stmask = lane_head[:, None] == lane_head[None, :]
    tri = np.arange(c)[None, :] <= np.arange(c)[:, None]
    return (jnp.asarray(mfeat, BF16), jnp.asarray(mpos, BF16), jnp.asarray(np.stack(tm), F32),
            jnp.asarray(stmask, F32), jnp.asarray(tri, BF16))


def _wkv_scan(rlkvab, s0, *, row0, n_rows, chunk, chunks_per_step, chained):
    rows = chunk * chunks_per_step
    n_steps = n_rows // rows
    assert n_rows % rows == 0 and row0 % rows == 0
    masks = _wkv_masks(chunk)
    st = (N_HEAD_GROUPS, GROUP_LANES, GROUP_LANES)
    row = pl.BlockSpec((rows, RWKV_DIM), lambda i: (row0 // rows + i, 0))
    if chained:
        n_state = 1
        st_spec = pl.BlockSpec((1,) + st, lambda i: (0, 0, 0, 0))
    else:
        n_state = n_rows // chunk
        st_spec = pl.BlockSpec((chunks_per_step,) + st, lambda i: (i, 0, 0, 0))
    in_specs = [row] * 6 + [st_spec] + [_const_spec(m.shape) for m in masks]
    out_shape = [jax.ShapeDtypeStruct((n_rows, RWKV_DIM), F32),
                 jax.ShapeDtypeStruct((n_state,) + st, F32)]
    out_specs = [pl.BlockSpec((rows, RWKV_DIM), lambda i: (i, 0)), st_spec]
    kern = functools.partial(_wkv_chunk_kernel, chunk=chunk, steps_per_seq=n_steps if chained else None)
    return pl.pallas_call(
        kern, out_shape=out_shape, grid=(n_steps,), in_specs=in_specs, out_specs=out_specs,
        scratch_shapes=[pltpu.VMEM(st, F32)],
        compiler_params=pltpu.CompilerParams(dimension_semantics=("arbitrary",),
                                             vmem_limit_bytes=VMEM_LIMIT),
        name="wkv_scan")(*rlkvab, s0, *masks)


def _mixer_post_kernel(yp_ref, ys_ref, bonus_ref, g_ref, yag_ref, sgb_ref, xp_ref, xs_ref, lng_ref, lnb_ref,
                       wob_ref, wo_ref, n2_ref, wr_ref, br_ref, ones_ref, tril_ref,
                       x1_ref, h2_ref, route_ref, counts_ref, cnt, *, n_prompt_tiles):
    i = pl.program_id(0)

    @pl.when(i == 0)
    def _():
        cnt[...] = jnp.zeros(cnt.shape, F32)

    x = jnp.where(i < n_prompt_tiles, xp_ref[...], xs_ref[...])
    ones = ones_ref[...]
    y = jnp.where(i < n_prompt_tiles, yp_ref[...], ys_ref[...])
    inv_n = 1.0 / RWKV_HEAD
    mean = _split2_dot(y, ones) * inv_n
    yc = y - mean
    var = _split2_dot(yc * yc, ones) * inv_n
    yn = yc * lax.rsqrt(var + GN_EPS) * lng_ref[...] + lnb_ref[...]
    yy = (yn + bonus_ref[...]) * g_ref[...]
    y_b = jnp.dot(yy.astype(BF16), wob_ref[...], preferred_element_type=F32)
    merged = yag_ref[...] + sgb_ref[...] * y_b
    x1 = x + jnp.dot(merged.astype(BF16), wo_ref[...], preferred_element_type=F32)
    x1_ref[...] = x1
    h2 = _rms(x1, n2_ref[...])
    h2_ref[...] = h2

    logits = jnp.dot(h2, wr_ref[...], precision=HIGHEST, preferred_element_type=F32) + br_ref[...]
    lane = lax.broadcasted_iota(jnp.int32, logits.shape, 1)
    neg = jnp.float32(-jnp.inf)
    big = jnp.int32(LANES)
    is_g = lane < N_GROUPS
    lgp = jnp.where(is_g, logits, neg)
    m_g = jnp.max(lgp, axis=-1, keepdims=True)
    grp = jnp.min(jnp.where(lgp == m_g, lane, big), axis=-1, keepdims=True)
    p_top = 1.0 / jnp.sum(jnp.where(is_g, jnp.exp(logits - m_g), 0.0), axis=-1, keepdims=True)
    e_lane = lane - N_GROUPS
    in_grp = (e_lane >= grp * EXPERTS_PER_GROUP) & (e_lane < (grp + 1) * EXPERTS_PER_GROUP)
    le = jnp.where(in_grp, logits, neg)
    m1 = jnp.max(le, axis=-1, keepdims=True)
    i1 = jnp.min(jnp.where(le == m1, lane, big), axis=-1, keepdims=True)
    le2 = jnp.where(lane == i1, neg, le)
    m2 = jnp.max(le2, axis=-1, keepdims=True)
    i2 = jnp.min(jnp.where(le2 == m2, lane, big), axis=-1, keepdims=True)
    ex = jnp.exp(m2 - m1)
    p1 = 1.0 / (1.0 + ex)
    p2 = ex / (1.0 + ex)
    oh1 = lane == i1 - N_GROUPS
    oh2 = lane == i2 - N_GROUPS
    both = jnp.where(oh1, 1.0, jnp.where(oh2, 1.0, 0.0))
    before = jnp.dot(tril_ref[...], both.astype(BF16), preferred_element_type=F32) + cnt[0:1, :]
    rank1 = jnp.sum(jnp.where(oh1, before, 0.0), axis=-1, keepdims=True)
    rank2 = jnp.sum(jnp.where(oh2, before, 0.0), axis=-1, keepdims=True)
    cnt[0:1, :] = cnt[0:1, :] + jnp.sum(both, axis=0, keepdims=True)
    counts_ref[...] = jnp.broadcast_to(cnt[0:1, :], counts_ref.shape)

    cols = [(i1 - N_GROUPS).astype(F32), (i2 - N_GROUPS).astype(F32), p_top * p1, p_top * p2,
            rank1, rank2]
    route = jnp.zeros(logits.shape, F32)
    for c, col in enumerate(cols):
        route = jnp.where(lane == c, col, route)
    route_ref[...] = route


ROUTE_EXPERT, ROUTE_WEIGHT, ROUTE_RANK = 0, 2, 4


def _mixer_post(y_p, y_s, bonus, g, yag, sgb, x_p, x_s, p, *, tm):
    n_p, d = x_p.shape
    n_tok = n_p + x_s.shape[0]
    npt = n_p // tm
    row = lambda w: pl.BlockSpec((tm, w), lambda i: (i, 0))
    tril = jnp.asarray(np.arange(tm)[None, :] < np.arange(tm)[:, None], BF16)
    consts = [p['lnx_g'], p['lnx_b'], p['w_out_b'], p['w_o'], p['norm2_g'], p['w_router'],
              p['b_router'], p['ones_bf16'], tril]
    pair = lambda w: [pl.BlockSpec((tm, w), lambda i: (jnp.minimum(i, npt - 1), 0)),
                      pl.BlockSpec((tm, w), lambda i: (jnp.maximum(i - npt, 0), 0))]
    in_specs = (pair(RWKV_DIM) + [row(RWKV_DIM)] * 2 + [row(d)] * 2 + pair(d)
                + [_const_spec(c.shape) for c in consts])
    out_shape = [jax.ShapeDtypeStruct((n_tok, d), F32), jax.ShapeDtypeStruct((n_tok, d), F32),
                 jax.ShapeDtypeStruct((n_tok, LANES), F32), jax.ShapeDtypeStruct((8, LANES), F32)]
    out_specs = [row(d), row(d), row(LANES), _const_spec((8, LANES))]
    return pl.pallas_call(
        functools.partial(_mixer_post_kernel, n_prompt_tiles=npt), out_shape=out_shape,
        grid=(n_tok // tm,), in_specs=in_specs, out_specs=out_specs,
        scratch_shapes=[pltpu.VMEM((8, LANES), F32)],
        compiler_params=pltpu.CompilerParams(dimension_semantics=("arbitrary",),
                                             vmem_limit_bytes=VMEM_LIMIT),
        name="mixer_post")(y_p, y_s, bonus, g, yag, sgb, x_p, x_s, *consts)


def _gather_rows(idx_ref, src_hbm, dst, sem, n_rows, *, unrolled):
    def start(r):
        pltpu.make_async_copy(src_hbm.at[pl.ds(idx_ref[r], 1)], dst.at[pl.ds(r, 1)], sem).start()
    if unrolled:
        for r in range(n_rows):
            start(r)
    else:
        def body(r, carry):
            start(r)
            return carry
        lax.fori_loop(0, n_rows, body, 0)


def _wait_rows(src_hbm, dst, sem, n_rows):
    pltpu.make_async_copy(src_hbm.at[pl.ds(0, n_rows)], dst, sem).wait()


def _moe_experts_kernel(bexp_ref, tok_cur_ref, tok_nxt_ref, h_hbm, wg_ref, wu_ref, wd_ref, yb_ref,
                        xbuf, sems, wg_bf, wu_bf, wd_bf):
    i = pl.program_id(0)
    rows = xbuf.shape[1]

    @pl.when(i == 0)
    def _():
        _gather_rows(tok_cur_ref.at[0, 0], h_hbm, xbuf.at[0], sems.at[0], rows, unrolled=False)

    @pl.when((i == 0) | (bexp_ref[i] != bexp_ref[jnp.maximum(i - 1, 0)]))
    def _():
        wg_bf[...] = wg_ref[0].astype(BF16)
        wu_bf[...] = wu_ref[0].astype(BF16)
        wd_bf[...] = wd_ref[0].astype(BF16)

    def step(slot):
        _wait_rows(h_hbm, xbuf.at[slot], sems.at[slot], rows)
        _gather_rows(tok_nxt_ref.at[0, 0], h_hbm, xbuf.at[1 - slot], sems.at[1 - slot], rows,
                     unrolled=True)
        xb = xbuf[slot].astype(BF16)
        hg = jnp.dot(xb, wg_bf[...], preferred_element_type=F32)
        hu = jnp.dot(xb, wu_bf[...], preferred_element_type=F32)
        hid = (hg * jax.nn.sigmoid(hg)) * hu
        yb_ref[...] = jnp.dot(hid.astype(BF16), wd_bf[...], preferred_element_type=F32)

        @pl.when(i == pl.num_programs(0) - 1)
        def _():
            _wait_rows(h_hbm, xbuf.at[1 - slot], sems.at[1 - slot], rows)

    for slot in range(2):
        pl.when(i % 2 == slot)(functools.partial(step, slot))


def _with_dummy_block(idx, n_blocks, blk):
    return jnp.concatenate([idx, jnp.zeros((blk,), idx.dtype)]).reshape(n_blocks + 1, 1, blk)


def _moe_experts(h2, buf_tok, block_expert, w_eg, w_eu, w_ed, *, blk):
    n_tok, d = h2.shape
    n_blocks = buf_tok.shape[0] // blk
    de = w_eg.shape[2]
    tok3 = _with_dummy_block(buf_tok, n_blocks, blk)
    grid_spec = pltpu.PrefetchScalarGridSpec(
        num_scalar_prefetch=1, grid=(n_blocks,),
        in_specs=[
            pl.BlockSpec((1, 1, blk), lambda i, be: (i, 0, 0), memory_space=pltpu.SMEM),
            pl.BlockSpec((1, 1, blk), lambda i, be: (i + 1, 0, 0), memory_space=pltpu.SMEM),
            pl.BlockSpec(memory_space=pl.ANY),
            pl.BlockSpec((1, d, de), lambda i, be: (be[i], 0, 0)),
            pl.BlockSpec((1, d, de), lambda i, be: (be[i], 0, 0)),
            pl.BlockSpec((1, de, d), lambda i, be: (be[i], 0, 0)),
        ],
        out_specs=pl.BlockSpec((blk, d), lambda i, be: (i, 0)),
        scratch_shapes=[pltpu.VMEM((2, blk, d), F32), pltpu.SemaphoreType.DMA((2,)),
                        pltpu.VMEM((d, de), BF16), pltpu.VMEM((d, de), BF16), pltpu.VMEM((de, d), BF16)])
    return pl.pallas_call(
        _moe_experts_kernel, out_shape=jax.ShapeDtypeStruct((n_blocks * blk, d), F32),
        grid_spec=grid_spec,
        compiler_params=pltpu.CompilerParams(dimension_semantics=("arbitrary",),
                                             vmem_limit_bytes=VMEM_LIMIT),
        name="moe_experts")(block_expert, tok3, tok3, h2, w_eg, w_eu, w_ed)


def _moe_combine_kernel(pos_cur_ref, pos_nxt_ref, x1_ref, route_ref, nf_ref, yb_hbm, outp_ref,
                        outs_ref, ybuf, sems, *, n_prompt_tiles):
    i = pl.program_id(0)
    rows = ybuf.shape[1]
    slot = i % 2

    @pl.when(i == 0)
    def _():
        _gather_rows(pos_cur_ref.at[0, 0], yb_hbm, ybuf.at[0], sems.at[0], rows, unrolled=False)

    _wait_rows(yb_hbm, ybuf.at[slot], sems.at[slot], rows)
    _gather_rows(pos_nxt_ref.at[0, 0], yb_hbm, ybuf.at[1 - slot], sems.at[1 - slot], rows, unrolled=True)
    tm = x1_ref.shape[0]
    route = route_ref[...]
    x2 = x1_ref[...]
    for j in range(TOP_K):
        x2 = x2 + ybuf[slot, j * tm:(j + 1) * tm, :] * route[:, ROUTE_WEIGHT + j:ROUTE_WEIGHT + j + 1]
    out = _rms(x2, nf_ref[...])

    @pl.when(i < n_prompt_tiles)
    def _():
        outp_ref[...] = out

    @pl.when(i >= n_prompt_tiles)
    def _():
        outs_ref[...] = out

    @pl.when(i == pl.num_programs(0) - 1)
    def _():
        _wait_rows(yb_hbm, ybuf.at[1 - slot], sems.at[1 - slot], rows)


def _moe_combine(pos, x1, route, normf_g, yb, *, n_prompt_rows, tm):
    n_tok, d = x1.shape
    n_tiles = n_tok // tm
    npt = n_prompt_rows // tm
    pos3 = _with_dummy_block(pos.reshape(n_tiles, tm, TOP_K).transpose(0, 2, 1).reshape(-1),
                             n_tiles, TOP_K * tm)
    in_specs = [
        pl.BlockSpec((1, 1, TOP_K * tm), lambda i: (i, 0, 0), memory_space=pltpu.SMEM),
        pl.BlockSpec((1, 1, TOP_K * tm), lambda i: (i + 1, 0, 0), memory_space=pltpu.SMEM),
        pl.BlockSpec((tm, d), lambda i: (i, 0)),
        pl.BlockSpec((tm, LANES), lambda i: (i, 0)),
        _const_spec(normf_g.shape),
        pl.BlockSpec(memory_space=pl.ANY),
    ]
    out_shape = [jax.ShapeDtypeStruct((n_prompt_rows, d), F32),
                 jax.ShapeDtypeStruct((n_tok - n_prompt_rows, d), F32)]
    out_specs = [pl.BlockSpec((tm, d), lambda i: (jnp.minimum(i, npt - 1), 0)),
                 pl.BlockSpec((tm, d), lambda i: (jnp.maximum(i - npt, 0), 0))]
    return pl.pallas_call(
        functools.partial(_moe_combine_kernel, n_prompt_tiles=npt), out_shape=out_shape,
        grid=(n_tiles,), in_specs=in_specs, out_specs=out_specs,
        scratch_shapes=[pltpu.VMEM((2, TOP_K * tm, d), F32), pltpu.SemaphoreType.DMA((2,))],
        compiler_params=pltpu.CompilerParams(dimension_semantics=("arbitrary",),
                                             vmem_limit_bytes=VMEM_LIMIT),
        name="moe_combine")(pos3, pos3, x1, route, normf_g, yb)


def _dispatch(route, counts, blk):
    n_tok = route.shape[0]
    n_assign = n_tok * TOP_K
    expert = route[:, ROUTE_EXPERT:ROUTE_EXPERT + TOP_K].astype(jnp.int32)
    rank = route[:, ROUTE_RANK:ROUTE_RANK + TOP_K].astype(jnp.int32)
    counts = counts[0, :N_EXPERTS].astype(jnp.int32)
    padded = (counts + blk - 1) // blk * blk
    pad_end = jnp.cumsum(padded)
    pad_start = pad_end - padded
    is_e = expert[:, :, None] == jnp.arange(N_EXPERTS, dtype=jnp.int32)
    dest = jnp.sum(jnp.where(is_e, pad_start, 0), axis=-1) + rank
    n_blocks = -(-n_assign // blk) + N_EXPERTS
    tok = jnp.arange(n_assign, dtype=jnp.int32) // TOP_K
    buf_tok = jnp.zeros((n_blocks * blk,), jnp.int32).at[dest.reshape(-1)].set(tok)
    block_start = jnp.arange(n_blocks, dtype=jnp.int32) * blk
    block_expert = jnp.minimum(jnp.sum(block_start[:, None] >= pad_end[None, :], axis=1),
                               N_EXPERTS - 1).astype(jnp.int32)
    return buf_tok, block_expert, dest


def _state_to_kernel(s):
    b = s.shape[0]
    s6 = s.reshape(b, N_HEAD_GROUPS, HEADS_PER_GROUP, RWKV_HEAD, 1, RWKV_HEAD)
    eye = jnp.eye(HEADS_PER_GROUP, dtype=s.dtype).reshape(1, 1, HEADS_PER_GROUP, 1, HEADS_PER_GROUP, 1)
    return (s6 * eye).reshape(b, N_HEAD_GROUPS, GROUP_LANES, GROUP_LANES)


def _state_from_kernel(s):
    b = s.shape[0]
    s6 = s.reshape(b, N_HEAD_GROUPS, HEADS_PER_GROUP, RWKV_HEAD, HEADS_PER_GROUP, RWKV_HEAD)
    diag = jnp.stack([s6[:, :, h, :, h, :] for h in range(HEADS_PER_GROUP)], axis=2)
    return diag.reshape(b, RWKV_HEADS, RWKV_HEAD, RWKV_HEAD)


def kernel(x_prompt, x_sample, state_conv, state_shift, state_wkv, norm1_g, w_in, conv_w, mu_shift, w0, w_lora_w, a0, w_lora_a, w_lora_g, k_k, k_a, r_k, lnx_g, lnx_b, w_out_a, w_out_b, w_o, norm2_g, w_router_group, b_router_group, w_router_expert, b_router_expert, w_e_gate, w_e_up, w_e_down, normf_g):
    depth = norm1_g.shape[0]
    bp, seq, d = x_prompt.shape
    db, dseq, _ = x_sample.shape
    assert depth == 1 and bp == 1, "single layer, single prompt stream"
    tm = ROW_TILE
    n_p, n_s = bp * seq, db * dseq
    assert n_p % tm == 0 and n_s % tm == 0 and tm % dseq == 0
    n_prompt_tiles, n_sample_tiles = n_p // tm, n_s // tm
    seqs = tm // dseq
    n_tok = n_p + n_s

    x_p = x_prompt.reshape(n_p, d)
    x_s = x_sample.reshape(n_s, d)

    l = 0
    c3 = 3 * CONV_DIM
    head_id = jnp.arange(RWKV_DIM, dtype=jnp.int32) // RWKV_HEAD
    ones_bf16 = (head_id[:, None] == head_id[None, :]).astype(BF16)
    zpad = jnp.zeros((LORA_W, RWKV_DIM), F32)
    n_r = N_GROUPS + N_EXPERTS
    p = {
        'norm1_g': norm1_g[l].reshape(1, d),
        'w_in_a': w_in[l][:, :c3].astype(BF16),
        'w_in_b': w_in[l][:, c3:c3 + SHIFT_DIM].astype(BF16),
        'w_in_g': w_in[l][:, c3 + SHIFT_DIM:].astype(BF16),
        'conv_w': conv_w[l],
        'mu_shift': mu_shift[l].reshape(1, SHIFT_DIM),
        'w0': w0[l].reshape(1, RWKV_DIM),
        'w_lora_w': jnp.concatenate([w_lora_w[l], zpad], axis=0),
        'a0': a0[l].reshape(1, RWKV_DIM),
        'w_lora_a': jnp.concatenate([zpad, w_lora_a[l]], axis=0),
        'w_lora_g': w_lora_g[l],
        'k_k': k_k[l].reshape(1, RWKV_DIM),
        'k_a': k_a[l].reshape(1, RWKV_DIM),
        'r_k': r_k[l].reshape(1, RWKV_DIM),
        'w_out_a': w_out_a[l].astype(BF16),
        'ones_bf16': ones_bf16,
        'lnx_g': lnx_g[l].reshape(1, RWKV_DIM),
        'lnx_b': lnx_b[l].reshape(1, RWKV_DIM),
        'w_out_b': w_out_b[l].astype(BF16),
        'w_o': w_o[l].astype(BF16),
        'norm2_g': norm2_g[l].reshape(1, d),
        'w_router': jnp.pad(jnp.concatenate([w_router_group[l], w_router_expert[l]], axis=1),
                            ((0, 0), (0, LANES - n_r))),
        'b_router': jnp.pad(jnp.concatenate([b_router_group[l], b_router_expert[l]]),
                            (0, LANES - n_r)).reshape(1, LANES),
    }
    st_conv_t = state_conv[l].reshape(n_sample_tiles, seqs * (CONV_WIDTH - 1), CONV_DIM)
    st_shift_t = state_shift[l].reshape(n_sample_tiles, seqs, SHIFT_DIM)

    (yag, sgb, r, w, k, v, a, b, bonus, g, ctail, stail) = _mixer_pre(
        x_p, x_s, st_conv_t, st_shift_t, p, seq_len=dseq, tm=tm)

    s0_prompt = jnp.zeros((1, N_HEAD_GROUPS, GROUP_LANES, GROUP_LANES), F32)
    s0_sample = _state_to_kernel(state_wkv[l])
    rlkvab = (r, w, k, v, a, b)
    cp = min(SCAN_CHUNK, n_p)
    assert cp & (cp - 1) == 0 and dseq & (dseq - 1) == 0
    y_p, s_p = _wkv_scan(rlkvab, s0_prompt, row0=0, n_rows=n_p, chunk=cp,
                         chunks_per_step=SCAN_CHUNKS_PER_STEP, chained=True)
    y_s, s_s = _wkv_scan(rlkvab, s0_sample, row0=n_p, n_rows=n_s, chunk=dseq,
                         chunks_per_step=SCAN_CHUNKS_PER_STEP, chained=False)

    x1, h2, route, counts = _mixer_post(y_p, y_s, bonus, g, yag, sgb, x_p, x_s, p, tm=tm)

    blk = MOE_ROWS
    buf_tok, block_expert, dest = _dispatch(route, counts, blk)
    yb = _moe_experts(h2, buf_tok, block_expert, w_e_gate[l], w_e_up[l], w_e_down[l], blk=blk)
    out_p, out_s = _moe_combine(dest, x1, route, normf_g.reshape(1, d), yb, n_prompt_rows=n_p, tm=tm)

    y_prompt = out_p.reshape(bp, seq, d)
    y_sample = out_s.reshape(db, dseq, d)
    conv_p = ctail[n_prompt_tiles - 1, 2 * (seqs - 1):2 * seqs].reshape(1, bp, CONV_WIDTH - 1, CONV_DIM)
    shift_p = stail[n_prompt_tiles - 1, seqs - 1].reshape(1, bp, 1, SHIFT_DIM)
    wkv_p = _state_from_kernel(s_p).reshape(1, bp, RWKV_HEADS, RWKV_HEAD, RWKV_HEAD)
    conv_s = ctail[n_prompt_tiles:].reshape(1, db, CONV_WIDTH - 1, CONV_DIM)
    shift_s = stail[n_prompt_tiles:].reshape(1, db, 1, SHIFT_DIM)
    wkv_s = _state_from_kernel(s_s).reshape(1, db, RWKV_HEADS, RWKV_HEAD, RWKV_HEAD)
    return (y_prompt, y_sample, conv_p, shift_p, wkv_p, conv_s, shift_s, wkv_s)
```

```python
import functools

import numpy as np
import jax
import jax.numpy as jnp
from jax import lax
from jax.experimental import pallas as pl
from jax.experimental.pallas import tpu as pltpu

F32 = jnp.float32
BF16 = jnp.bfloat16
HIGHEST = lax.Precision.HIGHEST

CONV_DIM = 512
CONV_WIDTH = 3
RWKV_HEAD = 64
RWKV_HEADS = 8
RWKV_DIM = RWKV_HEADS * RWKV_HEAD
LORA_W = 64
LORA_A = 64
LORA_G = 128
SHIFT_DIM = 3 * RWKV_DIM + LORA_W + LORA_A + LORA_G
N_GROUPS = 4
EXPERTS_PER_GROUP = 8
N_EXPERTS = N_GROUPS * EXPERTS_PER_GROUP
TOP_K = 2
RMS_EPS = 1e-6
GN_EPS = 64e-5

LANES = 128
ROW_TILE = 256
SCAN_CHUNK = 64
SCAN_CHUNKS_PER_STEP = 2
MOE_ROWS = 256
HIST = 8
VMEM_LIMIT = 56 * 1024 * 1024


def _rms(x, g):
    return x * lax.rsqrt(jnp.mean(x * x, axis=-1, keepdims=True) + RMS_EPS) * g


def _split2_dot(x, ones_bf16):
    hi = x.astype(BF16)
    lo = (x - hi.astype(F32)).astype(BF16)
    return (jnp.dot(hi, ones_bf16, preferred_element_type=F32)
            + jnp.dot(lo, ones_bf16, preferred_element_type=F32))


def _const_spec(shape):
    nd = len(shape)
    return pl.BlockSpec(shape, lambda *_: (0,) * nd)


def _mixer_pre_kernel(xp_ref, xs_ref, stc_ref, sts_ref, n1_ref, wa_ref, wb_ref, wg_ref, convw_ref, mu_ref,
                      w0_ref, lw_ref, a0_ref, la_ref, lgw_ref, kk_ref, ka_ref, rk_ref, woa_ref,
                      ones_ref,
                      yag_ref, sgb_ref, r_ref, w_ref, k_ref, v_ref, a_ref, b_ref, bonus_ref, g_ref,
                      ctail_ref, stail_ref,
                      cbuf, sbuf, p1buf, p2buf, spbuf, *, n_prompt_tiles, seq_len):
    i = pl.program_id(0)
    tm = xp_ref.shape[0]
    seqs = tm // seq_len

    @pl.when(i == 0)
    def _():
        cbuf[0:HIST, :] = jnp.zeros((HIST, CONV_DIM), F32)
        sbuf[0:HIST, :] = jnp.zeros((HIST, SHIFT_DIM), F32)

    x = jnp.where(i < n_prompt_tiles, xp_ref[...], xs_ref[...])
    h = _rms(x, n1_ref[...]).astype(BF16)

    pa = jnp.dot(h, wa_ref[...], preferred_element_type=F32)
    g_in = pa[:, 0:CONV_DIM]
    g_out = pa[:, CONV_DIM:2 * CONV_DIM]
    x_c = pa[:, 2 * CONV_DIM:3 * CONV_DIM]
    bx = g_in * x_c
    cbuf[HIST:HIST + tm, :] = bx
    p1buf[...] = cbuf[HIST - 1:HIST - 1 + tm, :]
    p2buf[...] = cbuf[HIST - 2:HIST - 2 + tm, :]

    @pl.when(i >= n_prompt_tiles)
    def _():
        for j in range(seqs):
            r0 = j * seq_len
            p1buf[r0:r0 + 1, :] = stc_ref[0, 2 * j + 1:2 * j + 2, :]
            p2buf[r0:r0 + 1, :] = stc_ref[0, 2 * j:2 * j + 1, :]
            p2buf[r0 + 1:r0 + 2, :] = stc_ref[0, 2 * j + 1:2 * j + 2, :]

    cw = convw_ref[...]
    conv = cw[0:1, :] * p2buf[...] + cw[1:2, :] * p1buf[...] + cw[2:3, :] * bx
    y_a = jnp.dot((g_out * conv).astype(BF16), woa_ref[...], preferred_element_type=F32)
    for j in range(seqs):
        r1 = HIST + (j + 1) * seq_len
        ctail_ref[0, 2 * j:2 * j + 2, :] = cbuf[r1 - 2:r1, :]
    cbuf[HIST - 2:HIST, :] = cbuf[HIST + tm - 2:HIST + tm, :]

    pg = jnp.dot(h, wg_ref[...], preferred_element_type=F32)
    d = pg.shape[1] // 2
    yag_ref[...] = jax.nn.sigmoid(pg[:, 0:d]) * y_a
    sgb_ref[...] = jax.nn.sigmoid(pg[:, d:2 * d])

    pb = jnp.dot(h, wb_ref[...], preferred_element_type=F32)
    sbuf[HIST:HIST + tm, :] = pb
    spbuf[...] = sbuf[HIST - 1:HIST - 1 + tm, :]

    @pl.when(i >= n_prompt_tiles)
    def _():
        for j in range(seqs):
            r0 = j * seq_len
            spbuf[r0:r0 + 1, :] = sts_ref[0, j:j + 1, :]

    for j in range(seqs):
        r1 = HIST + (j + 1) * seq_len
        stail_ref[0, j:j + 1, :] = sbuf[r1 - 1:r1, :]
    sbuf[HIST - 1:HIST, :] = sbuf[HIST + tm - 1:HIST + tm, :]

    s = pb + (spbuf[...] - pb) * mu_ref[...]
    o1, o2, o3 = RWKV_DIM, 2 * RWKV_DIM, 3 * RWKV_DIM
    r = s[:, 0:o1]
    k = s[:, o1:o2]
    v = s[:, o2:o3]
    s_l = s[:, o3:o3 + LORA_W + LORA_A]
    lg = s[:, o3 + LORA_W + LORA_A:]
    z = w0_ref[...] + jnp.dot(jnp.tanh(s_l), lw_ref[...], precision=HIGHEST,
                              preferred_element_type=F32)
    w_log = -jax.nn.softplus(-z) - 0.5
    log_decay = -jnp.exp(w_log)
    a = jax.nn.sigmoid(a0_ref[...] + jnp.dot(s_l, la_ref[...], precision=HIGHEST,
                                             preferred_element_type=F32))
    g = jnp.dot(jax.nn.sigmoid(lg), lgw_ref[...], precision=HIGHEST, preferred_element_type=F32)

    ones = ones_ref[...]
    kk = k * kk_ref[...]
    kk_n = kk / jnp.maximum(jnp.sqrt(_split2_dot(kk * kk, ones)), 1e-12)
    k2 = k * (1.0 + (a - 1.0) * ka_ref[...])
    bonus = _split2_dot(r * k2 * rk_ref[...], ones) * v

    r_ref[...] = r
    w_ref[...] = log_decay
    k_ref[...] = k2
    v_ref[...] = v
    a_ref[...] = -kk_n
    b_ref[...] = kk_n * a
    bonus_ref[...] = bonus
    g_ref[...] = g


def _mixer_pre(x_p, x_s, st_conv_t, st_shift_t, p, *, seq_len, tm):
    n_p, d = x_p.shape
    n_tok = n_p + x_s.shape[0]
    n_prompt_tiles = n_p // tm
    n_tiles = n_tok // tm
    seqs = tm // seq_len
    row = lambda w: pl.BlockSpec((tm, w), lambda i: (i, 0))
    st_idx = lambda i: (jnp.maximum(i - n_prompt_tiles, 0), 0, 0)
    consts = [p['norm1_g'], p['w_in_a'], p['w_in_b'], p['w_in_g'], p['conv_w'], p['mu_shift'],
              p['w0'], p['w_lora_w'], p['a0'], p['w_lora_a'], p['w_lora_g'], p['k_k'], p['k_a'],
              p['r_k'], p['w_out_a'], p['ones_bf16']]
    in_specs = [pl.BlockSpec((tm, d), lambda i: (jnp.minimum(i, n_prompt_tiles - 1), 0)),
                pl.BlockSpec((tm, d), lambda i: (jnp.maximum(i - n_prompt_tiles, 0), 0)),
                pl.BlockSpec((1, 2 * seqs, CONV_DIM), st_idx),
                pl.BlockSpec((1, seqs, SHIFT_DIM), st_idx)] + [_const_spec(c.shape) for c in consts]
    sds = lambda w: jax.ShapeDtypeStruct((n_tok, w), F32)
    out_shape = [sds(d), sds(d)] + [sds(RWKV_DIM)] * 8 + [
        jax.ShapeDtypeStruct((n_tiles, 2 * seqs, CONV_DIM), F32),
        jax.ShapeDtypeStruct((n_tiles, seqs, SHIFT_DIM), F32)]
    out_specs = [row(d), row(d)] + [row(RWKV_DIM)] * 8 + [
        pl.BlockSpec((1, 2 * seqs, CONV_DIM), lambda i: (i, 0, 0)),
        pl.BlockSpec((1, seqs, SHIFT_DIM), lambda i: (i, 0, 0))]
    kern = functools.partial(_mixer_pre_kernel, n_prompt_tiles=n_prompt_tiles, seq_len=seq_len)
    return pl.pallas_call(
        kern, out_shape=out_shape, grid=(n_tiles,), in_specs=in_specs, out_specs=out_specs,
        scratch_shapes=[pltpu.VMEM((tm + HIST, CONV_DIM), F32), pltpu.VMEM((tm + HIST, SHIFT_DIM), F32),
                        pltpu.VMEM((tm, CONV_DIM), F32), pltpu.VMEM((tm, CONV_DIM), F32),
                        pltpu.VMEM((tm, SHIFT_DIM), F32)],
        compiler_params=pltpu.CompilerParams(dimension_semantics=("arbitrary",),
                                             vmem_limit_bytes=VMEM_LIMIT),
        name="mixer_pre")(x_p, x_s, st_conv_t, st_shift_t, *consts)


GROUP_LANES = 256
HEADS_PER_GROUP = GROUP_LANES // RWKV_HEAD
N_HEAD_GROUPS = RWKV_DIM // GROUP_LANES
NN = (((1,), (0,)), ((), ()))
NT = (((1,), (1,)), ((), ()))


def _split2(x):
    hi = x.astype(BF16)
    lo = (x - hi.astype(F32)).astype(BF16)
    return hi, lo


def _mm(xs, ys, dims=NN):
    x1, x2 = xs
    y1, y2 = ys
    d = lambda p, q: lax.dot_general(p, q, dims, preferred_element_type=F32)
    m = x1.shape[0]
    both = d(jnp.concatenate([x1, x2], axis=0), y1)
    return both[0:m] + both[m:2 * m] + d(x1, y2)


def _cat2(ps, qs, axis):
    return tuple(jnp.concatenate([p, q], axis=axis) for p, q in zip(ps, qs))


def _wkv_chunk_kernel(r_ref, lw_ref, k_ref, v_ref, a_ref, b_ref, s0_ref, mfeat_ref, mpos_ref,
                      tmask_ref, stmask_ref, tri_ref, y_ref, sout_ref, state,
                      *, chunk, steps_per_seq):
    i = pl.program_id(0)
    c = chunk
    n_chunks = r_ref.shape[0] // c
    hc = HEADS_PER_GROUP * c
    levels = c.bit_length() - 1
    chained = steps_per_seq is not None
    groups = range(N_HEAD_GROUPS)
    units = [(j, g) for j in range(n_chunks) for g in groups]

    if chained:
        @pl.when(i % steps_per_seq == 0)
        def _():
            state[...] = s0_ref[0]

    mfeat = mfeat_ref[...]
    mpos = mpos_ref[...]
    strict = tmask_ref[0]
    incl = tmask_ref[1]
    eye = incl - strict
    tri = tri_ref[...]

    def ld(ref, j, g):
        return ref[j * c:(j + 1) * c, g * GROUP_LANES:(g + 1) * GROUP_LANES]

    def bd2(m):
        mask = mpos if m.shape[1] == hc else mfeat
        return tuple(jnp.concatenate([p] * HEADS_PER_GROUP, axis=0) * mask for p in _split2(m))

    def cumsum_rows(x):
        p1 = x.astype(BF16)
        r1 = x - p1.astype(F32)
        p2 = r1.astype(BF16)
        p3 = (r1 - p2.astype(F32)).astype(BF16)
        d = lambda q: jnp.dot(tri, q, preferred_element_type=F32)
        return d(p1) + d(p2) + d(p3)

    cum = {u: cumsum_rows(ld(lw_ref, *u)) for u in units}
    ar, bk_end, p_end, a_ab, a_ak, a_rb, a_rk, v, vbd = ({} for _ in range(9))
    for u in units:
        cm = cum[u]
        cum_last = cm[c - 1:c, :]
        e_neg = jnp.exp(-cm)
        e_end = jnp.exp(cum_last - cm)
        b_raw = ld(b_ref, *u)
        k_raw = ld(k_ref, *u)
        ar[u] = _split2(jnp.concatenate([ld(a_ref, *u) * jnp.exp(cm - ld(lw_ref, *u)),
                                         ld(r_ref, *u) * jnp.exp(cm)], axis=0))
        bk_end[u] = _split2(jnp.concatenate([b_raw * e_end, k_raw * e_end], axis=0))
        p_end[u] = jnp.exp(cum_last)
        v[u] = ld(v_ref, *u)
        vbd[u] = bd2(v[u])
        gram = _mm(ar[u], _cat2(bd2(b_raw * e_neg), bd2(k_raw * e_neg), 0), NT)
        a_ab[u] = jnp.where(strict > 0, gram[0:c, 0:hc], 0.0)
        a_ak[u] = jnp.where(strict > 0, gram[0:c, hc:2 * hc], 0.0)
        a_rb[u] = jnp.where(incl > 0, gram[c:2 * c, 0:hc], 0.0)
        a_rk[u] = jnp.where(incl > 0, gram[c:2 * c, hc:2 * hc], 0.0)

    inv = {u: eye + a_ab[u] * tmask_ref[2] for u in units}
    for lvl in range(2, levels + 1):
        lm = tmask_ref[1 + lvl]
        t1 = {u: _mm(_split2(a_ab[u] * lm), bd2(inv[u])) for u in units}
        inv = {u: inv[u] + _mm(_split2(inv[u]), bd2(t1[u])) for u in units}
    akv = {u: _mm(_split2(a_ak[u]), vbd[u]) for u in units}
    inv2 = {u: _split2(inv[u]) for u in units}
    arbk = {u: _split2(jnp.concatenate([a_rb[u], a_rk[u]], axis=1)) for u in units}

    stmask = stmask_ref[...]
    s_cur = [state[g] for g in groups] if chained else None
    for j in range(n_chunks):
        s_prev = s_cur if chained else [s0_ref[j, g] for g in groups]
        x0 = [_mm(ar[j, g], _split2(s_prev[g]), NT) for g in groups]
        uu = [_mm(inv2[j, g], bd2(x0[g][0:c] + akv[j, g])) for g in groups]
        for g in groups:
            yy = x0[g][c:2 * c] + _mm(arbk[j, g], _cat2(bd2(uu[g]), vbd[j, g], 0))
            y_ref[j * c:(j + 1) * c, g * GROUP_LANES:(g + 1) * GROUP_LANES] = yy
        s_new = []
        for g in groups:
            uv_t = jnp.transpose(jnp.concatenate([uu[g], v[j, g]], axis=0))
            upd = _mm(_split2(uv_t), bk_end[j, g])
            s_new.append(s_prev[g] * p_end[j, g] + stmask * upd)
        if chained:
            s_cur = s_new
        else:
            for g in groups:
                sout_ref[j, g] = s_new[g]

    if chained:
        for g in groups:
            state[g] = s_cur[g]

        @pl.when(i % steps_per_seq == steps_per_seq - 1)
        def _():
            sout_ref[0] = state[...]


def _wkv_masks(c):
    hc = HEADS_PER_GROUP * c
    levels = c.bit_length() - 1
    t = np.arange(c)[:, None]
    s = (np.arange(hc) % c)[None, :]
    tm = [s < t, s <= t]
    for lvl in range(1, levels + 1):
        half = 1 << (lvl - 1)
        tm.append(((t >> lvl) == (s >> lvl)) & ((t & half) != 0) & ((s & half) == 0))
    row_head = (np.arange(hc) // c)[:, None]
    mfeat = row_head == (np.arange(GROUP_LANES) // RWKV_HEAD)[None, :]
    mpos = row_head == (np.arange(hc) // c)[None, :]
    lane_head = np.arange(GROUP_LANES) // RWKV_HEAD
    stmask = lane_head[:, None] == lane_head[None, :]
    tri = np.arange(c)[None, :] <= np.arange(c)[:, None]
    return (jnp.asarray(mfeat, BF16), jnp.asarray(mpos, BF16), jnp.asarray(np.stack(tm), F32),
            jnp.asarray(stmask, F32), jnp.asarray(tri, BF16))


def _wkv_scan(rlkvab, s0, *, row0, n_rows, chunk, chunks_per_step, chained):
    rows = chunk * chunks_per_step
    n_steps = n_rows // rows
    assert n_rows % rows == 0 and row0 % rows == 0
    masks = _wkv_masks(chunk)
    st = (N_HEAD_GROUPS, GROUP_LANES, GROUP_LANES)
    row = pl.BlockSpec((rows, RWKV_DIM), lambda i: (row0 // rows + i, 0))
    if chained:
        n_state = 1
        st_spec = pl.BlockSpec((1,) + st, lambda i: (0, 0, 0, 0))
    else:
        n_state = n_rows // chunk
        st_spec = pl.BlockSpec((chunks_per_step,) + st, lambda i: (i, 0, 0, 0))
    in_specs = [row] * 6 + [st_spec] + [_const_spec(m.shape) for m in masks]
    out_shape = [jax.ShapeDtypeStruct((n_rows, RWKV_DIM), F32),
                 jax.ShapeDtypeStruct((n_state,) + st, F32)]
    out_specs = [pl.BlockSpec((rows, RWKV_DIM), lambda i: (i, 0)), st_spec]
    kern = functools.partial(_wkv_chunk_kernel, chunk=chunk, steps_per_seq=n_steps if chained else None)
    return pl.pallas_call(
        kern, out_shape=out_shape, grid=(n_steps,), in_specs=in_specs, out_specs=out_specs,
        scratch_shapes=[pltpu.VMEM(st, F32)],
        compiler_params=pltpu.CompilerParams(dimension_semantics=("arbitrary",),
                                             vmem_limit_bytes=VMEM_LIMIT),
        name="wkv_scan")(*rlkvab, s0, *masks)


def _mixer_post_kernel(yp_ref, ys_ref, bonus_ref, g_ref, yag_ref, sgb_ref, xp_ref, xs_ref, lng_ref, lnb_ref,
                       wob_ref, wo_ref, n2_ref, wr_ref, br_ref, ones_ref, tril_ref,
                       x1_ref, h2_ref, route_ref, counts_ref, cnt, *, n_prompt_tiles):
    i = pl.program_id(0)

    @pl.when(i == 0)
    def _():
        cnt[...] = jnp.zeros(cnt.shape, F32)

    x = jnp.where(i < n_prompt_tiles, xp_ref[...], xs_ref[...])
    ones = ones_ref[...]
    y = jnp.where(i < n_prompt_tiles, yp_ref[...], ys_ref[...])
    inv_n = 1.0 / RWKV_HEAD
    mean = _split2_dot(y, ones) * inv_n
    yc = y - mean
    var = _split2_dot(yc * yc, ones) * inv_n
    yn = yc * lax.rsqrt(var + GN_EPS) * lng_ref[...] + lnb_ref[...]
    yy = (yn + bonus_ref[...]) * g_ref[...]
    y_b = jnp.dot(yy.astype(BF16), wob_ref[...], preferred_element_type=F32)
    merged = yag_ref[...] + sgb_ref[...] * y_b
    x1 = x + jnp.dot(merged.astype(BF16), wo_ref[...], preferred_element_type=F32)
    x1_ref[...] = x1
    h2 = _rms(x1, n2_ref[...])
    h2_ref[...] = h2

    logits = jnp.dot(h2, wr_ref[...], precision=HIGHEST, preferred_element_type=F32) + br_ref[...]
    lane = lax.broadcasted_iota(jnp.int32, logits.shape, 1)
    neg = jnp.float32(-jnp.inf)
    big = jnp.int32(LANES)
    is_g = lane < N_GROUPS
    lgp = jnp.where(is_g, logits, neg)
    m_g = jnp.max(lgp, axis=-1, keepdims=True)
    grp = jnp.min(jnp.where(lgp == m_g, lane, big), axis=-1, keepdims=True)
    p_top = 1.0 / jnp.sum(jnp.where(is_g, jnp.exp(logits - m_g), 0.0), axis=-1, keepdims=True)
    e_lane = lane - N_GROUPS
    in_grp = (e_lane >= grp * EXPERTS_PER_GROUP) & (e_lane < (grp + 1) * EXPERTS_PER_GROUP)
    le = jnp.where(in_grp, logits, neg)
    m1 = jnp.max(le, axis=-1, keepdims=True)
    i1 = jnp.min(jnp.where(le == m1, lane, big), axis=-1, keepdims=True)
    le2 = jnp.where(lane == i1, neg, le)
    m2 = jnp.max(le2, axis=-1, keepdims=True)
    i2 = jnp.min(jnp.where(le2 == m2, lane, big), axis=-1, keepdims=True)
    ex = jnp.exp(m2 - m1)
    p1 = 1.0 / (1.0 + ex)
    p2 = ex / (1.0 + ex)
    oh1 = lane == i1 - N_GROUPS
    oh2 = lane == i2 - N_GROUPS
    both = jnp.where(oh1, 1.0, jnp.where(oh2, 1.0, 0.0))
    before = jnp.dot(tril_ref[...], both.astype(BF16), preferred_element_type=F32) + cnt[0:1, :]
    rank1 = jnp.sum(jnp.where(oh1, before, 0.0), axis=-1, keepdims=True)
    rank2 = jnp.sum(jnp.where(oh2, before, 0.0), axis=-1, keepdims=True)
    cnt[0:1, :] = cnt[0:1, :] + jnp.sum(both, axis=0, keepdims=True)
    counts_ref[...] = jnp.broadcast_to(cnt[0:1, :], counts_ref.shape)

    cols = [(i1 - N_GROUPS).astype(F32), (i2 - N_GROUPS).astype(F32), p_top * p1, p_top * p2,
            rank1, rank2]
    route = jnp.zeros(logits.shape, F32)
    for c, col in enumerate(cols):
        route = jnp.where(lane == c, col, route)
    route_ref[...] = route


ROUTE_EXPERT, ROUTE_WEIGHT, ROUTE_RANK = 0, 2, 4


def _mixer_post(y_p, y_s, bonus, g, yag, sgb, x_p, x_s, p, *, tm):
    n_p, d = x_p.shape
    n_tok = n_p + x_s.shape[0]
    npt = n_p // tm
    row = lambda w: pl.BlockSpec((tm, w), lambda i: (i, 0))
    tril = jnp.asarray(np.arange(tm)[None, :] < np.arange(tm)[:, None], BF16)
    consts = [p['lnx_g'], p['lnx_b'], p['w_out_b'], p['w_o'], p['norm2_g'], p['w_router'],
              p['b_router'], p['ones_bf16'], tril]
    pair = lambda w: [pl.BlockSpec((tm, w), lambda i: (jnp.minimum(i, npt - 1), 0)),
                      pl.BlockSpec((tm, w), lambda i: (jnp.maximum(i - npt, 0), 0))]
    in_specs = (pair(RWKV_DIM) + [row(RWKV_DIM)] * 2 + [row(d)] * 2 + pair(d)
                + [_const_spec(c.shape) for c in consts])
    out_shape = [jax.ShapeDtypeStruct((n_tok, d), F32), jax.ShapeDtypeStruct((n_tok, d), F32),
                 jax.ShapeDtypeStruct((n_tok, LANES), F32), jax.ShapeDtypeStruct((8, LANES), F32)]
    out_specs = [row(d), row(d), row(LANES), _const_spec((8, LANES))]
    return pl.pallas_call(
        functools.partial(_mixer_post_kernel, n_prompt_tiles=npt), out_shape=out_shape,
        grid=(n_tok // tm,), in_specs=in_specs, out_specs=out_specs,
        scratch_shapes=[pltpu.VMEM((8, LANES), F32)],
        compiler_params=pltpu.CompilerParams(dimension_semantics=("arbitrary",),
                                             vmem_limit_bytes=VMEM_LIMIT),
        name="mixer_post")(y_p, y_s, bonus, g, yag, sgb, x_p, x_s, *consts)


def _gather_rows(idx_ref, src_hbm, dst, sem, n_rows, *, unrolled):
    def start(r):
        pltpu.make_async_copy(src_hbm.at[pl.ds(idx_ref[r], 1)], dst.at[pl.ds(r, 1)], sem).start()
    if unrolled:
        for r in range(n_rows):
            start(r)
    else:
        def body(r, carry):
            start(r)
            return carry
        lax.fori_loop(0, n_rows, body, 0)


def _wait_rows(src_hbm, dst, sem, n_rows):
    pltpu.make_async_copy(src_hbm.at[pl.ds(0, n_rows)], dst, sem).wait()


SCATTER_LAG = 2


def _moe_scatter_kernel(dest_ref, h_hbm, xs_in_hbm, xs_hbm, sem):
    del xs_in_hbm
    i = pl.program_id(0)
    n_steps = pl.num_programs(0)
    n = dest_ref.shape[-1]
    tm = n // TOP_K
    tile = h_hbm.at[pl.ds(pl.multiple_of(i * tm, tm), tm)]

    def wait_batch():
        pltpu.make_async_copy(h_hbm.at[pl.ds(0, n)], xs_hbm.at[pl.ds(0, n)], sem).wait()

    for r in range(n):
        pltpu.make_async_copy(tile.at[pl.ds(r % tm, 1)],
                              xs_hbm.at[pl.ds(dest_ref[0, 0, r], 1)], sem).start()

    @pl.when(i >= SCATTER_LAG)
    def _():
        wait_batch()

    @pl.when(i == n_steps - 1)
    def _():
        for _ in range(SCATTER_LAG):
            wait_batch()


def _moe_scatter(h2, pos3, n_rows):
    n_tok, d = h2.shape
    n_tiles, _, n = pos3.shape
    assert n_tiles > SCATTER_LAG
    xs0 = jnp.zeros((n_rows, d), h2.dtype)
    return pl.pallas_call(
        _moe_scatter_kernel, out_shape=jax.ShapeDtypeStruct((n_rows, d), h2.dtype), grid=(n_tiles,),
        in_specs=[pl.BlockSpec((1, 1, n), lambda i: (i, 0, 0), memory_space=pltpu.SMEM),
                  pl.BlockSpec(memory_space=pl.ANY), pl.BlockSpec(memory_space=pl.ANY)],
        out_specs=pl.BlockSpec(memory_space=pl.ANY),
        scratch_shapes=[pltpu.SemaphoreType.DMA(())], input_output_aliases={2: 0},
        compiler_params=pltpu.CompilerParams(dimension_semantics=("arbitrary",), has_side_effects=True),
        name="moe_scatter")(pos3, h2, xs0)


def _moe_experts_kernel(bexp_ref, nused_ref, x_ref, wg_ref, wu_ref, wd_ref, yb_ref, wg_bf, wu_bf, wd_bf):
    i = pl.program_id(0)

    @pl.when((i == 0) | (bexp_ref[i] != bexp_ref[jnp.maximum(i - 1, 0)]))
    def _():
        wg_bf[...] = wg_ref[0].astype(BF16)
        wu_bf[...] = wu_ref[0].astype(BF16)
        wd_bf[...] = wd_ref[0].astype(BF16)

    @pl.when(i < nused_ref[0])
    def _():
        xb = x_ref[...].astype(BF16)
        hg = jnp.dot(xb, wg_bf[...], preferred_element_type=F32)
        hu = jnp.dot(xb, wu_bf[...], preferred_element_type=F32)
        hid = (hg * jax.nn.sigmoid(hg)) * hu
        yb_ref[...] = jnp.dot(hid.astype(BF16), wd_bf[...], preferred_element_type=F32)

    @pl.when(i >= nused_ref[0])
    def _():
        yb_ref[...] = jnp.zeros(yb_ref.shape, F32)


def _moe_experts(xs, block_expert, n_used, w_eg, w_eu, w_ed, *, blk):
    n_rows, d = xs.shape
    n_blocks = n_rows // blk
    de = w_eg.shape[2]
    grid_spec = pltpu.PrefetchScalarGridSpec(
        num_scalar_prefetch=2, grid=(n_blocks,),
        in_specs=[
            pl.BlockSpec((blk, d), lambda i, be, nu: (i, 0)),
            pl.BlockSpec((1, d, de), lambda i, be, nu: (be[i], 0, 0)),
            pl.BlockSpec((1, d, de), lambda i, be, nu: (be[i], 0, 0)),
            pl.BlockSpec((1, de, d), lambda i, be, nu: (be[i], 0, 0)),
        ],
        out_specs=pl.BlockSpec((blk, d), lambda i, be, nu: (i, 0)),
        scratch_shapes=[pltpu.VMEM((d, de), BF16), pltpu.VMEM((d, de), BF16), pltpu.VMEM((de, d), BF16)])
    return pl.pallas_call(
        _moe_experts_kernel, out_shape=jax.ShapeDtypeStruct((n_rows, d), F32), grid_spec=grid_spec,
        compiler_params=pltpu.CompilerParams(dimension_semantics=("arbitrary",),
                                             vmem_limit_bytes=VMEM_LIMIT),
        name="moe_experts")(block_expert, n_used, xs, w_eg, w_eu, w_ed)


def _with_dummy_block(idx, n_blocks, blk):
    return jnp.concatenate([idx, jnp.zeros((blk,), idx.dtype)]).reshape(n_blocks + 1, 1, blk)


def _moe_combine_kernel(pos_cur_ref, pos_nxt_ref, x1_ref, route_ref, nf_ref, yb_hbm, outp_ref,
                        outs_ref, ybuf, sems, *, n_prompt_tiles):
    i = pl.program_id(0)
    rows = ybuf.shape[1]
    slot = i % 2

    @pl.when(i == 0)
    def _():
        _gather_rows(pos_cur_ref.at[0, 0], yb_hbm, ybuf.at[0], sems.at[0], rows, unrolled=False)

    _wait_rows(yb_hbm, ybuf.at[slot], sems.at[slot], rows)
    _gather_rows(pos_nxt_ref.at[0, 0], yb_hbm, ybuf.at[1 - slot], sems.at[1 - slot], rows, unrolled=True)
    tm = x1_ref.shape[0]
    route = route_ref[...]
    x2 = x1_ref[...]
    for j in range(TOP_K):
        x2 = x2 + ybuf[slot, j * tm:(j + 1) * tm, :] * route[:, ROUTE_WEIGHT + j:ROUTE_WEIGHT + j + 1]
    out = _rms(x2, nf_ref[...])

    @pl.when(i < n_prompt_tiles)
    def _():
        outp_ref[...] = out

    @pl.when(i >= n_prompt_tiles)
    def _():
        outs_ref[...] = out

    @pl.when(i == pl.num_programs(0) - 1)
    def _():
        _wait_rows(yb_hbm, ybuf.at[1 - slot], sems.at[1 - slot], rows)


def _moe_combine(pos_tiles, x1, route, normf_g, yb, *, n_prompt_rows, tm):
    n_tok, d = x1.shape
    n_tiles = n_tok // tm
    npt = n_prompt_rows // tm
    pos3 = _with_dummy_block(pos_tiles.reshape(-1), n_tiles, TOP_K * tm)
    in_specs = [
        pl.BlockSpec((1, 1, TOP_K * tm), lambda i: (i, 0, 0), memory_space=pltpu.SMEM),
        pl.BlockSpec((1, 1, TOP_K * tm), lambda i: (i + 1, 0, 0), memory_space=pltpu.SMEM),
        pl.BlockSpec((tm, d), lambda i: (i, 0)),
        pl.BlockSpec((tm, LANES), lambda i: (i, 0)),
        _const_spec(normf_g.shape),
        pl.BlockSpec(memory_space=pl.ANY),
    ]
    out_shape = [jax.ShapeDtypeStruct((n_prompt_rows, d), F32),
                 jax.ShapeDtypeStruct((n_tok - n_prompt_rows, d), F32)]
    out_specs = [pl.BlockSpec((tm, d), lambda i: (jnp.minimum(i, npt - 1), 0)),
                 pl.BlockSpec((tm, d), lambda i: (jnp.maximum(i - npt, 0), 0))]
    return pl.pallas_call(
        functools.partial(_moe_combine_kernel, n_prompt_tiles=npt), out_shape=out_shape,
        grid=(n_tiles,), in_specs=in_specs, out_specs=out_specs,
        scratch_shapes=[pltpu.VMEM((2, TOP_K * tm, d), F32), pltpu.SemaphoreType.DMA((2,))],
        compiler_params=pltpu.CompilerParams(dimension_semantics=("arbitrary",),
                                             vmem_limit_bytes=VMEM_LIMIT),
        name="moe_combine")(pos3, pos3, x1, route, normf_g, yb)


def _dispatch(route, counts, blk, tm):
    n_tok = route.shape[0]
    n_assign = n_tok * TOP_K
    expert = route[:, ROUTE_EXPERT:ROUTE_EXPERT + TOP_K].astype(jnp.int32)
    rank = route[:, ROUTE_RANK:ROUTE_RANK + TOP_K].astype(jnp.int32)
    counts = counts[0, :N_EXPERTS].astype(jnp.int32)
    padded = (counts + blk - 1) // blk * blk
    pad_end = jnp.cumsum(padded)
    pad_start = pad_end - padded
    is_e = expert[:, :, None] == jnp.arange(N_EXPERTS, dtype=jnp.int32)
    dest = jnp.sum(jnp.where(is_e, pad_start, 0), axis=-1) + rank
    n_blocks = -(-n_assign // blk) + N_EXPERTS
    block_start = jnp.arange(n_blocks, dtype=jnp.int32) * blk
    block_expert = jnp.minimum(jnp.sum(block_start[:, None] >= pad_end[None, :], axis=1),
                               N_EXPERTS - 1).astype(jnp.int32)
    n_used = (pad_end[-1] // blk).reshape(1)
    pos_tiles = dest.reshape(n_tok // tm, tm, TOP_K).transpose(0, 2, 1).reshape(n_tok // tm, 1, TOP_K * tm)
    return pos_tiles, block_expert, n_used, n_blocks


def _state_to_kernel(s):
    b = s.shape[0]
    s6 = s.reshape(b, N_HEAD_GROUPS, HEADS_PER_GROUP, RWKV_HEAD, 1, RWKV_HEAD)
    eye = jnp.eye(HEADS_PER_GROUP, dtype=s.dtype).reshape(1, 1, HEADS_PER_GROUP, 1, HEADS_PER_GROUP, 1)
    return (s6 * eye).reshape(b, N_HEAD_GROUPS, GROUP_LANES, GROUP_LANES)


def _state_from_kernel(s):
    b = s.shape[0]
    s6 = s.reshape(b, N_HEAD_GROUPS, HEADS_PER_GROUP, RWKV_HEAD, HEADS_PER_GROUP, RWKV_HEAD)
    diag = jnp.stack([s6[:, :, h, :, h, :] for h in range(HEADS_PER_GROUP)], axis=2)
    return diag.reshape(b, RWKV_HEADS, RWKV_HEAD, RWKV_HEAD)


def kernel(x_prompt, x_sample, state_conv, state_shift, state_wkv, norm1_g, w_in, conv_w, mu_shift, w0, w_lora_w, a0, w_lora_a, w_lora_g, k_k, k_a, r_k, lnx_g, lnx_b, w_out_a, w_out_b, w_o, norm2_g, w_router_group, b_router_group, w_router_expert, b_router_expert, w_e_gate, w_e_up, w_e_down, normf_g):
    depth = norm1_g.shape[0]
    bp, seq, d = x_prompt.shape
    db, dseq, _ = x_sample.shape
    assert depth == 1 and bp == 1, "single layer, single prompt stream"
    tm = ROW_TILE
    n_p, n_s = bp * seq, db * dseq
    assert n_p % tm == 0 and n_s % tm == 0 and tm % dseq == 0
    n_prompt_tiles, n_sample_tiles = n_p // tm, n_s // tm
    seqs = tm // dseq
    n_tok = n_p + n_s

    x_p = x_prompt.reshape(n_p, d)
    x_s = x_sample.reshape(n_s, d)

    l = 0
    c3 = 3 * CONV_DIM
    head_id = jnp.arange(RWKV_DIM, dtype=jnp.int32) // RWKV_HEAD
    ones_bf16 = (head_id[:, None] == head_id[None, :]).astype(BF16)
    zpad = jnp.zeros((LORA_W, RWKV_DIM), F32)
    n_r = N_GROUPS + N_EXPERTS
    p = {
        'norm1_g': norm1_g[l].reshape(1, d),
        'w_in_a': w_in[l][:, :c3].astype(BF16),
        'w_in_b': w_in[l][:, c3:c3 + SHIFT_DIM].astype(BF16),
        'w_in_g': w_in[l][:, c3 + SHIFT_DIM:].astype(BF16),
        'conv_w': conv_w[l],
        'mu_shift': mu_shift[l].reshape(1, SHIFT_DIM),
        'w0': w0[l].reshape(1, RWKV_DIM),
        'w_lora_w': jnp.concatenate([w_lora_w[l], zpad], axis=0),
        'a0': a0[l].reshape(1, RWKV_DIM),
        'w_lora_a': jnp.concatenate([zpad, w_lora_a[l]], axis=0),
        'w_lora_g': w_lora_g[l],
        'k_k': k_k[l].reshape(1, RWKV_DIM),
        'k_a': k_a[l].reshape(1, RWKV_DIM),
        'r_k': r_k[l].reshape(1, RWKV_DIM),
        'w_out_a': w_out_a[l].astype(BF16),
        'ones_bf16': ones_bf16,
        'lnx_g': lnx_g[l].reshape(1, RWKV_DIM),
        'lnx_b': lnx_b[l].reshape(1, RWKV_DIM),
        'w_out_b': w_out_b[l].astype(BF16),
        'w_o': w_o[l].astype(BF16),
        'norm2_g': norm2_g[l].reshape(1, d),
        'w_router': jnp.pad(jnp.concatenate([w_router_group[l], w_router_expert[l]], axis=1),
                            ((0, 0), (0, LANES - n_r))),
        'b_router': jnp.pad(jnp.concatenate([b_router_group[l], b_router_expert[l]]),
                            (0, LANES - n_r)).reshape(1, LANES),
    }
    st_conv_t = state_conv[l].reshape(n_sample_tiles, seqs * (CONV_WIDTH - 1), CONV_DIM)
    st_shift_t = state_shift[l].reshape(n_sample_tiles, seqs, SHIFT_DIM)

    (yag, sgb, r, w, k, v, a, b, bonus, g, ctail, stail) = _mixer_pre(
        x_p, x_s, st_conv_t, st_shift_t, p, seq_len=dseq, tm=tm)

    s0_prompt = jnp.zeros((1, N_HEAD_GROUPS, GROUP_LANES, GROUP_LANES), F32)
    s0_sample = _state_to_kernel(state_wkv[l])
    rlkvab = (r, w, k, v, a, b)
    cp = min(SCAN_CHUNK, n_p)
    assert cp & (cp - 1) == 0 and dseq & (dseq - 1) == 0
    y_p, s_p = _wkv_scan(rlkvab, s0_prompt, row0=0, n_rows=n_p, chunk=cp,
                         chunks_per_step=SCAN_CHUNKS_PER_STEP, chained=True)
    y_s, s_s = _wkv_scan(rlkvab, s0_sample, row0=n_p, n_rows=n_s, chunk=dseq,
                         chunks_per_step=SCAN_CHUNKS_PER_STEP, chained=False)

    x1, h2, route, counts = _mixer_post(y_p, y_s, bonus, g, yag, sgb, x_p, x_s, p, tm=tm)

    blk = MOE_ROWS
    pos_tiles, block_expert, n_used, n_blocks = _dispatch(route, counts, blk, tm)
    xs = _moe_scatter(h2, pos_tiles, n_blocks * blk)
    yb = _moe_experts(xs, block_expert, n_used, w_e_gate[l], w_e_up[l], w_e_down[l], blk=blk)
    out_p, out_s = _moe_combine(pos_tiles, x1, route, normf_g.reshape(1, d), yb, n_prompt_rows=n_p, tm=tm)

    y_prompt = out_p.reshape(bp, seq, d)
    y_sample = out_s.reshape(db, dseq, d)
    conv_p = ctail[n_prompt_tiles - 1, 2 * (seqs - 1):2 * seqs].reshape(1, bp, CONV_WIDTH - 1, CONV_DIM)
    shift_p = stail[n_prompt_tiles - 1, seqs - 1].reshape(1, bp, 1, SHIFT_DIM)
    wkv_p = _state_from_kernel(s_p).reshape(1, bp, RWKV_HEADS, RWKV_HEAD, RWKV_HEAD)
    conv_s = ctail[n_prompt_tiles:].reshape(1, db, CONV_WIDTH - 1, CONV_DIM)
    shift_s = stail[n_prompt_tiles:].reshape(1, db, 1, SHIFT_DIM)
    wkv_s = _state_from_kernel(s_s).reshape(1, db, RWKV_HEADS, RWKV_HEAD, RWKV_HEAD)
    return (y_prompt, y_sample, conv_p, shift_p, wkv_p, conv_s, shift_s, wkv_s)
```

```python
import functools

import numpy as np
import jax
import jax.numpy as jnp
from jax import lax
from jax.experimental import pallas as pl
from jax.experimental.pallas import tpu as pltpu

F32 = jnp.float32
BF16 = jnp.bfloat16
HIGHEST = lax.Precision.HIGHEST

CONV_DIM = 512
CONV_WIDTH = 3
RWKV_HEAD = 64
RWKV_HEADS = 8
RWKV_DIM = RWKV_HEADS * RWKV_HEAD
LORA_W = 64
LORA_A = 64
LORA_G = 128
SHIFT_DIM = 3 * RWKV_DIM + LORA_W + LORA_A + LORA_G
N_GROUPS = 4
EXPERTS_PER_GROUP = 8
N_EXPERTS = N_GROUPS * EXPERTS_PER_GROUP
TOP_K = 2
RMS_EPS = 1e-6
GN_EPS = 64e-5

LANES = 128
ROW_TILE = 256
SCAN_CHUNK = 64
SCAN_CHUNKS_PER_STEP = 2
MOE_ROWS = 256
HIST = 8
VMEM_LIMIT = 56 * 1024 * 1024


def _rms(x, g):
    return x * lax.rsqrt(jnp.mean(x * x, axis=-1, keepdims=True) + RMS_EPS) * g


def _split2_dot(x, ones_bf16):
    hi = x.astype(BF16)
    lo = (x - hi.astype(F32)).astype(BF16)
    return (jnp.dot(hi, ones_bf16, preferred_element_type=F32)
            + jnp.dot(lo, ones_bf16, preferred_element_type=F32))


def _const_spec(shape):
    nd = len(shape)
    return pl.BlockSpec(shape, lambda *_: (0,) * nd)


def _mixer_pre_kernel(xp_ref, xs_ref, stc_ref, sts_ref, n1_ref, wa_ref, wb_ref, wg_ref, convw_ref, mu_ref,
                      w0_ref, lw_ref, a0_ref, la_ref, lgw_ref, kk_ref, ka_ref, rk_ref, woa_ref,
                      ones_ref,
                      yag_ref, sgb_ref, r_ref, w_ref, k_ref, v_ref, a_ref, b_ref, bonus_ref, g_ref,
                      ctail_ref, stail_ref,
                      cbuf, sbuf, p1buf, p2buf, spbuf, *, n_prompt_tiles, seq_len):
    i = pl.program_id(0)
    tm = xp_ref.shape[0]
    seqs = tm // seq_len

    @pl.when(i == 0)
    def _():
        cbuf[0:HIST, :] = jnp.zeros((HIST, CONV_DIM), F32)
        sbuf[0:HIST, :] = jnp.zeros((HIST, SHIFT_DIM), F32)

    x = jnp.where(i < n_prompt_tiles, xp_ref[...], xs_ref[...])
    h = _rms(x, n1_ref[...]).astype(BF16)

    pa = jnp.dot(h, wa_ref[...], preferred_element_type=F32)
    g_in = pa[:, 0:CONV_DIM]
    g_out = pa[:, CONV_DIM:2 * CONV_DIM]
    x_c = pa[:, 2 * CONV_DIM:3 * CONV_DIM]
    bx = g_in * x_c
    cbuf[HIST:HIST + tm, :] = bx
    p1buf[...] = cbuf[HIST - 1:HIST - 1 + tm, :]
    p2buf[...] = cbuf[HIST - 2:HIST - 2 + tm, :]

    @pl.when(i >= n_prompt_tiles)
    def _():
        for j in range(seqs):
            r0 = j * seq_len
            p1buf[r0:r0 + 1, :] = stc_ref[0, 2 * j + 1:2 * j + 2, :]
            p2buf[r0:r0 + 1, :] = stc_ref[0, 2 * j:2 * j + 1, :]
            p2buf[r0 + 1:r0 + 2, :] = stc_ref[0, 2 * j + 1:2 * j + 2, :]

    cw = convw_ref[...]
    conv = cw[0:1, :] * p2buf[...] + cw[1:2, :] * p1buf[...] + cw[2:3, :] * bx
    y_a = jnp.dot((g_out * conv).astype(BF16), woa_ref[...], preferred_element_type=F32)
    for j in range(seqs):
        r1 = HIST + (j + 1) * seq_len
        ctail_ref[0, 2 * j:2 * j + 2, :] = cbuf[r1 - 2:r1, :]
    cbuf[HIST - 2:HIST, :] = cbuf[HIST + tm - 2:HIST + tm, :]

    pg = jnp.dot(h, wg_ref[...], preferred_element_type=F32)
    d = pg.shape[1] // 2
    yag_ref[...] = jax.nn.sigmoid(pg[:, 0:d]) * y_a
    sgb_ref[...] = jax.nn.sigmoid(pg[:, d:2 * d])

    pb = jnp.dot(h, wb_ref[...], preferred_element_type=F32)
    sbuf[HIST:HIST + tm, :] = pb
    spbuf[...] = sbuf[HIST - 1:HIST - 1 + tm, :]

    @pl.when(i >= n_prompt_tiles)
    def _():
        for j in range(seqs):
            r0 = j * seq_len
            spbuf[r0:r0 + 1, :] = sts_ref[0, j:j + 1, :]

    for j in range(seqs):
        r1 = HIST + (j + 1) * seq_len
        stail_ref[0, j:j + 1, :] = sbuf[r1 - 1:r1, :]
    sbuf[HIST - 1:HIST, :] = sbuf[HIST + tm - 1:HIST + tm, :]

    s = pb + (spbuf[...] - pb) * mu_ref[...]
    o1, o2, o3 = RWKV_DIM, 2 * RWKV_DIM, 3 * RWKV_DIM
    r = s[:, 0:o1]
    k = s[:, o1:o2]
    v = s[:, o2:o3]
    s_l = s[:, o3:o3 + LORA_W + LORA_A]
    lg = s[:, o3 + LORA_W + LORA_A:]
    z = w0_ref[...] + jnp.dot(jnp.tanh(s_l), lw_ref[...], precision=HIGHEST,
                              preferred_element_type=F32)
    w_log = -jax.nn.softplus(-z) - 0.5
    log_decay = -jnp.exp(w_log)
    a = jax.nn.sigmoid(a0_ref[...] + jnp.dot(s_l, la_ref[...], precision=HIGHEST,
                                             preferred_element_type=F32))
    g = jnp.dot(jax.nn.sigmoid(lg), lgw_ref[...], precision=HIGHEST, preferred_element_type=F32)

    ones = ones_ref[...]
    kk = k * kk_ref[...]
    kk_n = kk / jnp.maximum(jnp.sqrt(_split2_dot(kk * kk, ones)), 1e-12)
    k2 = k * (1.0 + (a - 1.0) * ka_ref[...])
    bonus = _split2_dot(r * k2 * rk_ref[...], ones) * v

    r_ref[...] = r
    w_ref[...] = log_decay
    k_ref[...] = k2
    v_ref[...] = v
    a_ref[...] = -kk_n
    b_ref[...] = kk_n * a
    bonus_ref[...] = bonus
    g_ref[...] = g


def _mixer_pre(x_p, x_s, st_conv_t, st_shift_t, p, *, seq_len, tm):
    n_p, d = x_p.shape
    n_tok = n_p + x_s.shape[0]
    n_prompt_tiles = n_p // tm
    n_tiles = n_tok // tm
    seqs = tm // seq_len
    row = lambda w: pl.BlockSpec((tm, w), lambda i: (i, 0))
    st_idx = lambda i: (jnp.maximum(i - n_prompt_tiles, 0), 0, 0)
    consts = [p['norm1_g'], p['w_in_a'], p['w_in_b'], p['w_in_g'], p['conv_w'], p['mu_shift'],
              p['w0'], p['w_lora_w'], p['a0'], p['w_lora_a'], p['w_lora_g'], p['k_k'], p['k_a'],
              p['r_k'], p['w_out_a'], p['ones_bf16']]
    in_specs = [pl.BlockSpec((tm, d), lambda i: (jnp.minimum(i, n_prompt_tiles - 1), 0)),
                pl.BlockSpec((tm, d), lambda i: (jnp.maximum(i - n_prompt_tiles, 0), 0)),
                pl.BlockSpec((1, 2 * seqs, CONV_DIM), st_idx),
                pl.BlockSpec((1, seqs, SHIFT_DIM), st_idx)] + [_const_spec(c.shape) for c in consts]
    sds = lambda w: jax.ShapeDtypeStruct((n_tok, w), F32)
    out_shape = [sds(d), sds(d)] + [sds(RWKV_DIM)] * 8 + [
        jax.ShapeDtypeStruct((n_tiles, 2 * seqs, CONV_DIM), F32),
        jax.ShapeDtypeStruct((n_tiles, seqs, SHIFT_DIM), F32)]
    out_specs = [row(d), row(d)] + [row(RWKV_DIM)] * 8 + [
        pl.BlockSpec((1, 2 * seqs, CONV_DIM), lambda i: (i, 0, 0)),
        pl.BlockSpec((1, seqs, SHIFT_DIM), lambda i: (i, 0, 0))]
    kern = functools.partial(_mixer_pre_kernel, n_prompt_tiles=n_prompt_tiles, seq_len=seq_len)
    return pl.pallas_call(
        kern, out_shape=out_shape, grid=(n_tiles,), in_specs=in_specs, out_specs=out_specs,
        scratch_shapes=[pltpu.VMEM((tm + HIST, CONV_DIM), F32), pltpu.VMEM((tm + HIST, SHIFT_DIM), F32),
                        pltpu.VMEM((tm, CONV_DIM), F32), pltpu.VMEM((tm, CONV_DIM), F32),
                        pltpu.VMEM((tm, SHIFT_DIM), F32)],
        compiler_params=pltpu.CompilerParams(dimension_semantics=("arbitrary",),
                                             vmem_limit_bytes=VMEM_LIMIT),
        name="mixer_pre")(x_p, x_s, st_conv_t, st_shift_t, *consts)


GROUP_LANES = 256
HEADS_PER_GROUP = GROUP_LANES // RWKV_HEAD
N_HEAD_GROUPS = RWKV_DIM // GROUP_LANES
NN = (((1,), (0,)), ((), ()))
NT = (((1,), (1,)), ((), ()))


def _split2(x):
    hi = x.astype(BF16)
    lo = (x - hi.astype(F32)).astype(BF16)
    return hi, lo


def _mm(xs, ys, dims=NN):
    x1, x2 = xs
    y1, y2 = ys
    d = lambda p, q: lax.dot_general(p, q, dims, preferred_element_type=F32)
    m = x1.shape[0]
    both = d(jnp.concatenate([x1, x2], axis=0), y1)
    return both[0:m] + both[m:2 * m] + d(x1, y2)


def _cat2(ps, qs, axis):
    return tuple(jnp.concatenate([p, q], axis=axis) for p, q in zip(ps, qs))


def _wkv_chunk_kernel(r_ref, lw_ref, k_ref, v_ref, a_ref, b_ref, s0_ref, mfeat_ref, mpos_ref,
                      tmask_ref, stmask_ref, tri_ref, y_ref, sout_ref, state,
                      *, chunk, steps_per_seq):
    i = pl.program_id(0)
    c = chunk
    n_chunks = r_ref.shape[0] // c
    hc = HEADS_PER_GROUP * c
    levels = c.bit_length() - 1
    chained = steps_per_seq is not None
    groups = range(N_HEAD_GROUPS)
    units = [(j, g) for j in range(n_chunks) for g in groups]

    if chained:
        @pl.when(i % steps_per_seq == 0)
        def _():
            state[...] = s0_ref[0]

    mfeat = mfeat_ref[...]
    mpos = mpos_ref[...]
    strict = tmask_ref[0]
    incl = tmask_ref[1]
    eye = incl - strict
    tri = tri_ref[...]

    def ld(ref, j, g):
        return ref[j * c:(j + 1) * c, g * GROUP_LANES:(g + 1) * GROUP_LANES]

    def bd2(m):
        mask = mpos if m.shape[1] == hc else mfeat
        return tuple(jnp.concatenate([p] * HEADS_PER_GROUP, axis=0) * mask for p in _split2(m))

    def cumsum_rows(x):
        p1 = x.astype(BF16)
        r1 = x - p1.astype(F32)
        p2 = r1.astype(BF16)
        p3 = (r1 - p2.astype(F32)).astype(BF16)
        d = lambda q: jnp.dot(tri, q, preferred_element_type=F32)
        return d(p1) + d(p2) + d(p3)

    cum = {u: cumsum_rows(ld(lw_ref, *u)) for u in units}
    ar, bk_end, p_end, a_ab, a_ak, a_rb, a_rk, v, vbd = ({} for _ in range(9))
    for u in units:
        cm = cum[u]
        cum_last = cm[c - 1:c, :]
        e_neg = jnp.exp(-cm)
        e_end = jnp.exp(cum_last - cm)
        b_raw = ld(b_ref, *u)
        k_raw = ld(k_ref, *u)
        ar[u] = _split2(jnp.concatenate([ld(a_ref, *u) * jnp.exp(cm - ld(lw_ref, *u)),
                                         ld(r_ref, *u) * jnp.exp(cm)], axis=0))
        bk_end[u] = _split2(jnp.concatenate([b_raw * e_end, k_raw * e_end], axis=0))
        p_end[u] = jnp.exp(cum_last)
        v[u] = ld(v_ref, *u)
        vbd[u] = bd2(v[u])
        gram = _mm(ar[u], _cat2(bd2(b_raw * e_neg), bd2(k_raw * e_neg), 0), NT)
        a_ab[u] = jnp.where(strict > 0, gram[0:c, 0:hc], 0.0)
        a_ak[u] = jnp.where(strict > 0, gram[0:c, hc:2 * hc], 0.0)
        a_rb[u] = jnp.where(incl > 0, gram[c:2 * c, 0:hc], 0.0)
        a_rk[u] = jnp.where(incl > 0, gram[c:2 * c, hc:2 * hc], 0.0)

    inv = {u: eye + a_ab[u] * tmask_ref[2] for u in units}
    for lvl in range(2, levels + 1):
        lm = tmask_ref[1 + lvl]
        t1 = {u: _mm(_split2(a_ab[u] * lm), bd2(inv[u])) for u in units}
        inv = {u: inv[u] + _mm(_split2(inv[u]), bd2(t1[u])) for u in units}
    akv = {u: _mm(_split2(a_ak[u]), vbd[u]) for u in units}
    inv2 = {u: _split2(inv[u]) for u in units}
    arbk = {u: _split2(jnp.concatenate([a_rb[u], a_rk[u]], axis=1)) for u in units}

    stmask = stmask_ref[...]
    s_cur = [state[g] for g in groups] if chained else None
    for j in range(n_chunks):
        s_prev = s_cur if chained else [s0_ref[j, g] for g in groups]
        x0 = [_mm(ar[j, g], _split2(s_prev[g]), NT) for g in groups]
        uu = [_mm(inv2[j, g], bd2(x0[g][0:c] + akv[j, g])) for g in groups]
        for g in groups:
            yy = x0[g][c:2 * c] + _mm(arbk[j, g], _cat2(bd2(uu[g]), vbd[j, g], 0))
            y_ref[j * c:(j + 1) * c, g * GROUP_LANES:(g + 1) * GROUP_LANES] = yy
        s_new = []
        for g in groups:
            uv_t = jnp.transpose(jnp.concatenate([uu[g], v[j, g]], axis=0))
            upd = _mm(_split2(uv_t), bk_end[j, g])
            s_new.append(s_prev[g] * p_end[j, g] + stmask * upd)
        if chained:
            s_cur = s_new
        else:
            for g in groups:
                sout_ref[j, g] = s_new[g]

    if chained:
        for g in groups:
            state[g] = s_cur[g]

        @pl.when(i % steps_per_seq == steps_per_seq - 1)
        def _():
            sout_ref[0] = state[...]


def _wkv_masks(c):
    hc = HEADS_PER_GROUP * c
    levels = c.bit_length() - 1
    t = np.arange(c)[:, None]
    s = (np.arange(hc) % c)[None, :]
    tm = [s < t, s <= t]
    for lvl in range(1, levels + 1):
        half = 1 << (lvl - 1)
        tm.append(((t >> lvl) == (s >> lvl)) & ((t & half) != 0) & ((s & half) == 0))
    row_head = (np.arange(hc) // c)[:, None]
    mfeat = row_head == (np.arange(GROUP_LANES) // RWKV_HEAD)[None, :]
    mpos = row_head == (np.arange(hc) // c)[None, :]
    lane_head = np.arange(GROUP_LANES) // RWKV_HEAD
    stmask = lane_head[:, None] == lane_head[None, :]
    tri = np.arange(c)[None, :] <= np.arange(c)[:, None]
    return (jnp.asarray(mfeat, BF16), jnp.asarray(mpos, BF16), jnp.asarray(np.stack(tm), F32),
            jnp.asarray(stmask, F32), jnp.asarray(tri, BF16))


def _wkv_scan(rlkvab, s0, *, row0, n_rows, chunk, chunks_per_step, chained):
    rows = chunk * chunks_per_step
    n_steps = n_rows // rows
    assert n_rows % rows == 0 and row0 % rows == 0
    masks = _wkv_masks(chunk)
    st = (N_HEAD_GROUPS, GROUP_LANES, GROUP_LANES)
    row = pl.BlockSpec((rows, RWKV_DIM), lambda i: (row0 // rows + i, 0))
    if chained:
        n_state = 1
        st_spec = pl.BlockSpec((1,) + st, lambda i: (0, 0, 0, 0))
    else:
        n_state = n_rows // chunk
        st_spec = pl.BlockSpec((chunks_per_step,) + st, lambda i: (i, 0, 0, 0))
    in_specs = [row] * 6 + [st_spec] + [_const_spec(m.shape) for m in masks]
    out_shape = [jax.ShapeDtypeStruct((n_rows, RWKV_DIM), F32),
                 jax.ShapeDtypeStruct((n_state,) + st, F32)]
    out_specs = [pl.BlockSpec((rows, RWKV_DIM), lambda i: (i, 0)), st_spec]
    kern = functools.partial(_wkv_chunk_kernel, chunk=chunk, steps_per_seq=n_steps if chained else None)
    return pl.pallas_call(
        kern, out_shape=out_shape, grid=(n_steps,), in_specs=in_specs, out_specs=out_specs,
        scratch_shapes=[pltpu.VMEM(st, F32)],
        compiler_params=pltpu.CompilerParams(dimension_semantics=("arbitrary",),
                                             vmem_limit_bytes=VMEM_LIMIT),
        name="wkv_scan")(*rlkvab, s0, *masks)


def _mixer_post_kernel(yp_ref, ys_ref, bonus_ref, g_ref, yag_ref, sgb_ref, xp_ref, xs_ref, lng_ref, lnb_ref,
                       wob_ref, wo_ref, n2_ref, wr_ref, br_ref, ones_ref, tril_ref,
                       x1_ref, h2_ref, route_ref, counts_ref, cnt, *, n_prompt_tiles):
    i = pl.program_id(0)

    @pl.when(i == 0)
    def _():
        cnt[...] = jnp.zeros(cnt.shape, F32)

    x = jnp.where(i < n_prompt_tiles, xp_ref[...], xs_ref[...])
    ones = ones_ref[...]
    y = jnp.where(i < n_prompt_tiles, yp_ref[...], ys_ref[...])
    inv_n = 1.0 / RWKV_HEAD
    mean = _split2_dot(y, ones) * inv_n
    yc = y - mean
    var = _split2_dot(yc * yc, ones) * inv_n
    yn = yc * lax.rsqrt(var + GN_EPS) * lng_ref[...] + lnb_ref[...]
    yy = (yn + bonus_ref[...]) * g_ref[...]
    y_b = jnp.dot(yy.astype(BF16), wob_ref[...], preferred_element_type=F32)
    merged = yag_ref[...] + sgb_ref[...] * y_b
    x1 = x + jnp.dot(merged.astype(BF16), wo_ref[...], preferred_element_type=F32)
    x1_ref[...] = x1
    h2 = _rms(x1, n2_ref[...])
    h2_ref[...] = h2

    logits = jnp.dot(h2, wr_ref[...], precision=HIGHEST, preferred_element_type=F32) + br_ref[...]
    lane = lax.broadcasted_iota(jnp.int32, logits.shape, 1)
    neg = jnp.float32(-jnp.inf)
    big = jnp.int32(LANES)
    is_g = lane < N_GROUPS
    lgp = jnp.where(is_g, logits, neg)
    m_g = jnp.max(lgp, axis=-1, keepdims=True)
    grp = jnp.min(jnp.where(lgp == m_g, lane, big), axis=-1, keepdims=True)
    p_top = 1.0 / jnp.sum(jnp.where(is_g, jnp.exp(logits - m_g), 0.0), axis=-1, keepdims=True)
    e_lane = lane - N_GROUPS
    in_grp = (e_lane >= grp * EXPERTS_PER_GROUP) & (e_lane < (grp + 1) * EXPERTS_PER_GROUP)
    le = jnp.where(in_grp, logits, neg)
    m1 = jnp.max(le, axis=-1, keepdims=True)
    i1 = jnp.min(jnp.where(le == m1, lane, big), axis=-1, keepdims=True)
    le2 = jnp.where(lane == i1, neg, le)
    m2 = jnp.max(le2, axis=-1, keepdims=True)
    i2 = jnp.min(jnp.where(le2 == m2, lane, big), axis=-1, keepdims=True)
    ex = jnp.exp(m2 - m1)
    p1 = 1.0 / (1.0 + ex)
    p2 = ex / (1.0 + ex)
    oh1 = lane == i1 - N_GROUPS
    oh2 = lane == i2 - N_GROUPS
    both = jnp.where(oh1, 1.0, jnp.where(oh2, 1.0, 0.0))
    before = jnp.dot(tril_ref[...], both.astype(BF16), preferred_element_type=F32) + cnt[0:1, :]
    rank1 = jnp.sum(jnp.where(oh1, before, 0.0), axis=-1, keepdims=True)
    rank2 = jnp.sum(jnp.where(oh2, before, 0.0), axis=-1, keepdims=True)
    cnt[0:1, :] = cnt[0:1, :] + jnp.sum(both, axis=0, keepdims=True)
    counts_ref[...] = jnp.broadcast_to(cnt[0:1, :], counts_ref.shape)

    cols = [(i1 - N_GROUPS).astype(F32), (i2 - N_GROUPS).astype(F32), p_top * p1, p_top * p2,
            rank1, rank2]
    route = jnp.zeros(logits.shape, F32)
    for c, col in enumerate(cols):
        route = jnp.where(lane == c, col, route)
    route_ref[...] = route


ROUTE_EXPERT, ROUTE_WEIGHT, ROUTE_RANK = 0, 2, 4


def _mixer_post(y_p, y_s, bonus, g, yag, sgb, x_p, x_s, p, *, tm):
    n_p, d = x_p.shape
    n_tok = n_p + x_s.shape[0]
    npt = n_p // tm
    row = lambda w: pl.BlockSpec((tm, w), lambda i: (i, 0))
    tril = jnp.asarray(np.arange(tm)[None, :] < np.arange(tm)[:, None], BF16)
    consts = [p['lnx_g'], p['lnx_b'], p['w_out_b'], p['w_o'], p['norm2_g'], p['w_router'],
              p['b_router'], p['ones_bf16'], tril]
    pair = lambda w: [pl.BlockSpec((tm, w), lambda i: (jnp.minimum(i, npt - 1), 0)),
                      pl.BlockSpec((tm, w), lambda i: (jnp.maximum(i - npt, 0), 0))]
    in_specs = (pair(RWKV_DIM) + [row(RWKV_DIM)] * 2 + [row(d)] * 2 + pair(d)
                + [_const_spec(c.shape) for c in consts])
    out_shape = [jax.ShapeDtypeStruct((n_tok, d), F32), jax.ShapeDtypeStruct((n_tok, d), F32),
                 jax.ShapeDtypeStruct((n_tok, LANES), F32), jax.ShapeDtypeStruct((8, LANES), F32)]
    out_specs = [row(d), row(d), row(LANES), _const_spec((8, LANES))]
    return pl.pallas_call(
        functools.partial(_mixer_post_kernel, n_prompt_tiles=npt), out_shape=out_shape,
        grid=(n_tok // tm,), in_specs=in_specs, out_specs=out_specs,
        scratch_shapes=[pltpu.VMEM((8, LANES), F32)],
        compiler_params=pltpu.CompilerParams(dimension_semantics=("arbitrary",),
                                             vmem_limit_bytes=VMEM_LIMIT),
        name="mixer_post")(y_p, y_s, bonus, g, yag, sgb, x_p, x_s, *consts)


def _gather_rows(idx_ref, src_hbm, dst, sem, n_rows, *, unrolled):
    def start(r):
        pltpu.make_async_copy(src_hbm.at[pl.ds(idx_ref[r], 1)], dst.at[pl.ds(r, 1)], sem).start()
    if unrolled:
        for r in range(n_rows):
            start(r)
    else:
        def body(r, carry):
            start(r)
            return carry
        lax.fori_loop(0, n_rows, body, 0)


def _wait_rows(src_hbm, dst, sem, n_rows):
    pltpu.make_async_copy(src_hbm.at[pl.ds(0, n_rows)], dst, sem).wait()


def _moe_scatter_kernel(dest_ref, h_ref, xs_in_hbm, xs_hbm, sem):
    del xs_in_hbm
    n = dest_ref.shape[-1]
    tm = h_ref.shape[0]
    for r in range(n):
        pltpu.make_async_copy(h_ref.at[pl.ds(r % tm, 1)],
                              xs_hbm.at[pl.ds(dest_ref[0, 0, r], 1)], sem).start()
    for j in range(n // tm):
        pltpu.make_async_copy(h_ref, xs_hbm.at[pl.ds(0, tm)], sem).wait()


def _moe_scatter(h2, pos3, n_rows):
    n_tok, d = h2.shape
    n_tiles, _, n = pos3.shape
    tm = n // TOP_K
    xs0 = jnp.zeros((n_rows, d), h2.dtype)
    return pl.pallas_call(
        _moe_scatter_kernel, out_shape=jax.ShapeDtypeStruct((n_rows, d), h2.dtype), grid=(n_tiles,),
        in_specs=[pl.BlockSpec((1, 1, n), lambda i: (i, 0, 0), memory_space=pltpu.SMEM),
                  pl.BlockSpec((tm, d), lambda i: (i, 0)), pl.BlockSpec(memory_space=pl.ANY)],
        out_specs=pl.BlockSpec(memory_space=pl.ANY),
        scratch_shapes=[pltpu.SemaphoreType.DMA(())], input_output_aliases={2: 0},
        compiler_params=pltpu.CompilerParams(dimension_semantics=("arbitrary",), has_side_effects=True,
                                             vmem_limit_bytes=VMEM_LIMIT),
        name="moe_scatter")(pos3, h2, xs0)


def _moe_experts_kernel(bexp_ref, nused_ref, x_ref, wg_ref, wu_ref, wd_ref, yb_ref, wg_bf, wu_bf, wd_bf):
    i = pl.program_id(0)

    @pl.when((i == 0) | (bexp_ref[i] != bexp_ref[jnp.maximum(i - 1, 0)]))
    def _():
        wg_bf[...] = wg_ref[0].astype(BF16)
        wu_bf[...] = wu_ref[0].astype(BF16)
        wd_bf[...] = wd_ref[0].astype(BF16)

    @pl.when(i < nused_ref[0])
    def _():
        xb = x_ref[...].astype(BF16)
        hg = jnp.dot(xb, wg_bf[...], preferred_element_type=F32)
        hu = jnp.dot(xb, wu_bf[...], preferred_element_type=F32)
        hid = (hg * jax.nn.sigmoid(hg)) * hu
        yb_ref[...] = jnp.dot(hid.astype(BF16), wd_bf[...], preferred_element_type=F32)

    @pl.when(i >= nused_ref[0])
    def _():
        yb_ref[...] = jnp.zeros(yb_ref.shape, F32)


def _moe_experts(xs, block_expert, n_used, w_eg, w_eu, w_ed, *, blk):
    n_rows, d = xs.shape
    n_blocks = n_rows // blk
    de = w_eg.shape[2]
    grid_spec = pltpu.PrefetchScalarGridSpec(
        num_scalar_prefetch=2, grid=(n_blocks,),
        in_specs=[
            pl.BlockSpec((blk, d), lambda i, be, nu: (i, 0)),
            pl.BlockSpec((1, d, de), lambda i, be, nu: (be[i], 0, 0)),
            pl.BlockSpec((1, d, de), lambda i, be, nu: (be[i], 0, 0)),
            pl.BlockSpec((1, de, d), lambda i, be, nu: (be[i], 0, 0)),
        ],
        out_specs=pl.BlockSpec((blk, d), lambda i, be, nu: (i, 0)),
        scratch_shapes=[pltpu.VMEM((d, de), BF16), pltpu.VMEM((d, de), BF16), pltpu.VMEM((de, d), BF16)])
    return pl.pallas_call(
        _moe_experts_kernel, out_shape=jax.ShapeDtypeStruct((n_rows, d), F32), grid_spec=grid_spec,
        compiler_params=pltpu.CompilerParams(dimension_semantics=("arbitrary",),
                                             vmem_limit_bytes=VMEM_LIMIT),
        name="moe_experts")(block_expert, n_used, xs, w_eg, w_eu, w_ed)


def _with_dummy_block(idx, n_blocks, blk):
    return jnp.concatenate([idx, jnp.zeros((blk,), idx.dtype)]).reshape(n_blocks + 1, 1, blk)


def _moe_combine_kernel(pos_cur_ref, pos_nxt_ref, x1_ref, route_ref, nf_ref, yb_hbm, outp_ref,
                        outs_ref, ybuf, sems, *, n_prompt_tiles):
    i = pl.program_id(0)
    rows = ybuf.shape[1]
    slot = i % 2

    @pl.when(i == 0)
    def _():
        _gather_rows(pos_cur_ref.at[0, 0], yb_hbm, ybuf.at[0], sems.at[0], rows, unrolled=False)

    _wait_rows(yb_hbm, ybuf.at[slot], sems.at[slot], rows)
    _gather_rows(pos_nxt_ref.at[0, 0], yb_hbm, ybuf.at[1 - slot], sems.at[1 - slot], rows, unrolled=True)
    tm = x1_ref.shape[0]
    route = route_ref[...]
    x2 = x1_ref[...]
    for j in range(TOP_K):
        x2 = x2 + ybuf[slot, j * tm:(j + 1) * tm, :] * route[:, ROUTE_WEIGHT + j:ROUTE_WEIGHT + j + 1]
    out = _rms(x2, nf_ref[...])

    @pl.when(i < n_prompt_tiles)
    def _():
        outp_ref[...] = out

    @pl.when(i >= n_prompt_tiles)
    def _():
        outs_ref[...] = out

    @pl.when(i == pl.num_programs(0) - 1)
    def _():
        _wait_rows(yb_hbm, ybuf.at[1 - slot], sems.at[1 - slot], rows)


def _moe_combine(pos_tiles, x1, route, normf_g, yb, *, n_prompt_rows, tm):
    n_tok, d = x1.shape
    n_tiles = n_tok // tm
    npt = n_prompt_rows // tm
    pos3 = _with_dummy_block(pos_tiles.reshape(-1), n_tiles, TOP_K * tm)
    in_specs = [
        pl.BlockSpec((1, 1, TOP_K * tm), lambda i: (i, 0, 0), memory_space=pltpu.SMEM),
        pl.BlockSpec((1, 1, TOP_K * tm), lambda i: (i + 1, 0, 0), memory_space=pltpu.SMEM),
        pl.BlockSpec((tm, d), lambda i: (i, 0)),
        pl.BlockSpec((tm, LANES), lambda i: (i, 0)),
        _const_spec(normf_g.shape),
        pl.BlockSpec(memory_space=pl.ANY),
    ]
    out_shape = [jax.ShapeDtypeStruct((n_prompt_rows, d), F32),
                 jax.ShapeDtypeStruct((n_tok - n_prompt_rows, d), F32)]
    out_specs = [pl.BlockSpec((tm, d), lambda i: (jnp.minimum(i, npt - 1), 0)),
                 pl.BlockSpec((tm, d), lambda i: (jnp.maximum(i - npt, 0), 0))]
    return pl.pallas_call(
        functools.partial(_moe_combine_kernel, n_prompt_tiles=npt), out_shape=out_shape,
        grid=(n_tiles,), in_specs=in_specs, out_specs=out_specs,
        scratch_shapes=[pltpu.VMEM((2, TOP_K * tm, d), F32), pltpu.SemaphoreType.DMA((2,))],
        compiler_params=pltpu.CompilerParams(dimension_semantics=("arbitrary",),
                                             vmem_limit_bytes=VMEM_LIMIT),
        name="moe_combine")(pos3, pos3, x1, route, normf_g, yb)


def _dispatch(route, counts, blk, tm):
    n_tok = route.shape[0]
    n_assign = n_tok * TOP_K
    expert = route[:, ROUTE_EXPERT:ROUTE_EXPERT + TOP_K].astype(jnp.int32)
    rank = route[:, ROUTE_RANK:ROUTE_RANK + TOP_K].astype(jnp.int32)
    counts = counts[0, :N_EXPERTS].astype(jnp.int32)
    padded = (counts + blk - 1) // blk * blk
    pad_end = jnp.cumsum(padded)
    pad_start = pad_end - padded
    is_e = expert[:, :, None] == jnp.arange(N_EXPERTS, dtype=jnp.int32)
    dest = jnp.sum(jnp.where(is_e, pad_start, 0), axis=-1) + rank
    n_blocks = -(-n_assign // blk) + N_EXPERTS
    block_start = jnp.arange(n_blocks, dtype=jnp.int32) * blk
    block_expert = jnp.minimum(jnp.sum(block_start[:, None] >= pad_end[None, :], axis=1),
                               N_EXPERTS - 1).astype(jnp.int32)
    n_used = (pad_end[-1] // blk).reshape(1)
    pos_tiles = dest.reshape(n_tok // tm, tm, TOP_K).transpose(0, 2, 1).reshape(n_tok // tm, 1, TOP_K * tm)
    return pos_tiles, block_expert, n_used, n_blocks


def _state_to_kernel(s):
    b = s.shape[0]
    s6 = s.reshape(b, N_HEAD_GROUPS, HEADS_PER_GROUP, RWKV_HEAD, 1, RWKV_HEAD)
    eye = jnp.eye(HEADS_PER_GROUP, dtype=s.dtype).reshape(1, 1, HEADS_PER_GROUP, 1, HEADS_PER_GROUP, 1)
    return (s6 * eye).reshape(b, N_HEAD_GROUPS, GROUP_LANES, GROUP_LANES)


def _state_from_kernel(s):
    b = s.shape[0]
    s6 = s.reshape(b, N_HEAD_GROUPS, HEADS_PER_GROUP, RWKV_HEAD, HEADS_PER_GROUP, RWKV_HEAD)
    diag = jnp.stack([s6[:, :, h, :, h, :] for h in range(HEADS_PER_GROUP)], axis=2)
    return diag.reshape(b, RWKV_HEADS, RWKV_HEAD, RWKV_HEAD)


def kernel(x_prompt, x_sample, state_conv, state_shift, state_wkv, norm1_g, w_in, conv_w, mu_shift, w0, w_lora_w, a0, w_lora_a, w_lora_g, k_k, k_a, r_k, lnx_g, lnx_b, w_out_a, w_out_b, w_o, norm2_g, w_router_group, b_router_group, w_router_expert, b_router_expert, w_e_gate, w_e_up, w_e_down, normf_g):
    depth = norm1_g.shape[0]
    bp, seq, d = x_prompt.shape
    db, dseq, _ = x_sample.shape
    assert depth == 1 and bp == 1, "single layer, single prompt stream"
    tm = ROW_TILE
    n_p, n_s = bp * seq, db * dseq
    assert n_p % tm == 0 and n_s % tm == 0 and tm % dseq == 0
    n_prompt_tiles, n_sample_tiles = n_p // tm, n_s // tm
    seqs = tm // dseq
    n_tok = n_p + n_s

    x_p = x_prompt.reshape(n_p, d)
    x_s = x_sample.reshape(n_s, d)

    l = 0
    c3 = 3 * CONV_DIM
    head_id = jnp.arange(RWKV_DIM, dtype=jnp.int32) // RWKV_HEAD
    ones_bf16 = (head_id[:, None] == head_id[None, :]).astype(BF16)
    zpad = jnp.zeros((LORA_W, RWKV_DIM), F32)
    n_r = N_GROUPS + N_EXPERTS
    p = {
        'norm1_g': norm1_g[l].reshape(1, d),
        'w_in_a': w_in[l][:, :c3].astype(BF16),
        'w_in_b': w_in[l][:, c3:c3 + SHIFT_DIM].astype(BF16),
        'w_in_g': w_in[l][:, c3 + SHIFT_DIM:].astype(BF16),
        'conv_w': conv_w[l],
        'mu_shift': mu_shift[l].reshape(1, SHIFT_DIM),
        'w0': w0[l].reshape(1, RWKV_DIM),
        'w_lora_w': jnp.concatenate([w_lora_w[l], zpad], axis=0),
        'a0': a0[l].reshape(1, RWKV_DIM),
        'w_lora_a': jnp.concatenate([zpad, w_lora_a[l]], axis=0),
        'w_lora_g': w_lora_g[l],
        'k_k': k_k[l].reshape(1, RWKV_DIM),
        'k_a': k_a[l].reshape(1, RWKV_DIM),
        'r_k': r_k[l].reshape(1, RWKV_DIM),
        'w_out_a': w_out_a[l].astype(BF16),
        'ones_bf16': ones_bf16,
        'lnx_g': lnx_g[l].reshape(1, RWKV_DIM),
        'lnx_b': lnx_b[l].reshape(1, RWKV_DIM),
        'w_out_b': w_out_b[l].astype(BF16),
        'w_o': w_o[l].astype(BF16),
        'norm2_g': norm2_g[l].reshape(1, d),
        'w_router': jnp.pad(jnp.concatenate([w_router_group[l], w_router_expert[l]], axis=1),
                            ((0, 0), (0, LANES - n_r))),
        'b_router': jnp.pad(jnp.concatenate([b_router_group[l], b_router_expert[l]]),
                            (0, LANES - n_r)).reshape(1, LANES),
    }
    st_conv_t = state_conv[l].reshape(n_sample_tiles, seqs * (CONV_WIDTH - 1), CONV_DIM)
    st_shift_t = state_shift[l].reshape(n_sample_tiles, seqs, SHIFT_DIM)

    (yag, sgb, r, w, k, v, a, b, bonus, g, ctail, stail) = _mixer_pre(
        x_p, x_s, st_conv_t, st_shift_t, p, seq_len=dseq, tm=tm)

    s0_prompt = jnp.zeros((1, N_HEAD_GROUPS, GROUP_LANES, GROUP_LANES), F32)
    s0_sample = _state_to_kernel(state_wkv[l])
    rlkvab = (r, w, k, v, a, b)
    cp = min(SCAN_CHUNK, n_p)
    assert cp & (cp - 1) == 0 and dseq & (dseq - 1) == 0
    y_p, s_p = _wkv_scan(rlkvab, s0_prompt, row0=0, n_rows=n_p, chunk=cp,
                         chunks_per_step=SCAN_CHUNKS_PER_STEP, chained=True)
    y_s, s_s = _wkv_scan(rlkvab, s0_sample, row0=n_p, n_rows=n_s, chunk=dseq,
                         chunks_per_step=SCAN_CHUNKS_PER_STEP, chained=False)

    x1, h2, route, counts = _mixer_post(y_p, y_s, bonus, g, yag, sgb, x_p, x_s, p, tm=tm)

    blk = MOE_ROWS
    pos_tiles, block_expert, n_used, n_blocks = _dispatch(route, counts, blk, tm)
    xs = _moe_scatter(h2, pos_tiles, n_blocks * blk)
    yb = _moe_experts(xs, block_expert, n_used, w_e_gate[l], w_e_up[l], w_e_down[l], blk=blk)
    out_p, out_s = _moe_combine(pos_tiles, x1, route, normf_g.reshape(1, d), yb, n_prompt_rows=n_p, tm=tm)

    y_prompt = out_p.reshape(bp, seq, d)
    y_sample = out_s.reshape(db, dseq, d)
    conv_p = ctail[n_prompt_tiles - 1, 2 * (seqs - 1):2 * seqs].reshape(1, bp, CONV_WIDTH - 1, CONV_DIM)
    shift_p = stail[n_prompt_tiles - 1, seqs - 1].reshape(1, bp, 1, SHIFT_DIM)
    wkv_p = _state_from_kernel(s_p).reshape(1, bp, RWKV_HEADS, RWKV_HEAD, RWKV_HEAD)
    conv_s = ctail[n_prompt_tiles:].reshape(1, db, CONV_WIDTH - 1, CONV_DIM)
    shift_s = stail[n_prompt_tiles:].reshape(1, db, 1, SHIFT_DIM)
    wkv_s = _state_from_kernel(s_s).reshape(1, db, RWKV_HEADS, RWKV_HEAD, RWKV_HEAD)
    return (y_prompt, y_sample, conv_p, shift_p, wkv_p, conv_s, shift_s, wkv_s)
```

```python
import functools

import numpy as np
import jax
import jax.numpy as jnp
from jax import lax
from jax.experimental import pallas as pl
from jax.experimental.pallas import tpu as pltpu

F32 = jnp.float32
BF16 = jnp.bfloat16

CONV_DIM = 512
CONV_WIDTH = 3
RWKV_HEAD = 64
RWKV_HEADS = 8
RWKV_DIM = RWKV_HEADS * RWKV_HEAD
LORA_W = 64
LORA_A = 64
LORA_G = 128
SHIFT_DIM = 3 * RWKV_DIM + LORA_W + LORA_A + LORA_G
N_GROUPS = 4
EXPERTS_PER_GROUP = 8
N_EXPERTS = N_GROUPS * EXPERTS_PER_GROUP
TOP_K = 2
RMS_EPS = 1e-6
GN_EPS = 64e-5

LANES = 128
ROW_TILE = 256
SCAN_CHUNK = 64
SCAN_CHUNKS_PER_STEP = 4
MOE_ROWS = 256
HIST = 8
VMEM_LIMIT = 56 * 1024 * 1024


def _rms(x, g):
    return x * lax.rsqrt(jnp.mean(x * x, axis=-1, keepdims=True) + RMS_EPS) * g


def _split2_dot(x, ones_bf16):
    hi = x.astype(BF16)
    lo = (x - hi.astype(F32)).astype(BF16)
    return (jnp.dot(hi, ones_bf16, preferred_element_type=F32)
            + jnp.dot(lo, ones_bf16, preferred_element_type=F32))


def _stack3(w):
    hi = w.astype(BF16)
    lo = (w - hi.astype(F32)).astype(BF16)
    return jnp.concatenate([hi, hi, lo], axis=0)


def _dot_stack3(x, w_stack):
    hi = x.astype(BF16)
    lo = (x - hi.astype(F32)).astype(BF16)
    return jnp.dot(jnp.concatenate([hi, lo, hi], axis=1), w_stack, preferred_element_type=F32)


def _const_spec(shape):
    nd = len(shape)
    return pl.BlockSpec(shape, lambda *_: (0,) * nd)


def _mixer_pre_kernel(xp_ref, xs_ref, stc_ref, sts_ref, n1_ref, wa_ref, wb_ref, wg_ref, convw_ref, mu_ref,
                      w0_ref, lw_ref, a0_ref, la_ref, lgw_ref, kk_ref, ka_ref, rk_ref, woa_ref,
                      ones_ref,
                      yag_ref, sgb_ref, r_ref, w_ref, k_ref, v_ref, a_ref, b_ref, bonus_ref, g_ref,
                      ctail_ref, stail_ref,
                      cbuf, sbuf, p1buf, p2buf, spbuf, *, n_prompt_tiles, seq_len):
    i = pl.program_id(0)
    tm = xp_ref.shape[0]
    seqs = tm // seq_len

    @pl.when(i == 0)
    def _():
        cbuf[0:HIST, :] = jnp.zeros((HIST, CONV_DIM), F32)
        sbuf[0:HIST, :] = jnp.zeros((HIST, SHIFT_DIM), F32)

    x = jnp.where(i < n_prompt_tiles, xp_ref[...], xs_ref[...])
    h = _rms(x, n1_ref[...]).astype(BF16)

    pa = jnp.dot(h, wa_ref[...], preferred_element_type=F32)
    g_in = pa[:, 0:CONV_DIM]
    g_out = pa[:, CONV_DIM:2 * CONV_DIM]
    x_c = pa[:, 2 * CONV_DIM:3 * CONV_DIM]
    bx = g_in * x_c
    cbuf[HIST:HIST + tm, :] = bx
    p1buf[...] = cbuf[HIST - 1:HIST - 1 + tm, :]
    p2buf[...] = cbuf[HIST - 2:HIST - 2 + tm, :]

    @pl.when(i >= n_prompt_tiles)
    def _():
        for j in range(seqs):
            r0 = j * seq_len
            p1buf[r0:r0 + 1, :] = stc_ref[0, 2 * j + 1:2 * j + 2, :]
            p2buf[r0:r0 + 1, :] = stc_ref[0, 2 * j:2 * j + 1, :]
            p2buf[r0 + 1:r0 + 2, :] = stc_ref[0, 2 * j + 1:2 * j + 2, :]

    cw = convw_ref[...]
    conv = cw[0:1, :] * p2buf[...] + cw[1:2, :] * p1buf[...] + cw[2:3, :] * bx
    y_a = jnp.dot((g_out * conv).astype(BF16), woa_ref[...], preferred_element_type=F32)
    for j in range(seqs):
        r1 = HIST + (j + 1) * seq_len
        ctail_ref[0, 2 * j:2 * j + 2, :] = cbuf[r1 - 2:r1, :]
    cbuf[HIST - 2:HIST, :] = cbuf[HIST + tm - 2:HIST + tm, :]

    pg = jnp.dot(h, wg_ref[...], preferred_element_type=F32)
    d = pg.shape[1] // 2
    yag_ref[...] = jax.nn.sigmoid(pg[:, 0:d]) * y_a
    sgb_ref[...] = jax.nn.sigmoid(pg[:, d:2 * d])

    pb = jnp.dot(h, wb_ref[...], preferred_element_type=F32)
    sbuf[HIST:HIST + tm, :] = pb
    spbuf[...] = sbuf[HIST - 1:HIST - 1 + tm, :]

    @pl.when(i >= n_prompt_tiles)
    def _():
        for j in range(seqs):
            r0 = j * seq_len
            spbuf[r0:r0 + 1, :] = sts_ref[0, j:j + 1, :]

    for j in range(seqs):
        r1 = HIST + (j + 1) * seq_len
        stail_ref[0, j:j + 1, :] = sbuf[r1 - 1:r1, :]
    sbuf[HIST - 1:HIST, :] = sbuf[HIST + tm - 1:HIST + tm, :]

    s = pb + (spbuf[...] - pb) * mu_ref[...]
    o1, o2, o3 = RWKV_DIM, 2 * RWKV_DIM, 3 * RWKV_DIM
    r = s[:, 0:o1]
    k = s[:, o1:o2]
    v = s[:, o2:o3]
    s_l = s[:, o3:o3 + LORA_W + LORA_A]
    lg = s[:, o3 + LORA_W + LORA_A:]
    z = w0_ref[...] + _dot_stack3(jnp.tanh(s_l), lw_ref[...])
    w_log = -jax.nn.softplus(-z) - 0.5
    log_decay = -jnp.exp(w_log)
    a = jax.nn.sigmoid(a0_ref[...] + _dot_stack3(s_l, la_ref[...]))
    g = _dot_stack3(jax.nn.sigmoid(lg), lgw_ref[...])

    ones = ones_ref[...]
    kk = k * kk_ref[...]
    kk_n = kk / jnp.maximum(jnp.sqrt(_split2_dot(kk * kk, ones)), 1e-12)
    k2 = k * (1.0 + (a - 1.0) * ka_ref[...])
    bonus = _split2_dot(r * k2 * rk_ref[...], ones) * v

    r_ref[...] = r
    w_ref[...] = log_decay
    k_ref[...] = k2
    v_ref[...] = v
    a_ref[...] = -kk_n
    b_ref[...] = kk_n * a
    bonus_ref[...] = bonus
    g_ref[...] = g


def _mixer_pre(x_p, x_s, st_conv_t, st_shift_t, p, *, seq_len, tm):
    n_p, d = x_p.shape
    n_tok = n_p + x_s.shape[0]
    n_prompt_tiles = n_p // tm
    n_tiles = n_tok // tm
    seqs = tm // seq_len
    row = lambda w: pl.BlockSpec((tm, w), lambda i: (i, 0))
    st_idx = lambda i: (jnp.maximum(i - n_prompt_tiles, 0), 0, 0)
    consts = [p['norm1_g'], p['w_in_a'], p['w_in_b'], p['w_in_g'], p['conv_w'], p['mu_shift'],
              p['w0'], p['w_lora_w'], p['a0'], p['w_lora_a'], p['w_lora_g'], p['k_k'], p['k_a'],
              p['r_k'], p['w_out_a'], p['ones_bf16']]
    in_specs = [pl.BlockSpec((tm, d), lambda i: (jnp.minimum(i, n_prompt_tiles - 1), 0)),
                pl.BlockSpec((tm, d), lambda i: (jnp.maximum(i - n_prompt_tiles, 0), 0)),
                pl.BlockSpec((1, 2 * seqs, CONV_DIM), st_idx),
                pl.BlockSpec((1, seqs, SHIFT_DIM), st_idx)] + [_const_spec(c.shape) for c in consts]
    sds = lambda w: jax.ShapeDtypeStruct((n_tok, w), F32)
    out_shape = [sds(d), sds(d)] + [sds(RWKV_DIM)] * 8 + [
        jax.ShapeDtypeStruct((n_tiles, 2 * seqs, CONV_DIM), F32),
        jax.ShapeDtypeStruct((n_tiles, seqs, SHIFT_DIM), F32)]
    out_specs = [row(d), row(d)] + [row(RWKV_DIM)] * 8 + [
        pl.BlockSpec((1, 2 * seqs, CONV_DIM), lambda i: (i, 0, 0)),
        pl.BlockSpec((1, seqs, SHIFT_DIM), lambda i: (i, 0, 0))]
    kern = functools.partial(_mixer_pre_kernel, n_prompt_tiles=n_prompt_tiles, seq_len=seq_len)
    return pl.pallas_call(
        kern, out_shape=out_shape, grid=(n_tiles,), in_specs=in_specs, out_specs=out_specs,
        scratch_shapes=[pltpu.VMEM((tm + HIST, CONV_DIM), F32), pltpu.VMEM((tm + HIST, SHIFT_DIM), F32),
                        pltpu.VMEM((tm, CONV_DIM), F32), pltpu.VMEM((tm, CONV_DIM), F32),
                        pltpu.VMEM((tm, SHIFT_DIM), F32)],
        compiler_params=pltpu.CompilerParams(dimension_semantics=("arbitrary",),
                                             vmem_limit_bytes=VMEM_LIMIT),
        name="mixer_pre")(x_p, x_s, st_conv_t, st_shift_t, *consts)


GROUP_LANES = 256
HEADS_PER_GROUP = GROUP_LANES // RWKV_HEAD
N_HEAD_GROUPS = RWKV_DIM // GROUP_LANES
NN = (((1,), (0,)), ((), ()))
NT = (((1,), (1,)), ((), ()))


def _split2(x):
    hi = x.astype(BF16)
    lo = (x - hi.astype(F32)).astype(BF16)
    return hi, lo


def _mm(xs, ys, dims=NN):
    x1, x2 = xs
    y1, y2 = ys
    d = lambda p, q: lax.dot_general(p, q, dims, preferred_element_type=F32)
    m = x1.shape[0]
    both = d(jnp.concatenate([x1, x2], axis=0), y1)
    return both[0:m] + both[m:2 * m] + d(x1, y2)


def _cat2(ps, qs, axis):
    return tuple(jnp.concatenate([p, q], axis=axis) for p, q in zip(ps, qs))


def _wkv_chunk_kernel(r_ref, lw_ref, k_ref, v_ref, a_ref, b_ref, s0_ref, mfeat_ref, mpos_ref,
                      tmask_ref, stmask_ref, tri_ref, y_ref, sout_ref, state,
                      *, chunk, steps_per_seq):
    i = pl.program_id(0)
    c = chunk
    n_chunks = r_ref.shape[0] // c
    hc = HEADS_PER_GROUP * c
    levels = c.bit_length() - 1
    chained = steps_per_seq is not None
    groups = range(N_HEAD_GROUPS)
    units = [(j, g) for j in range(n_chunks) for g in groups]

    if chained:
        @pl.when(i % steps_per_seq == 0)
        def _():
            state[...] = s0_ref[0]

    mfeat = mfeat_ref[...]
    mpos = mpos_ref[...]
    strict = tmask_ref[0]
    incl = tmask_ref[1]
    eye = incl - strict
    tri = tri_ref[...]

    def ld(ref, j, g):
        return ref[j * c:(j + 1) * c, g * GROUP_LANES:(g + 1) * GROUP_LANES]

    def bd_split(ps):
        mask = mpos if ps[0].shape[1] == hc else mfeat
        return tuple(jnp.concatenate([p] * HEADS_PER_GROUP, axis=0) * mask for p in ps)

    def bd2(m):
        return bd_split(_split2(m))

    def cumsum_rows(x):
        p1 = x.astype(BF16)
        r1 = x - p1.astype(F32)
        p2 = r1.astype(BF16)
        p3 = (r1 - p2.astype(F32)).astype(BF16)
        d = lambda q: jnp.dot(tri, q, preferred_element_type=F32)
        return d(p1) + d(p2) + d(p3)

    cum = {u: cumsum_rows(ld(lw_ref, *u)) for u in units}
    ar, bk_end, p_end, a_ab, a_ak, a_rb, a_rk, v, vbd = ({} for _ in range(9))
    for u in units:
        cm = cum[u]
        cum_last = cm[c - 1:c, :]
        e_neg = jnp.exp(-cm)
        e_end = jnp.exp(cum_last - cm)
        b_raw = ld(b_ref, *u)
        k_raw = ld(k_ref, *u)
        ar[u] = _split2(jnp.concatenate([ld(a_ref, *u) * jnp.exp(cm - ld(lw_ref, *u)),
                                         ld(r_ref, *u) * jnp.exp(cm)], axis=0))
        bk_end[u] = _split2(jnp.concatenate([b_raw * e_end, k_raw * e_end], axis=0))
        p_end[u] = jnp.exp(cum_last)
        v[u] = ld(v_ref, *u)
        vbd[u] = bd2(v[u])
        gram = _mm(ar[u], _cat2(bd2(b_raw * e_neg), bd2(k_raw * e_neg), 0), NT)
        a_ab[u] = jnp.where(strict > 0, gram[0:c, 0:hc], 0.0)
        a_ak[u] = jnp.where(strict > 0, gram[0:c, hc:2 * hc], 0.0)
        a_rb[u] = jnp.where(incl > 0, gram[c:2 * c, 0:hc], 0.0)
        a_rk[u] = jnp.where(incl > 0, gram[c:2 * c, hc:2 * hc], 0.0)

    a_ab2 = {u: _split2(a_ab[u]) for u in units}
    inv = {u: eye + a_ab[u] * tmask_ref[2] for u in units}
    inv2 = {u: _split2(inv[u]) for u in units}
    for lvl in range(2, levels + 1):
        lm = tmask_ref[1 + lvl]
        t1 = {u: _mm(a_ab2[u], bd_split(inv2[u])) for u in units}
        inv = {u: inv[u] + lm * _mm(inv2[u], bd2(t1[u])) for u in units}
        inv2 = {u: _split2(inv[u]) for u in units}
    akv = {u: _mm(_split2(a_ak[u]), vbd[u]) for u in units}
    arbk = {u: _split2(jnp.concatenate([a_rb[u], a_rk[u]], axis=1)) for u in units}

    stmask = stmask_ref[...]
    s_cur = [state[g] for g in groups] if chained else None
    for j in range(n_chunks):
        s_prev = s_cur if chained else [s0_ref[j, g] for g in groups]
        x0 = [_mm(ar[j, g], _split2(s_prev[g]), NT) for g in groups]
        uu = [_mm(inv2[j, g], bd2(x0[g][0:c] + akv[j, g])) for g in groups]
        for g in groups:
            yy = x0[g][c:2 * c] + _mm(arbk[j, g], _cat2(bd2(uu[g]), vbd[j, g], 0))
            y_ref[j * c:(j + 1) * c, g * GROUP_LANES:(g + 1) * GROUP_LANES] = yy
        s_new = []
        for g in groups:
            uv_t = jnp.transpose(jnp.concatenate([uu[g], v[j, g]], axis=0))
            upd = _mm(_split2(uv_t), bk_end[j, g])
            s_new.append(s_prev[g] * p_end[j, g] + stmask * upd)
        if chained:
            s_cur = s_new
        else:
            for g in groups:
                sout_ref[j, g] = s_new[g]

    if chained:
        for g in groups:
            state[g] = s_cur[g]

        @pl.when(i % steps_per_seq == steps_per_seq - 1)
        def _():
            sout_ref[0] = state[...]


def _wkv_masks(c):
    hc = HEADS_PER_GROUP * c
    levels = c.bit_length() - 1
    t = np.arange(c)[:, None]
    s = (np.arange(hc) % c)[None, :]
    tm = [s < t, s <= t]
    for lvl in range(1, levels + 1):
        half = 1 << (lvl - 1)
        tm.append(((t >> lvl) == (s >> lvl)) & ((t & half) != 0) & ((s & half) == 0))
    row_head = (np.arange(hc) // c)[:, None]
    mfeat = row_head == (np.arange(GROUP_LANES) // RWKV_HEAD)[None, :]
    mpos = row_head == (np.arange(hc) // c)[None, :]
    lane_head = np.arange(GROUP_LANES) // RWKV_HEAD
    stmask = lane_head[:, None] == lane_head[None, :]
    tri = np.arange(c)[None, :] <= np.arange(c)[:, None]
    return (jnp.asarray(mfeat, BF16), jnp.asarray(mpos, BF16), jnp.asarray(np.stack(tm), F32),
            jnp.asarray(stmask, F32), jnp.asarray(tri, BF16))


def _wkv_scan(rlkvab, s0, *, row0, n_rows, chunk, chunks_per_step, chained):
    rows = chunk * chunks_per_step
    n_steps = n_rows // rows
    assert n_rows % rows == 0 and row0 % rows == 0
    masks = _wkv_masks(chunk)
    st = (N_HEAD_GROUPS, GROUP_LANES, GROUP_LANES)
    row = pl.BlockSpec((rows, RWKV_DIM), lambda i: (row0 // rows + i, 0))
    if chained:
        n_state = 1
        st_spec = pl.BlockSpec((1,) + st, lambda i: (0, 0, 0, 0))
    else:
        n_state = n_rows // chunk
        st_spec = pl.BlockSpec((chunks_per_step,) + st, lambda i: (i, 0, 0, 0))
    in_specs = [row] * 6 + [st_spec] + [_const_spec(m.shape) for m in masks]
    out_shape = [jax.ShapeDtypeStruct((n_rows, RWKV_DIM), F32),
                 jax.ShapeDtypeStruct((n_state,) + st, F32)]
    out_specs = [pl.BlockSpec((rows, RWKV_DIM), lambda i: (i, 0)), st_spec]
    kern = functools.partial(_wkv_chunk_kernel, chunk=chunk, steps_per_seq=n_steps if chained else None)
    return pl.pallas_call(
        kern, out_shape=out_shape, grid=(n_steps,), in_specs=in_specs, out_specs=out_specs,
        scratch_shapes=[pltpu.VMEM(st, F32)],
        compiler_params=pltpu.CompilerParams(dimension_semantics=("arbitrary",),
                                             vmem_limit_bytes=VMEM_LIMIT),
        name="wkv_scan")(*rlkvab, s0, *masks)


def _mixer_post_kernel(yp_ref, ys_ref, bonus_ref, g_ref, yag_ref, sgb_ref, xp_ref, xs_ref, lng_ref, lnb_ref,
                       wob_ref, wo_ref, n2_ref, wr_ref, br_ref, ones_ref, tril_ref,
                       x1_ref, h2_ref, route_ref, counts_ref, cnt, *, n_prompt_tiles):
    i = pl.program_id(0)

    @pl.when(i == 0)
    def _():
        cnt[...] = jnp.zeros(cnt.shape, F32)

    x = jnp.where(i < n_prompt_tiles, xp_ref[...], xs_ref[...])
    ones = ones_ref[...]
    y = jnp.where(i < n_prompt_tiles, yp_ref[...], ys_ref[...])
    inv_n = 1.0 / RWKV_HEAD
    mean = _split2_dot(y, ones) * inv_n
    yc = y - mean
    var = _split2_dot(yc * yc, ones) * inv_n
    yn = yc * lax.rsqrt(var + GN_EPS) * lng_ref[...] + lnb_ref[...]
    yy = (yn + bonus_ref[...]) * g_ref[...]
    y_b = jnp.dot(yy.astype(BF16), wob_ref[...], preferred_element_type=F32)
    merged = yag_ref[...] + sgb_ref[...] * y_b
    x1 = x + jnp.dot(merged.astype(BF16), wo_ref[...], preferred_element_type=F32)
    x1_ref[...] = x1
    h2 = _rms(x1, n2_ref[...])
    h2_ref[...] = h2

    logits = _dot_stack3(h2, wr_ref[...]) + br_ref[...]
    lane = lax.broadcasted_iota(jnp.int32, logits.shape, 1)
    neg = jnp.float32(-jnp.inf)
    big = jnp.int32(LANES)
    is_g = lane < N_GROUPS
    lgp = jnp.where(is_g, logits, neg)
    m_g = jnp.max(lgp, axis=-1, keepdims=True)
    grp = jnp.min(jnp.where(lgp == m_g, lane, big), axis=-1, keepdims=True)
    p_top = 1.0 / jnp.sum(jnp.where(is_g, jnp.exp(logits - m_g), 0.0), axis=-1, keepdims=True)
    e_lane = lane - N_GROUPS
    in_grp = (e_lane >= grp * EXPERTS_PER_GROUP) & (e_lane < (grp + 1) * EXPERTS_PER_GROUP)
    le = jnp.where(in_grp, logits, neg)
    m1 = jnp.max(le, axis=-1, keepdims=True)
    i1 = jnp.min(jnp.where(le == m1, lane, big), axis=-1, keepdims=True)
    le2 = jnp.where(lane == i1, neg, le)
    m2 = jnp.max(le2, axis=-1, keepdims=True)
    i2 = jnp.min(jnp.where(le2 == m2, lane, big), axis=-1, keepdims=True)
    ex = jnp.exp(m2 - m1)
    p1 = 1.0 / (1.0 + ex)
    p2 = ex / (1.0 + ex)
    oh1 = lane == i1 - N_GROUPS
    oh2 = lane == i2 - N_GROUPS
    both = jnp.where(oh1, 1.0, jnp.where(oh2, 1.0, 0.0))
    before = jnp.dot(tril_ref[...], both.astype(BF16), preferred_element_type=F32) + cnt[0:1, :]
    rank1 = jnp.sum(jnp.where(oh1, before, 0.0), axis=-1, keepdims=True)
    rank2 = jnp.sum(jnp.where(oh2, before, 0.0), axis=-1, keepdims=True)
    cnt[0:1, :] = cnt[0:1, :] + jnp.sum(both, axis=0, keepdims=True)
    counts_ref[...] = jnp.broadcast_to(cnt[0:1, :], counts_ref.shape)

    cols = [(i1 - N_GROUPS).astype(F32), (i2 - N_GROUPS).astype(F32), p_top * p1, p_top * p2,
            rank1, rank2]
    route = jnp.zeros(logits.shape, F32)
    for c, col in enumerate(cols):
        route = jnp.where(lane == c, col, route)
    route_ref[...] = route


ROUTE_EXPERT, ROUTE_WEIGHT, ROUTE_RANK = 0, 2, 4


def _mixer_post(y_p, y_s, bonus, g, yag, sgb, x_p, x_s, p, *, tm):
    n_p, d = x_p.shape
    n_tok = n_p + x_s.shape[0]
    npt = n_p // tm
    row = lambda w: pl.BlockSpec((tm, w), lambda i: (i, 0))
    tril = jnp.asarray(np.arange(tm)[None, :] < np.arange(tm)[:, None], BF16)
    consts = [p['lnx_g'], p['lnx_b'], p['w_out_b'], p['w_o'], p['norm2_g'], p['w_router'],
              p['b_router'], p['ones_bf16'], tril]
    pair = lambda w: [pl.BlockSpec((tm, w), lambda i: (jnp.minimum(i, npt - 1), 0)),
                      pl.BlockSpec((tm, w), lambda i: (jnp.maximum(i - npt, 0), 0))]
    in_specs = (pair(RWKV_DIM) + [row(RWKV_DIM)] * 2 + [row(d)] * 2 + pair(d)
                + [_const_spec(c.shape) for c in consts])
    out_shape = [jax.ShapeDtypeStruct((n_tok, d), F32), jax.ShapeDtypeStruct((n_tok, d), F32),
                 jax.ShapeDtypeStruct((n_tok, LANES), F32), jax.ShapeDtypeStruct((8, LANES), F32)]
    out_specs = [row(d), row(d), row(LANES), _const_spec((8, LANES))]
    return pl.pallas_call(
        functools.partial(_mixer_post_kernel, n_prompt_tiles=npt), out_shape=out_shape,
        grid=(n_tok // tm,), in_specs=in_specs, out_specs=out_specs,
        scratch_shapes=[pltpu.VMEM((8, LANES), F32)],
        compiler_params=pltpu.CompilerParams(dimension_semantics=("arbitrary",),
                                             vmem_limit_bytes=VMEM_LIMIT),
        name="mixer_post")(y_p, y_s, bonus, g, yag, sgb, x_p, x_s, *consts)


def _gather_rows(idx_ref, src_hbm, dst, sem, n_rows, *, unrolled):
    def start(r):
        pltpu.make_async_copy(src_hbm.at[pl.ds(idx_ref[r], 1)], dst.at[pl.ds(r, 1)], sem).start()
    if unrolled:
        for r in range(n_rows):
            start(r)
    else:
        def body(r, carry):
            start(r)
            return carry
        lax.fori_loop(0, n_rows, body, 0)


def _wait_rows(src_hbm, dst, sem, n_rows):
    pltpu.make_async_copy(src_hbm.at[pl.ds(0, n_rows)], dst, sem).wait()


def _moe_scatter_kernel(dest_ref, h_ref, xs_in_hbm, xs_hbm, sem):
    del xs_in_hbm
    n = dest_ref.shape[-1]
    tm = h_ref.shape[0]
    for r in range(n):
        pltpu.make_async_copy(h_ref.at[pl.ds(r % tm, 1)],
                              xs_hbm.at[pl.ds(dest_ref[0, 0, r], 1)], sem).start()
    for j in range(n // tm):
        pltpu.make_async_copy(h_ref, xs_hbm.at[pl.ds(0, tm)], sem).wait()


def _moe_scatter(h2, pos3, n_rows):
    n_tok, d = h2.shape
    n_tiles, _, n = pos3.shape
    tm = n // TOP_K
    xs0 = jnp.zeros((n_rows, d), h2.dtype)
    return pl.pallas_call(
        _moe_scatter_kernel, out_shape=jax.ShapeDtypeStruct((n_rows, d), h2.dtype), grid=(n_tiles,),
        in_specs=[pl.BlockSpec((1, 1, n), lambda i: (i, 0, 0), memory_space=pltpu.SMEM),
                  pl.BlockSpec((tm, d), lambda i: (i, 0)), pl.BlockSpec(memory_space=pl.ANY)],
        out_specs=pl.BlockSpec(memory_space=pl.ANY),
        scratch_shapes=[pltpu.SemaphoreType.DMA(())], input_output_aliases={2: 0},
        compiler_params=pltpu.CompilerParams(dimension_semantics=("arbitrary",), has_side_effects=True,
                                             vmem_limit_bytes=VMEM_LIMIT),
        name="moe_scatter")(pos3, h2, xs0)


def _moe_experts_kernel(bexp_ref, nused_ref, x_ref, wg_ref, wu_ref, wd_ref, yb_ref, wg_bf, wu_bf, wd_bf):
    i = pl.program_id(0)

    @pl.when((i == 0) | (bexp_ref[i] != bexp_ref[jnp.maximum(i - 1, 0)]))
    def _():
        wg_bf[...] = wg_ref[0].astype(BF16)
        wu_bf[...] = wu_ref[0].astype(BF16)
        wd_bf[...] = wd_ref[0].astype(BF16)

    @pl.when(i < nused_ref[0])
    def _():
        xb = x_ref[...].astype(BF16)
        hg = jnp.dot(xb, wg_bf[...], preferred_element_type=F32)
        hu = jnp.dot(xb, wu_bf[...], preferred_element_type=F32)
        hid = (hg * jax.nn.sigmoid(hg)) * hu
        yb_ref[...] = jnp.dot(hid.astype(BF16), wd_bf[...], preferred_element_type=F32)

    @pl.when(i >= nused_ref[0])
    def _():
        yb_ref[...] = jnp.zeros(yb_ref.shape, F32)


def _moe_experts(xs, block_expert, n_used, w_eg, w_eu, w_ed, *, blk):
    n_rows, d = xs.shape
    n_blocks = n_rows // blk
    de = w_eg.shape[2]
    grid_spec = pltpu.PrefetchScalarGridSpec(
        num_scalar_prefetch=2, grid=(n_blocks,),
        in_specs=[
            pl.BlockSpec((blk, d), lambda i, be, nu: (i, 0)),
            pl.BlockSpec((1, d, de), lambda i, be, nu: (be[i], 0, 0)),
            pl.BlockSpec((1, d, de), lambda i, be, nu: (be[i], 0, 0)),
            pl.BlockSpec((1, de, d), lambda i, be, nu: (be[i], 0, 0)),
        ],
        out_specs=pl.BlockSpec((blk, d), lambda i, be, nu: (i, 0)),
        scratch_shapes=[pltpu.VMEM((d, de), BF16), pltpu.VMEM((d, de), BF16), pltpu.VMEM((de, d), BF16)])
    return pl.pallas_call(
        _moe_experts_kernel, out_shape=jax.ShapeDtypeStruct((n_rows, d), F32), grid_spec=grid_spec,
        compiler_params=pltpu.CompilerParams(dimension_semantics=("arbitrary",),
                                             vmem_limit_bytes=VMEM_LIMIT),
        name="moe_experts")(block_expert, n_used, xs, w_eg, w_eu, w_ed)


def _with_dummy_block(idx, n_blocks, blk):
    return jnp.concatenate([idx, jnp.zeros((blk,), idx.dtype)]).reshape(n_blocks + 1, 1, blk)


def _moe_combine_kernel(pos_cur_ref, pos_nxt_ref, x1_ref, route_ref, nf_ref, yb_hbm, outp_ref,
                        outs_ref, ybuf, sems, *, n_prompt_tiles):
    i = pl.program_id(0)
    rows = ybuf.shape[1]
    slot = i % 2

    @pl.when(i == 0)
    def _():
        _gather_rows(pos_cur_ref.at[0, 0], yb_hbm, ybuf.at[0], sems.at[0], rows, unrolled=False)

    _wait_rows(yb_hbm, ybuf.at[slot], sems.at[slot], rows)
    _gather_rows(pos_nxt_ref.at[0, 0], yb_hbm, ybuf.at[1 - slot], sems.at[1 - slot], rows, unrolled=True)
    tm = x1_ref.shape[0]
    route = route_ref[...]
    x2 = x1_ref[...]
    for j in range(TOP_K):
        x2 = x2 + ybuf[slot, j * tm:(j + 1) * tm, :] * route[:, ROUTE_WEIGHT + j:ROUTE_WEIGHT + j + 1]
    out = _rms(x2, nf_ref[...])

    @pl.when(i < n_prompt_tiles)
    def _():
        outp_ref[...] = out

    @pl.when(i >= n_prompt_tiles)
    def _():
        outs_ref[...] = out

    @pl.when(i == pl.num_programs(0) - 1)
    def _():
        _wait_rows(yb_hbm, ybuf.at[1 - slot], sems.at[1 - slot], rows)


def _moe_combine(pos_tiles, x1, route, normf_g, yb, *, n_prompt_rows, tm):
    n_tok, d = x1.shape
    n_tiles = n_tok // tm
    npt = n_prompt_rows // tm
    pos3 = _with_dummy_block(pos_tiles.reshape(-1), n_tiles, TOP_K * tm)
    in_specs = [
        pl.BlockSpec((1, 1, TOP_K * tm), lambda i: (i, 0, 0), memory_space=pltpu.SMEM),
        pl.BlockSpec((1, 1, TOP_K * tm), lambda i: (i + 1, 0, 0), memory_space=pltpu.SMEM),
        pl.BlockSpec((tm, d), lambda i: (i, 0)),
        pl.BlockSpec((tm, LANES), lambda i: (i, 0)),
        _const_spec(normf_g.shape),
        pl.BlockSpec(memory_space=pl.ANY),
    ]
    out_shape = [jax.ShapeDtypeStruct((n_prompt_rows, d), F32),
                 jax.ShapeDtypeStruct((n_tok - n_prompt_rows, d), F32)]
    out_specs = [pl.BlockSpec((tm, d), lambda i: (jnp.minimum(i, npt - 1), 0)),
                 pl.BlockSpec((tm, d), lambda i: (jnp.maximum(i - npt, 0), 0))]
    return pl.pallas_call(
        functools.partial(_moe_combine_kernel, n_prompt_tiles=npt), out_shape=out_shape,
        grid=(n_tiles,), in_specs=in_specs, out_specs=out_specs,
        scratch_shapes=[pltpu.VMEM((2, TOP_K * tm, d), F32), pltpu.SemaphoreType.DMA((2,))],
        compiler_params=pltpu.CompilerParams(dimension_semantics=("arbitrary",),
                                             vmem_limit_bytes=VMEM_LIMIT),
        name="moe_combine")(pos3, pos3, x1, route, normf_g, yb)


def _dispatch(route, counts, blk, tm):
    n_tok = route.shape[0]
    n_assign = n_tok * TOP_K
    expert = route[:, ROUTE_EXPERT:ROUTE_EXPERT + TOP_K].astype(jnp.int32)
    rank = route[:, ROUTE_RANK:ROUTE_RANK + TOP_K].astype(jnp.int32)
    counts = counts[0, :N_EXPERTS].astype(jnp.int32)
    padded = (counts + blk - 1) // blk * blk
    pad_end = jnp.cumsum(padded)
    pad_start = pad_end - padded
    is_e = expert[:, :, None] == jnp.arange(N_EXPERTS, dtype=jnp.int32)
    dest = jnp.sum(jnp.where(is_e, pad_start, 0), axis=-1) + rank
    n_blocks = -(-n_assign // blk) + N_EXPERTS
    block_start = jnp.arange(n_blocks, dtype=jnp.int32) * blk
    block_expert = jnp.minimum(jnp.sum(block_start[:, None] >= pad_end[None, :], axis=1),
                               N_EXPERTS - 1).astype(jnp.int32)
    n_used = (pad_end[-1] // blk).reshape(1)
    pos_tiles = dest.reshape(n_tok // tm, tm, TOP_K).transpose(0, 2, 1).reshape(n_tok // tm, 1, TOP_K * tm)
    return pos_tiles, block_expert, n_used, n_blocks


def _state_to_kernel(s):
    b = s.shape[0]
    s6 = s.reshape(b, N_HEAD_GROUPS, HEADS_PER_GROUP, RWKV_HEAD, 1, RWKV_HEAD)
    eye = jnp.eye(HEADS_PER_GROUP, dtype=s.dtype).reshape(1, 1, HEADS_PER_GROUP, 1, HEADS_PER_GROUP, 1)
    return (s6 * eye).reshape(b, N_HEAD_GROUPS, GROUP_LANES, GROUP_LANES)


def _state_from_kernel(s):
    b = s.shape[0]
    s6 = s.reshape(b, N_HEAD_GROUPS, HEADS_PER_GROUP, RWKV_HEAD, HEADS_PER_GROUP, RWKV_HEAD)
    diag = jnp.stack([s6[:, :, h, :, h, :] for h in range(HEADS_PER_GROUP)], axis=2)
    return diag.reshape(b, RWKV_HEADS, RWKV_HEAD, RWKV_HEAD)


def kernel(x_prompt, x_sample, state_conv, state_shift, state_wkv, norm1_g, w_in, conv_w, mu_shift, w0, w_lora_w, a0, w_lora_a, w_lora_g, k_k, k_a, r_k, lnx_g, lnx_b, w_out_a, w_out_b, w_o, norm2_g, w_router_group, b_router_group, w_router_expert, b_router_expert, w_e_gate, w_e_up, w_e_down, normf_g):
    depth = norm1_g.shape[0]
    bp, seq, d = x_prompt.shape
    db, dseq, _ = x_sample.shape
    assert depth == 1 and bp == 1, "single layer, single prompt stream"
    tm = ROW_TILE
    n_p, n_s = bp * seq, db * dseq
    assert n_p % tm == 0 and n_s % tm == 0 and tm % dseq == 0
    n_prompt_tiles, n_sample_tiles = n_p // tm, n_s // tm
    seqs = tm // dseq
    n_tok = n_p + n_s

    x_p = x_prompt.reshape(n_p, d)
    x_s = x_sample.reshape(n_s, d)

    l = 0
    c3 = 3 * CONV_DIM
    head_id = jnp.arange(RWKV_DIM, dtype=jnp.int32) // RWKV_HEAD
    ones_bf16 = (head_id[:, None] == head_id[None, :]).astype(BF16)
    zpad = jnp.zeros((LORA_W, RWKV_DIM), F32)
    n_r = N_GROUPS + N_EXPERTS
    p = {
        'norm1_g': norm1_g[l].reshape(1, d),
        'w_in_a': w_in[l][:, :c3].astype(BF16),
        'w_in_b': w_in[l][:, c3:c3 + SHIFT_DIM].astype(BF16),
        'w_in_g': w_in[l][:, c3 + SHIFT_DIM:].astype(BF16),
        'conv_w': conv_w[l],
        'mu_shift': mu_shift[l].reshape(1, SHIFT_DIM),
        'w0': w0[l].reshape(1, RWKV_DIM),
        'w_lora_w': _stack3(jnp.concatenate([w_lora_w[l], zpad], axis=0)),
        'a0': a0[l].reshape(1, RWKV_DIM),
        'w_lora_a': _stack3(jnp.concatenate([zpad, w_lora_a[l]], axis=0)),
        'w_lora_g': _stack3(w_lora_g[l]),
        'k_k': k_k[l].reshape(1, RWKV_DIM),
        'k_a': k_a[l].reshape(1, RWKV_DIM),
        'r_k': r_k[l].reshape(1, RWKV_DIM),
        'w_out_a': w_out_a[l].astype(BF16),
        'ones_bf16': ones_bf16,
        'lnx_g': lnx_g[l].reshape(1, RWKV_DIM),
        'lnx_b': lnx_b[l].reshape(1, RWKV_DIM),
        'w_out_b': w_out_b[l].astype(BF16),
        'w_o': w_o[l].astype(BF16),
        'norm2_g': norm2_g[l].reshape(1, d),
        'w_router': _stack3(jnp.pad(jnp.concatenate([w_router_group[l], w_router_expert[l]], axis=1),
                                    ((0, 0), (0, LANES - n_r)))),
        'b_router': jnp.pad(jnp.concatenate([b_router_group[l], b_router_expert[l]]),
                            (0, LANES - n_r)).reshape(1, LANES),
    }
    st_conv_t = state_conv[l].reshape(n_sample_tiles, seqs * (CONV_WIDTH - 1), CONV_DIM)
    st_shift_t = state_shift[l].reshape(n_sample_tiles, seqs, SHIFT_DIM)

    (yag, sgb, r, w, k, v, a, b, bonus, g, ctail, stail) = _mixer_pre(
        x_p, x_s, st_conv_t, st_shift_t, p, seq_len=dseq, tm=tm)

    s0_prompt = jnp.zeros((1, N_HEAD_GROUPS, GROUP_LANES, GROUP_LANES), F32)
    s0_sample = _state_to_kernel(state_wkv[l])
    rlkvab = (r, w, k, v, a, b)
    cp = min(SCAN_CHUNK, n_p)
    assert cp & (cp - 1) == 0 and dseq & (dseq - 1) == 0
    y_p, s_p = _wkv_scan(rlkvab, s0_prompt, row0=0, n_rows=n_p, chunk=cp,
                         chunks_per_step=SCAN_CHUNKS_PER_STEP, chained=True)
    y_s, s_s = _wkv_scan(rlkvab, s0_sample, row0=n_p, n_rows=n_s, chunk=dseq,
                         chunks_per_step=SCAN_CHUNKS_PER_STEP, chained=False)

    x1, h2, route, counts = _mixer_post(y_p, y_s, bonus, g, yag, sgb, x_p, x_s, p, tm=tm)

    blk = MOE_ROWS
    pos_tiles, block_expert, n_used, n_blocks = _dispatch(route, counts, blk, tm)
    xs = _moe_scatter(h2, pos_tiles, n_blocks * blk)
    yb = _moe_experts(xs, block_expert, n_used, w_e_gate[l], w_e_up[l], w_e_down[l], blk=blk)
    out_p, out_s = _moe_combine(pos_tiles, x1, route, normf_g.reshape(1, d), yb, n_prompt_rows=n_p, tm=tm)

    y_prompt = out_p.reshape(bp, seq, d)
    y_sample = out_s.reshape(db, dseq, d)
    conv_p = ctail[n_prompt_tiles - 1, 2 * (seqs - 1):2 * seqs].reshape(1, bp, CONV_WIDTH - 1, CONV_DIM)
    shift_p = stail[n_prompt_tiles - 1, seqs - 1].reshape(1, bp, 1, SHIFT_DIM)
    wkv_p = _state_from_kernel(s_p).reshape(1, bp, RWKV_HEADS, RWKV_HEAD, RWKV_HEAD)
    conv_s = ctail[n_prompt_tiles:].reshape(1, db, CONV_WIDTH - 1, CONV_DIM)
    shift_s = stail[n_prompt_tiles:].reshape(1, db, 1, SHIFT_DIM)
    wkv_s = _state_from_kernel(s_s).reshape(1, db, RWKV_HEADS, RWKV_HEAD, RWKV_HEAD)
    return (y_prompt, y_sample, conv_p, shift_p, wkv_p, conv_s, shift_s, wkv_s)
```

```python
import functools

import numpy as np
import jax
import jax.numpy as jnp
from jax import lax
from jax.experimental import pallas as pl
from jax.experimental.pallas import tpu as pltpu

F32 = jnp.float32
BF16 = jnp.bfloat16

CONV_DIM = 512
CONV_WIDTH = 3
RWKV_HEAD = 64
RWKV_HEADS = 8
RWKV_DIM = RWKV_HEADS * RWKV_HEAD
LORA_W = 64
LORA_A = 64
LORA_G = 128
SHIFT_DIM = 3 * RWKV_DIM + LORA_W + LORA_A + LORA_G
N_GROUPS = 4
EXPERTS_PER_GROUP = 8
N_EXPERTS = N_GROUPS * EXPERTS_PER_GROUP
TOP_K = 2
RMS_EPS = 1e-6
GN_EPS = 64e-5

LANES = 128
ROW_TILE = 256
SCAN_CHUNK = 64
SCAN_CHUNKS_PER_STEP = 4
MOE_ROWS = 256
HIST = 8
VMEM_LIMIT = 56 * 1024 * 1024


def _rms(x, g):
    return x * lax.rsqrt(jnp.mean(x * x, axis=-1, keepdims=True) + RMS_EPS) * g


def _split2_dot(x, ones_bf16):
    hi = x.astype(BF16)
    lo = (x - hi.astype(F32)).astype(BF16)
    return (jnp.dot(hi, ones_bf16, preferred_element_type=F32)
            + jnp.dot(lo, ones_bf16, preferred_element_type=F32))


def _stack3(w):
    hi = w.astype(BF16)
    lo = (w - hi.astype(F32)).astype(BF16)
    return jnp.concatenate([hi, hi, lo], axis=0)


def _dot_stack3(x, w_stack):
    hi = x.astype(BF16)
    lo = (x - hi.astype(F32)).astype(BF16)
    return jnp.dot(jnp.concatenate([hi, lo, hi], axis=1), w_stack, preferred_element_type=F32)


def _const_spec(shape):
    nd = len(shape)
    return pl.BlockSpec(shape, lambda *_: (0,) * nd)


def _mixer_pre_kernel(xp_ref, xs_ref, stc_ref, sts_ref, n1_ref, wa_ref, wb_ref, wg_ref, convw_ref, mu_ref,
                      w0_ref, lw_ref, a0_ref, la_ref, lgw_ref, kk_ref, ka_ref, rk_ref, woa_ref,
                      ones_ref,
                      yag_ref, sgb_ref, r_ref, w_ref, k_ref, v_ref, a_ref, b_ref, bonus_ref, g_ref,
                      ctail_ref, stail_ref,
                      cbuf, sbuf, p1buf, p2buf, spbuf, *, n_prompt_tiles, seq_len):
    i = pl.program_id(0)
    tm = xp_ref.shape[0]
    seqs = tm // seq_len

    @pl.when(i == 0)
    def _():
        cbuf[0:HIST, :] = jnp.zeros((HIST, CONV_DIM), F32)
        sbuf[0:HIST, :] = jnp.zeros((HIST, SHIFT_DIM), F32)

    x = jnp.where(i < n_prompt_tiles, xp_ref[...], xs_ref[...])
    h = _rms(x, n1_ref[...]).astype(BF16)

    pa = jnp.dot(h, wa_ref[...], preferred_element_type=F32)
    g_in = pa[:, 0:CONV_DIM]
    g_out = pa[:, CONV_DIM:2 * CONV_DIM]
    x_c = pa[:, 2 * CONV_DIM:3 * CONV_DIM]
    bx = g_in * x_c
    cbuf[HIST:HIST + tm, :] = bx
    p1buf[...] = cbuf[HIST - 1:HIST - 1 + tm, :]
    p2buf[...] = cbuf[HIST - 2:HIST - 2 + tm, :]

    @pl.when(i >= n_prompt_tiles)
    def _():
        for j in range(seqs):
            r0 = j * seq_len
            p1buf[r0:r0 + 1, :] = stc_ref[0, 2 * j + 1:2 * j + 2, :]
            p2buf[r0:r0 + 1, :] = stc_ref[0, 2 * j:2 * j + 1, :]
            p2buf[r0 + 1:r0 + 2, :] = stc_ref[0, 2 * j + 1:2 * j + 2, :]

    cw = convw_ref[...]
    conv = cw[0:1, :] * p2buf[...] + cw[1:2, :] * p1buf[...] + cw[2:3, :] * bx
    y_a = jnp.dot((g_out * conv).astype(BF16), woa_ref[...], preferred_element_type=F32)
    for j in range(seqs):
        r1 = HIST + (j + 1) * seq_len
        ctail_ref[0, 2 * j:2 * j + 2, :] = cbuf[r1 - 2:r1, :]
    cbuf[HIST - 2:HIST, :] = cbuf[HIST + tm - 2:HIST + tm, :]

    pg = jnp.dot(h, wg_ref[...], preferred_element_type=F32)
    d = pg.shape[1] // 2
    yag_ref[...] = jax.nn.sigmoid(pg[:, 0:d]) * y_a
    sgb_ref[...] = jax.nn.sigmoid(pg[:, d:2 * d])

    pb = jnp.dot(h, wb_ref[...], preferred_element_type=F32)
    sbuf[HIST:HIST + tm, :] = pb
    spbuf[...] = sbuf[HIST - 1:HIST - 1 + tm, :]

    @pl.when(i >= n_prompt_tiles)
    def _():
        for j in range(seqs):
            r0 = j * seq_len
            spbuf[r0:r0 + 1, :] = sts_ref[0, j:j + 1, :]

    for j in range(seqs):
        r1 = HIST + (j + 1) * seq_len
        stail_ref[0, j:j + 1, :] = sbuf[r1 - 1:r1, :]
    sbuf[HIST - 1:HIST, :] = sbuf[HIST + tm - 1:HIST + tm, :]

    s = pb + (spbuf[...] - pb) * mu_ref[...]
    o1, o2, o3 = RWKV_DIM, 2 * RWKV_DIM, 3 * RWKV_DIM
    r = s[:, 0:o1]
    k = s[:, o1:o2]
    v = s[:, o2:o3]
    s_l = s[:, o3:o3 + LORA_W + LORA_A]
    lg = s[:, o3 + LORA_W + LORA_A:]
    z = w0_ref[...] + _dot_stack3(jnp.tanh(s_l), lw_ref[...])
    w_log = -jax.nn.softplus(-z) - 0.5
    log_decay = -jnp.exp(w_log)
    a = jax.nn.sigmoid(a0_ref[...] + _dot_stack3(s_l, la_ref[...]))
    g = _dot_stack3(jax.nn.sigmoid(lg), lgw_ref[...])

    ones = ones_ref[...]
    kk = k * kk_ref[...]
    kk_n = kk / jnp.maximum(jnp.sqrt(_split2_dot(kk * kk, ones)), 1e-12)
    k2 = k * (1.0 + (a - 1.0) * ka_ref[...])
    bonus = _split2_dot(r * k2 * rk_ref[...], ones) * v

    r_ref[...] = r
    w_ref[...] = log_decay
    k_ref[...] = k2
    v_ref[...] = v
    a_ref[...] = -kk_n
    b_ref[...] = kk_n * a
    bonus_ref[...] = bonus
    g_ref[...] = g


def _mixer_pre(x_p, x_s, st_conv_t, st_shift_t, p, *, seq_len, tm):
    n_p, d = x_p.shape
    n_tok = n_p + x_s.shape[0]
    n_prompt_tiles = n_p // tm
    n_tiles = n_tok // tm
    seqs = tm // seq_len
    row = lambda w: pl.BlockSpec((tm, w), lambda i: (i, 0))
    st_idx = lambda i: (jnp.maximum(i - n_prompt_tiles, 0), 0, 0)
    consts = [p['norm1_g'], p['w_in_a'], p['w_in_b'], p['w_in_g'], p['conv_w'], p['mu_shift'],
              p['w0'], p['w_lora_w'], p['a0'], p['w_lora_a'], p['w_lora_g'], p['k_k'], p['k_a'],
              p['r_k'], p['w_out_a'], p['ones_bf16']]
    in_specs = [pl.BlockSpec((tm, d), lambda i: (jnp.minimum(i, n_prompt_tiles - 1), 0)),
                pl.BlockSpec((tm, d), lambda i: (jnp.maximum(i - n_prompt_tiles, 0), 0)),
                pl.BlockSpec((1, 2 * seqs, CONV_DIM), st_idx),
                pl.BlockSpec((1, seqs, SHIFT_DIM), st_idx)] + [_const_spec(c.shape) for c in consts]
    sds = lambda w: jax.ShapeDtypeStruct((n_tok, w), F32)
    out_shape = [sds(d), sds(d)] + [sds(RWKV_DIM)] * 8 + [
        jax.ShapeDtypeStruct((n_tiles, 2 * seqs, CONV_DIM), F32),
        jax.ShapeDtypeStruct((n_tiles, seqs, SHIFT_DIM), F32)]
    out_specs = [row(d), row(d)] + [row(RWKV_DIM)] * 8 + [
        pl.BlockSpec((1, 2 * seqs, CONV_DIM), lambda i: (i, 0, 0)),
        pl.BlockSpec((1, seqs, SHIFT_DIM), lambda i: (i, 0, 0))]
    kern = functools.partial(_mixer_pre_kernel, n_prompt_tiles=n_prompt_tiles, seq_len=seq_len)
    return pl.pallas_call(
        kern, out_shape=out_shape, grid=(n_tiles,), in_specs=in_specs, out_specs=out_specs,
        scratch_shapes=[pltpu.VMEM((tm + HIST, CONV_DIM), F32), pltpu.VMEM((tm + HIST, SHIFT_DIM), F32),
                        pltpu.VMEM((tm, CONV_DIM), F32), pltpu.VMEM((tm, CONV_DIM), F32),
                        pltpu.VMEM((tm, SHIFT_DIM), F32)],
        compiler_params=pltpu.CompilerParams(dimension_semantics=("arbitrary",),
                                             vmem_limit_bytes=VMEM_LIMIT),
        name="mixer_pre")(x_p, x_s, st_conv_t, st_shift_t, *consts)


GROUP_LANES = 256
HEADS_PER_GROUP = GROUP_LANES // RWKV_HEAD
N_HEAD_GROUPS = RWKV_DIM // GROUP_LANES
NN = (((1,), (0,)), ((), ()))
NT = (((1,), (1,)), ((), ()))


def _split2(x):
    hi = x.astype(BF16)
    lo = (x - hi.astype(F32)).astype(BF16)
    return hi, lo


def _mm(xs, ys, dims=NN):
    x1, x2 = xs
    y1, y2 = ys
    d = lambda p, q: lax.dot_general(p, q, dims, preferred_element_type=F32)
    m = x1.shape[0]
    both = d(jnp.concatenate([x1, x2], axis=0), y1)
    return both[0:m] + both[m:2 * m] + d(x1, y2)


def _cat2(ps, qs, axis):
    return tuple(jnp.concatenate([p, q], axis=axis) for p, q in zip(ps, qs))


def _wkv_masks(c):
    hc = HEADS_PER_GROUP * c
    levels = c.bit_length() - 1
    t = np.arange(c)[:, None]
    s = (np.arange(hc) % c)[None, :]
    tm = [s < t, s <= t]
    for lvl in range(1, levels + 1):
        half = 1 << (lvl - 1)
        tm.append(((t >> lvl) == (s >> lvl)) & ((t & half) != 0) & ((s & half) == 0))
    row_head = (np.arange(hc) // c)[:, None]
    mfeat = row_head == (np.arange(GROUP_LANES) // RWKV_HEAD)[None, :]
    mpos = row_head == (np.arange(hc) // c)[None, :]
    lane_head = np.arange(GROUP_LANES) // RWKV_HEAD
    stmask = lane_head[:, None] == lane_head[None, :]
    tri = np.arange(c)[None, :] <= np.arange(c)[:, None]
    return (jnp.asarray(mfeat, BF16), jnp.asarray(mpos, BF16), jnp.asarray(np.stack(tm), F32),
            jnp.asarray(stmask, F32), jnp.asarray(tri, BF16))


def _interleave(*gens):
    live = list(gens)
    while live:
        for gen in list(live):
            try:
                next(gen)
            except StopIteration:
                live.remove(gen)


def _wkv_pipe_kernel(r_ref, lw_ref, k_ref, v_ref, a_ref, b_ref, s0_ref, mfeat_ref, mpos_ref,
                     tmask_ref, stmask_ref, tri_ref, y_ref, sout_ref,
                     state, sv_ar, sv_inv, sv_akv, sv_arbk, sv_vbd, sv_v, sv_bk, sv_pend,
                     *, chunk, n_steps, chained):
    i = pl.program_id(0)
    c = chunk
    n_chunks = r_ref.shape[0] // c
    hc = HEADS_PER_GROUP * c
    levels = c.bit_length() - 1
    groups = range(N_HEAD_GROUPS)
    units = [(j, g) for j in range(n_chunks) for g in groups]
    uid = {u: n for n, u in enumerate(units)}

    @pl.when(i == 0)
    def _():
        for ref in (sv_ar, sv_inv, sv_akv, sv_arbk, sv_vbd, sv_v, sv_bk):
            ref[...] = jnp.zeros(ref.shape, ref.dtype)
        sv_pend[...] = jnp.ones(sv_pend.shape, F32)
        if chained:
            state[...] = s0_ref[0]

    mfeat = mfeat_ref[...]
    mpos = mpos_ref[...]
    strict = tmask_ref[0]
    incl = tmask_ref[1]
    eye = incl - strict
    tri = tri_ref[...]
    stmask = stmask_ref[...]

    def ld(ref, j, g):
        return ref[j * c:(j + 1) * c, g * GROUP_LANES:(g + 1) * GROUP_LANES]

    def bd_split(ps):
        mask = mpos if ps[0].shape[1] == hc else mfeat
        return tuple(jnp.concatenate([p] * HEADS_PER_GROUP, axis=0) * mask for p in ps)

    def bd2(m):
        return bd_split(_split2(m))

    def cumsum_rows(x):
        p1 = x.astype(BF16)
        r1 = x - p1.astype(F32)
        p2 = r1.astype(BF16)
        p3 = (r1 - p2.astype(F32)).astype(BF16)
        d = lambda q: jnp.dot(tri, q, preferred_element_type=F32)
        return d(p1) + d(p2) + d(p3)

    new = {}

    def prepare():
        cum = {u: cumsum_rows(ld(lw_ref, *u)) for u in units}
        yield
        ar, bk_end, p_end, a_ab, a_ak, a_rb, a_rk, v, vbd = ({} for _ in range(9))
        for u in units:
            cm = cum[u]
            cum_last = cm[c - 1:c, :]
            e_neg = jnp.exp(-cm)
            e_end = jnp.exp(cum_last - cm)
            b_raw = ld(b_ref, *u)
            k_raw = ld(k_ref, *u)
            ar[u] = _split2(jnp.concatenate([ld(a_ref, *u) * jnp.exp(cm - ld(lw_ref, *u)),
                                             ld(r_ref, *u) * jnp.exp(cm)], axis=0))
            bk_end[u] = _split2(jnp.concatenate([b_raw * e_end, k_raw * e_end], axis=0))
            p_end[u] = jnp.exp(cum_last)
            v[u] = ld(v_ref, *u)
            vbd[u] = bd2(v[u])
            gram = _mm(ar[u], _cat2(bd2(b_raw * e_neg), bd2(k_raw * e_neg), 0), NT)
            a_ab[u] = jnp.where(strict > 0, gram[0:c, 0:hc], 0.0)
            a_ak[u] = jnp.where(strict > 0, gram[0:c, hc:2 * hc], 0.0)
            a_rb[u] = jnp.where(incl > 0, gram[c:2 * c, 0:hc], 0.0)
            a_rk[u] = jnp.where(incl > 0, gram[c:2 * c, hc:2 * hc], 0.0)
            if uid[u] % 2 == 1:
                yield
        a_ab2 = {u: _split2(a_ab[u]) for u in units}
        inv = {u: eye + a_ab[u] * tmask_ref[2] for u in units}
        inv2 = {u: _split2(inv[u]) for u in units}
        for lvl in range(2, levels + 1):
            lm = tmask_ref[1 + lvl]
            t1 = {u: _mm(a_ab2[u], bd_split(inv2[u])) for u in units}
            yield
            inv = {u: inv[u] + lm * _mm(inv2[u], bd2(t1[u])) for u in units}
            inv2 = {u: _split2(inv[u]) for u in units}
            yield
        akv = {u: _mm(_split2(a_ak[u]), vbd[u]) for u in units}
        arbk = {u: _split2(jnp.concatenate([a_rb[u], a_rk[u]], axis=1)) for u in units}
        new.update(ar=ar, inv=inv2, akv=akv, arbk=arbk, vbd=vbd, v=v, bk=bk_end, pend=p_end)

    def serial():
        pair = lambda ref, n: (ref[0, n], ref[1, n])
        s_cur = [state[g] for g in groups] if chained else None
        for j in range(n_chunks):
            ns = [uid[j, g] for g in groups]
            s_prev = s_cur if chained else [s0_ref[j, g] for g in groups]
            x0 = [_mm(pair(sv_ar, n), _split2(s_prev[g]), NT) for g, n in zip(groups, ns)]
            yield
            uu = [_mm(pair(sv_inv, n), bd2(x0[g][0:c] + sv_akv[n])) for g, n in zip(groups, ns)]
            yield
            for g, n in zip(groups, ns):
                yy = x0[g][c:2 * c] + _mm(pair(sv_arbk, n), _cat2(bd2(uu[g]), pair(sv_vbd, n), 0))
                y_ref[j * c:(j + 1) * c, g * GROUP_LANES:(g + 1) * GROUP_LANES] = yy
            s_new = []
            for g, n in zip(groups, ns):
                uv_t = jnp.transpose(jnp.concatenate([uu[g], sv_v[n]], axis=0))
                upd = _mm(_split2(uv_t), pair(sv_bk, n))
                s_new.append(s_prev[g] * sv_pend[n, 0:1, :] + stmask * upd)
            yield
            if chained:
                s_cur = s_new
            else:
                for g in groups:
                    sout_ref[j, g] = s_new[g]
        if chained:
            for g in groups:
                state[g] = s_cur[g]
        new['last_state'] = s_new

    _interleave(serial(), prepare())

    bits = sum(pltpu.bitcast(s[0:8, 0:LANES], jnp.uint32) for s in new['last_state'])
    zero = pltpu.bitcast(lax.shift_right_logical(bits, jnp.uint32(32)), F32)[0:1, 0:1]
    zero_bf = zero.astype(BF16)
    for u, n in uid.items():
        for name, ref in (('ar', sv_ar), ('inv', sv_inv), ('arbk', sv_arbk), ('vbd', sv_vbd), ('bk', sv_bk)):
            for half in range(2):
                ref[half, n] = new[name][u][half] + zero_bf
        sv_akv[n] = new['akv'][u] + zero
        sv_v[n] = new['v'][u] + zero
        sv_pend[n] = jnp.broadcast_to(new['pend'][u] + zero, sv_pend.shape[1:])

    if chained:
        @pl.when(i == n_steps)
        def _():
            sout_ref[0] = state[...]


def _wkv_scan(rlkvab, s0, *, row0, n_rows, chunk, chunks_per_step, chained):
    rows = chunk * chunks_per_step
    n_steps = n_rows // rows
    assert n_rows % rows == 0 and row0 % rows == 0
    masks = _wkv_masks(chunk)
    hc = HEADS_PER_GROUP * chunk
    n_units = chunks_per_step * N_HEAD_GROUPS
    st = (N_HEAD_GROUPS, GROUP_LANES, GROUP_LANES)
    prev = lambda i: jnp.maximum(i - 1, 0)
    row_in = pl.BlockSpec((rows, RWKV_DIM), lambda i: (row0 // rows + jnp.minimum(i, n_steps - 1), 0))
    if chained:
        n_state = 1
        st_spec = pl.BlockSpec((1,) + st, lambda i: (0, 0, 0, 0))
    else:
        n_state = n_rows // chunk
        st_spec = pl.BlockSpec((chunks_per_step,) + st, lambda i: (prev(i), 0, 0, 0))
    in_specs = [row_in] * 6 + [st_spec] + [_const_spec(m.shape) for m in masks]
    out_shape = [jax.ShapeDtypeStruct((n_rows, RWKV_DIM), F32),
                 jax.ShapeDtypeStruct((n_state,) + st, F32)]
    out_specs = [pl.BlockSpec((rows, RWKV_DIM), lambda i: (prev(i), 0)), st_spec]
    scratch = [pltpu.VMEM(st, F32),
               pltpu.VMEM((2, n_units, 2 * chunk, GROUP_LANES), BF16),
               pltpu.VMEM((2, n_units, chunk, hc), BF16),
               pltpu.VMEM((n_units, chunk, GROUP_LANES), F32),
               pltpu.VMEM((2, n_units, chunk, 2 * hc), BF16),
               pltpu.VMEM((2, n_units, hc, GROUP_LANES), BF16),
               pltpu.VMEM((n_units, chunk, GROUP_LANES), F32),
               pltpu.VMEM((2, n_units, 2 * chunk, GROUP_LANES), BF16),
               pltpu.VMEM((n_units, 8, GROUP_LANES), F32)]
    kern = functools.partial(_wkv_pipe_kernel, chunk=chunk, n_steps=n_steps, chained=chained)
    return pl.pallas_call(
        kern, out_shape=out_shape, grid=(n_steps + 1,), in_specs=in_specs, out_specs=out_specs,
        scratch_shapes=scratch,
        compiler_params=pltpu.CompilerParams(dimension_semantics=("arbitrary",),
                                             vmem_limit_bytes=VMEM_LIMIT),
        name="wkv_scan")(*rlkvab, s0, *masks)


def _mixer_post_kernel(yp_ref, ys_ref, bonus_ref, g_ref, yag_ref, sgb_ref, xp_ref, xs_ref, lng_ref, lnb_ref,
                       wob_ref, wo_ref, n2_ref, wr_ref, br_ref, ones_ref, tril_ref,
                       x1_ref, h2_ref, route_ref, counts_ref, cnt, *, n_prompt_tiles):
    i = pl.program_id(0)

    @pl.when(i == 0)
    def _():
        cnt[...] = jnp.zeros(cnt.shape, F32)

    x = jnp.where(i < n_prompt_tiles, xp_ref[...], xs_ref[...])
    ones = ones_ref[...]
    y = jnp.where(i < n_prompt_tiles, yp_ref[...], ys_ref[...])
    inv_n = 1.0 / RWKV_HEAD
    mean = _split2_dot(y, ones) * inv_n
    yc = y - mean
    var = _split2_dot(yc * yc, ones) * inv_n
    yn = yc * lax.rsqrt(var + GN_EPS) * lng_ref[...] + lnb_ref[...]
    yy = (yn + bonus_ref[...]) * g_ref[...]
    y_b = jnp.dot(yy.astype(BF16), wob_ref[...], preferred_element_type=F32)
    merged = yag_ref[...] + sgb_ref[...] * y_b
    x1 = x + jnp.dot(merged.astype(BF16), wo_ref[...], preferred_element_type=F32)
    x1_ref[...] = x1
    h2 = _rms(x1, n2_ref[...])
    h2_ref[...] = h2

    logits = _dot_stack3(h2, wr_ref[...]) + br_ref[...]
    lane = lax.broadcasted_iota(jnp.int32, logits.shape, 1)
    neg = jnp.float32(-jnp.inf)
    big = jnp.int32(LANES)
    is_g = lane < N_GROUPS
    lgp = jnp.where(is_g, logits, neg)
    m_g = jnp.max(lgp, axis=-1, keepdims=True)
    grp = jnp.min(jnp.where(lgp == m_g, lane, big), axis=-1, keepdims=True)
    p_top = 1.0 / jnp.sum(jnp.where(is_g, jnp.exp(logits - m_g), 0.0), axis=-1, keepdims=True)
    e_lane = lane - N_GROUPS
    in_grp = (e_lane >= grp * EXPERTS_PER_GROUP) & (e_lane < (grp + 1) * EXPERTS_PER_GROUP)
    le = jnp.where(in_grp, logits, neg)
    m1 = jnp.max(le, axis=-1, keepdims=True)
    i1 = jnp.min(jnp.where(le == m1, lane, big), axis=-1, keepdims=True)
    le2 = jnp.where(lane == i1, neg, le)
    m2 = jnp.max(le2, axis=-1, keepdims=True)
    i2 = jnp.min(jnp.where(le2 == m2, lane, big), axis=-1, keepdims=True)
    ex = jnp.exp(m2 - m1)
    p1 = 1.0 / (1.0 + ex)
    p2 = ex / (1.0 + ex)
    oh1 = lane == i1 - N_GROUPS
    oh2 = lane == i2 - N_GROUPS
    both = jnp.where(oh1, 1.0, jnp.where(oh2, 1.0, 0.0))
    before = jnp.dot(tril_ref[...], both.astype(BF16), preferred_element_type=F32) + cnt[0:1, :]
    rank1 = jnp.sum(jnp.where(oh1, before, 0.0), axis=-1, keepdims=True)
    rank2 = jnp.sum(jnp.where(oh2, before, 0.0), axis=-1, keepdims=True)
    cnt[0:1, :] = cnt[0:1, :] + jnp.sum(both, axis=0, keepdims=True)
    counts_ref[...] = jnp.broadcast_to(cnt[0:1, :], counts_ref.shape)

    cols = [(i1 - N_GROUPS).astype(F32), (i2 - N_GROUPS).astype(F32), p_top * p1, p_top * p2,
            rank1, rank2]
    route = jnp.zeros(logits.shape, F32)
    for c, col in enumerate(cols):
        route = jnp.where(lane == c, col, route)
    route_ref[...] = route


ROUTE_EXPERT, ROUTE_WEIGHT, ROUTE_RANK = 0, 2, 4


def _mixer_post(y_p, y_s, bonus, g, yag, sgb, x_p, x_s, p, *, tm):
    n_p, d = x_p.shape
    n_tok = n_p + x_s.shape[0]
    npt = n_p // tm
    row = lambda w: pl.BlockSpec((tm, w), lambda i: (i, 0))
    tril = jnp.asarray(np.arange(tm)[None, :] < np.arange(tm)[:, None], BF16)
    consts = [p['lnx_g'], p['lnx_b'], p['w_out_b'], p['w_o'], p['norm2_g'], p['w_router'],
              p['b_router'], p['ones_bf16'], tril]
    pair = lambda w: [pl.BlockSpec((tm, w), lambda i: (jnp.minimum(i, npt - 1), 0)),
                      pl.BlockSpec((tm, w), lambda i: (jnp.maximum(i - npt, 0), 0))]
    in_specs = (pair(RWKV_DIM) + [row(RWKV_DIM)] * 2 + [row(d)] * 2 + pair(d)
                + [_const_spec(c.shape) for c in consts])
    out_shape = [jax.ShapeDtypeStruct((n_tok, d), F32), jax.ShapeDtypeStruct((n_tok, d), F32),
                 jax.ShapeDtypeStruct((n_tok, LANES), F32), jax.ShapeDtypeStruct((8, LANES), F32)]
    out_specs = [row(d), row(d), row(LANES), _const_spec((8, LANES))]
    return pl.pallas_call(
        functools.partial(_mixer_post_kernel, n_prompt_tiles=npt), out_shape=out_shape,
        grid=(n_tok // tm,), in_specs=in_specs, out_specs=out_specs,
        scratch_shapes=[pltpu.VMEM((8, LANES), F32)],
        compiler_params=pltpu.CompilerParams(dimension_semantics=("arbitrary",),
                                             vmem_limit_bytes=VMEM_LIMIT),
        name="mixer_post")(y_p, y_s, bonus, g, yag, sgb, x_p, x_s, *consts)


def _gather_rows(idx_ref, src_hbm, dst, sem, n_rows, *, unrolled):
    def start(r):
        pltpu.make_async_copy(src_hbm.at[pl.ds(idx_ref[r], 1)], dst.at[pl.ds(r, 1)], sem).start()
    if unrolled:
        for r in range(n_rows):
            start(r)
    else:
        def body(r, carry):
            start(r)
            return carry
        lax.fori_loop(0, n_rows, body, 0)


def _wait_rows(src_hbm, dst, sem, n_rows):
    pltpu.make_async_copy(src_hbm.at[pl.ds(0, n_rows)], dst, sem).wait()


def _moe_scatter_kernel(dest_ref, h_ref, xs_in_hbm, xs_hbm, sem):
    del xs_in_hbm
    n = dest_ref.shape[-1]
    tm = h_ref.shape[0]
    for r in range(n):
        pltpu.make_async_copy(h_ref.at[pl.ds(r % tm, 1)],
                              xs_hbm.at[pl.ds(dest_ref[0, 0, r], 1)], sem).start()
    for j in range(n // tm):
        pltpu.make_async_copy(h_ref, xs_hbm.at[pl.ds(0, tm)], sem).wait()


def _moe_scatter(h2, pos3, n_rows):
    n_tok, d = h2.shape
    n_tiles, _, n = pos3.shape
    tm = n // TOP_K
    xs0 = jnp.zeros((n_rows, d), h2.dtype)
    return pl.pallas_call(
        _moe_scatter_kernel, out_shape=jax.ShapeDtypeStruct((n_rows, d), h2.dtype), grid=(n_tiles,),
        in_specs=[pl.BlockSpec((1, 1, n), lambda i: (i, 0, 0), memory_space=pltpu.SMEM),
                  pl.BlockSpec((tm, d), lambda i: (i, 0)), pl.BlockSpec(memory_space=pl.ANY)],
        out_specs=pl.BlockSpec(memory_space=pl.ANY),
        scratch_shapes=[pltpu.SemaphoreType.DMA(())], input_output_aliases={2: 0},
        compiler_params=pltpu.CompilerParams(dimension_semantics=("arbitrary",), has_side_effects=True,
                                             vmem_limit_bytes=VMEM_LIMIT),
        name="moe_scatter")(pos3, h2, xs0)


def _moe_experts_kernel(bexp_ref, nused_ref, x_ref, wg_ref, wu_ref, wd_ref, yb_ref, wg_bf, wu_bf, wd_bf):
    i = pl.program_id(0)

    @pl.when((i == 0) | (bexp_ref[i] != bexp_ref[jnp.maximum(i - 1, 0)]))
    def _():
        wg_bf[...] = wg_ref[0].astype(BF16)
        wu_bf[...] = wu_ref[0].astype(BF16)
        wd_bf[...] = wd_ref[0].astype(BF16)

    @pl.when(i < nused_ref[0])
    def _():
        xb = x_ref[...].astype(BF16)
        hg = jnp.dot(xb, wg_bf[...], preferred_element_type=F32)
        hu = jnp.dot(xb, wu_bf[...], preferred_element_type=F32)
        hid = (hg * jax.nn.sigmoid(hg)) * hu
        yb_ref[...] = jnp.dot(hid.astype(BF16), wd_bf[...], preferred_element_type=F32)

    @pl.when(i >= nused_ref[0])
    def _():
        yb_ref[...] = jnp.zeros(yb_ref.shape, F32)


def _moe_experts(xs, block_expert, n_used, w_eg, w_eu, w_ed, *, blk):
    n_rows, d = xs.shape
    n_blocks = n_rows // blk
    de = w_eg.shape[2]
    grid_spec = pltpu.PrefetchScalarGridSpec(
        num_scalar_prefetch=2, grid=(n_blocks,),
        in_specs=[
            pl.BlockSpec((blk, d), lambda i, be, nu: (i, 0)),
            pl.BlockSpec((1, d, de), lambda i, be, nu: (be[i], 0, 0)),
            pl.BlockSpec((1, d, de), lambda i, be, nu: (be[i], 0, 0)),
            pl.BlockSpec((1, de, d), lambda i, be, nu: (be[i], 0, 0)),
        ],
        out_specs=pl.BlockSpec((blk, d), lambda i, be, nu: (i, 0)),
        scratch_shapes=[pltpu.VMEM((d, de), BF16), pltpu.VMEM((d, de), BF16), pltpu.VMEM((de, d), BF16)])
    return pl.pallas_call(
        _moe_experts_kernel, out_shape=jax.ShapeDtypeStruct((n_rows, d), F32), grid_spec=grid_spec,
        compiler_params=pltpu.CompilerParams(dimension_semantics=("arbitrary",),
                                             vmem_limit_bytes=VMEM_LIMIT),
        name="moe_experts")(block_expert, n_used, xs, w_eg, w_eu, w_ed)


def _with_dummy_block(idx, n_blocks, blk):
    return jnp.concatenate([idx, jnp.zeros((blk,), idx.dtype)]).reshape(n_blocks + 1, 1, blk)


def _moe_combine_kernel(pos_cur_ref, pos_nxt_ref, x1_ref, route_ref, nf_ref, yb_hbm, outp_ref,
                        outs_ref, ybuf, sems, *, n_prompt_tiles):
    i = pl.program_id(0)
    rows = ybuf.shape[1]
    slot = i % 2

    @pl.when(i == 0)
    def _():
        _gather_rows(pos_cur_ref.at[0, 0], yb_hbm, ybuf.at[0], sems.at[0], rows, unrolled=False)

    _wait_rows(yb_hbm, ybuf.at[slot], sems.at[slot], rows)
    _gather_rows(pos_nxt_ref.at[0, 0], yb_hbm, ybuf.at[1 - slot], sems.at[1 - slot], rows, unrolled=True)
    tm = x1_ref.shape[0]
    route = route_ref[...]
    x2 = x1_ref[...]
    for j in range(TOP_K):
        x2 = x2 + ybuf[slot, j * tm:(j + 1) * tm, :] * route[:, ROUTE_WEIGHT + j:ROUTE_WEIGHT + j + 1]
    out = _rms(x2, nf_ref[...])

    @pl.when(i < n_prompt_tiles)
    def _():
        outp_ref[...] = out

    @pl.when(i >= n_prompt_tiles)
    def _():
        outs_ref[...] = out

    @pl.when(i == pl.num_programs(0) - 1)
    def _():
        _wait_rows(yb_hbm, ybuf.at[1 - slot], sems.at[1 - slot], rows)


def _moe_combine(pos_tiles, x1, route, normf_g, yb, *, n_prompt_rows, tm):
    n_tok, d = x1.shape
    n_tiles = n_tok // tm
    npt = n_prompt_rows // tm
    pos3 = _with_dummy_block(pos_tiles.reshape(-1), n_tiles, TOP_K * tm)
    in_specs = [
        pl.BlockSpec((1, 1, TOP_K * tm), lambda i: (i, 0, 0), memory_space=pltpu.SMEM),
        pl.BlockSpec((1, 1, TOP_K * tm), lambda i: (i + 1, 0, 0), memory_space=pltpu.SMEM),
        pl.BlockSpec((tm, d), lambda i: (i, 0)),
        pl.BlockSpec((tm, LANES), lambda i: (i, 0)),
        _const_spec(normf_g.shape),
        pl.BlockSpec(memory_space=pl.ANY),
    ]
    out_shape = [jax.ShapeDtypeStruct((n_prompt_rows, d), F32),
                 jax.ShapeDtypeStruct((n_tok - n_prompt_rows, d), F32)]
    out_specs = [pl.BlockSpec((tm, d), lambda i: (jnp.minimum(i, npt - 1), 0)),
                 pl.BlockSpec((tm, d), lambda i: (jnp.maximum(i - npt, 0), 0))]
    return pl.pallas_call(
        functools.partial(_moe_combine_kernel, n_prompt_tiles=npt), out_shape=out_shape,
        grid=(n_tiles,), in_specs=in_specs, out_specs=out_specs,
        scratch_shapes=[pltpu.VMEM((2, TOP_K * tm, d), F32), pltpu.SemaphoreType.DMA((2,))],
        compiler_params=pltpu.CompilerParams(dimension_semantics=("arbitrary",),
                                             vmem_limit_bytes=VMEM_LIMIT),
        name="moe_combine")(pos3, pos3, x1, route, normf_g, yb)


def _dispatch(route, counts, blk, tm):
    n_tok = route.shape[0]
    n_assign = n_tok * TOP_K
    expert = route[:, ROUTE_EXPERT:ROUTE_EXPERT + TOP_K].astype(jnp.int32)
    rank = route[:, ROUTE_RANK:ROUTE_RANK + TOP_K].astype(jnp.int32)
    counts = counts[0, :N_EXPERTS].astype(jnp.int32)
    padded = (counts + blk - 1) // blk * blk
    pad_end = jnp.cumsum(padded)
    pad_start = pad_end - padded
    is_e = expert[:, :, None] == jnp.arange(N_EXPERTS, dtype=jnp.int32)
    dest = jnp.sum(jnp.where(is_e, pad_start, 0), axis=-1) + rank
    n_blocks = -(-n_assign // blk) + N_EXPERTS
    block_start = jnp.arange(n_blocks, dtype=jnp.int32) * blk
    block_expert = jnp.minimum(jnp.sum(block_start[:, None] >= pad_end[None, :], axis=1),
                               N_EXPERTS - 1).astype(jnp.int32)
    n_used = (pad_end[-1] // blk).reshape(1)
    pos_tiles = dest.reshape(n_tok // tm, tm, TOP_K).transpose(0, 2, 1).reshape(n_tok // tm, 1, TOP_K * tm)
    return pos_tiles, block_expert, n_used, n_blocks


def _state_to_kernel(s):
    b = s.shape[0]
    s6 = s.reshape(b, N_HEAD_GROUPS, HEADS_PER_GROUP, RWKV_HEAD, 1, RWKV_HEAD)
    eye = jnp.eye(HEADS_PER_GROUP, dtype=s.dtype).reshape(1, 1, HEADS_PER_GROUP, 1, HEADS_PER_GROUP, 1)
    return (s6 * eye).reshape(b, N_HEAD_GROUPS, GROUP_LANES, GROUP_LANES)


def _state_from_kernel(s):
    b = s.shape[0]
    s6 = s.reshape(b, N_HEAD_GROUPS, HEADS_PER_GROUP, RWKV_HEAD, HEADS_PER_GROUP, RWKV_HEAD)
    diag = jnp.stack([s6[:, :, h, :, h, :] for h in range(HEADS_PER_GROUP)], axis=2)
    return diag.reshape(b, RWKV_HEADS, RWKV_HEAD, RWKV_HEAD)


def kernel(x_prompt, x_sample, state_conv, state_shift, state_wkv, norm1_g, w_in, conv_w, mu_shift, w0, w_lora_w, a0, w_lora_a, w_lora_g, k_k, k_a, r_k, lnx_g, lnx_b, w_out_a, w_out_b, w_o, norm2_g, w_router_group, b_router_group, w_router_expert, b_router_expert, w_e_gate, w_e_up, w_e_down, normf_g):
    depth = norm1_g.shape[0]
    bp, seq, d = x_prompt.shape
    db, dseq, _ = x_sample.shape
    assert depth == 1 and bp == 1, "single layer, single prompt stream"
    tm = ROW_TILE
    n_p, n_s = bp * seq, db * dseq
    assert n_p % tm == 0 and n_s % tm == 0 and tm % dseq == 0
    n_prompt_tiles, n_sample_tiles = n_p // tm, n_s // tm
    seqs = tm // dseq
    n_tok = n_p + n_s

    x_p = x_prompt.reshape(n_p, d)
    x_s = x_sample.reshape(n_s, d)

    l = 0
    c3 = 3 * CONV_DIM
    head_id = jnp.arange(RWKV_DIM, dtype=jnp.int32) // RWKV_HEAD
    ones_bf16 = (head_id[:, None] == head_id[None, :]).astype(BF16)
    zpad = jnp.zeros((LORA_W, RWKV_DIM), F32)
    n_r = N_GROUPS + N_EXPERTS
    p = {
        'norm1_g': norm1_g[l].reshape(1, d),
        'w_in_a': w_in[l][:, :c3].astype(BF16),
        'w_in_b': w_in[l][:, c3:c3 + SHIFT_DIM].astype(BF16),
        'w_in_g': w_in[l][:, c3 + SHIFT_DIM:].astype(BF16),
        'conv_w': conv_w[l],
        'mu_shift': mu_shift[l].reshape(1, SHIFT_DIM),
        'w0': w0[l].reshape(1, RWKV_DIM),
        'w_lora_w': _stack3(jnp.concatenate([w_lora_w[l], zpad], axis=0)),
        'a0': a0[l].reshape(1, RWKV_DIM),
        'w_lora_a': _stack3(jnp.concatenate([zpad, w_lora_a[l]], axis=0)),
        'w_lora_g': _stack3(w_lora_g[l]),
        'k_k': k_k[l].reshape(1, RWKV_DIM),
        'k_a': k_a[l].reshape(1, RWKV_DIM),
        'r_k': r_k[l].reshape(1, RWKV_DIM),
        'w_out_a': w_out_a[l].astype(BF16),
        'ones_bf16': ones_bf16,
        'lnx_g': lnx_g[l].reshape(1, RWKV_DIM),
        'lnx_b': lnx_b[l].reshape(1, RWKV_DIM),
        'w_out_b': w_out_b[l].astype(BF16),
        'w_o': w_o[l].astype(BF16),
        'norm2_g': norm2_g[l].reshape(1, d),
        'w_router': _stack3(jnp.pad(jnp.concatenate([w_router_group[l], w_router_expert[l]], axis=1),
                                    ((0, 0), (0, LANES - n_r)))),
        'b_router': jnp.pad(jnp.concatenate([b_router_group[l], b_router_expert[l]]),
                            (0, LANES - n_r)).reshape(1, LANES),
    }
    st_conv_t = state_conv[l].reshape(n_sample_tiles, seqs * (CONV_WIDTH - 1), CONV_DIM)
    st_shift_t = state_shift[l].reshape(n_sample_tiles, seqs, SHIFT_DIM)

    (yag, sgb, r, w, k, v, a, b, bonus, g, ctail, stail) = _mixer_pre(
        x_p, x_s, st_conv_t, st_shift_t, p, seq_len=dseq, tm=tm)

    s0_prompt = jnp.zeros((1, N_HEAD_GROUPS, GROUP_LANES, GROUP_LANES), F32)
    s0_sample = _state_to_kernel(state_wkv[l])
    rlkvab = (r, w, k, v, a, b)
    cp = min(SCAN_CHUNK, n_p)
    assert cp & (cp - 1) == 0 and dseq & (dseq - 1) == 0
    y_p, s_p = _wkv_scan(rlkvab, s0_prompt, row0=0, n_rows=n_p, chunk=cp,
                         chunks_per_step=SCAN_CHUNKS_PER_STEP, chained=True)
    y_s, s_s = _wkv_scan(rlkvab, s0_sample, row0=n_p, n_rows=n_s, chunk=dseq,
                         chunks_per_step=SCAN_CHUNKS_PER_STEP, chained=False)

    x1, h2, route, counts = _mixer_post(y_p, y_s, bonus, g, yag, sgb, x_p, x_s, p, tm=tm)

    blk = MOE_ROWS
    pos_tiles, block_expert, n_used, n_blocks = _dispatch(route, counts, blk, tm)
    xs = _moe_scatter(h2, pos_tiles, n_blocks * blk)
    yb = _moe_experts(xs, block_expert, n_used, w_e_gate[l], w_e_up[l], w_e_down[l], blk=blk)
    out_p, out_s = _moe_combine(pos_tiles, x1, route, normf_g.reshape(1, d), yb, n_prompt_rows=n_p, tm=tm)

    y_prompt = out_p.reshape(bp, seq, d)
    y_sample = out_s.reshape(db, dseq, d)
    conv_p = ctail[n_prompt_tiles - 1, 2 * (seqs - 1):2 * seqs].reshape(1, bp, CONV_WIDTH - 1, CONV_DIM)
    shift_p = stail[n_prompt_tiles - 1, seqs - 1].reshape(1, bp, 1, SHIFT_DIM)
    wkv_p = _state_from_kernel(s_p).reshape(1, bp, RWKV_HEADS, RWKV_HEAD, RWKV_HEAD)
    conv_s = ctail[n_prompt_tiles:].reshape(1, db, CONV_WIDTH - 1, CONV_DIM)
    shift_s = stail[n_prompt_tiles:].reshape(1, db, 1, SHIFT_DIM)
    wkv_s = _state_from_kernel(s_s).reshape(1, db, RWKV_HEADS, RWKV_HEAD, RWKV_HEAD)
    return (y_prompt, y_sample, conv_p, shift_p, wkv_p, conv_s, shift_s, wkv_s)
```

```python
import functools

import numpy as np
import jax
import jax.numpy as jnp
from jax import lax
from jax.experimental import pallas as pl
from jax.experimental.pallas import tpu as pltpu

F32 = jnp.float32
BF16 = jnp.bfloat16

CONV_DIM = 512
CONV_WIDTH = 3
RWKV_HEAD = 64
RWKV_HEADS = 8
RWKV_DIM = RWKV_HEADS * RWKV_HEAD
LORA_W = 64
LORA_A = 64
LORA_G = 128
SHIFT_DIM = 3 * RWKV_DIM + LORA_W + LORA_A + LORA_G
N_GROUPS = 4
EXPERTS_PER_GROUP = 8
N_EXPERTS = N_GROUPS * EXPERTS_PER_GROUP
TOP_K = 2
RMS_EPS = 1e-6
GN_EPS = 64e-5

LANES = 128
ROW_TILE = 256
SCAN_CHUNK = 64
SCAN_CHUNKS_PER_STEP = 4
MOE_ROWS = 256
HIST = 8
VMEM_LIMIT = 56 * 1024 * 1024


def _rms(x, g):
    return x * lax.rsqrt(jnp.mean(x * x, axis=-1, keepdims=True) + RMS_EPS) * g


def _split2_dot(x, ones_bf16):
    hi = x.astype(BF16)
    lo = (x - hi.astype(F32)).astype(BF16)
    return (jnp.dot(hi, ones_bf16, preferred_element_type=F32)
            + jnp.dot(lo, ones_bf16, preferred_element_type=F32))


def _stack3(w):
    hi = w.astype(BF16)
    lo = (w - hi.astype(F32)).astype(BF16)
    return jnp.concatenate([hi, hi, lo], axis=0)


def _dot_stack3(x, w_stack):
    hi = x.astype(BF16)
    lo = (x - hi.astype(F32)).astype(BF16)
    return jnp.dot(jnp.concatenate([hi, lo, hi], axis=1), w_stack, preferred_element_type=F32)


def _const_spec(shape):
    nd = len(shape)
    return pl.BlockSpec(shape, lambda *_: (0,) * nd)


def _mixer_pre_kernel(xp_ref, xs_ref, stc_ref, sts_ref, n1_ref, wa_ref, wb_ref, wg_ref, convw_ref, mu_ref,
                      w0_ref, lw_ref, a0_ref, la_ref, lgw_ref, kk_ref, ka_ref, rk_ref, woa_ref,
                      ones_ref,
                      yag_ref, sgb_ref, r_ref, w_ref, k_ref, v_ref, a_ref, b_ref, bonus_ref, g_ref,
                      ctail_ref, stail_ref,
                      cbuf, sbuf, p1buf, p2buf, spbuf, *, n_prompt_tiles, seq_len):
    i = pl.program_id(0)
    tm = xp_ref.shape[0]
    seqs = tm // seq_len

    @pl.when(i == 0)
    def _():
        cbuf[0:HIST, :] = jnp.zeros((HIST, CONV_DIM), F32)
        sbuf[0:HIST, :] = jnp.zeros((HIST, SHIFT_DIM), F32)

    x = jnp.where(i < n_prompt_tiles, xp_ref[...], xs_ref[...])
    h = _rms(x, n1_ref[...]).astype(BF16)

    pa = jnp.dot(h, wa_ref[...], preferred_element_type=F32)
    g_in = pa[:, 0:CONV_DIM]
    g_out = pa[:, CONV_DIM:2 * CONV_DIM]
    x_c = pa[:, 2 * CONV_DIM:3 * CONV_DIM]
    bx = g_in * x_c
    cbuf[HIST:HIST + tm, :] = bx
    p1buf[...] = cbuf[HIST - 1:HIST - 1 + tm, :]
    p2buf[...] = cbuf[HIST - 2:HIST - 2 + tm, :]

    @pl.when(i >= n_prompt_tiles)
    def _():
        for j in range(seqs):
            r0 = j * seq_len
            p1buf[r0:r0 + 1, :] = stc_ref[0, 2 * j + 1:2 * j + 2, :]
            p2buf[r0:r0 + 1, :] = stc_ref[0, 2 * j:2 * j + 1, :]
            p2buf[r0 + 1:r0 + 2, :] = stc_ref[0, 2 * j + 1:2 * j + 2, :]

    cw = convw_ref[...]
    conv = cw[0:1, :] * p2buf[...] + cw[1:2, :] * p1buf[...] + cw[2:3, :] * bx
    y_a = jnp.dot((g_out * conv).astype(BF16), woa_ref[...], preferred_element_type=F32)
    for j in range(seqs):
        r1 = HIST + (j + 1) * seq_len
        ctail_ref[0, 2 * j:2 * j + 2, :] = cbuf[r1 - 2:r1, :]
    cbuf[HIST - 2:HIST, :] = cbuf[HIST + tm - 2:HIST + tm, :]

    pg = jnp.dot(h, wg_ref[...], preferred_element_type=F32)
    d = pg.shape[1] // 2
    yag_ref[...] = jax.nn.sigmoid(pg[:, 0:d]) * y_a
    sgb_ref[...] = jax.nn.sigmoid(pg[:, d:2 * d])

    pb = jnp.dot(h, wb_ref[...], preferred_element_type=F32)
    sbuf[HIST:HIST + tm, :] = pb
    spbuf[...] = sbuf[HIST - 1:HIST - 1 + tm, :]

    @pl.when(i >= n_prompt_tiles)
    def _():
        for j in range(seqs):
            r0 = j * seq_len
            spbuf[r0:r0 + 1, :] = sts_ref[0, j:j + 1, :]

    for j in range(seqs):
        r1 = HIST + (j + 1) * seq_len
        stail_ref[0, j:j + 1, :] = sbuf[r1 - 1:r1, :]
    sbuf[HIST - 1:HIST, :] = sbuf[HIST + tm - 1:HIST + tm, :]

    s = pb + (spbuf[...] - pb) * mu_ref[...]
    o1, o2, o3 = RWKV_DIM, 2 * RWKV_DIM, 3 * RWKV_DIM
    r = s[:, 0:o1]
    k = s[:, o1:o2]
    v = s[:, o2:o3]
    s_l = s[:, o3:o3 + LORA_W + LORA_A]
    lg = s[:, o3 + LORA_W + LORA_A:]
    z = w0_ref[...] + _dot_stack3(jnp.tanh(s_l), lw_ref[...])
    w_log = -jax.nn.softplus(-z) - 0.5
    log_decay = -jnp.exp(w_log)
    a = jax.nn.sigmoid(a0_ref[...] + _dot_stack3(s_l, la_ref[...]))
    g = _dot_stack3(jax.nn.sigmoid(lg), lgw_ref[...])

    ones = ones_ref[...]
    kk = k * kk_ref[...]
    kk_n = kk / jnp.maximum(jnp.sqrt(_split2_dot(kk * kk, ones)), 1e-12)
    k2 = k * (1.0 + (a - 1.0) * ka_ref[...])
    bonus = _split2_dot(r * k2 * rk_ref[...], ones) * v

    r_ref[...] = r
    w_ref[...] = log_decay
    k_ref[...] = k2
    v_ref[...] = v
    a_ref[...] = -kk_n
    b_ref[...] = kk_n * a
    bonus_ref[...] = bonus
    g_ref[...] = g


def _mixer_pre(x_p, x_s, st_conv_t, st_shift_t, p, *, seq_len, tm):
    n_p, d = x_p.shape
    n_tok = n_p + x_s.shape[0]
    n_prompt_tiles = n_p // tm
    n_tiles = n_tok // tm
    seqs = tm // seq_len
    row = lambda w: pl.BlockSpec((tm, w), lambda i: (i, 0))
    st_idx = lambda i: (jnp.maximum(i - n_prompt_tiles, 0), 0, 0)
    consts = [p['norm1_g'], p['w_in_a'], p['w_in_b'], p['w_in_g'], p['conv_w'], p['mu_shift'],
              p['w0'], p['w_lora_w'], p['a0'], p['w_lora_a'], p['w_lora_g'], p['k_k'], p['k_a'],
              p['r_k'], p['w_out_a'], p['ones_bf16']]
    in_specs = [pl.BlockSpec((tm, d), lambda i: (jnp.minimum(i, n_prompt_tiles - 1), 0)),
                pl.BlockSpec((tm, d), lambda i: (jnp.maximum(i - n_prompt_tiles, 0), 0)),
                pl.BlockSpec((1, 2 * seqs, CONV_DIM), st_idx),
                pl.BlockSpec((1, seqs, SHIFT_DIM), st_idx)] + [_const_spec(c.shape) for c in consts]
    sds = lambda w: jax.ShapeDtypeStruct((n_tok, w), F32)
    out_shape = [sds(d), sds(d)] + [sds(RWKV_DIM)] * 8 + [
        jax.ShapeDtypeStruct((n_tiles, 2 * seqs, CONV_DIM), F32),
        jax.ShapeDtypeStruct((n_tiles, seqs, SHIFT_DIM), F32)]
    out_specs = [row(d), row(d)] + [row(RWKV_DIM)] * 8 + [
        pl.BlockSpec((1, 2 * seqs, CONV_DIM), lambda i: (i, 0, 0)),
        pl.BlockSpec((1, seqs, SHIFT_DIM), lambda i: (i, 0, 0))]
    kern = functools.partial(_mixer_pre_kernel, n_prompt_tiles=n_prompt_tiles, seq_len=seq_len)
    return pl.pallas_call(
        kern, out_shape=out_shape, grid=(n_tiles,), in_specs=in_specs, out_specs=out_specs,
        scratch_shapes=[pltpu.VMEM((tm + HIST, CONV_DIM), F32), pltpu.VMEM((tm + HIST, SHIFT_DIM), F32),
                        pltpu.VMEM((tm, CONV_DIM), F32), pltpu.VMEM((tm, CONV_DIM), F32),
                        pltpu.VMEM((tm, SHIFT_DIM), F32)],
        compiler_params=pltpu.CompilerParams(dimension_semantics=("arbitrary",),
                                             vmem_limit_bytes=VMEM_LIMIT),
        name="mixer_pre")(x_p, x_s, st_conv_t, st_shift_t, *consts)


GROUP_LANES = 256
HEADS_PER_GROUP = GROUP_LANES // RWKV_HEAD
N_HEAD_GROUPS = RWKV_DIM // GROUP_LANES
NN = (((1,), (0,)), ((), ()))
NT = (((1,), (1,)), ((), ()))


def _split2(x):
    hi = x.astype(BF16)
    lo = (x - hi.astype(F32)).astype(BF16)
    return hi, lo


def _mm(xs, ys, dims=NN):
    x1, x2 = xs
    y1, y2 = ys
    d = lambda p, q: lax.dot_general(p, q, dims, preferred_element_type=F32)
    m = x1.shape[0]
    both = d(jnp.concatenate([x1, x2], axis=0), y1)
    return both[0:m] + both[m:2 * m] + d(x1, y2)


def _cat2(ps, qs, axis):
    return tuple(jnp.concatenate([p, q], axis=axis) for p, q in zip(ps, qs))


def _wkv_masks(c):
    hc = HEADS_PER_GROUP * c
    levels = c.bit_length() - 1
    t = np.arange(c)[:, None]
    s = (np.arange(hc) % c)[None, :]
    tm = [s < t, s <= t]
    for lvl in range(1, levels + 1):
        half = 1 << (lvl - 1)
        tm.append(((t >> lvl) == (s >> lvl)) & ((t & half) != 0) & ((s & half) == 0))
    row_head = (np.arange(hc) // c)[:, None]
    mfeat = row_head == (np.arange(GROUP_LANES) // RWKV_HEAD)[None, :]
    mpos = row_head == (np.arange(hc) // c)[None, :]
    lane_head = np.arange(GROUP_LANES) // RWKV_HEAD
    stmask = lane_head[:, None] == lane_head[None, :]
    tri = np.arange(c)[None, :] <= np.arange(c)[:, None]
    return (jnp.asarray(mfeat, BF16), jnp.asarray(mpos, BF16), jnp.asarray(np.stack(tm), F32),
            jnp.asarray(stmask, F32), jnp.asarray(tri, BF16))


def _interleave(*gens):
    live = list(gens)
    while live:
        for gen in list(live):
            try:
                next(gen)
            except StopIteration:
                live.remove(gen)


def _wkv_pipe_kernel(r_ref, lw_ref, k_ref, v_ref, a_ref, b_ref, s0_ref, mfeat_ref, mpos_ref,
                     tmask_ref, stmask_ref, tri_ref, y_ref, sout_ref,
                     state, sv_ar, sv_inv, sv_akv, sv_arbk, sv_vbd, sv_v, sv_bk, sv_pend,
                     *, chunk, n_steps, chained):
    i = pl.program_id(0)
    c = chunk
    n_chunks = r_ref.shape[0] // c
    hc = HEADS_PER_GROUP * c
    levels = c.bit_length() - 1
    groups = range(N_HEAD_GROUPS)
    units = [(j, g) for j in range(n_chunks) for g in groups]
    uid = {u: n for n, u in enumerate(units)}

    @pl.when(i == 0)
    def _():
        for ref in (sv_ar, sv_inv, sv_akv, sv_arbk, sv_vbd, sv_v, sv_bk):
            ref[...] = jnp.zeros(ref.shape, ref.dtype)
        sv_pend[...] = jnp.ones(sv_pend.shape, F32)
        if chained:
            state[...] = s0_ref[0]

    mfeat = mfeat_ref[...]
    mpos = mpos_ref[...]
    strict = tmask_ref[0]
    incl = tmask_ref[1]
    eye = incl - strict
    tri = tri_ref[...]
    stmask = stmask_ref[...]

    def ld(ref, j, g):
        return ref[j * c:(j + 1) * c, g * GROUP_LANES:(g + 1) * GROUP_LANES]

    def bd_split(ps):
        mask = mpos if ps[0].shape[1] == hc else mfeat
        return tuple(jnp.concatenate([p] * HEADS_PER_GROUP, axis=0) * mask for p in ps)

    def bd2(m):
        return bd_split(_split2(m))

    def cumsum_rows(x):
        p1 = x.astype(BF16)
        r1 = x - p1.astype(F32)
        p2 = r1.astype(BF16)
        p3 = (r1 - p2.astype(F32)).astype(BF16)
        d = lambda q: jnp.dot(tri, q, preferred_element_type=F32)
        return d(p1) + d(p2) + d(p3)

    new = {}

    def prepare():
        cum = {u: cumsum_rows(ld(lw_ref, *u)) for u in units}
        yield
        ar, bk_end, p_end, a_ab, a_ak, a_rb, a_rk, v, vbd = ({} for _ in range(9))
        for u in units:
            cm = cum[u]
            cum_last = cm[c - 1:c, :]
            e_neg = jnp.exp(-cm)
            e_end = jnp.exp(cum_last - cm)
            b_raw = ld(b_ref, *u)
            k_raw = ld(k_ref, *u)
            ar[u] = _split2(jnp.concatenate([ld(a_ref, *u) * jnp.exp(cm - ld(lw_ref, *u)),
                                             ld(r_ref, *u) * jnp.exp(cm)], axis=0))
            bk_end[u] = _split2(jnp.concatenate([b_raw * e_end, k_raw * e_end], axis=0))
            p_end[u] = jnp.exp(cum_last)
            v[u] = ld(v_ref, *u)
            vbd[u] = bd2(v[u])
            gram = _mm(ar[u], _cat2(bd2(b_raw * e_neg), bd2(k_raw * e_neg), 0), NT)
            a_ab[u] = jnp.where(strict > 0, gram[0:c, 0:hc], 0.0)
            a_ak[u] = jnp.where(strict > 0, gram[0:c, hc:2 * hc], 0.0)
            a_rb[u] = jnp.where(incl > 0, gram[c:2 * c, 0:hc], 0.0)
            a_rk[u] = jnp.where(incl > 0, gram[c:2 * c, hc:2 * hc], 0.0)
            if uid[u] % 2 == 1:
                yield
        a_ab2 = {u: _split2(a_ab[u]) for u in units}
        inv = {u: eye + a_ab[u] * tmask_ref[2] for u in units}
        inv2 = {u: _split2(inv[u]) for u in units}
        for lvl in range(2, levels + 1):
            lm = tmask_ref[1 + lvl]
            t1 = {u: _mm(a_ab2[u], bd_split(inv2[u])) for u in units}
            yield
            inv = {u: inv[u] + lm * _mm(inv2[u], bd2(t1[u])) for u in units}
            inv2 = {u: _split2(inv[u]) for u in units}
            yield
        akv = {u: _mm(_split2(a_ak[u]), vbd[u]) for u in units}
        arbk = {u: _split2(jnp.concatenate([a_rb[u], a_rk[u]], axis=1)) for u in units}
        new.update(ar=ar, inv=inv2, akv=akv, arbk=arbk, vbd=vbd, v=v, bk=bk_end, pend=p_end)

    def serial():
        pair = lambda ref, n: (ref[0, n], ref[1, n])
        s_cur = [state[g] for g in groups] if chained else None
        for j in range(n_chunks):
            ns = [uid[j, g] for g in groups]
            s_prev = s_cur if chained else [s0_ref[j, g] for g in groups]
            x0 = [_mm(pair(sv_ar, n), _split2(s_prev[g]), NT) for g, n in zip(groups, ns)]
            yield
            uu = [_mm(pair(sv_inv, n), bd2(x0[g][0:c] + sv_akv[n])) for g, n in zip(groups, ns)]
            yield
            for g, n in zip(groups, ns):
                yy = x0[g][c:2 * c] + _mm(pair(sv_arbk, n), _cat2(bd2(uu[g]), pair(sv_vbd, n), 0))
                y_ref[j * c:(j + 1) * c, g * GROUP_LANES:(g + 1) * GROUP_LANES] = yy
            s_new = []
            for g, n in zip(groups, ns):
                uv_t = jnp.transpose(jnp.concatenate([uu[g], sv_v[n]], axis=0))
                upd = _mm(_split2(uv_t), pair(sv_bk, n))
                s_new.append(s_prev[g] * sv_pend[n, 0:1, :] + stmask * upd)
            yield
            if chained:
                s_cur = s_new
            else:
                for g in groups:
                    sout_ref[j, g] = s_new[g]
        if chained:
            for g in groups:
                state[g] = s_cur[g]
        new['last_state'] = s_new

    _interleave(serial(), prepare())

    bits = sum(pltpu.bitcast(s[0:8, 0:LANES], jnp.uint32) for s in new['last_state'])
    zero = pltpu.bitcast(lax.shift_right_logical(bits, jnp.uint32(32)), F32)[0:1, 0:1]
    zero_bf = zero.astype(BF16)
    for u, n in uid.items():
        for name, ref in (('ar', sv_ar), ('inv', sv_inv), ('arbk', sv_arbk), ('vbd', sv_vbd), ('bk', sv_bk)):
            for half in range(2):
                ref[half, n] = new[name][u][half] + zero_bf
        sv_akv[n] = new['akv'][u] + zero
        sv_v[n] = new['v'][u] + zero
        sv_pend[n] = jnp.broadcast_to(new['pend'][u] + zero, sv_pend.shape[1:])

    if chained:
        @pl.when(i == n_steps)
        def _():
            sout_ref[0] = state[...]


def _wkv_scan(rlkvab, s0, *, row0, n_rows, chunk, chunks_per_step, chained):
    rows = chunk * chunks_per_step
    n_steps = n_rows // rows
    assert n_rows % rows == 0 and row0 % rows == 0
    masks = _wkv_masks(chunk)
    hc = HEADS_PER_GROUP * chunk
    n_units = chunks_per_step * N_HEAD_GROUPS
    st = (N_HEAD_GROUPS, GROUP_LANES, GROUP_LANES)
    prev = lambda i: jnp.maximum(i - 1, 0)
    row_in = pl.BlockSpec((rows, RWKV_DIM), lambda i: (row0 // rows + jnp.minimum(i, n_steps - 1), 0))
    if chained:
        n_state = 1
        st_spec = pl.BlockSpec((1,) + st, lambda i: (0, 0, 0, 0))
    else:
        n_state = n_rows // chunk
        st_spec = pl.BlockSpec((chunks_per_step,) + st, lambda i: (prev(i), 0, 0, 0))
    in_specs = [row_in] * 6 + [st_spec] + [_const_spec(m.shape) for m in masks]
    out_shape = [jax.ShapeDtypeStruct((n_rows, RWKV_DIM), F32),
                 jax.ShapeDtypeStruct((n_state,) + st, F32)]
    out_specs = [pl.BlockSpec((rows, RWKV_DIM), lambda i: (prev(i), 0)), st_spec]
    scratch = [pltpu.VMEM(st, F32),
               pltpu.VMEM((2, n_units, 2 * chunk, GROUP_LANES), BF16),
               pltpu.VMEM((2, n_units, chunk, hc), BF16),
               pltpu.VMEM((n_units, chunk, GROUP_LANES), F32),
               pltpu.VMEM((2, n_units, chunk, 2 * hc), BF16),
               pltpu.VMEM((2, n_units, hc, GROUP_LANES), BF16),
               pltpu.VMEM((n_units, chunk, GROUP_LANES), F32),
               pltpu.VMEM((2, n_units, 2 * chunk, GROUP_LANES), BF16),
               pltpu.VMEM((n_units, 8, GROUP_LANES), F32)]
    kern = functools.partial(_wkv_pipe_kernel, chunk=chunk, n_steps=n_steps, chained=chained)
    return pl.pallas_call(
        kern, out_shape=out_shape, grid=(n_steps + 1,), in_specs=in_specs, out_specs=out_specs,
        scratch_shapes=scratch,
        compiler_params=pltpu.CompilerParams(dimension_semantics=("arbitrary",),
                                             vmem_limit_bytes=VMEM_LIMIT),
        name="wkv_scan")(*rlkvab, s0, *masks)


def _mixer_post_kernel(yp_ref, ys_ref, bonus_ref, g_ref, yag_ref, sgb_ref, xp_ref, xs_ref, lng_ref, lnb_ref,
                       wob_ref, wo_ref, n2_ref, wr_ref, br_ref, ones_ref, tril_ref,
                       x1_ref, h2_ref, route_ref, counts_ref, cnt, *, n_prompt_tiles):
    i = pl.program_id(0)

    @pl.when(i == 0)
    def _():
        cnt[...] = jnp.zeros(cnt.shape, F32)

    x = jnp.where(i < n_prompt_tiles, xp_ref[...], xs_ref[...])
    ones = ones_ref[...]
    y = jnp.where(i < n_prompt_tiles, yp_ref[...], ys_ref[...])
    inv_n = 1.0 / RWKV_HEAD
    mean = _split2_dot(y, ones) * inv_n
    yc = y - mean
    var = _split2_dot(yc * yc, ones) * inv_n
    yn = yc * lax.rsqrt(var + GN_EPS) * lng_ref[...] + lnb_ref[...]
    yy = (yn + bonus_ref[...]) * g_ref[...]
    y_b = jnp.dot(yy.astype(BF16), wob_ref[...], preferred_element_type=F32)
    merged = yag_ref[...] + sgb_ref[...] * y_b
    x1 = x + jnp.dot(merged.astype(BF16), wo_ref[...], preferred_element_type=F32)
    x1_ref[...] = x1
    h2 = _rms(x1, n2_ref[...])
    h2_ref[...] = h2

    logits = _dot_stack3(h2, wr_ref[...]) + br_ref[...]
    lane = lax.broadcasted_iota(jnp.int32, logits.shape, 1)
    neg = jnp.float32(-jnp.inf)
    big = jnp.int32(LANES)
    is_g = lane < N_GROUPS
    lgp = jnp.where(is_g, logits, neg)
    m_g = jnp.max(lgp, axis=-1, keepdims=True)
    grp = jnp.min(jnp.where(lgp == m_g, lane, big), axis=-1, keepdims=True)
    p_top = 1.0 / jnp.sum(jnp.where(is_g, jnp.exp(logits - m_g), 0.0), axis=-1, keepdims=True)
    e_lane = lane - N_GROUPS
    in_grp = (e_lane >= grp * EXPERTS_PER_GROUP) & (e_lane < (grp + 1) * EXPERTS_PER_GROUP)
    le = jnp.where(in_grp, logits, neg)
    m1 = jnp.max(le, axis=-1, keepdims=True)
    i1 = jnp.min(jnp.where(le == m1, lane, big), axis=-1, keepdims=True)
    le2 = jnp.where(lane == i1, neg, le)
    m2 = jnp.max(le2, axis=-1, keepdims=True)
    i2 = jnp.min(jnp.where(le2 == m2, lane, big), axis=-1, keepdims=True)
    ex = jnp.exp(m2 - m1)
    p1 = 1.0 / (1.0 + ex)
    p2 = ex / (1.0 + ex)
    oh1 = lane == i1 - N_GROUPS
    oh2 = lane == i2 - N_GROUPS
    both = jnp.where(oh1, 1.0, jnp.where(oh2, 1.0, 0.0))
    before = jnp.dot(tril_ref[...], both.astype(BF16), preferred_element_type=F32) + cnt[0:1, :]
    rank1 = jnp.sum(jnp.where(oh1, before, 0.0), axis=-1, keepdims=True)
    rank2 = jnp.sum(jnp.where(oh2, before, 0.0), axis=-1, keepdims=True)
    cnt[0:1, :] = cnt[0:1, :] + jnp.sum(both, axis=0, keepdims=True)
    counts_ref[...] = jnp.broadcast_to(cnt[0:1, :], counts_ref.shape)

    cols = [(i1 - N_GROUPS).astype(F32), (i2 - N_GROUPS).astype(F32), p_top * p1, p_top * p2,
            rank1, rank2]
    route = jnp.zeros(logits.shape, F32)
    for c, col in enumerate(cols):
        route = jnp.where(lane == c, col, route)
    route_ref[...] = route


ROUTE_EXPERT, ROUTE_WEIGHT, ROUTE_RANK = 0, 2, 4


def _mixer_post(y_p, y_s, bonus, g, yag, sgb, x_p, x_s, p, *, tm):
    n_p, d = x_p.shape
    n_tok = n_p + x_s.shape[0]
    npt = n_p // tm
    row = lambda w: pl.BlockSpec((tm, w), lambda i: (i, 0))
    tril = jnp.asarray(np.arange(tm)[None, :] < np.arange(tm)[:, None], BF16)
    consts = [p['lnx_g'], p['lnx_b'], p['w_out_b'], p['w_o'], p['norm2_g'], p['w_router'],
              p['b_router'], p['ones_bf16'], tril]
    pair = lambda w: [pl.BlockSpec((tm, w), lambda i: (jnp.minimum(i, npt - 1), 0)),
                      pl.BlockSpec((tm, w), lambda i: (jnp.maximum(i - npt, 0), 0))]
    in_specs = (pair(RWKV_DIM) + [row(RWKV_DIM)] * 2 + [row(d)] * 2 + pair(d)
                + [_const_spec(c.shape) for c in consts])
    out_shape = [jax.ShapeDtypeStruct((n_tok, d), F32), jax.ShapeDtypeStruct((n_tok, d), F32),
                 jax.ShapeDtypeStruct((n_tok, LANES), F32), jax.ShapeDtypeStruct((8, LANES), F32)]
    out_specs = [row(d), row(d), row(LANES), _const_spec((8, LANES))]
    return pl.pallas_call(
        functools.partial(_mixer_post_kernel, n_prompt_tiles=npt), out_shape=out_shape,
        grid=(n_tok // tm,), in_specs=in_specs, out_specs=out_specs,
        scratch_shapes=[pltpu.VMEM((8, LANES), F32)],
        compiler_params=pltpu.CompilerParams(dimension_semantics=("arbitrary",),
                                             vmem_limit_bytes=VMEM_LIMIT),
        name="mixer_post")(y_p, y_s, bonus, g, yag, sgb, x_p, x_s, *consts)


def _gather_rows(idx_ref, src_hbm, dst, sem, n_rows, *, unrolled):
    def start(r):
        pltpu.make_async_copy(src_hbm.at[pl.ds(idx_ref[r], 1)], dst.at[pl.ds(r, 1)], sem).start()
    if unrolled:
        for r in range(n_rows):
            start(r)
    else:
        def body(r, carry):
            start(r)
            return carry
        lax.fori_loop(0, n_rows, body, 0)


def _wait_rows(src_hbm, dst, sem, n_rows):
    pltpu.make_async_copy(src_hbm.at[pl.ds(0, n_rows)], dst, sem).wait()


def _moe_scatter_kernel(dest_ref, h_ref, xs_in_hbm, xs_hbm, sem):
    del xs_in_hbm
    n = dest_ref.shape[-1]
    tm = h_ref.shape[0]
    for r in range(n):
        pltpu.make_async_copy(h_ref.at[pl.ds(r % tm, 1)],
                              xs_hbm.at[pl.ds(dest_ref[0, 0, r], 1)], sem).start()
    for j in range(n // tm):
        pltpu.make_async_copy(h_ref, xs_hbm.at[pl.ds(0, tm)], sem).wait()


def _moe_scatter(h2, pos3, n_rows):
    n_tok, d = h2.shape
    n_tiles, _, n = pos3.shape
    tm = n // TOP_K
    xs0 = jnp.zeros((n_rows, d), h2.dtype)
    return pl.pallas_call(
        _moe_scatter_kernel, out_shape=jax.ShapeDtypeStruct((n_rows, d), h2.dtype), grid=(n_tiles,),
        in_specs=[pl.BlockSpec((1, 1, n), lambda i: (i, 0, 0), memory_space=pltpu.SMEM),
                  pl.BlockSpec((tm, d), lambda i: (i, 0)), pl.BlockSpec(memory_space=pl.ANY)],
        out_specs=pl.BlockSpec(memory_space=pl.ANY),
        scratch_shapes=[pltpu.SemaphoreType.DMA(())], input_output_aliases={2: 0},
        compiler_params=pltpu.CompilerParams(dimension_semantics=("arbitrary",), has_side_effects=True,
                                             vmem_limit_bytes=VMEM_LIMIT),
        name="moe_scatter")(pos3, h2, xs0)


def _moe_experts_kernel(bexp_ref, nused_ref, x_ref, wg_ref, wu_ref, wd_ref, yb_ref, wg_bf, wu_bf, wd_bf):
    i = pl.program_id(0)

    @pl.when((i == 0) | (bexp_ref[i] != bexp_ref[jnp.maximum(i - 1, 0)]))
    def _():
        wg_bf[...] = wg_ref[0].astype(BF16)
        wu_bf[...] = wu_ref[0].astype(BF16)
        wd_bf[...] = wd_ref[0].astype(BF16)

    @pl.when(i < nused_ref[0])
    def _():
        xb = x_ref[...].astype(BF16)
        hg = jnp.dot(xb, wg_bf[...], preferred_element_type=F32)
        hu = jnp.dot(xb, wu_bf[...], preferred_element_type=F32)
        hid = (hg * jax.nn.sigmoid(hg)) * hu
        yb_ref[...] = jnp.dot(hid.astype(BF16), wd_bf[...], preferred_element_type=F32)

    @pl.when(i >= nused_ref[0])
    def _():
        yb_ref[...] = jnp.zeros(yb_ref.shape, F32)


def _moe_experts(xs, block_expert, n_used, w_eg, w_eu, w_ed, *, blk):
    n_rows, d = xs.shape
    n_blocks = n_rows // blk
    de = w_eg.shape[2]
    grid_spec = pltpu.PrefetchScalarGridSpec(
        num_scalar_prefetch=2, grid=(n_blocks,),
        in_specs=[
            pl.BlockSpec((blk, d), lambda i, be, nu: (jnp.minimum(i, nu[0] - 1), 0)),
            pl.BlockSpec((1, d, de), lambda i, be, nu: (be[i], 0, 0)),
            pl.BlockSpec((1, d, de), lambda i, be, nu: (be[i], 0, 0)),
            pl.BlockSpec((1, de, d), lambda i, be, nu: (be[i], 0, 0)),
        ],
        out_specs=pl.BlockSpec((blk, d), lambda i, be, nu: (i, 0)),
        scratch_shapes=[pltpu.VMEM((d, de), BF16), pltpu.VMEM((d, de), BF16), pltpu.VMEM((de, d), BF16)])
    return pl.pallas_call(
        _moe_experts_kernel, out_shape=jax.ShapeDtypeStruct((n_rows, d), F32), grid_spec=grid_spec,
        compiler_params=pltpu.CompilerParams(dimension_semantics=("arbitrary",),
                                             vmem_limit_bytes=VMEM_LIMIT),
        name="moe_experts")(block_expert, n_used, xs, w_eg, w_eu, w_ed)


GATHER_AHEAD = 2


def _with_dummy_blocks(idx, n_blocks, blk):
    pad = jnp.zeros((GATHER_AHEAD * blk,), idx.dtype)
    return jnp.concatenate([idx, pad]).reshape(n_blocks + GATHER_AHEAD, 1, blk)


def _moe_combine_kernel(*refs, n_prompt_tiles):
    pos_refs = refs[:GATHER_AHEAD + 1]
    x1_ref, route_ref, nf_ref, yb_hbm, outp_ref, outs_ref, ybuf, sems = refs[GATHER_AHEAD + 1:]
    i = pl.program_id(0)
    n_slots = GATHER_AHEAD + 1
    rows = ybuf.shape[1]
    slot = i % n_slots

    @pl.when(i == 0)
    def _():
        for a in range(GATHER_AHEAD):
            _gather_rows(pos_refs[a].at[0, 0], yb_hbm, ybuf.at[a], sems.at[a], rows, unrolled=False)

    _wait_rows(yb_hbm, ybuf.at[slot], sems.at[slot], rows)
    ahead = (i + GATHER_AHEAD) % n_slots
    _gather_rows(pos_refs[GATHER_AHEAD].at[0, 0], yb_hbm, ybuf.at[ahead], sems.at[ahead], rows, unrolled=True)
    tm = x1_ref.shape[0]
    route = route_ref[...]
    x2 = x1_ref[...]
    for j in range(TOP_K):
        x2 = x2 + ybuf[slot, j * tm:(j + 1) * tm, :] * route[:, ROUTE_WEIGHT + j:ROUTE_WEIGHT + j + 1]
    out = _rms(x2, nf_ref[...])

    @pl.when(i < n_prompt_tiles)
    def _():
        outp_ref[...] = out

    @pl.when(i >= n_prompt_tiles)
    def _():
        outs_ref[...] = out

    @pl.when(i == pl.num_programs(0) - 1)
    def _():
        for a in range(1, n_slots):
            s = (i + a) % n_slots
            _wait_rows(yb_hbm, ybuf.at[s], sems.at[s], rows)


def _moe_combine(pos_tiles, x1, route, normf_g, yb, *, n_prompt_rows, tm):
    n_tok, d = x1.shape
    n_tiles = n_tok // tm
    npt = n_prompt_rows // tm
    pos3 = _with_dummy_blocks(pos_tiles.reshape(-1), n_tiles, TOP_K * tm)
    pos_spec = lambda a: pl.BlockSpec((1, 1, TOP_K * tm), lambda i: (i + a, 0, 0), memory_space=pltpu.SMEM)
    in_specs = [pos_spec(a) for a in range(GATHER_AHEAD + 1)] + [
        pl.BlockSpec((tm, d), lambda i: (i, 0)),
        pl.BlockSpec((tm, LANES), lambda i: (i, 0)),
        _const_spec(normf_g.shape),
        pl.BlockSpec(memory_space=pl.ANY),
    ]
    out_shape = [jax.ShapeDtypeStruct((n_prompt_rows, d), F32),
                 jax.ShapeDtypeStruct((n_tok - n_prompt_rows, d), F32)]
    out_specs = [pl.BlockSpec((tm, d), lambda i: (jnp.minimum(i, npt - 1), 0)),
                 pl.BlockSpec((tm, d), lambda i: (jnp.maximum(i - npt, 0), 0))]
    return pl.pallas_call(
        functools.partial(_moe_combine_kernel, n_prompt_tiles=npt), out_shape=out_shape,
        grid=(n_tiles,), in_specs=in_specs, out_specs=out_specs,
        scratch_shapes=[pltpu.VMEM((GATHER_AHEAD + 1, TOP_K * tm, d), F32),
                        pltpu.SemaphoreType.DMA((GATHER_AHEAD + 1,))],
        compiler_params=pltpu.CompilerParams(dimension_semantics=("arbitrary",),
                                             vmem_limit_bytes=VMEM_LIMIT),
        name="moe_combine")(*([pos3] * (GATHER_AHEAD + 1)), x1, route, normf_g, yb)


def _dispatch(route, counts, blk, tm):
    n_tok = route.shape[0]
    n_assign = n_tok * TOP_K
    expert = route[:, ROUTE_EXPERT:ROUTE_EXPERT + TOP_K].astype(jnp.int32)
    rank = route[:, ROUTE_RANK:ROUTE_RANK + TOP_K].astype(jnp.int32)
    counts = counts[0, :N_EXPERTS].astype(jnp.int32)
    padded = (counts + blk - 1) // blk * blk
    pad_end = jnp.cumsum(padded)
    pad_start = pad_end - padded
    is_e = expert[:, :, None] == jnp.arange(N_EXPERTS, dtype=jnp.int32)
    dest = jnp.sum(jnp.where(is_e, pad_start, 0), axis=-1) + rank
    n_blocks = -(-n_assign // blk) + N_EXPERTS
    block_start = jnp.arange(n_blocks, dtype=jnp.int32) * blk
    block_expert = jnp.minimum(jnp.sum(block_start[:, None] >= pad_end[None, :], axis=1),
                               N_EXPERTS - 1).astype(jnp.int32)
    n_used = (pad_end[-1] // blk).reshape(1)
    pos_tiles = dest.reshape(n_tok // tm, tm, TOP_K).transpose(0, 2, 1).reshape(n_tok // tm, 1, TOP_K * tm)
    return pos_tiles, block_expert, n_used, n_blocks


def _state_to_kernel(s):
    b = s.shape[0]
    s6 = s.reshape(b, N_HEAD_GROUPS, HEADS_PER_GROUP, RWKV_HEAD, 1, RWKV_HEAD)
    eye = jnp.eye(HEADS_PER_GROUP, dtype=s.dtype).reshape(1, 1, HEADS_PER_GROUP, 1, HEADS_PER_GROUP, 1)
    return (s6 * eye).reshape(b, N_HEAD_GROUPS, GROUP_LANES, GROUP_LANES)


def _state_from_kernel(s):
    b = s.shape[0]
    s6 = s.reshape(b, N_HEAD_GROUPS, HEADS_PER_GROUP, RWKV_HEAD, HEADS_PER_GROUP, RWKV_HEAD)
    diag = jnp.stack([s6[:, :, h, :, h, :] for h in range(HEADS_PER_GROUP)], axis=2)
    return diag.reshape(b, RWKV_HEADS, RWKV_HEAD, RWKV_HEAD)


def kernel(x_prompt, x_sample, state_conv, state_shift, state_wkv, norm1_g, w_in, conv_w, mu_shift, w0, w_lora_w, a0, w_lora_a, w_lora_g, k_k, k_a, r_k, lnx_g, lnx_b, w_out_a, w_out_b, w_o, norm2_g, w_router_group, b_router_group, w_router_expert, b_router_expert, w_e_gate, w_e_up, w_e_down, normf_g):
    depth = norm1_g.shape[0]
    bp, seq, d = x_prompt.shape
    db, dseq, _ = x_sample.shape
    assert depth == 1 and bp == 1, "single layer, single prompt stream"
    tm = ROW_TILE
    n_p, n_s = bp * seq, db * dseq
    assert n_p % tm == 0 and n_s % tm == 0 and tm % dseq == 0
    n_prompt_tiles, n_sample_tiles = n_p // tm, n_s // tm
    seqs = tm // dseq
    n_tok = n_p + n_s

    x_p = x_prompt.reshape(n_p, d)
    x_s = x_sample.reshape(n_s, d)

    l = 0
    c3 = 3 * CONV_DIM
    head_id = jnp.arange(RWKV_DIM, dtype=jnp.int32) // RWKV_HEAD
    ones_bf16 = (head_id[:, None] == head_id[None, :]).astype(BF16)
    zpad = jnp.zeros((LORA_W, RWKV_DIM), F32)
    n_r = N_GROUPS + N_EXPERTS
    p = {
        'norm1_g': norm1_g[l].reshape(1, d),
        'w_in_a': w_in[l][:, :c3].astype(BF16),
        'w_in_b': w_in[l][:, c3:c3 + SHIFT_DIM].astype(BF16),
        'w_in_g': w_in[l][:, c3 + SHIFT_DIM:].astype(BF16),
        'conv_w': conv_w[l],
        'mu_shift': mu_shift[l].reshape(1, SHIFT_DIM),
        'w0': w0[l].reshape(1, RWKV_DIM),
        'w_lora_w': _stack3(jnp.concatenate([w_lora_w[l], zpad], axis=0)),
        'a0': a0[l].reshape(1, RWKV_DIM),
        'w_lora_a': _stack3(jnp.concatenate([zpad, w_lora_a[l]], axis=0)),
        'w_lora_g': _stack3(w_lora_g[l]),
        'k_k': k_k[l].reshape(1, RWKV_DIM),
        'k_a': k_a[l].reshape(1, RWKV_DIM),
        'r_k': r_k[l].reshape(1, RWKV_DIM),
        'w_out_a': w_out_a[l].astype(BF16),
        'ones_bf16': ones_bf16,
        'lnx_g': lnx_g[l].reshape(1, RWKV_DIM),
        'lnx_b': lnx_b[l].reshape(1, RWKV_DIM),
        'w_out_b': w_out_b[l].astype(BF16),
        'w_o': w_o[l].astype(BF16),
        'norm2_g': norm2_g[l].reshape(1, d),
        'w_router': _stack3(jnp.pad(jnp.concatenate([w_router_group[l], w_router_expert[l]], axis=1),
                                    ((0, 0), (0, LANES - n_r)))),
        'b_router': jnp.pad(jnp.concatenate([b_router_group[l], b_router_expert[l]]),
                            (0, LANES - n_r)).reshape(1, LANES),
    }
    st_conv_t = state_conv[l].reshape(n_sample_tiles, seqs * (CONV_WIDTH - 1), CONV_DIM)
    st_shift_t = state_shift[l].reshape(n_sample_tiles, seqs, SHIFT_DIM)

    (yag, sgb, r, w, k, v, a, b, bonus, g, ctail, stail) = _mixer_pre(
        x_p, x_s, st_conv_t, st_shift_t, p, seq_len=dseq, tm=tm)

    s0_prompt = jnp.zeros((1, N_HEAD_GROUPS, GROUP_LANES, GROUP_LANES), F32)
    s0_sample = _state_to_kernel(state_wkv[l])
    rlkvab = (r, w, k, v, a, b)
    cp = min(SCAN_CHUNK, n_p)
    assert cp & (cp - 1) == 0 and dseq & (dseq - 1) == 0
    y_p, s_p = _wkv_scan(rlkvab, s0_prompt, row0=0, n_rows=n_p, chunk=cp,
                         chunks_per_step=SCAN_CHUNKS_PER_STEP, chained=True)
    y_s, s_s = _wkv_scan(rlkvab, s0_sample, row0=n_p, n_rows=n_s, chunk=dseq,
                         chunks_per_step=SCAN_CHUNKS_PER_STEP, chained=False)

    x1, h2, route, counts = _mixer_post(y_p, y_s, bonus, g, yag, sgb, x_p, x_s, p, tm=tm)

    blk = MOE_ROWS
    pos_tiles, block_expert, n_used, n_blocks = _dispatch(route, counts, blk, tm)
    xs = _moe_scatter(h2, pos_tiles, n_blocks * blk)
    yb = _moe_experts(xs, block_expert, n_used, w_e_gate[l], w_e_up[l], w_e_down[l], blk=blk)
    out_p, out_s = _moe_combine(pos_tiles, x1, route, normf_g.reshape(1, d), yb, n_prompt_rows=n_p, tm=tm)

    y_prompt = out_p.reshape(bp, seq, d)
    y_sample = out_s.reshape(db, dseq, d)
    conv_p = ctail[n_prompt_tiles - 1, 2 * (seqs - 1):2 * seqs].reshape(1, bp, CONV_WIDTH - 1, CONV_DIM)
    shift_p = stail[n_prompt_tiles - 1, seqs - 1].reshape(1, bp, 1, SHIFT_DIM)
    wkv_p = _state_from_kernel(s_p).reshape(1, bp, RWKV_HEADS, RWKV_HEAD, RWKV_HEAD)
    conv_s = ctail[n_prompt_tiles:].reshape(1, db, CONV_WIDTH - 1, CONV_DIM)
    shift_s = stail[n_prompt_tiles:].reshape(1, db, 1, SHIFT_DIM)
    wkv_s = _state_from_kernel(s_s).reshape(1, db, RWKV_HEADS, RWKV_HEAD, RWKV_HEAD)
    return (y_prompt, y_sample, conv_p, shift_p, wkv_p, conv_s, shift_s, wkv_s)
```

```python
import functools

import numpy as np
import jax
import jax.numpy as jnp
from jax import lax
from jax.experimental import pallas as pl
from jax.experimental.pallas import tpu as pltpu

F32 = jnp.float32
BF16 = jnp.bfloat16

CONV_DIM = 512
CONV_WIDTH = 3
RWKV_HEAD = 64
RWKV_HEADS = 8
RWKV_DIM = RWKV_HEADS * RWKV_HEAD
LORA_W = 64
LORA_A = 64
LORA_G = 128
SHIFT_DIM = 3 * RWKV_DIM + LORA_W + LORA_A + LORA_G
N_GROUPS = 4
EXPERTS_PER_GROUP = 8
N_EXPERTS = N_GROUPS * EXPERTS_PER_GROUP
TOP_K = 2
RMS_EPS = 1e-6
GN_EPS = 64e-5

LANES = 128
ROW_TILE = 256
SCAN_CHUNK = 64
SCAN_CHUNKS_PER_STEP = 4
MOE_ROWS = 256
HIST = 8
VMEM_LIMIT = 56 * 1024 * 1024


def _rms(x, g):
    return x * lax.rsqrt(jnp.mean(x * x, axis=-1, keepdims=True) + RMS_EPS) * g


def _split2_dot(x, ones_bf16):
    hi = x.astype(BF16)
    lo = (x - hi.astype(F32)).astype(BF16)
    return (jnp.dot(hi, ones_bf16, preferred_element_type=F32)
            + jnp.dot(lo, ones_bf16, preferred_element_type=F32))


def _pack_bf16_pairs(x):
    n = x.shape[1] // 2
    lo = pltpu.bitcast(x[:, :n].astype(BF16).astype(F32), jnp.uint32)
    hi = pltpu.bitcast(x[:, n:].astype(BF16).astype(F32), jnp.uint32)
    return hi | (lo >> 16)


def _unpack_bf16_pairs(p):
    lo = pltpu.bitcast(p << 16, F32)
    hi = pltpu.bitcast(p & jnp.uint32(0xFFFF0000), F32)
    return jnp.concatenate([lo, hi], axis=1)


def _stack3(w):
    hi = w.astype(BF16)
    lo = (w - hi.astype(F32)).astype(BF16)
    return jnp.concatenate([hi, hi, lo], axis=0)


def _dot_stack3(x, w_stack):
    hi = x.astype(BF16)
    lo = (x - hi.astype(F32)).astype(BF16)
    return jnp.dot(jnp.concatenate([hi, lo, hi], axis=1), w_stack, preferred_element_type=F32)


def _const_spec(shape):
    nd = len(shape)
    return pl.BlockSpec(shape, lambda *_: (0,) * nd)


def _mixer_pre_kernel(xp_ref, xs_ref, stc_ref, sts_ref, n1_ref, wa_ref, wb_ref, wg_ref, convw_ref, mu_ref,
                      w0_ref, lw_ref, a0_ref, la_ref, lgw_ref, kk_ref, ka_ref, rk_ref, woa_ref,
                      ones_ref,
                      yag_ref, sgb_ref, r_ref, w_ref, k_ref, v_ref, a_ref, b_ref, bonus_ref, g_ref,
                      ctail_ref, stail_ref,
                      cbuf, sbuf, p1buf, p2buf, spbuf, *, n_prompt_tiles, seq_len):
    i = pl.program_id(0)
    tm = xp_ref.shape[0]
    seqs = tm // seq_len

    @pl.when(i == 0)
    def _():
        cbuf[0:HIST, :] = jnp.zeros((HIST, CONV_DIM), F32)
        sbuf[0:HIST, :] = jnp.zeros((HIST, SHIFT_DIM), F32)

    x = jnp.where(i < n_prompt_tiles, xp_ref[...], xs_ref[...])
    h = _rms(x, n1_ref[...]).astype(BF16)

    pa = jnp.dot(h, wa_ref[...], preferred_element_type=F32)
    g_in = pa[:, 0:CONV_DIM]
    g_out = pa[:, CONV_DIM:2 * CONV_DIM]
    x_c = pa[:, 2 * CONV_DIM:3 * CONV_DIM]
    bx = g_in * x_c
    cbuf[HIST:HIST + tm, :] = bx
    p1buf[...] = cbuf[HIST - 1:HIST - 1 + tm, :]
    p2buf[...] = cbuf[HIST - 2:HIST - 2 + tm, :]

    @pl.when(i >= n_prompt_tiles)
    def _():
        for j in range(seqs):
            r0 = j * seq_len
            p1buf[r0:r0 + 1, :] = stc_ref[0, 2 * j + 1:2 * j + 2, :]
            p2buf[r0:r0 + 1, :] = stc_ref[0, 2 * j:2 * j + 1, :]
            p2buf[r0 + 1:r0 + 2, :] = stc_ref[0, 2 * j + 1:2 * j + 2, :]

    cw = convw_ref[...]
    conv = cw[0:1, :] * p2buf[...] + cw[1:2, :] * p1buf[...] + cw[2:3, :] * bx
    y_a = jnp.dot((g_out * conv).astype(BF16), woa_ref[...], preferred_element_type=F32)
    for j in range(seqs):
        r1 = HIST + (j + 1) * seq_len
        ctail_ref[0, 2 * j:2 * j + 2, :] = cbuf[r1 - 2:r1, :]
    cbuf[HIST - 2:HIST, :] = cbuf[HIST + tm - 2:HIST + tm, :]

    pg = jnp.dot(h, wg_ref[...], preferred_element_type=F32)
    d = pg.shape[1] // 2
    yag_ref[...] = jax.nn.sigmoid(pg[:, 0:d]) * y_a
    sgb_ref[...] = jax.nn.sigmoid(pg[:, d:2 * d])

    pb = jnp.dot(h, wb_ref[...], preferred_element_type=F32)
    sbuf[HIST:HIST + tm, :] = pb
    spbuf[...] = sbuf[HIST - 1:HIST - 1 + tm, :]

    @pl.when(i >= n_prompt_tiles)
    def _():
        for j in range(seqs):
            r0 = j * seq_len
            spbuf[r0:r0 + 1, :] = sts_ref[0, j:j + 1, :]

    for j in range(seqs):
        r1 = HIST + (j + 1) * seq_len
        stail_ref[0, j:j + 1, :] = sbuf[r1 - 1:r1, :]
    sbuf[HIST - 1:HIST, :] = sbuf[HIST + tm - 1:HIST + tm, :]

    s = pb + (spbuf[...] - pb) * mu_ref[...]
    o1, o2, o3 = RWKV_DIM, 2 * RWKV_DIM, 3 * RWKV_DIM
    r = s[:, 0:o1]
    k = s[:, o1:o2]
    v = s[:, o2:o3]
    s_l = s[:, o3:o3 + LORA_W + LORA_A]
    lg = s[:, o3 + LORA_W + LORA_A:]
    z = w0_ref[...] + _dot_stack3(jnp.tanh(s_l), lw_ref[...])
    w_log = -jax.nn.softplus(-z) - 0.5
    log_decay = -jnp.exp(w_log)
    a = jax.nn.sigmoid(a0_ref[...] + _dot_stack3(s_l, la_ref[...]))
    g = _dot_stack3(jax.nn.sigmoid(lg), lgw_ref[...])

    ones = ones_ref[...]
    kk = k * kk_ref[...]
    kk_n = kk / jnp.maximum(jnp.sqrt(_split2_dot(kk * kk, ones)), 1e-12)
    k2 = k * (1.0 + (a - 1.0) * ka_ref[...])
    bonus = _split2_dot(r * k2 * rk_ref[...], ones) * v

    r_ref[...] = r
    w_ref[...] = log_decay
    k_ref[...] = k2
    v_ref[...] = v
    a_ref[...] = -kk_n
    b_ref[...] = kk_n * a
    bonus_ref[...] = bonus
    g_ref[...] = g


def _mixer_pre(x_p, x_s, st_conv_t, st_shift_t, p, *, seq_len, tm):
    n_p, d = x_p.shape
    n_tok = n_p + x_s.shape[0]
    n_prompt_tiles = n_p // tm
    n_tiles = n_tok // tm
    seqs = tm // seq_len
    row = lambda w: pl.BlockSpec((tm, w), lambda i: (i, 0))
    st_idx = lambda i: (jnp.maximum(i - n_prompt_tiles, 0), 0, 0)
    consts = [p['norm1_g'], p['w_in_a'], p['w_in_b'], p['w_in_g'], p['conv_w'], p['mu_shift'],
              p['w0'], p['w_lora_w'], p['a0'], p['w_lora_a'], p['w_lora_g'], p['k_k'], p['k_a'],
              p['r_k'], p['w_out_a'], p['ones_bf16']]
    in_specs = [pl.BlockSpec((tm, d), lambda i: (jnp.minimum(i, n_prompt_tiles - 1), 0)),
                pl.BlockSpec((tm, d), lambda i: (jnp.maximum(i - n_prompt_tiles, 0), 0)),
                pl.BlockSpec((1, 2 * seqs, CONV_DIM), st_idx),
                pl.BlockSpec((1, seqs, SHIFT_DIM), st_idx)] + [_const_spec(c.shape) for c in consts]
    sds = lambda w: jax.ShapeDtypeStruct((n_tok, w), F32)
    out_shape = [sds(d), sds(d)] + [sds(RWKV_DIM)] * 8 + [
        jax.ShapeDtypeStruct((n_tiles, 2 * seqs, CONV_DIM), F32),
        jax.ShapeDtypeStruct((n_tiles, seqs, SHIFT_DIM), F32)]
    out_specs = [row(d), row(d)] + [row(RWKV_DIM)] * 8 + [
        pl.BlockSpec((1, 2 * seqs, CONV_DIM), lambda i: (i, 0, 0)),
        pl.BlockSpec((1, seqs, SHIFT_DIM), lambda i: (i, 0, 0))]
    kern = functools.partial(_mixer_pre_kernel, n_prompt_tiles=n_prompt_tiles, seq_len=seq_len)
    return pl.pallas_call(
        kern, out_shape=out_shape, grid=(n_tiles,), in_specs=in_specs, out_specs=out_specs,
        scratch_shapes=[pltpu.VMEM((tm + HIST, CONV_DIM), F32), pltpu.VMEM((tm + HIST, SHIFT_DIM), F32),
                        pltpu.VMEM((tm, CONV_DIM), F32), pltpu.VMEM((tm, CONV_DIM), F32),
                        pltpu.VMEM((tm, SHIFT_DIM), F32)],
        compiler_params=pltpu.CompilerParams(dimension_semantics=("arbitrary",),
                                             vmem_limit_bytes=VMEM_LIMIT),
        name="mixer_pre")(x_p, x_s, st_conv_t, st_shift_t, *consts)


GROUP_LANES = 256
HEADS_PER_GROUP = GROUP_LANES // RWKV_HEAD
N_HEAD_GROUPS = RWKV_DIM // GROUP_LANES
NN = (((1,), (0,)), ((), ()))
NT = (((1,), (1,)), ((), ()))


def _split2(x):
    hi = x.astype(BF16)
    lo = (x - hi.astype(F32)).astype(BF16)
    return hi, lo


def _mm(xs, ys, dims=NN):
    x1, x2 = xs
    y1, y2 = ys
    d = lambda p, q: lax.dot_general(p, q, dims, preferred_element_type=F32)
    m = x1.shape[0]
    both = d(jnp.concatenate([x1, x2], axis=0), y1)
    return both[0:m] + both[m:2 * m] + d(x1, y2)


def _cat2(ps, qs, axis):
    return tuple(jnp.concatenate([p, q], axis=axis) for p, q in zip(ps, qs))


def _wkv_masks(c):
    hc = HEADS_PER_GROUP * c
    levels = c.bit_length() - 1
    t = np.arange(c)[:, None]
    s = (np.arange(hc) % c)[None, :]
    tm = [s < t, s <= t]
    for lvl in range(1, levels + 1):
        half = 1 << (lvl - 1)
        tm.append(((t >> lvl) == (s >> lvl)) & ((t & half) != 0) & ((s & half) == 0))
    row_head = (np.arange(hc) // c)[:, None]
    mfeat = row_head == (np.arange(GROUP_LANES) // RWKV_HEAD)[None, :]
    mpos = row_head == (np.arange(hc) // c)[None, :]
    lane_head = np.arange(GROUP_LANES) // RWKV_HEAD
    stmask = lane_head[:, None] == lane_head[None, :]
    tri = np.arange(c)[None, :] <= np.arange(c)[:, None]
    return (jnp.asarray(mfeat, BF16), jnp.asarray(mpos, BF16), jnp.asarray(np.stack(tm), F32),
            jnp.asarray(stmask, F32), jnp.asarray(tri, BF16))


def _interleave(*gens):
    live = list(gens)
    while live:
        for gen in list(live):
            try:
                next(gen)
            except StopIteration:
                live.remove(gen)


def _wkv_pipe_kernel(r_ref, lw_ref, k_ref, v_ref, a_ref, b_ref, s0_ref, mfeat_ref, mpos_ref,
                     tmask_ref, stmask_ref, tri_ref, y_ref, sout_ref,
                     state, sv_ar, sv_inv, sv_akv, sv_arbk, sv_vbd, sv_v, sv_bk, sv_pend,
                     *, chunk, n_steps, chained):
    i = pl.program_id(0)
    c = chunk
    n_chunks = r_ref.shape[0] // c
    hc = HEADS_PER_GROUP * c
    levels = c.bit_length() - 1
    groups = range(N_HEAD_GROUPS)
    units = [(j, g) for j in range(n_chunks) for g in groups]
    uid = {u: n for n, u in enumerate(units)}

    @pl.when(i == 0)
    def _():
        for ref in (sv_ar, sv_inv, sv_akv, sv_arbk, sv_vbd, sv_v, sv_bk):
            ref[...] = jnp.zeros(ref.shape, ref.dtype)
        sv_pend[...] = jnp.ones(sv_pend.shape, F32)
        if chained:
            state[...] = s0_ref[0]

    mfeat = mfeat_ref[...]
    mpos = mpos_ref[...]
    strict = tmask_ref[0]
    incl = tmask_ref[1]
    eye = incl - strict
    tri = tri_ref[...]
    stmask = stmask_ref[...]

    def ld(ref, j, g):
        return ref[j * c:(j + 1) * c, g * GROUP_LANES:(g + 1) * GROUP_LANES]

    def bd_split(ps):
        mask = mpos if ps[0].shape[1] == hc else mfeat
        return tuple(jnp.concatenate([p] * HEADS_PER_GROUP, axis=0) * mask for p in ps)

    def bd2(m):
        return bd_split(_split2(m))

    def cumsum_rows(x):
        p1 = x.astype(BF16)
        r1 = x - p1.astype(F32)
        p2 = r1.astype(BF16)
        p3 = (r1 - p2.astype(F32)).astype(BF16)
        d = lambda q: jnp.dot(tri, q, preferred_element_type=F32)
        return d(p1) + d(p2) + d(p3)

    new = {}

    def prepare():
        cum = {u: cumsum_rows(ld(lw_ref, *u)) for u in units}
        yield
        ar, bk_end, p_end, a_ab, a_ak, a_rb, a_rk, v, vbd = ({} for _ in range(9))
        for u in units:
            cm = cum[u]
            cum_last = cm[c - 1:c, :]
            e_neg = jnp.exp(-cm)
            e_end = jnp.exp(cum_last - cm)
            b_raw = ld(b_ref, *u)
            k_raw = ld(k_ref, *u)
            ar[u] = _split2(jnp.concatenate([ld(a_ref, *u) * jnp.exp(cm - ld(lw_ref, *u)),
                                             ld(r_ref, *u) * jnp.exp(cm)], axis=0))
            bk_end[u] = _split2(jnp.concatenate([b_raw * e_end, k_raw * e_end], axis=0))
            p_end[u] = jnp.exp(cum_last)
            v[u] = ld(v_ref, *u)
            vbd[u] = bd2(v[u])
            gram = _mm(ar[u], _cat2(bd2(b_raw * e_neg), bd2(k_raw * e_neg), 0), NT)
            a_ab[u] = jnp.where(strict > 0, gram[0:c, 0:hc], 0.0)
            a_ak[u] = jnp.where(strict > 0, gram[0:c, hc:2 * hc], 0.0)
            a_rb[u] = jnp.where(incl > 0, gram[c:2 * c, 0:hc], 0.0)
            a_rk[u] = jnp.where(incl > 0, gram[c:2 * c, hc:2 * hc], 0.0)
            if uid[u] % 2 == 1:
                yield
        a_ab2 = {u: _split2(a_ab[u]) for u in units}
        inv = {u: eye + a_ab[u] * tmask_ref[2] for u in units}
        inv2 = {u: _split2(inv[u]) for u in units}
        for lvl in range(2, levels + 1):
            lm = tmask_ref[1 + lvl]
            t1 = {u: _mm(a_ab2[u], bd_split(inv2[u])) for u in units}
            yield
            inv = {u: inv[u] + lm * _mm(inv2[u], bd2(t1[u])) for u in units}
            inv2 = {u: _split2(inv[u]) for u in units}
            yield
        akv = {u: _mm(_split2(a_ak[u]), vbd[u]) for u in units}
        arbk = {u: _split2(jnp.concatenate([a_rb[u], a_rk[u]], axis=1)) for u in units}
        new.update(ar=ar, inv=inv2, akv=akv, arbk=arbk, vbd=vbd, v=v, bk=bk_end, pend=p_end)

    def serial():
        pair = lambda ref, n: (ref[0, n], ref[1, n])
        s_cur = [state[g] for g in groups] if chained else None
        for j in range(n_chunks):
            ns = [uid[j, g] for g in groups]
            s_prev = s_cur if chained else [s0_ref[j, g] for g in groups]
            x0 = [_mm(pair(sv_ar, n), _split2(s_prev[g]), NT) for g, n in zip(groups, ns)]
            yield
            uu = [_mm(pair(sv_inv, n), bd2(x0[g][0:c] + sv_akv[n])) for g, n in zip(groups, ns)]
            yield
            for g, n in zip(groups, ns):
                yy = x0[g][c:2 * c] + _mm(pair(sv_arbk, n), _cat2(bd2(uu[g]), pair(sv_vbd, n), 0))
                y_ref[j * c:(j + 1) * c, g * GROUP_LANES:(g + 1) * GROUP_LANES] = yy
            s_new = []
            for g, n in zip(groups, ns):
                uv_t = jnp.transpose(jnp.concatenate([uu[g], sv_v[n]], axis=0))
                upd = _mm(_split2(uv_t), pair(sv_bk, n))
                s_new.append(s_prev[g] * sv_pend[n, 0:1, :] + stmask * upd)
            yield
            if chained:
                s_cur = s_new
            else:
                for g in groups:
                    sout_ref[j, g] = s_new[g]
        if chained:
            for g in groups:
                state[g] = s_cur[g]
        new['last_state'] = s_new

    _interleave(serial(), prepare())

    bits = sum(pltpu.bitcast(s[0:8, 0:LANES], jnp.uint32) for s in new['last_state'])
    zero = pltpu.bitcast(lax.shift_right_logical(bits, jnp.uint32(32)), F32)[0:1, 0:1]
    zero_bf = zero.astype(BF16)
    for u, n in uid.items():
        for name, ref in (('ar', sv_ar), ('inv', sv_inv), ('arbk', sv_arbk), ('vbd', sv_vbd), ('bk', sv_bk)):
            for half in range(2):
                ref[half, n] = new[name][u][half] + zero_bf
        sv_akv[n] = new['akv'][u] + zero
        sv_v[n] = new['v'][u] + zero
        sv_pend[n] = jnp.broadcast_to(new['pend'][u] + zero, sv_pend.shape[1:])

    if chained:
        @pl.when(i == n_steps)
        def _():
            sout_ref[0] = state[...]


def _wkv_scan(rlkvab, s0, *, row0, n_rows, chunk, chunks_per_step, chained):
    rows = chunk * chunks_per_step
    n_steps = n_rows // rows
    assert n_rows % rows == 0 and row0 % rows == 0
    masks = _wkv_masks(chunk)
    hc = HEADS_PER_GROUP * chunk
    n_units = chunks_per_step * N_HEAD_GROUPS
    st = (N_HEAD_GROUPS, GROUP_LANES, GROUP_LANES)
    prev = lambda i: jnp.maximum(i - 1, 0)
    row_in = pl.BlockSpec((rows, RWKV_DIM), lambda i: (row0 // rows + jnp.minimum(i, n_steps - 1), 0))
    if chained:
        n_state = 1
        st_spec = pl.BlockSpec((1,) + st, lambda i: (0, 0, 0, 0))
    else:
        n_state = n_rows // chunk
        st_spec = pl.BlockSpec((chunks_per_step,) + st, lambda i: (prev(i), 0, 0, 0))
    in_specs = [row_in] * 6 + [st_spec] + [_const_spec(m.shape) for m in masks]
    out_shape = [jax.ShapeDtypeStruct((n_rows, RWKV_DIM), F32),
                 jax.ShapeDtypeStruct((n_state,) + st, F32)]
    out_specs = [pl.BlockSpec((rows, RWKV_DIM), lambda i: (prev(i), 0)), st_spec]
    scratch = [pltpu.VMEM(st, F32),
               pltpu.VMEM((2, n_units, 2 * chunk, GROUP_LANES), BF16),
               pltpu.VMEM((2, n_units, chunk, hc), BF16),
               pltpu.VMEM((n_units, chunk, GROUP_LANES), F32),
               pltpu.VMEM((2, n_units, chunk, 2 * hc), BF16),
               pltpu.VMEM((2, n_units, hc, GROUP_LANES), BF16),
               pltpu.VMEM((n_units, chunk, GROUP_LANES), F32),
               pltpu.VMEM((2, n_units, 2 * chunk, GROUP_LANES), BF16),
               pltpu.VMEM((n_units, 8, GROUP_LANES), F32)]
    kern = functools.partial(_wkv_pipe_kernel, chunk=chunk, n_steps=n_steps, chained=chained)
    return pl.pallas_call(
        kern, out_shape=out_shape, grid=(n_steps + 1,), in_specs=in_specs, out_specs=out_specs,
        scratch_shapes=scratch,
        compiler_params=pltpu.CompilerParams(dimension_semantics=("arbitrary",),
                                             vmem_limit_bytes=VMEM_LIMIT),
        name="wkv_scan")(*rlkvab, s0, *masks)


def _mixer_post_kernel(yp_ref, ys_ref, bonus_ref, g_ref, yag_ref, sgb_ref, xp_ref, xs_ref, lng_ref, lnb_ref,
                       wob_ref, wo_ref, n2_ref, wr_ref, br_ref, ones_ref, tril_ref,
                       x1_ref, h2_ref, route_ref, counts_ref, cnt, *, n_prompt_tiles):
    i = pl.program_id(0)

    @pl.when(i == 0)
    def _():
        cnt[...] = jnp.zeros(cnt.shape, F32)

    x = jnp.where(i < n_prompt_tiles, xp_ref[...], xs_ref[...])
    ones = ones_ref[...]
    y = jnp.where(i < n_prompt_tiles, yp_ref[...], ys_ref[...])
    inv_n = 1.0 / RWKV_HEAD
    mean = _split2_dot(y, ones) * inv_n
    yc = y - mean
    var = _split2_dot(yc * yc, ones) * inv_n
    yn = yc * lax.rsqrt(var + GN_EPS) * lng_ref[...] + lnb_ref[...]
    yy = (yn + bonus_ref[...]) * g_ref[...]
    y_b = jnp.dot(yy.astype(BF16), wob_ref[...], preferred_element_type=F32)
    merged = yag_ref[...] + sgb_ref[...] * y_b
    x1 = x + jnp.dot(merged.astype(BF16), wo_ref[...], preferred_element_type=F32)
    x1_ref[...] = x1
    h2 = _rms(x1, n2_ref[...])
    h2_ref[...] = _pack_bf16_pairs(h2)

    logits = _dot_stack3(h2, wr_ref[...]) + br_ref[...]
    lane = lax.broadcasted_iota(jnp.int32, logits.shape, 1)
    neg = jnp.float32(-jnp.inf)
    big = jnp.int32(LANES)
    is_g = lane < N_GROUPS
    lgp = jnp.where(is_g, logits, neg)
    m_g = jnp.max(lgp, axis=-1, keepdims=True)
    grp = jnp.min(jnp.where(lgp == m_g, lane, big), axis=-1, keepdims=True)
    p_top = 1.0 / jnp.sum(jnp.where(is_g, jnp.exp(logits - m_g), 0.0), axis=-1, keepdims=True)
    e_lane = lane - N_GROUPS
    in_grp = (e_lane >= grp * EXPERTS_PER_GROUP) & (e_lane < (grp + 1) * EXPERTS_PER_GROUP)
    le = jnp.where(in_grp, logits, neg)
    m1 = jnp.max(le, axis=-1, keepdims=True)
    i1 = jnp.min(jnp.where(le == m1, lane, big), axis=-1, keepdims=True)
    le2 = jnp.where(lane == i1, neg, le)
    m2 = jnp.max(le2, axis=-1, keepdims=True)
    i2 = jnp.min(jnp.where(le2 == m2, lane, big), axis=-1, keepdims=True)
    ex = jnp.exp(m2 - m1)
    p1 = 1.0 / (1.0 + ex)
    p2 = ex / (1.0 + ex)
    oh1 = lane == i1 - N_GROUPS
    oh2 = lane == i2 - N_GROUPS
    both = jnp.where(oh1, 1.0, jnp.where(oh2, 1.0, 0.0))
    before = jnp.dot(tril_ref[...], both.astype(BF16), preferred_element_type=F32) + cnt[0:1, :]
    rank1 = jnp.sum(jnp.where(oh1, before, 0.0), axis=-1, keepdims=True)
    rank2 = jnp.sum(jnp.where(oh2, before, 0.0), axis=-1, keepdims=True)
    cnt[0:1, :] = cnt[0:1, :] + jnp.sum(both, axis=0, keepdims=True)
    counts_ref[...] = jnp.broadcast_to(cnt[0:1, :], counts_ref.shape)

    cols = [(i1 - N_GROUPS).astype(F32), (i2 - N_GROUPS).astype(F32), p_top * p1, p_top * p2,
            rank1, rank2]
    route = jnp.zeros(logits.shape, F32)
    for c, col in enumerate(cols):
        route = jnp.where(lane == c, col, route)
    route_ref[...] = route


ROUTE_EXPERT, ROUTE_WEIGHT, ROUTE_RANK = 0, 2, 4


def _mixer_post(y_p, y_s, bonus, g, yag, sgb, x_p, x_s, p, *, tm):
    n_p, d = x_p.shape
    n_tok = n_p + x_s.shape[0]
    npt = n_p // tm
    row = lambda w: pl.BlockSpec((tm, w), lambda i: (i, 0))
    tril = jnp.asarray(np.arange(tm)[None, :] < np.arange(tm)[:, None], BF16)
    consts = [p['lnx_g'], p['lnx_b'], p['w_out_b'], p['w_o'], p['norm2_g'], p['w_router'],
              p['b_router'], p['ones_bf16'], tril]
    pair = lambda w: [pl.BlockSpec((tm, w), lambda i: (jnp.minimum(i, npt - 1), 0)),
                      pl.BlockSpec((tm, w), lambda i: (jnp.maximum(i - npt, 0), 0))]
    in_specs = (pair(RWKV_DIM) + [row(RWKV_DIM)] * 2 + [row(d)] * 2 + pair(d)
                + [_const_spec(c.shape) for c in consts])
    out_shape = [jax.ShapeDtypeStruct((n_tok, d), F32), jax.ShapeDtypeStruct((n_tok, d // 2), jnp.uint32),
                 jax.ShapeDtypeStruct((n_tok, LANES), F32), jax.ShapeDtypeStruct((8, LANES), F32)]
    out_specs = [row(d), row(d // 2), row(LANES), _const_spec((8, LANES))]
    return pl.pallas_call(
        functools.partial(_mixer_post_kernel, n_prompt_tiles=npt), out_shape=out_shape,
        grid=(n_tok // tm,), in_specs=in_specs, out_specs=out_specs,
        scratch_shapes=[pltpu.VMEM((8, LANES), F32)],
        compiler_params=pltpu.CompilerParams(dimension_semantics=("arbitrary",),
                                             vmem_limit_bytes=VMEM_LIMIT),
        name="mixer_post")(y_p, y_s, bonus, g, yag, sgb, x_p, x_s, *consts)


def _gather_rows(idx_ref, src_hbm, dst, sem, n_rows, *, unrolled):
    def start(r):
        pltpu.make_async_copy(src_hbm.at[pl.ds(idx_ref[r], 1)], dst.at[pl.ds(r, 1)], sem).start()
    if unrolled:
        for r in range(n_rows):
            start(r)
    else:
        def body(r, carry):
            start(r)
            return carry
        lax.fori_loop(0, n_rows, body, 0)


def _wait_rows(src_hbm, dst, sem, n_rows):
    pltpu.make_async_copy(src_hbm.at[pl.ds(0, n_rows)], dst, sem).wait()


def _moe_scatter_kernel(dest_ref, h_ref, xs_in_hbm, xs_hbm, sem):
    del xs_in_hbm
    n = dest_ref.shape[-1]
    tm = h_ref.shape[0]
    for r in range(n):
        pltpu.make_async_copy(h_ref.at[pl.ds(r % tm, 1)],
                              xs_hbm.at[pl.ds(dest_ref[0, 0, r], 1)], sem).start()
    for j in range(n // tm):
        pltpu.make_async_copy(h_ref, xs_hbm.at[pl.ds(0, tm)], sem).wait()


def _moe_scatter(h2, pos3, n_rows):
    n_tok, d = h2.shape
    n_tiles, _, n = pos3.shape
    tm = n // TOP_K
    xs0 = jnp.zeros((n_rows, d), h2.dtype)
    return pl.pallas_call(
        _moe_scatter_kernel, out_shape=jax.ShapeDtypeStruct((n_rows, d), h2.dtype), grid=(n_tiles,),
        in_specs=[pl.BlockSpec((1, 1, n), lambda i: (i, 0, 0), memory_space=pltpu.SMEM),
                  pl.BlockSpec((tm, d), lambda i: (i, 0)), pl.BlockSpec(memory_space=pl.ANY)],
        out_specs=pl.BlockSpec(memory_space=pl.ANY),
        scratch_shapes=[pltpu.SemaphoreType.DMA(())], input_output_aliases={2: 0},
        compiler_params=pltpu.CompilerParams(dimension_semantics=("arbitrary",), has_side_effects=True,
                                             vmem_limit_bytes=VMEM_LIMIT),
        name="moe_scatter")(pos3, h2, xs0)


def _moe_experts_kernel(bexp_ref, nused_ref, x_ref, wg_ref, wu_ref, wd_ref, yb_ref, wg_bf, wu_bf, wd_bf):
    i = pl.program_id(0)

    @pl.when((i == 0) | (bexp_ref[i] != bexp_ref[jnp.maximum(i - 1, 0)]))
    def _():
        wg_bf[...] = wg_ref[0].astype(BF16)
        wu_bf[...] = wu_ref[0].astype(BF16)
        wd_bf[...] = wd_ref[0].astype(BF16)

    @pl.when(i < nused_ref[0])
    def _():
        xb = _unpack_bf16_pairs(x_ref[...]).astype(BF16)
        hg = jnp.dot(xb, wg_bf[...], preferred_element_type=F32)
        hu = jnp.dot(xb, wu_bf[...], preferred_element_type=F32)
        hid = (hg * jax.nn.sigmoid(hg)) * hu
        yb_ref[...] = _pack_bf16_pairs(jnp.dot(hid.astype(BF16), wd_bf[...], preferred_element_type=F32))

    @pl.when(i >= nused_ref[0])
    def _():
        yb_ref[...] = jnp.zeros(yb_ref.shape, yb_ref.dtype)


def _moe_experts(xs, block_expert, n_used, w_eg, w_eu, w_ed, *, blk):
    n_rows, dp = xs.shape
    n_blocks = n_rows // blk
    d, de = w_eg.shape[1:]
    grid_spec = pltpu.PrefetchScalarGridSpec(
        num_scalar_prefetch=2, grid=(n_blocks,),
        in_specs=[
            pl.BlockSpec((blk, dp), lambda i, be, nu: (jnp.minimum(i, nu[0] - 1), 0)),
            pl.BlockSpec((1, d, de), lambda i, be, nu: (be[i], 0, 0)),
            pl.BlockSpec((1, d, de), lambda i, be, nu: (be[i], 0, 0)),
            pl.BlockSpec((1, de, d), lambda i, be, nu: (be[i], 0, 0)),
        ],
        out_specs=pl.BlockSpec((blk, dp), lambda i, be, nu: (i, 0)),
        scratch_shapes=[pltpu.VMEM((d, de), BF16), pltpu.VMEM((d, de), BF16), pltpu.VMEM((de, d), BF16)])
    return pl.pallas_call(
        _moe_experts_kernel, out_shape=jax.ShapeDtypeStruct((n_rows, dp), xs.dtype), grid_spec=grid_spec,
        compiler_params=pltpu.CompilerParams(dimension_semantics=("arbitrary",),
                                             vmem_limit_bytes=VMEM_LIMIT),
        name="moe_experts")(block_expert, n_used, xs, w_eg, w_eu, w_ed)


GATHER_AHEAD = 2


def _with_dummy_blocks(idx, n_blocks, blk):
    pad = jnp.zeros((GATHER_AHEAD * blk,), idx.dtype)
    return jnp.concatenate([idx, pad]).reshape(n_blocks + GATHER_AHEAD, 1, blk)


def _moe_combine_kernel(*refs, n_prompt_tiles):
    pos_refs = refs[:GATHER_AHEAD + 1]
    x1_ref, route_ref, nf_ref, yb_hbm, outp_ref, outs_ref, ybuf, sems = refs[GATHER_AHEAD + 1:]
    i = pl.program_id(0)
    n_slots = GATHER_AHEAD + 1
    rows = ybuf.shape[1]
    slot = i % n_slots

    @pl.when(i == 0)
    def _():
        for a in range(GATHER_AHEAD):
            _gather_rows(pos_refs[a].at[0, 0], yb_hbm, ybuf.at[a], sems.at[a], rows, unrolled=False)

    _wait_rows(yb_hbm, ybuf.at[slot], sems.at[slot], rows)
    ahead = (i + GATHER_AHEAD) % n_slots
    _gather_rows(pos_refs[GATHER_AHEAD].at[0, 0], yb_hbm, ybuf.at[ahead], sems.at[ahead], rows, unrolled=True)
    tm = x1_ref.shape[0]
    route = route_ref[...]
    x2 = x1_ref[...]
    for j in range(TOP_K):
        yj = _unpack_bf16_pairs(ybuf[slot, j * tm:(j + 1) * tm, :])
        x2 = x2 + yj * route[:, ROUTE_WEIGHT + j:ROUTE_WEIGHT + j + 1]
    out = _rms(x2, nf_ref[...])

    @pl.when(i < n_prompt_tiles)
    def _():
        outp_ref[...] = out

    @pl.when(i >= n_prompt_tiles)
    def _():
        outs_ref[...] = out

    @pl.when(i == pl.num_programs(0) - 1)
    def _():
        for a in range(1, n_slots):
            s = (i + a) % n_slots
            _wait_rows(yb_hbm, ybuf.at[s], sems.at[s], rows)


def _moe_combine(pos_tiles, x1, route, normf_g, yb, *, n_prompt_rows, tm):
    n_tok, d = x1.shape
    n_tiles = n_tok // tm
    npt = n_prompt_rows // tm
    pos3 = _with_dummy_blocks(pos_tiles.reshape(-1), n_tiles, TOP_K * tm)
    pos_spec = lambda a: pl.BlockSpec((1, 1, TOP_K * tm), lambda i: (i + a, 0, 0), memory_space=pltpu.SMEM)
    in_specs = [pos_spec(a) for a in range(GATHER_AHEAD + 1)] + [
        pl.BlockSpec((tm, d), lambda i: (i, 0)),
        pl.BlockSpec((tm, LANES), lambda i: (i, 0)),
        _const_spec(normf_g.shape),
        pl.BlockSpec(memory_space=pl.ANY),
    ]
    out_shape = [jax.ShapeDtypeStruct((n_prompt_rows, d), F32),
                 jax.ShapeDtypeStruct((n_tok - n_prompt_rows, d), F32)]
    out_specs = [pl.BlockSpec((tm, d), lambda i: (jnp.minimum(i, npt - 1), 0)),
                 pl.BlockSpec((tm, d), lambda i: (jnp.maximum(i - npt, 0), 0))]
    return pl.pallas_call(
        functools.partial(_moe_combine_kernel, n_prompt_tiles=npt), out_shape=out_shape,
        grid=(n_tiles,), in_specs=in_specs, out_specs=out_specs,
        scratch_shapes=[pltpu.VMEM((GATHER_AHEAD + 1, TOP_K * tm, yb.shape[1]), yb.dtype),
                        pltpu.SemaphoreType.DMA((GATHER_AHEAD + 1,))],
        compiler_params=pltpu.CompilerParams(dimension_semantics=("arbitrary",),
                                             vmem_limit_bytes=VMEM_LIMIT),
        name="moe_combine")(*([pos3] * (GATHER_AHEAD + 1)), x1, route, normf_g, yb)


def _dispatch(route, counts, blk, tm):
    n_tok = route.shape[0]
    n_assign = n_tok * TOP_K
    expert = route[:, ROUTE_EXPERT:ROUTE_EXPERT + TOP_K].astype(jnp.int32)
    rank = route[:, ROUTE_RANK:ROUTE_RANK + TOP_K].astype(jnp.int32)
    counts = counts[0, :N_EXPERTS].astype(jnp.int32)
    padded = (counts + blk - 1) // blk * blk
    pad_end = jnp.cumsum(padded)
    pad_start = pad_end - padded
    is_e = expert[:, :, None] == jnp.arange(N_EXPERTS, dtype=jnp.int32)
    dest = jnp.sum(jnp.where(is_e, pad_start, 0), axis=-1) + rank
    n_blocks = -(-n_assign // blk) + N_EXPERTS
    block_start = jnp.arange(n_blocks, dtype=jnp.int32) * blk
    block_expert = jnp.minimum(jnp.sum(block_start[:, None] >= pad_end[None, :], axis=1),
                               N_EXPERTS - 1).astype(jnp.int32)
    n_used = (pad_end[-1] // blk).reshape(1)
    pos_tiles = dest.reshape(n_tok // tm, tm, TOP_K).transpose(0, 2, 1).reshape(n_tok // tm, 1, TOP_K * tm)
    return pos_tiles, block_expert, n_used, n_blocks


def _state_to_kernel(s):
    b = s.shape[0]
    s6 = s.reshape(b, N_HEAD_GROUPS, HEADS_PER_GROUP, RWKV_HEAD, 1, RWKV_HEAD)
    eye = jnp.eye(HEADS_PER_GROUP, dtype=s.dtype).reshape(1, 1, HEADS_PER_GROUP, 1, HEADS_PER_GROUP, 1)
    return (s6 * eye).reshape(b, N_HEAD_GROUPS, GROUP_LANES, GROUP_LANES)


def _state_from_kernel(s):
    b = s.shape[0]
    s6 = s.reshape(b, N_HEAD_GROUPS, HEADS_PER_GROUP, RWKV_HEAD, HEADS_PER_GROUP, RWKV_HEAD)
    diag = jnp.stack([s6[:, :, h, :, h, :] for h in range(HEADS_PER_GROUP)], axis=2)
    return diag.reshape(b, RWKV_HEADS, RWKV_HEAD, RWKV_HEAD)


def kernel(x_prompt, x_sample, state_conv, state_shift, state_wkv, norm1_g, w_in, conv_w, mu_shift, w0, w_lora_w, a0, w_lora_a, w_lora_g, k_k, k_a, r_k, lnx_g, lnx_b, w_out_a, w_out_b, w_o, norm2_g, w_router_group, b_router_group, w_router_expert, b_router_expert, w_e_gate, w_e_up, w_e_down, normf_g):
    depth = norm1_g.shape[0]
    bp, seq, d = x_prompt.shape
    db, dseq, _ = x_sample.shape
    assert depth == 1 and bp == 1, "single layer, single prompt stream"
    tm = ROW_TILE
    n_p, n_s = bp * seq, db * dseq
    assert n_p % tm == 0 and n_s % tm == 0 and tm % dseq == 0
    n_prompt_tiles, n_sample_tiles = n_p // tm, n_s // tm
    seqs = tm // dseq
    n_tok = n_p + n_s

    x_p = x_prompt.reshape(n_p, d)
    x_s = x_sample.reshape(n_s, d)

    l = 0
    c3 = 3 * CONV_DIM
    head_id = jnp.arange(RWKV_DIM, dtype=jnp.int32) // RWKV_HEAD
    ones_bf16 = (head_id[:, None] == head_id[None, :]).astype(BF16)
    zpad = jnp.zeros((LORA_W, RWKV_DIM), F32)
    n_r = N_GROUPS + N_EXPERTS
    p = {
        'norm1_g': norm1_g[l].reshape(1, d),
        'w_in_a': w_in[l][:, :c3].astype(BF16),
        'w_in_b': w_in[l][:, c3:c3 + SHIFT_DIM].astype(BF16),
        'w_in_g': w_in[l][:, c3 + SHIFT_DIM:].astype(BF16),
        'conv_w': conv_w[l],
        'mu_shift': mu_shift[l].reshape(1, SHIFT_DIM),
        'w0': w0[l].reshape(1, RWKV_DIM),
        'w_lora_w': _stack3(jnp.concatenate([w_lora_w[l], zpad], axis=0)),
        'a0': a0[l].reshape(1, RWKV_DIM),
        'w_lora_a': _stack3(jnp.concatenate([zpad, w_lora_a[l]], axis=0)),
        'w_lora_g': _stack3(w_lora_g[l]),
        'k_k': k_k[l].reshape(1, RWKV_DIM),
        'k_a': k_a[l].reshape(1, RWKV_DIM),
        'r_k': r_k[l].reshape(1, RWKV_DIM),
        'w_out_a': w_out_a[l].astype(BF16),
        'ones_bf16': ones_bf16,
        'lnx_g': lnx_g[l].reshape(1, RWKV_DIM),
        'lnx_b': lnx_b[l].reshape(1, RWKV_DIM),
        'w_out_b': w_out_b[l].astype(BF16),
        'w_o': w_o[l].astype(BF16),
        'norm2_g': norm2_g[l].reshape(1, d),
        'w_router': _stack3(jnp.pad(jnp.concatenate([w_router_group[l], w_router_expert[l]], axis=1),
                                    ((0, 0), (0, LANES - n_r)))),
        'b_router': jnp.pad(jnp.concatenate([b_router_group[l], b_router_expert[l]]),
                            (0, LANES - n_r)).reshape(1, LANES),
    }
    st_conv_t = state_conv[l].reshape(n_sample_tiles, seqs * (CONV_WIDTH - 1), CONV_DIM)
    st_shift_t = state_shift[l].reshape(n_sample_tiles, seqs, SHIFT_DIM)

    (yag, sgb, r, w, k, v, a, b, bonus, g, ctail, stail) = _mixer_pre(
        x_p, x_s, st_conv_t, st_shift_t, p, seq_len=dseq, tm=tm)

    s0_prompt = jnp.zeros((1, N_HEAD_GROUPS, GROUP_LANES, GROUP_LANES), F32)
    s0_sample = _state_to_kernel(state_wkv[l])
    rlkvab = (r, w, k, v, a, b)
    cp = min(SCAN_CHUNK, n_p)
    assert cp & (cp - 1) == 0 and dseq & (dseq - 1) == 0
    y_p, s_p = _wkv_scan(rlkvab, s0_prompt, row0=0, n_rows=n_p, chunk=cp,
                         chunks_per_step=SCAN_CHUNKS_PER_STEP, chained=True)
    y_s, s_s = _wkv_scan(rlkvab, s0_sample, row0=n_p, n_rows=n_s, chunk=dseq,
                         chunks_per_step=SCAN_CHUNKS_PER_STEP, chained=False)

    x1, h2, route, counts = _mixer_post(y_p, y_s, bonus, g, yag, sgb, x_p, x_s, p, tm=tm)

    blk = MOE_ROWS
    pos_tiles, block_expert, n_used, n_blocks = _dispatch(route, counts, blk, tm)
    xs = _moe_scatter(h2, pos_tiles, n_blocks * blk)
    yb = _moe_experts(xs, block_expert, n_used, w_e_gate[l], w_e_up[l], w_e_down[l], blk=blk)
    out_p, out_s = _moe_combine(pos_tiles, x1, route, normf_g.reshape(1, d), yb, n_prompt_rows=n_p, tm=tm)

    y_prompt = out_p.reshape(bp, seq, d)
    y_sample = out_s.reshape(db, dseq, d)
    conv_p = ctail[n_prompt_tiles - 1, 2 * (seqs - 1):2 * seqs].reshape(1, bp, CONV_WIDTH - 1, CONV_DIM)
    shift_p = stail[n_prompt_tiles - 1, seqs - 1].reshape(1, bp, 1, SHIFT_DIM)
    wkv_p = _state_from_kernel(s_p).reshape(1, bp, RWKV_HEADS, RWKV_HEAD, RWKV_HEAD)
    conv_s = ctail[n_prompt_tiles:].reshape(1, db, CONV_WIDTH - 1, CONV_DIM)
    shift_s = stail[n_prompt_tiles:].reshape(1, db, 1, SHIFT_DIM)
    wkv_s = _state_from_kernel(s_s).reshape(1, db, RWKV_HEADS, RWKV_HEAD, RWKV_HEAD)
    return (y_prompt, y_sample, conv_p, shift_p, wkv_p, conv_s, shift_s, wkv_s)
```

```python
import functools

import numpy as np
import jax
import jax.numpy as jnp
from jax import lax
from jax.experimental import pallas as pl
from jax.experimental.pallas import tpu as pltpu

F32 = jnp.float32
BF16 = jnp.bfloat16

CONV_DIM = 512
CONV_WIDTH = 3
RWKV_HEAD = 64
RWKV_HEADS = 8
RWKV_DIM = RWKV_HEADS * RWKV_HEAD
LORA_W = 64
LORA_A = 64
LORA_G = 128
SHIFT_DIM = 3 * RWKV_DIM + LORA_W + LORA_A + LORA_G
N_GROUPS = 4
EXPERTS_PER_GROUP = 8
N_EXPERTS = N_GROUPS * EXPERTS_PER_GROUP
TOP_K = 2
RMS_EPS = 1e-6
GN_EPS = 64e-5

LANES = 128
ROW_TILE = 256
SCAN_CHUNK = 64
SCAN_CHUNKS_PER_STEP = 4
MOE_ROWS = 256
HIST = 8
VMEM_LIMIT = 56 * 1024 * 1024


def _rms(x, g):
    return x * lax.rsqrt(jnp.mean(x * x, axis=-1, keepdims=True) + RMS_EPS) * g


def _split2_dot(x, ones_bf16):
    w = ones_bf16.shape[0]
    hi = x.astype(BF16)
    lo = (x - hi.astype(F32)).astype(BF16)
    parts = []
    for c0 in range(0, x.shape[1], w):
        both = jnp.concatenate([hi[:, c0:c0 + w], lo[:, c0:c0 + w]], axis=0)
        s = jnp.dot(both, ones_bf16, preferred_element_type=F32)
        parts.append(s[0:x.shape[0]] + s[x.shape[0]:])
    return jnp.concatenate(parts, axis=1)


def _pack_bf16_pairs(x):
    n = x.shape[1] // 2
    lo = pltpu.bitcast(x[:, :n].astype(BF16).astype(F32), jnp.uint32)
    hi = pltpu.bitcast(x[:, n:].astype(BF16).astype(F32), jnp.uint32)
    return hi | (lo >> 16)


def _unpack_bf16_pairs(p):
    lo = pltpu.bitcast(p << 16, F32)
    hi = pltpu.bitcast(p & jnp.uint32(0xFFFF0000), F32)
    return jnp.concatenate([lo, hi], axis=1)


def _stack3(w):
    hi = w.astype(BF16)
    lo = (w - hi.astype(F32)).astype(BF16)
    return jnp.concatenate([hi, hi, lo], axis=0)


def _dot_stack3(x, w_stack):
    hi = x.astype(BF16)
    lo = (x - hi.astype(F32)).astype(BF16)
    return jnp.dot(jnp.concatenate([hi, lo, hi], axis=1), w_stack, preferred_element_type=F32)


def _const_spec(shape):
    nd = len(shape)
    return pl.BlockSpec(shape, lambda *_: (0,) * nd)


def _mixer_pre_kernel(xp_ref, xs_ref, stc_ref, sts_ref, n1_ref, wa_ref, wb_ref, wg_ref, convw_ref, mu_ref,
                      w0_ref, lw_ref, a0_ref, la_ref, lgw_ref, kk_ref, ka_ref, rk_ref, woa_ref,
                      ones_ref,
                      yag_ref, sgb_ref, r_ref, w_ref, k_ref, v_ref, a_ref, b_ref, bonus_ref, g_ref,
                      ctail_ref, stail_ref,
                      cbuf, sbuf, p1buf, p2buf, spbuf, *, n_prompt_tiles, seq_len):
    i = pl.program_id(0)
    tm = xp_ref.shape[0]
    seqs = tm // seq_len

    @pl.when(i == 0)
    def _():
        cbuf[0:HIST, :] = jnp.zeros((HIST, CONV_DIM), F32)
        sbuf[0:HIST, :] = jnp.zeros((HIST, SHIFT_DIM), F32)

    x = jnp.where(i < n_prompt_tiles, xp_ref[...], xs_ref[...])
    h = _rms(x, n1_ref[...]).astype(BF16)

    pa = jnp.dot(h, wa_ref[...], preferred_element_type=F32)
    g_in = pa[:, 0:CONV_DIM]
    g_out = pa[:, CONV_DIM:2 * CONV_DIM]
    x_c = pa[:, 2 * CONV_DIM:3 * CONV_DIM]
    bx = g_in * x_c
    cbuf[HIST:HIST + tm, :] = bx
    p1buf[...] = cbuf[HIST - 1:HIST - 1 + tm, :]
    p2buf[...] = cbuf[HIST - 2:HIST - 2 + tm, :]

    @pl.when(i >= n_prompt_tiles)
    def _():
        for j in range(seqs):
            r0 = j * seq_len
            p1buf[r0:r0 + 1, :] = stc_ref[0, 2 * j + 1:2 * j + 2, :]
            p2buf[r0:r0 + 1, :] = stc_ref[0, 2 * j:2 * j + 1, :]
            p2buf[r0 + 1:r0 + 2, :] = stc_ref[0, 2 * j + 1:2 * j + 2, :]

    cw = convw_ref[...]
    conv = cw[0:1, :] * p2buf[...] + cw[1:2, :] * p1buf[...] + cw[2:3, :] * bx
    y_a = jnp.dot((g_out * conv).astype(BF16), woa_ref[...], preferred_element_type=F32)
    for j in range(seqs):
        r1 = HIST + (j + 1) * seq_len
        ctail_ref[0, 2 * j:2 * j + 2, :] = cbuf[r1 - 2:r1, :]
    cbuf[HIST - 2:HIST, :] = cbuf[HIST + tm - 2:HIST + tm, :]

    pg = jnp.dot(h, wg_ref[...], preferred_element_type=F32)
    d = pg.shape[1] // 2
    yag_ref[...] = jax.nn.sigmoid(pg[:, 0:d]) * y_a
    sgb_ref[...] = jax.nn.sigmoid(pg[:, d:2 * d])

    pb = jnp.dot(h, wb_ref[...], preferred_element_type=F32)
    sbuf[HIST:HIST + tm, :] = pb
    spbuf[...] = sbuf[HIST - 1:HIST - 1 + tm, :]

    @pl.when(i >= n_prompt_tiles)
    def _():
        for j in range(seqs):
            r0 = j * seq_len
            spbuf[r0:r0 + 1, :] = sts_ref[0, j:j + 1, :]

    for j in range(seqs):
        r1 = HIST + (j + 1) * seq_len
        stail_ref[0, j:j + 1, :] = sbuf[r1 - 1:r1, :]
    sbuf[HIST - 1:HIST, :] = sbuf[HIST + tm - 1:HIST + tm, :]

    s = pb + (spbuf[...] - pb) * mu_ref[...]
    o1, o2, o3 = RWKV_DIM, 2 * RWKV_DIM, 3 * RWKV_DIM
    r = s[:, 0:o1]
    k = s[:, o1:o2]
    v = s[:, o2:o3]
    s_l = s[:, o3:o3 + LORA_W + LORA_A]
    lg = s[:, o3 + LORA_W + LORA_A:]
    z = w0_ref[...] + _dot_stack3(jnp.tanh(s_l), lw_ref[...])
    w_log = -jax.nn.softplus(-z) - 0.5
    log_decay = -jnp.exp(w_log)
    a = jax.nn.sigmoid(a0_ref[...] + _dot_stack3(s_l, la_ref[...]))
    g = _dot_stack3(jax.nn.sigmoid(lg), lgw_ref[...])

    ones = ones_ref[...]
    kk = k * kk_ref[...]
    kk_n = kk / jnp.maximum(jnp.sqrt(_split2_dot(kk * kk, ones)), 1e-12)
    k2 = k * (1.0 + (a - 1.0) * ka_ref[...])
    bonus = _split2_dot(r * k2 * rk_ref[...], ones) * v

    r_ref[...] = r
    w_ref[...] = log_decay
    k_ref[...] = k2
    v_ref[...] = v
    a_ref[...] = -kk_n
    b_ref[...] = kk_n * a
    bonus_ref[...] = bonus
    g_ref[...] = g


def _mixer_pre(x_p, x_s, st_conv_t, st_shift_t, p, *, seq_len, tm):
    n_p, d = x_p.shape
    n_tok = n_p + x_s.shape[0]
    n_prompt_tiles = n_p // tm
    n_tiles = n_tok // tm
    seqs = tm // seq_len
    row = lambda w: pl.BlockSpec((tm, w), lambda i: (i, 0))
    st_idx = lambda i: (jnp.maximum(i - n_prompt_tiles, 0), 0, 0)
    consts = [p['norm1_g'], p['w_in_a'], p['w_in_b'], p['w_in_g'], p['conv_w'], p['mu_shift'],
              p['w0'], p['w_lora_w'], p['a0'], p['w_lora_a'], p['w_lora_g'], p['k_k'], p['k_a'],
              p['r_k'], p['w_out_a'], p['ones_bf16']]
    in_specs = [pl.BlockSpec((tm, d), lambda i: (jnp.minimum(i, n_prompt_tiles - 1), 0)),
                pl.BlockSpec((tm, d), lambda i: (jnp.maximum(i - n_prompt_tiles, 0), 0)),
                pl.BlockSpec((1, 2 * seqs, CONV_DIM), st_idx),
                pl.BlockSpec((1, seqs, SHIFT_DIM), st_idx)] + [_const_spec(c.shape) for c in consts]
    sds = lambda w: jax.ShapeDtypeStruct((n_tok, w), F32)
    out_shape = [sds(d), sds(d)] + [sds(RWKV_DIM)] * 8 + [
        jax.ShapeDtypeStruct((n_tiles, 2 * seqs, CONV_DIM), F32),
        jax.ShapeDtypeStruct((n_tiles, seqs, SHIFT_DIM), F32)]
    out_specs = [row(d), row(d)] + [row(RWKV_DIM)] * 8 + [
        pl.BlockSpec((1, 2 * seqs, CONV_DIM), lambda i: (i, 0, 0)),
        pl.BlockSpec((1, seqs, SHIFT_DIM), lambda i: (i, 0, 0))]
    kern = functools.partial(_mixer_pre_kernel, n_prompt_tiles=n_prompt_tiles, seq_len=seq_len)
    return pl.pallas_call(
        kern, out_shape=out_shape, grid=(n_tiles,), in_specs=in_specs, out_specs=out_specs,
        scratch_shapes=[pltpu.VMEM((tm + HIST, CONV_DIM), F32), pltpu.VMEM((tm + HIST, SHIFT_DIM), F32),
                        pltpu.VMEM((tm, CONV_DIM), F32), pltpu.VMEM((tm, CONV_DIM), F32),
                        pltpu.VMEM((tm, SHIFT_DIM), F32)],
        compiler_params=pltpu.CompilerParams(dimension_semantics=("arbitrary",),
                                             vmem_limit_bytes=VMEM_LIMIT),
        name="mixer_pre")(x_p, x_s, st_conv_t, st_shift_t, *consts)


GROUP_LANES = 256
HEADS_PER_GROUP = GROUP_LANES // RWKV_HEAD
N_HEAD_GROUPS = RWKV_DIM // GROUP_LANES
NN = (((1,), (0,)), ((), ()))
NT = (((1,), (1,)), ((), ()))


def _split2(x):
    hi = x.astype(BF16)
    lo = (x - hi.astype(F32)).astype(BF16)
    return hi, lo


def _mm(xs, ys, dims=NN):
    x1, x2 = xs
    y1, y2 = ys
    d = lambda p, q: lax.dot_general(p, q, dims, preferred_element_type=F32)
    m = x1.shape[0]
    both = d(jnp.concatenate([x1, x2], axis=0), y1)
    return both[0:m] + both[m:2 * m] + d(x1, y2)


def _cat2(ps, qs, axis):
    return tuple(jnp.concatenate([p, q], axis=axis) for p, q in zip(ps, qs))


def _wkv_masks(c):
    hc = HEADS_PER_GROUP * c
    levels = c.bit_length() - 1
    t = np.arange(c)[:, None]
    s = (np.arange(hc) % c)[None, :]
    tm = [s < t, s <= t]
    for lvl in range(1, levels + 1):
        half = 1 << (lvl - 1)
        tm.append(((t >> lvl) == (s >> lvl)) & ((t & half) != 0) & ((s & half) == 0))
    row_head = (np.arange(hc) // c)[:, None]
    mfeat = row_head == (np.arange(GROUP_LANES) // RWKV_HEAD)[None, :]
    mpos = row_head == (np.arange(hc) // c)[None, :]
    lane_head = np.arange(GROUP_LANES) // RWKV_HEAD
    stmask = lane_head[:, None] == lane_head[None, :]
    tri = np.arange(c)[None, :] <= np.arange(c)[:, None]
    return (jnp.asarray(mfeat, BF16), jnp.asarray(mpos, BF16), jnp.asarray(np.stack(tm), F32),
            jnp.asarray(stmask, F32), jnp.asarray(tri, BF16))


def _interleave(*gens):
    live = list(gens)
    while live:
        for gen in list(live):
            try:
                next(gen)
            except StopIteration:
                live.remove(gen)


def _wkv_pipe_kernel(r_ref, lw_ref, k_ref, v_ref, a_ref, b_ref, s0_ref, mfeat_ref, mpos_ref,
                     tmask_ref, stmask_ref, tri_ref, y_ref, sout_ref,
                     state, sv_ar, sv_inv, sv_akv, sv_arbk, sv_vbd, sv_v, sv_bk, sv_pend,
                     *, chunk, n_steps, chained):
    i = pl.program_id(0)
    c = chunk
    n_chunks = r_ref.shape[0] // c
    hc = HEADS_PER_GROUP * c
    levels = c.bit_length() - 1
    groups = range(N_HEAD_GROUPS)
    units = [(j, g) for j in range(n_chunks) for g in groups]
    uid = {u: n for n, u in enumerate(units)}

    stmask = stmask_ref[...]

    def expand(sc):
        pair = jnp.concatenate([sc, sc], axis=1)
        return jnp.concatenate([pair] * (GROUP_LANES // pair.shape[1]), axis=1) * stmask

    def compact(s):
        half = s[:, 0:GROUP_LANES // 2] + s[:, GROUP_LANES // 2:]
        return half[:, 0:RWKV_HEAD] + half[:, RWKV_HEAD:]

    @pl.when(i == 0)
    def _():
        for ref in (sv_ar, sv_inv, sv_akv, sv_arbk, sv_vbd, sv_v, sv_bk):
            ref[...] = jnp.zeros(ref.shape, ref.dtype)
        sv_pend[...] = jnp.ones(sv_pend.shape, F32)
        if chained:
            for g in groups:
                state[g] = expand(s0_ref[0, g])

    mfeat = mfeat_ref[...]
    mpos = mpos_ref[...]
    strict = tmask_ref[0]
    incl = tmask_ref[1]
    eye = incl - strict
    tri = tri_ref[...]

    def ld(ref, j, g):
        return ref[j * c:(j + 1) * c, g * GROUP_LANES:(g + 1) * GROUP_LANES]

    def bd_split(ps):
        mask = mpos if ps[0].shape[1] == hc else mfeat
        return tuple(jnp.concatenate([p] * HEADS_PER_GROUP, axis=0) * mask for p in ps)

    def bd2(m):
        return bd_split(_split2(m))

    def cumsum_rows(x):
        p1 = x.astype(BF16)
        r1 = x - p1.astype(F32)
        p2 = r1.astype(BF16)
        p3 = (r1 - p2.astype(F32)).astype(BF16)
        d = lambda q: jnp.dot(tri, q, preferred_element_type=F32)
        return d(p1) + d(p2) + d(p3)

    new = {}

    def prepare():
        cum = {u: cumsum_rows(ld(lw_ref, *u)) for u in units}
        yield
        ar, bk_end, p_end, a_ab, a_ak, a_rb, a_rk, v, vbd = ({} for _ in range(9))
        for u in units:
            cm = cum[u]
            cum_last = cm[c - 1:c, :]
            e_neg = jnp.exp(-cm)
            e_end = jnp.exp(cum_last - cm)
            b_raw = ld(b_ref, *u)
            k_raw = ld(k_ref, *u)
            ar[u] = _split2(jnp.concatenate([ld(a_ref, *u) * jnp.exp(cm - ld(lw_ref, *u)),
                                             ld(r_ref, *u) * jnp.exp(cm)], axis=0))
            bk_end[u] = _split2(jnp.concatenate([b_raw * e_end, k_raw * e_end], axis=0))
            p_end[u] = jnp.exp(cum_last)
            v[u] = ld(v_ref, *u)
            vbd[u] = bd2(v[u])
            gram = _mm(ar[u], _cat2(bd2(b_raw * e_neg), bd2(k_raw * e_neg), 0), NT)
            a_ab[u] = jnp.where(strict > 0, gram[0:c, 0:hc], 0.0)
            a_ak[u] = jnp.where(strict > 0, gram[0:c, hc:2 * hc], 0.0)
            a_rb[u] = jnp.where(incl > 0, gram[c:2 * c, 0:hc], 0.0)
            a_rk[u] = jnp.where(incl > 0, gram[c:2 * c, hc:2 * hc], 0.0)
            if uid[u] % 2 == 1:
                yield
        a_ab2 = {u: _split2(a_ab[u]) for u in units}
        inv = {u: eye + a_ab[u] * tmask_ref[2] for u in units}
        inv2 = {u: _split2(inv[u]) for u in units}
        for lvl in range(2, levels + 1):
            lm = tmask_ref[1 + lvl]
            t1 = {u: _mm(a_ab2[u], bd_split(inv2[u])) for u in units}
            yield
            inv = {u: inv[u] + lm * _mm(inv2[u], bd2(t1[u])) for u in units}
            inv2 = {u: _split2(inv[u]) for u in units}
            yield
        akv = {u: _mm(_split2(a_ak[u]), vbd[u]) for u in units}
        arbk = {u: _split2(jnp.concatenate([a_rb[u], a_rk[u]], axis=1)) for u in units}
        new.update(ar=ar, inv=inv2, akv=akv, arbk=arbk, vbd=vbd, v=v, bk=bk_end, pend=p_end)

    def serial():
        pair = lambda ref, n: (ref[0, n], ref[1, n])
        s_cur = [state[g] for g in groups] if chained else None
        for j in range(n_chunks):
            ns = [uid[j, g] for g in groups]
            s_prev = s_cur if chained else [expand(s0_ref[j, g]) for g in groups]
            x0 = [_mm(pair(sv_ar, n), _split2(s_prev[g]), NT) for g, n in zip(groups, ns)]
            yield
            uu = [_mm(pair(sv_inv, n), bd2(x0[g][0:c] + sv_akv[n])) for g, n in zip(groups, ns)]
            yield
            for g, n in zip(groups, ns):
                yy = x0[g][c:2 * c] + _mm(pair(sv_arbk, n), _cat2(bd2(uu[g]), pair(sv_vbd, n), 0))
                y_ref[j * c:(j + 1) * c, g * GROUP_LANES:(g + 1) * GROUP_LANES] = yy
            s_new = []
            for g, n in zip(groups, ns):
                uv_t = jnp.transpose(jnp.concatenate([uu[g], sv_v[n]], axis=0))
                upd = _mm(_split2(uv_t), pair(sv_bk, n))
                s_new.append(s_prev[g] * sv_pend[n, 0:1, :] + stmask * upd)
            yield
            if chained:
                s_cur = s_new
            else:
                for g in groups:
                    sout_ref[j, g] = compact(s_new[g])
        if chained:
            for g in groups:
                state[g] = s_cur[g]
        new['last_state'] = s_new

    _interleave(serial(), prepare())

    bits = sum(pltpu.bitcast(s[0:8, 0:LANES], jnp.uint32) for s in new['last_state'])
    zero = pltpu.bitcast(lax.shift_right_logical(bits, jnp.uint32(32)), F32)[0:1, 0:1]
    zero_bf = zero.astype(BF16)
    for u, n in uid.items():
        for name, ref in (('ar', sv_ar), ('inv', sv_inv), ('arbk', sv_arbk), ('vbd', sv_vbd), ('bk', sv_bk)):
            for half in range(2):
                ref[half, n] = new[name][u][half] + zero_bf
        sv_akv[n] = new['akv'][u] + zero
        sv_v[n] = new['v'][u] + zero
        sv_pend[n] = jnp.broadcast_to(new['pend'][u] + zero, sv_pend.shape[1:])

    if chained:
        @pl.when(i == n_steps)
        def _():
            for g in groups:
                sout_ref[0, g] = compact(state[g])


def _wkv_scan(rlkvab, s0, *, row0, n_rows, chunk, chunks_per_step, chained):
    rows = chunk * chunks_per_step
    n_steps = n_rows // rows
    assert n_rows % rows == 0 and row0 % rows == 0
    masks = _wkv_masks(chunk)
    hc = HEADS_PER_GROUP * chunk
    n_units = chunks_per_step * N_HEAD_GROUPS
    st = (N_HEAD_GROUPS, GROUP_LANES, RWKV_HEAD)
    prev = lambda i: jnp.maximum(i - 1, 0)
    row_in = pl.BlockSpec((rows, RWKV_DIM), lambda i: (row0 // rows + jnp.minimum(i, n_steps - 1), 0))
    if chained:
        n_state = 1
        st_spec = pl.BlockSpec((1,) + st, lambda i: (0, 0, 0, 0))
    else:
        n_state = n_rows // chunk
        st_spec = pl.BlockSpec((chunks_per_step,) + st, lambda i: (prev(i), 0, 0, 0))
    in_specs = [row_in] * 6 + [st_spec] + [_const_spec(m.shape) for m in masks]
    out_shape = [jax.ShapeDtypeStruct((n_rows, RWKV_DIM), F32),
                 jax.ShapeDtypeStruct((n_state,) + st, F32)]
    out_specs = [pl.BlockSpec((rows, RWKV_DIM), lambda i: (prev(i), 0)), st_spec]
    scratch = [pltpu.VMEM((N_HEAD_GROUPS, GROUP_LANES, GROUP_LANES), F32),
               pltpu.VMEM((2, n_units, 2 * chunk, GROUP_LANES), BF16),
               pltpu.VMEM((2, n_units, chunk, hc), BF16),
               pltpu.VMEM((n_units, chunk, GROUP_LANES), F32),
               pltpu.VMEM((2, n_units, chunk, 2 * hc), BF16),
               pltpu.VMEM((2, n_units, hc, GROUP_LANES), BF16),
               pltpu.VMEM((n_units, chunk, GROUP_LANES), F32),
               pltpu.VMEM((2, n_units, 2 * chunk, GROUP_LANES), BF16),
               pltpu.VMEM((n_units, 8, GROUP_LANES), F32)]
    kern = functools.partial(_wkv_pipe_kernel, chunk=chunk, n_steps=n_steps, chained=chained)
    return pl.pallas_call(
        kern, out_shape=out_shape, grid=(n_steps + 1,), in_specs=in_specs, out_specs=out_specs,
        scratch_shapes=scratch,
        compiler_params=pltpu.CompilerParams(dimension_semantics=("arbitrary",),
                                             vmem_limit_bytes=VMEM_LIMIT),
        name="wkv_scan")(*rlkvab, s0, *masks)


def _mixer_post_kernel(yp_ref, ys_ref, bonus_ref, g_ref, yag_ref, sgb_ref, xp_ref, xs_ref, lng_ref, lnb_ref,
                       wob_ref, wo_ref, n2_ref, wr_ref, br_ref, ones_ref, tril_ref,
                       x1_ref, h2_ref, route_ref, counts_ref, cnt, *, n_prompt_tiles):
    i = pl.program_id(0)

    @pl.when(i == 0)
    def _():
        cnt[...] = jnp.zeros(cnt.shape, F32)

    x = jnp.where(i < n_prompt_tiles, xp_ref[...], xs_ref[...])
    ones = ones_ref[...]
    y = jnp.where(i < n_prompt_tiles, yp_ref[...], ys_ref[...])
    inv_n = 1.0 / RWKV_HEAD
    mean = _split2_dot(y, ones) * inv_n
    yc = y - mean
    var = _split2_dot(yc * yc, ones) * inv_n
    yn = yc * lax.rsqrt(var + GN_EPS) * lng_ref[...] + lnb_ref[...]
    yy = (yn + bonus_ref[...]) * g_ref[...]
    y_b = jnp.dot(yy.astype(BF16), wob_ref[...], preferred_element_type=F32)
    merged = yag_ref[...] + sgb_ref[...] * y_b
    x1 = x + jnp.dot(merged.astype(BF16), wo_ref[...], preferred_element_type=F32)
    x1_ref[...] = x1
    h2 = _rms(x1, n2_ref[...])
    h2_ref[...] = _pack_bf16_pairs(h2)

    logits = _dot_stack3(h2, wr_ref[...]) + br_ref[...]
    lane = lax.broadcasted_iota(jnp.int32, logits.shape, 1)
    neg = jnp.float32(-jnp.inf)
    big = jnp.int32(LANES)
    is_g = lane < N_GROUPS
    lgp = jnp.where(is_g, logits, neg)
    m_g = jnp.max(lgp, axis=-1, keepdims=True)
    grp = jnp.min(jnp.where(lgp == m_g, lane, big), axis=-1, keepdims=True)
    p_top = 1.0 / jnp.sum(jnp.where(is_g, jnp.exp(logits - m_g), 0.0), axis=-1, keepdims=True)
    e_lane = lane - N_GROUPS
    in_grp = (e_lane >= grp * EXPERTS_PER_GROUP) & (e_lane < (grp + 1) * EXPERTS_PER_GROUP)
    le = jnp.where(in_grp, logits, neg)
    m1 = jnp.max(le, axis=-1, keepdims=True)
    i1 = jnp.min(jnp.where(le == m1, lane, big), axis=-1, keepdims=True)
    le2 = jnp.where(lane == i1, neg, le)
    m2 = jnp.max(le2, axis=-1, keepdims=True)
    i2 = jnp.min(jnp.where(le2 == m2, lane, big), axis=-1, keepdims=True)
    ex = jnp.exp(m2 - m1)
    p1 = 1.0 / (1.0 + ex)
    p2 = ex / (1.0 + ex)
    oh1 = lane == i1 - N_GROUPS
    oh2 = lane == i2 - N_GROUPS
    both = jnp.where(oh1, 1.0, jnp.where(oh2, 1.0, 0.0))
    before = jnp.dot(tril_ref[...], both.astype(BF16), preferred_element_type=F32) + cnt[0:1, :]
    rank1 = jnp.sum(jnp.where(oh1, before, 0.0), axis=-1, keepdims=True)
    rank2 = jnp.sum(jnp.where(oh2, before, 0.0), axis=-1, keepdims=True)
    cnt[0:1, :] = cnt[0:1, :] + jnp.sum(both, axis=0, keepdims=True)
    counts_ref[...] = jnp.broadcast_to(cnt[0:1, :], counts_ref.shape)

    cols = [(i1 - N_GROUPS).astype(F32), (i2 - N_GROUPS).astype(F32), p_top * p1, p_top * p2,
            rank1, rank2]
    route = jnp.zeros(logits.shape, F32)
    for c, col in enumerate(cols):
        route = jnp.where(lane == c, col, route)
    route_ref[...] = route


ROUTE_EXPERT, ROUTE_WEIGHT, ROUTE_RANK = 0, 2, 4


def _mixer_post(y_p, y_s, bonus, g, yag, sgb, x_p, x_s, p, *, tm):
    n_p, d = x_p.shape
    n_tok = n_p + x_s.shape[0]
    npt = n_p // tm
    row = lambda w: pl.BlockSpec((tm, w), lambda i: (i, 0))
    tril = jnp.asarray(np.arange(tm)[None, :] < np.arange(tm)[:, None], BF16)
    consts = [p['lnx_g'], p['lnx_b'], p['w_out_b'], p['w_o'], p['norm2_g'], p['w_router'],
              p['b_router'], p['ones_bf16'], tril]
    pair = lambda w: [pl.BlockSpec((tm, w), lambda i: (jnp.minimum(i, npt - 1), 0)),
                      pl.BlockSpec((tm, w), lambda i: (jnp.maximum(i - npt, 0), 0))]
    in_specs = (pair(RWKV_DIM) + [row(RWKV_DIM)] * 2 + [row(d)] * 2 + pair(d)
                + [_const_spec(c.shape) for c in consts])
    out_shape = [jax.ShapeDtypeStruct((n_tok, d), F32), jax.ShapeDtypeStruct((n_tok, d // 2), jnp.uint32),
                 jax.ShapeDtypeStruct((n_tok, LANES), F32), jax.ShapeDtypeStruct((8, LANES), F32)]
    out_specs = [row(d), row(d // 2), row(LANES), _const_spec((8, LANES))]
    return pl.pallas_call(
        functools.partial(_mixer_post_kernel, n_prompt_tiles=npt), out_shape=out_shape,
        grid=(n_tok // tm,), in_specs=in_specs, out_specs=out_specs,
        scratch_shapes=[pltpu.VMEM((8, LANES), F32)],
        compiler_params=pltpu.CompilerParams(dimension_semantics=("arbitrary",),
                                             vmem_limit_bytes=VMEM_LIMIT),
        name="mixer_post")(y_p, y_s, bonus, g, yag, sgb, x_p, x_s, *consts)


def _gather_rows(idx_ref, src_hbm, dst, sem, n_rows, *, unrolled):
    def start(r):
        pltpu.make_async_copy(src_hbm.at[pl.ds(idx_ref[r], 1)], dst.at[pl.ds(r, 1)], sem).start()
    if unrolled:
        for r in range(n_rows):
            start(r)
    else:
        def body(r, carry):
            start(r)
            return carry
        lax.fori_loop(0, n_rows, body, 0)


def _wait_rows(src_hbm, dst, sem, n_rows):
    pltpu.make_async_copy(src_hbm.at[pl.ds(0, n_rows)], dst, sem).wait()


def _moe_scatter_kernel(dest_ref, h_ref, xs_in_hbm, xs_hbm, sem):
    del xs_in_hbm
    n = dest_ref.shape[-1]
    tm = h_ref.shape[0]
    for r in range(n):
        pltpu.make_async_copy(h_ref.at[pl.ds(r % tm, 1)],
                              xs_hbm.at[pl.ds(dest_ref[0, 0, r], 1)], sem).start()
    for j in range(n // tm):
        pltpu.make_async_copy(h_ref, xs_hbm.at[pl.ds(0, tm)], sem).wait()


def _moe_scatter(h2, pos3, n_rows):
    n_tok, d = h2.shape
    n_tiles, _, n = pos3.shape
    tm = n // TOP_K
    xs0 = jnp.zeros((n_rows, d), h2.dtype)
    return pl.pallas_call(
        _moe_scatter_kernel, out_shape=jax.ShapeDtypeStruct((n_rows, d), h2.dtype), grid=(n_tiles,),
        in_specs=[pl.BlockSpec((1, 1, n), lambda i: (i, 0, 0), memory_space=pltpu.SMEM),
                  pl.BlockSpec((tm, d), lambda i: (i, 0)), pl.BlockSpec(memory_space=pl.ANY)],
        out_specs=pl.BlockSpec(memory_space=pl.ANY),
        scratch_shapes=[pltpu.SemaphoreType.DMA(())], input_output_aliases={2: 0},
        compiler_params=pltpu.CompilerParams(dimension_semantics=("arbitrary",), has_side_effects=True,
                                             vmem_limit_bytes=VMEM_LIMIT),
        name="moe_scatter")(pos3, h2, xs0)


def _moe_experts_kernel(bexp_ref, nused_ref, x_ref, wg_ref, wu_ref, wd_ref, yb_ref, wg_bf, wu_bf, wd_bf):
    i = pl.program_id(0)

    @pl.when((i == 0) | (bexp_ref[i] != bexp_ref[jnp.maximum(i - 1, 0)]))
    def _():
        wg_bf[...] = wg_ref[0].astype(BF16)
        wu_bf[...] = wu_ref[0].astype(BF16)
        wd_bf[...] = wd_ref[0].astype(BF16)

    @pl.when(i < nused_ref[0])
    def _():
        xb = _unpack_bf16_pairs(x_ref[...]).astype(BF16)
        hg = jnp.dot(xb, wg_bf[...], preferred_element_type=F32)
        hu = jnp.dot(xb, wu_bf[...], preferred_element_type=F32)
        hid = (hg * jax.nn.sigmoid(hg)) * hu
        yb_ref[...] = _pack_bf16_pairs(jnp.dot(hid.astype(BF16), wd_bf[...], preferred_element_type=F32))

    @pl.when(i >= nused_ref[0])
    def _():
        yb_ref[...] = jnp.zeros(yb_ref.shape, yb_ref.dtype)


def _moe_experts(xs, block_expert, n_used, w_eg, w_eu, w_ed, *, blk):
    n_rows, dp = xs.shape
    n_blocks = n_rows // blk
    d, de = w_eg.shape[1:]
    grid_spec = pltpu.PrefetchScalarGridSpec(
        num_scalar_prefetch=2, grid=(n_blocks,),
        in_specs=[
            pl.BlockSpec((blk, dp), lambda i, be, nu: (jnp.minimum(i, nu[0] - 1), 0)),
            pl.BlockSpec((1, d, de), lambda i, be, nu: (be[i], 0, 0)),
            pl.BlockSpec((1, d, de), lambda i, be, nu: (be[i], 0, 0)),
            pl.BlockSpec((1, de, d), lambda i, be, nu: (be[i], 0, 0)),
        ],
        out_specs=pl.BlockSpec((blk, dp), lambda i, be, nu: (i, 0)),
        scratch_shapes=[pltpu.VMEM((d, de), BF16), pltpu.VMEM((d, de), BF16), pltpu.VMEM((de, d), BF16)])
    return pl.pallas_call(
        _moe_experts_kernel, out_shape=jax.ShapeDtypeStruct((n_rows, dp), xs.dtype), grid_spec=grid_spec,
        compiler_params=pltpu.CompilerParams(dimension_semantics=("arbitrary",),
                                             vmem_limit_bytes=VMEM_LIMIT),
        name="moe_experts")(block_expert, n_used, xs, w_eg, w_eu, w_ed)


GATHER_AHEAD = 2


def _with_dummy_blocks(idx, n_blocks, blk):
    pad = jnp.zeros((GATHER_AHEAD * blk,), idx.dtype)
    return jnp.concatenate([idx, pad]).reshape(n_blocks + GATHER_AHEAD, 1, blk)


def _moe_combine_kernel(*refs, n_prompt_tiles):
    pos_refs = refs[:GATHER_AHEAD + 1]
    x1_ref, route_ref, nf_ref, yb_hbm, outp_ref, outs_ref, ybuf, sems = refs[GATHER_AHEAD + 1:]
    i = pl.program_id(0)
    n_slots = GATHER_AHEAD + 1
    rows = ybuf.shape[1]
    slot = i % n_slots

    @pl.when(i == 0)
    def _():
        for a in range(GATHER_AHEAD):
            _gather_rows(pos_refs[a].at[0, 0], yb_hbm, ybuf.at[a], sems.at[a], rows, unrolled=False)

    _wait_rows(yb_hbm, ybuf.at[slot], sems.at[slot], rows)
    ahead = (i + GATHER_AHEAD) % n_slots
    _gather_rows(pos_refs[GATHER_AHEAD].at[0, 0], yb_hbm, ybuf.at[ahead], sems.at[ahead], rows, unrolled=True)
    tm = x1_ref.shape[0]
    route = route_ref[...]
    x2 = x1_ref[...]
    for j in range(TOP_K):
        yj = _unpack_bf16_pairs(ybuf[slot, j * tm:(j + 1) * tm, :])
        x2 = x2 + yj * route[:, ROUTE_WEIGHT + j:ROUTE_WEIGHT + j + 1]
    out = _rms(x2, nf_ref[...])

    @pl.when(i < n_prompt_tiles)
    def _():
        outp_ref[...] = out

    @pl.when(i >= n_prompt_tiles)
    def _():
        outs_ref[...] = out

    @pl.when(i == pl.num_programs(0) - 1)
    def _():
        for a in range(1, n_slots):
            s = (i + a) % n_slots
            _wait_rows(yb_hbm, ybuf.at[s], sems.at[s], rows)


def _moe_combine(pos_tiles, x1, route, normf_g, yb, *, n_prompt_rows, tm):
    n_tok, d = x1.shape
    n_tiles = n_tok // tm
    npt = n_prompt_rows // tm
    pos3 = _with_dummy_blocks(pos_tiles.reshape(-1), n_tiles, TOP_K * tm)
    pos_spec = lambda a: pl.BlockSpec((1, 1, TOP_K * tm), lambda i: (i + a, 0, 0), memory_space=pltpu.SMEM)
    in_specs = [pos_spec(a) for a in range(GATHER_AHEAD + 1)] + [
        pl.BlockSpec((tm, d), lambda i: (i, 0)),
        pl.BlockSpec((tm, LANES), lambda i: (i, 0)),
        _const_spec(normf_g.shape),
        pl.BlockSpec(memory_space=pl.ANY),
    ]
    out_shape = [jax.ShapeDtypeStruct((n_prompt_rows, d), F32),
                 jax.ShapeDtypeStruct((n_tok - n_prompt_rows, d), F32)]
    out_specs = [pl.BlockSpec((tm, d), lambda i: (jnp.minimum(i, npt - 1), 0)),
                 pl.BlockSpec((tm, d), lambda i: (jnp.maximum(i - npt, 0), 0))]
    return pl.pallas_call(
        functools.partial(_moe_combine_kernel, n_prompt_tiles=npt), out_shape=out_shape,
        grid=(n_tiles,), in_specs=in_specs, out_specs=out_specs,
        scratch_shapes=[pltpu.VMEM((GATHER_AHEAD + 1, TOP_K * tm, yb.shape[1]), yb.dtype),
                        pltpu.SemaphoreType.DMA((GATHER_AHEAD + 1,))],
        compiler_params=pltpu.CompilerParams(dimension_semantics=("arbitrary",),
                                             vmem_limit_bytes=VMEM_LIMIT),
        name="moe_combine")(*([pos3] * (GATHER_AHEAD + 1)), x1, route, normf_g, yb)


def _dispatch(route, counts, blk, tm):
    n_tok = route.shape[0]
    n_assign = n_tok * TOP_K
    expert = route[:, ROUTE_EXPERT:ROUTE_EXPERT + TOP_K].astype(jnp.int32)
    rank = route[:, ROUTE_RANK:ROUTE_RANK + TOP_K].astype(jnp.int32)
    counts = counts[0, :N_EXPERTS].astype(jnp.int32)
    padded = (counts + blk - 1) // blk * blk
    pad_end = jnp.cumsum(padded)
    pad_start = pad_end - padded
    is_e = expert[:, :, None] == jnp.arange(N_EXPERTS, dtype=jnp.int32)
    dest = jnp.sum(jnp.where(is_e, pad_start, 0), axis=-1) + rank
    n_blocks = -(-n_assign // blk) + N_EXPERTS
    block_start = jnp.arange(n_blocks, dtype=jnp.int32) * blk
    block_expert = jnp.minimum(jnp.sum(block_start[:, None] >= pad_end[None, :], axis=1),
                               N_EXPERTS - 1).astype(jnp.int32)
    n_used = (pad_end[-1] // blk).reshape(1)
    pos_tiles = dest.reshape(n_tok // tm, tm, TOP_K).transpose(0, 2, 1).reshape(n_tok // tm, 1, TOP_K * tm)
    return pos_tiles, block_expert, n_used, n_blocks


def _state_to_kernel(s):
    return s.reshape(s.shape[0], N_HEAD_GROUPS, GROUP_LANES, RWKV_HEAD)


def _state_from_kernel(s):
    return s.reshape(s.shape[0], RWKV_HEADS, RWKV_HEAD, RWKV_HEAD)


def kernel(x_prompt, x_sample, state_conv, state_shift, state_wkv, norm1_g, w_in, conv_w, mu_shift, w0, w_lora_w, a0, w_lora_a, w_lora_g, k_k, k_a, r_k, lnx_g, lnx_b, w_out_a, w_out_b, w_o, norm2_g, w_router_group, b_router_group, w_router_expert, b_router_expert, w_e_gate, w_e_up, w_e_down, normf_g):
    depth = norm1_g.shape[0]
    bp, seq, d = x_prompt.shape
    db, dseq, _ = x_sample.shape
    assert depth == 1 and bp == 1, "single layer, single prompt stream"
    tm = ROW_TILE
    n_p, n_s = bp * seq, db * dseq
    assert n_p % tm == 0 and n_s % tm == 0 and tm % dseq == 0
    n_prompt_tiles, n_sample_tiles = n_p // tm, n_s // tm
    seqs = tm // dseq
    n_tok = n_p + n_s

    x_p = x_prompt.reshape(n_p, d)
    x_s = x_sample.reshape(n_s, d)

    l = 0
    c3 = 3 * CONV_DIM
    head_id = np.arange(GROUP_LANES) // RWKV_HEAD
    ones_bf16 = jnp.asarray(head_id[:, None] == head_id[None, :], BF16)
    zpad = jnp.zeros((LORA_W, RWKV_DIM), F32)
    n_r = N_GROUPS + N_EXPERTS
    p = {
        'norm1_g': norm1_g[l].reshape(1, d),
        'w_in_a': w_in[l][:, :c3].astype(BF16),
        'w_in_b': w_in[l][:, c3:c3 + SHIFT_DIM].astype(BF16),
        'w_in_g': w_in[l][:, c3 + SHIFT_DIM:].astype(BF16),
        'conv_w': conv_w[l],
        'mu_shift': mu_shift[l].reshape(1, SHIFT_DIM),
        'w0': w0[l].reshape(1, RWKV_DIM),
        'w_lora_w': _stack3(jnp.concatenate([w_lora_w[l], zpad], axis=0)),
        'a0': a0[l].reshape(1, RWKV_DIM),
        'w_lora_a': _stack3(jnp.concatenate([zpad, w_lora_a[l]], axis=0)),
        'w_lora_g': _stack3(w_lora_g[l]),
        'k_k': k_k[l].reshape(1, RWKV_DIM),
        'k_a': k_a[l].reshape(1, RWKV_DIM),
        'r_k': r_k[l].reshape(1, RWKV_DIM),
        'w_out_a': w_out_a[l].astype(BF16),
        'ones_bf16': ones_bf16,
        'lnx_g': lnx_g[l].reshape(1, RWKV_DIM),
        'lnx_b': lnx_b[l].reshape(1, RWKV_DIM),
        'w_out_b': w_out_b[l].astype(BF16),
        'w_o': w_o[l].astype(BF16),
        'norm2_g': norm2_g[l].reshape(1, d),
        'w_router': _stack3(jnp.pad(jnp.concatenate([w_router_group[l], w_router_expert[l]], axis=1),
                                    ((0, 0), (0, LANES - n_r)))),
        'b_router': jnp.pad(jnp.concatenate([b_router_group[l], b_router_expert[l]]),
                            (0, LANES - n_r)).reshape(1, LANES),
    }
    st_conv_t = state_conv[l].reshape(n_sample_tiles, seqs * (CONV_WIDTH - 1), CONV_DIM)
    st_shift_t = state_shift[l].reshape(n_sample_tiles, seqs, SHIFT_DIM)

    (yag, sgb, r, w, k, v, a, b, bonus, g, ctail, stail) = _mixer_pre(
        x_p, x_s, st_conv_t, st_shift_t, p, seq_len=dseq, tm=tm)

    s0_prompt = jnp.zeros((1, N_HEAD_GROUPS, GROUP_LANES, RWKV_HEAD), F32)
    s0_sample = _state_to_kernel(state_wkv[l])
    rlkvab = (r, w, k, v, a, b)
    cp = min(SCAN_CHUNK, n_p)
    assert cp & (cp - 1) == 0 and dseq & (dseq - 1) == 0
    y_p, s_p = _wkv_scan(rlkvab, s0_prompt, row0=0, n_rows=n_p, chunk=cp,
                         chunks_per_step=SCAN_CHUNKS_PER_STEP, chained=True)
    y_s, s_s = _wkv_scan(rlkvab, s0_sample, row0=n_p, n_rows=n_s, chunk=dseq,
                         chunks_per_step=SCAN_CHUNKS_PER_STEP, chained=False)

    x1, h2, route, counts = _mixer_post(y_p, y_s, bonus, g, yag, sgb, x_p, x_s, p, tm=tm)

    blk = MOE_ROWS
    pos_tiles, block_expert, n_used, n_blocks = _dispatch(route, counts, blk, tm)
    xs = _moe_scatter(h2, pos_tiles, n_blocks * blk)
    yb = _moe_experts(xs, block_expert, n_used, w_e_gate[l], w_e_up[l], w_e_down[l], blk=blk)
    out_p, out_s = _moe_combine(pos_tiles, x1, route, normf_g.reshape(1, d), yb, n_prompt_rows=n_p, tm=tm)

    y_prompt = out_p.reshape(bp, seq, d)
    y_sample = out_s.reshape(db, dseq, d)
    conv_p = ctail[n_prompt_tiles - 1, 2 * (seqs - 1):2 * seqs].reshape(1, bp, CONV_WIDTH - 1, CONV_DIM)
    shift_p = stail[n_prompt_tiles - 1, seqs - 1].reshape(1, bp, 1, SHIFT_DIM)
    wkv_p = _state_from_kernel(s_p).reshape(1, bp, RWKV_HEADS, RWKV_HEAD, RWKV_HEAD)
    conv_s = ctail[n_prompt_tiles:].reshape(1, db, CONV_WIDTH - 1, CONV_DIM)
    shift_s = stail[n_prompt_tiles:].reshape(1, db, 1, SHIFT_DIM)
    wkv_s = _state_from_kernel(s_s).reshape(1, db, RWKV_HEADS, RWKV_HEAD, RWKV_HEAD)
    return (y_prompt, y_sample, conv_p, shift_p, wkv_p, conv_s, shift_s, wkv_s)
```

```python
import functools

import numpy as np
import jax
import jax.numpy as jnp
from jax import lax
from jax.experimental import pallas as pl
from jax.experimental.pallas import tpu as pltpu

F32 = jnp.float32
BF16 = jnp.bfloat16

CONV_DIM = 512
CONV_WIDTH = 3
RWKV_HEAD = 64
RWKV_HEADS = 8
RWKV_DIM = RWKV_HEADS * RWKV_HEAD
LORA_W = 64
LORA_A = 64
LORA_G = 128
SHIFT_DIM = 3 * RWKV_DIM + LORA_W + LORA_A + LORA_G
N_GROUPS = 4
EXPERTS_PER_GROUP = 8
N_EXPERTS = N_GROUPS * EXPERTS_PER_GROUP
TOP_K = 2
RMS_EPS = 1e-6
GN_EPS = 64e-5

LANES = 128
ROW_TILE = 256
SCAN_CHUNK = 64
SCAN_CHUNKS_PER_STEP = 4
MOE_ROWS = 256
HIST = 8
VMEM_LIMIT = 56 * 1024 * 1024


def _rms(x, g):
    return x * lax.rsqrt(jnp.mean(x * x, axis=-1, keepdims=True) + RMS_EPS) * g


def _split2_dot(x, ones_bf16):
    w = ones_bf16.shape[0]
    hi = x.astype(BF16)
    lo = (x - hi.astype(F32)).astype(BF16)
    parts = []
    for c0 in range(0, x.shape[1], w):
        both = jnp.concatenate([hi[:, c0:c0 + w], lo[:, c0:c0 + w]], axis=0)
        s = jnp.dot(both, ones_bf16, preferred_element_type=F32)
        parts.append(s[0:x.shape[0]] + s[x.shape[0]:])
    return jnp.concatenate(parts, axis=1)


def _pack_bf16_pairs(x):
    n = x.shape[1] // 2
    lo = pltpu.bitcast(x[:, :n].astype(BF16).astype(F32), jnp.uint32)
    hi = pltpu.bitcast(x[:, n:].astype(BF16).astype(F32), jnp.uint32)
    return hi | (lo >> 16)


def _unpack_bf16_pairs(p):
    lo = pltpu.bitcast(p << 16, F32)
    hi = pltpu.bitcast(p & jnp.uint32(0xFFFF0000), F32)
    return jnp.concatenate([lo, hi], axis=1)


def _stack3(w):
    hi = w.astype(BF16)
    lo = (w - hi.astype(F32)).astype(BF16)
    return jnp.concatenate([hi, hi, lo], axis=0)


def _dot_stack3(x, w_stack):
    hi = x.astype(BF16)
    lo = (x - hi.astype(F32)).astype(BF16)
    return jnp.dot(jnp.concatenate([hi, lo, hi], axis=1), w_stack, preferred_element_type=F32)


def _const_spec(shape):
    nd = len(shape)
    return pl.BlockSpec(shape, lambda *_: (0,) * nd)


def _mixer_pre_kernel(xp_ref, xs_ref, stc_ref, sts_ref, n1_ref, wa_ref, wb_ref, wg_ref, convw_ref, mu_ref,
                      w0_ref, lw_ref, a0_ref, la_ref, lgw_ref, kk_ref, ka_ref, rk_ref, woa_ref,
                      ones_ref,
                      yag_ref, sgb_ref, r_ref, w_ref, k_ref, v_ref, a_ref, b_ref, bonus_ref, g_ref,
                      ctail_ref, stail_ref,
                      cbuf, sbuf, p1buf, p2buf, spbuf, *, n_prompt_tiles, seq_len):
    i = pl.program_id(0)
    tm = xp_ref.shape[0]
    seqs = tm // seq_len

    @pl.when(i == 0)
    def _():
        cbuf[0:HIST, :] = jnp.zeros((HIST, CONV_DIM), F32)
        sbuf[0:HIST, :] = jnp.zeros((HIST, SHIFT_DIM), F32)

    x = jnp.where(i < n_prompt_tiles, xp_ref[...], xs_ref[...])
    h = _rms(x, n1_ref[...]).astype(BF16)

    pa = jnp.dot(h, wa_ref[...], preferred_element_type=F32)
    g_in = pa[:, 0:CONV_DIM]
    g_out = pa[:, CONV_DIM:2 * CONV_DIM]
    x_c = pa[:, 2 * CONV_DIM:3 * CONV_DIM]
    bx = g_in * x_c
    cbuf[HIST:HIST + tm, :] = bx
    p1buf[...] = cbuf[HIST - 1:HIST - 1 + tm, :]
    p2buf[...] = cbuf[HIST - 2:HIST - 2 + tm, :]

    @pl.when(i >= n_prompt_tiles)
    def _():
        for j in range(seqs):
            r0 = j * seq_len
            p1buf[r0:r0 + 1, :] = stc_ref[0, 2 * j + 1:2 * j + 2, :]
            p2buf[r0:r0 + 1, :] = stc_ref[0, 2 * j:2 * j + 1, :]
            p2buf[r0 + 1:r0 + 2, :] = stc_ref[0, 2 * j + 1:2 * j + 2, :]

    cw = convw_ref[...]
    conv = cw[0:1, :] * p2buf[...] + cw[1:2, :] * p1buf[...] + cw[2:3, :] * bx
    y_a = jnp.dot((g_out * conv).astype(BF16), woa_ref[...], preferred_element_type=F32)
    for j in range(seqs):
        r1 = HIST + (j + 1) * seq_len
        ctail_ref[0, 2 * j:2 * j + 2, :] = cbuf[r1 - 2:r1, :]
    cbuf[HIST - 2:HIST, :] = cbuf[HIST + tm - 2:HIST + tm, :]

    pg = jnp.dot(h, wg_ref[...], preferred_element_type=F32)
    d = pg.shape[1] // 2
    yag_ref[...] = jax.nn.sigmoid(pg[:, 0:d]) * y_a
    sgb_ref[...] = jax.nn.sigmoid(pg[:, d:2 * d])

    pb = jnp.dot(h, wb_ref[...], preferred_element_type=F32)
    sbuf[HIST:HIST + tm, :] = pb
    spbuf[...] = sbuf[HIST - 1:HIST - 1 + tm, :]

    @pl.when(i >= n_prompt_tiles)
    def _():
        for j in range(seqs):
            r0 = j * seq_len
            spbuf[r0:r0 + 1, :] = sts_ref[0, j:j + 1, :]

    for j in range(seqs):
        r1 = HIST + (j + 1) * seq_len
        stail_ref[0, j:j + 1, :] = sbuf[r1 - 1:r1, :]
    sbuf[HIST - 1:HIST, :] = sbuf[HIST + tm - 1:HIST + tm, :]

    s = pb + (spbuf[...] - pb) * mu_ref[...]
    o1, o2, o3 = RWKV_DIM, 2 * RWKV_DIM, 3 * RWKV_DIM
    r = s[:, 0:o1]
    k = s[:, o1:o2]
    v = s[:, o2:o3]
    s_l = s[:, o3:o3 + LORA_W + LORA_A]
    lg = s[:, o3 + LORA_W + LORA_A:]
    z = w0_ref[...] + _dot_stack3(jnp.tanh(s_l), lw_ref[...])
    w_log = -jax.nn.softplus(-z) - 0.5
    log_decay = -jnp.exp(w_log)
    a = jax.nn.sigmoid(a0_ref[...] + _dot_stack3(s_l, la_ref[...]))
    g = _dot_stack3(jax.nn.sigmoid(lg), lgw_ref[...])

    ones = ones_ref[...]
    kk = k * kk_ref[...]
    kk_n = kk / jnp.maximum(jnp.sqrt(_split2_dot(kk * kk, ones)), 1e-12)
    k2 = k * (1.0 + (a - 1.0) * ka_ref[...])
    bonus = _split2_dot(r * k2 * rk_ref[...], ones) * v

    r_ref[...] = r
    w_ref[...] = log_decay
    k_ref[...] = k2
    v_ref[...] = v
    a_ref[...] = -kk_n
    b_ref[...] = kk_n * a
    bonus_ref[...] = bonus
    g_ref[...] = g


def _mixer_pre(x_p, x_s, st_conv_t, st_shift_t, p, *, seq_len, tm):
    n_p, d = x_p.shape
    n_tok = n_p + x_s.shape[0]
    n_prompt_tiles = n_p // tm
    n_tiles = n_tok // tm
    seqs = tm // seq_len
    row = lambda w: pl.BlockSpec((tm, w), lambda i: (i, 0))
    st_idx = lambda i: (jnp.maximum(i - n_prompt_tiles, 0), 0, 0)
    consts = [p['norm1_g'], p['w_in_a'], p['w_in_b'], p['w_in_g'], p['conv_w'], p['mu_shift'],
              p['w0'], p['w_lora_w'], p['a0'], p['w_lora_a'], p['w_lora_g'], p['k_k'], p['k_a'],
              p['r_k'], p['w_out_a'], p['ones_bf16']]
    in_specs = [pl.BlockSpec((tm, d), lambda i: (jnp.minimum(i, n_prompt_tiles - 1), 0)),
                pl.BlockSpec((tm, d), lambda i: (jnp.maximum(i - n_prompt_tiles, 0), 0)),
                pl.BlockSpec((1, 2 * seqs, CONV_DIM), st_idx),
                pl.BlockSpec((1, seqs, SHIFT_DIM), st_idx)] + [_const_spec(c.shape) for c in consts]
    sds = lambda w: jax.ShapeDtypeStruct((n_tok, w), F32)
    out_shape = [sds(d), sds(d)] + [sds(RWKV_DIM)] * 8 + [
        jax.ShapeDtypeStruct((n_tiles, 2 * seqs, CONV_DIM), F32),
        jax.ShapeDtypeStruct((n_tiles, seqs, SHIFT_DIM), F32)]
    out_specs = [row(d), row(d)] + [row(RWKV_DIM)] * 8 + [
        pl.BlockSpec((1, 2 * seqs, CONV_DIM), lambda i: (i, 0, 0)),
        pl.BlockSpec((1, seqs, SHIFT_DIM), lambda i: (i, 0, 0))]
    kern = functools.partial(_mixer_pre_kernel, n_prompt_tiles=n_prompt_tiles, seq_len=seq_len)
    return pl.pallas_call(
        kern, out_shape=out_shape, grid=(n_tiles,), in_specs=in_specs, out_specs=out_specs,
        scratch_shapes=[pltpu.VMEM((tm + HIST, CONV_DIM), F32), pltpu.VMEM((tm + HIST, SHIFT_DIM), F32),
                        pltpu.VMEM((tm, CONV_DIM), F32), pltpu.VMEM((tm, CONV_DIM), F32),
                        pltpu.VMEM((tm, SHIFT_DIM), F32)],
        compiler_params=pltpu.CompilerParams(dimension_semantics=("arbitrary",),
                                             vmem_limit_bytes=VMEM_LIMIT),
        name="mixer_pre")(x_p, x_s, st_conv_t, st_shift_t, *consts)


GROUP_LANES = 256
HEADS_PER_GROUP = GROUP_LANES // RWKV_HEAD
N_HEAD_GROUPS = RWKV_DIM // GROUP_LANES
NN = (((1,), (0,)), ((), ()))
NT = (((1,), (1,)), ((), ()))


def _split2(x):
    hi = x.astype(BF16)
    lo = (x - hi.astype(F32)).astype(BF16)
    return hi, lo


def _mm(xs, ys, dims=NN):
    x1, x2 = xs
    y1, y2 = ys
    d = lambda p, q: lax.dot_general(p, q, dims, preferred_element_type=F32)
    m = x1.shape[0]
    both = d(jnp.concatenate([x1, x2], axis=0), y1)
    return both[0:m] + both[m:2 * m] + d(x1, y2)


def _cat2(ps, qs, axis):
    return tuple(jnp.concatenate([p, q], axis=axis) for p, q in zip(ps, qs))


def _wkv_masks(c):
    hc = HEADS_PER_GROUP * c
    levels = c.bit_length() - 1
    t = np.arange(c)[:, None]
    s = (np.arange(hc) % c)[None, :]
    tm = [s < t, s <= t]
    for lvl in range(1, levels + 1):
        half = 1 << (lvl - 1)
        tm.append(((t >> lvl) == (s >> lvl)) & ((t & half) != 0) & ((s & half) == 0))
    row_head = (np.arange(hc) // c)[:, None]
    mfeat = row_head == (np.arange(GROUP_LANES) // RWKV_HEAD)[None, :]
    mpos = row_head == (np.arange(hc) // c)[None, :]
    lane_head = np.arange(GROUP_LANES) // RWKV_HEAD
    stmask = lane_head[:, None] == lane_head[None, :]
    tri = np.arange(c)[None, :] <= np.arange(c)[:, None]
    return (jnp.asarray(mfeat, BF16), jnp.asarray(mpos, BF16), jnp.asarray(np.stack(tm), F32),
            jnp.asarray(stmask, F32), jnp.asarray(tri, BF16))


def _interleave(*gens):
    live = list(gens)
    while live:
        for gen in list(live):
            try:
                next(gen)
            except StopIteration:
                live.remove(gen)


def _wkv_pipe_kernel(r_ref, lw_ref, k_ref, v_ref, a_ref, b_ref, s0_ref, mfeat_ref, mpos_ref,
                     tmask_ref, stmask_ref, tri_ref, y_ref, sout_ref,
                     state, sv_ar, sv_inv, sv_akv, sv_arbk, sv_vbd, sv_v, sv_bk, sv_pend,
                     *, chunk, n_steps, chained):
    i = pl.program_id(0)
    c = chunk
    n_chunks = r_ref.shape[0] // c
    hc = HEADS_PER_GROUP * c
    levels = c.bit_length() - 1
    groups = range(N_HEAD_GROUPS)
    units = [(j, g) for j in range(n_chunks) for g in groups]
    uid = {u: n for n, u in enumerate(units)}

    stmask = stmask_ref[...]

    def expand(sc):
        pair = jnp.concatenate([sc, sc], axis=1)
        return jnp.concatenate([pair] * (GROUP_LANES // pair.shape[1]), axis=1) * stmask

    def compact(s):
        half = s[:, 0:GROUP_LANES // 2] + s[:, GROUP_LANES // 2:]
        return half[:, 0:RWKV_HEAD] + half[:, RWKV_HEAD:]

    @pl.when(i == 0)
    def _():
        for ref in (sv_ar, sv_inv, sv_akv, sv_arbk, sv_vbd, sv_v, sv_bk):
            ref[...] = jnp.zeros(ref.shape, ref.dtype)
        sv_pend[...] = jnp.ones(sv_pend.shape, F32)
        if chained:
            for g in groups:
                state[g] = expand(s0_ref[0, g])

    mfeat = mfeat_ref[...]
    mpos = mpos_ref[...]
    strict = tmask_ref[0]
    incl = tmask_ref[1]
    eye = incl - strict
    tri = tri_ref[...]

    def ld(ref, j, g):
        return ref[j * c:(j + 1) * c, g * GROUP_LANES:(g + 1) * GROUP_LANES]

    def bd_split(ps):
        mask = mpos if ps[0].shape[1] == hc else mfeat
        return tuple(jnp.concatenate([p] * HEADS_PER_GROUP, axis=0) * mask for p in ps)

    def bd2(m):
        return bd_split(_split2(m))

    def cumsum_rows(x):
        p1 = x.astype(BF16)
        r1 = x - p1.astype(F32)
        p2 = r1.astype(BF16)
        p3 = (r1 - p2.astype(F32)).astype(BF16)
        d = lambda q: jnp.dot(tri, q, preferred_element_type=F32)
        return d(p1) + d(p2) + d(p3)

    new = {}

    def prepare():
        cum = {u: cumsum_rows(ld(lw_ref, *u)) for u in units}
        yield
        ar, bk_end, p_end, a_ab, a_ak, a_rb, a_rk, v, vbd = ({} for _ in range(9))
        for u in units:
            cm = cum[u]
            cum_last = cm[c - 1:c, :]
            e_neg = jnp.exp(-cm)
            e_end = jnp.exp(cum_last - cm)
            b_raw = ld(b_ref, *u)
            k_raw = ld(k_ref, *u)
            ar[u] = _split2(jnp.concatenate([ld(a_ref, *u) * jnp.exp(cm - ld(lw_ref, *u)),
                                             ld(r_ref, *u) * jnp.exp(cm)], axis=0))
            bk_end[u] = _split2(jnp.concatenate([b_raw * e_end, k_raw * e_end], axis=0))
            p_end[u] = jnp.exp(cum_last)
            v[u] = ld(v_ref, *u)
            vbd[u] = bd2(v[u])
            gram = _mm(ar[u], _cat2(bd2(b_raw * e_neg), bd2(k_raw * e_neg), 0), NT)
            a_ab[u] = jnp.where(strict > 0, gram[0:c, 0:hc], 0.0)
            a_ak[u] = jnp.where(strict > 0, gram[0:c, hc:2 * hc], 0.0)
            a_rb[u] = jnp.where(incl > 0, gram[c:2 * c, 0:hc], 0.0)
            a_rk[u] = jnp.where(incl > 0, gram[c:2 * c, hc:2 * hc], 0.0)
            if uid[u] % 2 == 1:
                yield
        a_ab2 = {u: _split2(a_ab[u]) for u in units}
        inv = {u: eye + a_ab[u] * tmask_ref[2] for u in units}
        inv2 = {u: _split2(inv[u]) for u in units}
        for lvl in range(2, levels + 1):
            lm = tmask_ref[1 + lvl]
            t1 = {u: _mm(a_ab2[u], bd_split(inv2[u])) for u in units}
            yield
            inv = {u: inv[u] + lm * _mm(inv2[u], bd2(t1[u])) for u in units}
            inv2 = {u: _split2(inv[u]) for u in units}
            yield
        akv = {u: _mm(_split2(a_ak[u]), vbd[u]) for u in units}
        arbk = {u: _split2(jnp.concatenate([a_rb[u], a_rk[u]], axis=1)) for u in units}
        new.update(ar=ar, inv=inv2, akv=akv, arbk=arbk, vbd=vbd, v=v, bk=bk_end, pend=p_end)

    def serial():
        pair = lambda ref, n: (ref[0, n], ref[1, n])
        s_cur = [state[g] for g in groups] if chained else None
        for j in range(n_chunks):
            ns = [uid[j, g] for g in groups]
            s_prev = s_cur if chained else [expand(s0_ref[j, g]) for g in groups]
            x0 = [_mm(pair(sv_ar, n), _split2(s_prev[g]), NT) for g, n in zip(groups, ns)]
            yield
            uu = [_mm(pair(sv_inv, n), bd2(x0[g][0:c] + sv_akv[n])) for g, n in zip(groups, ns)]
            yield
            for g, n in zip(groups, ns):
                yy = x0[g][c:2 * c] + _mm(pair(sv_arbk, n), _cat2(bd2(uu[g]), pair(sv_vbd, n), 0))
                y_ref[j * c:(j + 1) * c, g * GROUP_LANES:(g + 1) * GROUP_LANES] = yy
            s_new = []
            for g, n in zip(groups, ns):
                uv_t = jnp.transpose(jnp.concatenate([uu[g], sv_v[n]], axis=0))
                upd = _mm(_split2(uv_t), pair(sv_bk, n))
                s_new.append(s_prev[g] * sv_pend[n, 0:1, :] + stmask * upd)
            yield
            if chained:
                s_cur = s_new
            else:
                for g in groups:
                    sout_ref[j, g] = compact(s_new[g])
        if chained:
            for g in groups:
                state[g] = s_cur[g]
        new['last_state'] = s_new

    _interleave(serial(), prepare())

    bits = sum(pltpu.bitcast(s[0:8, 0:LANES], jnp.uint32) for s in new['last_state'])
    zero = pltpu.bitcast(lax.shift_right_logical(bits, jnp.uint32(32)), F32)[0:1, 0:1]
    zero_bf = zero.astype(BF16)
    for u, n in uid.items():
        for name, ref in (('ar', sv_ar), ('inv', sv_inv), ('arbk', sv_arbk), ('vbd', sv_vbd), ('bk', sv_bk)):
            for half in range(2):
                ref[half, n] = new[name][u][half] + zero_bf
        sv_akv[n] = new['akv'][u] + zero
        sv_v[n] = new['v'][u] + zero
        sv_pend[n] = jnp.broadcast_to(new['pend'][u] + zero, sv_pend.shape[1:])

    if chained:
        @pl.when(i == n_steps)
        def _():
            for g in groups:
                sout_ref[0, g] = compact(state[g])


def _wkv_scan(rlkvab, s0, *, row0, n_rows, chunk, chunks_per_step, chained):
    rows = chunk * chunks_per_step
    n_steps = n_rows // rows
    assert n_rows % rows == 0 and row0 % rows == 0
    masks = _wkv_masks(chunk)
    hc = HEADS_PER_GROUP * chunk
    n_units = chunks_per_step * N_HEAD_GROUPS
    st = (N_HEAD_GROUPS, GROUP_LANES, RWKV_HEAD)
    prev = lambda i: jnp.maximum(i - 1, 0)
    row_in = pl.BlockSpec((rows, RWKV_DIM), lambda i: (row0 // rows + jnp.minimum(i, n_steps - 1), 0))
    if chained:
        n_state = 1
        st_spec = pl.BlockSpec((1,) + st, lambda i: (0, 0, 0, 0))
    else:
        n_state = n_rows // chunk
        st_spec = pl.BlockSpec((chunks_per_step,) + st, lambda i: (prev(i), 0, 0, 0))
    in_specs = [row_in] * 6 + [st_spec] + [_const_spec(m.shape) for m in masks]
    out_shape = [jax.ShapeDtypeStruct((n_rows, RWKV_DIM), F32),
                 jax.ShapeDtypeStruct((n_state,) + st, F32)]
    out_specs = [pl.BlockSpec((rows, RWKV_DIM), lambda i: (prev(i), 0)), st_spec]
    scratch = [pltpu.VMEM((N_HEAD_GROUPS, GROUP_LANES, GROUP_LANES), F32),
               pltpu.VMEM((2, n_units, 2 * chunk, GROUP_LANES), BF16),
               pltpu.VMEM((2, n_units, chunk, hc), BF16),
               pltpu.VMEM((n_units, chunk, GROUP_LANES), F32),
               pltpu.VMEM((2, n_units, chunk, 2 * hc), BF16),
               pltpu.VMEM((2, n_units, hc, GROUP_LANES), BF16),
               pltpu.VMEM((n_units, chunk, GROUP_LANES), F32),
               pltpu.VMEM((2, n_units, 2 * chunk, GROUP_LANES), BF16),
               pltpu.VMEM((n_units, 8, GROUP_LANES), F32)]
    kern = functools.partial(_wkv_pipe_kernel, chunk=chunk, n_steps=n_steps, chained=chained)
    return pl.pallas_call(
        kern, out_shape=out_shape, grid=(n_steps + 1,), in_specs=in_specs, out_specs=out_specs,
        scratch_shapes=scratch,
        compiler_params=pltpu.CompilerParams(dimension_semantics=("arbitrary",),
                                             vmem_limit_bytes=VMEM_LIMIT),
        name="wkv_scan")(*rlkvab, s0, *masks)


def _mixer_post_kernel(yp_ref, ys_ref, bonus_ref, g_ref, yag_ref, sgb_ref, xp_ref, xs_ref, lng_ref, lnb_ref,
                       wob_ref, wo_ref, n2_ref, wr_ref, br_ref, ones_ref, tril_ref,
                       x1_ref, h2_ref, route_ref, counts_ref, cnt, *, n_prompt_tiles):
    i = pl.program_id(0)

    @pl.when(i == 0)
    def _():
        cnt[...] = jnp.zeros(cnt.shape, F32)

    x = jnp.where(i < n_prompt_tiles, xp_ref[...], xs_ref[...])
    ones = ones_ref[...]
    y = jnp.where(i < n_prompt_tiles, yp_ref[...], ys_ref[...])
    inv_n = 1.0 / RWKV_HEAD
    mean = _split2_dot(y, ones) * inv_n
    yc = y - mean
    var = _split2_dot(yc * yc, ones) * inv_n
    yn = yc * lax.rsqrt(var + GN_EPS) * lng_ref[...] + lnb_ref[...]
    yy = (yn + bonus_ref[...]) * g_ref[...]
    y_b = jnp.dot(yy.astype(BF16), wob_ref[...], preferred_element_type=F32)
    merged = yag_ref[...] + sgb_ref[...] * y_b
    x1 = x + jnp.dot(merged.astype(BF16), wo_ref[...], preferred_element_type=F32)
    x1_ref[...] = x1
    h2 = _rms(x1, n2_ref[...])
    h2_ref[...] = _pack_bf16_pairs(h2)

    logits = _dot_stack3(h2, wr_ref[...]) + br_ref[...]
    lane = lax.broadcasted_iota(jnp.int32, logits.shape, 1)
    neg = jnp.float32(-jnp.inf)
    big = jnp.int32(LANES)
    is_g = lane < N_GROUPS
    lgp = jnp.where(is_g, logits, neg)
    m_g = jnp.max(lgp, axis=-1, keepdims=True)
    grp = jnp.min(jnp.where(lgp == m_g, lane, big), axis=-1, keepdims=True)
    p_top = 1.0 / jnp.sum(jnp.where(is_g, jnp.exp(logits - m_g), 0.0), axis=-1, keepdims=True)
    e_lane = lane - N_GROUPS
    in_grp = (e_lane >= grp * EXPERTS_PER_GROUP) & (e_lane < (grp + 1) * EXPERTS_PER_GROUP)
    le = jnp.where(in_grp, logits, neg)
    m1 = jnp.max(le, axis=-1, keepdims=True)
    i1 = jnp.min(jnp.where(le == m1, lane, big), axis=-1, keepdims=True)
    le2 = jnp.where(lane == i1, neg, le)
    m2 = jnp.max(le2, axis=-1, keepdims=True)
    i2 = jnp.min(jnp.where(le2 == m2, lane, big), axis=-1, keepdims=True)
    ex = jnp.exp(m2 - m1)
    p1 = 1.0 / (1.0 + ex)
    p2 = ex / (1.0 + ex)
    oh1 = lane == i1 - N_GROUPS
    oh2 = lane == i2 - N_GROUPS
    both = jnp.where(oh1, 1.0, jnp.where(oh2, 1.0, 0.0))
    before = jnp.dot(tril_ref[...], both.astype(BF16), preferred_element_type=F32) + cnt[0:1, :]
    rank1 = jnp.sum(jnp.where(oh1, before, 0.0), axis=-1, keepdims=True)
    rank2 = jnp.sum(jnp.where(oh2, before, 0.0), axis=-1, keepdims=True)
    cnt[0:1, :] = cnt[0:1, :] + jnp.sum(both, axis=0, keepdims=True)
    counts_ref[...] = jnp.broadcast_to(cnt[0:1, :], counts_ref.shape)

    cols = [(i1 - N_GROUPS).astype(F32), (i2 - N_GROUPS).astype(F32), p_top * p1, p_top * p2,
            rank1, rank2]
    route = jnp.zeros(logits.shape, F32)
    for c, col in enumerate(cols):
        route = jnp.where(lane == c, col, route)
    route_ref[...] = route


ROUTE_EXPERT, ROUTE_WEIGHT, ROUTE_RANK = 0, 2, 4


def _mixer_post(y_p, y_s, bonus, g, yag, sgb, x_p, x_s, p, *, tm):
    n_p, d = x_p.shape
    n_tok = n_p + x_s.shape[0]
    npt = n_p // tm
    row = lambda w: pl.BlockSpec((tm, w), lambda i: (i, 0))
    tril = jnp.asarray(np.arange(tm)[None, :] < np.arange(tm)[:, None], BF16)
    consts = [p['lnx_g'], p['lnx_b'], p['w_out_b'], p['w_o'], p['norm2_g'], p['w_router'],
              p['b_router'], p['ones_bf16'], tril]
    pair = lambda w: [pl.BlockSpec((tm, w), lambda i: (jnp.minimum(i, npt - 1), 0)),
                      pl.BlockSpec((tm, w), lambda i: (jnp.maximum(i - npt, 0), 0))]
    in_specs = (pair(RWKV_DIM) + [row(RWKV_DIM)] * 2 + [row(d)] * 2 + pair(d)
                + [_const_spec(c.shape) for c in consts])
    out_shape = [jax.ShapeDtypeStruct((n_tok, d), F32), jax.ShapeDtypeStruct((n_tok, d // 2), jnp.uint32),
                 jax.ShapeDtypeStruct((n_tok, LANES), F32), jax.ShapeDtypeStruct((8, LANES), F32)]
    out_specs = [row(d), row(d // 2), row(LANES), _const_spec((8, LANES))]
    return pl.pallas_call(
        functools.partial(_mixer_post_kernel, n_prompt_tiles=npt), out_shape=out_shape,
        grid=(n_tok // tm,), in_specs=in_specs, out_specs=out_specs,
        scratch_shapes=[pltpu.VMEM((8, LANES), F32)],
        compiler_params=pltpu.CompilerParams(dimension_semantics=("arbitrary",),
                                             vmem_limit_bytes=VMEM_LIMIT),
        name="mixer_post")(y_p, y_s, bonus, g, yag, sgb, x_p, x_s, *consts)


N_DMA_PRIORITIES = 2


def _gather_rows(idx_ref, src_hbm, dst, sem, n_rows, *, unrolled):
    def start(r, priority):
        pltpu.make_async_copy(src_hbm.at[pl.ds(idx_ref[r], 1)], dst.at[pl.ds(r, 1)],
                              sem).start(priority=priority)
    if unrolled:
        for r in range(n_rows):
            start(r, r % N_DMA_PRIORITIES)
    else:
        def body(r, carry):
            start(r, 0)
            return carry
        lax.fori_loop(0, n_rows, body, 0)


def _wait_rows(src_hbm, dst, sem, n_rows):
    pltpu.make_async_copy(src_hbm.at[pl.ds(0, n_rows)], dst, sem).wait()


def _moe_scatter_kernel(dest_ref, h_ref, xs_in_hbm, xs_hbm, sem):
    del xs_in_hbm
    n = dest_ref.shape[-1]
    tm = h_ref.shape[0]
    for r in range(n):
        pltpu.make_async_copy(h_ref.at[pl.ds(r % tm, 1)], xs_hbm.at[pl.ds(dest_ref[0, 0, r], 1)],
                              sem).start(priority=r % N_DMA_PRIORITIES)
    for j in range(n // tm):
        pltpu.make_async_copy(h_ref, xs_hbm.at[pl.ds(0, tm)], sem).wait()


def _moe_scatter(h2, pos3, n_rows):
    n_tok, d = h2.shape
    n_tiles, _, n = pos3.shape
    tm = n // TOP_K
    xs0 = jnp.zeros((n_rows, d), h2.dtype)
    return pl.pallas_call(
        _moe_scatter_kernel, out_shape=jax.ShapeDtypeStruct((n_rows, d), h2.dtype), grid=(n_tiles,),
        in_specs=[pl.BlockSpec((1, 1, n), lambda i: (i, 0, 0), memory_space=pltpu.SMEM),
                  pl.BlockSpec((tm, d), lambda i: (i, 0)), pl.BlockSpec(memory_space=pl.ANY)],
        out_specs=pl.BlockSpec(memory_space=pl.ANY),
        scratch_shapes=[pltpu.SemaphoreType.DMA(())], input_output_aliases={2: 0},
        compiler_params=pltpu.CompilerParams(dimension_semantics=("arbitrary",), has_side_effects=True,
                                             vmem_limit_bytes=VMEM_LIMIT),
        name="moe_scatter")(pos3, h2, xs0)


def _moe_experts_kernel(bexp_ref, nused_ref, x_ref, wg_ref, wu_ref, wd_ref, yb_ref, wg_bf, wu_bf, wd_bf):
    i = pl.program_id(0)

    @pl.when((i == 0) | (bexp_ref[i] != bexp_ref[jnp.maximum(i - 1, 0)]))
    def _():
        wg_bf[...] = wg_ref[0].astype(BF16)
        wu_bf[...] = wu_ref[0].astype(BF16)
        wd_bf[...] = wd_ref[0].astype(BF16)

    @pl.when(i < nused_ref[0])
    def _():
        xb = _unpack_bf16_pairs(x_ref[...]).astype(BF16)
        hg = jnp.dot(xb, wg_bf[...], preferred_element_type=F32)
        hu = jnp.dot(xb, wu_bf[...], preferred_element_type=F32)
        hid = (hg * jax.nn.sigmoid(hg)) * hu
        yb_ref[...] = _pack_bf16_pairs(jnp.dot(hid.astype(BF16), wd_bf[...], preferred_element_type=F32))

    @pl.when(i >= nused_ref[0])
    def _():
        yb_ref[...] = jnp.zeros(yb_ref.shape, yb_ref.dtype)


def _moe_experts(xs, block_expert, n_used, w_eg, w_eu, w_ed, *, blk):
    n_rows, dp = xs.shape
    n_blocks = n_rows // blk
    d, de = w_eg.shape[1:]
    grid_spec = pltpu.PrefetchScalarGridSpec(
        num_scalar_prefetch=2, grid=(n_blocks,),
        in_specs=[
            pl.BlockSpec((blk, dp), lambda i, be, nu: (jnp.minimum(i, nu[0] - 1), 0)),
            pl.BlockSpec((1, d, de), lambda i, be, nu: (be[i], 0, 0)),
            pl.BlockSpec((1, d, de), lambda i, be, nu: (be[i], 0, 0)),
            pl.BlockSpec((1, de, d), lambda i, be, nu: (be[i], 0, 0)),
        ],
        out_specs=pl.BlockSpec((blk, dp), lambda i, be, nu: (i, 0)),
        scratch_shapes=[pltpu.VMEM((d, de), BF16), pltpu.VMEM((d, de), BF16), pltpu.VMEM((de, d), BF16)])
    return pl.pallas_call(
        _moe_experts_kernel, out_shape=jax.ShapeDtypeStruct((n_rows, dp), xs.dtype), grid_spec=grid_spec,
        compiler_params=pltpu.CompilerParams(dimension_semantics=("arbitrary",),
                                             vmem_limit_bytes=VMEM_LIMIT),
        name="moe_experts")(block_expert, n_used, xs, w_eg, w_eu, w_ed)


GATHER_AHEAD = 2


def _with_dummy_blocks(idx, n_blocks, blk):
    pad = jnp.zeros((GATHER_AHEAD * blk,), idx.dtype)
    return jnp.concatenate([idx, pad]).reshape(n_blocks + GATHER_AHEAD, 1, blk)


def _moe_combine_kernel(*refs, n_prompt_tiles):
    pos_refs = refs[:GATHER_AHEAD + 1]
    x1_ref, route_ref, nf_ref, yb_hbm, outp_ref, outs_ref, ybuf, sems = refs[GATHER_AHEAD + 1:]
    i = pl.program_id(0)
    n_slots = GATHER_AHEAD + 1
    rows = ybuf.shape[1]
    slot = i % n_slots

    @pl.when(i == 0)
    def _():
        for a in range(GATHER_AHEAD):
            _gather_rows(pos_refs[a].at[0, 0], yb_hbm, ybuf.at[a], sems.at[a], rows, unrolled=False)

    _wait_rows(yb_hbm, ybuf.at[slot], sems.at[slot], rows)
    ahead = (i + GATHER_AHEAD) % n_slots
    _gather_rows(pos_refs[GATHER_AHEAD].at[0, 0], yb_hbm, ybuf.at[ahead], sems.at[ahead], rows, unrolled=True)
    tm = x1_ref.shape[0]
    route = route_ref[...]
    x2 = x1_ref[...]
    for j in range(TOP_K):
        yj = _unpack_bf16_pairs(ybuf[slot, j * tm:(j + 1) * tm, :])
        x2 = x2 + yj * route[:, ROUTE_WEIGHT + j:ROUTE_WEIGHT + j + 1]
    out = _rms(x2, nf_ref[...])

    @pl.when(i < n_prompt_tiles)
    def _():
        outp_ref[...] = out

    @pl.when(i >= n_prompt_tiles)
    def _():
        outs_ref[...] = out

    @pl.when(i == pl.num_programs(0) - 1)
    def _():
        for a in range(1, n_slots):
            s = (i + a) % n_slots
            _wait_rows(yb_hbm, ybuf.at[s], sems.at[s], rows)


def _moe_combine(pos_tiles, x1, route, normf_g, yb, *, n_prompt_rows, tm):
    n_tok, d = x1.shape
    n_tiles = n_tok // tm
    npt = n_prompt_rows // tm
    pos3 = _with_dummy_blocks(pos_tiles.reshape(-1), n_tiles, TOP_K * tm)
    pos_spec = lambda a: pl.BlockSpec((1, 1, TOP_K * tm), lambda i: (i + a, 0, 0), memory_space=pltpu.SMEM)
    in_specs = [pos_spec(a) for a in range(GATHER_AHEAD + 1)] + [
        pl.BlockSpec((tm, d), lambda i: (i, 0)),
        pl.BlockSpec((tm, LANES), lambda i: (i, 0)),
        _const_spec(normf_g.shape),
        pl.BlockSpec(memory_space=pl.ANY),
    ]
    out_shape = [jax.ShapeDtypeStruct((n_prompt_rows, d), F32),
                 jax.ShapeDtypeStruct((n_tok - n_prompt_rows, d), F32)]
    out_specs = [pl.BlockSpec((tm, d), lambda i: (jnp.minimum(i, npt - 1), 0)),
                 pl.BlockSpec((tm, d), lambda i: (jnp.maximum(i - npt, 0), 0))]
    return pl.pallas_call(
        functools.partial(_moe_combine_kernel, n_prompt_tiles=npt), out_shape=out_shape,
        grid=(n_tiles,), in_specs=in_specs, out_specs=out_specs,
        scratch_shapes=[pltpu.VMEM((GATHER_AHEAD + 1, TOP_K * tm, yb.shape[1]), yb.dtype),
                        pltpu.SemaphoreType.DMA((GATHER_AHEAD + 1,))],
        compiler_params=pltpu.CompilerParams(dimension_semantics=("arbitrary",),
                                             vmem_limit_bytes=VMEM_LIMIT),
        name="moe_combine")(*([pos3] * (GATHER_AHEAD + 1)), x1, route, normf_g, yb)


def _dispatch(route, counts, blk, tm):
    n_tok = route.shape[0]
    n_assign = n_tok * TOP_K
    expert = route[:, ROUTE_EXPERT:ROUTE_EXPERT + TOP_K].astype(jnp.int32)
    rank = route[:, ROUTE_RANK:ROUTE_RANK + TOP_K].astype(jnp.int32)
    counts = counts[0, :N_EXPERTS].astype(jnp.int32)
    padded = (counts + blk - 1) // blk * blk
    pad_end = jnp.cumsum(padded)
    pad_start = pad_end - padded
    is_e = expert[:, :, None] == jnp.arange(N_EXPERTS, dtype=jnp.int32)
    dest = jnp.sum(jnp.where(is_e, pad_start, 0), axis=-1) + rank
    n_blocks = -(-n_assign // blk) + N_EXPERTS
    block_start = jnp.arange(n_blocks, dtype=jnp.int32) * blk
    block_expert = jnp.minimum(jnp.sum(block_start[:, None] >= pad_end[None, :], axis=1),
                               N_EXPERTS - 1).astype(jnp.int32)
    n_used = (pad_end[-1] // blk).reshape(1)
    pos_tiles = dest.reshape(n_tok // tm, tm, TOP_K).transpose(0, 2, 1).reshape(n_tok // tm, 1, TOP_K * tm)
    return pos_tiles, block_expert, n_used, n_blocks


def _state_to_kernel(s):
    return s.reshape(s.shape[0], N_HEAD_GROUPS, GROUP_LANES, RWKV_HEAD)


def _state_from_kernel(s):
    return s.reshape(s.shape[0], RWKV_HEADS, RWKV_HEAD, RWKV_HEAD)


def kernel(x_prompt, x_sample, state_conv, state_shift, state_wkv, norm1_g, w_in, conv_w, mu_shift, w0, w_lora_w, a0, w_lora_a, w_lora_g, k_k, k_a, r_k, lnx_g, lnx_b, w_out_a, w_out_b, w_o, norm2_g, w_router_group, b_router_group, w_router_expert, b_router_expert, w_e_gate, w_e_up, w_e_down, normf_g):
    depth = norm1_g.shape[0]
    bp, seq, d = x_prompt.shape
    db, dseq, _ = x_sample.shape
    assert depth == 1 and bp == 1, "single layer, single prompt stream"
    tm = ROW_TILE
    n_p, n_s = bp * seq, db * dseq
    assert n_p % tm == 0 and n_s % tm == 0 and tm % dseq == 0
    n_prompt_tiles, n_sample_tiles = n_p // tm, n_s // tm
    seqs = tm // dseq
    n_tok = n_p + n_s

    x_p = x_prompt.reshape(n_p, d)
    x_s = x_sample.reshape(n_s, d)

    l = 0
    c3 = 3 * CONV_DIM
    head_id = np.arange(GROUP_LANES) // RWKV_HEAD
    ones_bf16 = jnp.asarray(head_id[:, None] == head_id[None, :], BF16)
    zpad = jnp.zeros((LORA_W, RWKV_DIM), F32)
    n_r = N_GROUPS + N_EXPERTS
    p = {
        'norm1_g': norm1_g[l].reshape(1, d),
        'w_in_a': w_in[l][:, :c3].astype(BF16),
        'w_in_b': w_in[l][:, c3:c3 + SHIFT_DIM].astype(BF16),
        'w_in_g': w_in[l][:, c3 + SHIFT_DIM:].astype(BF16),
        'conv_w': conv_w[l],
        'mu_shift': mu_shift[l].reshape(1, SHIFT_DIM),
        'w0': w0[l].reshape(1, RWKV_DIM),
        'w_lora_w': _stack3(jnp.concatenate([w_lora_w[l], zpad], axis=0)),
        'a0': a0[l].reshape(1, RWKV_DIM),
        'w_lora_a': _stack3(jnp.concatenate([zpad, w_lora_a[l]], axis=0)),
        'w_lora_g': _stack3(w_lora_g[l]),
        'k_k': k_k[l].reshape(1, RWKV_DIM),
        'k_a': k_a[l].reshape(1, RWKV_DIM),
        'r_k': r_k[l].reshape(1, RWKV_DIM),
        'w_out_a': w_out_a[l].astype(BF16),
        'ones_bf16': ones_bf16,
        'lnx_g': lnx_g[l].reshape(1, RWKV_DIM),
        'lnx_b': lnx_b[l].reshape(1, RWKV_DIM),
        'w_out_b': w_out_b[l].astype(BF16),
        'w_o': w_o[l].astype(BF16),
        'norm2_g': norm2_g[l].reshape(1, d),
        'w_router': _stack3(jnp.pad(jnp.concatenate([w_router_group[l], w_router_expert[l]], axis=1),
                                    ((0, 0), (0, LANES - n_r)))),
        'b_router': jnp.pad(jnp.concatenate([b_router_group[l], b_router_expert[l]]),
                            (0, LANES - n_r)).reshape(1, LANES),
    }
    st_conv_t = state_conv[l].reshape(n_sample_tiles, seqs * (CONV_WIDTH - 1), CONV_DIM)
    st_shift_t = state_shift[l].reshape(n_sample_tiles, seqs, SHIFT_DIM)

    (yag, sgb, r, w, k, v, a, b, bonus, g, ctail, stail) = _mixer_pre(
        x_p, x_s, st_conv_t, st_shift_t, p, seq_len=dseq, tm=tm)

    s0_prompt = jnp.zeros((1, N_HEAD_GROUPS, GROUP_LANES, RWKV_HEAD), F32)
    s0_sample = _state_to_kernel(state_wkv[l])
    rlkvab = (r, w, k, v, a, b)
    cp = min(SCAN_CHUNK, n_p)
    assert cp & (cp - 1) == 0 and dseq & (dseq - 1) == 0
    y_p, s_p = _wkv_scan(rlkvab, s0_prompt, row0=0, n_rows=n_p, chunk=cp,
                         chunks_per_step=SCAN_CHUNKS_PER_STEP, chained=True)
    y_s, s_s = _wkv_scan(rlkvab, s0_sample, row0=n_p, n_rows=n_s, chunk=dseq,
                         chunks_per_step=SCAN_CHUNKS_PER_STEP, chained=False)

    x1, h2, route, counts = _mixer_post(y_p, y_s, bonus, g, yag, sgb, x_p, x_s, p, tm=tm)

    blk = MOE_ROWS
    pos_tiles, block_expert, n_used, n_blocks = _dispatch(route, counts, blk, tm)
    xs = _moe_scatter(h2, pos_tiles, n_blocks * blk)
    yb = _moe_experts(xs, block_expert, n_used, w_e_gate[l], w_e_up[l], w_e_down[l], blk=blk)
    out_p, out_s = _moe_combine(pos_tiles, x1, route, normf_g.reshape(1, d), yb, n_prompt_rows=n_p, tm=tm)

    y_prompt = out_p.reshape(bp, seq, d)
    y_sample = out_s.reshape(db, dseq, d)
    conv_p = ctail[n_prompt_tiles - 1, 2 * (seqs - 1):2 * seqs].reshape(1, bp, CONV_WIDTH - 1, CONV_DIM)
    shift_p = stail[n_prompt_tiles - 1, seqs - 1].reshape(1, bp, 1, SHIFT_DIM)
    wkv_p = _state_from_kernel(s_p).reshape(1, bp, RWKV_HEADS, RWKV_HEAD, RWKV_HEAD)
    conv_s = ctail[n_prompt_tiles:].reshape(1, db, CONV_WIDTH - 1, CONV_DIM)
    shift_s = stail[n_prompt_tiles:].reshape(1, db, 1, SHIFT_DIM)
    wkv_s = _state_from_kernel(s_s).reshape(1, db, RWKV_HEADS, RWKV_HEAD, RWKV_HEAD)
    return (y_prompt, y_sample, conv_p, shift_p, wkv_p, conv_s, shift_s, wkv_s)
```

```python
import functools

import numpy as np
import jax
import jax.numpy as jnp
from jax import lax
from jax.experimental import pallas as pl
from jax.experimental.pallas import tpu as pltpu

F32 = jnp.float32
BF16 = jnp.bfloat16

CONV_DIM = 512
CONV_WIDTH = 3
RWKV_HEAD = 64
RWKV_HEADS = 8
RWKV_DIM = RWKV_HEADS * RWKV_HEAD
LORA_W = 64
LORA_A = 64
LORA_G = 128
SHIFT_DIM = 3 * RWKV_DIM + LORA_W + LORA_A + LORA_G
N_GROUPS = 4
EXPERTS_PER_GROUP = 8
N_EXPERTS = N_GROUPS * EXPERTS_PER_GROUP
TOP_K = 2
RMS_EPS = 1e-6
GN_EPS = 64e-5

LANES = 128
ROW_TILE = 256
SCAN_CHUNK = 64
SCAN_CHUNKS_PER_STEP = 4
MOE_ROWS = 256
HIST = 8
VMEM_LIMIT = 56 * 1024 * 1024


def _rms(x, g):
    return x * lax.rsqrt(jnp.mean(x * x, axis=-1, keepdims=True) + RMS_EPS) * g


def _split2_dot(x, ones_bf16):
    w = ones_bf16.shape[0]
    hi = x.astype(BF16)
    lo = (x - hi.astype(F32)).astype(BF16)
    parts = []
    for c0 in range(0, x.shape[1], w):
        both = jnp.concatenate([hi[:, c0:c0 + w], lo[:, c0:c0 + w]], axis=0)
        s = jnp.dot(both, ones_bf16, preferred_element_type=F32)
        parts.append(s[0:x.shape[0]] + s[x.shape[0]:])
    return jnp.concatenate(parts, axis=1)


def _pack_bf16_pairs(x):
    n = x.shape[1] // 2
    lo = pltpu.bitcast(x[:, :n].astype(BF16).astype(F32), jnp.uint32)
    hi = pltpu.bitcast(x[:, n:].astype(BF16).astype(F32), jnp.uint32)
    return hi | (lo >> 16)


def _unpack_bf16_pairs(p):
    lo = pltpu.bitcast(p << 16, F32)
    hi = pltpu.bitcast(p & jnp.uint32(0xFFFF0000), F32)
    return jnp.concatenate([lo, hi], axis=1)


def _stack3(w):
    hi = w.astype(BF16)
    lo = (w - hi.astype(F32)).astype(BF16)
    return jnp.concatenate([hi, hi, lo], axis=0)


def _dot_stack3(x, w_stack):
    hi = x.astype(BF16)
    lo = (x - hi.astype(F32)).astype(BF16)
    return jnp.dot(jnp.concatenate([hi, lo, hi], axis=1), w_stack, preferred_element_type=F32)


def _const_spec(shape):
    nd = len(shape)
    return pl.BlockSpec(shape, lambda *_: (0,) * nd)


def _interleave(*gens):
    live = list(gens)
    while live:
        for gen in list(live):
            try:
                next(gen)
            except StopIteration:
                live.remove(gen)


def _zero_after(x):
    return pltpu.bitcast(lax.shift_right_logical(pltpu.bitcast(x, jnp.uint32), jnp.uint32(32)), F32)


def _mixer_pre_tile(x_ref, stc_ref, sts_ref, n1_ref, wa_ref, wb_ref, wg_ref, convw_ref, mu_ref,
                    w0_ref, lw_ref, a0_ref, la_ref, lgw_ref, kk_ref, ka_ref, rk_ref, woa_ref, ones_ref,
                    yag_ref, sgb_ref, r_ref, w_ref, k_ref, v_ref, a_ref, b_ref, bonus_ref, g_ref,
                    ctail_ref, stail_ref, ccarry, scarry, *, sample, seq_len):
    tm = x_ref.shape[0]
    seqs = tm // seq_len
    h = _rms(x_ref[...], n1_ref[...]).astype(BF16)

    def prev_rows(val, k, heads):
        row = lax.broadcasted_iota(jnp.int32, val.shape, 0)
        out = pltpu.roll(val, k, axis=0)
        for r0, head in heads.items():
            out = jnp.where(row == r0, head, out)
        return out

    pb = jnp.dot(h, wb_ref[...], preferred_element_type=F32)
    old_s = scarry[...]
    if sample:
        s_heads = {j * seq_len: sts_ref[0, j:j + 1, :] for j in range(seqs)}
    else:
        s_heads = {0: old_s[HIST - 1:HIST, :]}
    prev = prev_rows(pb, 1, s_heads)
    for j in range(seqs):
        r1 = (j + 1) * seq_len
        stail_ref[0, j:j + 1, :] = pb[r1 - 1:r1, :]
    scarry[...] = pb[tm - HIST:tm, :] + _zero_after(old_s)

    def conv_and_gates():
        n_a = wa_ref.shape[1] // 3
        g_in = jnp.dot(h, wa_ref[:, 0:n_a], preferred_element_type=F32)
        yield
        g_out = jnp.dot(h, wa_ref[:, n_a:2 * n_a], preferred_element_type=F32)
        yield
        x_c = jnp.dot(h, wa_ref[:, 2 * n_a:3 * n_a], preferred_element_type=F32)
        bx = g_in * x_c
        old_c = ccarry[...]
        if sample:
            h1 = {j * seq_len: stc_ref[0, 2 * j + 1:2 * j + 2, :] for j in range(seqs)}
            h2 = {j * seq_len: stc_ref[0, 2 * j:2 * j + 1, :] for j in range(seqs)}
            h2.update({j * seq_len + 1: stc_ref[0, 2 * j + 1:2 * j + 2, :] for j in range(seqs)})
        else:
            h1 = {0: old_c[HIST - 1:HIST, :]}
            h2 = {0: old_c[HIST - 2:HIST - 1, :], 1: old_c[HIST - 1:HIST, :]}
        cw = convw_ref[...]
        conv = cw[0:1, :] * prev_rows(bx, 2, h2) + cw[1:2, :] * prev_rows(bx, 1, h1) + cw[2:3, :] * bx
        for j in range(seqs):
            r1 = (j + 1) * seq_len
            ctail_ref[0, 2 * j:2 * j + 2, :] = bx[r1 - 2:r1, :]
        ccarry[...] = bx[tm - HIST:tm, :] + _zero_after(old_c)
        yield
        y_a = jnp.dot((g_out * conv).astype(BF16), woa_ref[...], preferred_element_type=F32)
        yield
        d = wg_ref.shape[1] // 2
        half = d // 2
        for c0 in range(0, d, half):
            pg = jnp.dot(h, wg_ref[:, c0:c0 + half], preferred_element_type=F32)
            yag_ref[:, c0:c0 + half] = jax.nn.sigmoid(pg) * y_a[:, c0:c0 + half]
            yield
        for c0 in range(0, d, half):
            pg = jnp.dot(h, wg_ref[:, d + c0:d + c0 + half], preferred_element_type=F32)
            sgb_ref[:, c0:c0 + half] = jax.nn.sigmoid(pg)
            yield

    def rwkv_pre():
        s = pb + (prev - pb) * mu_ref[...]
        o1, o2, o3 = RWKV_DIM, 2 * RWKV_DIM, 3 * RWKV_DIM
        r = s[:, 0:o1]
        k = s[:, o1:o2]
        v = s[:, o2:o3]
        s_l = s[:, o3:o3 + LORA_W + LORA_A]
        lg = s[:, o3 + LORA_W + LORA_A:]
        r_ref[...] = r
        v_ref[...] = v
        yield
        z = w0_ref[...] + _dot_stack3(jnp.tanh(s_l), lw_ref[...])
        w_log = -jax.nn.softplus(-z) - 0.5
        w_ref[...] = -jnp.exp(w_log)
        yield
        a = jax.nn.sigmoid(a0_ref[...] + _dot_stack3(s_l, la_ref[...]))
        yield
        g_ref[...] = _dot_stack3(jax.nn.sigmoid(lg), lgw_ref[...])
        yield
        ones = ones_ref[...]
        kk = k * kk_ref[...]
        kk_n = kk / jnp.maximum(jnp.sqrt(_split2_dot(kk * kk, ones)), 1e-12)
        a_ref[...] = -kk_n
        b_ref[...] = kk_n * a
        yield
        k2 = k * (1.0 + (a - 1.0) * ka_ref[...])
        k_ref[...] = k2
        yield
        bonus_ref[...] = _split2_dot(r * k2 * rk_ref[...], ones) * v

    _interleave(conv_and_gates(), rwkv_pre())


def _mixer_pre_kernel(xp_ref, xs_ref, *refs, n_prompt_tiles, seq_len):
    i = pl.program_id(0)
    ccarry, scarry = refs[-2:]

    @pl.when(i == 0)
    def _():
        ccarry[...] = jnp.zeros(ccarry.shape, F32)
        scarry[...] = jnp.zeros(scarry.shape, F32)

    pl.when(i < n_prompt_tiles)(
        functools.partial(_mixer_pre_tile, xp_ref, *refs, sample=False, seq_len=seq_len))
    pl.when(i >= n_prompt_tiles)(
        functools.partial(_mixer_pre_tile, xs_ref, *refs, sample=True, seq_len=seq_len))


def _mixer_pre(x_p, x_s, st_conv_t, st_shift_t, p, *, seq_len, tm):
    n_p, d = x_p.shape
    n_tok = n_p + x_s.shape[0]
    n_prompt_tiles = n_p // tm
    n_tiles = n_tok // tm
    seqs = tm // seq_len
    row = lambda w: pl.BlockSpec((tm, w), lambda i: (i, 0))
    st_idx = lambda i: (jnp.maximum(i - n_prompt_tiles, 0), 0, 0)
    consts = [p['norm1_g'], p['w_in_a'], p['w_in_b'], p['w_in_g'], p['conv_w'], p['mu_shift'],
              p['w0'], p['w_lora_w'], p['a0'], p['w_lora_a'], p['w_lora_g'], p['k_k'], p['k_a'],
              p['r_k'], p['w_out_a'], p['ones_bf16']]
    in_specs = [pl.BlockSpec((tm, d), lambda i: (jnp.minimum(i, n_prompt_tiles - 1), 0)),
                pl.BlockSpec((tm, d), lambda i: (jnp.maximum(i - n_prompt_tiles, 0), 0)),
                pl.BlockSpec((1, 2 * seqs, CONV_DIM), st_idx),
                pl.BlockSpec((1, seqs, SHIFT_DIM), st_idx)] + [_const_spec(c.shape) for c in consts]
    sds = lambda w: jax.ShapeDtypeStruct((n_tok, w), F32)
    out_shape = [sds(d), sds(d)] + [sds(RWKV_DIM)] * 8 + [
        jax.ShapeDtypeStruct((n_tiles, 2 * seqs, CONV_DIM), F32),
        jax.ShapeDtypeStruct((n_tiles, seqs, SHIFT_DIM), F32)]
    out_specs = [row(d), row(d)] + [row(RWKV_DIM)] * 8 + [
        pl.BlockSpec((1, 2 * seqs, CONV_DIM), lambda i: (i, 0, 0)),
        pl.BlockSpec((1, seqs, SHIFT_DIM), lambda i: (i, 0, 0))]
    kern = functools.partial(_mixer_pre_kernel, n_prompt_tiles=n_prompt_tiles, seq_len=seq_len)
    return pl.pallas_call(
        kern, out_shape=out_shape, grid=(n_tiles,), in_specs=in_specs, out_specs=out_specs,
        scratch_shapes=[pltpu.VMEM((HIST, CONV_DIM), F32), pltpu.VMEM((HIST, SHIFT_DIM), F32)],
        compiler_params=pltpu.CompilerParams(dimension_semantics=("arbitrary",),
                                             vmem_limit_bytes=VMEM_LIMIT),
        name="mixer_pre")(x_p, x_s, st_conv_t, st_shift_t, *consts)


GROUP_LANES = 256
HEADS_PER_GROUP = GROUP_LANES // RWKV_HEAD
N_HEAD_GROUPS = RWKV_DIM // GROUP_LANES
NN = (((1,), (0,)), ((), ()))
NT = (((1,), (1,)), ((), ()))


def _split2(x):
    hi = x.astype(BF16)
    lo = (x - hi.astype(F32)).astype(BF16)
    return hi, lo


def _mm(xs, ys, dims=NN):
    x1, x2 = xs
    y1, y2 = ys
    d = lambda p, q: lax.dot_general(p, q, dims, preferred_element_type=F32)
    m = x1.shape[0]
    both = d(jnp.concatenate([x1, x2], axis=0), y1)
    return both[0:m] + both[m:2 * m] + d(x1, y2)


def _cat2(ps, qs, axis):
    return tuple(jnp.concatenate([p, q], axis=axis) for p, q in zip(ps, qs))


def _wkv_masks(c):
    hc = HEADS_PER_GROUP * c
    levels = c.bit_length() - 1
    t = np.arange(c)[:, None]
    s = (np.arange(hc) % c)[None, :]
    tm = [s < t, s <= t]
    for lvl in range(1, levels + 1):
        half = 1 << (lvl - 1)
        tm.append(((t >> lvl) == (s >> lvl)) & ((t & half) != 0) & ((s & half) == 0))
    row_head = (np.arange(hc) // c)[:, None]
    mfeat = row_head == (np.arange(GROUP_LANES) // RWKV_HEAD)[None, :]
    mpos = row_head == (np.arange(hc) // c)[None, :]
    lane_head = np.arange(GROUP_LANES) // RWKV_HEAD
    stmask = lane_head[:, None] == lane_head[None, :]
    tri = np.arange(c)[None, :] <= np.arange(c)[:, None]
    return (jnp.asarray(mfeat, BF16), jnp.asarray(mpos, BF16), jnp.asarray(np.stack(tm), F32),
            jnp.asarray(stmask, F32), jnp.asarray(tri, BF16))


def _wkv_pipe_kernel(r_ref, lw_ref, k_ref, v_ref, a_ref, b_ref, s0_ref, mfeat_ref, mpos_ref,
                     tmask_ref, stmask_ref, tri_ref, y_ref, sout_ref,
                     state, sv_ar, sv_inv, sv_akv, sv_arbk, sv_vbd, sv_v, sv_bk, sv_pend,
                     *, chunk, n_steps, chained):
    i = pl.program_id(0)
    c = chunk
    n_chunks = r_ref.shape[0] // c
    hc = HEADS_PER_GROUP * c
    levels = c.bit_length() - 1
    groups = range(N_HEAD_GROUPS)
    units = [(j, g) for j in range(n_chunks) for g in groups]
    uid = {u: n for n, u in enumerate(units)}

    stmask = stmask_ref[...]

    def expand(sc):
        pair = jnp.concatenate([sc, sc], axis=1)
        return jnp.concatenate([pair] * (GROUP_LANES // pair.shape[1]), axis=1) * stmask

    def compact(s):
        half = s[:, 0:GROUP_LANES // 2] + s[:, GROUP_LANES // 2:]
        return half[:, 0:RWKV_HEAD] + half[:, RWKV_HEAD:]

    @pl.when(i == 0)
    def _():
        for ref in (sv_ar, sv_inv, sv_akv, sv_arbk, sv_vbd, sv_v, sv_bk):
            ref[...] = jnp.zeros(ref.shape, ref.dtype)
        sv_pend[...] = jnp.ones(sv_pend.shape, F32)
        if chained:
            for g in groups:
                state[g] = expand(s0_ref[0, g])

    mfeat = mfeat_ref[...]
    mpos = mpos_ref[...]
    strict = tmask_ref[0]
    incl = tmask_ref[1]
    eye = incl - strict
    tri = tri_ref[...]

    def ld(ref, j, g):
        return ref[j * c:(j + 1) * c, g * GROUP_LANES:(g + 1) * GROUP_LANES]

    def bd_split(ps):
        mask = mpos if ps[0].shape[1] == hc else mfeat
        return tuple(jnp.concatenate([p] * HEADS_PER_GROUP, axis=0) * mask for p in ps)

    def bd2(m):
        return bd_split(_split2(m))

    def cumsum_rows(x):
        p1 = x.astype(BF16)
        r1 = x - p1.astype(F32)
        p2 = r1.astype(BF16)
        p3 = (r1 - p2.astype(F32)).astype(BF16)
        d = lambda q: jnp.dot(tri, q, preferred_element_type=F32)
        return d(p1) + d(p2) + d(p3)

    new = {}

    def prepare():
        cum = {u: cumsum_rows(ld(lw_ref, *u)) for u in units}
        yield
        ar, bk_end, p_end, a_ab, a_ak, a_rb, a_rk, v, vbd = ({} for _ in range(9))
        for u in units:
            cm = cum[u]
            cum_last = cm[c - 1:c, :]
            e_neg = jnp.exp(-cm)
            e_end = jnp.exp(cum_last - cm)
            b_raw = ld(b_ref, *u)
            k_raw = ld(k_ref, *u)
            ar[u] = _split2(jnp.concatenate([ld(a_ref, *u) * jnp.exp(cm - ld(lw_ref, *u)),
                                             ld(r_ref, *u) * jnp.exp(cm)], axis=0))
            bk_end[u] = _split2(jnp.concatenate([b_raw * e_end, k_raw * e_end], axis=0))
            p_end[u] = jnp.exp(cum_last)
            v[u] = ld(v_ref, *u)
            vbd[u] = bd2(v[u])
            gram = _mm(ar[u], _cat2(bd2(b_raw * e_neg), bd2(k_raw * e_neg), 0), NT)
            a_ab[u] = jnp.where(strict > 0, gram[0:c, 0:hc], 0.0)
            a_ak[u] = jnp.where(strict > 0, gram[0:c, hc:2 * hc], 0.0)
            a_rb[u] = jnp.where(incl > 0, gram[c:2 * c, 0:hc], 0.0)
            a_rk[u] = jnp.where(incl > 0, gram[c:2 * c, hc:2 * hc], 0.0)
            if uid[u] % 2 == 1:
                yield
        a_ab2 = {u: _split2(a_ab[u]) for u in units}
        inv = {u: eye + a_ab[u] * tmask_ref[2] for u in units}
        inv2 = {u: _split2(inv[u]) for u in units}
        for lvl in range(2, levels + 1):
            lm = tmask_ref[1 + lvl]
            t1 = {u: _mm(a_ab2[u], bd_split(inv2[u])) for u in units}
            yield
            inv = {u: inv[u] + lm * _mm(inv2[u], bd2(t1[u])) for u in units}
            inv2 = {u: _split2(inv[u]) for u in units}
            yield
        akv = {u: _mm(_split2(a_ak[u]), vbd[u]) for u in units}
        arbk = {u: _split2(jnp.concatenate([a_rb[u], a_rk[u]], axis=1)) for u in units}
        new.update(ar=ar, inv=inv2, akv=akv, arbk=arbk, vbd=vbd, v=v, bk=bk_end, pend=p_end)

    def serial():
        pair = lambda ref, n: (ref[0, n], ref[1, n])
        s_cur = [state[g] for g in groups] if chained else None
        for j in range(n_chunks):
            ns = [uid[j, g] for g in groups]
            s_prev = s_cur if chained else [expand(s0_ref[j, g]) for g in groups]
            x0 = [_mm(pair(sv_ar, n), _split2(s_prev[g]), NT) for g, n in zip(groups, ns)]
            yield
            uu = [_mm(pair(sv_inv, n), bd2(x0[g][0:c] + sv_akv[n])) for g, n in zip(groups, ns)]
            yield
            for g, n in zip(groups, ns):
                yy = x0[g][c:2 * c] + _mm(pair(sv_arbk, n), _cat2(bd2(uu[g]), pair(sv_vbd, n), 0))
                y_ref[j * c:(j + 1) * c, g * GROUP_LANES:(g + 1) * GROUP_LANES] = yy
            s_new = []
            for g, n in zip(groups, ns):
                uv_t = jnp.transpose(jnp.concatenate([uu[g], sv_v[n]], axis=0))
                upd = _mm(_split2(uv_t), pair(sv_bk, n))
                s_new.append(s_prev[g] * sv_pend[n, 0:1, :] + stmask * upd)
            yield
            if chained:
                s_cur = s_new
            else:
                for g in groups:
                    sout_ref[j, g] = compact(s_new[g])
        if chained:
            for g in groups:
                state[g] = s_cur[g]
        new['last_state'] = s_new

    _interleave(serial(), prepare())

    zero = sum(_zero_after(s[0:8, 0:LANES]) for s in new['last_state'])[0:1, 0:1]
    zero_bf = zero.astype(BF16)
    for u, n in uid.items():
        for name, ref in (('ar', sv_ar), ('inv', sv_inv), ('arbk', sv_arbk), ('vbd', sv_vbd), ('bk', sv_bk)):
            for half in range(2):
                ref[half, n] = new[name][u][half] + zero_bf
        sv_akv[n] = new['akv'][u] + zero
        sv_v[n] = new['v'][u] + zero
        sv_pend[n] = jnp.broadcast_to(new['pend'][u] + zero, sv_pend.shape[1:])

    if chained:
        @pl.when(i == n_steps)
        def _():
            for g in groups:
                sout_ref[0, g] = compact(state[g])


def _wkv_scan(rlkvab, s0, *, row0, n_rows, chunk, chunks_per_step, chained):
    rows = chunk * chunks_per_step
    n_steps = n_rows // rows
    assert n_rows % rows == 0 and row0 % rows == 0
    masks = _wkv_masks(chunk)
    hc = HEADS_PER_GROUP * chunk
    n_units = chunks_per_step * N_HEAD_GROUPS
    st = (N_HEAD_GROUPS, GROUP_LANES, RWKV_HEAD)
    prev = lambda i: jnp.maximum(i - 1, 0)
    row_in = pl.BlockSpec((rows, RWKV_DIM), lambda i: (row0 // rows + jnp.minimum(i, n_steps - 1), 0))
    if chained:
        n_state = 1
        st_spec = pl.BlockSpec((1,) + st, lambda i: (0, 0, 0, 0))
    else:
        n_state = n_rows // chunk
        st_spec = pl.BlockSpec((chunks_per_step,) + st, lambda i: (prev(i), 0, 0, 0))
    in_specs = [row_in] * 6 + [st_spec] + [_const_spec(m.shape) for m in masks]
    out_shape = [jax.ShapeDtypeStruct((n_rows, RWKV_DIM), F32),
                 jax.ShapeDtypeStruct((n_state,) + st, F32)]
    out_specs = [pl.BlockSpec((rows, RWKV_DIM), lambda i: (prev(i), 0)), st_spec]
    scratch = [pltpu.VMEM((N_HEAD_GROUPS, GROUP_LANES, GROUP_LANES), F32),
               pltpu.VMEM((2, n_units, 2 * chunk, GROUP_LANES), BF16),
               pltpu.VMEM((2, n_units, chunk, hc), BF16),
               pltpu.VMEM((n_units, chunk, GROUP_LANES), F32),
               pltpu.VMEM((2, n_units, chunk, 2 * hc), BF16),
               pltpu.VMEM((2, n_units, hc, GROUP_LANES), BF16),
               pltpu.VMEM((n_units, chunk, GROUP_LANES), F32),
               pltpu.VMEM((2, n_units, 2 * chunk, GROUP_LANES), BF16),
               pltpu.VMEM((n_units, 8, GROUP_LANES), F32)]
    kern = functools.partial(_wkv_pipe_kernel, chunk=chunk, n_steps=n_steps, chained=chained)
    return pl.pallas_call(
        kern, out_shape=out_shape, grid=(n_steps + 1,), in_specs=in_specs, out_specs=out_specs,
        scratch_shapes=scratch,
        compiler_params=pltpu.CompilerParams(dimension_semantics=("arbitrary",),
                                             vmem_limit_bytes=VMEM_LIMIT),
        name="wkv_scan")(*rlkvab, s0, *masks)


def _mixer_post_kernel(yp_ref, ys_ref, bonus_ref, g_ref, yag_ref, sgb_ref, xp_ref, xs_ref, lng_ref, lnb_ref,
                       wob_ref, wo_ref, n2_ref, wr_ref, br_ref, ones_ref, tril_ref,
                       x1_ref, h2_ref, route_ref, counts_ref, cnt, *, n_prompt_tiles):
    i = pl.program_id(0)

    @pl.when(i == 0)
    def _():
        cnt[...] = jnp.zeros(cnt.shape, F32)

    x = jnp.where(i < n_prompt_tiles, xp_ref[...], xs_ref[...])
    ones = ones_ref[...]
    y = jnp.where(i < n_prompt_tiles, yp_ref[...], ys_ref[...])
    inv_n = 1.0 / RWKV_HEAD
    mean = _split2_dot(y, ones) * inv_n
    yc = y - mean
    var = _split2_dot(yc * yc, ones) * inv_n
    yn = yc * lax.rsqrt(var + GN_EPS) * lng_ref[...] + lnb_ref[...]
    yy = (yn + bonus_ref[...]) * g_ref[...]
    y_b = jnp.dot(yy.astype(BF16), wob_ref[...], preferred_element_type=F32)
    merged = yag_ref[...] + sgb_ref[...] * y_b
    x1 = x + jnp.dot(merged.astype(BF16), wo_ref[...], preferred_element_type=F32)
    x1_ref[...] = x1
    h2 = _rms(x1, n2_ref[...])
    h2_ref[...] = _pack_bf16_pairs(h2)

    logits = _dot_stack3(h2, wr_ref[...]) + br_ref[...]
    lane = lax.broadcasted_iota(jnp.int32, logits.shape, 1)
    neg = jnp.float32(-jnp.inf)
    big = jnp.int32(LANES)
    is_g = lane < N_GROUPS
    lgp = jnp.where(is_g, logits, neg)
    m_g = jnp.max(lgp, axis=-1, keepdims=True)
    grp = jnp.min(jnp.where(lgp == m_g, lane, big), axis=-1, keepdims=True)
    p_top = 1.0 / jnp.sum(jnp.where(is_g, jnp.exp(logits - m_g), 0.0), axis=-1, keepdims=True)
    e_lane = lane - N_GROUPS
    in_grp = (e_lane >= grp * EXPERTS_PER_GROUP) & (e_lane < (grp + 1) * EXPERTS_PER_GROUP)
    le = jnp.where(in_grp, logits, neg)
    m1 = jnp.max(le, axis=-1, keepdims=True)
    i1 = jnp.min(jnp.where(le == m1, lane, big), axis=-1, keepdims=True)
    le2 = jnp.where(lane == i1, neg, le)
    m2 = jnp.max(le2, axis=-1, keepdims=True)
    i2 = jnp.min(jnp.where(le2 == m2, lane, big), axis=-1, keepdims=True)
    ex = jnp.exp(m2 - m1)
    p1 = 1.0 / (1.0 + ex)
    p2 = ex / (1.0 + ex)
    oh1 = lane == i1 - N_GROUPS
    oh2 = lane == i2 - N_GROUPS
    both = jnp.where(oh1, 1.0, jnp.where(oh2, 1.0, 0.0))
    before = jnp.dot(tril_ref[...], both.astype(BF16), preferred_element_type=F32) + cnt[0:1, :]
    rank1 = jnp.sum(jnp.where(oh1, before, 0.0), axis=-1, keepdims=True)
    rank2 = jnp.sum(jnp.where(oh2, before, 0.0), axis=-1, keepdims=True)
    cnt[0:1, :] = cnt[0:1, :] + jnp.sum(both, axis=0, keepdims=True)
    counts_ref[...] = jnp.broadcast_to(cnt[0:1, :], counts_ref.shape)

    cols = [(i1 - N_GROUPS).astype(F32), (i2 - N_GROUPS).astype(F32), p_top * p1, p_top * p2,
            rank1, rank2]
    route = jnp.zeros(logits.shape, F32)
    for c, col in enumerate(cols):
        route = jnp.where(lane == c, col, route)
    route_ref[...] = route


ROUTE_EXPERT, ROUTE_WEIGHT, ROUTE_RANK = 0, 2, 4


def _mixer_post(y_p, y_s, bonus, g, yag, sgb, x_p, x_s, p, *, tm):
    n_p, d = x_p.shape
    n_tok = n_p + x_s.shape[0]
    npt = n_p // tm
    row = lambda w: pl.BlockSpec((tm, w), lambda i: (i, 0))
    tril = jnp.asarray(np.arange(tm)[None, :] < np.arange(tm)[:, None], BF16)
    consts = [p['lnx_g'], p['lnx_b'], p['w_out_b'], p['w_o'], p['norm2_g'], p['w_router'],
              p['b_router'], p['ones_bf16'], tril]
    pair = lambda w: [pl.BlockSpec((tm, w), lambda i: (jnp.minimum(i, npt - 1), 0)),
                      pl.BlockSpec((tm, w), lambda i: (jnp.maximum(i - npt, 0), 0))]
    in_specs = (pair(RWKV_DIM) + [row(RWKV_DIM)] * 2 + [row(d)] * 2 + pair(d)
                + [_const_spec(c.shape) for c in consts])
    out_shape = [jax.ShapeDtypeStruct((n_tok, d), F32), jax.ShapeDtypeStruct((n_tok, d // 2), jnp.uint32),
                 jax.ShapeDtypeStruct((n_tok, LANES), F32), jax.ShapeDtypeStruct((8, LANES), F32)]
    out_specs = [row(d), row(d // 2), row(LANES), _const_spec((8, LANES))]
    return pl.pallas_call(
        functools.partial(_mixer_post_kernel, n_prompt_tiles=npt), out_shape=out_shape,
        grid=(n_tok // tm,), in_specs=in_specs, out_specs=out_specs,
        scratch_shapes=[pltpu.VMEM((8, LANES), F32)],
        compiler_params=pltpu.CompilerParams(dimension_semantics=("arbitrary",),
                                             vmem_limit_bytes=VMEM_LIMIT),
        name="mixer_post")(y_p, y_s, bonus, g, yag, sgb, x_p, x_s, *consts)


N_DMA_PRIORITIES = 2


def _gather_rows(idx_ref, src_hbm, dst, sem, n_rows, *, unrolled):
    def start(r, priority):
        pltpu.make_async_copy(src_hbm.at[pl.ds(idx_ref[r], 1)], dst.at[pl.ds(r, 1)],
                              sem).start(priority=priority)
    if unrolled:
        for r in range(n_rows):
            start(r, r % N_DMA_PRIORITIES)
    else:
        def body(r, carry):
            start(r, 0)
            return carry
        lax.fori_loop(0, n_rows, body, 0)


def _wait_rows(src_hbm, dst, sem, n_rows):
    pltpu.make_async_copy(src_hbm.at[pl.ds(0, n_rows)], dst, sem).wait()


def _moe_scatter_kernel(dest_ref, h_ref, xs_in_hbm, xs_hbm, sem):
    del xs_in_hbm
    n = dest_ref.shape[-1]
    tm = h_ref.shape[0]
    for r in range(n):
        pltpu.make_async_copy(h_ref.at[pl.ds(r % tm, 1)], xs_hbm.at[pl.ds(dest_ref[0, 0, r], 1)],
                              sem).start(priority=r % N_DMA_PRIORITIES)
    for j in range(n // tm):
        pltpu.make_async_copy(h_ref, xs_hbm.at[pl.ds(0, tm)], sem).wait()


def _moe_scatter(h2, pos3, n_rows):
    n_tok, d = h2.shape
    n_tiles, _, n = pos3.shape
    tm = n // TOP_K
    xs0 = jnp.zeros((n_rows, d), h2.dtype)
    return pl.pallas_call(
        _moe_scatter_kernel, out_shape=jax.ShapeDtypeStruct((n_rows, d), h2.dtype), grid=(n_tiles,),
        in_specs=[pl.BlockSpec((1, 1, n), lambda i: (i, 0, 0), memory_space=pltpu.SMEM),
                  pl.BlockSpec((tm, d), lambda i: (i, 0)), pl.BlockSpec(memory_space=pl.ANY)],
        out_specs=pl.BlockSpec(memory_space=pl.ANY),
        scratch_shapes=[pltpu.SemaphoreType.DMA(())], input_output_aliases={2: 0},
        compiler_params=pltpu.CompilerParams(dimension_semantics=("arbitrary",), has_side_effects=True,
                                             vmem_limit_bytes=VMEM_LIMIT),
        name="moe_scatter")(pos3, h2, xs0)


def _moe_experts_kernel(bexp_ref, nused_ref, x_ref, wg_ref, wu_ref, wd_ref, yb_ref, wg_bf, wu_bf, wd_bf):
    i = pl.program_id(0)

    @pl.when((i == 0) | (bexp_ref[i] != bexp_ref[jnp.maximum(i - 1, 0)]))
    def _():
        wg_bf[...] = wg_ref[0].astype(BF16)
        wu_bf[...] = wu_ref[0].astype(BF16)
        wd_bf[...] = wd_ref[0].astype(BF16)

    @pl.when(i < nused_ref[0])
    def _():
        xb = _unpack_bf16_pairs(x_ref[...]).astype(BF16)
        hg = jnp.dot(xb, wg_bf[...], preferred_element_type=F32)
        hu = jnp.dot(xb, wu_bf[...], preferred_element_type=F32)
        hid = (hg * jax.nn.sigmoid(hg)) * hu
        yb_ref[...] = _pack_bf16_pairs(jnp.dot(hid.astype(BF16), wd_bf[...], preferred_element_type=F32))

    @pl.when(i >= nused_ref[0])
    def _():
        yb_ref[...] = jnp.zeros(yb_ref.shape, yb_ref.dtype)


def _moe_experts(xs, block_expert, n_used, w_eg, w_eu, w_ed, *, blk):
    n_rows, dp = xs.shape
    n_blocks = n_rows // blk
    d, de = w_eg.shape[1:]
    grid_spec = pltpu.PrefetchScalarGridSpec(
        num_scalar_prefetch=2, grid=(n_blocks,),
        in_specs=[
            pl.BlockSpec((blk, dp), lambda i, be, nu: (jnp.minimum(i, nu[0] - 1), 0)),
            pl.BlockSpec((1, d, de), lambda i, be, nu: (be[i], 0, 0)),
            pl.BlockSpec((1, d, de), lambda i, be, nu: (be[i], 0, 0)),
            pl.BlockSpec((1, de, d), lambda i, be, nu: (be[i], 0, 0)),
        ],
        out_specs=pl.BlockSpec((blk, dp), lambda i, be, nu: (i, 0)),
        scratch_shapes=[pltpu.VMEM((d, de), BF16), pltpu.VMEM((d, de), BF16), pltpu.VMEM((de, d), BF16)])
    return pl.pallas_call(
        _moe_experts_kernel, out_shape=jax.ShapeDtypeStruct((n_rows, dp), xs.dtype), grid_spec=grid_spec,
        compiler_params=pltpu.CompilerParams(dimension_semantics=("arbitrary",),
                                             vmem_limit_bytes=VMEM_LIMIT),
        name="moe_experts")(block_expert, n_used, xs, w_eg, w_eu, w_ed)


GATHER_AHEAD = 2


def _with_dummy_blocks(idx, n_blocks, blk):
    pad = jnp.zeros((GATHER_AHEAD * blk,), idx.dtype)
    return jnp.concatenate([idx, pad]).reshape(n_blocks + GATHER_AHEAD, 1, blk)


def _moe_combine_kernel(*refs, n_prompt_tiles):
    pos_refs = refs[:GATHER_AHEAD + 1]
    x1_ref, route_ref, nf_ref, yb_hbm, outp_ref, outs_ref, ybuf, sems = refs[GATHER_AHEAD + 1:]
    i = pl.program_id(0)
    n_slots = GATHER_AHEAD + 1
    rows = ybuf.shape[1]
    slot = i % n_slots

    @pl.when(i == 0)
    def _():
        for a in range(GATHER_AHEAD):
            _gather_rows(pos_refs[a].at[0, 0], yb_hbm, ybuf.at[a], sems.at[a], rows, unrolled=False)

    _wait_rows(yb_hbm, ybuf.at[slot], sems.at[slot], rows)
    ahead = (i + GATHER_AHEAD) % n_slots
    _gather_rows(pos_refs[GATHER_AHEAD].at[0, 0], yb_hbm, ybuf.at[ahead], sems.at[ahead], rows, unrolled=True)
    tm = x1_ref.shape[0]
    route = route_ref[...]
    x2 = x1_ref[...]
    for j in range(TOP_K):
        yj = _unpack_bf16_pairs(ybuf[slot, j * tm:(j + 1) * tm, :])
        x2 = x2 + yj * route[:, ROUTE_WEIGHT + j:ROUTE_WEIGHT + j + 1]
    out = _rms(x2, nf_ref[...])

    @pl.when(i < n_prompt_tiles)
    def _():
        outp_ref[...] = out

    @pl.when(i >= n_prompt_tiles)
    def _():
        outs_ref[...] = out

    @pl.when(i == pl.num_programs(0) - 1)
    def _():
        for a in range(1, n_slots):
            s = (i + a) % n_slots
            _wait_rows(yb_hbm, ybuf.at[s], sems.at[s], rows)


def _moe_combine(pos_tiles, x1, route, normf_g, yb, *, n_prompt_rows, tm):
    n_tok, d = x1.shape
    n_tiles = n_tok // tm
    npt = n_prompt_rows // tm
    pos3 = _with_dummy_blocks(pos_tiles.reshape(-1), n_tiles, TOP_K * tm)
    pos_spec = lambda a: pl.BlockSpec((1, 1, TOP_K * tm), lambda i: (i + a, 0, 0), memory_space=pltpu.SMEM)
    in_specs = [pos_spec(a) for a in range(GATHER_AHEAD + 1)] + [
        pl.BlockSpec((tm, d), lambda i: (i, 0)),
        pl.BlockSpec((tm, LANES), lambda i: (i, 0)),
        _const_spec(normf_g.shape),
        pl.BlockSpec(memory_space=pl.ANY),
    ]
    out_shape = [jax.ShapeDtypeStruct((n_prompt_rows, d), F32),
                 jax.ShapeDtypeStruct((n_tok - n_prompt_rows, d), F32)]
    out_specs = [pl.BlockSpec((tm, d), lambda i: (jnp.minimum(i, npt - 1), 0)),
                 pl.BlockSpec((tm, d), lambda i: (jnp.maximum(i - npt, 0), 0))]
    return pl.pallas_call(
        functools.partial(_moe_combine_kernel, n_prompt_tiles=npt), out_shape=out_shape,
        grid=(n_tiles,), in_specs=in_specs, out_specs=out_specs,
        scratch_shapes=[pltpu.VMEM((GATHER_AHEAD + 1, TOP_K * tm, yb.shape[1]), yb.dtype),
                        pltpu.SemaphoreType.DMA((GATHER_AHEAD + 1,))],
        compiler_params=pltpu.CompilerParams(dimension_semantics=("arbitrary",),
                                             vmem_limit_bytes=VMEM_LIMIT),
        name="moe_combine")(*([pos3] * (GATHER_AHEAD + 1)), x1, route, normf_g, yb)


def _dispatch(route, counts, blk, tm):
    n_tok = route.shape[0]
    n_assign = n_tok * TOP_K
    expert = route[:, ROUTE_EXPERT:ROUTE_EXPERT + TOP_K].astype(jnp.int32)
    rank = route[:, ROUTE_RANK:ROUTE_RANK + TOP_K].astype(jnp.int32)
    counts = counts[0, :N_EXPERTS].astype(jnp.int32)
    padded = (counts + blk - 1) // blk * blk
    pad_end = jnp.cumsum(padded)
    pad_start = pad_end - padded
    is_e = expert[:, :, None] == jnp.arange(N_EXPERTS, dtype=jnp.int32)
    dest = jnp.sum(jnp.where(is_e, pad_start, 0), axis=-1) + rank
    n_blocks = -(-n_assign // blk) + N_EXPERTS
    block_start = jnp.arange(n_blocks, dtype=jnp.int32) * blk
    block_expert = jnp.minimum(jnp.sum(block_start[:, None] >= pad_end[None, :], axis=1),
                               N_EXPERTS - 1).astype(jnp.int32)
    n_used = (pad_end[-1] // blk).reshape(1)
    pos_tiles = dest.reshape(n_tok // tm, tm, TOP_K).transpose(0, 2, 1).reshape(n_tok // tm, 1, TOP_K * tm)
    return pos_tiles, block_expert, n_used, n_blocks


def _state_to_kernel(s):
    return s.reshape(s.shape[0], N_HEAD_GROUPS, GROUP_LANES, RWKV_HEAD)


def _state_from_kernel(s):
    return s.reshape(s.shape[0], RWKV_HEADS, RWKV_HEAD, RWKV_HEAD)


def kernel(x_prompt, x_sample, state_conv, state_shift, state_wkv, norm1_g, w_in, conv_w, mu_shift, w0, w_lora_w, a0, w_lora_a, w_lora_g, k_k, k_a, r_k, lnx_g, lnx_b, w_out_a, w_out_b, w_o, norm2_g, w_router_group, b_router_group, w_router_expert, b_router_expert, w_e_gate, w_e_up, w_e_down, normf_g):
    depth = norm1_g.shape[0]
    bp, seq, d = x_prompt.shape
    db, dseq, _ = x_sample.shape
    assert depth == 1 and bp == 1, "single layer, single prompt stream"
    tm = ROW_TILE
    n_p, n_s = bp * seq, db * dseq
    assert n_p % tm == 0 and n_s % tm == 0 and tm % dseq == 0
    n_prompt_tiles, n_sample_tiles = n_p // tm, n_s // tm
    seqs = tm // dseq
    n_tok = n_p + n_s

    x_p = x_prompt.reshape(n_p, d)
    x_s = x_sample.reshape(n_s, d)

    l = 0
    c3 = 3 * CONV_DIM
    head_id = np.arange(GROUP_LANES) // RWKV_HEAD
    ones_bf16 = jnp.asarray(head_id[:, None] == head_id[None, :], BF16)
    zpad = jnp.zeros((LORA_W, RWKV_DIM), F32)
    n_r = N_GROUPS + N_EXPERTS
    p = {
        'norm1_g': norm1_g[l].reshape(1, d),
        'w_in_a': w_in[l][:, :c3].astype(BF16),
        'w_in_b': w_in[l][:, c3:c3 + SHIFT_DIM].astype(BF16),
        'w_in_g': w_in[l][:, c3 + SHIFT_DIM:].astype(BF16),
        'conv_w': conv_w[l],
        'mu_shift': mu_shift[l].reshape(1, SHIFT_DIM),
        'w0': w0[l].reshape(1, RWKV_DIM),
        'w_lora_w': _stack3(jnp.concatenate([w_lora_w[l], zpad], axis=0)),
        'a0': a0[l].reshape(1, RWKV_DIM),
        'w_lora_a': _stack3(jnp.concatenate([zpad, w_lora_a[l]], axis=0)),
        'w_lora_g': _stack3(w_lora_g[l]),
        'k_k': k_k[l].reshape(1, RWKV_DIM),
        'k_a': k_a[l].reshape(1, RWKV_DIM),
        'r_k': r_k[l].reshape(1, RWKV_DIM),
        'w_out_a': w_out_a[l].astype(BF16),
        'ones_bf16': ones_bf16,
        'lnx_g': lnx_g[l].reshape(1, RWKV_DIM),
        'lnx_b': lnx_b[l].reshape(1, RWKV_DIM),
        'w_out_b': w_out_b[l].astype(BF16),
        'w_o': w_o[l].astype(BF16),
        'norm2_g': norm2_g[l].reshape(1, d),
        'w_router': _stack3(jnp.pad(jnp.concatenate([w_router_group[l], w_router_expert[l]], axis=1),
                                    ((0, 0), (0, LANES - n_r)))),
        'b_router': jnp.pad(jnp.concatenate([b_router_group[l], b_router_expert[l]]),
                            (0, LANES - n_r)).reshape(1, LANES),
    }
    st_conv_t = state_conv[l].reshape(n_sample_tiles, seqs * (CONV_WIDTH - 1), CONV_DIM)
    st_shift_t = state_shift[l].reshape(n_sample_tiles, seqs, SHIFT_DIM)

    (yag, sgb, r, w, k, v, a, b, bonus, g, ctail, stail) = _mixer_pre(
        x_p, x_s, st_conv_t, st_shift_t, p, seq_len=dseq, tm=tm)

    s0_prompt = jnp.zeros((1, N_HEAD_GROUPS, GROUP_LANES, RWKV_HEAD), F32)
    s0_sample = _state_to_kernel(state_wkv[l])
    rlkvab = (r, w, k, v, a, b)
    cp = min(SCAN_CHUNK, n_p)
    assert cp & (cp - 1) == 0 and dseq & (dseq - 1) == 0
    y_p, s_p = _wkv_scan(rlkvab, s0_prompt, row0=0, n_rows=n_p, chunk=cp,
                         chunks_per_step=SCAN_CHUNKS_PER_STEP, chained=True)
    y_s, s_s = _wkv_scan(rlkvab, s0_sample, row0=n_p, n_rows=n_s, chunk=dseq,
                         chunks_per_step=SCAN_CHUNKS_PER_STEP, chained=False)

    x1, h2, route, counts = _mixer_post(y_p, y_s, bonus, g, yag, sgb, x_p, x_s, p, tm=tm)

    blk = MOE_ROWS
    pos_tiles, block_expert, n_used, n_blocks = _dispatch(route, counts, blk, tm)
    xs = _moe_scatter(h2, pos_tiles, n_blocks * blk)
    yb = _moe_experts(xs, block_expert, n_used, w_e_gate[l], w_e_up[l], w_e_down[l], blk=blk)
    out_p, out_s = _moe_combine(pos_tiles, x1, route, normf_g.reshape(1, d), yb, n_prompt_rows=n_p, tm=tm)

    y_prompt = out_p.reshape(bp, seq, d)
    y_sample = out_s.reshape(db, dseq, d)
    conv_p = ctail[n_prompt_tiles - 1, 2 * (seqs - 1):2 * seqs].reshape(1, bp, CONV_WIDTH - 1, CONV_DIM)
    shift_p = stail[n_prompt_tiles - 1, seqs - 1].reshape(1, bp, 1, SHIFT_DIM)
    wkv_p = _state_from_kernel(s_p).reshape(1, bp, RWKV_HEADS, RWKV_HEAD, RWKV_HEAD)
    conv_s = ctail[n_prompt_tiles:].reshape(1, db, CONV_WIDTH - 1, CONV_DIM)
    shift_s = stail[n_prompt_tiles:].reshape(1, db, 1, SHIFT_DIM)
    wkv_s = _state_from_kernel(s_s).reshape(1, db, RWKV_HEADS, RWKV_HEAD, RWKV_HEAD)
    return (y_prompt, y_sample, conv_p, shift_p, wkv_p, conv_s, shift_s, wkv_s)
```

```python
import functools

import numpy as np
import jax
import jax.numpy as jnp
from jax import lax
from jax.experimental import pallas as pl
from jax.experimental.pallas import tpu as pltpu

F32 = jnp.float32
BF16 = jnp.bfloat16

CONV_DIM = 512
CONV_WIDTH = 3
RWKV_HEAD = 64
RWKV_HEADS = 8
RWKV_DIM = RWKV_HEADS * RWKV_HEAD
LORA_W = 64
LORA_A = 64
LORA_G = 128
SHIFT_DIM = 3 * RWKV_DIM + LORA_W + LORA_A + LORA_G
N_GROUPS = 4
EXPERTS_PER_GROUP = 8
N_EXPERTS = N_GROUPS * EXPERTS_PER_GROUP
TOP_K = 2
RMS_EPS = 1e-6
GN_EPS = 64e-5

LANES = 128
ROW_TILE = 256
SCAN_CHUNK = 64
SCAN_CHUNKS_PER_STEP = 4
MOE_ROWS = 256
HIST = 8
VMEM_LIMIT = 56 * 1024 * 1024


def _rms(x, g):
    return x * lax.rsqrt(jnp.mean(x * x, axis=-1, keepdims=True) + RMS_EPS) * g


def _split2_dot(x, ones_bf16):
    w = ones_bf16.shape[0]
    hi = x.astype(BF16)
    lo = (x - hi.astype(F32)).astype(BF16)
    parts = []
    for c0 in range(0, x.shape[1], w):
        both = jnp.concatenate([hi[:, c0:c0 + w], lo[:, c0:c0 + w]], axis=0)
        s = jnp.dot(both, ones_bf16, preferred_element_type=F32)
        parts.append(s[0:x.shape[0]] + s[x.shape[0]:])
    return jnp.concatenate(parts, axis=1)


def _pack_bf16_pairs(x):
    n = x.shape[1] // 2
    lo = pltpu.bitcast(x[:, :n].astype(BF16).astype(F32), jnp.uint32)
    hi = pltpu.bitcast(x[:, n:].astype(BF16).astype(F32), jnp.uint32)
    return hi | (lo >> 16)


def _unpack_bf16_pairs(p):
    lo = pltpu.bitcast(p << 16, F32)
    hi = pltpu.bitcast(p & jnp.uint32(0xFFFF0000), F32)
    return jnp.concatenate([lo, hi], axis=1)


def _stack3(w):
    hi = w.astype(BF16)
    lo = (w - hi.astype(F32)).astype(BF16)
    return jnp.concatenate([hi, hi, lo], axis=0)


def _dot_stack3(x, w_stack):
    hi = x.astype(BF16)
    lo = (x - hi.astype(F32)).astype(BF16)
    return jnp.dot(jnp.concatenate([hi, lo, hi], axis=1), w_stack, preferred_element_type=F32)


def _const_spec(shape):
    nd = len(shape)
    return pl.BlockSpec(shape, lambda *_: (0,) * nd)


def _interleave(*gens):
    live = list(gens)
    while live:
        for gen in list(live):
            try:
                next(gen)
            except StopIteration:
                live.remove(gen)


def _zero_after(x):
    return pltpu.bitcast(lax.shift_right_logical(pltpu.bitcast(x, jnp.uint32), jnp.uint32(32)), F32)


def _mixer_pre_tile(x_ref, stc_ref, sts_ref, n1_ref, wa_ref, wb_ref, wg_ref, convw_ref, mu_ref,
                    w0_ref, lw_ref, a0_ref, la_ref, lgw_ref, kk_ref, ka_ref, rk_ref, woa_ref, ones_ref,
                    yag_ref, sgb_ref, r_ref, w_ref, k_ref, v_ref, a_ref, b_ref, bonus_ref, g_ref,
                    ctail_ref, stail_ref, ccarry, scarry, *, sample, seq_len):
    tm = x_ref.shape[0]
    seqs = tm // seq_len
    h = _rms(x_ref[...], n1_ref[...]).astype(BF16)

    def prev_rows(val, k, heads):
        row = lax.broadcasted_iota(jnp.int32, val.shape, 0)
        out = pltpu.roll(val, k, axis=0)
        for r0, head in heads.items():
            out = jnp.where(row == r0, head, out)
        return out

    pb = jnp.dot(h, wb_ref[...], preferred_element_type=F32)
    old_s = scarry[...]
    if sample:
        s_heads = {j * seq_len: sts_ref[0, j:j + 1, :] for j in range(seqs)}
    else:
        s_heads = {0: old_s[HIST - 1:HIST, :]}
    prev = prev_rows(pb, 1, s_heads)
    for j in range(seqs):
        r1 = (j + 1) * seq_len
        stail_ref[0, j:j + 1, :] = pb[r1 - 1:r1, :]
    scarry[...] = pb[tm - HIST:tm, :] + _zero_after(old_s)

    def conv_and_gates():
        n_a = wa_ref.shape[1] // 3
        g_in = jnp.dot(h, wa_ref[:, 0:n_a], preferred_element_type=F32)
        yield
        g_out = jnp.dot(h, wa_ref[:, n_a:2 * n_a], preferred_element_type=F32)
        yield
        x_c = jnp.dot(h, wa_ref[:, 2 * n_a:3 * n_a], preferred_element_type=F32)
        bx = g_in * x_c
        old_c = ccarry[...]
        if sample:
            h1 = {j * seq_len: stc_ref[0, 2 * j + 1:2 * j + 2, :] for j in range(seqs)}
            h2 = {j * seq_len: stc_ref[0, 2 * j:2 * j + 1, :] for j in range(seqs)}
            h2.update({j * seq_len + 1: stc_ref[0, 2 * j + 1:2 * j + 2, :] for j in range(seqs)})
        else:
            h1 = {0: old_c[HIST - 1:HIST, :]}
            h2 = {0: old_c[HIST - 2:HIST - 1, :], 1: old_c[HIST - 1:HIST, :]}
        cw = convw_ref[...]
        conv = cw[0:1, :] * prev_rows(bx, 2, h2) + cw[1:2, :] * prev_rows(bx, 1, h1) + cw[2:3, :] * bx
        for j in range(seqs):
            r1 = (j + 1) * seq_len
            ctail_ref[0, 2 * j:2 * j + 2, :] = bx[r1 - 2:r1, :]
        ccarry[...] = bx[tm - HIST:tm, :] + _zero_after(old_c)
        yield
        y_a = jnp.dot((g_out * conv).astype(BF16), woa_ref[...], preferred_element_type=F32)
        yield
        d = wg_ref.shape[1] // 2
        half = d // 2
        for c0 in range(0, d, half):
            pg = jnp.dot(h, wg_ref[:, c0:c0 + half], preferred_element_type=F32)
            yag_ref[:, c0:c0 + half] = (jax.nn.sigmoid(pg) * y_a[:, c0:c0 + half]).astype(yag_ref.dtype)
            yield
        for c0 in range(0, d, half):
            pg = jnp.dot(h, wg_ref[:, d + c0:d + c0 + half], preferred_element_type=F32)
            sgb_ref[:, c0:c0 + half] = jax.nn.sigmoid(pg).astype(sgb_ref.dtype)
            yield

    def rwkv_pre():
        s = pb + (prev - pb) * mu_ref[...]
        o1, o2, o3 = RWKV_DIM, 2 * RWKV_DIM, 3 * RWKV_DIM
        r = s[:, 0:o1]
        k = s[:, o1:o2]
        v = s[:, o2:o3]
        s_l = s[:, o3:o3 + LORA_W + LORA_A]
        lg = s[:, o3 + LORA_W + LORA_A:]
        r_ref[...] = r
        v_ref[...] = v
        yield
        z = w0_ref[...] + _dot_stack3(jnp.tanh(s_l), lw_ref[...])
        w_log = -jax.nn.softplus(-z) - 0.5
        w_ref[...] = -jnp.exp(w_log)
        yield
        a = jax.nn.sigmoid(a0_ref[...] + _dot_stack3(s_l, la_ref[...]))
        yield
        g_ref[...] = _dot_stack3(jax.nn.sigmoid(lg), lgw_ref[...]).astype(g_ref.dtype)
        yield
        ones = ones_ref[...]
        kk = k * kk_ref[...]
        kk_n = kk / jnp.maximum(jnp.sqrt(_split2_dot(kk * kk, ones)), 1e-12)
        a_ref[...] = -kk_n
        b_ref[...] = kk_n * a
        yield
        k2 = k * (1.0 + (a - 1.0) * ka_ref[...])
        k_ref[...] = k2
        yield
        bonus_ref[...] = (_split2_dot(r * k2 * rk_ref[...], ones) * v).astype(bonus_ref.dtype)

    _interleave(conv_and_gates(), rwkv_pre())


def _mixer_pre_kernel(xp_ref, xs_ref, *refs, n_prompt_tiles, seq_len):
    i = pl.program_id(0)
    ccarry, scarry = refs[-2:]

    @pl.when(i == 0)
    def _():
        ccarry[...] = jnp.zeros(ccarry.shape, F32)
        scarry[...] = jnp.zeros(scarry.shape, F32)

    pl.when(i < n_prompt_tiles)(
        functools.partial(_mixer_pre_tile, xp_ref, *refs, sample=False, seq_len=seq_len))
    pl.when(i >= n_prompt_tiles)(
        functools.partial(_mixer_pre_tile, xs_ref, *refs, sample=True, seq_len=seq_len))


def _mixer_pre(x_p, x_s, st_conv_t, st_shift_t, p, *, seq_len, tm):
    n_p, d = x_p.shape
    n_tok = n_p + x_s.shape[0]
    n_prompt_tiles = n_p // tm
    n_tiles = n_tok // tm
    seqs = tm // seq_len
    row = lambda w: pl.BlockSpec((tm, w), lambda i: (i, 0))
    st_idx = lambda i: (jnp.maximum(i - n_prompt_tiles, 0), 0, 0)
    consts = [p['norm1_g'], p['w_in_a'], p['w_in_b'], p['w_in_g'], p['conv_w'], p['mu_shift'],
              p['w0'], p['w_lora_w'], p['a0'], p['w_lora_a'], p['w_lora_g'], p['k_k'], p['k_a'],
              p['r_k'], p['w_out_a'], p['ones_bf16']]
    in_specs = [pl.BlockSpec((tm, d), lambda i: (jnp.minimum(i, n_prompt_tiles - 1), 0)),
                pl.BlockSpec((tm, d), lambda i: (jnp.maximum(i - n_prompt_tiles, 0), 0)),
                pl.BlockSpec((1, 2 * seqs, CONV_DIM), st_idx),
                pl.BlockSpec((1, seqs, SHIFT_DIM), st_idx)] + [_const_spec(c.shape) for c in consts]
    sds = lambda w, dt=F32: jax.ShapeDtypeStruct((n_tok, w), dt)
    out_shape = [sds(d, BF16), sds(d, BF16)] + [sds(RWKV_DIM)] * 6 + [sds(RWKV_DIM, BF16)] * 2 + [
        jax.ShapeDtypeStruct((n_tiles, 2 * seqs, CONV_DIM), F32),
        jax.ShapeDtypeStruct((n_tiles, seqs, SHIFT_DIM), F32)]
    out_specs = [row(d), row(d)] + [row(RWKV_DIM)] * 8 + [
        pl.BlockSpec((1, 2 * seqs, CONV_DIM), lambda i: (i, 0, 0)),
        pl.BlockSpec((1, seqs, SHIFT_DIM), lambda i: (i, 0, 0))]
    kern = functools.partial(_mixer_pre_kernel, n_prompt_tiles=n_prompt_tiles, seq_len=seq_len)
    return pl.pallas_call(
        kern, out_shape=out_shape, grid=(n_tiles,), in_specs=in_specs, out_specs=out_specs,
        scratch_shapes=[pltpu.VMEM((HIST, CONV_DIM), F32), pltpu.VMEM((HIST, SHIFT_DIM), F32)],
        compiler_params=pltpu.CompilerParams(dimension_semantics=("arbitrary",),
                                             vmem_limit_bytes=VMEM_LIMIT),
        name="mixer_pre")(x_p, x_s, st_conv_t, st_shift_t, *consts)


GROUP_LANES = 256
HEADS_PER_GROUP = GROUP_LANES // RWKV_HEAD
N_HEAD_GROUPS = RWKV_DIM // GROUP_LANES
NN = (((1,), (0,)), ((), ()))
NT = (((1,), (1,)), ((), ()))


def _split2(x):
    hi = x.astype(BF16)
    lo = (x - hi.astype(F32)).astype(BF16)
    return hi, lo


def _mm(xs, ys, dims=NN):
    x1, x2 = xs
    y1, y2 = ys
    d = lambda p, q: lax.dot_general(p, q, dims, preferred_element_type=F32)
    m = x1.shape[0]
    both = d(jnp.concatenate([x1, x2], axis=0), y1)
    return both[0:m] + both[m:2 * m] + d(x1, y2)


def _cat2(ps, qs, axis):
    return tuple(jnp.concatenate([p, q], axis=axis) for p, q in zip(ps, qs))


def _wkv_masks(c):
    hc = HEADS_PER_GROUP * c
    levels = c.bit_length() - 1
    t = np.arange(c)[:, None]
    s = (np.arange(hc) % c)[None, :]
    tm = [s < t, s <= t]
    for lvl in range(1, levels + 1):
        half = 1 << (lvl - 1)
        tm.append(((t >> lvl) == (s >> lvl)) & ((t & half) != 0) & ((s & half) == 0))
    row_head = (np.arange(hc) // c)[:, None]
    mfeat = row_head == (np.arange(GROUP_LANES) // RWKV_HEAD)[None, :]
    mpos = row_head == (np.arange(hc) // c)[None, :]
    lane_head = np.arange(GROUP_LANES) // RWKV_HEAD
    stmask = lane_head[:, None] == lane_head[None, :]
    tri = np.arange(c)[None, :] <= np.arange(c)[:, None]
    return (jnp.asarray(mfeat, BF16), jnp.asarray(mpos, BF16), jnp.asarray(np.stack(tm), F32),
            jnp.asarray(stmask, F32), jnp.asarray(tri, BF16))


def _wkv_pipe_kernel(r_ref, lw_ref, k_ref, v_ref, a_ref, b_ref, s0_ref, mfeat_ref, mpos_ref,
                     tmask_ref, stmask_ref, tri_ref, y_ref, sout_ref,
                     state, sv_ar, sv_inv, sv_akv, sv_arbk, sv_vbd, sv_v, sv_bk, sv_pend,
                     *, chunk, n_steps, chained):
    i = pl.program_id(0)
    c = chunk
    n_chunks = r_ref.shape[0] // c
    hc = HEADS_PER_GROUP * c
    levels = c.bit_length() - 1
    groups = range(N_HEAD_GROUPS)
    units = [(j, g) for j in range(n_chunks) for g in groups]
    uid = {u: n for n, u in enumerate(units)}

    stmask = stmask_ref[...]

    def expand(sc):
        pair = jnp.concatenate([sc, sc], axis=1)
        return jnp.concatenate([pair] * (GROUP_LANES // pair.shape[1]), axis=1) * stmask

    def compact(s):
        half = s[:, 0:GROUP_LANES // 2] + s[:, GROUP_LANES // 2:]
        return half[:, 0:RWKV_HEAD] + half[:, RWKV_HEAD:]

    @pl.when(i == 0)
    def _():
        for ref in (sv_ar, sv_inv, sv_akv, sv_arbk, sv_vbd, sv_v, sv_bk):
            ref[...] = jnp.zeros(ref.shape, ref.dtype)
        sv_pend[...] = jnp.ones(sv_pend.shape, F32)
        if chained:
            for g in groups:
                state[g] = expand(s0_ref[0, g])

    mfeat = mfeat_ref[...]
    mpos = mpos_ref[...]
    strict = tmask_ref[0]
    incl = tmask_ref[1]
    eye = incl - strict
    tri = tri_ref[...]

    def ld(ref, j, g):
        return ref[j * c:(j + 1) * c, g * GROUP_LANES:(g + 1) * GROUP_LANES]

    def bd_split(ps):
        mask = mpos if ps[0].shape[1] == hc else mfeat
        return tuple(jnp.concatenate([p] * HEADS_PER_GROUP, axis=0) * mask for p in ps)

    def bd2(m):
        return bd_split(_split2(m))

    def cumsum_rows(x):
        p1 = x.astype(BF16)
        r1 = x - p1.astype(F32)
        p2 = r1.astype(BF16)
        p3 = (r1 - p2.astype(F32)).astype(BF16)
        d = lambda q: jnp.dot(tri, q, preferred_element_type=F32)
        return d(p1) + d(p2) + d(p3)

    new = {}

    def prepare():
        cum = {u: cumsum_rows(ld(lw_ref, *u)) for u in units}
        yield
        ar, bk_end, p_end, a_ab, a_ak, a_rb, a_rk, v, vbd = ({} for _ in range(9))
        for u in units:
            cm = cum[u]
            cum_last = cm[c - 1:c, :]
            e_neg = jnp.exp(-cm)
            e_end = jnp.exp(cum_last - cm)
            b_raw = ld(b_ref, *u)
            k_raw = ld(k_ref, *u)
            ar[u] = _split2(jnp.concatenate([ld(a_ref, *u) * jnp.exp(cm - ld(lw_ref, *u)),
                                             ld(r_ref, *u) * jnp.exp(cm)], axis=0))
            bk_end[u] = _split2(jnp.concatenate([b_raw * e_end, k_raw * e_end], axis=0))
            p_end[u] = jnp.exp(cum_last)
            v[u] = ld(v_ref, *u)
            vbd[u] = bd2(v[u])
            gram = _mm(ar[u], _cat2(bd2(b_raw * e_neg), bd2(k_raw * e_neg), 0), NT)
            a_ab[u] = jnp.where(strict > 0, gram[0:c, 0:hc], 0.0)
            a_ak[u] = jnp.where(strict > 0, gram[0:c, hc:2 * hc], 0.0)
            a_rb[u] = jnp.where(incl > 0, gram[c:2 * c, 0:hc], 0.0)
            a_rk[u] = jnp.where(incl > 0, gram[c:2 * c, hc:2 * hc], 0.0)
            if uid[u] % 2 == 1:
                yield
        a_ab2 = {u: _split2(a_ab[u]) for u in units}
        inv = {u: eye + a_ab[u] * tmask_ref[2] for u in units}
        inv2 = {u: _split2(inv[u]) for u in units}
        for lvl in range(2, levels + 1):
            lm = tmask_ref[1 + lvl]
            t1 = {u: _mm(a_ab2[u], bd_split(inv2[u])) for u in units}
            yield
            inv = {u: inv[u] + lm * _mm(inv2[u], bd2(t1[u])) for u in units}
            inv2 = {u: _split2(inv[u]) for u in units}
            yield
        akv = {u: _mm(_split2(a_ak[u]), vbd[u]) for u in units}
        arbk = {u: _split2(jnp.concatenate([a_rb[u], a_rk[u]], axis=1)) for u in units}
        new.update(ar=ar, inv=inv2, akv=akv, arbk=arbk, vbd=vbd, v=v, bk=bk_end, pend=p_end)

    def serial():
        pair = lambda ref, n: (ref[0, n], ref[1, n])
        s_cur = [state[g] for g in groups] if chained else None
        for j in range(n_chunks):
            ns = [uid[j, g] for g in groups]
            s_prev = s_cur if chained else [expand(s0_ref[j, g]) for g in groups]
            x0 = [_mm(pair(sv_ar, n), _split2(s_prev[g]), NT) for g, n in zip(groups, ns)]
            yield
            uu = [_mm(pair(sv_inv, n), bd2(x0[g][0:c] + sv_akv[n])) for g, n in zip(groups, ns)]
            yield
            for g, n in zip(groups, ns):
                yy = x0[g][c:2 * c] + _mm(pair(sv_arbk, n), _cat2(bd2(uu[g]), pair(sv_vbd, n), 0))
                y_ref[j * c:(j + 1) * c, g * GROUP_LANES:(g + 1) * GROUP_LANES] = yy
            s_new = []
            for g, n in zip(groups, ns):
                uv_t = jnp.transpose(jnp.concatenate([uu[g], sv_v[n]], axis=0))
                upd = _mm(_split2(uv_t), pair(sv_bk, n))
                s_new.append(s_prev[g] * sv_pend[n, 0:1, :] + stmask * upd)
            yield
            if chained:
                s_cur = s_new
            else:
                for g in groups:
                    sout_ref[j, g] = compact(s_new[g])
        if chained:
            for g in groups:
                state[g] = s_cur[g]
        new['last_state'] = s_new

    _interleave(serial(), prepare())

    zero = sum(_zero_after(s[0:8, 0:LANES]) for s in new['last_state'])[0:1, 0:1]
    zero_bf = zero.astype(BF16)
    for u, n in uid.items():
        for name, ref in (('ar', sv_ar), ('inv', sv_inv), ('arbk', sv_arbk), ('vbd', sv_vbd), ('bk', sv_bk)):
            for half in range(2):
                ref[half, n] = new[name][u][half] + zero_bf
        sv_akv[n] = new['akv'][u] + zero
        sv_v[n] = new['v'][u] + zero
        sv_pend[n] = jnp.broadcast_to(new['pend'][u] + zero, sv_pend.shape[1:])

    if chained:
        @pl.when(i == n_steps)
        def _():
            for g in groups:
                sout_ref[0, g] = compact(state[g])


def _wkv_scan(rlkvab, s0, *, row0, n_rows, chunk, chunks_per_step, chained):
    rows = chunk * chunks_per_step
    n_steps = n_rows // rows
    assert n_rows % rows == 0 and row0 % rows == 0
    masks = _wkv_masks(chunk)
    hc = HEADS_PER_GROUP * chunk
    n_units = chunks_per_step * N_HEAD_GROUPS
    st = (N_HEAD_GROUPS, GROUP_LANES, RWKV_HEAD)
    prev = lambda i: jnp.maximum(i - 1, 0)
    row_in = pl.BlockSpec((rows, RWKV_DIM), lambda i: (row0 // rows + jnp.minimum(i, n_steps - 1), 0))
    if chained:
        n_state = 1
        st_spec = pl.BlockSpec((1,) + st, lambda i: (0, 0, 0, 0))
    else:
        n_state = n_rows // chunk
        st_spec = pl.BlockSpec((chunks_per_step,) + st, lambda i: (prev(i), 0, 0, 0))
    in_specs = [row_in] * 6 + [st_spec] + [_const_spec(m.shape) for m in masks]
    out_shape = [jax.ShapeDtypeStruct((n_rows, RWKV_DIM), F32),
                 jax.ShapeDtypeStruct((n_state,) + st, F32)]
    out_specs = [pl.BlockSpec((rows, RWKV_DIM), lambda i: (prev(i), 0)), st_spec]
    scratch = [pltpu.VMEM((N_HEAD_GROUPS, GROUP_LANES, GROUP_LANES), F32),
               pltpu.VMEM((2, n_units, 2 * chunk, GROUP_LANES), BF16),
               pltpu.VMEM((2, n_units, chunk, hc), BF16),
               pltpu.VMEM((n_units, chunk, GROUP_LANES), F32),
               pltpu.VMEM((2, n_units, chunk, 2 * hc), BF16),
               pltpu.VMEM((2, n_units, hc, GROUP_LANES), BF16),
               pltpu.VMEM((n_units, chunk, GROUP_LANES), F32),
               pltpu.VMEM((2, n_units, 2 * chunk, GROUP_LANES), BF16),
               pltpu.VMEM((n_units, 8, GROUP_LANES), F32)]
    kern = functools.partial(_wkv_pipe_kernel, chunk=chunk, n_steps=n_steps, chained=chained)
    return pl.pallas_call(
        kern, out_shape=out_shape, grid=(n_steps + 1,), in_specs=in_specs, out_specs=out_specs,
        scratch_shapes=scratch,
        compiler_params=pltpu.CompilerParams(dimension_semantics=("arbitrary",),
                                             vmem_limit_bytes=VMEM_LIMIT),
        name="wkv_scan")(*rlkvab, s0, *masks)


def _mixer_post_kernel(yp_ref, ys_ref, bonus_ref, g_ref, yag_ref, sgb_ref, xp_ref, xs_ref, lng_ref, lnb_ref,
                       wob_ref, wo_ref, n2_ref, wr_ref, br_ref, ones_ref, tril_ref,
                       x1_ref, h2_ref, route_ref, counts_ref, cnt, *, n_prompt_tiles):
    i = pl.program_id(0)

    @pl.when(i == 0)
    def _():
        cnt[...] = jnp.zeros(cnt.shape, F32)

    tm = x1_ref.shape[0]
    n_parts = 1
    rp = tm // n_parts
    is_prompt = i < n_prompt_tiles
    ones = ones_ref[...]
    neg = jnp.float32(-jnp.inf)
    big = jnp.int32(LANES)
    lane = lax.broadcasted_iota(jnp.int32, (rp, LANES), 1)
    picked = {}

    def part(q):
        rs = slice(q * rp, (q + 1) * rp)
        y = jnp.where(is_prompt, yp_ref[rs, :], ys_ref[rs, :])
        inv_n = 1.0 / RWKV_HEAD
        mean = _split2_dot(y, ones) * inv_n
        yc = y - mean
        var = _split2_dot(yc * yc, ones) * inv_n
        yn = yc * lax.rsqrt(var + GN_EPS) * lng_ref[...] + lnb_ref[...]
        yy = (yn + bonus_ref[rs, :].astype(F32)) * g_ref[rs, :].astype(F32)
        yield
        y_b = jnp.dot(yy.astype(BF16), wob_ref[...], preferred_element_type=F32)
        merged = yag_ref[rs, :].astype(F32) + sgb_ref[rs, :].astype(F32) * y_b
        yield
        x = jnp.where(is_prompt, xp_ref[rs, :], xs_ref[rs, :])
        x1 = x + jnp.dot(merged.astype(BF16), wo_ref[...], preferred_element_type=F32)
        x1_ref[rs, :] = x1
        h2 = _rms(x1, n2_ref[...])
        h2_ref[rs, :] = _pack_bf16_pairs(h2)
        yield
        logits = _dot_stack3(h2, wr_ref[...]) + br_ref[...]
        yield
        is_g = lane < N_GROUPS
        lgp = jnp.where(is_g, logits, neg)
        m_g = jnp.max(lgp, axis=-1, keepdims=True)
        grp = jnp.min(jnp.where(lgp == m_g, lane, big), axis=-1, keepdims=True)
        p_top = 1.0 / jnp.sum(jnp.where(is_g, jnp.exp(logits - m_g), 0.0), axis=-1, keepdims=True)
        e_lane = lane - N_GROUPS
        in_grp = (e_lane >= grp * EXPERTS_PER_GROUP) & (e_lane < (grp + 1) * EXPERTS_PER_GROUP)
        le = jnp.where(in_grp, logits, neg)
        m1 = jnp.max(le, axis=-1, keepdims=True)
        i1 = jnp.min(jnp.where(le == m1, lane, big), axis=-1, keepdims=True)
        le2 = jnp.where(lane == i1, neg, le)
        m2 = jnp.max(le2, axis=-1, keepdims=True)
        i2 = jnp.min(jnp.where(le2 == m2, lane, big), axis=-1, keepdims=True)
        ex = jnp.exp(m2 - m1)
        picked[q] = (i1 - N_GROUPS, i2 - N_GROUPS, p_top / (1.0 + ex), p_top * ex / (1.0 + ex))

    gens = [part(q) for q in range(n_parts)]
    next(gens[0])
    _interleave(*gens)

    hot = [[lane == e for e in picked[q][0:2]] for q in range(n_parts)]
    both = jnp.concatenate([jnp.where(h1, 1.0, jnp.where(h2_, 1.0, 0.0)) for h1, h2_ in hot], axis=0)
    before = jnp.dot(tril_ref[...], both.astype(BF16), preferred_element_type=F32) + cnt[0:1, :]
    cnt[0:1, :] = cnt[0:1, :] + jnp.sum(both, axis=0, keepdims=True)
    counts_ref[...] = jnp.broadcast_to(cnt[0:1, :], counts_ref.shape)
    for q in range(n_parts):
        rs = slice(q * rp, (q + 1) * rp)
        e1, e2, w1, w2 = picked[q]
        ranks = [jnp.sum(jnp.where(h, before[rs, :], 0.0), axis=-1, keepdims=True) for h in hot[q]]
        route = jnp.zeros((rp, LANES), F32)
        for c, col in enumerate([e1.astype(F32), e2.astype(F32), w1, w2] + ranks):
            route = jnp.where(lane == c, col, route)
        route_ref[rs, :] = route


ROUTE_EXPERT, ROUTE_WEIGHT, ROUTE_RANK = 0, 2, 4


def _mixer_post(y_p, y_s, bonus, g, yag, sgb, x_p, x_s, p, *, tm):
    n_p, d = x_p.shape
    n_tok = n_p + x_s.shape[0]
    npt = n_p // tm
    row = lambda w: pl.BlockSpec((tm, w), lambda i: (i, 0))
    tril = jnp.asarray(np.arange(tm)[None, :] < np.arange(tm)[:, None], BF16)
    consts = [p['lnx_g'], p['lnx_b'], p['w_out_b'], p['w_o'], p['norm2_g'], p['w_router'],
              p['b_router'], p['ones_bf16'], tril]
    pair = lambda w: [pl.BlockSpec((tm, w), lambda i: (jnp.minimum(i, npt - 1), 0)),
                      pl.BlockSpec((tm, w), lambda i: (jnp.maximum(i - npt, 0), 0))]
    in_specs = (pair(RWKV_DIM) + [row(RWKV_DIM)] * 2 + [row(d)] * 2 + pair(d)
                + [_const_spec(c.shape) for c in consts])
    out_shape = [jax.ShapeDtypeStruct((n_tok, d), F32), jax.ShapeDtypeStruct((n_tok, d // 2), jnp.uint32),
                 jax.ShapeDtypeStruct((n_tok, LANES), F32), jax.ShapeDtypeStruct((8, LANES), F32)]
    out_specs = [row(d), row(d // 2), row(LANES), _const_spec((8, LANES))]
    return pl.pallas_call(
        functools.partial(_mixer_post_kernel, n_prompt_tiles=npt), out_shape=out_shape,
        grid=(n_tok // tm,), in_specs=in_specs, out_specs=out_specs,
        scratch_shapes=[pltpu.VMEM((8, LANES), F32)],
        compiler_params=pltpu.CompilerParams(dimension_semantics=("arbitrary",),
                                             vmem_limit_bytes=VMEM_LIMIT),
        name="mixer_post")(y_p, y_s, bonus, g, yag, sgb, x_p, x_s, *consts)


N_DMA_PRIORITIES = 2


def _gather_rows(idx_ref, src_hbm, dst, sem, n_rows, *, unrolled):
    def start(r, priority):
        pltpu.make_async_copy(src_hbm.at[pl.ds(idx_ref[r], 1)], dst.at[pl.ds(r, 1)],
                              sem).start(priority=priority)
    if unrolled:
        for r in range(n_rows):
            start(r, r % N_DMA_PRIORITIES)
    else:
        def body(r, carry):
            start(r, 0)
            return carry
        lax.fori_loop(0, n_rows, body, 0)


def _wait_rows(src_hbm, dst, sem, n_rows):
    pltpu.make_async_copy(src_hbm.at[pl.ds(0, n_rows)], dst, sem).wait()


def _moe_scatter_kernel(dest_ref, h_ref, xs_in_hbm, xs_hbm, sem):
    del xs_in_hbm
    n = dest_ref.shape[-1]
    tm = h_ref.shape[0]
    for r in range(n):
        pltpu.make_async_copy(h_ref.at[pl.ds(r % tm, 1)], xs_hbm.at[pl.ds(dest_ref[0, 0, r], 1)],
                              sem).start(priority=r % N_DMA_PRIORITIES)
    for j in range(n // tm):
        pltpu.make_async_copy(h_ref, xs_hbm.at[pl.ds(0, tm)], sem).wait()


def _moe_scatter(h2, pos3, n_rows):
    n_tok, d = h2.shape
    n_tiles, _, n = pos3.shape
    tm = n // TOP_K
    xs0 = jnp.zeros((n_rows, d), h2.dtype)
    return pl.pallas_call(
        _moe_scatter_kernel, out_shape=jax.ShapeDtypeStruct((n_rows, d), h2.dtype), grid=(n_tiles,),
        in_specs=[pl.BlockSpec((1, 1, n), lambda i: (i, 0, 0), memory_space=pltpu.SMEM),
                  pl.BlockSpec((tm, d), lambda i: (i, 0)), pl.BlockSpec(memory_space=pl.ANY)],
        out_specs=pl.BlockSpec(memory_space=pl.ANY),
        scratch_shapes=[pltpu.SemaphoreType.DMA(())], input_output_aliases={2: 0},
        compiler_params=pltpu.CompilerParams(dimension_semantics=("arbitrary",), has_side_effects=True,
                                             vmem_limit_bytes=VMEM_LIMIT),
        name="moe_scatter")(pos3, h2, xs0)


def _moe_experts_kernel(bexp_ref, nused_ref, x_ref, wg_ref, wu_ref, wd_ref, yb_ref, wg_bf, wu_bf, wd_bf):
    i = pl.program_id(0)

    @pl.when((i == 0) | (bexp_ref[i] != bexp_ref[jnp.maximum(i - 1, 0)]))
    def _():
        wg_bf[...] = wg_ref[0].astype(BF16)
        wu_bf[...] = wu_ref[0].astype(BF16)
        wd_bf[...] = wd_ref[0].astype(BF16)

    @pl.when(i < nused_ref[0])
    def _():
        xb = _unpack_bf16_pairs(x_ref[...]).astype(BF16)
        hg = jnp.dot(xb, wg_bf[...], preferred_element_type=F32)
        hu = jnp.dot(xb, wu_bf[...], preferred_element_type=F32)
        hid = (hg * jax.nn.sigmoid(hg)) * hu
        yb_ref[...] = _pack_bf16_pairs(jnp.dot(hid.astype(BF16), wd_bf[...], preferred_element_type=F32))

    @pl.when(i >= nused_ref[0])
    def _():
        yb_ref[...] = jnp.zeros(yb_ref.shape, yb_ref.dtype)


def _moe_experts(xs, block_expert, n_used, w_eg, w_eu, w_ed, *, blk):
    n_rows, dp = xs.shape
    n_blocks = n_rows // blk
    d, de = w_eg.shape[1:]
    grid_spec = pltpu.PrefetchScalarGridSpec(
        num_scalar_prefetch=2, grid=(n_blocks,),
        in_specs=[
            pl.BlockSpec((blk, dp), lambda i, be, nu: (jnp.minimum(i, nu[0] - 1), 0)),
            pl.BlockSpec((1, d, de), lambda i, be, nu: (be[i], 0, 0)),
            pl.BlockSpec((1, d, de), lambda i, be, nu: (be[i], 0, 0)),
            pl.BlockSpec((1, de, d), lambda i, be, nu: (be[i], 0, 0)),
        ],
        out_specs=pl.BlockSpec((blk, dp), lambda i, be, nu: (i, 0)),
        scratch_shapes=[pltpu.VMEM((d, de), BF16), pltpu.VMEM((d, de), BF16), pltpu.VMEM((de, d), BF16)])
    return pl.pallas_call(
        _moe_experts_kernel, out_shape=jax.ShapeDtypeStruct((n_rows, dp), xs.dtype), grid_spec=grid_spec,
        compiler_params=pltpu.CompilerParams(dimension_semantics=("arbitrary",),
                                             vmem_limit_bytes=VMEM_LIMIT),
        name="moe_experts")(block_expert, n_used, xs, w_eg, w_eu, w_ed)


GATHER_AHEAD = 2


def _with_dummy_blocks(idx, n_blocks, blk):
    pad = jnp.zeros((GATHER_AHEAD * blk,), idx.dtype)
    return jnp.concatenate([idx, pad]).reshape(n_blocks + GATHER_AHEAD, 1, blk)


def _moe_combine_kernel(*refs, n_prompt_tiles):
    pos_refs = refs[:GATHER_AHEAD + 1]
    x1_ref, route_ref, nf_ref, yb_hbm, outp_ref, outs_ref, ybuf, sems = refs[GATHER_AHEAD + 1:]
    i = pl.program_id(0)
    n_slots = GATHER_AHEAD + 1
    rows = ybuf.shape[1]
    slot = i % n_slots

    @pl.when(i == 0)
    def _():
        for a in range(GATHER_AHEAD):
            _gather_rows(pos_refs[a].at[0, 0], yb_hbm, ybuf.at[a], sems.at[a], rows, unrolled=False)

    _wait_rows(yb_hbm, ybuf.at[slot], sems.at[slot], rows)
    ahead = (i + GATHER_AHEAD) % n_slots
    _gather_rows(pos_refs[GATHER_AHEAD].at[0, 0], yb_hbm, ybuf.at[ahead], sems.at[ahead], rows, unrolled=True)
    tm = x1_ref.shape[0]
    route = route_ref[...]
    x2 = x1_ref[...]
    for j in range(TOP_K):
        yj = _unpack_bf16_pairs(ybuf[slot, j * tm:(j + 1) * tm, :])
        x2 = x2 + yj * route[:, ROUTE_WEIGHT + j:ROUTE_WEIGHT + j + 1]
    out = _rms(x2, nf_ref[...])

    @pl.when(i < n_prompt_tiles)
    def _():
        outp_ref[...] = out

    @pl.when(i >= n_prompt_tiles)
    def _():
        outs_ref[...] = out

    @pl.when(i == pl.num_programs(0) - 1)
    def _():
        for a in range(1, n_slots):
            s = (i + a) % n_slots
            _wait_rows(yb_hbm, ybuf.at[s], sems.at[s], rows)


def _moe_combine(pos_tiles, x1, route, normf_g, yb, *, n_prompt_rows, tm):
    n_tok, d = x1.shape
    n_tiles = n_tok // tm
    npt = n_prompt_rows // tm
    pos3 = _with_dummy_blocks(pos_tiles.reshape(-1), n_tiles, TOP_K * tm)
    pos_spec = lambda a: pl.BlockSpec((1, 1, TOP_K * tm), lambda i: (i + a, 0, 0), memory_space=pltpu.SMEM)
    in_specs = [pos_spec(a) for a in range(GATHER_AHEAD + 1)] + [
        pl.BlockSpec((tm, d), lambda i: (i, 0)),
        pl.BlockSpec((tm, LANES), lambda i: (i, 0)),
        _const_spec(normf_g.shape),
        pl.BlockSpec(memory_space=pl.ANY),
    ]
    out_shape = [jax.ShapeDtypeStruct((n_prompt_rows, d), F32),
                 jax.ShapeDtypeStruct((n_tok - n_prompt_rows, d), F32)]
    out_specs = [pl.BlockSpec((tm, d), lambda i: (jnp.minimum(i, npt - 1), 0)),
                 pl.BlockSpec((tm, d), lambda i: (jnp.maximum(i - npt, 0), 0))]
    return pl.pallas_call(
        functools.partial(_moe_combine_kernel, n_prompt_tiles=npt), out_shape=out_shape,
        grid=(n_tiles,), in_specs=in_specs, out_specs=out_specs,
        scratch_shapes=[pltpu.VMEM((GATHER_AHEAD + 1, TOP_K * tm, yb.shape[1]), yb.dtype),
                        pltpu.SemaphoreType.DMA((GATHER_AHEAD + 1,))],
        compiler_params=pltpu.CompilerParams(dimension_semantics=("arbitrary",),
                                             vmem_limit_bytes=VMEM_LIMIT),
        name="moe_combine")(*([pos3] * (GATHER_AHEAD + 1)), x1, route, normf_g, yb)


def _dispatch(route, counts, blk, tm):
    n_tok = route.shape[0]
    n_assign = n_tok * TOP_K
    expert = route[:, ROUTE_EXPERT:ROUTE_EXPERT + TOP_K].astype(jnp.int32)
    rank = route[:, ROUTE_RANK:ROUTE_RANK + TOP_K].astype(jnp.int32)
    counts = counts[0, :N_EXPERTS].astype(jnp.int32)
    padded = (counts + blk - 1) // blk * blk
    pad_end = jnp.cumsum(padded)
    pad_start = pad_end - padded
    is_e = expert[:, :, None] == jnp.arange(N_EXPERTS, dtype=jnp.int32)
    dest = jnp.sum(jnp.where(is_e, pad_start, 0), axis=-1) + rank
    n_blocks = -(-n_assign // blk) + N_EXPERTS
    block_start = jnp.arange(n_blocks, dtype=jnp.int32) * blk
    block_expert = jnp.minimum(jnp.sum(block_start[:, None] >= pad_end[None, :], axis=1),
                               N_EXPERTS - 1).astype(jnp.int32)
    n_used = (pad_end[-1] // blk).reshape(1)
    pos_tiles = dest.reshape(n_tok // tm, tm, TOP_K).transpose(0, 2, 1).reshape(n_tok // tm, 1, TOP_K * tm)
    return pos_tiles, block_expert, n_used, n_blocks


def _state_to_kernel(s):
    return s.reshape(s.shape[0], N_HEAD_GROUPS, GROUP_LANES, RWKV_HEAD)


def _state_from_kernel(s):
    return s.reshape(s.shape[0], RWKV_HEADS, RWKV_HEAD, RWKV_HEAD)


def kernel(x_prompt, x_sample, state_conv, state_shift, state_wkv, norm1_g, w_in, conv_w, mu_shift, w0, w_lora_w, a0, w_lora_a, w_lora_g, k_k, k_a, r_k, lnx_g, lnx_b, w_out_a, w_out_b, w_o, norm2_g, w_router_group, b_router_group, w_router_expert, b_router_expert, w_e_gate, w_e_up, w_e_down, normf_g):
    depth = norm1_g.shape[0]
    bp, seq, d = x_prompt.shape
    db, dseq, _ = x_sample.shape
    assert depth == 1 and bp == 1, "single layer, single prompt stream"
    tm = ROW_TILE
    n_p, n_s = bp * seq, db * dseq
    assert n_p % tm == 0 and n_s % tm == 0 and tm % dseq == 0
    n_prompt_tiles, n_sample_tiles = n_p // tm, n_s // tm
    seqs = tm // dseq
    n_tok = n_p + n_s

    x_p = x_prompt.reshape(n_p, d)
    x_s = x_sample.reshape(n_s, d)

    l = 0
    c3 = 3 * CONV_DIM
    head_id = np.arange(GROUP_LANES) // RWKV_HEAD
    ones_bf16 = jnp.asarray(head_id[:, None] == head_id[None, :], BF16)
    zpad = jnp.zeros((LORA_W, RWKV_DIM), F32)
    n_r = N_GROUPS + N_EXPERTS
    p = {
        'norm1_g': norm1_g[l].reshape(1, d),
        'w_in_a': w_in[l][:, :c3].astype(BF16),
        'w_in_b': w_in[l][:, c3:c3 + SHIFT_DIM].astype(BF16),
        'w_in_g': w_in[l][:, c3 + SHIFT_DIM:].astype(BF16),
        'conv_w': conv_w[l],
        'mu_shift': mu_shift[l].reshape(1, SHIFT_DIM),
        'w0': w0[l].reshape(1, RWKV_DIM),
        'w_lora_w': _stack3(jnp.concatenate([w_lora_w[l], zpad], axis=0)),
        'a0': a0[l].reshape(1, RWKV_DIM),
        'w_lora_a': _stack3(jnp.concatenate([zpad, w_lora_a[l]], axis=0)),
        'w_lora_g': _stack3(w_lora_g[l]),
        'k_k': k_k[l].reshape(1, RWKV_DIM),
        'k_a': k_a[l].reshape(1, RWKV_DIM),
        'r_k': r_k[l].reshape(1, RWKV_DIM),
        'w_out_a': w_out_a[l].astype(BF16),
        'ones_bf16': ones_bf16,
        'lnx_g': lnx_g[l].reshape(1, RWKV_DIM),
        'lnx_b': lnx_b[l].reshape(1, RWKV_DIM),
        'w_out_b': w_out_b[l].astype(BF16),
        'w_o': w_o[l].astype(BF16),
        'norm2_g': norm2_g[l].reshape(1, d),
        'w_router': _stack3(jnp.pad(jnp.concatenate([w_router_group[l], w_router_expert[l]], axis=1),
                                    ((0, 0), (0, LANES - n_r)))),
        'b_router': jnp.pad(jnp.concatenate([b_router_group[l], b_router_expert[l]]),
                            (0, LANES - n_r)).reshape(1, LANES),
    }
    st_conv_t = state_conv[l].reshape(n_sample_tiles, seqs * (CONV_WIDTH - 1), CONV_DIM)
    st_shift_t = state_shift[l].reshape(n_sample_tiles, seqs, SHIFT_DIM)

    (yag, sgb, r, w, k, v, a, b, bonus, g, ctail, stail) = _mixer_pre(
        x_p, x_s, st_conv_t, st_shift_t, p, seq_len=dseq, tm=tm)

    s0_prompt = jnp.zeros((1, N_HEAD_GROUPS, GROUP_LANES, RWKV_HEAD), F32)
    s0_sample = _state_to_kernel(state_wkv[l])
    rlkvab = (r, w, k, v, a, b)
    cp = min(SCAN_CHUNK, n_p)
    assert cp & (cp - 1) == 0 and dseq & (dseq - 1) == 0
    y_p, s_p = _wkv_scan(rlkvab, s0_prompt, row0=0, n_rows=n_p, chunk=cp,
                         chunks_per_step=SCAN_CHUNKS_PER_STEP, chained=True)
    y_s, s_s = _wkv_scan(rlkvab, s0_sample, row0=n_p, n_rows=n_s, chunk=dseq,
                         chunks_per_step=SCAN_CHUNKS_PER_STEP, chained=False)

    x1, h2, route, counts = _mixer_post(y_p, y_s, bonus, g, yag, sgb, x_p, x_s, p, tm=tm)

    blk = MOE_ROWS
    pos_tiles, block_expert, n_used, n_blocks = _dispatch(route, counts, blk, tm)
    xs = _moe_scatter(h2, pos_tiles, n_blocks * blk)
    yb = _moe_experts(xs, block_expert, n_used, w_e_gate[l], w_e_up[l], w_e_down[l], blk=blk)
    out_p, out_s = _moe_combine(pos_tiles, x1, route, normf_g.reshape(1, d), yb, n_prompt_rows=n_p, tm=tm)

    y_prompt = out_p.reshape(bp, seq, d)
    y_sample = out_s.reshape(db, dseq, d)
    conv_p = ctail[n_prompt_tiles - 1, 2 * (seqs - 1):2 * seqs].reshape(1, bp, CONV_WIDTH - 1, CONV_DIM)
    shift_p = stail[n_prompt_tiles - 1, seqs - 1].reshape(1, bp, 1, SHIFT_DIM)
    wkv_p = _state_from_kernel(s_p).reshape(1, bp, RWKV_HEADS, RWKV_HEAD, RWKV_HEAD)
    conv_s = ctail[n_prompt_tiles:].reshape(1, db, CONV_WIDTH - 1, CONV_DIM)
    shift_s = stail[n_prompt_tiles:].reshape(1, db, 1, SHIFT_DIM)
    wkv_s = _state_from_kernel(s_s).reshape(1, db, RWKV_HEADS, RWKV_HEAD, RWKV_HEAD)
    return (y_prompt, y_sample, conv_p, shift_p, wkv_p, conv_s, shift_s, wkv_s)
```

```python
import functools

import numpy as np
import jax
import jax.numpy as jnp
from jax import lax
from jax.experimental import pallas as pl
from jax.experimental.pallas import tpu as pltpu

F32 = jnp.float32
BF16 = jnp.bfloat16

CONV_DIM = 512
CONV_WIDTH = 3
RWKV_HEAD = 64
RWKV_HEADS = 8
RWKV_DIM = RWKV_HEADS * RWKV_HEAD
LORA_W = 64
LORA_A = 64
LORA_G = 128
SHIFT_DIM = 3 * RWKV_DIM + LORA_W + LORA_A + LORA_G
N_GROUPS = 4
EXPERTS_PER_GROUP = 8
N_EXPERTS = N_GROUPS * EXPERTS_PER_GROUP
TOP_K = 2
RMS_EPS = 1e-6
GN_EPS = 64e-5

LANES = 128
ROW_TILE = 256
POST_TILES = 2
SCAN_CHUNK = 64
SCAN_CHUNKS_PER_STEP = 4
MOE_ROWS = 256
HIST = 8
VMEM_LIMIT = 56 * 1024 * 1024


def _rms(x, g):
    return x * lax.rsqrt(jnp.mean(x * x, axis=-1, keepdims=True) + RMS_EPS) * g


def _split2_dot(x, ones_bf16):
    w = ones_bf16.shape[0]
    hi = x.astype(BF16)
    lo = (x - hi.astype(F32)).astype(BF16)
    parts = []
    for c0 in range(0, x.shape[1], w):
        both = jnp.concatenate([hi[:, c0:c0 + w], lo[:, c0:c0 + w]], axis=0)
        s = jnp.dot(both, ones_bf16, preferred_element_type=F32)
        parts.append(s[0:x.shape[0]] + s[x.shape[0]:])
    return jnp.concatenate(parts, axis=1)


def _pack_bf16_pairs(x):
    n = x.shape[1] // 2
    lo = pltpu.bitcast(x[:, :n].astype(BF16).astype(F32), jnp.uint32)
    hi = pltpu.bitcast(x[:, n:].astype(BF16).astype(F32), jnp.uint32)
    return hi | (lo >> 16)


def _unpack_bf16_pairs(p):
    lo = pltpu.bitcast(p << 16, F32)
    hi = pltpu.bitcast(p & jnp.uint32(0xFFFF0000), F32)
    return jnp.concatenate([lo, hi], axis=1)


def _stack3(w):
    hi = w.astype(BF16)
    lo = (w - hi.astype(F32)).astype(BF16)
    return jnp.concatenate([hi, hi, lo], axis=0)


def _dot_stack3(x, w_stack):
    hi = x.astype(BF16)
    lo = (x - hi.astype(F32)).astype(BF16)
    return jnp.dot(jnp.concatenate([hi, lo, hi], axis=1), w_stack, preferred_element_type=F32)


def _const_spec(shape):
    nd = len(shape)
    return pl.BlockSpec(shape, lambda *_: (0,) * nd)


def _interleave(*gens):
    live = list(gens)
    while live:
        for gen in list(live):
            try:
                next(gen)
            except StopIteration:
                live.remove(gen)


def _zero_after(x):
    return pltpu.bitcast(lax.shift_right_logical(pltpu.bitcast(x, jnp.uint32), jnp.uint32(32)), F32)


def _mixer_pre_tile(x_ref, stc_ref, sts_ref, n1_ref, wa_ref, wb_ref, wg_ref, convw_ref, mu_ref,
                    w0_ref, lw_ref, a0_ref, la_ref, lgw_ref, kk_ref, ka_ref, rk_ref, woa_ref, ones_ref,
                    yag_ref, sgb_ref, r_ref, w_ref, k_ref, v_ref, a_ref, b_ref, bonus_ref, g_ref,
                    ctail_ref, stail_ref, ccarry, scarry, *, sample, seq_len):
    tm = x_ref.shape[0]
    seqs = tm // seq_len
    h = _rms(x_ref[...], n1_ref[...]).astype(BF16)

    def prev_rows(val, k, heads):
        row = lax.broadcasted_iota(jnp.int32, val.shape, 0)
        out = pltpu.roll(val, k, axis=0)
        for r0, head in heads.items():
            out = jnp.where(row == r0, head, out)
        return out

    pb = jnp.dot(h, wb_ref[...], preferred_element_type=F32)
    old_s = scarry[...]
    if sample:
        s_heads = {j * seq_len: sts_ref[0, j:j + 1, :] for j in range(seqs)}
    else:
        s_heads = {0: old_s[HIST - 1:HIST, :]}
    prev = prev_rows(pb, 1, s_heads)
    for j in range(seqs):
        r1 = (j + 1) * seq_len
        stail_ref[0, j:j + 1, :] = pb[r1 - 1:r1, :]
    scarry[...] = pb[tm - HIST:tm, :] + _zero_after(old_s)

    def conv_and_gates():
        n_a = wa_ref.shape[1] // 3
        g_in = jnp.dot(h, wa_ref[:, 0:n_a], preferred_element_type=F32)
        yield
        g_out = jnp.dot(h, wa_ref[:, n_a:2 * n_a], preferred_element_type=F32)
        yield
        x_c = jnp.dot(h, wa_ref[:, 2 * n_a:3 * n_a], preferred_element_type=F32)
        bx = g_in * x_c
        old_c = ccarry[...]
        if sample:
            h1 = {j * seq_len: stc_ref[0, 2 * j + 1:2 * j + 2, :] for j in range(seqs)}
            h2 = {j * seq_len: stc_ref[0, 2 * j:2 * j + 1, :] for j in range(seqs)}
            h2.update({j * seq_len + 1: stc_ref[0, 2 * j + 1:2 * j + 2, :] for j in range(seqs)})
        else:
            h1 = {0: old_c[HIST - 1:HIST, :]}
            h2 = {0: old_c[HIST - 2:HIST - 1, :], 1: old_c[HIST - 1:HIST, :]}
        cw = convw_ref[...]
        conv = cw[0:1, :] * prev_rows(bx, 2, h2) + cw[1:2, :] * prev_rows(bx, 1, h1) + cw[2:3, :] * bx
        for j in range(seqs):
            r1 = (j + 1) * seq_len
            ctail_ref[0, 2 * j:2 * j + 2, :] = bx[r1 - 2:r1, :]
        ccarry[...] = bx[tm - HIST:tm, :] + _zero_after(old_c)
        yield
        y_a = jnp.dot((g_out * conv).astype(BF16), woa_ref[...], preferred_element_type=F32)
        yield
        d = wg_ref.shape[1] // 2
        half = d // 2
        for c0 in range(0, d, half):
            pg = jnp.dot(h, wg_ref[:, c0:c0 + half], preferred_element_type=F32)
            yag_ref[:, c0:c0 + half] = (jax.nn.sigmoid(pg) * y_a[:, c0:c0 + half]).astype(yag_ref.dtype)
            yield
        for c0 in range(0, d, half):
            pg = jnp.dot(h, wg_ref[:, d + c0:d + c0 + half], preferred_element_type=F32)
            sgb_ref[:, c0:c0 + half] = jax.nn.sigmoid(pg).astype(sgb_ref.dtype)
            yield

    def rwkv_pre():
        s = pb + (prev - pb) * mu_ref[...]
        o1, o2, o3 = RWKV_DIM, 2 * RWKV_DIM, 3 * RWKV_DIM
        r = s[:, 0:o1]
        k = s[:, o1:o2]
        v = s[:, o2:o3]
        s_l = s[:, o3:o3 + LORA_W + LORA_A]
        lg = s[:, o3 + LORA_W + LORA_A:]
        r_ref[...] = r
        v_ref[...] = v
        yield
        z = w0_ref[...] + _dot_stack3(jnp.tanh(s_l), lw_ref[...])
        w_log = -jax.nn.softplus(-z) - 0.5
        w_ref[...] = -jnp.exp(w_log)
        yield
        a = jax.nn.sigmoid(a0_ref[...] + _dot_stack3(s_l, la_ref[...]))
        yield
        g_ref[...] = _dot_stack3(jax.nn.sigmoid(lg), lgw_ref[...]).astype(g_ref.dtype)
        yield
        ones = ones_ref[...]
        kk = k * kk_ref[...]
        kk_n = kk / jnp.maximum(jnp.sqrt(_split2_dot(kk * kk, ones)), 1e-12)
        a_ref[...] = -kk_n
        b_ref[...] = kk_n * a
        yield
        k2 = k * (1.0 + (a - 1.0) * ka_ref[...])
        k_ref[...] = k2
        yield
        bonus_ref[...] = (_split2_dot(r * k2 * rk_ref[...], ones) * v).astype(bonus_ref.dtype)

    _interleave(conv_and_gates(), rwkv_pre())


def _mixer_pre_kernel(xp_ref, xs_ref, *refs, n_prompt_tiles, seq_len):
    i = pl.program_id(0)
    ccarry, scarry = refs[-2:]

    @pl.when(i == 0)
    def _():
        ccarry[...] = jnp.zeros(ccarry.shape, F32)
        scarry[...] = jnp.zeros(scarry.shape, F32)

    pl.when(i < n_prompt_tiles)(
        functools.partial(_mixer_pre_tile, xp_ref, *refs, sample=False, seq_len=seq_len))
    pl.when(i >= n_prompt_tiles)(
        functools.partial(_mixer_pre_tile, xs_ref, *refs, sample=True, seq_len=seq_len))


def _mixer_pre(x_p, x_s, st_conv_t, st_shift_t, p, *, seq_len, tm):
    n_p, d = x_p.shape
    n_tok = n_p + x_s.shape[0]
    n_prompt_tiles = n_p // tm
    n_tiles = n_tok // tm
    seqs = tm // seq_len
    row = lambda w: pl.BlockSpec((tm, w), lambda i: (i, 0))
    st_idx = lambda i: (jnp.maximum(i - n_prompt_tiles, 0), 0, 0)
    consts = [p['norm1_g'], p['w_in_a'], p['w_in_b'], p['w_in_g'], p['conv_w'], p['mu_shift'],
              p['w0'], p['w_lora_w'], p['a0'], p['w_lora_a'], p['w_lora_g'], p['k_k'], p['k_a'],
              p['r_k'], p['w_out_a'], p['ones_bf16']]
    in_specs = [pl.BlockSpec((tm, d), lambda i: (jnp.minimum(i, n_prompt_tiles - 1), 0)),
                pl.BlockSpec((tm, d), lambda i: (jnp.maximum(i - n_prompt_tiles, 0), 0)),
                pl.BlockSpec((1, 2 * seqs, CONV_DIM), st_idx),
                pl.BlockSpec((1, seqs, SHIFT_DIM), st_idx)] + [_const_spec(c.shape) for c in consts]
    sds = lambda w, dt=F32: jax.ShapeDtypeStruct((n_tok, w), dt)
    out_shape = [sds(d, BF16), sds(d, BF16)] + [sds(RWKV_DIM)] * 6 + [sds(RWKV_DIM, BF16)] * 2 + [
        jax.ShapeDtypeStruct((n_tiles, 2 * seqs, CONV_DIM), F32),
        jax.ShapeDtypeStruct((n_tiles, seqs, SHIFT_DIM), F32)]
    out_specs = [row(d), row(d)] + [row(RWKV_DIM)] * 8 + [
        pl.BlockSpec((1, 2 * seqs, CONV_DIM), lambda i: (i, 0, 0)),
        pl.BlockSpec((1, seqs, SHIFT_DIM), lambda i: (i, 0, 0))]
    kern = functools.partial(_mixer_pre_kernel, n_prompt_tiles=n_prompt_tiles, seq_len=seq_len)
    return pl.pallas_call(
        kern, out_shape=out_shape, grid=(n_tiles,), in_specs=in_specs, out_specs=out_specs,
        scratch_shapes=[pltpu.VMEM((HIST, CONV_DIM), F32), pltpu.VMEM((HIST, SHIFT_DIM), F32)],
        compiler_params=pltpu.CompilerParams(dimension_semantics=("arbitrary",),
                                             vmem_limit_bytes=VMEM_LIMIT),
        name="mixer_pre")(x_p, x_s, st_conv_t, st_shift_t, *consts)


GROUP_LANES = 256
HEADS_PER_GROUP = GROUP_LANES // RWKV_HEAD
N_HEAD_GROUPS = RWKV_DIM // GROUP_LANES
NN = (((1,), (0,)), ((), ()))
NT = (((1,), (1,)), ((), ()))


def _split2(x):
    hi = x.astype(BF16)
    lo = (x - hi.astype(F32)).astype(BF16)
    return hi, lo


def _mm(xs, ys, dims=NN):
    x1, x2 = xs
    y1, y2 = ys
    d = lambda p, q: lax.dot_general(p, q, dims, preferred_element_type=F32)
    m = x1.shape[0]
    both = d(jnp.concatenate([x1, x2], axis=0), y1)
    return both[0:m] + both[m:2 * m] + d(x1, y2)


def _cat2(ps, qs, axis):
    return tuple(jnp.concatenate([p, q], axis=axis) for p, q in zip(ps, qs))


def _wkv_masks(c):
    hc = HEADS_PER_GROUP * c
    levels = c.bit_length() - 1
    t = np.arange(c)[:, None]
    s = (np.arange(hc) % c)[None, :]
    tm = [s < t, s <= t]
    for lvl in range(1, levels + 1):
        half = 1 << (lvl - 1)
        tm.append(((t >> lvl) == (s >> lvl)) & ((t & half) != 0) & ((s & half) == 0))
    row_head = (np.arange(hc) // c)[:, None]
    mfeat = row_head == (np.arange(GROUP_LANES) // RWKV_HEAD)[None, :]
    mpos = row_head == (np.arange(hc) // c)[None, :]
    lane_head = np.arange(GROUP_LANES) // RWKV_HEAD
    stmask = lane_head[:, None] == lane_head[None, :]
    tri = np.arange(c)[None, :] <= np.arange(c)[:, None]
    return (jnp.asarray(mfeat, BF16), jnp.asarray(mpos, BF16), jnp.asarray(np.stack(tm), F32),
            jnp.asarray(stmask, F32), jnp.asarray(tri, BF16))


def _wkv_pipe_kernel(r_ref, lw_ref, k_ref, v_ref, a_ref, b_ref, s0_ref, mfeat_ref, mpos_ref,
                     tmask_ref, stmask_ref, tri_ref, y_ref, sout_ref,
                     state, sv_ar, sv_inv, sv_akv, sv_arbk, sv_vbd, sv_v, sv_bk, sv_pend,
                     *, chunk, n_steps, chained):
    i = pl.program_id(0)
    c = chunk
    n_chunks = r_ref.shape[0] // c
    hc = HEADS_PER_GROUP * c
    levels = c.bit_length() - 1
    groups = range(N_HEAD_GROUPS)
    units = [(j, g) for j in range(n_chunks) for g in groups]
    uid = {u: n for n, u in enumerate(units)}

    stmask = stmask_ref[...]

    def expand(sc):
        pair = jnp.concatenate([sc, sc], axis=1)
        return jnp.concatenate([pair] * (GROUP_LANES // pair.shape[1]), axis=1) * stmask

    def compact(s):
        half = s[:, 0:GROUP_LANES // 2] + s[:, GROUP_LANES // 2:]
        return half[:, 0:RWKV_HEAD] + half[:, RWKV_HEAD:]

    @pl.when(i == 0)
    def _():
        for ref in (sv_ar, sv_inv, sv_akv, sv_arbk, sv_vbd, sv_v, sv_bk):
            ref[...] = jnp.zeros(ref.shape, ref.dtype)
        sv_pend[...] = jnp.ones(sv_pend.shape, F32)
        if chained:
            for g in groups:
                state[g] = expand(s0_ref[0, g])

    mfeat = mfeat_ref[...]
    mpos = mpos_ref[...]
    strict = tmask_ref[0]
    incl = tmask_ref[1]
    eye = incl - strict
    tri = tri_ref[...]

    def ld(ref, j, g):
        return ref[j * c:(j + 1) * c, g * GROUP_LANES:(g + 1) * GROUP_LANES]

    def bd_split(ps):
        mask = mpos if ps[0].shape[1] == hc else mfeat
        return tuple(jnp.concatenate([p] * HEADS_PER_GROUP, axis=0) * mask for p in ps)

    def bd2(m):
        return bd_split(_split2(m))

    def cumsum_rows(x):
        p1 = x.astype(BF16)
        r1 = x - p1.astype(F32)
        p2 = r1.astype(BF16)
        p3 = (r1 - p2.astype(F32)).astype(BF16)
        d = lambda q: jnp.dot(tri, q, preferred_element_type=F32)
        return d(p1) + d(p2) + d(p3)

    new = {}

    def prepare():
        cum = {u: cumsum_rows(ld(lw_ref, *u)) for u in units}
        yield
        ar, bk_end, p_end, a_ab, a_ak, a_rb, a_rk, v, vbd = ({} for _ in range(9))
        for u in units:
            cm = cum[u]
            cum_last = cm[c - 1:c, :]
            e_neg = jnp.exp(-cm)
            e_end = jnp.exp(cum_last - cm)
            b_raw = ld(b_ref, *u)
            k_raw = ld(k_ref, *u)
            ar[u] = _split2(jnp.concatenate([ld(a_ref, *u) * jnp.exp(cm - ld(lw_ref, *u)),
                                             ld(r_ref, *u) * jnp.exp(cm)], axis=0))
            bk_end[u] = _split2(jnp.concatenate([b_raw * e_end, k_raw * e_end], axis=0))
            p_end[u] = jnp.exp(cum_last)
            v[u] = ld(v_ref, *u)
            vbd[u] = bd2(v[u])
            gram = _mm(ar[u], _cat2(bd2(b_raw * e_neg), bd2(k_raw * e_neg), 0), NT)
            a_ab[u] = jnp.where(strict > 0, gram[0:c, 0:hc], 0.0)
            a_ak[u] = jnp.where(strict > 0, gram[0:c, hc:2 * hc], 0.0)
            a_rb[u] = jnp.where(incl > 0, gram[c:2 * c, 0:hc], 0.0)
            a_rk[u] = jnp.where(incl > 0, gram[c:2 * c, hc:2 * hc], 0.0)
            if uid[u] % 2 == 1:
                yield
        a_ab2 = {u: _split2(a_ab[u]) for u in units}
        inv = {u: eye + a_ab[u] * tmask_ref[2] for u in units}
        inv2 = {u: _split2(inv[u]) for u in units}
        for lvl in range(2, levels + 1):
            lm = tmask_ref[1 + lvl]
            t1 = {u: _mm(a_ab2[u], bd_split(inv2[u])) for u in units}
            yield
            inv = {u: inv[u] + lm * _mm(inv2[u], bd2(t1[u])) for u in units}
            inv2 = {u: _split2(inv[u]) for u in units}
            yield
        akv = {u: _mm(_split2(a_ak[u]), vbd[u]) for u in units}
        arbk = {u: _split2(jnp.concatenate([a_rb[u], a_rk[u]], axis=1)) for u in units}
        new.update(ar=ar, inv=inv2, akv=akv, arbk=arbk, vbd=vbd, v=v, bk=bk_end, pend=p_end)

    def serial():
        pair = lambda ref, n: (ref[0, n], ref[1, n])
        s_cur = [state[g] for g in groups] if chained else None
        for j in range(n_chunks):
            ns = [uid[j, g] for g in groups]
            s_prev = s_cur if chained else [expand(s0_ref[j, g]) for g in groups]
            x0 = [_mm(pair(sv_ar, n), _split2(s_prev[g]), NT) for g, n in zip(groups, ns)]
            yield
            uu = [_mm(pair(sv_inv, n), bd2(x0[g][0:c] + sv_akv[n])) for g, n in zip(groups, ns)]
            yield
            for g, n in zip(groups, ns):
                yy = x0[g][c:2 * c] + _mm(pair(sv_arbk, n), _cat2(bd2(uu[g]), pair(sv_vbd, n), 0))
                y_ref[j * c:(j + 1) * c, g * GROUP_LANES:(g + 1) * GROUP_LANES] = yy
            s_new = []
            for g, n in zip(groups, ns):
                uv_t = jnp.transpose(jnp.concatenate([uu[g], sv_v[n]], axis=0))
                upd = _mm(_split2(uv_t), pair(sv_bk, n))
                s_new.append(s_prev[g] * sv_pend[n, 0:1, :] + stmask * upd)
            yield
            if chained:
                s_cur = s_new
            else:
                for g in groups:
                    sout_ref[j, g] = compact(s_new[g])
        if chained:
            for g in groups:
                state[g] = s_cur[g]
        new['last_state'] = s_new

    _interleave(serial(), prepare())

    zero = sum(_zero_after(s[0:8, 0:LANES]) for s in new['last_state'])[0:1, 0:1]
    zero_bf = zero.astype(BF16)
    for u, n in uid.items():
        for name, ref in (('ar', sv_ar), ('inv', sv_inv), ('arbk', sv_arbk), ('vbd', sv_vbd), ('bk', sv_bk)):
            for half in range(2):
                ref[half, n] = new[name][u][half] + zero_bf
        sv_akv[n] = new['akv'][u] + zero
        sv_v[n] = new['v'][u] + zero
        sv_pend[n] = jnp.broadcast_to(new['pend'][u] + zero, sv_pend.shape[1:])

    if chained:
        @pl.when(i == n_steps)
        def _():
            for g in groups:
                sout_ref[0, g] = compact(state[g])


def _wkv_scan(rlkvab, s0, *, row0, n_rows, chunk, chunks_per_step, chained):
    rows = chunk * chunks_per_step
    n_steps = n_rows // rows
    assert n_rows % rows == 0 and row0 % rows == 0
    masks = _wkv_masks(chunk)
    hc = HEADS_PER_GROUP * chunk
    n_units = chunks_per_step * N_HEAD_GROUPS
    st = (N_HEAD_GROUPS, GROUP_LANES, RWKV_HEAD)
    prev = lambda i: jnp.maximum(i - 1, 0)
    row_in = pl.BlockSpec((rows, RWKV_DIM), lambda i: (row0 // rows + jnp.minimum(i, n_steps - 1), 0))
    if chained:
        n_state = 1
        st_spec = pl.BlockSpec((1,) + st, lambda i: (0, 0, 0, 0))
    else:
        n_state = n_rows // chunk
        st_spec = pl.BlockSpec((chunks_per_step,) + st, lambda i: (prev(i), 0, 0, 0))
    in_specs = [row_in] * 6 + [st_spec] + [_const_spec(m.shape) for m in masks]
    out_shape = [jax.ShapeDtypeStruct((n_rows, RWKV_DIM), F32),
                 jax.ShapeDtypeStruct((n_state,) + st, F32)]
    out_specs = [pl.BlockSpec((rows, RWKV_DIM), lambda i: (prev(i), 0)), st_spec]
    scratch = [pltpu.VMEM((N_HEAD_GROUPS, GROUP_LANES, GROUP_LANES), F32),
               pltpu.VMEM((2, n_units, 2 * chunk, GROUP_LANES), BF16),
               pltpu.VMEM((2, n_units, chunk, hc), BF16),
               pltpu.VMEM((n_units, chunk, GROUP_LANES), F32),
               pltpu.VMEM((2, n_units, chunk, 2 * hc), BF16),
               pltpu.VMEM((2, n_units, hc, GROUP_LANES), BF16),
               pltpu.VMEM((n_units, chunk, GROUP_LANES), F32),
               pltpu.VMEM((2, n_units, 2 * chunk, GROUP_LANES), BF16),
               pltpu.VMEM((n_units, 8, GROUP_LANES), F32)]
    kern = functools.partial(_wkv_pipe_kernel, chunk=chunk, n_steps=n_steps, chained=chained)
    return pl.pallas_call(
        kern, out_shape=out_shape, grid=(n_steps + 1,), in_specs=in_specs, out_specs=out_specs,
        scratch_shapes=scratch,
        compiler_params=pltpu.CompilerParams(dimension_semantics=("arbitrary",),
                                             vmem_limit_bytes=VMEM_LIMIT),
        name="wkv_scan")(*rlkvab, s0, *masks)


def _mixer_post_kernel(yp_ref, ys_ref, bonus_ref, g_ref, yag_ref, sgb_ref, xp_ref, xs_ref, lng_ref, lnb_ref,
                       wob_ref, wo_ref, n2_ref, wr_ref, br_ref, ones_ref, tril_ref,
                       x1_ref, h2_ref, route_ref, counts_ref, cnt, *, n_prompt_tiles):
    i = pl.program_id(0)

    @pl.when(i == 0)
    def _():
        cnt[...] = jnp.zeros(cnt.shape, F32)

    tm = x1_ref.shape[0]
    n_parts = tm // ROW_TILE
    rp = tm // n_parts
    is_prompt = i < n_prompt_tiles
    ones = ones_ref[...]
    neg = jnp.float32(-jnp.inf)
    big = jnp.int32(LANES)
    lane = lax.broadcasted_iota(jnp.int32, (rp, LANES), 1)
    picked = {}

    def part(q):
        rs = slice(q * rp, (q + 1) * rp)
        y = jnp.where(is_prompt, yp_ref[rs, :], ys_ref[rs, :])
        inv_n = 1.0 / RWKV_HEAD
        mean = _split2_dot(y, ones) * inv_n
        yc = y - mean
        var = _split2_dot(yc * yc, ones) * inv_n
        yn = yc * lax.rsqrt(var + GN_EPS) * lng_ref[...] + lnb_ref[...]
        yy = (yn + bonus_ref[rs, :].astype(F32)) * g_ref[rs, :].astype(F32)
        yield
        y_b = jnp.dot(yy.astype(BF16), wob_ref[...], preferred_element_type=F32)
        merged = yag_ref[rs, :].astype(F32) + sgb_ref[rs, :].astype(F32) * y_b
        yield
        x = jnp.where(is_prompt, xp_ref[rs, :], xs_ref[rs, :])
        x1 = x + jnp.dot(merged.astype(BF16), wo_ref[...], preferred_element_type=F32)
        x1_ref[rs, :] = x1
        h2 = _rms(x1, n2_ref[...])
        h2_ref[rs, :] = _pack_bf16_pairs(h2)
        yield
        logits = _dot_stack3(h2, wr_ref[...]) + br_ref[...]
        yield
        is_g = lane < N_GROUPS
        lgp = jnp.where(is_g, logits, neg)
        m_g = jnp.max(lgp, axis=-1, keepdims=True)
        grp = jnp.min(jnp.where(lgp == m_g, lane, big), axis=-1, keepdims=True)
        p_top = 1.0 / jnp.sum(jnp.where(is_g, jnp.exp(logits - m_g), 0.0), axis=-1, keepdims=True)
        e_lane = lane - N_GROUPS
        in_grp = (e_lane >= grp * EXPERTS_PER_GROUP) & (e_lane < (grp + 1) * EXPERTS_PER_GROUP)
        le = jnp.where(in_grp, logits, neg)
        m1 = jnp.max(le, axis=-1, keepdims=True)
        i1 = jnp.min(jnp.where(le == m1, lane, big), axis=-1, keepdims=True)
        le2 = jnp.where(lane == i1, neg, le)
        m2 = jnp.max(le2, axis=-1, keepdims=True)
        i2 = jnp.min(jnp.where(le2 == m2, lane, big), axis=-1, keepdims=True)
        ex = jnp.exp(m2 - m1)
        picked[q] = (i1 - N_GROUPS, i2 - N_GROUPS, p_top / (1.0 + ex), p_top * ex / (1.0 + ex))

    gens = [part(q) for q in range(n_parts)]
    next(gens[0])
    _interleave(*gens)

    hot = [[lane == e for e in picked[q][0:2]] for q in range(n_parts)]
    both = jnp.concatenate([jnp.where(h1, 1.0, jnp.where(h2_, 1.0, 0.0)) for h1, h2_ in hot], axis=0)
    before = jnp.dot(tril_ref[...], both.astype(BF16), preferred_element_type=F32) + cnt[0:1, :]
    cnt[0:1, :] = cnt[0:1, :] + jnp.sum(both, axis=0, keepdims=True)
    counts_ref[...] = jnp.broadcast_to(cnt[0:1, :], counts_ref.shape)
    for q in range(n_parts):
        rs = slice(q * rp, (q + 1) * rp)
        e1, e2, w1, w2 = picked[q]
        ranks = [jnp.sum(jnp.where(h, before[rs, :], 0.0), axis=-1, keepdims=True) for h in hot[q]]
        route = jnp.zeros((rp, LANES), F32)
        for c, col in enumerate([e1.astype(F32), e2.astype(F32), w1, w2] + ranks):
            route = jnp.where(lane == c, col, route)
        route_ref[rs, :] = route


ROUTE_EXPERT, ROUTE_WEIGHT, ROUTE_RANK = 0, 2, 4


def _mixer_post(y_p, y_s, bonus, g, yag, sgb, x_p, x_s, p, *, tm):
    n_p, d = x_p.shape
    n_tok = n_p + x_s.shape[0]
    npt = n_p // tm
    row = lambda w: pl.BlockSpec((tm, w), lambda i: (i, 0))
    tril = jnp.asarray(np.arange(tm)[None, :] < np.arange(tm)[:, None], BF16)
    consts = [p['lnx_g'], p['lnx_b'], p['w_out_b'], p['w_o'], p['norm2_g'], p['w_router'],
              p['b_router'], p['ones_bf16'], tril]
    pair = lambda w: [pl.BlockSpec((tm, w), lambda i: (jnp.minimum(i, npt - 1), 0)),
                      pl.BlockSpec((tm, w), lambda i: (jnp.maximum(i - npt, 0), 0))]
    in_specs = (pair(RWKV_DIM) + [row(RWKV_DIM)] * 2 + [row(d)] * 2 + pair(d)
                + [_const_spec(c.shape) for c in consts])
    out_shape = [jax.ShapeDtypeStruct((n_tok, d), F32), jax.ShapeDtypeStruct((n_tok, d // 2), jnp.uint32),
                 jax.ShapeDtypeStruct((n_tok, LANES), F32), jax.ShapeDtypeStruct((8, LANES), F32)]
    out_specs = [row(d), row(d // 2), row(LANES), _const_spec((8, LANES))]
    return pl.pallas_call(
        functools.partial(_mixer_post_kernel, n_prompt_tiles=npt), out_shape=out_shape,
        grid=(n_tok // tm,), in_specs=in_specs, out_specs=out_specs,
        scratch_shapes=[pltpu.VMEM((8, LANES), F32)],
        compiler_params=pltpu.CompilerParams(dimension_semantics=("arbitrary",),
                                             vmem_limit_bytes=VMEM_LIMIT),
        name="mixer_post")(y_p, y_s, bonus, g, yag, sgb, x_p, x_s, *consts)


N_DMA_PRIORITIES = 2


def _gather_rows(idx_ref, src_hbm, dst, sem, n_rows, *, unrolled):
    def start(r, priority):
        pltpu.make_async_copy(src_hbm.at[pl.ds(idx_ref[r], 1)], dst.at[pl.ds(r, 1)],
                              sem).start(priority=priority)
    if unrolled:
        for r in range(n_rows):
            start(r, r % N_DMA_PRIORITIES)
    else:
        def body(r, carry):
            start(r, 0)
            return carry
        lax.fori_loop(0, n_rows, body, 0)


def _wait_rows(src_hbm, dst, sem, n_rows):
    pltpu.make_async_copy(src_hbm.at[pl.ds(0, n_rows)], dst, sem).wait()


def _moe_scatter_kernel(dest_ref, h_ref, xs_in_hbm, xs_hbm, sem):
    del xs_in_hbm
    n = dest_ref.shape[-1]
    tm = h_ref.shape[0]
    for r in range(n):
        pltpu.make_async_copy(h_ref.at[pl.ds(r % tm, 1)], xs_hbm.at[pl.ds(dest_ref[0, 0, r], 1)],
                              sem).start(priority=r % N_DMA_PRIORITIES)
    for j in range(n // tm):
        pltpu.make_async_copy(h_ref, xs_hbm.at[pl.ds(0, tm)], sem).wait()


def _moe_scatter(h2, pos3, n_rows):
    n_tok, d = h2.shape
    n_tiles, _, n = pos3.shape
    tm = n // TOP_K
    xs0 = jnp.zeros((n_rows, d), h2.dtype)
    return pl.pallas_call(
        _moe_scatter_kernel, out_shape=jax.ShapeDtypeStruct((n_rows, d), h2.dtype), grid=(n_tiles,),
        in_specs=[pl.BlockSpec((1, 1, n), lambda i: (i, 0, 0), memory_space=pltpu.SMEM),
                  pl.BlockSpec((tm, d), lambda i: (i, 0)), pl.BlockSpec(memory_space=pl.ANY)],
        out_specs=pl.BlockSpec(memory_space=pl.ANY),
        scratch_shapes=[pltpu.SemaphoreType.DMA(())], input_output_aliases={2: 0},
        compiler_params=pltpu.CompilerParams(dimension_semantics=("arbitrary",), has_side_effects=True,
                                             vmem_limit_bytes=VMEM_LIMIT),
        name="moe_scatter")(pos3, h2, xs0)


def _moe_experts_kernel(bexp_ref, nused_ref, x_ref, wg_ref, wu_ref, wd_ref, yb_ref, wg_bf, wu_bf, wd_bf):
    i = pl.program_id(0)

    @pl.when((i == 0) | (bexp_ref[i] != bexp_ref[jnp.maximum(i - 1, 0)]))
    def _():
        wg_bf[...] = wg_ref[0].astype(BF16)
        wu_bf[...] = wu_ref[0].astype(BF16)
        wd_bf[...] = wd_ref[0].astype(BF16)

    @pl.when(i < nused_ref[0])
    def _():
        xb = _unpack_bf16_pairs(x_ref[...]).astype(BF16)
        hg = jnp.dot(xb, wg_bf[...], preferred_element_type=F32)
        hu = jnp.dot(xb, wu_bf[...], preferred_element_type=F32)
        hid = (hg * jax.nn.sigmoid(hg)) * hu
        yb_ref[...] = _pack_bf16_pairs(jnp.dot(hid.astype(BF16), wd_bf[...], preferred_element_type=F32))

    @pl.when(i >= nused_ref[0])
    def _():
        yb_ref[...] = jnp.zeros(yb_ref.shape, yb_ref.dtype)


def _moe_experts(xs, block_expert, n_used, w_eg, w_eu, w_ed, *, blk):
    n_rows, dp = xs.shape
    n_blocks = n_rows // blk
    d, de = w_eg.shape[1:]
    grid_spec = pltpu.PrefetchScalarGridSpec(
        num_scalar_prefetch=2, grid=(n_blocks,),
        in_specs=[
            pl.BlockSpec((blk, dp), lambda i, be, nu: (jnp.minimum(i, nu[0] - 1), 0)),
            pl.BlockSpec((1, d, de), lambda i, be, nu: (be[i], 0, 0)),
            pl.BlockSpec((1, d, de), lambda i, be, nu: (be[i], 0, 0)),
            pl.BlockSpec((1, de, d), lambda i, be, nu: (be[i], 0, 0)),
        ],
        out_specs=pl.BlockSpec((blk, dp), lambda i, be, nu: (i, 0)),
        scratch_shapes=[pltpu.VMEM((d, de), BF16), pltpu.VMEM((d, de), BF16), pltpu.VMEM((de, d), BF16)])
    return pl.pallas_call(
        _moe_experts_kernel, out_shape=jax.ShapeDtypeStruct((n_rows, dp), xs.dtype), grid_spec=grid_spec,
        compiler_params=pltpu.CompilerParams(dimension_semantics=("arbitrary",),
                                             vmem_limit_bytes=VMEM_LIMIT),
        name="moe_experts")(block_expert, n_used, xs, w_eg, w_eu, w_ed)


GATHER_AHEAD = 2


def _with_dummy_blocks(idx, n_blocks, blk):
    pad = jnp.zeros((GATHER_AHEAD * blk,), idx.dtype)
    return jnp.concatenate([idx, pad]).reshape(n_blocks + GATHER_AHEAD, 1, blk)


def _moe_combine_kernel(*refs, n_prompt_tiles):
    pos_refs = refs[:GATHER_AHEAD + 1]
    x1_ref, route_ref, nf_ref, yb_hbm, outp_ref, outs_ref, ybuf, sems = refs[GATHER_AHEAD + 1:]
    i = pl.program_id(0)
    n_slots = GATHER_AHEAD + 1
    rows = ybuf.shape[1]
    slot = i % n_slots

    @pl.when(i == 0)
    def _():
        for a in range(GATHER_AHEAD):
            _gather_rows(pos_refs[a].at[0, 0], yb_hbm, ybuf.at[a], sems.at[a], rows, unrolled=False)

    _wait_rows(yb_hbm, ybuf.at[slot], sems.at[slot], rows)
    ahead = (i + GATHER_AHEAD) % n_slots
    _gather_rows(pos_refs[GATHER_AHEAD].at[0, 0], yb_hbm, ybuf.at[ahead], sems.at[ahead], rows, unrolled=True)
    tm = x1_ref.shape[0]
    route = route_ref[...]
    x2 = x1_ref[...]
    for j in range(TOP_K):
        yj = _unpack_bf16_pairs(ybuf[slot, j * tm:(j + 1) * tm, :])
        x2 = x2 + yj * route[:, ROUTE_WEIGHT + j:ROUTE_WEIGHT + j + 1]
    out = _rms(x2, nf_ref[...])

    @pl.when(i < n_prompt_tiles)
    def _():
        outp_ref[...] = out

    @pl.when(i >= n_prompt_tiles)
    def _():
        outs_ref[...] = out

    @pl.when(i == pl.num_programs(0) - 1)
    def _():
        for a in range(1, n_slots):
            s = (i + a) % n_slots
            _wait_rows(yb_hbm, ybuf.at[s], sems.at[s], rows)


def _moe_combine(pos_tiles, x1, route, normf_g, yb, *, n_prompt_rows, tm):
    n_tok, d = x1.shape
    n_tiles = n_tok // tm
    npt = n_prompt_rows // tm
    pos3 = _with_dummy_blocks(pos_tiles.reshape(-1), n_tiles, TOP_K * tm)
    pos_spec = lambda a: pl.BlockSpec((1, 1, TOP_K * tm), lambda i: (i + a, 0, 0), memory_space=pltpu.SMEM)
    in_specs = [pos_spec(a) for a in range(GATHER_AHEAD + 1)] + [
        pl.BlockSpec((tm, d), lambda i: (i, 0)),
        pl.BlockSpec((tm, LANES), lambda i: (i, 0)),
        _const_spec(normf_g.shape),
        pl.BlockSpec(memory_space=pl.ANY),
    ]
    out_shape = [jax.ShapeDtypeStruct((n_prompt_rows, d), F32),
                 jax.ShapeDtypeStruct((n_tok - n_prompt_rows, d), F32)]
    out_specs = [pl.BlockSpec((tm, d), lambda i: (jnp.minimum(i, npt - 1), 0)),
                 pl.BlockSpec((tm, d), lambda i: (jnp.maximum(i - npt, 0), 0))]
    return pl.pallas_call(
        functools.partial(_moe_combine_kernel, n_prompt_tiles=npt), out_shape=out_shape,
        grid=(n_tiles,), in_specs=in_specs, out_specs=out_specs,
        scratch_shapes=[pltpu.VMEM((GATHER_AHEAD + 1, TOP_K * tm, yb.shape[1]), yb.dtype),
                        pltpu.SemaphoreType.DMA((GATHER_AHEAD + 1,))],
        compiler_params=pltpu.CompilerParams(dimension_semantics=("arbitrary",),
                                             vmem_limit_bytes=VMEM_LIMIT),
        name="moe_combine")(*([pos3] * (GATHER_AHEAD + 1)), x1, route, normf_g, yb)


def _dispatch(route, counts, blk, tm):
    n_tok = route.shape[0]
    n_assign = n_tok * TOP_K
    expert = route[:, ROUTE_EXPERT:ROUTE_EXPERT + TOP_K].astype(jnp.int32)
    rank = route[:, ROUTE_RANK:ROUTE_RANK + TOP_K].astype(jnp.int32)
    counts = counts[0, :N_EXPERTS].astype(jnp.int32)
    padded = (counts + blk - 1) // blk * blk
    pad_end = jnp.cumsum(padded)
    pad_start = pad_end - padded
    is_e = expert[:, :, None] == jnp.arange(N_EXPERTS, dtype=jnp.int32)
    dest = jnp.sum(jnp.where(is_e, pad_start, 0), axis=-1) + rank
    n_blocks = -(-n_assign // blk) + N_EXPERTS
    block_start = jnp.arange(n_blocks, dtype=jnp.int32) * blk
    block_expert = jnp.minimum(jnp.sum(block_start[:, None] >= pad_end[None, :], axis=1),
                               N_EXPERTS - 1).astype(jnp.int32)
    n_used = (pad_end[-1] // blk).reshape(1)
    pos_tiles = dest.reshape(n_tok // tm, tm, TOP_K).transpose(0, 2, 1).reshape(n_tok // tm, 1, TOP_K * tm)
    return pos_tiles, block_expert, n_used, n_blocks


def _state_to_kernel(s):
    return s.reshape(s.shape[0], N_HEAD_GROUPS, GROUP_LANES, RWKV_HEAD)


def _state_from_kernel(s):
    return s.reshape(s.shape[0], RWKV_HEADS, RWKV_HEAD, RWKV_HEAD)


def kernel(x_prompt, x_sample, state_conv, state_shift, state_wkv, norm1_g, w_in, conv_w, mu_shift, w0, w_lora_w, a0, w_lora_a, w_lora_g, k_k, k_a, r_k, lnx_g, lnx_b, w_out_a, w_out_b, w_o, norm2_g, w_router_group, b_router_group, w_router_expert, b_router_expert, w_e_gate, w_e_up, w_e_down, normf_g):
    depth = norm1_g.shape[0]
    bp, seq, d = x_prompt.shape
    db, dseq, _ = x_sample.shape
    assert depth == 1 and bp == 1, "single layer, single prompt stream"
    tm = ROW_TILE
    n_p, n_s = bp * seq, db * dseq
    assert n_p % tm == 0 and n_s % tm == 0 and tm % dseq == 0
    n_prompt_tiles, n_sample_tiles = n_p // tm, n_s // tm
    seqs = tm // dseq
    n_tok = n_p + n_s

    x_p = x_prompt.reshape(n_p, d)
    x_s = x_sample.reshape(n_s, d)

    l = 0
    c3 = 3 * CONV_DIM
    head_id = np.arange(GROUP_LANES) // RWKV_HEAD
    ones_bf16 = jnp.asarray(head_id[:, None] == head_id[None, :], BF16)
    zpad = jnp.zeros((LORA_W, RWKV_DIM), F32)
    n_r = N_GROUPS + N_EXPERTS
    p = {
        'norm1_g': norm1_g[l].reshape(1, d),
        'w_in_a': w_in[l][:, :c3].astype(BF16),
        'w_in_b': w_in[l][:, c3:c3 + SHIFT_DIM].astype(BF16),
        'w_in_g': w_in[l][:, c3 + SHIFT_DIM:].astype(BF16),
        'conv_w': conv_w[l],
        'mu_shift': mu_shift[l].reshape(1, SHIFT_DIM),
        'w0': w0[l].reshape(1, RWKV_DIM),
        'w_lora_w': _stack3(jnp.concatenate([w_lora_w[l], zpad], axis=0)),
        'a0': a0[l].reshape(1, RWKV_DIM),
        'w_lora_a': _stack3(jnp.concatenate([zpad, w_lora_a[l]], axis=0)),
        'w_lora_g': _stack3(w_lora_g[l]),
        'k_k': k_k[l].reshape(1, RWKV_DIM),
        'k_a': k_a[l].reshape(1, RWKV_DIM),
        'r_k': r_k[l].reshape(1, RWKV_DIM),
        'w_out_a': w_out_a[l].astype(BF16),
        'ones_bf16': ones_bf16,
        'lnx_g': lnx_g[l].reshape(1, RWKV_DIM),
        'lnx_b': lnx_b[l].reshape(1, RWKV_DIM),
        'w_out_b': w_out_b[l].astype(BF16),
        'w_o': w_o[l].astype(BF16),
        'norm2_g': norm2_g[l].reshape(1, d),
        'w_router': _stack3(jnp.pad(jnp.concatenate([w_router_group[l], w_router_expert[l]], axis=1),
                                    ((0, 0), (0, LANES - n_r)))),
        'b_router': jnp.pad(jnp.concatenate([b_router_group[l], b_router_expert[l]]),
                            (0, LANES - n_r)).reshape(1, LANES),
    }
    st_conv_t = state_conv[l].reshape(n_sample_tiles, seqs * (CONV_WIDTH - 1), CONV_DIM)
    st_shift_t = state_shift[l].reshape(n_sample_tiles, seqs, SHIFT_DIM)

    (yag, sgb, r, w, k, v, a, b, bonus, g, ctail, stail) = _mixer_pre(
        x_p, x_s, st_conv_t, st_shift_t, p, seq_len=dseq, tm=tm)

    s0_prompt = jnp.zeros((1, N_HEAD_GROUPS, GROUP_LANES, RWKV_HEAD), F32)
    s0_sample = _state_to_kernel(state_wkv[l])
    rlkvab = (r, w, k, v, a, b)
    cp = min(SCAN_CHUNK, n_p)
    assert cp & (cp - 1) == 0 and dseq & (dseq - 1) == 0
    y_p, s_p = _wkv_scan(rlkvab, s0_prompt, row0=0, n_rows=n_p, chunk=cp,
                         chunks_per_step=SCAN_CHUNKS_PER_STEP, chained=True)
    y_s, s_s = _wkv_scan(rlkvab, s0_sample, row0=n_p, n_rows=n_s, chunk=dseq,
                         chunks_per_step=SCAN_CHUNKS_PER_STEP, chained=False)

    post_tm = POST_TILES * tm
    assert n_p % post_tm == 0 and n_s % post_tm == 0
    x1, h2, route, counts = _mixer_post(y_p, y_s, bonus, g, yag, sgb, x_p, x_s, p, tm=post_tm)

    blk = MOE_ROWS
    pos_tiles, block_expert, n_used, n_blocks = _dispatch(route, counts, blk, tm)
    xs = _moe_scatter(h2, pos_tiles, n_blocks * blk)
    yb = _moe_experts(xs, block_expert, n_used, w_e_gate[l], w_e_up[l], w_e_down[l], blk=blk)
    out_p, out_s = _moe_combine(pos_tiles, x1, route, normf_g.reshape(1, d), yb, n_prompt_rows=n_p, tm=tm)

    y_prompt = out_p.reshape(bp, seq, d)
    y_sample = out_s.reshape(db, dseq, d)
    conv_p = ctail[n_prompt_tiles - 1, 2 * (seqs - 1):2 * seqs].reshape(1, bp, CONV_WIDTH - 1, CONV_DIM)
    shift_p = stail[n_prompt_tiles - 1, seqs - 1].reshape(1, bp, 1, SHIFT_DIM)
    wkv_p = _state_from_kernel(s_p).reshape(1, bp, RWKV_HEADS, RWKV_HEAD, RWKV_HEAD)
    conv_s = ctail[n_prompt_tiles:].reshape(1, db, CONV_WIDTH - 1, CONV_DIM)
    shift_s = stail[n_prompt_tiles:].reshape(1, db, 1, SHIFT_DIM)
    wkv_s = _state_from_kernel(s_s).reshape(1, db, RWKV_HEADS, RWKV_HEAD, RWKV_HEAD)
    return (y_prompt, y_sample, conv_p, shift_p, wkv_p, conv_s, shift_s, wkv_s)
```

```python
import functools

import numpy as np
import jax
import jax.numpy as jnp
from jax import lax
from jax.experimental import pallas as pl
from jax.experimental.pallas import tpu as pltpu

F32 = jnp.float32
BF16 = jnp.bfloat16

CONV_DIM = 512
CONV_WIDTH = 3
RWKV_HEAD = 64
RWKV_HEADS = 8
RWKV_DIM = RWKV_HEADS * RWKV_HEAD
LORA_W = 64
LORA_A = 64
LORA_G = 128
SHIFT_DIM = 3 * RWKV_DIM + LORA_W + LORA_A + LORA_G
N_GROUPS = 4
EXPERTS_PER_GROUP = 8
N_EXPERTS = N_GROUPS * EXPERTS_PER_GROUP
TOP_K = 2
RMS_EPS = 1e-6
GN_EPS = 64e-5

LANES = 128
ROW_TILE = 256
POST_TILES = 2
SCAN_CHUNK = 64
SCAN_CHUNKS_PER_STEP = 4
MOE_ROWS = 512
HIST = 8
VMEM_LIMIT = 56 * 1024 * 1024


def _rms(x, g):
    return x * lax.rsqrt(jnp.mean(x * x, axis=-1, keepdims=True) + RMS_EPS) * g


def _split2_dot(x, ones_bf16):
    w = ones_bf16.shape[0]
    hi = x.astype(BF16)
    lo = (x - hi.astype(F32)).astype(BF16)
    parts = []
    for c0 in range(0, x.shape[1], w):
        both = jnp.concatenate([hi[:, c0:c0 + w], lo[:, c0:c0 + w]], axis=0)
        s = jnp.dot(both, ones_bf16, preferred_element_type=F32)
        parts.append(s[0:x.shape[0]] + s[x.shape[0]:])
    return jnp.concatenate(parts, axis=1)


def _pack_bf16_pairs(x):
    n = x.shape[1] // 2
    lo = pltpu.bitcast(x[:, :n].astype(BF16).astype(F32), jnp.uint32)
    hi = pltpu.bitcast(x[:, n:].astype(BF16).astype(F32), jnp.uint32)
    return hi | (lo >> 16)


def _unpack_bf16_pairs(p):
    lo = pltpu.bitcast(p << 16, F32)
    hi = pltpu.bitcast(p & jnp.uint32(0xFFFF0000), F32)
    return jnp.concatenate([lo, hi], axis=1)


def _stack3(w):
    hi = w.astype(BF16)
    lo = (w - hi.astype(F32)).astype(BF16)
    return jnp.concatenate([hi, hi, lo], axis=0)


def _dot_stack3(x, w_stack):
    hi = x.astype(BF16)
    lo = (x - hi.astype(F32)).astype(BF16)
    return jnp.dot(jnp.concatenate([hi, lo, hi], axis=1), w_stack, preferred_element_type=F32)


def _const_spec(shape):
    nd = len(shape)
    return pl.BlockSpec(shape, lambda *_: (0,) * nd)


def _interleave(*gens):
    live = list(gens)
    while live:
        for gen in list(live):
            try:
                next(gen)
            except StopIteration:
                live.remove(gen)


def _zero_after(x):
    return pltpu.bitcast(lax.shift_right_logical(pltpu.bitcast(x, jnp.uint32), jnp.uint32(32)), F32)


def _mixer_pre_tile(x_ref, stc_ref, sts_ref, n1_ref, wa_ref, wb_ref, wg_ref, convw_ref, mu_ref,
                    w0_ref, lw_ref, a0_ref, la_ref, lgw_ref, kk_ref, ka_ref, rk_ref, woa_ref, ones_ref,
                    yag_ref, sgb_ref, r_ref, w_ref, k_ref, v_ref, a_ref, b_ref, bonus_ref, g_ref,
                    ctail_ref, stail_ref, ccarry, scarry, *, sample, seq_len):
    tm = x_ref.shape[0]
    seqs = tm // seq_len
    h = _rms(x_ref[...], n1_ref[...]).astype(BF16)

    def prev_rows(val, k, heads):
        row = lax.broadcasted_iota(jnp.int32, val.shape, 0)
        out = pltpu.roll(val, k, axis=0)
        for r0, head in heads.items():
            out = jnp.where(row == r0, head, out)
        return out

    pb = jnp.dot(h, wb_ref[...], preferred_element_type=F32)
    old_s = scarry[...]
    if sample:
        s_heads = {j * seq_len: sts_ref[0, j:j + 1, :] for j in range(seqs)}
    else:
        s_heads = {0: old_s[HIST - 1:HIST, :]}
    prev = prev_rows(pb, 1, s_heads)
    for j in range(seqs):
        r1 = (j + 1) * seq_len
        stail_ref[0, j:j + 1, :] = pb[r1 - 1:r1, :]
    scarry[...] = pb[tm - HIST:tm, :] + _zero_after(old_s)

    def conv_and_gates():
        n_a = wa_ref.shape[1] // 3
        g_in = jnp.dot(h, wa_ref[:, 0:n_a], preferred_element_type=F32)
        yield
        g_out = jnp.dot(h, wa_ref[:, n_a:2 * n_a], preferred_element_type=F32)
        yield
        x_c = jnp.dot(h, wa_ref[:, 2 * n_a:3 * n_a], preferred_element_type=F32)
        bx = g_in * x_c
        old_c = ccarry[...]
        if sample:
            h1 = {j * seq_len: stc_ref[0, 2 * j + 1:2 * j + 2, :] for j in range(seqs)}
            h2 = {j * seq_len: stc_ref[0, 2 * j:2 * j + 1, :] for j in range(seqs)}
            h2.update({j * seq_len + 1: stc_ref[0, 2 * j + 1:2 * j + 2, :] for j in range(seqs)})
        else:
            h1 = {0: old_c[HIST - 1:HIST, :]}
            h2 = {0: old_c[HIST - 2:HIST - 1, :], 1: old_c[HIST - 1:HIST, :]}
        cw = convw_ref[...]
        conv = cw[0:1, :] * prev_rows(bx, 2, h2) + cw[1:2, :] * prev_rows(bx, 1, h1) + cw[2:3, :] * bx
        for j in range(seqs):
            r1 = (j + 1) * seq_len
            ctail_ref[0, 2 * j:2 * j + 2, :] = bx[r1 - 2:r1, :]
        ccarry[...] = bx[tm - HIST:tm, :] + _zero_after(old_c)
        yield
        y_a = jnp.dot((g_out * conv).astype(BF16), woa_ref[...], preferred_element_type=F32)
        yield
        d = wg_ref.shape[1] // 2
        half = d // 2
        for c0 in range(0, d, half):
            pg = jnp.dot(h, wg_ref[:, c0:c0 + half], preferred_element_type=F32)
            yag_ref[:, c0:c0 + half] = (jax.nn.sigmoid(pg) * y_a[:, c0:c0 + half]).astype(yag_ref.dtype)
            yield
        for c0 in range(0, d, half):
            pg = jnp.dot(h, wg_ref[:, d + c0:d + c0 + half], preferred_element_type=F32)
            sgb_ref[:, c0:c0 + half] = jax.nn.sigmoid(pg).astype(sgb_ref.dtype)
            yield

    def rwkv_pre():
        s = pb + (prev - pb) * mu_ref[...]
        o1, o2, o3 = RWKV_DIM, 2 * RWKV_DIM, 3 * RWKV_DIM
        r = s[:, 0:o1]
        k = s[:, o1:o2]
        v = s[:, o2:o3]
        s_l = s[:, o3:o3 + LORA_W + LORA_A]
        lg = s[:, o3 + LORA_W + LORA_A:]
        r_ref[...] = r
        v_ref[...] = v
        yield
        z = w0_ref[...] + _dot_stack3(jnp.tanh(s_l), lw_ref[...])
        w_log = -jax.nn.softplus(-z) - 0.5
        w_ref[...] = -jnp.exp(w_log)
        yield
        a = jax.nn.sigmoid(a0_ref[...] + _dot_stack3(s_l, la_ref[...]))
        yield
        g_ref[...] = _dot_stack3(jax.nn.sigmoid(lg), lgw_ref[...]).astype(g_ref.dtype)
        yield
        ones = ones_ref[...]
        kk = k * kk_ref[...]
        kk_n = kk / jnp.maximum(jnp.sqrt(_split2_dot(kk * kk, ones)), 1e-12)
        a_ref[...] = -kk_n
        b_ref[...] = kk_n * a
        yield
        k2 = k * (1.0 + (a - 1.0) * ka_ref[...])
        k_ref[...] = k2
        yield
        bonus_ref[...] = (_split2_dot(r * k2 * rk_ref[...], ones) * v).astype(bonus_ref.dtype)

    _interleave(conv_and_gates(), rwkv_pre())


def _mixer_pre_kernel(xp_ref, xs_ref, *refs, n_prompt_tiles, seq_len):
    i = pl.program_id(0)
    ccarry, scarry = refs[-2:]

    @pl.when(i == 0)
    def _():
        ccarry[...] = jnp.zeros(ccarry.shape, F32)
        scarry[...] = jnp.zeros(scarry.shape, F32)

    pl.when(i < n_prompt_tiles)(
        functools.partial(_mixer_pre_tile, xp_ref, *refs, sample=False, seq_len=seq_len))
    pl.when(i >= n_prompt_tiles)(
        functools.partial(_mixer_pre_tile, xs_ref, *refs, sample=True, seq_len=seq_len))


def _mixer_pre(x_p, x_s, st_conv_t, st_shift_t, p, *, seq_len, tm):
    n_p, d = x_p.shape
    n_tok = n_p + x_s.shape[0]
    n_prompt_tiles = n_p // tm
    n_tiles = n_tok // tm
    seqs = tm // seq_len
    row = lambda w: pl.BlockSpec((tm, w), lambda i: (i, 0))
    st_idx = lambda i: (jnp.maximum(i - n_prompt_tiles, 0), 0, 0)
    consts = [p['norm1_g'], p['w_in_a'], p['w_in_b'], p['w_in_g'], p['conv_w'], p['mu_shift'],
              p['w0'], p['w_lora_w'], p['a0'], p['w_lora_a'], p['w_lora_g'], p['k_k'], p['k_a'],
              p['r_k'], p['w_out_a'], p['ones_bf16']]
    in_specs = [pl.BlockSpec((tm, d), lambda i: (jnp.minimum(i, n_prompt_tiles - 1), 0)),
                pl.BlockSpec((tm, d), lambda i: (jnp.maximum(i - n_prompt_tiles, 0), 0)),
                pl.BlockSpec((1, 2 * seqs, CONV_DIM), st_idx),
                pl.BlockSpec((1, seqs, SHIFT_DIM), st_idx)] + [_const_spec(c.shape) for c in consts]
    sds = lambda w, dt=F32: jax.ShapeDtypeStruct((n_tok, w), dt)
    out_shape = [sds(d, BF16), sds(d, BF16)] + [sds(RWKV_DIM)] * 6 + [sds(RWKV_DIM, BF16)] * 2 + [
        jax.ShapeDtypeStruct((n_tiles, 2 * seqs, CONV_DIM), F32),
        jax.ShapeDtypeStruct((n_tiles, seqs, SHIFT_DIM), F32)]
    out_specs = [row(d), row(d)] + [row(RWKV_DIM)] * 8 + [
        pl.BlockSpec((1, 2 * seqs, CONV_DIM), lambda i: (i, 0, 0)),
        pl.BlockSpec((1, seqs, SHIFT_DIM), lambda i: (i, 0, 0))]
    kern = functools.partial(_mixer_pre_kernel, n_prompt_tiles=n_prompt_tiles, seq_len=seq_len)
    return pl.pallas_call(
        kern, out_shape=out_shape, grid=(n_tiles,), in_specs=in_specs, out_specs=out_specs,
        scratch_shapes=[pltpu.VMEM((HIST, CONV_DIM), F32), pltpu.VMEM((HIST, SHIFT_DIM), F32)],
        compiler_params=pltpu.CompilerParams(dimension_semantics=("arbitrary",),
                                             vmem_limit_bytes=VMEM_LIMIT),
        name="mixer_pre")(x_p, x_s, st_conv_t, st_shift_t, *consts)


GROUP_LANES = 256
HEADS_PER_GROUP = GROUP_LANES // RWKV_HEAD
N_HEAD_GROUPS = RWKV_DIM // GROUP_LANES
NN = (((1,), (0,)), ((), ()))
NT = (((1,), (1,)), ((), ()))


def _split2(x):
    hi = x.astype(BF16)
    lo = (x - hi.astype(F32)).astype(BF16)
    return hi, lo


def _mm(xs, ys, dims=NN):
    x1, x2 = xs
    y1, y2 = ys
    d = lambda p, q: lax.dot_general(p, q, dims, preferred_element_type=F32)
    m = x1.shape[0]
    both = d(jnp.concatenate([x1, x2], axis=0), y1)
    return both[0:m] + both[m:2 * m] + d(x1, y2)


def _cat2(ps, qs, axis):
    return tuple(jnp.concatenate([p, q], axis=axis) for p, q in zip(ps, qs))


def _wkv_masks(c):
    hc = HEADS_PER_GROUP * c
    levels = c.bit_length() - 1
    t = np.arange(c)[:, None]
    s = (np.arange(hc) % c)[None, :]
    tm = [s < t, s <= t]
    for lvl in range(1, levels + 1):
        half = 1 << (lvl - 1)
        tm.append(((t >> lvl) == (s >> lvl)) & ((t & half) != 0) & ((s & half) == 0))
    row_head = (np.arange(hc) // c)[:, None]
    mfeat = row_head == (np.arange(GROUP_LANES) // RWKV_HEAD)[None, :]
    mpos = row_head == (np.arange(hc) // c)[None, :]
    lane_head = np.arange(GROUP_LANES) // RWKV_HEAD
    stmask = lane_head[:, None] == lane_head[None, :]
    tri = np.arange(c)[None, :] <= np.arange(c)[:, None]
    return (jnp.asarray(mfeat, BF16), jnp.asarray(mpos, BF16), jnp.asarray(np.stack(tm), F32),
            jnp.asarray(stmask, F32), jnp.asarray(tri, BF16))


def _wkv_pipe_kernel(r_ref, lw_ref, k_ref, v_ref, a_ref, b_ref, s0_ref, mfeat_ref, mpos_ref,
                     tmask_ref, stmask_ref, tri_ref, y_ref, sout_ref,
                     state, sv_ar, sv_inv, sv_akv, sv_arbk, sv_vbd, sv_v, sv_bk, sv_pend,
                     *, chunk, n_steps, chained):
    i = pl.program_id(0)
    c = chunk
    n_chunks = r_ref.shape[0] // c
    hc = HEADS_PER_GROUP * c
    levels = c.bit_length() - 1
    groups = range(N_HEAD_GROUPS)
    units = [(j, g) for j in range(n_chunks) for g in groups]
    uid = {u: n for n, u in enumerate(units)}

    stmask = stmask_ref[...]

    def expand(sc):
        pair = jnp.concatenate([sc, sc], axis=1)
        return jnp.concatenate([pair] * (GROUP_LANES // pair.shape[1]), axis=1) * stmask

    def compact(s):
        half = s[:, 0:GROUP_LANES // 2] + s[:, GROUP_LANES // 2:]
        return half[:, 0:RWKV_HEAD] + half[:, RWKV_HEAD:]

    @pl.when(i == 0)
    def _():
        for ref in (sv_ar, sv_inv, sv_akv, sv_arbk, sv_vbd, sv_v, sv_bk):
            ref[...] = jnp.zeros(ref.shape, ref.dtype)
        sv_pend[...] = jnp.ones(sv_pend.shape, F32)
        if chained:
            for g in groups:
                state[g] = expand(s0_ref[0, g])

    mfeat = mfeat_ref[...]
    mpos = mpos_ref[...]
    strict = tmask_ref[0]
    incl = tmask_ref[1]
    eye = incl - strict
    tri = tri_ref[...]

    def ld(ref, j, g):
        return ref[j * c:(j + 1) * c, g * GROUP_LANES:(g + 1) * GROUP_LANES]

    def bd_split(ps):
        mask = mpos if ps[0].shape[1] == hc else mfeat
        return tuple(jnp.concatenate([p] * HEADS_PER_GROUP, axis=0) * mask for p in ps)

    def bd2(m):
        return bd_split(_split2(m))

    def cumsum_rows(x):
        p1 = x.astype(BF16)
        r1 = x - p1.astype(F32)
        p2 = r1.astype(BF16)
        p3 = (r1 - p2.astype(F32)).astype(BF16)
        d = lambda q: jnp.dot(tri, q, preferred_element_type=F32)
        return d(p1) + d(p2) + d(p3)

    new = {}

    def prepare():
        cum = {u: cumsum_rows(ld(lw_ref, *u)) for u in units}
        yield
        ar, bk_end, p_end, a_ab, a_ak, a_rb, a_rk, v, vbd = ({} for _ in range(9))
        for u in units:
            cm = cum[u]
            cum_last = cm[c - 1:c, :]
            e_neg = jnp.exp(-cm)
            e_end = jnp.exp(cum_last - cm)
            b_raw = ld(b_ref, *u)
            k_raw = ld(k_ref, *u)
            ar[u] = _split2(jnp.concatenate([ld(a_ref, *u) * jnp.exp(cm - ld(lw_ref, *u)),
                                             ld(r_ref, *u) * jnp.exp(cm)], axis=0))
            bk_end[u] = _split2(jnp.concatenate([b_raw * e_end, k_raw * e_end], axis=0))
            p_end[u] = jnp.exp(cum_last)
            v[u] = ld(v_ref, *u)
            vbd[u] = bd2(v[u])
            gram = _mm(ar[u], _cat2(bd2(b_raw * e_neg), bd2(k_raw * e_neg), 0), NT)
            a_ab[u] = jnp.where(strict > 0, gram[0:c, 0:hc], 0.0)
            a_ak[u] = jnp.where(strict > 0, gram[0:c, hc:2 * hc], 0.0)
            a_rb[u] = jnp.where(incl > 0, gram[c:2 * c, 0:hc], 0.0)
            a_rk[u] = jnp.where(incl > 0, gram[c:2 * c, hc:2 * hc], 0.0)
            if uid[u] % 2 == 1:
                yield
        a_ab2 = {u: _split2(a_ab[u]) for u in units}
        inv = {u: eye + a_ab[u] * tmask_ref[2] for u in units}
        inv2 = {u: _split2(inv[u]) for u in units}
        for lvl in range(2, levels + 1):
            lm = tmask_ref[1 + lvl]
            t1 = {u: _mm(a_ab2[u], bd_split(inv2[u])) for u in units}
            yield
            inv = {u: inv[u] + lm * _mm(inv2[u], bd2(t1[u])) for u in units}
            inv2 = {u: _split2(inv[u]) for u in units}
            yield
        akv = {u: _mm(_split2(a_ak[u]), vbd[u]) for u in units}
        arbk = {u: _split2(jnp.concatenate([a_rb[u], a_rk[u]], axis=1)) for u in units}
        new.update(ar=ar, inv=inv2, akv=akv, arbk=arbk, vbd=vbd, v=v, bk=bk_end, pend=p_end)

    def serial():
        pair = lambda ref, n: (ref[0, n], ref[1, n])
        s_cur = [state[g] for g in groups] if chained else None
        for j in range(n_chunks):
            ns = [uid[j, g] for g in groups]
            s_prev = s_cur if chained else [expand(s0_ref[j, g]) for g in groups]
            x0 = [_mm(pair(sv_ar, n), _split2(s_prev[g]), NT) for g, n in zip(groups, ns)]
            yield
            uu = [_mm(pair(sv_inv, n), bd2(x0[g][0:c] + sv_akv[n])) for g, n in zip(groups, ns)]
            yield
            for g, n in zip(groups, ns):
                yy = x0[g][c:2 * c] + _mm(pair(sv_arbk, n), _cat2(bd2(uu[g]), pair(sv_vbd, n), 0))
                y_ref[j * c:(j + 1) * c, g * GROUP_LANES:(g + 1) * GROUP_LANES] = yy
            s_new = []
            for g, n in zip(groups, ns):
                uv_t = jnp.transpose(jnp.concatenate([uu[g], sv_v[n]], axis=0))
                upd = _mm(_split2(uv_t), pair(sv_bk, n))
                s_new.append(s_prev[g] * sv_pend[n, 0:1, :] + stmask * upd)
            yield
            if chained:
                s_cur = s_new
            else:
                for g in groups:
                    sout_ref[j, g] = compact(s_new[g])
        if chained:
            for g in groups:
                state[g] = s_cur[g]
        new['last_state'] = s_new

    _interleave(serial(), prepare())

    zero = sum(_zero_after(s[0:8, 0:LANES]) for s in new['last_state'])[0:1, 0:1]
    zero_bf = zero.astype(BF16)
    for u, n in uid.items():
        for name, ref in (('ar', sv_ar), ('inv', sv_inv), ('arbk', sv_arbk), ('vbd', sv_vbd), ('bk', sv_bk)):
            for half in range(2):
                ref[half, n] = new[name][u][half] + zero_bf
        sv_akv[n] = new['akv'][u] + zero
        sv_v[n] = new['v'][u] + zero
        sv_pend[n] = jnp.broadcast_to(new['pend'][u] + zero, sv_pend.shape[1:])

    if chained:
        @pl.when(i == n_steps)
        def _():
            for g in groups:
                sout_ref[0, g] = compact(state[g])


def _wkv_scan(rlkvab, s0, *, row0, n_rows, chunk, chunks_per_step, chained):
    rows = chunk * chunks_per_step
    n_steps = n_rows // rows
    assert n_rows % rows == 0 and row0 % rows == 0
    masks = _wkv_masks(chunk)
    hc = HEADS_PER_GROUP * chunk
    n_units = chunks_per_step * N_HEAD_GROUPS
    st = (N_HEAD_GROUPS, GROUP_LANES, RWKV_HEAD)
    prev = lambda i: jnp.maximum(i - 1, 0)
    row_in = pl.BlockSpec((rows, RWKV_DIM), lambda i: (row0 // rows + jnp.minimum(i, n_steps - 1), 0))
    if chained:
        n_state = 1
        st_spec = pl.BlockSpec((1,) + st, lambda i: (0, 0, 0, 0))
    else:
        n_state = n_rows // chunk
        st_spec = pl.BlockSpec((chunks_per_step,) + st, lambda i: (prev(i), 0, 0, 0))
    in_specs = [row_in] * 6 + [st_spec] + [_const_spec(m.shape) for m in masks]
    out_shape = [jax.ShapeDtypeStruct((n_rows, RWKV_DIM), F32),
                 jax.ShapeDtypeStruct((n_state,) + st, F32)]
    out_specs = [pl.BlockSpec((rows, RWKV_DIM), lambda i: (prev(i), 0)), st_spec]
    scratch = [pltpu.VMEM((N_HEAD_GROUPS, GROUP_LANES, GROUP_LANES), F32),
               pltpu.VMEM((2, n_units, 2 * chunk, GROUP_LANES), BF16),
               pltpu.VMEM((2, n_units, chunk, hc), BF16),
               pltpu.VMEM((n_units, chunk, GROUP_LANES), F32),
               pltpu.VMEM((2, n_units, chunk, 2 * hc), BF16),
               pltpu.VMEM((2, n_units, hc, GROUP_LANES), BF16),
               pltpu.VMEM((n_units, chunk, GROUP_LANES), F32),
               pltpu.VMEM((2, n_units, 2 * chunk, GROUP_LANES), BF16),
               pltpu.VMEM((n_units, 8, GROUP_LANES), F32)]
    kern = functools.partial(_wkv_pipe_kernel, chunk=chunk, n_steps=n_steps, chained=chained)
    return pl.pallas_call(
        kern, out_shape=out_shape, grid=(n_steps + 1,), in_specs=in_specs, out_specs=out_specs,
        scratch_shapes=scratch,
        compiler_params=pltpu.CompilerParams(dimension_semantics=("arbitrary",),
                                             vmem_limit_bytes=VMEM_LIMIT),
        name="wkv_scan")(*rlkvab, s0, *masks)


def _mixer_post_kernel(yp_ref, ys_ref, bonus_ref, g_ref, yag_ref, sgb_ref, xp_ref, xs_ref, lng_ref, lnb_ref,
                       wob_ref, wo_ref, n2_ref, wr_ref, br_ref, ones_ref, tril_ref,
                       x1_ref, h2_ref, route_ref, counts_ref, cnt, *, n_prompt_tiles):
    i = pl.program_id(0)

    @pl.when(i == 0)
    def _():
        cnt[...] = jnp.zeros(cnt.shape, F32)

    tm = x1_ref.shape[0]
    n_parts = tm // ROW_TILE
    rp = tm // n_parts
    is_prompt = i < n_prompt_tiles
    ones = ones_ref[...]
    neg = jnp.float32(-jnp.inf)
    big = jnp.int32(LANES)
    lane = lax.broadcasted_iota(jnp.int32, (rp, LANES), 1)
    picked = {}

    def part(q):
        rs = slice(q * rp, (q + 1) * rp)
        y = jnp.where(is_prompt, yp_ref[rs, :], ys_ref[rs, :])
        inv_n = 1.0 / RWKV_HEAD
        mean = _split2_dot(y, ones) * inv_n
        yc = y - mean
        var = _split2_dot(yc * yc, ones) * inv_n
        yn = yc * lax.rsqrt(var + GN_EPS) * lng_ref[...] + lnb_ref[...]
        yy = (yn + bonus_ref[rs, :].astype(F32)) * g_ref[rs, :].astype(F32)
        yield
        y_b = jnp.dot(yy.astype(BF16), wob_ref[...], preferred_element_type=F32)
        merged = yag_ref[rs, :].astype(F32) + sgb_ref[rs, :].astype(F32) * y_b
        yield
        x = jnp.where(is_prompt, xp_ref[rs, :], xs_ref[rs, :])
        x1 = x + jnp.dot(merged.astype(BF16), wo_ref[...], preferred_element_type=F32)
        x1_ref[rs, :] = x1
        h2 = _rms(x1, n2_ref[...])
        h2_ref[rs, :] = _pack_bf16_pairs(h2)
        yield
        logits = _dot_stack3(h2, wr_ref[...]) + br_ref[...]
        yield
        is_g = lane < N_GROUPS
        lgp = jnp.where(is_g, logits, neg)
        m_g = jnp.max(lgp, axis=-1, keepdims=True)
        grp = jnp.min(jnp.where(lgp == m_g, lane, big), axis=-1, keepdims=True)
        p_top = 1.0 / jnp.sum(jnp.where(is_g, jnp.exp(logits - m_g), 0.0), axis=-1, keepdims=True)
        e_lane = lane - N_GROUPS
        in_grp = (e_lane >= grp * EXPERTS_PER_GROUP) & (e_lane < (grp + 1) * EXPERTS_PER_GROUP)
        le = jnp.where(in_grp, logits, neg)
        m1 = jnp.max(le, axis=-1, keepdims=True)
        i1 = jnp.min(jnp.where(le == m1, lane, big), axis=-1, keepdims=True)
        le2 = jnp.where(lane == i1, neg, le)
        m2 = jnp.max(le2, axis=-1, keepdims=True)
        i2 = jnp.min(jnp.where(le2 == m2, lane, big), axis=-1, keepdims=True)
        ex = jnp.exp(m2 - m1)
        picked[q] = (i1 - N_GROUPS, i2 - N_GROUPS, p_top / (1.0 + ex), p_top * ex / (1.0 + ex))

    gens = [part(q) for q in range(n_parts)]
    next(gens[0])
    _interleave(*gens)

    hot = [[lane == e for e in picked[q][0:2]] for q in range(n_parts)]
    both = jnp.concatenate([jnp.where(h1, 1.0, jnp.where(h2_, 1.0, 0.0)) for h1, h2_ in hot], axis=0)
    before = jnp.dot(tril_ref[...], both.astype(BF16), preferred_element_type=F32) + cnt[0:1, :]
    cnt[0:1, :] = cnt[0:1, :] + jnp.sum(both, axis=0, keepdims=True)
    counts_ref[...] = jnp.broadcast_to(cnt[0:1, :], counts_ref.shape)
    for q in range(n_parts):
        rs = slice(q * rp, (q + 1) * rp)
        e1, e2, w1, w2 = picked[q]
        ranks = [jnp.sum(jnp.where(h, before[rs, :], 0.0), axis=-1, keepdims=True) for h in hot[q]]
        route = jnp.zeros((rp, LANES), F32)
        for c, col in enumerate([e1.astype(F32), e2.astype(F32), w1, w2] + ranks):
            route = jnp.where(lane == c, col, route)
        route_ref[rs, :] = route


ROUTE_EXPERT, ROUTE_WEIGHT, ROUTE_RANK = 0, 2, 4


def _mixer_post(y_p, y_s, bonus, g, yag, sgb, x_p, x_s, p, *, tm):
    n_p, d = x_p.shape
    n_tok = n_p + x_s.shape[0]
    npt = n_p // tm
    row = lambda w: pl.BlockSpec((tm, w), lambda i: (i, 0))
    tril = jnp.asarray(np.arange(tm)[None, :] < np.arange(tm)[:, None], BF16)
    consts = [p['lnx_g'], p['lnx_b'], p['w_out_b'], p['w_o'], p['norm2_g'], p['w_router'],
              p['b_router'], p['ones_bf16'], tril]
    pair = lambda w: [pl.BlockSpec((tm, w), lambda i: (jnp.minimum(i, npt - 1), 0)),
                      pl.BlockSpec((tm, w), lambda i: (jnp.maximum(i - npt, 0), 0))]
    in_specs = (pair(RWKV_DIM) + [row(RWKV_DIM)] * 2 + [row(d)] * 2 + pair(d)
                + [_const_spec(c.shape) for c in consts])
    out_shape = [jax.ShapeDtypeStruct((n_tok, d), F32), jax.ShapeDtypeStruct((n_tok, d // 2), jnp.uint32),
                 jax.ShapeDtypeStruct((n_tok, LANES), F32), jax.ShapeDtypeStruct((8, LANES), F32)]
    out_specs = [row(d), row(d // 2), row(LANES), _const_spec((8, LANES))]
    return pl.pallas_call(
        functools.partial(_mixer_post_kernel, n_prompt_tiles=npt), out_shape=out_shape,
        grid=(n_tok // tm,), in_specs=in_specs, out_specs=out_specs,
        scratch_shapes=[pltpu.VMEM((8, LANES), F32)],
        compiler_params=pltpu.CompilerParams(dimension_semantics=("arbitrary",),
                                             vmem_limit_bytes=VMEM_LIMIT),
        name="mixer_post")(y_p, y_s, bonus, g, yag, sgb, x_p, x_s, *consts)


N_DMA_PRIORITIES = 2


def _gather_rows(idx_ref, src_hbm, dst, sem, n_rows, *, unrolled):
    def start(r, priority):
        pltpu.make_async_copy(src_hbm.at[pl.ds(idx_ref[r], 1)], dst.at[pl.ds(r, 1)],
                              sem).start(priority=priority)
    if unrolled:
        for r in range(n_rows):
            start(r, r % N_DMA_PRIORITIES)
    else:
        def body(r, carry):
            start(r, 0)
            return carry
        lax.fori_loop(0, n_rows, body, 0)


def _wait_rows(src_hbm, dst, sem, n_rows):
    pltpu.make_async_copy(src_hbm.at[pl.ds(0, n_rows)], dst, sem).wait()


def _moe_scatter_kernel(dest_ref, h_ref, xs_in_hbm, xs_hbm, sem):
    del xs_in_hbm
    n = dest_ref.shape[-1]
    tm = h_ref.shape[0]
    for r in range(n):
        pltpu.make_async_copy(h_ref.at[pl.ds(r % tm, 1)], xs_hbm.at[pl.ds(dest_ref[0, 0, r], 1)],
                              sem).start(priority=r % N_DMA_PRIORITIES)
    for j in range(n // tm):
        pltpu.make_async_copy(h_ref, xs_hbm.at[pl.ds(0, tm)], sem).wait()


def _moe_scatter(h2, pos3, n_rows):
    n_tok, d = h2.shape
    n_tiles, _, n = pos3.shape
    tm = n // TOP_K
    xs0 = jnp.zeros((n_rows, d), h2.dtype)
    return pl.pallas_call(
        _moe_scatter_kernel, out_shape=jax.ShapeDtypeStruct((n_rows, d), h2.dtype), grid=(n_tiles,),
        in_specs=[pl.BlockSpec((1, 1, n), lambda i: (i, 0, 0), memory_space=pltpu.SMEM),
                  pl.BlockSpec((tm, d), lambda i: (i, 0)), pl.BlockSpec(memory_space=pl.ANY)],
        out_specs=pl.BlockSpec(memory_space=pl.ANY),
        scratch_shapes=[pltpu.SemaphoreType.DMA(())], input_output_aliases={2: 0},
        compiler_params=pltpu.CompilerParams(dimension_semantics=("arbitrary",), has_side_effects=True,
                                             vmem_limit_bytes=VMEM_LIMIT),
        name="moe_scatter")(pos3, h2, xs0)


def _moe_experts_kernel(bexp_ref, nused_ref, x_ref, wg_ref, wu_ref, wd_ref, yb_ref, wg_bf, wu_bf, wd_bf):
    i = pl.program_id(0)

    @pl.when((i == 0) | (bexp_ref[i] != bexp_ref[jnp.maximum(i - 1, 0)]))
    def _():
        wg_bf[...] = wg_ref[0].astype(BF16)
        wu_bf[...] = wu_ref[0].astype(BF16)
        wd_bf[...] = wd_ref[0].astype(BF16)

    @pl.when(i < nused_ref[0])
    def _():
        def part(rs):
            xb = _unpack_bf16_pairs(x_ref[rs, :]).astype(BF16)
            hg = jnp.dot(xb, wg_bf[...], preferred_element_type=F32)
            yield
            hu = jnp.dot(xb, wu_bf[...], preferred_element_type=F32)
            hid = (hg * jax.nn.sigmoid(hg)) * hu
            yield
            yb_ref[rs, :] = _pack_bf16_pairs(jnp.dot(hid.astype(BF16), wd_bf[...], preferred_element_type=F32))

        rows = x_ref.shape[0]
        _interleave(*[part(slice(r0, r0 + ROW_TILE)) for r0 in range(0, rows, ROW_TILE)])

    @pl.when(i >= nused_ref[0])
    def _():
        yb_ref[...] = jnp.zeros(yb_ref.shape, yb_ref.dtype)


def _moe_experts(xs, block_expert, n_used, w_eg, w_eu, w_ed, *, blk):
    n_rows, dp = xs.shape
    n_blocks = n_rows // blk
    d, de = w_eg.shape[1:]
    grid_spec = pltpu.PrefetchScalarGridSpec(
        num_scalar_prefetch=2, grid=(n_blocks,),
        in_specs=[
            pl.BlockSpec((blk, dp), lambda i, be, nu: (jnp.minimum(i, nu[0] - 1), 0)),
            pl.BlockSpec((1, d, de), lambda i, be, nu: (be[i], 0, 0)),
            pl.BlockSpec((1, d, de), lambda i, be, nu: (be[i], 0, 0)),
            pl.BlockSpec((1, de, d), lambda i, be, nu: (be[i], 0, 0)),
        ],
        out_specs=pl.BlockSpec((blk, dp), lambda i, be, nu: (i, 0)),
        scratch_shapes=[pltpu.VMEM((d, de), BF16), pltpu.VMEM((d, de), BF16), pltpu.VMEM((de, d), BF16)])
    return pl.pallas_call(
        _moe_experts_kernel, out_shape=jax.ShapeDtypeStruct((n_rows, dp), xs.dtype), grid_spec=grid_spec,
        compiler_params=pltpu.CompilerParams(dimension_semantics=("arbitrary",),
                                             vmem_limit_bytes=VMEM_LIMIT),
        name="moe_experts")(block_expert, n_used, xs, w_eg, w_eu, w_ed)


GATHER_AHEAD = 2


def _with_dummy_blocks(idx, n_blocks, blk):
    pad = jnp.zeros((GATHER_AHEAD * blk,), idx.dtype)
    return jnp.concatenate([idx, pad]).reshape(n_blocks + GATHER_AHEAD, 1, blk)


def _moe_combine_kernel(*refs, n_prompt_tiles):
    pos_refs = refs[:GATHER_AHEAD + 1]
    x1_ref, route_ref, nf_ref, yb_hbm, outp_ref, outs_ref, ybuf, sems = refs[GATHER_AHEAD + 1:]
    i = pl.program_id(0)
    n_slots = GATHER_AHEAD + 1
    rows = ybuf.shape[1]
    slot = i % n_slots

    @pl.when(i == 0)
    def _():
        for a in range(GATHER_AHEAD):
            _gather_rows(pos_refs[a].at[0, 0], yb_hbm, ybuf.at[a], sems.at[a], rows, unrolled=False)

    _wait_rows(yb_hbm, ybuf.at[slot], sems.at[slot], rows)
    ahead = (i + GATHER_AHEAD) % n_slots
    _gather_rows(pos_refs[GATHER_AHEAD].at[0, 0], yb_hbm, ybuf.at[ahead], sems.at[ahead], rows, unrolled=True)
    tm = x1_ref.shape[0]
    route = route_ref[...]
    x2 = x1_ref[...]
    for j in range(TOP_K):
        yj = _unpack_bf16_pairs(ybuf[slot, j * tm:(j + 1) * tm, :])
        x2 = x2 + yj * route[:, ROUTE_WEIGHT + j:ROUTE_WEIGHT + j + 1]
    out = _rms(x2, nf_ref[...])

    @pl.when(i < n_prompt_tiles)
    def _():
        outp_ref[...] = out

    @pl.when(i >= n_prompt_tiles)
    def _():
        outs_ref[...] = out

    @pl.when(i == pl.num_programs(0) - 1)
    def _():
        for a in range(1, n_slots):
            s = (i + a) % n_slots
            _wait_rows(yb_hbm, ybuf.at[s], sems.at[s], rows)


def _moe_combine(pos_tiles, x1, route, normf_g, yb, *, n_prompt_rows, tm):
    n_tok, d = x1.shape
    n_tiles = n_tok // tm
    npt = n_prompt_rows // tm
    pos3 = _with_dummy_blocks(pos_tiles.reshape(-1), n_tiles, TOP_K * tm)
    pos_spec = lambda a: pl.BlockSpec((1, 1, TOP_K * tm), lambda i: (i + a, 0, 0), memory_space=pltpu.SMEM)
    in_specs = [pos_spec(a) for a in range(GATHER_AHEAD + 1)] + [
        pl.BlockSpec((tm, d), lambda i: (i, 0)),
        pl.BlockSpec((tm, LANES), lambda i: (i, 0)),
        _const_spec(normf_g.shape),
        pl.BlockSpec(memory_space=pl.ANY),
    ]
    out_shape = [jax.ShapeDtypeStruct((n_prompt_rows, d), F32),
                 jax.ShapeDtypeStruct((n_tok - n_prompt_rows, d), F32)]
    out_specs = [pl.BlockSpec((tm, d), lambda i: (jnp.minimum(i, npt - 1), 0)),
                 pl.BlockSpec((tm, d), lambda i: (jnp.maximum(i - npt, 0), 0))]
    return pl.pallas_call(
        functools.partial(_moe_combine_kernel, n_prompt_tiles=npt), out_shape=out_shape,
        grid=(n_tiles,), in_specs=in_specs, out_specs=out_specs,
        scratch_shapes=[pltpu.VMEM((GATHER_AHEAD + 1, TOP_K * tm, yb.shape[1]), yb.dtype),
                        pltpu.SemaphoreType.DMA((GATHER_AHEAD + 1,))],
        compiler_params=pltpu.CompilerParams(dimension_semantics=("arbitrary",),
                                             vmem_limit_bytes=VMEM_LIMIT),
        name="moe_combine")(*([pos3] * (GATHER_AHEAD + 1)), x1, route, normf_g, yb)


def _dispatch(route, counts, blk, tm):
    n_tok = route.shape[0]
    n_assign = n_tok * TOP_K
    expert = route[:, ROUTE_EXPERT:ROUTE_EXPERT + TOP_K].astype(jnp.int32)
    rank = route[:, ROUTE_RANK:ROUTE_RANK + TOP_K].astype(jnp.int32)
    counts = counts[0, :N_EXPERTS].astype(jnp.int32)
    padded = (counts + blk - 1) // blk * blk
    pad_end = jnp.cumsum(padded)
    pad_start = pad_end - padded
    is_e = expert[:, :, None] == jnp.arange(N_EXPERTS, dtype=jnp.int32)
    dest = jnp.sum(jnp.where(is_e, pad_start, 0), axis=-1) + rank
    n_blocks = -(-n_assign // blk) + N_EXPERTS
    block_start = jnp.arange(n_blocks, dtype=jnp.int32) * blk
    block_expert = jnp.minimum(jnp.sum(block_start[:, None] >= pad_end[None, :], axis=1),
                               N_EXPERTS - 1).astype(jnp.int32)
    n_used = (pad_end[-1] // blk).reshape(1)
    pos_tiles = dest.reshape(n_tok // tm, tm, TOP_K).transpose(0, 2, 1).reshape(n_tok // tm, 1, TOP_K * tm)
    return pos_tiles, block_expert, n_used, n_blocks


def _state_to_kernel(s):
    return s.reshape(s.shape[0], N_HEAD_GROUPS, GROUP_LANES, RWKV_HEAD)


def _state_from_kernel(s):
    return s.reshape(s.shape[0], RWKV_HEADS, RWKV_HEAD, RWKV_HEAD)


def kernel(x_prompt, x_sample, state_conv, state_shift, state_wkv, norm1_g, w_in, conv_w, mu_shift, w0, w_lora_w, a0, w_lora_a, w_lora_g, k_k, k_a, r_k, lnx_g, lnx_b, w_out_a, w_out_b, w_o, norm2_g, w_router_group, b_router_group, w_router_expert, b_router_expert, w_e_gate, w_e_up, w_e_down, normf_g):
    depth = norm1_g.shape[0]
    bp, seq, d = x_prompt.shape
    db, dseq, _ = x_sample.shape
    assert depth == 1 and bp == 1, "single layer, single prompt stream"
    tm = ROW_TILE
    n_p, n_s = bp * seq, db * dseq
    assert n_p % tm == 0 and n_s % tm == 0 and tm % dseq == 0
    n_prompt_tiles, n_sample_tiles = n_p // tm, n_s // tm
    seqs = tm // dseq
    n_tok = n_p + n_s

    x_p = x_prompt.reshape(n_p, d)
    x_s = x_sample.reshape(n_s, d)

    l = 0
    c3 = 3 * CONV_DIM
    head_id = np.arange(GROUP_LANES) // RWKV_HEAD
    ones_bf16 = jnp.asarray(head_id[:, None] == head_id[None, :], BF16)
    zpad = jnp.zeros((LORA_W, RWKV_DIM), F32)
    n_r = N_GROUPS + N_EXPERTS
    p = {
        'norm1_g': norm1_g[l].reshape(1, d),
        'w_in_a': w_in[l][:, :c3].astype(BF16),
        'w_in_b': w_in[l][:, c3:c3 + SHIFT_DIM].astype(BF16),
        'w_in_g': w_in[l][:, c3 + SHIFT_DIM:].astype(BF16),
        'conv_w': conv_w[l],
        'mu_shift': mu_shift[l].reshape(1, SHIFT_DIM),
        'w0': w0[l].reshape(1, RWKV_DIM),
        'w_lora_w': _stack3(jnp.concatenate([w_lora_w[l], zpad], axis=0)),
        'a0': a0[l].reshape(1, RWKV_DIM),
        'w_lora_a': _stack3(jnp.concatenate([zpad, w_lora_a[l]], axis=0)),
        'w_lora_g': _stack3(w_lora_g[l]),
        'k_k': k_k[l].reshape(1, RWKV_DIM),
        'k_a': k_a[l].reshape(1, RWKV_DIM),
        'r_k': r_k[l].reshape(1, RWKV_DIM),
        'w_out_a': w_out_a[l].astype(BF16),
        'ones_bf16': ones_bf16,
        'lnx_g': lnx_g[l].reshape(1, RWKV_DIM),
        'lnx_b': lnx_b[l].reshape(1, RWKV_DIM),
        'w_out_b': w_out_b[l].astype(BF16),
        'w_o': w_o[l].astype(BF16),
        'norm2_g': norm2_g[l].reshape(1, d),
        'w_router': _stack3(jnp.pad(jnp.concatenate([w_router_group[l], w_router_expert[l]], axis=1),
                                    ((0, 0), (0, LANES - n_r)))),
        'b_router': jnp.pad(jnp.concatenate([b_router_group[l], b_router_expert[l]]),
                            (0, LANES - n_r)).reshape(1, LANES),
    }
    st_conv_t = state_conv[l].reshape(n_sample_tiles, seqs * (CONV_WIDTH - 1), CONV_DIM)
    st_shift_t = state_shift[l].reshape(n_sample_tiles, seqs, SHIFT_DIM)

    (yag, sgb, r, w, k, v, a, b, bonus, g, ctail, stail) = _mixer_pre(
        x_p, x_s, st_conv_t, st_shift_t, p, seq_len=dseq, tm=tm)

    s0_prompt = jnp.zeros((1, N_HEAD_GROUPS, GROUP_LANES, RWKV_HEAD), F32)
    s0_sample = _state_to_kernel(state_wkv[l])
    rlkvab = (r, w, k, v, a, b)
    cp = min(SCAN_CHUNK, n_p)
    assert cp & (cp - 1) == 0 and dseq & (dseq - 1) == 0
    y_p, s_p = _wkv_scan(rlkvab, s0_prompt, row0=0, n_rows=n_p, chunk=cp,
                         chunks_per_step=SCAN_CHUNKS_PER_STEP, chained=True)
    y_s, s_s = _wkv_scan(rlkvab, s0_sample, row0=n_p, n_rows=n_s, chunk=dseq,
                         chunks_per_step=SCAN_CHUNKS_PER_STEP, chained=False)

    post_tm = POST_TILES * tm
    assert n_p % post_tm == 0 and n_s % post_tm == 0
    x1, h2, route, counts = _mixer_post(y_p, y_s, bonus, g, yag, sgb, x_p, x_s, p, tm=post_tm)

    blk = MOE_ROWS
    pos_tiles, block_expert, n_used, n_blocks = _dispatch(route, counts, blk, tm)
    xs = _moe_scatter(h2, pos_tiles, n_blocks * blk)
    yb = _moe_experts(xs, block_expert, n_used, w_e_gate[l], w_e_up[l], w_e_down[l], blk=blk)
    out_p, out_s = _moe_combine(pos_tiles, x1, route, normf_g.reshape(1, d), yb, n_prompt_rows=n_p, tm=tm)

    y_prompt = out_p.reshape(bp, seq, d)
    y_sample = out_s.reshape(db, dseq, d)
    conv_p = ctail[n_prompt_tiles - 1, 2 * (seqs - 1):2 * seqs].reshape(1, bp, CONV_WIDTH - 1, CONV_DIM)
    shift_p = stail[n_prompt_tiles - 1, seqs - 1].reshape(1, bp, 1, SHIFT_DIM)
    wkv_p = _state_from_kernel(s_p).reshape(1, bp, RWKV_HEADS, RWKV_HEAD, RWKV_HEAD)
    conv_s = ctail[n_prompt_tiles:].reshape(1, db, CONV_WIDTH - 1, CONV_DIM)
    shift_s = stail[n_prompt_tiles:].reshape(1, db, 1, SHIFT_DIM)
    wkv_s = _state_from_kernel(s_s).reshape(1, db, RWKV_HEADS, RWKV_HEAD, RWKV_HEAD)
    return (y_prompt, y_sample, conv_p, shift_p, wkv_p, conv_s, shift_s, wkv_s)
```

```python
import functools

import numpy as np
import jax
import jax.numpy as jnp
from jax import lax
from jax.experimental import pallas as pl
from jax.experimental.pallas import tpu as pltpu

F32 = jnp.float32
BF16 = jnp.bfloat16

CONV_DIM = 512
CONV_WIDTH = 3
RWKV_HEAD = 64
RWKV_HEADS = 8
RWKV_DIM = RWKV_HEADS * RWKV_HEAD
LORA_W = 64
LORA_A = 64
LORA_G = 128
SHIFT_DIM = 3 * RWKV_DIM + LORA_W + LORA_A + LORA_G
N_GROUPS = 4
EXPERTS_PER_GROUP = 8
N_EXPERTS = N_GROUPS * EXPERTS_PER_GROUP
TOP_K = 2
RMS_EPS = 1e-6
GN_EPS = 64e-5

LANES = 128
ROW_TILE = 256
POST_TILES = 2
SCAN_CHUNK = 64
SCAN_CHUNKS_PER_STEP = 4
MOE_ROWS = 512
HIST = 8
VMEM_LIMIT = 56 * 1024 * 1024


def _rms(x, g):
    return x * lax.rsqrt(jnp.mean(x * x, axis=-1, keepdims=True) + RMS_EPS) * g


def _split2_dot(x, ones_bf16):
    w = ones_bf16.shape[0]
    hi = x.astype(BF16)
    lo = (x - hi.astype(F32)).astype(BF16)
    parts = []
    for c0 in range(0, x.shape[1], w):
        both = jnp.concatenate([hi[:, c0:c0 + w], lo[:, c0:c0 + w]], axis=0)
        s = jnp.dot(both, ones_bf16, preferred_element_type=F32)
        parts.append(s[0:x.shape[0]] + s[x.shape[0]:])
    return jnp.concatenate(parts, axis=1)


def _pack_bf16_pairs(x):
    n = x.shape[1] // 2
    lo = pltpu.bitcast(x[:, :n].astype(BF16).astype(F32), jnp.uint32)
    hi = pltpu.bitcast(x[:, n:].astype(BF16).astype(F32), jnp.uint32)
    return hi | (lo >> 16)


def _unpack_bf16_pairs(p):
    lo = pltpu.bitcast(p << 16, F32)
    hi = pltpu.bitcast(p & jnp.uint32(0xFFFF0000), F32)
    return jnp.concatenate([lo, hi], axis=1)


def _stack3(w):
    hi = w.astype(BF16)
    lo = (w - hi.astype(F32)).astype(BF16)
    return jnp.concatenate([hi, hi, lo], axis=0)


def _dot_stack3(x, w_stack):
    hi = x.astype(BF16)
    lo = (x - hi.astype(F32)).astype(BF16)
    return jnp.dot(jnp.concatenate([hi, lo, hi], axis=1), w_stack, preferred_element_type=F32)


def _const_spec(shape):
    nd = len(shape)
    return pl.BlockSpec(shape, lambda *_: (0,) * nd)


def _interleave(*gens):
    live = list(gens)
    while live:
        for gen in list(live):
            try:
                next(gen)
            except StopIteration:
                live.remove(gen)


def _zero_after(x):
    return pltpu.bitcast(lax.shift_right_logical(pltpu.bitcast(x, jnp.uint32), jnp.uint32(32)), F32)


def _mixer_pre_tile(x_ref, stc_ref, sts_ref, n1_ref, wa_ref, wb_ref, wg_ref, convw_ref, mu_ref,
                    w0_ref, lw_ref, a0_ref, la_ref, lgw_ref, kk_ref, ka_ref, rk_ref, woa_ref, ones_ref,
                    yag_ref, sgb_ref, r_ref, w_ref, k_ref, v_ref, a_ref, b_ref, bonus_ref, g_ref,
                    ctail_ref, stail_ref, ccarry, scarry, *, sample, seq_len):
    tm = x_ref.shape[0]
    seqs = tm // seq_len
    h = _rms(x_ref[...], n1_ref[...]).astype(BF16)

    def prev_rows(val, k, heads):
        row = lax.broadcasted_iota(jnp.int32, val.shape, 0)
        out = pltpu.roll(val, k, axis=0)
        for r0, head in heads.items():
            out = jnp.where(row == r0, head, out)
        return out

    pb = jnp.dot(h, wb_ref[...], preferred_element_type=F32)
    old_s = scarry[...]
    if sample:
        s_heads = {j * seq_len: sts_ref[0, j:j + 1, :] for j in range(seqs)}
    else:
        s_heads = {0: old_s[HIST - 1:HIST, :]}
    prev = prev_rows(pb, 1, s_heads)
    for j in range(seqs):
        r1 = (j + 1) * seq_len
        stail_ref[0, j:j + 1, :] = pb[r1 - 1:r1, :]
    scarry[...] = pb[tm - HIST:tm, :] + _zero_after(old_s)

    def conv_and_gates():
        n_a = wa_ref.shape[1] // 3
        g_in = jnp.dot(h, wa_ref[:, 0:n_a], preferred_element_type=F32)
        yield
        g_out = jnp.dot(h, wa_ref[:, n_a:2 * n_a], preferred_element_type=F32)
        yield
        x_c = jnp.dot(h, wa_ref[:, 2 * n_a:3 * n_a], preferred_element_type=F32)
        bx = g_in * x_c
        old_c = ccarry[...]
        if sample:
            h1 = {j * seq_len: stc_ref[0, 2 * j + 1:2 * j + 2, :] for j in range(seqs)}
            h2 = {j * seq_len: stc_ref[0, 2 * j:2 * j + 1, :] for j in range(seqs)}
            h2.update({j * seq_len + 1: stc_ref[0, 2 * j + 1:2 * j + 2, :] for j in range(seqs)})
        else:
            h1 = {0: old_c[HIST - 1:HIST, :]}
            h2 = {0: old_c[HIST - 2:HIST - 1, :], 1: old_c[HIST - 1:HIST, :]}
        cw = convw_ref[...]
        conv = cw[0:1, :] * prev_rows(bx, 2, h2) + cw[1:2, :] * prev_rows(bx, 1, h1) + cw[2:3, :] * bx
        for j in range(seqs):
            r1 = (j + 1) * seq_len
            ctail_ref[0, 2 * j:2 * j + 2, :] = bx[r1 - 2:r1, :]
        ccarry[...] = bx[tm - HIST:tm, :] + _zero_after(old_c)
        yield
        y_a = jnp.dot((g_out * conv).astype(BF16), woa_ref[...], preferred_element_type=F32)
        yield
        d = wg_ref.shape[1] // 2
        half = d // 2
        for c0 in range(0, d, half):
            pg = jnp.dot(h, wg_ref[:, c0:c0 + half], preferred_element_type=F32)
            yag_ref[:, c0:c0 + half] = (jax.nn.sigmoid(pg) * y_a[:, c0:c0 + half]).astype(yag_ref.dtype)
            yield
        for c0 in range(0, d, half):
            pg = jnp.dot(h, wg_ref[:, d + c0:d + c0 + half], preferred_element_type=F32)
            sgb_ref[:, c0:c0 + half] = jax.nn.sigmoid(pg).astype(sgb_ref.dtype)
            yield

    def rwkv_pre():
        s = pb + (prev - pb) * mu_ref[...]
        o1, o2, o3 = RWKV_DIM, 2 * RWKV_DIM, 3 * RWKV_DIM
        r = s[:, 0:o1]
        k = s[:, o1:o2]
        v = s[:, o2:o3]
        s_l = s[:, o3:o3 + LORA_W + LORA_A]
        lg = s[:, o3 + LORA_W + LORA_A:]
        r_ref[...] = r
        v_ref[...] = v
        yield
        z = w0_ref[...] + _dot_stack3(jnp.tanh(s_l), lw_ref[...])
        w_log = -jax.nn.softplus(-z) - 0.5
        w_ref[...] = -jnp.exp(w_log)
        yield
        a = jax.nn.sigmoid(a0_ref[...] + _dot_stack3(s_l, la_ref[...]))
        yield
        g_ref[...] = _dot_stack3(jax.nn.sigmoid(lg), lgw_ref[...]).astype(g_ref.dtype)
        yield
        ones = ones_ref[...]
        kk = k * kk_ref[...]
        kk_n = kk / jnp.maximum(jnp.sqrt(_split2_dot(kk * kk, ones)), 1e-12)
        a_ref[...] = -kk_n
        b_ref[...] = kk_n * a
        yield
        k2 = k * (1.0 + (a - 1.0) * ka_ref[...])
        k_ref[...] = k2
        yield
        bonus_ref[...] = (_split2_dot(r * k2 * rk_ref[...], ones) * v).astype(bonus_ref.dtype)

    _interleave(conv_and_gates(), rwkv_pre())


def _mixer_pre_kernel(xp_ref, xs_ref, *refs, n_prompt_tiles, seq_len):
    i = pl.program_id(0)
    ccarry, scarry = refs[-2:]

    @pl.when(i == 0)
    def _():
        ccarry[...] = jnp.zeros(ccarry.shape, F32)
        scarry[...] = jnp.zeros(scarry.shape, F32)

    pl.when(i < n_prompt_tiles)(
        functools.partial(_mixer_pre_tile, xp_ref, *refs, sample=False, seq_len=seq_len))
    pl.when(i >= n_prompt_tiles)(
        functools.partial(_mixer_pre_tile, xs_ref, *refs, sample=True, seq_len=seq_len))


def _mixer_pre(x_p, x_s, st_conv_t, st_shift_t, p, *, seq_len, tm):
    n_p, d = x_p.shape
    n_tok = n_p + x_s.shape[0]
    n_prompt_tiles = n_p // tm
    n_tiles = n_tok // tm
    seqs = tm // seq_len
    row = lambda w: pl.BlockSpec((tm, w), lambda i: (i, 0))
    st_idx = lambda i: (jnp.maximum(i - n_prompt_tiles, 0), 0, 0)
    consts = [p['norm1_g'], p['w_in_a'], p['w_in_b'], p['w_in_g'], p['conv_w'], p['mu_shift'],
              p['w0'], p['w_lora_w'], p['a0'], p['w_lora_a'], p['w_lora_g'], p['k_k'], p['k_a'],
              p['r_k'], p['w_out_a'], p['ones_bf16']]
    in_specs = [pl.BlockSpec((tm, d), lambda i: (jnp.minimum(i, n_prompt_tiles - 1), 0)),
                pl.BlockSpec((tm, d), lambda i: (jnp.maximum(i - n_prompt_tiles, 0), 0)),
                pl.BlockSpec((1, 2 * seqs, CONV_DIM), st_idx),
                pl.BlockSpec((1, seqs, SHIFT_DIM), st_idx)] + [_const_spec(c.shape) for c in consts]
    sds = lambda w, dt=F32: jax.ShapeDtypeStruct((n_tok, w), dt)
    out_shape = [sds(d, BF16), sds(d, BF16)] + [sds(RWKV_DIM)] * 6 + [sds(RWKV_DIM, BF16)] * 2 + [
        jax.ShapeDtypeStruct((n_tiles, 2 * seqs, CONV_DIM), F32),
        jax.ShapeDtypeStruct((n_tiles, seqs, SHIFT_DIM), F32)]
    out_specs = [row(d), row(d)] + [row(RWKV_DIM)] * 8 + [
        pl.BlockSpec((1, 2 * seqs, CONV_DIM), lambda i: (i, 0, 0)),
        pl.BlockSpec((1, seqs, SHIFT_DIM), lambda i: (i, 0, 0))]
    kern = functools.partial(_mixer_pre_kernel, n_prompt_tiles=n_prompt_tiles, seq_len=seq_len)
    return pl.pallas_call(
        kern, out_shape=out_shape, grid=(n_tiles,), in_specs=in_specs, out_specs=out_specs,
        scratch_shapes=[pltpu.VMEM((HIST, CONV_DIM), F32), pltpu.VMEM((HIST, SHIFT_DIM), F32)],
        compiler_params=pltpu.CompilerParams(dimension_semantics=("arbitrary",),
                                             vmem_limit_bytes=VMEM_LIMIT),
        name="mixer_pre")(x_p, x_s, st_conv_t, st_shift_t, *consts)


GROUP_LANES = 256
HEADS_PER_GROUP = GROUP_LANES // RWKV_HEAD
N_HEAD_GROUPS = RWKV_DIM // GROUP_LANES
NN = (((1,), (0,)), ((), ()))
NT = (((1,), (1,)), ((), ()))


def _split2(x):
    hi = x.astype(BF16)
    lo = (x - hi.astype(F32)).astype(BF16)
    return hi, lo


def _mm(xs, ys, dims=NN):
    x1, x2 = xs
    y1, y2 = ys
    d = lambda p, q: lax.dot_general(p, q, dims, preferred_element_type=F32)
    m = x1.shape[0]
    both = d(jnp.concatenate([x1, x2], axis=0), y1)
    return both[0:m] + both[m:2 * m] + d(x1, y2)


def _cat2(ps, qs, axis):
    return tuple(jnp.concatenate([p, q], axis=axis) for p, q in zip(ps, qs))


def _wkv_masks(c):
    hc = HEADS_PER_GROUP * c
    levels = c.bit_length() - 1
    t = np.arange(c)[:, None]
    s = (np.arange(hc) % c)[None, :]
    tm = [s < t, s <= t]
    for lvl in range(1, levels + 1):
        half = 1 << (lvl - 1)
        tm.append(((t >> lvl) == (s >> lvl)) & ((t & half) != 0) & ((s & half) == 0))
    row_head = (np.arange(hc) // c)[:, None]
    mfeat = row_head == (np.arange(GROUP_LANES) // RWKV_HEAD)[None, :]
    mpos = row_head == (np.arange(hc) // c)[None, :]
    lane_head = np.arange(GROUP_LANES) // RWKV_HEAD
    stmask = lane_head[:, None] == lane_head[None, :]
    tri = np.arange(c)[None, :] <= np.arange(c)[:, None]
    return (jnp.asarray(mfeat, BF16), jnp.asarray(mpos, BF16), jnp.asarray(np.stack(tm), F32),
            jnp.asarray(stmask, F32), jnp.asarray(tri, BF16))


def _wkv_pipe_kernel(r_ref, lw_ref, k_ref, v_ref, a_ref, b_ref, s0_ref, mfeat_ref, mpos_ref,
                     tmask_ref, stmask_ref, tri_ref, y_ref, sout_ref,
                     state, sv_ar, sv_inv, sv_akv, sv_arbk, sv_vbd, sv_v, sv_bk, sv_pend,
                     *, chunk, n_steps, chained):
    i = pl.program_id(0)
    c = chunk
    n_chunks = r_ref.shape[0] // c
    hc = HEADS_PER_GROUP * c
    levels = c.bit_length() - 1
    groups = range(N_HEAD_GROUPS)
    units = [(j, g) for j in range(n_chunks) for g in groups]
    uid = {u: n for n, u in enumerate(units)}

    stmask = stmask_ref[...]

    def expand(sc):
        pair = jnp.concatenate([sc, sc], axis=1)
        return jnp.concatenate([pair] * (GROUP_LANES // pair.shape[1]), axis=1) * stmask

    def compact(s):
        half = s[:, 0:GROUP_LANES // 2] + s[:, GROUP_LANES // 2:]
        return half[:, 0:RWKV_HEAD] + half[:, RWKV_HEAD:]

    @pl.when(i == 0)
    def _():
        for ref in (sv_ar, sv_inv, sv_akv, sv_arbk, sv_vbd, sv_v, sv_bk):
            ref[...] = jnp.zeros(ref.shape, ref.dtype)
        sv_pend[...] = jnp.ones(sv_pend.shape, F32)
        if chained:
            for g in groups:
                state[g] = expand(s0_ref[0, g])

    mfeat = mfeat_ref[...]
    mpos = mpos_ref[...]
    strict = tmask_ref[0]
    incl = tmask_ref[1]
    eye = incl - strict
    tri = tri_ref[...]

    def ld(ref, j, g):
        return ref[j * c:(j + 1) * c, g * GROUP_LANES:(g + 1) * GROUP_LANES]

    def bd_split(ps):
        mask = mpos if ps[0].shape[1] == hc else mfeat
        return tuple(jnp.concatenate([p] * HEADS_PER_GROUP, axis=0) * mask for p in ps)

    def bd2(m):
        return bd_split(_split2(m))

    def cumsum_rows(x):
        p1 = x.astype(BF16)
        r1 = x - p1.astype(F32)
        p2 = r1.astype(BF16)
        p3 = (r1 - p2.astype(F32)).astype(BF16)
        d = lambda q: jnp.dot(tri, q, preferred_element_type=F32)
        return d(p1) + d(p2) + d(p3)

    new = {}

    def prepare():
        cum = {u: cumsum_rows(ld(lw_ref, *u)) for u in units}
        yield
        ar, bk_end, p_end, a_ab, a_ak, a_rb, a_rk, v, vbd = ({} for _ in range(9))
        for u in units:
            cm = cum[u]
            cum_last = cm[c - 1:c, :]
            e_neg = jnp.exp(-cm)
            e_end = jnp.exp(cum_last - cm)
            b_raw = ld(b_ref, *u)
            k_raw = ld(k_ref, *u)
            ar[u] = _split2(jnp.concatenate([ld(a_ref, *u) * jnp.exp(cm - ld(lw_ref, *u)),
                                             ld(r_ref, *u) * jnp.exp(cm)], axis=0))
            bk_end[u] = _split2(jnp.concatenate([b_raw * e_end, k_raw * e_end], axis=0))
            p_end[u] = jnp.exp(cum_last)
            v[u] = ld(v_ref, *u)
            vbd[u] = bd2(v[u])
            gram = _mm(ar[u], _cat2(bd2(b_raw * e_neg), bd2(k_raw * e_neg), 0), NT)
            a_ab[u] = jnp.where(strict > 0, gram[0:c, 0:hc], 0.0)
            a_ak[u] = jnp.where(strict > 0, gram[0:c, hc:2 * hc], 0.0)
            a_rb[u] = jnp.where(incl > 0, gram[c:2 * c, 0:hc], 0.0)
            a_rk[u] = jnp.where(incl > 0, gram[c:2 * c, hc:2 * hc], 0.0)
            if uid[u] % 2 == 1:
                yield
        a_ab2 = {u: _split2(a_ab[u]) for u in units}
        inv = {u: eye + a_ab[u] * tmask_ref[2] for u in units}
        inv2 = {u: _split2(inv[u]) for u in units}
        for lvl in range(2, levels + 1):
            lm = tmask_ref[1 + lvl]
            t1 = {u: _mm(a_ab2[u], bd_split(inv2[u])) for u in units}
            yield
            inv = {u: inv[u] + lm * _mm(inv2[u], bd2(t1[u])) for u in units}
            inv2 = {u: _split2(inv[u]) for u in units}
            yield
        akv = {u: _mm(_split2(a_ak[u]), vbd[u]) for u in units}
        arbk = {u: _split2(jnp.concatenate([a_rb[u], a_rk[u]], axis=1)) for u in units}
        new.update(ar=ar, inv=inv2, akv=akv, arbk=arbk, vbd=vbd, v=v, bk=bk_end, pend=p_end)

    def serial():
        pair = lambda ref, n: (ref[0, n], ref[1, n])
        s_cur = [state[g] for g in groups] if chained else None
        for j in range(n_chunks):
            ns = [uid[j, g] for g in groups]
            s_prev = s_cur if chained else [expand(s0_ref[j, g]) for g in groups]
            x0 = [_mm(pair(sv_ar, n), _split2(s_prev[g]), NT) for g, n in zip(groups, ns)]
            yield
            uu = [_mm(pair(sv_inv, n), bd2(x0[g][0:c] + sv_akv[n])) for g, n in zip(groups, ns)]
            yield
            for g, n in zip(groups, ns):
                yy = x0[g][c:2 * c] + _mm(pair(sv_arbk, n), _cat2(bd2(uu[g]), pair(sv_vbd, n), 0))
                y_ref[j * c:(j + 1) * c, g * GROUP_LANES:(g + 1) * GROUP_LANES] = yy
            s_new = []
            for g, n in zip(groups, ns):
                uv_t = jnp.transpose(jnp.concatenate([uu[g], sv_v[n]], axis=0))
                upd = _mm(_split2(uv_t), pair(sv_bk, n))
                s_new.append(s_prev[g] * sv_pend[n, 0:1, :] + stmask * upd)
            yield
            if chained:
                s_cur = s_new
            else:
                for g in groups:
                    sout_ref[j, g] = compact(s_new[g])
        if chained:
            for g in groups:
                state[g] = s_cur[g]
        new['last_state'] = s_new

    _interleave(serial(), prepare())

    zero = sum(_zero_after(s[0:8, 0:LANES]) for s in new['last_state'])[0:1, 0:1]
    zero_bf = zero.astype(BF16)
    for u, n in uid.items():
        for name, ref in (('ar', sv_ar), ('inv', sv_inv), ('arbk', sv_arbk), ('vbd', sv_vbd), ('bk', sv_bk)):
            for half in range(2):
                ref[half, n] = new[name][u][half] + zero_bf
        sv_akv[n] = new['akv'][u] + zero
        sv_v[n] = new['v'][u] + zero
        sv_pend[n] = jnp.broadcast_to(new['pend'][u] + zero, sv_pend.shape[1:])

    if chained:
        @pl.when(i == n_steps)
        def _():
            for g in groups:
                sout_ref[0, g] = compact(state[g])


def _wkv_scan(rlkvab, s0, *, row0, n_rows, chunk, chunks_per_step, chained):
    rows = chunk * chunks_per_step
    n_steps = n_rows // rows
    assert n_rows % rows == 0 and row0 % rows == 0
    masks = _wkv_masks(chunk)
    hc = HEADS_PER_GROUP * chunk
    n_units = chunks_per_step * N_HEAD_GROUPS
    st = (N_HEAD_GROUPS, GROUP_LANES, RWKV_HEAD)
    prev = lambda i: jnp.maximum(i - 1, 0)
    row_in = pl.BlockSpec((rows, RWKV_DIM), lambda i: (row0 // rows + jnp.minimum(i, n_steps - 1), 0))
    if chained:
        n_state = 1
        st_spec = pl.BlockSpec((1,) + st, lambda i: (0, 0, 0, 0))
    else:
        n_state = n_rows // chunk
        st_spec = pl.BlockSpec((chunks_per_step,) + st, lambda i: (prev(i), 0, 0, 0))
    in_specs = [row_in] * 6 + [st_spec] + [_const_spec(m.shape) for m in masks]
    out_shape = [jax.ShapeDtypeStruct((n_rows, RWKV_DIM), F32),
                 jax.ShapeDtypeStruct((n_state,) + st, F32)]
    out_specs = [pl.BlockSpec((rows, RWKV_DIM), lambda i: (prev(i), 0)), st_spec]
    scratch = [pltpu.VMEM((N_HEAD_GROUPS, GROUP_LANES, GROUP_LANES), F32),
               pltpu.VMEM((2, n_units, 2 * chunk, GROUP_LANES), BF16),
               pltpu.VMEM((2, n_units, chunk, hc), BF16),
               pltpu.VMEM((n_units, chunk, GROUP_LANES), F32),
               pltpu.VMEM((2, n_units, chunk, 2 * hc), BF16),
               pltpu.VMEM((2, n_units, hc, GROUP_LANES), BF16),
               pltpu.VMEM((n_units, chunk, GROUP_LANES), F32),
               pltpu.VMEM((2, n_units, 2 * chunk, GROUP_LANES), BF16),
               pltpu.VMEM((n_units, 8, GROUP_LANES), F32)]
    kern = functools.partial(_wkv_pipe_kernel, chunk=chunk, n_steps=n_steps, chained=chained)
    return pl.pallas_call(
        kern, out_shape=out_shape, grid=(n_steps + 1,), in_specs=in_specs, out_specs=out_specs,
        scratch_shapes=scratch,
        compiler_params=pltpu.CompilerParams(dimension_semantics=("arbitrary",),
                                             vmem_limit_bytes=VMEM_LIMIT),
        name="wkv_scan")(*rlkvab, s0, *masks)


def _mixer_post_kernel(yp_ref, ys_ref, bonus_ref, g_ref, yag_ref, sgb_ref, xp_ref, xs_ref, lng_ref, lnb_ref,
                       wob_ref, wo_ref, n2_ref, wr_ref, br_ref, ones_ref, tril_ref,
                       x1_ref, h2_ref, route_ref, counts_ref, cnt, *, n_prompt_tiles):
    i = pl.program_id(0)

    @pl.when(i == 0)
    def _():
        cnt[...] = jnp.zeros(cnt.shape, F32)

    tm = x1_ref.shape[0]
    n_parts = tm // ROW_TILE
    rp = tm // n_parts
    is_prompt = i < n_prompt_tiles
    ones = ones_ref[...]
    neg = jnp.float32(-jnp.inf)
    big = jnp.int32(LANES)
    lane = lax.broadcasted_iota(jnp.int32, (rp, LANES), 1)
    picked = {}

    def part(q):
        rs = slice(q * rp, (q + 1) * rp)
        y = jnp.where(is_prompt, yp_ref[rs, :], ys_ref[rs, :])
        inv_n = 1.0 / RWKV_HEAD
        mean = _split2_dot(y, ones) * inv_n
        yc = y - mean
        var = _split2_dot(yc * yc, ones) * inv_n
        yn = yc * lax.rsqrt(var + GN_EPS) * lng_ref[...] + lnb_ref[...]
        yy = (yn + bonus_ref[rs, :].astype(F32)) * g_ref[rs, :].astype(F32)
        yield
        y_b = jnp.dot(yy.astype(BF16), wob_ref[...], preferred_element_type=F32)
        merged = yag_ref[rs, :].astype(F32) + sgb_ref[rs, :].astype(F32) * y_b
        yield
        x = jnp.where(is_prompt, xp_ref[rs, :], xs_ref[rs, :])
        x1 = x + jnp.dot(merged.astype(BF16), wo_ref[...], preferred_element_type=F32)
        x1_ref[rs, :] = x1
        h2 = _rms(x1, n2_ref[...])
        h2_ref[rs, :] = _pack_bf16_pairs(h2)
        yield
        logits = _dot_stack3(h2, wr_ref[...]) + br_ref[...]
        yield
        is_g = lane < N_GROUPS
        lgp = jnp.where(is_g, logits, neg)
        m_g = jnp.max(lgp, axis=-1, keepdims=True)
        grp = jnp.min(jnp.where(lgp == m_g, lane, big), axis=-1, keepdims=True)
        p_top = 1.0 / jnp.sum(jnp.where(is_g, jnp.exp(logits - m_g), 0.0), axis=-1, keepdims=True)
        e_lane = lane - N_GROUPS
        in_grp = (e_lane >= grp * EXPERTS_PER_GROUP) & (e_lane < (grp + 1) * EXPERTS_PER_GROUP)
        le = jnp.where(in_grp, logits, neg)
        m1 = jnp.max(le, axis=-1, keepdims=True)
        i1 = jnp.min(jnp.where(le == m1, lane, big), axis=-1, keepdims=True)
        le2 = jnp.where(lane == i1, neg, le)
        m2 = jnp.max(le2, axis=-1, keepdims=True)
        i2 = jnp.min(jnp.where(le2 == m2, lane, big), axis=-1, keepdims=True)
        ex = jnp.exp(m2 - m1)
        picked[q] = (i1 - N_GROUPS, i2 - N_GROUPS, p_top / (1.0 + ex), p_top * ex / (1.0 + ex))

    gens = [part(q) for q in range(n_parts)]
    next(gens[0])
    _interleave(*gens)

    hot = [[lane == e for e in picked[q][0:2]] for q in range(n_parts)]
    both = jnp.concatenate([jnp.where(h1, 1.0, jnp.where(h2_, 1.0, 0.0)) for h1, h2_ in hot], axis=0)
    before = jnp.dot(tril_ref[...], both.astype(BF16), preferred_element_type=F32) + cnt[0:1, :]
    cnt[0:1, :] = cnt[0:1, :] + jnp.sum(both, axis=0, keepdims=True)
    counts_ref[...] = jnp.broadcast_to(cnt[0:1, :], counts_ref.shape)
    for q in range(n_parts):
        rs = slice(q * rp, (q + 1) * rp)
        e1, e2, w1, w2 = picked[q]
        ranks = [jnp.sum(jnp.where(h, before[rs, :], 0.0), axis=-1, keepdims=True) for h in hot[q]]
        route = jnp.zeros((rp, LANES), F32)
        for c, col in enumerate([e1.astype(F32), e2.astype(F32), w1, w2] + ranks):
            route = jnp.where(lane == c, col, route)
        route_ref[rs, :] = route


ROUTE_EXPERT, ROUTE_WEIGHT, ROUTE_RANK = 0, 2, 4


def _mixer_post(y_p, y_s, bonus, g, yag, sgb, x_p, x_s, p, *, tm):
    n_p, d = x_p.shape
    n_tok = n_p + x_s.shape[0]
    npt = n_p // tm
    row = lambda w: pl.BlockSpec((tm, w), lambda i: (i, 0))
    tril = jnp.asarray(np.arange(tm)[None, :] < np.arange(tm)[:, None], BF16)
    consts = [p['lnx_g'], p['lnx_b'], p['w_out_b'], p['w_o'], p['norm2_g'], p['w_router'],
              p['b_router'], p['ones_bf16'], tril]
    pair = lambda w: [pl.BlockSpec((tm, w), lambda i: (jnp.minimum(i, npt - 1), 0)),
                      pl.BlockSpec((tm, w), lambda i: (jnp.maximum(i - npt, 0), 0))]
    in_specs = (pair(RWKV_DIM) + [row(RWKV_DIM)] * 2 + [row(d)] * 2 + pair(d)
                + [_const_spec(c.shape) for c in consts])
    out_shape = [jax.ShapeDtypeStruct((n_tok, d), F32), jax.ShapeDtypeStruct((n_tok, d // 2), jnp.uint32),
                 jax.ShapeDtypeStruct((n_tok, LANES), F32), jax.ShapeDtypeStruct((8, LANES), F32)]
    out_specs = [row(d), row(d // 2), row(LANES), _const_spec((8, LANES))]
    return pl.pallas_call(
        functools.partial(_mixer_post_kernel, n_prompt_tiles=npt), out_shape=out_shape,
        grid=(n_tok // tm,), in_specs=in_specs, out_specs=out_specs,
        scratch_shapes=[pltpu.VMEM((8, LANES), F32)],
        compiler_params=pltpu.CompilerParams(dimension_semantics=("arbitrary",),
                                             vmem_limit_bytes=VMEM_LIMIT),
        name="mixer_post")(y_p, y_s, bonus, g, yag, sgb, x_p, x_s, *consts)


N_DMA_PRIORITIES = 2


def _gather_rows(idx_ref, src_hbm, dst, sem, n_rows, *, unrolled):
    def start(r, priority):
        pltpu.make_async_copy(src_hbm.at[pl.ds(idx_ref[r], 1)], dst.at[pl.ds(r, 1)],
                              sem).start(priority=priority)
    if unrolled:
        for r in range(n_rows):
            start(r, r % N_DMA_PRIORITIES)
    else:
        def body(r, carry):
            start(r, 0)
            return carry
        lax.fori_loop(0, n_rows, body, 0)


def _wait_rows(src_hbm, dst, sem, n_rows):
    pltpu.make_async_copy(src_hbm.at[pl.ds(0, n_rows)], dst, sem).wait()


def _moe_scatter_kernel(dest_ref, h_ref, xs_in_hbm, xs_hbm, sem):
    del xs_in_hbm
    n = dest_ref.shape[-1]
    tm = h_ref.shape[0]
    for r in range(n):
        pltpu.make_async_copy(h_ref.at[pl.ds(r % tm, 1)], xs_hbm.at[pl.ds(dest_ref[0, 0, r], 1)],
                              sem).start(priority=r % N_DMA_PRIORITIES)
    for j in range(n // tm):
        pltpu.make_async_copy(h_ref, xs_hbm.at[pl.ds(0, tm)], sem).wait()


def _moe_scatter(h2, pos3, n_rows):
    n_tok, d = h2.shape
    n_tiles, _, n = pos3.shape
    tm = n // TOP_K
    xs0 = jnp.zeros((n_rows, d), h2.dtype)
    return pl.pallas_call(
        _moe_scatter_kernel, out_shape=jax.ShapeDtypeStruct((n_rows, d), h2.dtype), grid=(n_tiles,),
        in_specs=[pl.BlockSpec((1, 1, n), lambda i: (i, 0, 0), memory_space=pltpu.SMEM),
                  pl.BlockSpec((tm, d), lambda i: (i, 0)), pl.BlockSpec(memory_space=pl.ANY)],
        out_specs=pl.BlockSpec(memory_space=pl.ANY),
        scratch_shapes=[pltpu.SemaphoreType.DMA(())], input_output_aliases={2: 0},
        compiler_params=pltpu.CompilerParams(dimension_semantics=("arbitrary",), has_side_effects=True,
                                             vmem_limit_bytes=VMEM_LIMIT),
        name="moe_scatter")(pos3, h2, xs0)


def _moe_experts_kernel(bexp_ref, nused_ref, x_ref, wg_ref, wu_ref, wd_ref, yb_ref, wg_bf, wu_bf, wd_bf):
    i = pl.program_id(0)

    @pl.when((i == 0) | (bexp_ref[i] != bexp_ref[jnp.maximum(i - 1, 0)]))
    def _():
        wg_bf[...] = wg_ref[0].astype(BF16)
        wu_bf[...] = wu_ref[0].astype(BF16)
        wd_bf[...] = wd_ref[0].astype(BF16)

    @pl.when(i < nused_ref[0])
    def _():
        def part(rs):
            xb = _unpack_bf16_pairs(x_ref[rs, :]).astype(BF16)
            hg = jnp.dot(xb, wg_bf[...], preferred_element_type=F32)
            yield
            hu = jnp.dot(xb, wu_bf[...], preferred_element_type=F32)
            hid = (hg * jax.nn.sigmoid(hg)) * hu
            yield
            yb_ref[rs, :] = _pack_bf16_pairs(jnp.dot(hid.astype(BF16), wd_bf[...], preferred_element_type=F32))

        rows = x_ref.shape[0]
        _interleave(*[part(slice(r0, r0 + ROW_TILE)) for r0 in range(0, rows, ROW_TILE)])

    @pl.when(i >= nused_ref[0])
    def _():
        yb_ref[...] = jnp.zeros(yb_ref.shape, yb_ref.dtype)


def _moe_experts(xs, block_expert, n_used, w_eg, w_eu, w_ed, *, blk):
    n_rows, dp = xs.shape
    n_blocks = n_rows // blk
    d, de = w_eg.shape[1:]
    grid_spec = pltpu.PrefetchScalarGridSpec(
        num_scalar_prefetch=2, grid=(n_blocks,),
        in_specs=[
            pl.BlockSpec((blk, dp), lambda i, be, nu: (jnp.minimum(i, nu[0] - 1), 0)),
            pl.BlockSpec((1, d, de), lambda i, be, nu: (be[i], 0, 0)),
            pl.BlockSpec((1, d, de), lambda i, be, nu: (be[i], 0, 0)),
            pl.BlockSpec((1, de, d), lambda i, be, nu: (be[i], 0, 0)),
        ],
        out_specs=pl.BlockSpec((blk, dp), lambda i, be, nu: (i, 0)),
        scratch_shapes=[pltpu.VMEM((d, de), BF16), pltpu.VMEM((d, de), BF16), pltpu.VMEM((de, d), BF16)])
    return pl.pallas_call(
        _moe_experts_kernel, out_shape=jax.ShapeDtypeStruct((n_rows, dp), xs.dtype), grid_spec=grid_spec,
        compiler_params=pltpu.CompilerParams(dimension_semantics=("arbitrary",),
                                             vmem_limit_bytes=VMEM_LIMIT),
        name="moe_experts")(block_expert, n_used, xs, w_eg, w_eu, w_ed)


GATHER_AHEAD = 2


def _with_dummy_blocks(idx, n_blocks, blk):
    pad = jnp.zeros((GATHER_AHEAD * blk,), idx.dtype)
    return jnp.concatenate([idx, pad]).reshape(n_blocks + GATHER_AHEAD, 1, blk)


def _moe_combine_kernel(*refs, n_prompt_tiles):
    pos_refs = refs[:GATHER_AHEAD + 1]
    x1_ref, route_ref, nf_ref, yb_hbm, outp_ref, outs_ref, ybuf, sems = refs[GATHER_AHEAD + 1:]
    i = pl.program_id(0)
    n_slots = GATHER_AHEAD + 1
    rows = ybuf.shape[1]
    slot = i % n_slots

    @pl.when(i == 0)
    def _():
        for a in range(GATHER_AHEAD):
            _gather_rows(pos_refs[a].at[0, 0], yb_hbm, ybuf.at[a], sems.at[a], rows, unrolled=False)

    _wait_rows(yb_hbm, ybuf.at[slot], sems.at[slot], rows)
    ahead = (i + GATHER_AHEAD) % n_slots
    _gather_rows(pos_refs[GATHER_AHEAD].at[0, 0], yb_hbm, ybuf.at[ahead], sems.at[ahead], rows, unrolled=True)
    tm = x1_ref.shape[0]
    route = route_ref[...]
    x2 = x1_ref[...]
    for j in range(TOP_K):
        yj = _unpack_bf16_pairs(ybuf[slot, j * tm:(j + 1) * tm, :])
        x2 = x2 + yj * route[:, ROUTE_WEIGHT + j:ROUTE_WEIGHT + j + 1]
    out = _rms(x2, nf_ref[...])

    @pl.when(i < n_prompt_tiles)
    def _():
        outp_ref[...] = out

    @pl.when(i >= n_prompt_tiles)
    def _():
        outs_ref[...] = out

    @pl.when(i == pl.num_programs(0) - 1)
    def _():
        for a in range(1, n_slots):
            s = (i + a) % n_slots
            _wait_rows(yb_hbm, ybuf.at[s], sems.at[s], rows)


def _moe_combine(pos_tiles, x1, route, normf_g, yb, *, n_prompt_rows, tm):
    n_tok, d = x1.shape
    n_tiles = n_tok // tm
    npt = n_prompt_rows // tm
    pos3 = _with_dummy_blocks(pos_tiles.reshape(-1), n_tiles, TOP_K * tm)
    pos_spec = lambda a: pl.BlockSpec((1, 1, TOP_K * tm), lambda i: (i + a, 0, 0), memory_space=pltpu.SMEM)
    in_specs = [pos_spec(a) for a in range(GATHER_AHEAD + 1)] + [
        pl.BlockSpec((tm, d), lambda i: (i, 0)),
        pl.BlockSpec((tm, LANES), lambda i: (i, 0)),
        _const_spec(normf_g.shape),
        pl.BlockSpec(memory_space=pl.ANY),
    ]
    out_shape = [jax.ShapeDtypeStruct((n_prompt_rows, d), F32),
                 jax.ShapeDtypeStruct((n_tok - n_prompt_rows, d), F32)]
    out_specs = [pl.BlockSpec((tm, d), lambda i: (jnp.minimum(i, npt - 1), 0)),
                 pl.BlockSpec((tm, d), lambda i: (jnp.maximum(i - npt, 0), 0))]
    return pl.pallas_call(
        functools.partial(_moe_combine_kernel, n_prompt_tiles=npt), out_shape=out_shape,
        grid=(n_tiles,), in_specs=in_specs, out_specs=out_specs,
        scratch_shapes=[pltpu.VMEM((GATHER_AHEAD + 1, TOP_K * tm, yb.shape[1]), yb.dtype),
                        pltpu.SemaphoreType.DMA((GATHER_AHEAD + 1,))],
        compiler_params=pltpu.CompilerParams(dimension_semantics=("arbitrary",),
                                             vmem_limit_bytes=VMEM_LIMIT),
        name="moe_combine")(*([pos3] * (GATHER_AHEAD + 1)), x1, route, normf_g, yb)


def _dispatch(route, counts, blk, tm):
    n_tok = route.shape[0]
    n_assign = n_tok * TOP_K
    expert = route[:, ROUTE_EXPERT:ROUTE_EXPERT + TOP_K].astype(jnp.int32)
    rank = route[:, ROUTE_RANK:ROUTE_RANK + TOP_K].astype(jnp.int32)
    counts = counts[0, :N_EXPERTS].astype(jnp.int32)
    padded = (counts + blk - 1) // blk * blk
    pad_end = jnp.cumsum(padded)
    pad_start = pad_end - padded
    is_e = expert[:, :, None] == jnp.arange(N_EXPERTS, dtype=jnp.int32)
    dest = jnp.sum(jnp.where(is_e, pad_start, 0), axis=-1) + rank
    n_blocks = -(-n_assign // blk) + N_EXPERTS
    block_start = jnp.arange(n_blocks, dtype=jnp.int32) * blk
    block_expert = jnp.minimum(jnp.sum(block_start[:, None] >= pad_end[None, :], axis=1),
                               N_EXPERTS - 1).astype(jnp.int32)
    n_used = (pad_end[-1] // blk).reshape(1)
    pos_tiles = dest.reshape(n_tok // tm, tm, TOP_K).transpose(0, 2, 1).reshape(n_tok // tm, 1, TOP_K * tm)
    return pos_tiles, block_expert, n_used, n_blocks


def _state_to_kernel(s):
    return s.reshape(s.shape[0], N_HEAD_GROUPS, GROUP_LANES, RWKV_HEAD)


def _state_from_kernel(s):
    return s.reshape(s.shape[0], RWKV_HEADS, RWKV_HEAD, RWKV_HEAD)


def kernel(x_prompt, x_sample, state_conv, state_shift, state_wkv, norm1_g, w_in, conv_w, mu_shift, w0, w_lora_w, a0, w_lora_a, w_lora_g, k_k, k_a, r_k, lnx_g, lnx_b, w_out_a, w_out_b, w_o, norm2_g, w_router_group, b_router_group, w_router_expert, b_router_expert, w_e_gate, w_e_up, w_e_down, normf_g):
    depth = norm1_g.shape[0]
    bp, seq, d = x_prompt.shape
    db, dseq, _ = x_sample.shape
    assert depth == 1 and bp == 1, "single layer, single prompt stream"
    tm = ROW_TILE
    n_p, n_s = bp * seq, db * dseq
    assert n_p % tm == 0 and n_s % tm == 0 and tm % dseq == 0
    n_prompt_tiles, n_sample_tiles = n_p // tm, n_s // tm
    seqs = tm // dseq
    n_tok = n_p + n_s

    x_p = x_prompt.reshape(n_p, d)
    x_s = x_sample.reshape(n_s, d)

    l = 0
    c3 = 3 * CONV_DIM
    head_id = np.arange(GROUP_LANES) // RWKV_HEAD
    ones_bf16 = jnp.asarray(head_id[:, None] == head_id[None, :], BF16)
    zpad = jnp.zeros((LORA_W, RWKV_DIM), F32)
    n_r = N_GROUPS + N_EXPERTS
    p = {
        'norm1_g': norm1_g[l].reshape(1, d),
        'w_in_a': w_in[l][:, :c3].astype(BF16),
        'w_in_b': w_in[l][:, c3:c3 + SHIFT_DIM].astype(BF16),
        'w_in_g': w_in[l][:, c3 + SHIFT_DIM:].astype(BF16),
        'conv_w': conv_w[l],
        'mu_shift': mu_shift[l].reshape(1, SHIFT_DIM),
        'w0': w0[l].reshape(1, RWKV_DIM),
        'w_lora_w': _stack3(jnp.concatenate([w_lora_w[l], zpad], axis=0)),
        'a0': a0[l].reshape(1, RWKV_DIM),
        'w_lora_a': _stack3(jnp.concatenate([zpad, w_lora_a[l]], axis=0)),
        'w_lora_g': _stack3(w_lora_g[l]),
        'k_k': k_k[l].reshape(1, RWKV_DIM),
        'k_a': k_a[l].reshape(1, RWKV_DIM),
        'r_k': r_k[l].reshape(1, RWKV_DIM),
        'w_out_a': w_out_a[l].astype(BF16),
        'ones_bf16': ones_bf16,
        'lnx_g': lnx_g[l].reshape(1, RWKV_DIM),
        'lnx_b': lnx_b[l].reshape(1, RWKV_DIM),
        'w_out_b': w_out_b[l].astype(BF16),
        'w_o': w_o[l].astype(BF16),
        'norm2_g': norm2_g[l].reshape(1, d),
        'w_router': _stack3(jnp.pad(jnp.concatenate([w_router_group[l], w_router_expert[l]], axis=1),
                                    ((0, 0), (0, LANES - n_r)))),
        'b_router': jnp.pad(jnp.concatenate([b_router_group[l], b_router_expert[l]]),
                            (0, LANES - n_r)).reshape(1, LANES),
    }
    st_conv_t = state_conv[l].reshape(n_sample_tiles, seqs * (CONV_WIDTH - 1), CONV_DIM)
    st_shift_t = state_shift[l].reshape(n_sample_tiles, seqs, SHIFT_DIM)

    (yag, sgb, r, w, k, v, a, b, bonus, g, ctail, stail) = _mixer_pre(
        x_p, x_s, st_conv_t, st_shift_t, p, seq_len=dseq, tm=tm)

    s0_prompt = jnp.zeros((1, N_HEAD_GROUPS, GROUP_LANES, RWKV_HEAD), F32)
    s0_sample = _state_to_kernel(state_wkv[l])
    rlkvab = (r, w, k, v, a, b)
    cp = min(SCAN_CHUNK, n_p)
    assert cp & (cp - 1) == 0 and dseq & (dseq - 1) == 0
    y_p, s_p = _wkv_scan(rlkvab, s0_prompt, row0=0, n_rows=n_p, chunk=cp,
                         chunks_per_step=SCAN_CHUNKS_PER_STEP, chained=True)
    y_s, s_s = _wkv_scan(rlkvab, s0_sample, row0=n_p, n_rows=n_s, chunk=dseq,
                         chunks_per_step=SCAN_CHUNKS_PER_STEP, chained=False)

    post_tm = POST_TILES * tm
    assert n_p % post_tm == 0 and n_s % post_tm == 0
    x1, h2, route, counts = _mixer_post(y_p, y_s, bonus, g, yag, sgb, x_p, x_s, p, tm=post_tm)

    blk = MOE_ROWS
    pos_tiles, block_expert, n_used, n_blocks = _dispatch(route, counts, blk, post_tm)
    xs = _moe_scatter(h2, pos_tiles, n_blocks * blk)
    yb = _moe_experts(xs, block_expert, n_used, w_e_gate[l], w_e_up[l], w_e_down[l], blk=blk)
    out_p, out_s = _moe_combine(pos_tiles, x1, route, normf_g.reshape(1, d), yb, n_prompt_rows=n_p,
                                tm=post_tm)

    y_prompt = out_p.reshape(bp, seq, d)
    y_sample = out_s.reshape(db, dseq, d)
    conv_p = ctail[n_prompt_tiles - 1, 2 * (seqs - 1):2 * seqs].reshape(1, bp, CONV_WIDTH - 1, CONV_DIM)
    shift_p = stail[n_prompt_tiles - 1, seqs - 1].reshape(1, bp, 1, SHIFT_DIM)
    wkv_p = _state_from_kernel(s_p).reshape(1, bp, RWKV_HEADS, RWKV_HEAD, RWKV_HEAD)
    conv_s = ctail[n_prompt_tiles:].reshape(1, db, CONV_WIDTH - 1, CONV_DIM)
    shift_s = stail[n_prompt_tiles:].reshape(1, db, 1, SHIFT_DIM)
    wkv_s = _state_from_kernel(s_s).reshape(1, db, RWKV_HEADS, RWKV_HEAD, RWKV_HEAD)
    return (y_prompt, y_sample, conv_p, shift_p, wkv_p, conv_s, shift_s, wkv_s)
```

```python
import functools

import numpy as np
import jax
import jax.numpy as jnp
from jax import lax
from jax.experimental import pallas as pl
from jax.experimental.pallas import tpu as pltpu

F32 = jnp.float32
BF16 = jnp.bfloat16

CONV_DIM = 512
CONV_WIDTH = 3
RWKV_HEAD = 64
RWKV_HEADS = 8
RWKV_DIM = RWKV_HEADS * RWKV_HEAD
LORA_W = 64
LORA_A = 64
LORA_G = 128
SHIFT_DIM = 3 * RWKV_DIM + LORA_W + LORA_A + LORA_G
N_GROUPS = 4
EXPERTS_PER_GROUP = 8
N_EXPERTS = N_GROUPS * EXPERTS_PER_GROUP
TOP_K = 2
RMS_EPS = 1e-6
GN_EPS = 64e-5

LANES = 128
ROW_TILE = 256
POST_TILES = 2
SCAN_CHUNK = 64
SCAN_CHUNKS_PER_STEP = 4
MOE_ROWS = 512
HIST = 8
VMEM_LIMIT = 56 * 1024 * 1024


def _rms(x, g):
    return x * lax.rsqrt(jnp.mean(x * x, axis=-1, keepdims=True) + RMS_EPS) * g


def _split2_dot(x, ones_bf16):
    w = ones_bf16.shape[0]
    hi = x.astype(BF16)
    lo = (x - hi.astype(F32)).astype(BF16)
    parts = []
    for c0 in range(0, x.shape[1], w):
        both = jnp.concatenate([hi[:, c0:c0 + w], lo[:, c0:c0 + w]], axis=0)
        s = jnp.dot(both, ones_bf16, preferred_element_type=F32)
        parts.append(s[0:x.shape[0]] + s[x.shape[0]:])
    return jnp.concatenate(parts, axis=1)


def _pack_bf16_pairs(x):
    n = x.shape[1] // 2
    lo = pltpu.bitcast(x[:, :n].astype(BF16).astype(F32), jnp.uint32)
    hi = pltpu.bitcast(x[:, n:].astype(BF16).astype(F32), jnp.uint32)
    return hi | (lo >> 16)


def _unpack_bf16_pairs(p):
    lo = pltpu.bitcast(p << 16, F32)
    hi = pltpu.bitcast(p & jnp.uint32(0xFFFF0000), F32)
    return jnp.concatenate([lo, hi], axis=1)


def _stack3(w):
    hi = w.astype(BF16)
    lo = (w - hi.astype(F32)).astype(BF16)
    return jnp.concatenate([hi, hi, lo], axis=0)


def _dot_stack3(x, w_stack):
    hi = x.astype(BF16)
    lo = (x - hi.astype(F32)).astype(BF16)
    return jnp.dot(jnp.concatenate([hi, lo, hi], axis=1), w_stack, preferred_element_type=F32)


def _const_spec(shape):
    nd = len(shape)
    return pl.BlockSpec(shape, lambda *_: (0,) * nd)


def _interleave(*gens):
    live = list(gens)
    while live:
        for gen in list(live):
            try:
                next(gen)
            except StopIteration:
                live.remove(gen)


def _zero_after(x):
    return pltpu.bitcast(lax.shift_right_logical(pltpu.bitcast(x, jnp.uint32), jnp.uint32(32)), F32)


def _mixer_pre_tile(x_ref, stc_ref, sts_ref, n1_ref, wa_ref, wb_ref, wg_ref, convw_ref, mu_ref,
                    w0_ref, lw_ref, a0_ref, la_ref, lgw_ref, kk_ref, ka_ref, rk_ref, woa_ref, ones_ref,
                    yag_ref, sgb_ref, r_ref, w_ref, k_ref, v_ref, a_ref, b_ref, bonus_ref, g_ref,
                    ctail_ref, stail_ref, ccarry, scarry, *, sample, seq_len):
    tm = x_ref.shape[0]
    seqs = tm // seq_len
    h = _rms(x_ref[...], n1_ref[...]).astype(BF16)

    def prev_rows(val, k, heads):
        row = lax.broadcasted_iota(jnp.int32, val.shape, 0)
        out = pltpu.roll(val, k, axis=0)
        for r0, head in heads.items():
            out = jnp.where(row == r0, head, out)
        return out

    pb = jnp.dot(h, wb_ref[...], preferred_element_type=F32)
    old_s = scarry[...]
    if sample:
        s_heads = {j * seq_len: sts_ref[0, j:j + 1, :] for j in range(seqs)}
    else:
        s_heads = {0: old_s[HIST - 1:HIST, :]}
    prev = prev_rows(pb, 1, s_heads)
    for j in range(seqs):
        r1 = (j + 1) * seq_len
        stail_ref[0, j:j + 1, :] = pb[r1 - 1:r1, :]
    scarry[...] = pb[tm - HIST:tm, :] + _zero_after(old_s)

    def conv_and_gates():
        n_a = wa_ref.shape[1] // 3
        g_in = jnp.dot(h, wa_ref[:, 0:n_a], preferred_element_type=F32)
        yield
        g_out = jnp.dot(h, wa_ref[:, n_a:2 * n_a], preferred_element_type=F32)
        yield
        x_c = jnp.dot(h, wa_ref[:, 2 * n_a:3 * n_a], preferred_element_type=F32)
        bx = g_in * x_c
        old_c = ccarry[...]
        if sample:
            h1 = {j * seq_len: stc_ref[0, 2 * j + 1:2 * j + 2, :] for j in range(seqs)}
            h2 = {j * seq_len: stc_ref[0, 2 * j:2 * j + 1, :] for j in range(seqs)}
            h2.update({j * seq_len + 1: stc_ref[0, 2 * j + 1:2 * j + 2, :] for j in range(seqs)})
        else:
            h1 = {0: old_c[HIST - 1:HIST, :]}
            h2 = {0: old_c[HIST - 2:HIST - 1, :], 1: old_c[HIST - 1:HIST, :]}
        cw = convw_ref[...]
        conv = cw[0:1, :] * prev_rows(bx, 2, h2) + cw[1:2, :] * prev_rows(bx, 1, h1) + cw[2:3, :] * bx
        for j in range(seqs):
            r1 = (j + 1) * seq_len
            ctail_ref[0, 2 * j:2 * j + 2, :] = bx[r1 - 2:r1, :]
        ccarry[...] = bx[tm - HIST:tm, :] + _zero_after(old_c)
        yield
        y_a = jnp.dot((g_out * conv).astype(BF16), woa_ref[...], preferred_element_type=F32)
        yield
        d = wg_ref.shape[1] // 2
        half = d // 2
        for c0 in range(0, d, half):
            pg = jnp.dot(h, wg_ref[:, c0:c0 + half], preferred_element_type=F32)
            yag_ref[:, c0:c0 + half] = (jax.nn.sigmoid(pg) * y_a[:, c0:c0 + half]).astype(yag_ref.dtype)
            yield
        for c0 in range(0, d, half):
            pg = jnp.dot(h, wg_ref[:, d + c0:d + c0 + half], preferred_element_type=F32)
            sgb_ref[:, c0:c0 + half] = jax.nn.sigmoid(pg).astype(sgb_ref.dtype)
            yield

    def rwkv_pre():
        s = pb + (prev - pb) * mu_ref[...]
        o1, o2, o3 = RWKV_DIM, 2 * RWKV_DIM, 3 * RWKV_DIM
        r = s[:, 0:o1]
        k = s[:, o1:o2]
        v = s[:, o2:o3]
        s_l = s[:, o3:o3 + LORA_W + LORA_A]
        lg = s[:, o3 + LORA_W + LORA_A:]
        r_ref[...] = r
        v_ref[...] = v
        yield
        z = w0_ref[...] + _dot_stack3(jnp.tanh(s_l), lw_ref[...])
        w_log = -jax.nn.softplus(-z) - 0.5
        w_ref[...] = -jnp.exp(w_log)
        yield
        a = jax.nn.sigmoid(a0_ref[...] + _dot_stack3(s_l, la_ref[...]))
        yield
        g_ref[...] = _dot_stack3(jax.nn.sigmoid(lg), lgw_ref[...]).astype(g_ref.dtype)
        yield
        ones = ones_ref[...]
        kk = k * kk_ref[...]
        kk_n = kk / jnp.maximum(jnp.sqrt(_split2_dot(kk * kk, ones)), 1e-12)
        a_ref[...] = -kk_n
        b_ref[...] = kk_n * a
        yield
        k2 = k * (1.0 + (a - 1.0) * ka_ref[...])
        k_ref[...] = k2
        yield
        bonus_ref[...] = (_split2_dot(r * k2 * rk_ref[...], ones) * v).astype(bonus_ref.dtype)

    _interleave(conv_and_gates(), rwkv_pre())


def _mixer_pre_kernel(xp_ref, xs_ref, *refs, n_prompt_tiles, seq_len):
    i = pl.program_id(0)
    ccarry, scarry = refs[-2:]

    @pl.when(i == 0)
    def _():
        ccarry[...] = jnp.zeros(ccarry.shape, F32)
        scarry[...] = jnp.zeros(scarry.shape, F32)

    pl.when(i < n_prompt_tiles)(
        functools.partial(_mixer_pre_tile, xp_ref, *refs, sample=False, seq_len=seq_len))
    pl.when(i >= n_prompt_tiles)(
        functools.partial(_mixer_pre_tile, xs_ref, *refs, sample=True, seq_len=seq_len))


def _mixer_pre(x_p, x_s, st_conv_t, st_shift_t, p, *, seq_len, tm):
    n_p, d = x_p.shape
    n_tok = n_p + x_s.shape[0]
    n_prompt_tiles = n_p // tm
    n_tiles = n_tok // tm
    seqs = tm // seq_len
    row = lambda w: pl.BlockSpec((tm, w), lambda i: (i, 0))
    st_idx = lambda i: (jnp.maximum(i - n_prompt_tiles, 0), 0, 0)
    consts = [p['norm1_g'], p['w_in_a'], p['w_in_b'], p['w_in_g'], p['conv_w'], p['mu_shift'],
              p['w0'], p['w_lora_w'], p['a0'], p['w_lora_a'], p['w_lora_g'], p['k_k'], p['k_a'],
              p['r_k'], p['w_out_a'], p['ones_bf16']]
    in_specs = [pl.BlockSpec((tm, d), lambda i: (jnp.minimum(i, n_prompt_tiles - 1), 0)),
                pl.BlockSpec((tm, d), lambda i: (jnp.maximum(i - n_prompt_tiles, 0), 0)),
                pl.BlockSpec((1, 2 * seqs, CONV_DIM), st_idx),
                pl.BlockSpec((1, seqs, SHIFT_DIM), st_idx)] + [_const_spec(c.shape) for c in consts]
    sds = lambda w, dt=F32: jax.ShapeDtypeStruct((n_tok, w), dt)
    out_shape = [sds(d, BF16), sds(d, BF16)] + [sds(RWKV_DIM)] * 6 + [sds(RWKV_DIM, BF16)] * 2 + [
        jax.ShapeDtypeStruct((n_tiles, 2 * seqs, CONV_DIM), F32),
        jax.ShapeDtypeStruct((n_tiles, seqs, SHIFT_DIM), F32)]
    out_specs = [row(d), row(d)] + [row(RWKV_DIM)] * 8 + [
        pl.BlockSpec((1, 2 * seqs, CONV_DIM), lambda i: (i, 0, 0)),
        pl.BlockSpec((1, seqs, SHIFT_DIM), lambda i: (i, 0, 0))]
    kern = functools.partial(_mixer_pre_kernel, n_prompt_tiles=n_prompt_tiles, seq_len=seq_len)
    return pl.pallas_call(
        kern, out_shape=out_shape, grid=(n_tiles,), in_specs=in_specs, out_specs=out_specs,
        scratch_shapes=[pltpu.VMEM((HIST, CONV_DIM), F32), pltpu.VMEM((HIST, SHIFT_DIM), F32)],
        compiler_params=pltpu.CompilerParams(dimension_semantics=("arbitrary",),
                                             vmem_limit_bytes=VMEM_LIMIT),
        name="mixer_pre")(x_p, x_s, st_conv_t, st_shift_t, *consts)


GROUP_LANES = 256
HEADS_PER_GROUP = GROUP_LANES // RWKV_HEAD
N_HEAD_GROUPS = RWKV_DIM // GROUP_LANES
NN = (((1,), (0,)), ((), ()))
NT = (((1,), (1,)), ((), ()))


def _split2(x):
    hi = x.astype(BF16)
    lo = (x - hi.astype(F32)).astype(BF16)
    return hi, lo


def _mm(xs, ys, dims=NN):
    x1, x2 = xs
    y1, y2 = ys
    d = lambda p, q: lax.dot_general(p, q, dims, preferred_element_type=F32)
    m = x1.shape[0]
    both = d(jnp.concatenate([x1, x2], axis=0), y1)
    return both[0:m] + both[m:2 * m] + d(x1, y2)


def _cat2(ps, qs, axis):
    return tuple(jnp.concatenate([p, q], axis=axis) for p, q in zip(ps, qs))


def _wkv_masks(c):
    hc = HEADS_PER_GROUP * c
    levels = c.bit_length() - 1
    t = np.arange(c)[:, None]
    s = (np.arange(hc) % c)[None, :]
    tm = [s < t, s <= t]
    for lvl in range(1, levels + 1):
        half = 1 << (lvl - 1)
        tm.append(((t >> lvl) == (s >> lvl)) & ((t & half) != 0) & ((s & half) == 0))
    row_head = (np.arange(hc) // c)[:, None]
    mfeat = row_head == (np.arange(GROUP_LANES) // RWKV_HEAD)[None, :]
    mpos = row_head == (np.arange(hc) // c)[None, :]
    lane_head = np.arange(GROUP_LANES) // RWKV_HEAD
    stmask = lane_head[:, None] == lane_head[None, :]
    tri = np.arange(c)[None, :] <= np.arange(c)[:, None]
    return (jnp.asarray(mfeat, BF16), jnp.asarray(mpos, BF16), jnp.asarray(np.stack(tm), F32),
            jnp.asarray(stmask, F32), jnp.asarray(tri, BF16))


def _wkv_pipe_kernel(r_ref, lw_ref, k_ref, v_ref, a_ref, b_ref, s0_ref, mfeat_ref, mpos_ref,
                     tmask_ref, stmask_ref, tri_ref, y_ref, sout_ref,
                     state, sv_ar, sv_inv, sv_akv, sv_arbk, sv_vbd, sv_v, sv_bk, sv_pend,
                     *, chunk, n_steps, chained):
    i = pl.program_id(0)
    c = chunk
    n_chunks = r_ref.shape[0] // c
    hc = HEADS_PER_GROUP * c
    levels = c.bit_length() - 1
    groups = range(N_HEAD_GROUPS)
    units = [(j, g) for j in range(n_chunks) for g in groups]
    uid = {u: n for n, u in enumerate(units)}

    stmask = stmask_ref[...]

    def expand(sc):
        pair = jnp.concatenate([sc, sc], axis=1)
        return jnp.concatenate([pair] * (GROUP_LANES // pair.shape[1]), axis=1) * stmask

    def compact(s):
        half = s[:, 0:GROUP_LANES // 2] + s[:, GROUP_LANES // 2:]
        return half[:, 0:RWKV_HEAD] + half[:, RWKV_HEAD:]

    @pl.when(i == 0)
    def _():
        for ref in (sv_ar, sv_inv, sv_akv, sv_arbk, sv_vbd, sv_v, sv_bk):
            ref[...] = jnp.zeros(ref.shape, ref.dtype)
        sv_pend[...] = jnp.ones(sv_pend.shape, F32)
        if chained:
            for g in groups:
                state[g] = expand(s0_ref[0, g])

    mfeat = mfeat_ref[...]
    mpos = mpos_ref[...]
    strict = tmask_ref[0]
    incl = tmask_ref[1]
    eye = incl - strict
    tri = tri_ref[...]

    def ld(ref, j, g):
        return ref[j * c:(j + 1) * c, g * GROUP_LANES:(g + 1) * GROUP_LANES]

    def bd_split(ps):
        mask = mpos if ps[0].shape[1] == hc else mfeat
        return tuple(jnp.concatenate([p] * HEADS_PER_GROUP, axis=0) * mask for p in ps)

    def bd2(m):
        return bd_split(_split2(m))

    def cumsum_rows(x):
        p1 = x.astype(BF16)
        r1 = x - p1.astype(F32)
        p2 = r1.astype(BF16)
        p3 = (r1 - p2.astype(F32)).astype(BF16)
        d = lambda q: jnp.dot(tri, q, preferred_element_type=F32)
        return d(p1) + d(p2) + d(p3)

    new = {}

    def prepare():
        cum = {u: cumsum_rows(ld(lw_ref, *u)) for u in units}
        yield
        ar, bk_end, p_end, a_ab, a_ak, a_rb, a_rk, v, vbd = ({} for _ in range(9))
        for u in units:
            cm = cum[u]
            cum_last = cm[c - 1:c, :]
            e_neg = jnp.exp(-cm)
            e_end = jnp.exp(cum_last - cm)
            b_raw = ld(b_ref, *u)
            k_raw = ld(k_ref, *u)
            ar[u] = _split2(jnp.concatenate([ld(a_ref, *u) * jnp.exp(cm - ld(lw_ref, *u)),
                                             ld(r_ref, *u) * jnp.exp(cm)], axis=0))
            bk_end[u] = _split2(jnp.concatenate([b_raw * e_end, k_raw * e_end], axis=0))
            p_end[u] = jnp.exp(cum_last)
            v[u] = ld(v_ref, *u)
            vbd[u] = bd2(v[u])
            gram = _mm(ar[u], _cat2(bd2(b_raw * e_neg), bd2(k_raw * e_neg), 0), NT)
            a_ab[u] = jnp.where(strict > 0, gram[0:c, 0:hc], 0.0)
            a_ak[u] = jnp.where(strict > 0, gram[0:c, hc:2 * hc], 0.0)
            a_rb[u] = jnp.where(incl > 0, gram[c:2 * c, 0:hc], 0.0)
            a_rk[u] = jnp.where(incl > 0, gram[c:2 * c, hc:2 * hc], 0.0)
            if uid[u] % 2 == 1:
                yield
        a_ab2 = {u: _split2(a_ab[u]) for u in units}
        inv = {u: eye + a_ab[u] * tmask_ref[2] for u in units}
        inv2 = {u: _split2(inv[u]) for u in units}
        for lvl in range(2, levels + 1):
            lm = tmask_ref[1 + lvl]
            t1 = {u: _mm(a_ab2[u], bd_split(inv2[u])) for u in units}
            yield
            inv = {u: inv[u] + lm * _mm(inv2[u], bd2(t1[u])) for u in units}
            inv2 = {u: _split2(inv[u]) for u in units}
            yield
        akv = {u: _mm(_split2(a_ak[u]), vbd[u]) for u in units}
        arbk = {u: _split2(jnp.concatenate([a_rb[u], a_rk[u]], axis=1)) for u in units}
        new.update(ar=ar, inv=inv2, akv=akv, arbk=arbk, vbd=vbd, v=v, bk=bk_end, pend=p_end)

    def serial():
        pair = lambda ref, n: (ref[0, n], ref[1, n])
        s_cur = [state[g] for g in groups] if chained else None
        for j in range(n_chunks):
            ns = [uid[j, g] for g in groups]
            s_prev = s_cur if chained else [expand(s0_ref[j, g]) for g in groups]
            x0 = [_mm(pair(sv_ar, n), _split2(s_prev[g]), NT) for g, n in zip(groups, ns)]
            yield
            uu = [_mm(pair(sv_inv, n), bd2(x0[g][0:c] + sv_akv[n])) for g, n in zip(groups, ns)]
            yield
            for g, n in zip(groups, ns):
                yy = x0[g][c:2 * c] + _mm(pair(sv_arbk, n), _cat2(bd2(uu[g]), pair(sv_vbd, n), 0))
                y_ref[j * c:(j + 1) * c, g * GROUP_LANES:(g + 1) * GROUP_LANES] = yy
            s_new = []
            for g, n in zip(groups, ns):
                uv_t = jnp.transpose(jnp.concatenate([uu[g], sv_v[n]], axis=0))
                upd = _mm(_split2(uv_t), pair(sv_bk, n))
                s_new.append(s_prev[g] * sv_pend[n, 0:1, :] + stmask * upd)
            yield
            if chained:
                s_cur = s_new
            else:
                for g in groups:
                    sout_ref[j, g] = compact(s_new[g])
        if chained:
            for g in groups:
                state[g] = s_cur[g]
        new['last_state'] = s_new

    _interleave(serial(), prepare())

    zero = sum(_zero_after(s[0:8, 0:LANES]) for s in new['last_state'])[0:1, 0:1]
    zero_bf = zero.astype(BF16)
    for u, n in uid.items():
        for name, ref in (('ar', sv_ar), ('inv', sv_inv), ('arbk', sv_arbk), ('vbd', sv_vbd), ('bk', sv_bk)):
            for half in range(2):
                ref[half, n] = new[name][u][half] + zero_bf
        sv_akv[n] = new['akv'][u] + zero
        sv_v[n] = new['v'][u] + zero
        sv_pend[n] = jnp.broadcast_to(new['pend'][u] + zero, sv_pend.shape[1:])

    if chained:
        @pl.when(i == n_steps)
        def _():
            for g in groups:
                sout_ref[0, g] = compact(state[g])


def _wkv_scan(rlkvab, s0, *, row0, n_rows, chunk, chunks_per_step, chained):
    rows = chunk * chunks_per_step
    n_steps = n_rows // rows
    assert n_rows % rows == 0 and row0 % rows == 0
    masks = _wkv_masks(chunk)
    hc = HEADS_PER_GROUP * chunk
    n_units = chunks_per_step * N_HEAD_GROUPS
    st = (N_HEAD_GROUPS, GROUP_LANES, RWKV_HEAD)
    prev = lambda i: jnp.maximum(i - 1, 0)
    row_in = pl.BlockSpec((rows, RWKV_DIM), lambda i: (row0 // rows + jnp.minimum(i, n_steps - 1), 0))
    if chained:
        n_state = 1
        st_spec = pl.BlockSpec((1,) + st, lambda i: (0, 0, 0, 0))
    else:
        n_state = n_rows // chunk
        st_spec = pl.BlockSpec((chunks_per_step,) + st, lambda i: (prev(i), 0, 0, 0))
    in_specs = [row_in] * 6 + [st_spec] + [_const_spec(m.shape) for m in masks]
    out_shape = [jax.ShapeDtypeStruct((n_rows, RWKV_DIM), F32),
                 jax.ShapeDtypeStruct((n_state,) + st, F32)]
    out_specs = [pl.BlockSpec((rows, RWKV_DIM), lambda i: (prev(i), 0)), st_spec]
    scratch = [pltpu.VMEM((N_HEAD_GROUPS, GROUP_LANES, GROUP_LANES), F32),
               pltpu.VMEM((2, n_units, 2 * chunk, GROUP_LANES), BF16),
               pltpu.VMEM((2, n_units, chunk, hc), BF16),
               pltpu.VMEM((n_units, chunk, GROUP_LANES), F32),
               pltpu.VMEM((2, n_units, chunk, 2 * hc), BF16),
               pltpu.VMEM((2, n_units, hc, GROUP_LANES), BF16),
               pltpu.VMEM((n_units, chunk, GROUP_LANES), F32),
               pltpu.VMEM((2, n_units, 2 * chunk, GROUP_LANES), BF16),
               pltpu.VMEM((n_units, 8, GROUP_LANES), F32)]
    kern = functools.partial(_wkv_pipe_kernel, chunk=chunk, n_steps=n_steps, chained=chained)
    return pl.pallas_call(
        kern, out_shape=out_shape, grid=(n_steps + 1,), in_specs=in_specs, out_specs=out_specs,
        scratch_shapes=scratch,
        compiler_params=pltpu.CompilerParams(dimension_semantics=("arbitrary",),
                                             vmem_limit_bytes=VMEM_LIMIT),
        name="wkv_scan")(*rlkvab, s0, *masks)


def _mixer_post_kernel(yp_ref, ys_ref, bonus_ref, g_ref, yag_ref, sgb_ref, xp_ref, xs_ref, lng_ref, lnb_ref,
                       wob_ref, wo_ref, n2_ref, wr_ref, br_ref, ones_ref, tril_ref,
                       x1_ref, h2_ref, route_ref, counts_ref, cnt, *, n_prompt_tiles):
    i = pl.program_id(0)

    @pl.when(i == 0)
    def _():
        cnt[...] = jnp.zeros(cnt.shape, F32)

    tm = x1_ref.shape[0]
    n_parts = tm // ROW_TILE
    rp = tm // n_parts
    is_prompt = i < n_prompt_tiles
    ones = ones_ref[...]
    neg = jnp.float32(-jnp.inf)
    big = jnp.int32(LANES)
    lane = lax.broadcasted_iota(jnp.int32, (rp, LANES), 1)
    picked = {}

    def part(q):
        rs = slice(q * rp, (q + 1) * rp)
        y = jnp.where(is_prompt, yp_ref[rs, :], ys_ref[rs, :])
        inv_n = 1.0 / RWKV_HEAD
        mean = _split2_dot(y, ones) * inv_n
        yc = y - mean
        var = _split2_dot(yc * yc, ones) * inv_n
        yn = yc * lax.rsqrt(var + GN_EPS) * lng_ref[...] + lnb_ref[...]
        yy = (yn + bonus_ref[rs, :].astype(F32)) * g_ref[rs, :].astype(F32)
        yield
        y_b = jnp.dot(yy.astype(BF16), wob_ref[...], preferred_element_type=F32)
        merged = yag_ref[rs, :].astype(F32) + sgb_ref[rs, :].astype(F32) * y_b
        yield
        x = jnp.where(is_prompt, xp_ref[rs, :], xs_ref[rs, :])
        x1 = x + jnp.dot(merged.astype(BF16), wo_ref[...], preferred_element_type=F32)
        x1_ref[rs, :] = x1
        h2 = _rms(x1, n2_ref[...])
        h2_ref[rs, :] = _pack_bf16_pairs(h2)
        yield
        logits = _dot_stack3(h2, wr_ref[...]) + br_ref[...]
        yield
        is_g = lane < N_GROUPS
        lgp = jnp.where(is_g, logits, neg)
        m_g = jnp.max(lgp, axis=-1, keepdims=True)
        grp = jnp.min(jnp.where(lgp == m_g, lane, big), axis=-1, keepdims=True)
        p_top = 1.0 / jnp.sum(jnp.where(is_g, jnp.exp(logits - m_g), 0.0), axis=-1, keepdims=True)
        e_lane = lane - N_GROUPS
        in_grp = (e_lane >= grp * EXPERTS_PER_GROUP) & (e_lane < (grp + 1) * EXPERTS_PER_GROUP)
        le = jnp.where(in_grp, logits, neg)
        m1 = jnp.max(le, axis=-1, keepdims=True)
        i1 = jnp.min(jnp.where(le == m1, lane, big), axis=-1, keepdims=True)
        le2 = jnp.where(lane == i1, neg, le)
        m2 = jnp.max(le2, axis=-1, keepdims=True)
        i2 = jnp.min(jnp.where(le2 == m2, lane, big), axis=-1, keepdims=True)
        ex = jnp.exp(m2 - m1)
        picked[q] = (i1 - N_GROUPS, i2 - N_GROUPS, p_top / (1.0 + ex), p_top * ex / (1.0 + ex))

    gens = [part(q) for q in range(n_parts)]
    next(gens[0])
    _interleave(*gens)

    hot = [[lane == e for e in picked[q][0:2]] for q in range(n_parts)]
    both = jnp.concatenate([jnp.where(h1, 1.0, jnp.where(h2_, 1.0, 0.0)) for h1, h2_ in hot], axis=0)
    before = jnp.dot(tril_ref[...], both.astype(BF16), preferred_element_type=F32) + cnt[0:1, :]
    cnt[0:1, :] = cnt[0:1, :] + jnp.sum(both, axis=0, keepdims=True)
    counts_ref[...] = jnp.broadcast_to(cnt[0:1, :], counts_ref.shape)
    for q in range(n_parts):
        rs = slice(q * rp, (q + 1) * rp)
        e1, e2, w1, w2 = picked[q]
        ranks = [jnp.sum(jnp.where(h, before[rs, :], 0.0), axis=-1, keepdims=True) for h in hot[q]]
        route = jnp.zeros((rp, LANES), F32)
        for c, col in enumerate([e1.astype(F32), e2.astype(F32), w1, w2] + ranks):
            route = jnp.where(lane == c, col, route)
        route_ref[rs, :] = route


ROUTE_EXPERT, ROUTE_WEIGHT, ROUTE_RANK = 0, 2, 4


def _mixer_post(y_p, y_s, bonus, g, yag, sgb, x_p, x_s, p, *, tm):
    n_p, d = x_p.shape
    n_tok = n_p + x_s.shape[0]
    npt = n_p // tm
    row = lambda w: pl.BlockSpec((tm, w), lambda i: (i, 0))
    tril = jnp.asarray(np.arange(tm)[None, :] < np.arange(tm)[:, None], BF16)
    consts = [p['lnx_g'], p['lnx_b'], p['w_out_b'], p['w_o'], p['norm2_g'], p['w_router'],
              p['b_router'], p['ones_bf16'], tril]
    pair = lambda w: [pl.BlockSpec((tm, w), lambda i: (jnp.minimum(i, npt - 1), 0)),
                      pl.BlockSpec((tm, w), lambda i: (jnp.maximum(i - npt, 0), 0))]
    in_specs = (pair(RWKV_DIM) + [row(RWKV_DIM)] * 2 + [row(d)] * 2 + pair(d)
                + [_const_spec(c.shape) for c in consts])
    out_shape = [jax.ShapeDtypeStruct((n_tok, d), F32), jax.ShapeDtypeStruct((n_tok, d // 2), jnp.uint32),
                 jax.ShapeDtypeStruct((n_tok, LANES), F32), jax.ShapeDtypeStruct((8, LANES), F32)]
    out_specs = [row(d), row(d // 2), row(LANES), _const_spec((8, LANES))]
    return pl.pallas_call(
        functools.partial(_mixer_post_kernel, n_prompt_tiles=npt), out_shape=out_shape,
        grid=(n_tok // tm,), in_specs=in_specs, out_specs=out_specs,
        scratch_shapes=[pltpu.VMEM((8, LANES), F32)],
        compiler_params=pltpu.CompilerParams(dimension_semantics=("arbitrary",),
                                             vmem_limit_bytes=VMEM_LIMIT),
        name="mixer_post")(y_p, y_s, bonus, g, yag, sgb, x_p, x_s, *consts)


N_DMA_PRIORITIES = 2


def _gather_rows(idx_ref, src_hbm, dst, sem, n_rows, *, unrolled):
    def start(r, priority):
        pltpu.make_async_copy(src_hbm.at[pl.ds(idx_ref[r], 1)], dst.at[pl.ds(r, 1)],
                              sem).start(priority=priority)
    if unrolled:
        for r in range(n_rows):
            start(r, r % N_DMA_PRIORITIES)
    else:
        def body(r, carry):
            start(r, 0)
            return carry
        lax.fori_loop(0, n_rows, body, 0)


def _wait_rows(src_hbm, dst, sem, n_rows):
    pltpu.make_async_copy(src_hbm.at[pl.ds(0, n_rows)], dst, sem).wait()


def _moe_scatter_kernel(dest_ref, h_ref, xs_in_hbm, xs_hbm, sem):
    del xs_in_hbm
    n = dest_ref.shape[-1]
    tm = h_ref.shape[0]
    for r in range(n):
        pltpu.make_async_copy(h_ref.at[pl.ds(r % tm, 1)], xs_hbm.at[pl.ds(dest_ref[0, 0, r], 1)],
                              sem).start(priority=r % N_DMA_PRIORITIES)
    for j in range(n // tm):
        pltpu.make_async_copy(h_ref, xs_hbm.at[pl.ds(0, tm)], sem).wait()


def _moe_scatter(h2, pos3, n_rows):
    n_tok, d = h2.shape
    n_tiles, _, n = pos3.shape
    tm = n // TOP_K
    xs0 = jnp.zeros((n_rows, d), h2.dtype)
    return pl.pallas_call(
        _moe_scatter_kernel, out_shape=jax.ShapeDtypeStruct((n_rows, d), h2.dtype), grid=(n_tiles,),
        in_specs=[pl.BlockSpec((1, 1, n), lambda i: (i, 0, 0), memory_space=pltpu.SMEM),
                  pl.BlockSpec((tm, d), lambda i: (i, 0)), pl.BlockSpec(memory_space=pl.ANY)],
        out_specs=pl.BlockSpec(memory_space=pl.ANY),
        scratch_shapes=[pltpu.SemaphoreType.DMA(())], input_output_aliases={2: 0},
        compiler_params=pltpu.CompilerParams(dimension_semantics=("arbitrary",), has_side_effects=True,
                                             vmem_limit_bytes=VMEM_LIMIT),
        name="moe_scatter")(pos3, h2, xs0)


def _moe_experts_kernel(bexp_ref, nused_ref, x_ref, wg_ref, wu_ref, wd_ref, yb_ref, wg_bf, wu_bf, wd_bf):
    i = pl.program_id(0)

    @pl.when((i == 0) | (bexp_ref[i] != bexp_ref[jnp.maximum(i - 1, 0)]))
    def _():
        wg_bf[...] = wg_ref[0].astype(BF16)
        wu_bf[...] = wu_ref[0].astype(BF16)
        wd_bf[...] = wd_ref[0].astype(BF16)

    @pl.when(i < nused_ref[0])
    def _():
        def part(rs):
            xb = _unpack_bf16_pairs(x_ref[rs, :]).astype(BF16)
            hg = jnp.dot(xb, wg_bf[...], preferred_element_type=F32)
            yield
            hu = jnp.dot(xb, wu_bf[...], preferred_element_type=F32)
            hid = (hg * jax.nn.sigmoid(hg)) * hu
            yield
            yb_ref[rs, :] = _pack_bf16_pairs(jnp.dot(hid.astype(BF16), wd_bf[...], preferred_element_type=F32))

        rows = x_ref.shape[0]
        _interleave(*[part(slice(r0, r0 + ROW_TILE)) for r0 in range(0, rows, ROW_TILE)])

    @pl.when(i >= nused_ref[0])
    def _():
        yb_ref[...] = jnp.zeros(yb_ref.shape, yb_ref.dtype)


def _moe_experts(xs, block_expert, n_used, w_eg, w_eu, w_ed, *, blk):
    n_rows, dp = xs.shape
    n_blocks = n_rows // blk
    d, de = w_eg.shape[1:]
    grid_spec = pltpu.PrefetchScalarGridSpec(
        num_scalar_prefetch=2, grid=(n_blocks,),
        in_specs=[
            pl.BlockSpec((blk, dp), lambda i, be, nu: (jnp.minimum(i, nu[0] - 1), 0)),
            pl.BlockSpec((1, d, de), lambda i, be, nu: (be[i], 0, 0)),
            pl.BlockSpec((1, d, de), lambda i, be, nu: (be[i], 0, 0)),
            pl.BlockSpec((1, de, d), lambda i, be, nu: (be[i], 0, 0)),
        ],
        out_specs=pl.BlockSpec((blk, dp), lambda i, be, nu: (i, 0)),
        scratch_shapes=[pltpu.VMEM((d, de), BF16), pltpu.VMEM((d, de), BF16), pltpu.VMEM((de, d), BF16)])
    return pl.pallas_call(
        _moe_experts_kernel, out_shape=jax.ShapeDtypeStruct((n_rows, dp), xs.dtype), grid_spec=grid_spec,
        compiler_params=pltpu.CompilerParams(dimension_semantics=("arbitrary",),
                                             vmem_limit_bytes=VMEM_LIMIT),
        name="moe_experts")(block_expert, n_used, xs, w_eg, w_eu, w_ed)


GATHER_AHEAD = 2


def _with_dummy_blocks(idx, n_blocks, blk):
    pad = jnp.zeros((GATHER_AHEAD * blk,), idx.dtype)
    return jnp.concatenate([idx, pad]).reshape(n_blocks + GATHER_AHEAD, 1, blk)


def _moe_combine_kernel(*refs, n_prompt_tiles):
    pos_refs = refs[:GATHER_AHEAD + 1]
    x1_ref, route_ref, nf_ref, yb_hbm, outp_ref, outs_ref, ybuf, sems = refs[GATHER_AHEAD + 1:]
    i = pl.program_id(0)
    n_slots = GATHER_AHEAD + 1
    rows = ybuf.shape[1]
    slot = i % n_slots

    @pl.when(i == 0)
    def _():
        for a in range(GATHER_AHEAD):
            _gather_rows(pos_refs[a].at[0, 0], yb_hbm, ybuf.at[a], sems.at[a], rows, unrolled=False)

    _wait_rows(yb_hbm, ybuf.at[slot], sems.at[slot], rows)
    ahead = (i + GATHER_AHEAD) % n_slots
    _gather_rows(pos_refs[GATHER_AHEAD].at[0, 0], yb_hbm, ybuf.at[ahead], sems.at[ahead], rows, unrolled=True)
    tm = x1_ref.shape[0]
    route = route_ref[...]
    x2 = x1_ref[...]
    for j in range(TOP_K):
        yj = _unpack_bf16_pairs(ybuf[slot, j * tm:(j + 1) * tm, :])
        x2 = x2 + yj * route[:, ROUTE_WEIGHT + j:ROUTE_WEIGHT + j + 1]
    out = _rms(x2, nf_ref[...])

    @pl.when(i < n_prompt_tiles)
    def _():
        outp_ref[...] = out

    @pl.when(i >= n_prompt_tiles)
    def _():
        outs_ref[...] = out

    @pl.when(i == pl.num_programs(0) - 1)
    def _():
        for a in range(1, n_slots):
            s = (i + a) % n_slots
            _wait_rows(yb_hbm, ybuf.at[s], sems.at[s], rows)


def _moe_combine(pos_tiles, x1, route, normf_g, yb, *, n_prompt_rows, tm):
    n_tok, d = x1.shape
    n_tiles = n_tok // tm
    npt = n_prompt_rows // tm
    pos3 = _with_dummy_blocks(pos_tiles.reshape(-1), n_tiles, TOP_K * tm)
    pos_spec = lambda a: pl.BlockSpec((1, 1, TOP_K * tm), lambda i: (i + a, 0, 0), memory_space=pltpu.SMEM)
    in_specs = [pos_spec(a) for a in range(GATHER_AHEAD + 1)] + [
        pl.BlockSpec((tm, d), lambda i: (i, 0)),
        pl.BlockSpec((tm, LANES), lambda i: (i, 0)),
        _const_spec(normf_g.shape),
        pl.BlockSpec(memory_space=pl.ANY),
    ]
    out_shape = [jax.ShapeDtypeStruct((n_prompt_rows, d), F32),
                 jax.ShapeDtypeStruct((n_tok - n_prompt_rows, d), F32)]
    out_specs = [pl.BlockSpec((tm, d), lambda i: (jnp.minimum(i, npt - 1), 0)),
                 pl.BlockSpec((tm, d), lambda i: (jnp.maximum(i - npt, 0), 0))]
    return pl.pallas_call(
        functools.partial(_moe_combine_kernel, n_prompt_tiles=npt), out_shape=out_shape,
        grid=(n_tiles,), in_specs=in_specs, out_specs=out_specs,
        scratch_shapes=[pltpu.VMEM((GATHER_AHEAD + 1, TOP_K * tm, yb.shape[1]), yb.dtype),
                        pltpu.SemaphoreType.DMA((GATHER_AHEAD + 1,))],
        compiler_params=pltpu.CompilerParams(dimension_semantics=("arbitrary",),
                                             vmem_limit_bytes=VMEM_LIMIT),
        name="moe_combine")(*([pos3] * (GATHER_AHEAD + 1)), x1, route, normf_g, yb)


def _dispatch(route, counts, blk, tm):
    n_tok = route.shape[0]
    n_assign = n_tok * TOP_K
    expert = route[:, ROUTE_EXPERT:ROUTE_EXPERT + TOP_K].astype(jnp.int32)
    rank = route[:, ROUTE_RANK:ROUTE_RANK + TOP_K].astype(jnp.int32)
    counts = counts[0, :N_EXPERTS].astype(jnp.int32)
    padded = (counts + blk - 1) // blk * blk
    pad_end = jnp.cumsum(padded)
    pad_start = pad_end - padded
    is_e = expert[:, :, None] == jnp.arange(N_EXPERTS, dtype=jnp.int32)
    dest = jnp.sum(jnp.where(is_e, pad_start, 0), axis=-1) + rank
    n_blocks = -(-n_assign // blk) + N_EXPERTS
    block_start = jnp.arange(n_blocks, dtype=jnp.int32) * blk
    block_expert = jnp.minimum(jnp.sum(block_start[:, None] >= pad_end[None, :], axis=1),
                               N_EXPERTS - 1).astype(jnp.int32)
    n_used = (pad_end[-1] // blk).reshape(1)
    pos_tiles = [dest.reshape(n_tok // t, t, TOP_K).transpose(0, 2, 1).reshape(n_tok // t, 1, TOP_K * t)
                 for t in tm]
    return pos_tiles, block_expert, n_used, n_blocks


def _state_to_kernel(s):
    return s.reshape(s.shape[0], N_HEAD_GROUPS, GROUP_LANES, RWKV_HEAD)


def _state_from_kernel(s):
    return s.reshape(s.shape[0], RWKV_HEADS, RWKV_HEAD, RWKV_HEAD)


def kernel(x_prompt, x_sample, state_conv, state_shift, state_wkv, norm1_g, w_in, conv_w, mu_shift, w0, w_lora_w, a0, w_lora_a, w_lora_g, k_k, k_a, r_k, lnx_g, lnx_b, w_out_a, w_out_b, w_o, norm2_g, w_router_group, b_router_group, w_router_expert, b_router_expert, w_e_gate, w_e_up, w_e_down, normf_g):
    depth = norm1_g.shape[0]
    bp, seq, d = x_prompt.shape
    db, dseq, _ = x_sample.shape
    assert depth == 1 and bp == 1, "single layer, single prompt stream"
    tm = ROW_TILE
    n_p, n_s = bp * seq, db * dseq
    assert n_p % tm == 0 and n_s % tm == 0 and tm % dseq == 0
    n_prompt_tiles, n_sample_tiles = n_p // tm, n_s // tm
    seqs = tm // dseq
    n_tok = n_p + n_s

    x_p = x_prompt.reshape(n_p, d)
    x_s = x_sample.reshape(n_s, d)

    l = 0
    c3 = 3 * CONV_DIM
    head_id = np.arange(GROUP_LANES) // RWKV_HEAD
    ones_bf16 = jnp.asarray(head_id[:, None] == head_id[None, :], BF16)
    zpad = jnp.zeros((LORA_W, RWKV_DIM), F32)
    n_r = N_GROUPS + N_EXPERTS
    p = {
        'norm1_g': norm1_g[l].reshape(1, d),
        'w_in_a': w_in[l][:, :c3].astype(BF16),
        'w_in_b': w_in[l][:, c3:c3 + SHIFT_DIM].astype(BF16),
        'w_in_g': w_in[l][:, c3 + SHIFT_DIM:].astype(BF16),
        'conv_w': conv_w[l],
        'mu_shift': mu_shift[l].reshape(1, SHIFT_DIM),
        'w0': w0[l].reshape(1, RWKV_DIM),
        'w_lora_w': _stack3(jnp.concatenate([w_lora_w[l], zpad], axis=0)),
        'a0': a0[l].reshape(1, RWKV_DIM),
        'w_lora_a': _stack3(jnp.concatenate([zpad, w_lora_a[l]], axis=0)),
        'w_lora_g': _stack3(w_lora_g[l]),
        'k_k': k_k[l].reshape(1, RWKV_DIM),
        'k_a': k_a[l].reshape(1, RWKV_DIM),
        'r_k': r_k[l].reshape(1, RWKV_DIM),
        'w_out_a': w_out_a[l].astype(BF16),
        'ones_bf16': ones_bf16,
        'lnx_g': lnx_g[l].reshape(1, RWKV_DIM),
        'lnx_b': lnx_b[l].reshape(1, RWKV_DIM),
        'w_out_b': w_out_b[l].astype(BF16),
        'w_o': w_o[l].astype(BF16),
        'norm2_g': norm2_g[l].reshape(1, d),
        'w_router': _stack3(jnp.pad(jnp.concatenate([w_router_group[l], w_router_expert[l]], axis=1),
                                    ((0, 0), (0, LANES - n_r)))),
        'b_router': jnp.pad(jnp.concatenate([b_router_group[l], b_router_expert[l]]),
                            (0, LANES - n_r)).reshape(1, LANES),
    }
    st_conv_t = state_conv[l].reshape(n_sample_tiles, seqs * (CONV_WIDTH - 1), CONV_DIM)
    st_shift_t = state_shift[l].reshape(n_sample_tiles, seqs, SHIFT_DIM)

    (yag, sgb, r, w, k, v, a, b, bonus, g, ctail, stail) = _mixer_pre(
        x_p, x_s, st_conv_t, st_shift_t, p, seq_len=dseq, tm=tm)

    s0_prompt = jnp.zeros((1, N_HEAD_GROUPS, GROUP_LANES, RWKV_HEAD), F32)
    s0_sample = _state_to_kernel(state_wkv[l])
    rlkvab = (r, w, k, v, a, b)
    cp = min(SCAN_CHUNK, n_p)
    assert cp & (cp - 1) == 0 and dseq & (dseq - 1) == 0
    y_p, s_p = _wkv_scan(rlkvab, s0_prompt, row0=0, n_rows=n_p, chunk=cp,
                         chunks_per_step=SCAN_CHUNKS_PER_STEP, chained=True)
    y_s, s_s = _wkv_scan(rlkvab, s0_sample, row0=n_p, n_rows=n_s, chunk=dseq,
                         chunks_per_step=SCAN_CHUNKS_PER_STEP, chained=False)

    post_tm = POST_TILES * tm
    assert n_p % post_tm == 0 and n_s % post_tm == 0
    x1, h2, route, counts = _mixer_post(y_p, y_s, bonus, g, yag, sgb, x_p, x_s, p, tm=post_tm)

    blk = MOE_ROWS
    (pos_scatter, pos_combine), block_expert, n_used, n_blocks = _dispatch(route, counts, blk, (post_tm, tm))
    xs = _moe_scatter(h2, pos_scatter, n_blocks * blk)
    yb = _moe_experts(xs, block_expert, n_used, w_e_gate[l], w_e_up[l], w_e_down[l], blk=blk)
    out_p, out_s = _moe_combine(pos_combine, x1, route, normf_g.reshape(1, d), yb, n_prompt_rows=n_p, tm=tm)

    y_prompt = out_p.reshape(bp, seq, d)
    y_sample = out_s.reshape(db, dseq, d)
    conv_p = ctail[n_prompt_tiles - 1, 2 * (seqs - 1):2 * seqs].reshape(1, bp, CONV_WIDTH - 1, CONV_DIM)
    shift_p = stail[n_prompt_tiles - 1, seqs - 1].reshape(1, bp, 1, SHIFT_DIM)
    wkv_p = _state_from_kernel(s_p).reshape(1, bp, RWKV_HEADS, RWKV_HEAD, RWKV_HEAD)
    conv_s = ctail[n_prompt_tiles:].reshape(1, db, CONV_WIDTH - 1, CONV_DIM)
    shift_s = stail[n_prompt_tiles:].reshape(1, db, 1, SHIFT_DIM)
    wkv_s = _state_from_kernel(s_s).reshape(1, db, RWKV_HEADS, RWKV_HEAD, RWKV_HEAD)
    return (y_prompt, y_sample, conv_p, shift_p, wkv_p, conv_s, shift_s, wkv_s)
```

```python
import functools

import numpy as np
import jax
import jax.numpy as jnp
from jax import lax
from jax.experimental import pallas as pl
from jax.experimental.pallas import tpu as pltpu

F32 = jnp.float32
BF16 = jnp.bfloat16

CONV_DIM = 512
CONV_WIDTH = 3
RWKV_HEAD = 64
RWKV_HEADS = 8
RWKV_DIM = RWKV_HEADS * RWKV_HEAD
LORA_W = 64
LORA_A = 64
LORA_G = 128
SHIFT_DIM = 3 * RWKV_DIM + LORA_W + LORA_A + LORA_G
N_GROUPS = 4
EXPERTS_PER_GROUP = 8
N_EXPERTS = N_GROUPS * EXPERTS_PER_GROUP
TOP_K = 2
RMS_EPS = 1e-6
GN_EPS = 64e-5

LANES = 128
ROW_TILE = 256
POST_TILES = 2
SCAN_CHUNK = 64
SCAN_CHUNKS_PER_STEP = 4
MOE_ROWS = 512
HIST = 8
VMEM_LIMIT = 56 * 1024 * 1024


def _rms(x, g):
    return x * lax.rsqrt(jnp.mean(x * x, axis=-1, keepdims=True) + RMS_EPS) * g


def _split2_dot(x, ones_bf16):
    w = ones_bf16.shape[0]
    hi = x.astype(BF16)
    lo = (x - hi.astype(F32)).astype(BF16)
    parts = []
    for c0 in range(0, x.shape[1], w):
        both = jnp.concatenate([hi[:, c0:c0 + w], lo[:, c0:c0 + w]], axis=0)
        s = jnp.dot(both, ones_bf16, preferred_element_type=F32)
        parts.append(s[0:x.shape[0]] + s[x.shape[0]:])
    return jnp.concatenate(parts, axis=1)


def _pack_bf16_pairs(x):
    n = x.shape[1] // 2
    lo = pltpu.bitcast(x[:, :n].astype(BF16).astype(F32), jnp.uint32)
    hi = pltpu.bitcast(x[:, n:].astype(BF16).astype(F32), jnp.uint32)
    return hi | (lo >> 16)


def _unpack_bf16_pairs(p):
    lo = pltpu.bitcast(p << 16, F32)
    hi = pltpu.bitcast(p & jnp.uint32(0xFFFF0000), F32)
    return jnp.concatenate([lo, hi], axis=1)


def _stack3(w):
    hi = w.astype(BF16)
    lo = (w - hi.astype(F32)).astype(BF16)
    return jnp.concatenate([hi, hi, lo], axis=0)


def _dot_stack3(x, w_stack):
    hi = x.astype(BF16)
    lo = (x - hi.astype(F32)).astype(BF16)
    return jnp.dot(jnp.concatenate([hi, lo, hi], axis=1), w_stack, preferred_element_type=F32)


def _const_spec(shape):
    nd = len(shape)
    return pl.BlockSpec(shape, lambda *_: (0,) * nd)


def _interleave(*gens):
    live = list(gens)
    while live:
        for gen in list(live):
            try:
                next(gen)
            except StopIteration:
                live.remove(gen)


def _zero_after(x):
    return pltpu.bitcast(lax.shift_right_logical(pltpu.bitcast(x, jnp.uint32), jnp.uint32(32)), F32)


def _mixer_pre_tile(x_ref, stc_ref, sts_ref, n1_ref, wa_ref, wb_ref, wg_ref, convw_ref, mu_ref,
                    w0_ref, lw_ref, a0_ref, la_ref, lgw_ref, kk_ref, ka_ref, rk_ref, woa_ref, ones_ref,
                    yag_ref, sgb_ref, r_ref, w_ref, k_ref, v_ref, a_ref, b_ref, bonus_ref, g_ref,
                    ctail_ref, stail_ref, ccarry, scarry, *, sample, seq_len):
    tm = x_ref.shape[0]
    seqs = tm // seq_len
    h = _rms(x_ref[...], n1_ref[...]).astype(BF16)

    def prev_rows(val, k, heads):
        row = lax.broadcasted_iota(jnp.int32, val.shape, 0)
        out = pltpu.roll(val, k, axis=0)
        for r0, head in heads.items():
            out = jnp.where(row == r0, head, out)
        return out

    pb = jnp.dot(h, wb_ref[...], preferred_element_type=F32)
    old_s = scarry[...]
    if sample:
        s_heads = {j * seq_len: sts_ref[0, j:j + 1, :] for j in range(seqs)}
    else:
        s_heads = {0: old_s[HIST - 1:HIST, :]}
    prev = prev_rows(pb, 1, s_heads)
    for j in range(seqs):
        r1 = (j + 1) * seq_len
        stail_ref[0, j:j + 1, :] = pb[r1 - 1:r1, :]
    scarry[...] = pb[tm - HIST:tm, :] + _zero_after(old_s)

    def conv_and_gates():
        n_a = wa_ref.shape[1] // 3
        g_in = jnp.dot(h, wa_ref[:, 0:n_a], preferred_element_type=F32)
        yield
        g_out = jnp.dot(h, wa_ref[:, n_a:2 * n_a], preferred_element_type=F32)
        yield
        x_c = jnp.dot(h, wa_ref[:, 2 * n_a:3 * n_a], preferred_element_type=F32)
        bx = g_in * x_c
        old_c = ccarry[...]
        if sample:
            h1 = {j * seq_len: stc_ref[0, 2 * j + 1:2 * j + 2, :] for j in range(seqs)}
            h2 = {j * seq_len: stc_ref[0, 2 * j:2 * j + 1, :] for j in range(seqs)}
            h2.update({j * seq_len + 1: stc_ref[0, 2 * j + 1:2 * j + 2, :] for j in range(seqs)})
        else:
            h1 = {0: old_c[HIST - 1:HIST, :]}
            h2 = {0: old_c[HIST - 2:HIST - 1, :], 1: old_c[HIST - 1:HIST, :]}
        cw = convw_ref[...]
        conv = cw[0:1, :] * prev_rows(bx, 2, h2) + cw[1:2, :] * prev_rows(bx, 1, h1) + cw[2:3, :] * bx
        for j in range(seqs):
            r1 = (j + 1) * seq_len
            ctail_ref[0, 2 * j:2 * j + 2, :] = bx[r1 - 2:r1, :]
        ccarry[...] = bx[tm - HIST:tm, :] + _zero_after(old_c)
        yield
        y_a = jnp.dot((g_out * conv).astype(BF16), woa_ref[...], preferred_element_type=F32)
        yield
        d = wg_ref.shape[1] // 2
        half = d // 2
        for c0 in range(0, d, half):
            pg = jnp.dot(h, wg_ref[:, c0:c0 + half], preferred_element_type=F32)
            yag_ref[:, c0:c0 + half] = (jax.nn.sigmoid(pg) * y_a[:, c0:c0 + half]).astype(yag_ref.dtype)
            yield
        for c0 in range(0, d, half):
            pg = jnp.dot(h, wg_ref[:, d + c0:d + c0 + half], preferred_element_type=F32)
            sgb_ref[:, c0:c0 + half] = jax.nn.sigmoid(pg).astype(sgb_ref.dtype)
            yield

    def rwkv_pre():
        s = pb + (prev - pb) * mu_ref[...]
        o1, o2, o3 = RWKV_DIM, 2 * RWKV_DIM, 3 * RWKV_DIM
        r = s[:, 0:o1]
        k = s[:, o1:o2]
        v = s[:, o2:o3]
        s_l = s[:, o3:o3 + LORA_W + LORA_A]
        lg = s[:, o3 + LORA_W + LORA_A:]
        r_ref[...] = r
        v_ref[...] = v
        yield
        z = w0_ref[...] + _dot_stack3(jnp.tanh(s_l), lw_ref[...])
        w_log = -jax.nn.softplus(-z) - 0.5
        w_ref[...] = -jnp.exp(w_log)
        yield
        a = jax.nn.sigmoid(a0_ref[...] + _dot_stack3(s_l, la_ref[...]))
        yield
        g_ref[...] = _dot_stack3(jax.nn.sigmoid(lg), lgw_ref[...]).astype(g_ref.dtype)
        yield
        ones = ones_ref[...]
        kk = k * kk_ref[...]
        kk_n = kk / jnp.maximum(jnp.sqrt(_split2_dot(kk * kk, ones)), 1e-12)
        a_ref[...] = -kk_n
        b_ref[...] = kk_n * a
        yield
        k2 = k * (1.0 + (a - 1.0) * ka_ref[...])
        k_ref[...] = k2
        yield
        bonus_ref[...] = (_split2_dot(r * k2 * rk_ref[...], ones) * v).astype(bonus_ref.dtype)

    _interleave(conv_and_gates(), rwkv_pre())


def _mixer_pre_kernel(xp_ref, xs_ref, *refs, n_prompt_tiles, seq_len):
    i = pl.program_id(0)
    ccarry, scarry = refs[-2:]

    @pl.when(i == 0)
    def _():
        ccarry[...] = jnp.zeros(ccarry.shape, F32)
        scarry[...] = jnp.zeros(scarry.shape, F32)

    pl.when(i < n_prompt_tiles)(
        functools.partial(_mixer_pre_tile, xp_ref, *refs, sample=False, seq_len=seq_len))
    pl.when(i >= n_prompt_tiles)(
        functools.partial(_mixer_pre_tile, xs_ref, *refs, sample=True, seq_len=seq_len))


def _mixer_pre(x_p, x_s, st_conv_t, st_shift_t, p, *, seq_len, tm):
    n_p, d = x_p.shape
    n_tok = n_p + x_s.shape[0]
    n_prompt_tiles = n_p // tm
    n_tiles = n_tok // tm
    seqs = tm // seq_len
    row = lambda w: pl.BlockSpec((tm, w), lambda i: (i, 0))
    st_idx = lambda i: (jnp.maximum(i - n_prompt_tiles, 0), 0, 0)
    consts = [p['norm1_g'], p['w_in_a'], p['w_in_b'], p['w_in_g'], p['conv_w'], p['mu_shift'],
              p['w0'], p['w_lora_w'], p['a0'], p['w_lora_a'], p['w_lora_g'], p['k_k'], p['k_a'],
              p['r_k'], p['w_out_a'], p['ones_bf16']]
    in_specs = [pl.BlockSpec((tm, d), lambda i: (jnp.minimum(i, n_prompt_tiles - 1), 0)),
                pl.BlockSpec((tm, d), lambda i: (jnp.maximum(i - n_prompt_tiles, 0), 0)),
                pl.BlockSpec((1, 2 * seqs, CONV_DIM), st_idx),
                pl.BlockSpec((1, seqs, SHIFT_DIM), st_idx)] + [_const_spec(c.shape) for c in consts]
    sds = lambda w, dt=F32: jax.ShapeDtypeStruct((n_tok, w), dt)
    out_shape = [sds(d, BF16), sds(d, BF16)] + [sds(RWKV_DIM)] * 6 + [sds(RWKV_DIM, BF16)] * 2 + [
        jax.ShapeDtypeStruct((n_tiles, 2 * seqs, CONV_DIM), F32),
        jax.ShapeDtypeStruct((n_tiles, seqs, SHIFT_DIM), F32)]
    out_specs = [row(d), row(d)] + [row(RWKV_DIM)] * 8 + [
        pl.BlockSpec((1, 2 * seqs, CONV_DIM), lambda i: (i, 0, 0)),
        pl.BlockSpec((1, seqs, SHIFT_DIM), lambda i: (i, 0, 0))]
    kern = functools.partial(_mixer_pre_kernel, n_prompt_tiles=n_prompt_tiles, seq_len=seq_len)
    return pl.pallas_call(
        kern, out_shape=out_shape, grid=(n_tiles,), in_specs=in_specs, out_specs=out_specs,
        scratch_shapes=[pltpu.VMEM((HIST, CONV_DIM), F32), pltpu.VMEM((HIST, SHIFT_DIM), F32)],
        compiler_params=pltpu.CompilerParams(dimension_semantics=("arbitrary",),
                                             vmem_limit_bytes=VMEM_LIMIT),
        name="mixer_pre")(x_p, x_s, st_conv_t, st_shift_t, *consts)


GROUP_LANES = 256
HEADS_PER_GROUP = GROUP_LANES // RWKV_HEAD
N_HEAD_GROUPS = RWKV_DIM // GROUP_LANES
NN = (((1,), (0,)), ((), ()))
NT = (((1,), (1,)), ((), ()))


def _split2(x):
    hi = x.astype(BF16)
    lo = (x - hi.astype(F32)).astype(BF16)
    return hi, lo


def _mm(xs, ys, dims=NN):
    x1, x2 = xs
    y1, y2 = ys
    d = lambda p, q: lax.dot_general(p, q, dims, preferred_element_type=F32)
    m = x1.shape[0]
    both = d(jnp.concatenate([x1, x2], axis=0), y1)
    return both[0:m] + both[m:2 * m] + d(x1, y2)


def _cat2(ps, qs, axis):
    return tuple(jnp.concatenate([p, q], axis=axis) for p, q in zip(ps, qs))


def _wkv_masks(c):
    hc = HEADS_PER_GROUP * c
    levels = c.bit_length() - 1
    t = np.arange(c)[:, None]
    s = (np.arange(hc) % c)[None, :]
    tm = [s < t, s <= t]
    for lvl in range(1, levels + 1):
        half = 1 << (lvl - 1)
        tm.append(((t >> lvl) == (s >> lvl)) & ((t & half) != 0) & ((s & half) == 0))
    row_head = (np.arange(hc) // c)[:, None]
    mfeat = row_head == (np.arange(GROUP_LANES) // RWKV_HEAD)[None, :]
    mpos = row_head == (np.arange(hc) // c)[None, :]
    lane_head = np.arange(GROUP_LANES) // RWKV_HEAD
    stmask = lane_head[:, None] == lane_head[None, :]
    tri = np.arange(c)[None, :] <= np.arange(c)[:, None]
    return (jnp.asarray(mfeat, BF16), jnp.asarray(mpos, BF16), jnp.asarray(np.stack(tm), F32),
            jnp.asarray(stmask, F32), jnp.asarray(tri, BF16))


def _wkv_pipe_kernel(r_ref, lw_ref, k_ref, v_ref, a_ref, b_ref, s0_ref, mfeat_ref, mpos_ref,
                     tmask_ref, stmask_ref, tri_ref, y_ref, sout_ref,
                     state, sv_ar, sv_inv, sv_akv, sv_arbk, sv_vbd, sv_v, sv_bk, sv_pend,
                     *, chunk, n_steps, chained):
    i = pl.program_id(0)
    c = chunk
    n_chunks = r_ref.shape[0] // c
    hc = HEADS_PER_GROUP * c
    levels = c.bit_length() - 1
    groups = range(N_HEAD_GROUPS)
    units = [(j, g) for j in range(n_chunks) for g in groups]
    uid = {u: n for n, u in enumerate(units)}

    stmask = stmask_ref[...]

    def expand(sc):
        pair = jnp.concatenate([sc, sc], axis=1)
        return jnp.concatenate([pair] * (GROUP_LANES // pair.shape[1]), axis=1) * stmask

    def compact(s):
        half = s[:, 0:GROUP_LANES // 2] + s[:, GROUP_LANES // 2:]
        return half[:, 0:RWKV_HEAD] + half[:, RWKV_HEAD:]

    @pl.when(i == 0)
    def _():
        for ref in (sv_ar, sv_inv, sv_akv, sv_arbk, sv_vbd, sv_v, sv_bk):
            ref[...] = jnp.zeros(ref.shape, ref.dtype)
        sv_pend[...] = jnp.ones(sv_pend.shape, F32)
        if chained:
            for g in groups:
                state[g] = expand(s0_ref[0, g])

    mfeat = mfeat_ref[...]
    mpos = mpos_ref[...]
    strict = tmask_ref[0]
    incl = tmask_ref[1]
    eye = incl - strict
    tri = tri_ref[...]

    def ld(ref, j, g):
        return ref[j * c:(j + 1) * c, g * GROUP_LANES:(g + 1) * GROUP_LANES]

    def bd_split(ps):
        mask = mpos if ps[0].shape[1] == hc else mfeat
        return tuple(jnp.concatenate([p] * HEADS_PER_GROUP, axis=0) * mask for p in ps)

    def bd2(m):
        return bd_split(_split2(m))

    def cumsum_rows(x):
        p1 = x.astype(BF16)
        r1 = x - p1.astype(F32)
        p2 = r1.astype(BF16)
        p3 = (r1 - p2.astype(F32)).astype(BF16)
        d = lambda q: jnp.dot(tri, q, preferred_element_type=F32)
        return d(p1) + d(p2) + d(p3)

    new = {}

    def prepare():
        cum = {u: cumsum_rows(ld(lw_ref, *u)) for u in units}
        yield
        ar, bk_end, p_end, a_ab, a_ak, a_rb, a_rk, v, vbd = ({} for _ in range(9))
        for u in units:
            cm = cum[u]
            cum_last = cm[c - 1:c, :]
            e_neg = jnp.exp(-cm)
            e_end = jnp.exp(cum_last - cm)
            b_raw = ld(b_ref, *u)
            k_raw = ld(k_ref, *u)
            ar[u] = _split2(jnp.concatenate([ld(a_ref, *u) * jnp.exp(cm - ld(lw_ref, *u)),
                                             ld(r_ref, *u) * jnp.exp(cm)], axis=0))
            bk_end[u] = _split2(jnp.concatenate([b_raw * e_end, k_raw * e_end], axis=0))
            p_end[u] = jnp.exp(cum_last)
            v[u] = ld(v_ref, *u)
            vbd[u] = bd2(v[u])
            gram = _mm(ar[u], _cat2(bd2(b_raw * e_neg), bd2(k_raw * e_neg), 0), NT)
            a_ab[u] = jnp.where(strict > 0, gram[0:c, 0:hc], 0.0)
            a_ak[u] = jnp.where(strict > 0, gram[0:c, hc:2 * hc], 0.0)
            a_rb[u] = jnp.where(incl > 0, gram[c:2 * c, 0:hc], 0.0)
            a_rk[u] = jnp.where(incl > 0, gram[c:2 * c, hc:2 * hc], 0.0)
            if uid[u] % 2 == 1:
                yield
        a_ab2 = {u: _split2(a_ab[u]) for u in units}
        inv = {u: eye + a_ab[u] * tmask_ref[2] for u in units}
        inv2 = {u: _split2(inv[u]) for u in units}
        for lvl in range(2, levels + 1):
            lm = tmask_ref[1 + lvl]
            t1 = {u: _mm(a_ab2[u], bd_split(inv2[u])) for u in units}
            yield
            inv = {u: inv[u] + lm * _mm(inv2[u], bd2(t1[u])) for u in units}
            inv2 = {u: _split2(inv[u]) for u in units}
            yield
        akv = {u: _mm(_split2(a_ak[u]), vbd[u]) for u in units}
        arbk = {u: _split2(jnp.concatenate([a_rb[u], a_rk[u]], axis=1)) for u in units}
        new.update(ar=ar, inv=inv2, akv=akv, arbk=arbk, vbd=vbd, v=v, bk=bk_end, pend=p_end)

    def serial():
        pair = lambda ref, n: (ref[0, n], ref[1, n])
        s_cur = [state[g] for g in groups] if chained else None
        for j in range(n_chunks):
            ns = [uid[j, g] for g in groups]
            s_prev = s_cur if chained else [expand(s0_ref[j, g]) for g in groups]
            x0 = [_mm(pair(sv_ar, n), _split2(s_prev[g]), NT) for g, n in zip(groups, ns)]
            yield
            uu = [_mm(pair(sv_inv, n), bd2(x0[g][0:c] + sv_akv[n])) for g, n in zip(groups, ns)]
            yield
            for g, n in zip(groups, ns):
                yy = x0[g][c:2 * c] + _mm(pair(sv_arbk, n), _cat2(bd2(uu[g]), pair(sv_vbd, n), 0))
                y_ref[j * c:(j + 1) * c, g * GROUP_LANES:(g + 1) * GROUP_LANES] = yy
            s_new = []
            for g, n in zip(groups, ns):
                uv_t = jnp.transpose(jnp.concatenate([uu[g], sv_v[n]], axis=0))
                upd = _mm(_split2(uv_t), pair(sv_bk, n))
                s_new.append(s_prev[g] * sv_pend[n, 0:1, :] + stmask * upd)
            yield
            if chained:
                s_cur = s_new
            else:
                for g in groups:
                    sout_ref[j, g] = compact(s_new[g])
        if chained:
            for g in groups:
                state[g] = s_cur[g]
        new['last_state'] = s_new

    _interleave(serial(), prepare())

    zero = sum(_zero_after(s[0:8, 0:LANES]) for s in new['last_state'])[0:1, 0:1]
    zero_bf = zero.astype(BF16)
    for u, n in uid.items():
        for name, ref in (('ar', sv_ar), ('inv', sv_inv), ('arbk', sv_arbk), ('vbd', sv_vbd), ('bk', sv_bk)):
            for half in range(2):
                ref[half, n] = new[name][u][half] + zero_bf
        sv_akv[n] = new['akv'][u] + zero
        sv_v[n] = new['v'][u] + zero
        sv_pend[n] = jnp.broadcast_to(new['pend'][u] + zero, sv_pend.shape[1:])

    if chained:
        @pl.when(i == n_steps)
        def _():
            for g in groups:
                sout_ref[0, g] = compact(state[g])


def _wkv_scan(rlkvab, s0, *, row0, n_rows, chunk, chunks_per_step, chained):
    rows = chunk * chunks_per_step
    n_steps = n_rows // rows
    assert n_rows % rows == 0 and row0 % rows == 0
    masks = _wkv_masks(chunk)
    hc = HEADS_PER_GROUP * chunk
    n_units = chunks_per_step * N_HEAD_GROUPS
    st = (N_HEAD_GROUPS, GROUP_LANES, RWKV_HEAD)
    prev = lambda i: jnp.maximum(i - 1, 0)
    row_in = pl.BlockSpec((rows, RWKV_DIM), lambda i: (row0 // rows + jnp.minimum(i, n_steps - 1), 0))
    if chained:
        n_state = 1
        st_spec = pl.BlockSpec((1,) + st, lambda i: (0, 0, 0, 0))
    else:
        n_state = n_rows // chunk
        st_spec = pl.BlockSpec((chunks_per_step,) + st, lambda i: (prev(i), 0, 0, 0))
    in_specs = [row_in] * 6 + [st_spec] + [_const_spec(m.shape) for m in masks]
    out_shape = [jax.ShapeDtypeStruct((n_rows, RWKV_DIM), F32),
                 jax.ShapeDtypeStruct((n_state,) + st, F32)]
    out_specs = [pl.BlockSpec((rows, RWKV_DIM), lambda i: (prev(i), 0)), st_spec]
    scratch = [pltpu.VMEM((N_HEAD_GROUPS, GROUP_LANES, GROUP_LANES), F32),
               pltpu.VMEM((2, n_units, 2 * chunk, GROUP_LANES), BF16),
               pltpu.VMEM((2, n_units, chunk, hc), BF16),
               pltpu.VMEM((n_units, chunk, GROUP_LANES), F32),
               pltpu.VMEM((2, n_units, chunk, 2 * hc), BF16),
               pltpu.VMEM((2, n_units, hc, GROUP_LANES), BF16),
               pltpu.VMEM((n_units, chunk, GROUP_LANES), F32),
               pltpu.VMEM((2, n_units, 2 * chunk, GROUP_LANES), BF16),
               pltpu.VMEM((n_units, 8, GROUP_LANES), F32)]
    kern = functools.partial(_wkv_pipe_kernel, chunk=chunk, n_steps=n_steps, chained=chained)
    return pl.pallas_call(
        kern, out_shape=out_shape, grid=(n_steps + 1,), in_specs=in_specs, out_specs=out_specs,
        scratch_shapes=scratch,
        compiler_params=pltpu.CompilerParams(dimension_semantics=("arbitrary",),
                                             vmem_limit_bytes=VMEM_LIMIT),
        name="wkv_scan")(*rlkvab, s0, *masks)


def _mixer_post_kernel(yp_ref, ys_ref, bonus_ref, g_ref, yag_ref, sgb_ref, xp_ref, xs_ref, lng_ref, lnb_ref,
                       wob_ref, wo_ref, n2_ref, wr_ref, br_ref, ones_ref, tril_ref,
                       x1_ref, h2_ref, route_ref, counts_ref, cnt, *, n_prompt_tiles):
    i = pl.program_id(0)

    @pl.when(i == 0)
    def _():
        cnt[...] = jnp.zeros(cnt.shape, F32)

    tm = x1_ref.shape[0]
    n_parts = tm // ROW_TILE
    rp = tm // n_parts
    is_prompt = i < n_prompt_tiles
    ones = ones_ref[...]
    neg = jnp.float32(-jnp.inf)
    big = jnp.int32(LANES)
    lane = lax.broadcasted_iota(jnp.int32, (rp, LANES), 1)
    picked = {}

    def part(q):
        rs = slice(q * rp, (q + 1) * rp)
        y = jnp.where(is_prompt, yp_ref[rs, :], ys_ref[rs, :])
        inv_n = 1.0 / RWKV_HEAD
        mean = _split2_dot(y, ones) * inv_n
        yc = y - mean
        var = _split2_dot(yc * yc, ones) * inv_n
        yn = yc * lax.rsqrt(var + GN_EPS) * lng_ref[...] + lnb_ref[...]
        yy = (yn + bonus_ref[rs, :].astype(F32)) * g_ref[rs, :].astype(F32)
        yield
        y_b = jnp.dot(yy.astype(BF16), wob_ref[...], preferred_element_type=F32)
        merged = yag_ref[rs, :].astype(F32) + sgb_ref[rs, :].astype(F32) * y_b
        yield
        x = jnp.where(is_prompt, xp_ref[rs, :], xs_ref[rs, :])
        x1 = x + jnp.dot(merged.astype(BF16), wo_ref[...], preferred_element_type=F32)
        x1_ref[rs, :] = x1
        h2 = _rms(x1, n2_ref[...])
        h2_ref[rs, :] = _pack_bf16_pairs(h2)
        yield
        logits = _dot_stack3(h2, wr_ref[...]) + br_ref[...]
        yield
        is_g = lane < N_GROUPS
        lgp = jnp.where(is_g, logits, neg)
        m_g = jnp.max(lgp, axis=-1, keepdims=True)
        grp = jnp.min(jnp.where(lgp == m_g, lane, big), axis=-1, keepdims=True)
        p_top = 1.0 / jnp.sum(jnp.where(is_g, jnp.exp(logits - m_g), 0.0), axis=-1, keepdims=True)
        e_lane = lane - N_GROUPS
        in_grp = (e_lane >= grp * EXPERTS_PER_GROUP) & (e_lane < (grp + 1) * EXPERTS_PER_GROUP)
        le = jnp.where(in_grp, logits, neg)
        m1 = jnp.max(le, axis=-1, keepdims=True)
        i1 = jnp.min(jnp.where(le == m1, lane, big), axis=-1, keepdims=True)
        le2 = jnp.where(lane == i1, neg, le)
        m2 = jnp.max(le2, axis=-1, keepdims=True)
        i2 = jnp.min(jnp.where(le2 == m2, lane, big), axis=-1, keepdims=True)
        ex = jnp.exp(m2 - m1)
        picked[q] = (i1 - N_GROUPS, i2 - N_GROUPS, p_top / (1.0 + ex), p_top * ex / (1.0 + ex))

    gens = [part(q) for q in range(n_parts)]
    next(gens[0])
    _interleave(*gens)

    hot = [[lane == e for e in picked[q][0:2]] for q in range(n_parts)]
    both = jnp.concatenate([jnp.where(h1, 1.0, jnp.where(h2_, 1.0, 0.0)) for h1, h2_ in hot], axis=0)
    before = jnp.dot(tril_ref[...], both.astype(BF16), preferred_element_type=F32) + cnt[0:1, :]
    cnt[0:1, :] = cnt[0:1, :] + jnp.sum(both, axis=0, keepdims=True)
    counts_ref[...] = jnp.broadcast_to(cnt[0:1, :], counts_ref.shape)
    for q in range(n_parts):
        rs = slice(q * rp, (q + 1) * rp)
        e1, e2, w1, w2 = picked[q]
        ranks = [jnp.sum(jnp.where(h, before[rs, :], 0.0), axis=-1, keepdims=True) for h in hot[q]]
        route = jnp.zeros((rp, LANES), F32)
        for c, col in enumerate([e1.astype(F32), e2.astype(F32), w1, w2] + ranks):
            route = jnp.where(lane == c, col, route)
        route_ref[rs, :] = route


ROUTE_EXPERT, ROUTE_WEIGHT, ROUTE_RANK = 0, 2, 4


def _mixer_post(y_p, y_s, bonus, g, yag, sgb, x_p, x_s, p, *, tm):
    n_p, d = x_p.shape
    n_tok = n_p + x_s.shape[0]
    npt = n_p // tm
    row = lambda w: pl.BlockSpec((tm, w), lambda i: (i, 0))
    tril = jnp.asarray(np.arange(tm)[None, :] < np.arange(tm)[:, None], BF16)
    consts = [p['lnx_g'], p['lnx_b'], p['w_out_b'], p['w_o'], p['norm2_g'], p['w_router'],
              p['b_router'], p['ones_bf16'], tril]
    pair = lambda w: [pl.BlockSpec((tm, w), lambda i: (jnp.minimum(i, npt - 1), 0)),
                      pl.BlockSpec((tm, w), lambda i: (jnp.maximum(i - npt, 0), 0))]
    in_specs = (pair(RWKV_DIM) + [row(RWKV_DIM)] * 2 + [row(d)] * 2 + pair(d)
                + [_const_spec(c.shape) for c in consts])
    out_shape = [jax.ShapeDtypeStruct((n_tok, d), F32), jax.ShapeDtypeStruct((n_tok, d // 2), jnp.uint32),
                 jax.ShapeDtypeStruct((n_tok, LANES), F32), jax.ShapeDtypeStruct((8, LANES), F32)]
    out_specs = [row(d), row(d // 2), row(LANES), _const_spec((8, LANES))]
    return pl.pallas_call(
        functools.partial(_mixer_post_kernel, n_prompt_tiles=npt), out_shape=out_shape,
        grid=(n_tok // tm,), in_specs=in_specs, out_specs=out_specs,
        scratch_shapes=[pltpu.VMEM((8, LANES), F32)],
        compiler_params=pltpu.CompilerParams(dimension_semantics=("arbitrary",),
                                             vmem_limit_bytes=VMEM_LIMIT),
        name="mixer_post")(y_p, y_s, bonus, g, yag, sgb, x_p, x_s, *consts)


N_DMA_PRIORITIES = 2


def _gather_rows(idx_ref, src_hbm, dst, sem, n_rows, *, unrolled):
    def start(r, priority):
        pltpu.make_async_copy(src_hbm.at[pl.ds(idx_ref[r], 1)], dst.at[pl.ds(r, 1)],
                              sem).start(priority=priority)
    if unrolled:
        for r in range(n_rows):
            start(r, r % N_DMA_PRIORITIES)
    else:
        def body(r, carry):
            start(r, 0)
            return carry
        lax.fori_loop(0, n_rows, body, 0)


def _wait_rows(src_hbm, dst, sem, n_rows):
    pltpu.make_async_copy(src_hbm.at[pl.ds(0, n_rows)], dst, sem).wait()


def _moe_scatter_kernel(dest_ref, h_ref, xs_hbm, zeros, sem):
    n = dest_ref.shape[-1]
    tm = h_ref.shape[0]

    @pl.when(pl.program_id(0) == 0)
    def _():
        zeros[...] = jnp.zeros(zeros.shape, zeros.dtype)
        fills = [pltpu.make_async_copy(zeros, xs_hbm.at[pl.ds(r0, tm)], sem)
                 for r0 in range(0, xs_hbm.shape[0], tm)]
        for f, fill in enumerate(fills):
            fill.start(priority=f % N_DMA_PRIORITIES)
        for fill in fills:
            fill.wait()

    for r in range(n):
        pltpu.make_async_copy(h_ref.at[pl.ds(r % tm, 1)], xs_hbm.at[pl.ds(dest_ref[0, 0, r], 1)],
                              sem).start(priority=r % N_DMA_PRIORITIES)
    for j in range(n // tm):
        pltpu.make_async_copy(h_ref, xs_hbm.at[pl.ds(0, tm)], sem).wait()


def _moe_scatter(h2, pos3, n_rows):
    n_tok, d = h2.shape
    n_tiles, _, n = pos3.shape
    tm = n // TOP_K
    assert n_rows % tm == 0
    return pl.pallas_call(
        _moe_scatter_kernel, out_shape=jax.ShapeDtypeStruct((n_rows, d), h2.dtype), grid=(n_tiles,),
        in_specs=[pl.BlockSpec((1, 1, n), lambda i: (i, 0, 0), memory_space=pltpu.SMEM),
                  pl.BlockSpec((tm, d), lambda i: (i, 0))],
        out_specs=pl.BlockSpec(memory_space=pl.ANY),
        scratch_shapes=[pltpu.VMEM((tm, d), h2.dtype), pltpu.SemaphoreType.DMA(())],
        compiler_params=pltpu.CompilerParams(dimension_semantics=("arbitrary",),
                                             vmem_limit_bytes=VMEM_LIMIT),
        name="moe_scatter")(pos3, h2)


def _moe_experts_kernel(bexp_ref, nused_ref, x_ref, wg_ref, wu_ref, wd_ref, yb_ref, wg_bf, wu_bf, wd_bf):
    i = pl.program_id(0)

    @pl.when((i == 0) | (bexp_ref[i] != bexp_ref[jnp.maximum(i - 1, 0)]))
    def _():
        wg_bf[...] = wg_ref[0].astype(BF16)
        wu_bf[...] = wu_ref[0].astype(BF16)
        wd_bf[...] = wd_ref[0].astype(BF16)

    @pl.when(i < nused_ref[0])
    def _():
        def part(rs):
            xb = _unpack_bf16_pairs(x_ref[rs, :]).astype(BF16)
            hg = jnp.dot(xb, wg_bf[...], preferred_element_type=F32)
            yield
            hu = jnp.dot(xb, wu_bf[...], preferred_element_type=F32)
            hid = (hg * jax.nn.sigmoid(hg)) * hu
            yield
            yb_ref[rs, :] = _pack_bf16_pairs(jnp.dot(hid.astype(BF16), wd_bf[...], preferred_element_type=F32))

        rows = x_ref.shape[0]
        _interleave(*[part(slice(r0, r0 + ROW_TILE)) for r0 in range(0, rows, ROW_TILE)])

    @pl.when(i >= nused_ref[0])
    def _():
        yb_ref[...] = jnp.zeros(yb_ref.shape, yb_ref.dtype)


def _moe_experts(xs, block_expert, n_used, w_eg, w_eu, w_ed, *, blk):
    n_rows, dp = xs.shape
    n_blocks = n_rows // blk
    d, de = w_eg.shape[1:]
    grid_spec = pltpu.PrefetchScalarGridSpec(
        num_scalar_prefetch=2, grid=(n_blocks,),
        in_specs=[
            pl.BlockSpec((blk, dp), lambda i, be, nu: (jnp.minimum(i, nu[0] - 1), 0)),
            pl.BlockSpec((1, d, de), lambda i, be, nu: (be[i], 0, 0)),
            pl.BlockSpec((1, d, de), lambda i, be, nu: (be[i], 0, 0)),
            pl.BlockSpec((1, de, d), lambda i, be, nu: (be[i], 0, 0)),
        ],
        out_specs=pl.BlockSpec((blk, dp), lambda i, be, nu: (i, 0)),
        scratch_shapes=[pltpu.VMEM((d, de), BF16), pltpu.VMEM((d, de), BF16), pltpu.VMEM((de, d), BF16)])
    return pl.pallas_call(
        _moe_experts_kernel, out_shape=jax.ShapeDtypeStruct((n_rows, dp), xs.dtype), grid_spec=grid_spec,
        compiler_params=pltpu.CompilerParams(dimension_semantics=("arbitrary",),
                                             vmem_limit_bytes=VMEM_LIMIT),
        name="moe_experts")(block_expert, n_used, xs, w_eg, w_eu, w_ed)


GATHER_AHEAD = 2


def _with_dummy_blocks(idx, n_blocks, blk):
    pad = jnp.zeros((GATHER_AHEAD * blk,), idx.dtype)
    return jnp.concatenate([idx, pad]).reshape(n_blocks + GATHER_AHEAD, 1, blk)


def _moe_combine_kernel(*refs, n_prompt_tiles):
    pos_refs = refs[:GATHER_AHEAD + 1]
    x1_ref, route_ref, nf_ref, yb_hbm, outp_ref, outs_ref, ybuf, sems = refs[GATHER_AHEAD + 1:]
    i = pl.program_id(0)
    n_slots = GATHER_AHEAD + 1
    rows = ybuf.shape[1]
    slot = i % n_slots

    @pl.when(i == 0)
    def _():
        for a in range(GATHER_AHEAD):
            _gather_rows(pos_refs[a].at[0, 0], yb_hbm, ybuf.at[a], sems.at[a], rows, unrolled=False)

    _wait_rows(yb_hbm, ybuf.at[slot], sems.at[slot], rows)
    ahead = (i + GATHER_AHEAD) % n_slots
    _gather_rows(pos_refs[GATHER_AHEAD].at[0, 0], yb_hbm, ybuf.at[ahead], sems.at[ahead], rows, unrolled=True)
    tm = x1_ref.shape[0]
    route = route_ref[...]
    x2 = x1_ref[...]
    for j in range(TOP_K):
        yj = _unpack_bf16_pairs(ybuf[slot, j * tm:(j + 1) * tm, :])
        x2 = x2 + yj * route[:, ROUTE_WEIGHT + j:ROUTE_WEIGHT + j + 1]
    out = _rms(x2, nf_ref[...])

    @pl.when(i < n_prompt_tiles)
    def _():
        outp_ref[...] = out

    @pl.when(i >= n_prompt_tiles)
    def _():
        outs_ref[...] = out

    @pl.when(i == pl.num_programs(0) - 1)
    def _():
        for a in range(1, n_slots):
            s = (i + a) % n_slots
            _wait_rows(yb_hbm, ybuf.at[s], sems.at[s], rows)


def _moe_combine(pos_tiles, x1, route, normf_g, yb, *, n_prompt_rows, tm):
    n_tok, d = x1.shape
    n_tiles = n_tok // tm
    npt = n_prompt_rows // tm
    pos3 = _with_dummy_blocks(pos_tiles.reshape(-1), n_tiles, TOP_K * tm)
    pos_spec = lambda a: pl.BlockSpec((1, 1, TOP_K * tm), lambda i: (i + a, 0, 0), memory_space=pltpu.SMEM)
    in_specs = [pos_spec(a) for a in range(GATHER_AHEAD + 1)] + [
        pl.BlockSpec((tm, d), lambda i: (i, 0)),
        pl.BlockSpec((tm, LANES), lambda i: (i, 0)),
        _const_spec(normf_g.shape),
        pl.BlockSpec(memory_space=pl.ANY),
    ]
    out_shape = [jax.ShapeDtypeStruct((n_prompt_rows, d), F32),
                 jax.ShapeDtypeStruct((n_tok - n_prompt_rows, d), F32)]
    out_specs = [pl.BlockSpec((tm, d), lambda i: (jnp.minimum(i, npt - 1), 0)),
                 pl.BlockSpec((tm, d), lambda i: (jnp.maximum(i - npt, 0), 0))]
    return pl.pallas_call(
        functools.partial(_moe_combine_kernel, n_prompt_tiles=npt), out_shape=out_shape,
        grid=(n_tiles,), in_specs=in_specs, out_specs=out_specs,
        scratch_shapes=[pltpu.VMEM((GATHER_AHEAD + 1, TOP_K * tm, yb.shape[1]), yb.dtype),
                        pltpu.SemaphoreType.DMA((GATHER_AHEAD + 1,))],
        compiler_params=pltpu.CompilerParams(dimension_semantics=("arbitrary",),
                                             vmem_limit_bytes=VMEM_LIMIT),
        name="moe_combine")(*([pos3] * (GATHER_AHEAD + 1)), x1, route, normf_g, yb)


def _dispatch(route, counts, blk, tm):
    n_tok = route.shape[0]
    n_assign = n_tok * TOP_K
    expert = route[:, ROUTE_EXPERT:ROUTE_EXPERT + TOP_K].astype(jnp.int32)
    rank = route[:, ROUTE_RANK:ROUTE_RANK + TOP_K].astype(jnp.int32)
    counts = counts[0, :N_EXPERTS].astype(jnp.int32)
    padded = (counts + blk - 1) // blk * blk
    pad_end = jnp.cumsum(padded)
    pad_start = pad_end - padded
    is_e = expert[:, :, None] == jnp.arange(N_EXPERTS, dtype=jnp.int32)
    dest = jnp.sum(jnp.where(is_e, pad_start, 0), axis=-1) + rank
    n_blocks = -(-n_assign // blk) + N_EXPERTS
    block_start = jnp.arange(n_blocks, dtype=jnp.int32) * blk
    block_expert = jnp.minimum(jnp.sum(block_start[:, None] >= pad_end[None, :], axis=1),
                               N_EXPERTS - 1).astype(jnp.int32)
    n_used = (pad_end[-1] // blk).reshape(1)
    pos_tiles = [dest.reshape(n_tok // t, t, TOP_K).transpose(0, 2, 1).reshape(n_tok // t, 1, TOP_K * t)
                 for t in tm]
    return pos_tiles, block_expert, n_used, n_blocks


def _state_to_kernel(s):
    return s.reshape(s.shape[0], N_HEAD_GROUPS, GROUP_LANES, RWKV_HEAD)


def _state_from_kernel(s):
    return s.reshape(s.shape[0], RWKV_HEADS, RWKV_HEAD, RWKV_HEAD)


def kernel(x_prompt, x_sample, state_conv, state_shift, state_wkv, norm1_g, w_in, conv_w, mu_shift, w0, w_lora_w, a0, w_lora_a, w_lora_g, k_k, k_a, r_k, lnx_g, lnx_b, w_out_a, w_out_b, w_o, norm2_g, w_router_group, b_router_group, w_router_expert, b_router_expert, w_e_gate, w_e_up, w_e_down, normf_g):
    depth = norm1_g.shape[0]
    bp, seq, d = x_prompt.shape
    db, dseq, _ = x_sample.shape
    assert depth == 1 and bp == 1, "single layer, single prompt stream"
    tm = ROW_TILE
    n_p, n_s = bp * seq, db * dseq
    assert n_p % tm == 0 and n_s % tm == 0 and tm % dseq == 0
    n_prompt_tiles, n_sample_tiles = n_p // tm, n_s // tm
    seqs = tm // dseq
    n_tok = n_p + n_s

    x_p = x_prompt.reshape(n_p, d)
    x_s = x_sample.reshape(n_s, d)

    l = 0
    c3 = 3 * CONV_DIM
    head_id = np.arange(GROUP_LANES) // RWKV_HEAD
    ones_bf16 = jnp.asarray(head_id[:, None] == head_id[None, :], BF16)
    zpad = jnp.zeros((LORA_W, RWKV_DIM), F32)
    n_r = N_GROUPS + N_EXPERTS
    p = {
        'norm1_g': norm1_g[l].reshape(1, d),
        'w_in_a': w_in[l][:, :c3].astype(BF16),
        'w_in_b': w_in[l][:, c3:c3 + SHIFT_DIM].astype(BF16),
        'w_in_g': w_in[l][:, c3 + SHIFT_DIM:].astype(BF16),
        'conv_w': conv_w[l],
        'mu_shift': mu_shift[l].reshape(1, SHIFT_DIM),
        'w0': w0[l].reshape(1, RWKV_DIM),
        'w_lora_w': _stack3(jnp.concatenate([w_lora_w[l], zpad], axis=0)),
        'a0': a0[l].reshape(1, RWKV_DIM),
        'w_lora_a': _stack3(jnp.concatenate([zpad, w_lora_a[l]], axis=0)),
        'w_lora_g': _stack3(w_lora_g[l]),
        'k_k': k_k[l].reshape(1, RWKV_DIM),
        'k_a': k_a[l].reshape(1, RWKV_DIM),
        'r_k': r_k[l].reshape(1, RWKV_DIM),
        'w_out_a': w_out_a[l].astype(BF16),
        'ones_bf16': ones_bf16,
        'lnx_g': lnx_g[l].reshape(1, RWKV_DIM),
        'lnx_b': lnx_b[l].reshape(1, RWKV_DIM),
        'w_out_b': w_out_b[l].astype(BF16),
        'w_o': w_o[l].astype(BF16),
        'norm2_g': norm2_g[l].reshape(1, d),
        'w_router': _stack3(jnp.pad(jnp.concatenate([w_router_group[l], w_router_expert[l]], axis=1),
                                    ((0, 0), (0, LANES - n_r)))),
        'b_router': jnp.pad(jnp.concatenate([b_router_group[l], b_router_expert[l]]),
                            (0, LANES - n_r)).reshape(1, LANES),
    }
    st_conv_t = state_conv[l].reshape(n_sample_tiles, seqs * (CONV_WIDTH - 1), CONV_DIM)
    st_shift_t = state_shift[l].reshape(n_sample_tiles, seqs, SHIFT_DIM)

    (yag, sgb, r, w, k, v, a, b, bonus, g, ctail, stail) = _mixer_pre(
        x_p, x_s, st_conv_t, st_shift_t, p, seq_len=dseq, tm=tm)

    s0_prompt = jnp.zeros((1, N_HEAD_GROUPS, GROUP_LANES, RWKV_HEAD), F32)
    s0_sample = _state_to_kernel(state_wkv[l])
    rlkvab = (r, w, k, v, a, b)
    cp = min(SCAN_CHUNK, n_p)
    assert cp & (cp - 1) == 0 and dseq & (dseq - 1) == 0
    y_p, s_p = _wkv_scan(rlkvab, s0_prompt, row0=0, n_rows=n_p, chunk=cp,
                         chunks_per_step=SCAN_CHUNKS_PER_STEP, chained=True)
    y_s, s_s = _wkv_scan(rlkvab, s0_sample, row0=n_p, n_rows=n_s, chunk=dseq,
                         chunks_per_step=SCAN_CHUNKS_PER_STEP, chained=False)

    post_tm = POST_TILES * tm
    assert n_p % post_tm == 0 and n_s % post_tm == 0
    x1, h2, route, counts = _mixer_post(y_p, y_s, bonus, g, yag, sgb, x_p, x_s, p, tm=post_tm)

    blk = MOE_ROWS
    (pos_scatter, pos_combine), block_expert, n_used, n_blocks = _dispatch(route, counts, blk, (post_tm, tm))
    xs = _moe_scatter(h2, pos_scatter, n_blocks * blk)
    yb = _moe_experts(xs, block_expert, n_used, w_e_gate[l], w_e_up[l], w_e_down[l], blk=blk)
    out_p, out_s = _moe_combine(pos_combine, x1, route, normf_g.reshape(1, d), yb, n_prompt_rows=n_p, tm=tm)

    y_prompt = out_p.reshape(bp, seq, d)
    y_sample = out_s.reshape(db, dseq, d)
    conv_p = ctail[n_prompt_tiles - 1, 2 * (seqs - 1):2 * seqs].reshape(1, bp, CONV_WIDTH - 1, CONV_DIM)
    shift_p = stail[n_prompt_tiles - 1, seqs - 1].reshape(1, bp, 1, SHIFT_DIM)
    wkv_p = _state_from_kernel(s_p).reshape(1, bp, RWKV_HEADS, RWKV_HEAD, RWKV_HEAD)
    conv_s = ctail[n_prompt_tiles:].reshape(1, db, CONV_WIDTH - 1, CONV_DIM)
    shift_s = stail[n_prompt_tiles:].reshape(1, db, 1, SHIFT_DIM)
    wkv_s = _state_from_kernel(s_s).reshape(1, db, RWKV_HEADS, RWKV_HEAD, RWKV_HEAD)
    return (y_prompt, y_sample, conv_p, shift_p, wkv_p, conv_s, shift_s, wkv_s)
```

```python
import functools

import numpy as np
import jax
import jax.numpy as jnp
from jax import lax
from jax.experimental import pallas as pl
from jax.experimental.pallas import tpu as pltpu

F32 = jnp.float32
BF16 = jnp.bfloat16

CONV_DIM = 512
CONV_WIDTH = 3
RWKV_HEAD = 64
RWKV_HEADS = 8
RWKV_DIM = RWKV_HEADS * RWKV_HEAD
LORA_W = 64
LORA_A = 64
LORA_G = 128
SHIFT_DIM = 3 * RWKV_DIM + LORA_W + LORA_A + LORA_G
N_GROUPS = 4
EXPERTS_PER_GROUP = 8
N_EXPERTS = N_GROUPS * EXPERTS_PER_GROUP
TOP_K = 2
RMS_EPS = 1e-6
GN_EPS = 64e-5

LANES = 128
ROW_TILE = 256
POST_TILES = 2
SCAN_CHUNK = 64
SCAN_CHUNKS_PER_STEP = 4
MOE_ROWS = 512
HIST = 8
VMEM_LIMIT = 56 * 1024 * 1024


def _rms(x, g):
    return x * lax.rsqrt(jnp.mean(x * x, axis=-1, keepdims=True) + RMS_EPS) * g


def _split2_dot(x, ones_bf16):
    w = ones_bf16.shape[0]
    hi = x.astype(BF16)
    lo = (x - hi.astype(F32)).astype(BF16)
    parts = []
    for c0 in range(0, x.shape[1], w):
        both = jnp.concatenate([hi[:, c0:c0 + w], lo[:, c0:c0 + w]], axis=0)
        s = jnp.dot(both, ones_bf16, preferred_element_type=F32)
        parts.append(s[0:x.shape[0]] + s[x.shape[0]:])
    return jnp.concatenate(parts, axis=1)


def _pack_bf16_pairs(x):
    n = x.shape[1] // 2
    lo = pltpu.bitcast(x[:, :n].astype(BF16).astype(F32), jnp.uint32)
    hi = pltpu.bitcast(x[:, n:].astype(BF16).astype(F32), jnp.uint32)
    return hi | (lo >> 16)


def _unpack_bf16_pairs(p):
    lo = pltpu.bitcast(p << 16, F32)
    hi = pltpu.bitcast(p & jnp.uint32(0xFFFF0000), F32)
    return jnp.concatenate([lo, hi], axis=1)


def _stack3(w):
    hi = w.astype(BF16)
    lo = (w - hi.astype(F32)).astype(BF16)
    return jnp.concatenate([hi, hi, lo], axis=0)


def _dot_stack3(x, w_stack):
    hi = x.astype(BF16)
    lo = (x - hi.astype(F32)).astype(BF16)
    return jnp.dot(jnp.concatenate([hi, lo, hi], axis=1), w_stack, preferred_element_type=F32)


def _const_spec(shape):
    nd = len(shape)
    return pl.BlockSpec(shape, lambda *_: (0,) * nd)


def _interleave(*gens):
    live = list(gens)
    while live:
        for gen in list(live):
            try:
                next(gen)
            except StopIteration:
                live.remove(gen)


def _zero_after(x):
    return pltpu.bitcast(lax.shift_right_logical(pltpu.bitcast(x, jnp.uint32), jnp.uint32(32)), F32)


def _mixer_pre_tile(x_ref, stc_ref, sts_ref, n1_ref, wa_ref, wb_ref, wg_ref, convw_ref, mu_ref,
                    w0_ref, lw_ref, a0_ref, la_ref, lgw_ref, kk_ref, ka_ref, rk_ref, woa_ref, ones_ref,
                    yag_ref, sgb_ref, r_ref, w_ref, k_ref, v_ref, a_ref, b_ref, bonus_ref, g_ref,
                    ctail_ref, stail_ref, ccarry, scarry, *, sample, seq_len):
    tm = x_ref.shape[0]
    seqs = tm // seq_len
    h = _rms(x_ref[...], n1_ref[...]).astype(BF16)

    def prev_rows(val, k, heads):
        row = lax.broadcasted_iota(jnp.int32, val.shape, 0)
        out = pltpu.roll(val, k, axis=0)
        for r0, head in heads.items():
            out = jnp.where(row == r0, head, out)
        return out

    pb = jnp.dot(h, wb_ref[...], preferred_element_type=F32)
    old_s = scarry[...]
    if sample:
        s_heads = {j * seq_len: sts_ref[0, j:j + 1, :] for j in range(seqs)}
    else:
        s_heads = {0: old_s[HIST - 1:HIST, :]}
    prev = prev_rows(pb, 1, s_heads)
    for j in range(seqs):
        r1 = (j + 1) * seq_len
        stail_ref[0, j:j + 1, :] = pb[r1 - 1:r1, :]
    scarry[...] = pb[tm - HIST:tm, :] + _zero_after(old_s)

    def conv_and_gates():
        n_a = wa_ref.shape[1] // 3
        g_in = jnp.dot(h, wa_ref[:, 0:n_a], preferred_element_type=F32)
        yield
        g_out = jnp.dot(h, wa_ref[:, n_a:2 * n_a], preferred_element_type=F32)
        yield
        x_c = jnp.dot(h, wa_ref[:, 2 * n_a:3 * n_a], preferred_element_type=F32)
        bx = g_in * x_c
        old_c = ccarry[...]
        if sample:
            h1 = {j * seq_len: stc_ref[0, 2 * j + 1:2 * j + 2, :] for j in range(seqs)}
            h2 = {j * seq_len: stc_ref[0, 2 * j:2 * j + 1, :] for j in range(seqs)}
            h2.update({j * seq_len + 1: stc_ref[0, 2 * j + 1:2 * j + 2, :] for j in range(seqs)})
        else:
            h1 = {0: old_c[HIST - 1:HIST, :]}
            h2 = {0: old_c[HIST - 2:HIST - 1, :], 1: old_c[HIST - 1:HIST, :]}
        cw = convw_ref[...]
        conv = cw[0:1, :] * prev_rows(bx, 2, h2) + cw[1:2, :] * prev_rows(bx, 1, h1) + cw[2:3, :] * bx
        for j in range(seqs):
            r1 = (j + 1) * seq_len
            ctail_ref[0, 2 * j:2 * j + 2, :] = bx[r1 - 2:r1, :]
        ccarry[...] = bx[tm - HIST:tm, :] + _zero_after(old_c)
        yield
        y_a = jnp.dot((g_out * conv).astype(BF16), woa_ref[...], preferred_element_type=F32)
        yield
        d = wg_ref.shape[1] // 2
        half = d // 2
        for c0 in range(0, d, half):
            pg = jnp.dot(h, wg_ref[:, c0:c0 + half], preferred_element_type=F32)
            yag_ref[:, c0:c0 + half] = (jax.nn.sigmoid(pg) * y_a[:, c0:c0 + half]).astype(yag_ref.dtype)
            yield
        for c0 in range(0, d, half):
            pg = jnp.dot(h, wg_ref[:, d + c0:d + c0 + half], preferred_element_type=F32)
            sgb_ref[:, c0:c0 + half] = jax.nn.sigmoid(pg).astype(sgb_ref.dtype)
            yield

    def rwkv_pre():
        s = pb + (prev - pb) * mu_ref[...]
        o1, o2, o3 = RWKV_DIM, 2 * RWKV_DIM, 3 * RWKV_DIM
        r = s[:, 0:o1]
        k = s[:, o1:o2]
        v = s[:, o2:o3]
        s_l = s[:, o3:o3 + LORA_W + LORA_A]
        lg = s[:, o3 + LORA_W + LORA_A:]
        r_ref[...] = r
        v_ref[...] = v
        yield
        z = w0_ref[...] + _dot_stack3(jnp.tanh(s_l), lw_ref[...])
        w_log = -jax.nn.softplus(-z) - 0.5
        w_ref[...] = -jnp.exp(w_log)
        yield
        a = jax.nn.sigmoid(a0_ref[...] + _dot_stack3(s_l, la_ref[...]))
        yield
        g_ref[...] = _dot_stack3(jax.nn.sigmoid(lg), lgw_ref[...]).astype(g_ref.dtype)
        yield
        ones = ones_ref[...]
        kk = k * kk_ref[...]
        kk_n = kk / jnp.maximum(jnp.sqrt(_split2_dot(kk * kk, ones)), 1e-12)
        a_ref[...] = -kk_n
        b_ref[...] = kk_n * a
        yield
        k2 = k * (1.0 + (a - 1.0) * ka_ref[...])
        k_ref[...] = k2
        yield
        bonus_ref[...] = (_split2_dot(r * k2 * rk_ref[...], ones) * v).astype(bonus_ref.dtype)

    _interleave(conv_and_gates(), rwkv_pre())


def _mixer_pre_kernel(xp_ref, xs_ref, *refs, n_prompt_tiles, seq_len):
    i = pl.program_id(0)
    ccarry, scarry = refs[-2:]

    @pl.when(i == 0)
    def _():
        ccarry[...] = jnp.zeros(ccarry.shape, F32)
        scarry[...] = jnp.zeros(scarry.shape, F32)

    pl.when(i < n_prompt_tiles)(
        functools.partial(_mixer_pre_tile, xp_ref, *refs, sample=False, seq_len=seq_len))
    pl.when(i >= n_prompt_tiles)(
        functools.partial(_mixer_pre_tile, xs_ref, *refs, sample=True, seq_len=seq_len))


def _mixer_pre(x_p, x_s, st_conv_t, st_shift_t, p, *, seq_len, tm):
    n_p, d = x_p.shape
    n_tok = n_p + x_s.shape[0]
    n_prompt_tiles = n_p // tm
    n_tiles = n_tok // tm
    seqs = tm // seq_len
    row = lambda w: pl.BlockSpec((tm, w), lambda i: (i, 0))
    st_idx = lambda i: (jnp.maximum(i - n_prompt_tiles, 0), 0, 0)
    consts = [p['norm1_g'], p['w_in_a'], p['w_in_b'], p['w_in_g'], p['conv_w'], p['mu_shift'],
              p['w0'], p['w_lora_w'], p['a0'], p['w_lora_a'], p['w_lora_g'], p['k_k'], p['k_a'],
              p['r_k'], p['w_out_a'], p['ones_bf16']]
    in_specs = [pl.BlockSpec((tm, d), lambda i: (jnp.minimum(i, n_prompt_tiles - 1), 0)),
                pl.BlockSpec((tm, d), lambda i: (jnp.maximum(i - n_prompt_tiles, 0), 0)),
                pl.BlockSpec((1, 2 * seqs, CONV_DIM), st_idx),
                pl.BlockSpec((1, seqs, SHIFT_DIM), st_idx)] + [_const_spec(c.shape) for c in consts]
    sds = lambda w, dt=F32: jax.ShapeDtypeStruct((n_tok, w), dt)
    out_shape = [sds(d, BF16), sds(d, BF16)] + [sds(RWKV_DIM)] * 6 + [sds(RWKV_DIM, BF16)] * 2 + [
        jax.ShapeDtypeStruct((n_tiles, 2 * seqs, CONV_DIM), F32),
        jax.ShapeDtypeStruct((n_tiles, seqs, SHIFT_DIM), F32)]
    out_specs = [row(d), row(d)] + [row(RWKV_DIM)] * 8 + [
        pl.BlockSpec((1, 2 * seqs, CONV_DIM), lambda i: (i, 0, 0)),
        pl.BlockSpec((1, seqs, SHIFT_DIM), lambda i: (i, 0, 0))]
    kern = functools.partial(_mixer_pre_kernel, n_prompt_tiles=n_prompt_tiles, seq_len=seq_len)
    return pl.pallas_call(
        kern, out_shape=out_shape, grid=(n_tiles,), in_specs=in_specs, out_specs=out_specs,
        scratch_shapes=[pltpu.VMEM((HIST, CONV_DIM), F32), pltpu.VMEM((HIST, SHIFT_DIM), F32)],
        compiler_params=pltpu.CompilerParams(dimension_semantics=("arbitrary",),
                                             vmem_limit_bytes=VMEM_LIMIT),
        name="mixer_pre")(x_p, x_s, st_conv_t, st_shift_t, *consts)


GROUP_LANES = 256
HEADS_PER_GROUP = GROUP_LANES // RWKV_HEAD
N_HEAD_GROUPS = RWKV_DIM // GROUP_LANES
NN = (((1,), (0,)), ((), ()))
NT = (((1,), (1,)), ((), ()))


def _split2(x):
    hi = x.astype(BF16)
    lo = (x - hi.astype(F32)).astype(BF16)
    return hi, lo


def _mm(xs, ys, dims=NN):
    x1, x2 = xs
    y1, y2 = ys
    d = lambda p, q: lax.dot_general(p, q, dims, preferred_element_type=F32)
    m = x1.shape[0]
    both = d(jnp.concatenate([x1, x2], axis=0), y1)
    return both[0:m] + both[m:2 * m] + d(x1, y2)


def _cat2(ps, qs, axis):
    return tuple(jnp.concatenate([p, q], axis=axis) for p, q in zip(ps, qs))


def _wkv_masks(c):
    hc = HEADS_PER_GROUP * c
    levels = c.bit_length() - 1
    t = np.arange(c)[:, None]
    s = (np.arange(hc) % c)[None, :]
    tm = [s < t, s <= t]
    for lvl in range(1, levels + 1):
        half = 1 << (lvl - 1)
        tm.append(((t >> lvl) == (s >> lvl)) & ((t & half) != 0) & ((s & half) == 0))
    row_head = (np.arange(hc) // c)[:, None]
    mfeat = row_head == (np.arange(GROUP_LANES) // RWKV_HEAD)[None, :]
    mpos = row_head == (np.arange(hc) // c)[None, :]
    lane_head = np.arange(GROUP_LANES) // RWKV_HEAD
    stmask = lane_head[:, None] == lane_head[None, :]
    tri = np.arange(c)[None, :] <= np.arange(c)[:, None]
    return (jnp.asarray(mfeat, BF16), jnp.asarray(mpos, BF16), jnp.asarray(np.stack(tm), F32),
            jnp.asarray(stmask, F32), jnp.asarray(tri, BF16))


def _wkv_pipe_kernel(r_ref, lw_ref, k_ref, v_ref, a_ref, b_ref, s0_ref, mfeat_ref, mpos_ref,
                     tmask_ref, stmask_ref, tri_ref, y_ref, sout_ref,
                     state, sv_ar, sv_inv, sv_akv, sv_arbk, sv_vbd, sv_v, sv_bk, sv_pend,
                     *, chunk, n_steps, chained):
    i = pl.program_id(0)
    c = chunk
    n_chunks = r_ref.shape[0] // c
    hc = HEADS_PER_GROUP * c
    levels = c.bit_length() - 1
    groups = range(N_HEAD_GROUPS)
    units = [(j, g) for j in range(n_chunks) for g in groups]
    uid = {u: n for n, u in enumerate(units)}

    stmask = stmask_ref[...]

    def expand(sc):
        pair = jnp.concatenate([sc, sc], axis=1)
        return jnp.concatenate([pair] * (GROUP_LANES // pair.shape[1]), axis=1) * stmask

    def compact(s):
        half = s[:, 0:GROUP_LANES // 2] + s[:, GROUP_LANES // 2:]
        return half[:, 0:RWKV_HEAD] + half[:, RWKV_HEAD:]

    @pl.when(i == 0)
    def _():
        for ref in (sv_ar, sv_inv, sv_akv, sv_arbk, sv_vbd, sv_v, sv_bk):
            ref[...] = jnp.zeros(ref.shape, ref.dtype)
        sv_pend[...] = jnp.ones(sv_pend.shape, F32)
        if chained:
            for g in groups:
                state[g] = expand(s0_ref[0, g])

    mfeat = mfeat_ref[...]
    mpos = mpos_ref[...]
    strict = tmask_ref[0]
    incl = tmask_ref[1]
    eye = incl - strict
    tri = tri_ref[...]

    def ld(ref, j, g):
        return ref[j * c:(j + 1) * c, g * GROUP_LANES:(g + 1) * GROUP_LANES]

    def bd_split(ps):
        mask = mpos if ps[0].shape[1] == hc else mfeat
        return tuple(jnp.concatenate([p] * HEADS_PER_GROUP, axis=0) * mask for p in ps)

    def bd2(m):
        return bd_split(_split2(m))

    def cumsum_rows(x):
        p1 = x.astype(BF16)
        r1 = x - p1.astype(F32)
        p2 = r1.astype(BF16)
        p3 = (r1 - p2.astype(F32)).astype(BF16)
        d = lambda q: jnp.dot(tri, q, preferred_element_type=F32)
        return d(p1) + d(p2) + d(p3)

    new = {}

    def prepare():
        cum = {u: cumsum_rows(ld(lw_ref, *u)) for u in units}
        yield
        ar, bk_end, p_end, a_ab, a_ak, a_rb, a_rk, v, vbd = ({} for _ in range(9))
        for u in units:
            cm = cum[u]
            cum_last = cm[c - 1:c, :]
            e_neg = jnp.exp(-cm)
            e_end = jnp.exp(cum_last - cm)
            b_raw = ld(b_ref, *u)
            k_raw = ld(k_ref, *u)
            ar[u] = _split2(jnp.concatenate([ld(a_ref, *u) * jnp.exp(cm - ld(lw_ref, *u)),
                                             ld(r_ref, *u) * jnp.exp(cm)], axis=0))
            bk_end[u] = _split2(jnp.concatenate([b_raw * e_end, k_raw * e_end], axis=0))
            p_end[u] = jnp.exp(cum_last)
            v[u] = ld(v_ref, *u)
            vbd[u] = bd2(v[u])
            gram = _mm(ar[u], _cat2(bd2(b_raw * e_neg), bd2(k_raw * e_neg), 0), NT)
            a_ab[u] = jnp.where(strict > 0, gram[0:c, 0:hc], 0.0)
            a_ak[u] = jnp.where(strict > 0, gram[0:c, hc:2 * hc], 0.0)
            a_rb[u] = jnp.where(incl > 0, gram[c:2 * c, 0:hc], 0.0)
            a_rk[u] = jnp.where(incl > 0, gram[c:2 * c, hc:2 * hc], 0.0)
            if uid[u] % 2 == 1:
                yield
        a_ab2 = {u: _split2(a_ab[u]) for u in units}
        inv = {u: eye + a_ab[u] * tmask_ref[2] for u in units}
        inv2 = {u: _split2(inv[u]) for u in units}
        for lvl in range(2, levels + 1):
            lm = tmask_ref[1 + lvl]
            t1 = {u: _mm(a_ab2[u], bd_split(inv2[u])) for u in units}
            yield
            inv = {u: inv[u] + lm * _mm(inv2[u], bd2(t1[u])) for u in units}
            inv2 = {u: _split2(inv[u]) for u in units}
            yield
        akv = {u: _mm(_split2(a_ak[u]), vbd[u]) for u in units}
        arbk = {u: _split2(jnp.concatenate([a_rb[u], a_rk[u]], axis=1)) for u in units}
        new.update(ar=ar, inv=inv2, akv=akv, arbk=arbk, vbd=vbd, v=v, bk=bk_end, pend=p_end)

    def serial():
        pair = lambda ref, n: (ref[0, n], ref[1, n])
        s_cur = [state[g] for g in groups] if chained else None
        for j in range(n_chunks):
            ns = [uid[j, g] for g in groups]
            s_prev = s_cur if chained else [expand(s0_ref[j, g]) for g in groups]
            x0 = [_mm(pair(sv_ar, n), _split2(s_prev[g]), NT) for g, n in zip(groups, ns)]
            yield
            uu = [_mm(pair(sv_inv, n), bd2(x0[g][0:c] + sv_akv[n])) for g, n in zip(groups, ns)]
            yield
            for g, n in zip(groups, ns):
                yy = x0[g][c:2 * c] + _mm(pair(sv_arbk, n), _cat2(bd2(uu[g]), pair(sv_vbd, n), 0))
                y_ref[j * c:(j + 1) * c, g * GROUP_LANES:(g + 1) * GROUP_LANES] = yy
            s_new = []
            for g, n in zip(groups, ns):
                uv_t = jnp.transpose(jnp.concatenate([uu[g], sv_v[n]], axis=0))
                upd = _mm(_split2(uv_t), pair(sv_bk, n))
                s_new.append(s_prev[g] * sv_pend[n, 0:1, :] + stmask * upd)
            yield
            if chained:
                s_cur = s_new
            else:
                for g in groups:
                    sout_ref[j, g] = compact(s_new[g])
        if chained:
            for g in groups:
                state[g] = s_cur[g]
        new['last_state'] = s_new

    _interleave(serial(), prepare())

    zero = sum(_zero_after(s[0:8, 0:LANES]) for s in new['last_state'])[0:1, 0:1]
    zero_bf = zero.astype(BF16)
    for u, n in uid.items():
        for name, ref in (('ar', sv_ar), ('inv', sv_inv), ('arbk', sv_arbk), ('vbd', sv_vbd), ('bk', sv_bk)):
            for half in range(2):
                ref[half, n] = new[name][u][half] + zero_bf
        sv_akv[n] = new['akv'][u] + zero
        sv_v[n] = new['v'][u] + zero
        sv_pend[n] = jnp.broadcast_to(new['pend'][u] + zero, sv_pend.shape[1:])

    if chained:
        @pl.when(i == n_steps)
        def _():
            for g in groups:
                sout_ref[0, g] = compact(state[g])


def _wkv_scan(rlkvab, s0, *, row0, n_rows, chunk, chunks_per_step, chained):
    rows = chunk * chunks_per_step
    n_steps = n_rows // rows
    assert n_rows % rows == 0 and row0 % rows == 0
    masks = _wkv_masks(chunk)
    hc = HEADS_PER_GROUP * chunk
    n_units = chunks_per_step * N_HEAD_GROUPS
    st = (N_HEAD_GROUPS, GROUP_LANES, RWKV_HEAD)
    prev = lambda i: jnp.maximum(i - 1, 0)
    row_in = pl.BlockSpec((rows, RWKV_DIM), lambda i: (row0 // rows + jnp.minimum(i, n_steps - 1), 0))
    if chained:
        n_state = 1
        st_spec = pl.BlockSpec((1,) + st, lambda i: (0, 0, 0, 0))
    else:
        n_state = n_rows // chunk
        st_spec = pl.BlockSpec((chunks_per_step,) + st, lambda i: (prev(i), 0, 0, 0))
    in_specs = [row_in] * 6 + [st_spec] + [_const_spec(m.shape) for m in masks]
    out_shape = [jax.ShapeDtypeStruct((n_rows, RWKV_DIM), F32),
                 jax.ShapeDtypeStruct((n_state,) + st, F32)]
    out_specs = [pl.BlockSpec((rows, RWKV_DIM), lambda i: (prev(i), 0)), st_spec]
    scratch = [pltpu.VMEM((N_HEAD_GROUPS, GROUP_LANES, GROUP_LANES), F32),
               pltpu.VMEM((2, n_units, 2 * chunk, GROUP_LANES), BF16),
               pltpu.VMEM((2, n_units, chunk, hc), BF16),
               pltpu.VMEM((n_units, chunk, GROUP_LANES), F32),
               pltpu.VMEM((2, n_units, chunk, 2 * hc), BF16),
               pltpu.VMEM((2, n_units, hc, GROUP_LANES), BF16),
               pltpu.VMEM((n_units, chunk, GROUP_LANES), F32),
               pltpu.VMEM((2, n_units, 2 * chunk, GROUP_LANES), BF16),
               pltpu.VMEM((n_units, 8, GROUP_LANES), F32)]
    kern = functools.partial(_wkv_pipe_kernel, chunk=chunk, n_steps=n_steps, chained=chained)
    return pl.pallas_call(
        kern, out_shape=out_shape, grid=(n_steps + 1,), in_specs=in_specs, out_specs=out_specs,
        scratch_shapes=scratch,
        compiler_params=pltpu.CompilerParams(dimension_semantics=("arbitrary",),
                                             vmem_limit_bytes=VMEM_LIMIT),
        name="wkv_scan")(*rlkvab, s0, *masks)


def _mixer_post_kernel(yp_ref, ys_ref, bonus_ref, g_ref, yag_ref, sgb_ref, xp_ref, xs_ref, lng_ref, lnb_ref,
                       wob_ref, wo_ref, n2_ref, wr_ref, br_ref, ones_ref, tril_ref,
                       x1_ref, h2_ref, route_ref, counts_ref, cnt, *, n_prompt_tiles):
    i = pl.program_id(0)

    @pl.when(i == 0)
    def _():
        cnt[...] = jnp.zeros(cnt.shape, F32)

    tm = x1_ref.shape[0]
    n_parts = tm // ROW_TILE
    rp = tm // n_parts
    is_prompt = i < n_prompt_tiles
    ones = ones_ref[...]
    neg = jnp.float32(-jnp.inf)
    big = jnp.int32(LANES)
    lane = lax.broadcasted_iota(jnp.int32, (rp, LANES), 1)
    picked = {}

    def part(q):
        rs = slice(q * rp, (q + 1) * rp)
        y = jnp.where(is_prompt, yp_ref[rs, :], ys_ref[rs, :])
        inv_n = 1.0 / RWKV_HEAD
        mean = _split2_dot(y, ones) * inv_n
        yc = y - mean
        var = _split2_dot(yc * yc, ones) * inv_n
        yn = yc * lax.rsqrt(var + GN_EPS) * lng_ref[...] + lnb_ref[...]
        yy = (yn + bonus_ref[rs, :].astype(F32)) * g_ref[rs, :].astype(F32)
        yield
        y_b = jnp.dot(yy.astype(BF16), wob_ref[...], preferred_element_type=F32)
        merged = yag_ref[rs, :].astype(F32) + sgb_ref[rs, :].astype(F32) * y_b
        yield
        x = jnp.where(is_prompt, xp_ref[rs, :], xs_ref[rs, :])
        x1 = x + jnp.dot(merged.astype(BF16), wo_ref[...], preferred_element_type=F32)
        x1_ref[rs, :] = x1
        h2 = _rms(x1, n2_ref[...])
        h2_ref[rs, :] = _pack_bf16_pairs(h2)
        yield
        logits = _dot_stack3(h2, wr_ref[...]) + br_ref[...]
        yield
        is_g = lane < N_GROUPS
        lgp = jnp.where(is_g, logits, neg)
        m_g = jnp.max(lgp, axis=-1, keepdims=True)
        grp = jnp.min(jnp.where(lgp == m_g, lane, big), axis=-1, keepdims=True)
        p_top = 1.0 / jnp.sum(jnp.where(is_g, jnp.exp(logits - m_g), 0.0), axis=-1, keepdims=True)
        e_lane = lane - N_GROUPS
        in_grp = (e_lane >= grp * EXPERTS_PER_GROUP) & (e_lane < (grp + 1) * EXPERTS_PER_GROUP)
        le = jnp.where(in_grp, logits, neg)
        m1 = jnp.max(le, axis=-1, keepdims=True)
        i1 = jnp.min(jnp.where(le == m1, lane, big), axis=-1, keepdims=True)
        le2 = jnp.where(lane == i1, neg, le)
        m2 = jnp.max(le2, axis=-1, keepdims=True)
        i2 = jnp.min(jnp.where(le2 == m2, lane, big), axis=-1, keepdims=True)
        ex = jnp.exp(m2 - m1)
        picked[q] = (i1 - N_GROUPS, i2 - N_GROUPS, p_top / (1.0 + ex), p_top * ex / (1.0 + ex))

    gens = [part(q) for q in range(n_parts)]
    next(gens[0])
    _interleave(*gens)

    hot = [[lane == e for e in picked[q][0:2]] for q in range(n_parts)]
    both = jnp.concatenate([jnp.where(h1, 1.0, jnp.where(h2_, 1.0, 0.0)) for h1, h2_ in hot], axis=0)
    before = jnp.dot(tril_ref[...], both.astype(BF16), preferred_element_type=F32) + cnt[0:1, :]
    cnt[0:1, :] = cnt[0:1, :] + jnp.sum(both, axis=0, keepdims=True)
    counts_ref[...] = jnp.broadcast_to(cnt[0:1, :], counts_ref.shape)
    for q in range(n_parts):
        rs = slice(q * rp, (q + 1) * rp)
        e1, e2, w1, w2 = picked[q]
        ranks = [jnp.sum(jnp.where(h, before[rs, :], 0.0), axis=-1, keepdims=True) for h in hot[q]]
        route = jnp.zeros((rp, LANES), F32)
        for c, col in enumerate([e1.astype(F32), e2.astype(F32), w1, w2] + ranks):
            route = jnp.where(lane == c, col, route)
        route_ref[rs, :] = route


ROUTE_EXPERT, ROUTE_WEIGHT, ROUTE_RANK = 0, 2, 4


def _mixer_post(y_p, y_s, bonus, g, yag, sgb, x_p, x_s, p, *, tm):
    n_p, d = x_p.shape
    n_tok = n_p + x_s.shape[0]
    npt = n_p // tm
    row = lambda w: pl.BlockSpec((tm, w), lambda i: (i, 0))
    tril = jnp.asarray(np.arange(tm)[None, :] < np.arange(tm)[:, None], BF16)
    consts = [p['lnx_g'], p['lnx_b'], p['w_out_b'], p['w_o'], p['norm2_g'], p['w_router'],
              p['b_router'], p['ones_bf16'], tril]
    pair = lambda w: [pl.BlockSpec((tm, w), lambda i: (jnp.minimum(i, npt - 1), 0)),
                      pl.BlockSpec((tm, w), lambda i: (jnp.maximum(i - npt, 0), 0))]
    in_specs = (pair(RWKV_DIM) + [row(RWKV_DIM)] * 2 + [row(d)] * 2 + pair(d)
                + [_const_spec(c.shape) for c in consts])
    out_shape = [jax.ShapeDtypeStruct((n_tok, d), F32), jax.ShapeDtypeStruct((n_tok, d // 2), jnp.uint32),
                 jax.ShapeDtypeStruct((n_tok, LANES), F32), jax.ShapeDtypeStruct((8, LANES), F32)]
    out_specs = [row(d), row(d // 2), row(LANES), _const_spec((8, LANES))]
    return pl.pallas_call(
        functools.partial(_mixer_post_kernel, n_prompt_tiles=npt), out_shape=out_shape,
        grid=(n_tok // tm,), in_specs=in_specs, out_specs=out_specs,
        scratch_shapes=[pltpu.VMEM((8, LANES), F32)],
        compiler_params=pltpu.CompilerParams(dimension_semantics=("arbitrary",),
                                             vmem_limit_bytes=VMEM_LIMIT),
        name="mixer_post")(y_p, y_s, bonus, g, yag, sgb, x_p, x_s, *consts)


N_DMA_PRIORITIES = 2


def _gather_rows(idx_ref, src_hbm, dst, sem, n_rows, *, unrolled):
    def start(r, priority):
        pltpu.make_async_copy(src_hbm.at[pl.ds(idx_ref[r], 1)], dst.at[pl.ds(r, 1)],
                              sem).start(priority=priority)
    if unrolled:
        for r in range(n_rows):
            start(r, r % N_DMA_PRIORITIES)
    else:
        def body(r, carry):
            start(r, 0)
            return carry
        lax.fori_loop(0, n_rows, body, 0)


def _wait_rows(src_hbm, dst, sem, n_rows):
    pltpu.make_async_copy(src_hbm.at[pl.ds(0, n_rows)], dst, sem).wait()


def _moe_scatter_kernel(dest_ref, h_ref, xs_in_hbm, xs_hbm, sem):
    del xs_in_hbm
    n = dest_ref.shape[-1]
    tm = h_ref.shape[0]
    for r in range(n):
        pltpu.make_async_copy(h_ref.at[pl.ds(r % tm, 1)], xs_hbm.at[pl.ds(dest_ref[0, 0, r], 1)],
                              sem).start(priority=r % N_DMA_PRIORITIES)
    for j in range(n // tm):
        pltpu.make_async_copy(h_ref, xs_hbm.at[pl.ds(0, tm)], sem).wait()


def _moe_scatter(h2, pos3, n_rows):
    n_tok, d = h2.shape
    n_tiles, _, n = pos3.shape
    tm = n // TOP_K
    xs0 = jnp.zeros((n_rows, d), h2.dtype)
    return pl.pallas_call(
        _moe_scatter_kernel, out_shape=jax.ShapeDtypeStruct((n_rows, d), h2.dtype), grid=(n_tiles,),
        in_specs=[pl.BlockSpec((1, 1, n), lambda i: (i, 0, 0), memory_space=pltpu.SMEM),
                  pl.BlockSpec((tm, d), lambda i: (i, 0)), pl.BlockSpec(memory_space=pl.ANY)],
        out_specs=pl.BlockSpec(memory_space=pl.ANY),
        scratch_shapes=[pltpu.SemaphoreType.DMA(())], input_output_aliases={2: 0},
        compiler_params=pltpu.CompilerParams(dimension_semantics=("arbitrary",), has_side_effects=True,
                                             vmem_limit_bytes=VMEM_LIMIT),
        name="moe_scatter")(pos3, h2, xs0)


def _moe_experts_kernel(bexp_ref, nused_ref, slot_ref, nexte_ref, x_ref, wg_hbm, wu_hbm, wd_hbm, yb_ref,
                        wg_bf, wu_bf, wd_bf, wg_st, wu_st, wd_st, sems):
    i = pl.program_id(0)

    def fetch(e, slot):
        return [pltpu.make_async_copy(src.at[e], dst.at[slot], sems.at[slot, n])
                for n, (src, dst) in enumerate(((wg_hbm, wg_st), (wu_hbm, wu_st), (wd_hbm, wd_st)))]

    @pl.when(i == 0)
    def _():
        for cp in fetch(bexp_ref[0], 0):
            cp.start()

    @pl.when((i == 0) | (bexp_ref[i] != bexp_ref[jnp.maximum(i - 1, 0)]))
    def _():
        slot = slot_ref[i]
        for cp in fetch(bexp_ref[i], slot):
            cp.wait()
        wg_bf[...] = wg_st[slot].astype(BF16)
        wu_bf[...] = wu_st[slot].astype(BF16)
        wd_bf[...] = wd_st[slot].astype(BF16)

        @pl.when(nexte_ref[i] >= 0)
        def _():
            for cp in fetch(nexte_ref[i], 1 - slot):
                cp.start()

    @pl.when(i < nused_ref[0])
    def _():
        def part(rs):
            xb = _unpack_bf16_pairs(x_ref[rs, :]).astype(BF16)
            hg = jnp.dot(xb, wg_bf[...], preferred_element_type=F32)
            yield
            hu = jnp.dot(xb, wu_bf[...], preferred_element_type=F32)
            hid = (hg * jax.nn.sigmoid(hg)) * hu
            yield
            yb_ref[rs, :] = _pack_bf16_pairs(jnp.dot(hid.astype(BF16), wd_bf[...], preferred_element_type=F32))

        rows = x_ref.shape[0]
        _interleave(*[part(slice(r0, r0 + ROW_TILE)) for r0 in range(0, rows, ROW_TILE)])

    @pl.when(i >= nused_ref[0])
    def _():
        yb_ref[...] = jnp.zeros(yb_ref.shape, yb_ref.dtype)


def _moe_experts(xs, block_expert, n_used, w_eg, w_eu, w_ed, *, blk):
    n_rows, dp = xs.shape
    n_blocks = n_rows // blk
    d, de = w_eg.shape[1:]
    changed = jnp.concatenate([jnp.zeros((1,), jnp.int32),
                               (block_expert[1:] != block_expert[:-1]).astype(jnp.int32)])
    slot = jnp.cumsum(changed) % 2
    first_greater = jnp.sum(block_expert[None, :] <= block_expert[:, None], axis=1)
    next_e = jnp.where(first_greater < n_blocks,
                       block_expert[jnp.minimum(first_greater, n_blocks - 1)], -1).astype(jnp.int32)
    hbm = pl.BlockSpec(memory_space=pl.ANY)
    grid_spec = pltpu.PrefetchScalarGridSpec(
        num_scalar_prefetch=4, grid=(n_blocks,),
        in_specs=[
            pl.BlockSpec((blk, dp), lambda i, be, nu, sl, ne: (jnp.minimum(i, nu[0] - 1), 0)),
            hbm, hbm, hbm,
        ],
        out_specs=pl.BlockSpec((blk, dp), lambda i, be, nu, sl, ne: (i, 0)),
        scratch_shapes=[pltpu.VMEM((d, de), BF16), pltpu.VMEM((d, de), BF16), pltpu.VMEM((de, d), BF16),
                        pltpu.VMEM((2, d, de), F32), pltpu.VMEM((2, d, de), F32), pltpu.VMEM((2, de, d), F32),
                        pltpu.SemaphoreType.DMA((2, 3))])
    return pl.pallas_call(
        _moe_experts_kernel, out_shape=jax.ShapeDtypeStruct((n_rows, dp), xs.dtype), grid_spec=grid_spec,
        compiler_params=pltpu.CompilerParams(dimension_semantics=("arbitrary",),
                                             vmem_limit_bytes=VMEM_LIMIT),
        name="moe_experts")(block_expert, n_used, slot.astype(jnp.int32), next_e, xs, w_eg, w_eu, w_ed)


GATHER_AHEAD = 2


def _with_dummy_blocks(idx, n_blocks, blk):
    pad = jnp.zeros((GATHER_AHEAD * blk,), idx.dtype)
    return jnp.concatenate([idx, pad]).reshape(n_blocks + GATHER_AHEAD, 1, blk)


def _moe_combine_kernel(*refs, n_prompt_tiles):
    pos_refs = refs[:GATHER_AHEAD + 1]
    x1_ref, route_ref, nf_ref, yb_hbm, outp_ref, outs_ref, ybuf, sems = refs[GATHER_AHEAD + 1:]
    i = pl.program_id(0)
    n_slots = GATHER_AHEAD + 1
    rows = ybuf.shape[1]
    slot = i % n_slots

    @pl.when(i == 0)
    def _():
        for a in range(GATHER_AHEAD):
            _gather_rows(pos_refs[a].at[0, 0], yb_hbm, ybuf.at[a], sems.at[a], rows, unrolled=False)

    _wait_rows(yb_hbm, ybuf.at[slot], sems.at[slot], rows)
    ahead = (i + GATHER_AHEAD) % n_slots
    _gather_rows(pos_refs[GATHER_AHEAD].at[0, 0], yb_hbm, ybuf.at[ahead], sems.at[ahead], rows, unrolled=True)
    tm = x1_ref.shape[0]
    route = route_ref[...]
    x2 = x1_ref[...]
    for j in range(TOP_K):
        yj = _unpack_bf16_pairs(ybuf[slot, j * tm:(j + 1) * tm, :])
        x2 = x2 + yj * route[:, ROUTE_WEIGHT + j:ROUTE_WEIGHT + j + 1]
    out = _rms(x2, nf_ref[...])

    @pl.when(i < n_prompt_tiles)
    def _():
        outp_ref[...] = out

    @pl.when(i >= n_prompt_tiles)
    def _():
        outs_ref[...] = out

    @pl.when(i == pl.num_programs(0) - 1)
    def _():
        for a in range(1, n_slots):
            s = (i + a) % n_slots
            _wait_rows(yb_hbm, ybuf.at[s], sems.at[s], rows)


def _moe_combine(pos_tiles, x1, route, normf_g, yb, *, n_prompt_rows, tm):
    n_tok, d = x1.shape
    n_tiles = n_tok // tm
    npt = n_prompt_rows // tm
    pos3 = _with_dummy_blocks(pos_tiles.reshape(-1), n_tiles, TOP_K * tm)
    pos_spec = lambda a: pl.BlockSpec((1, 1, TOP_K * tm), lambda i: (i + a, 0, 0), memory_space=pltpu.SMEM)
    in_specs = [pos_spec(a) for a in range(GATHER_AHEAD + 1)] + [
        pl.BlockSpec((tm, d), lambda i: (i, 0)),
        pl.BlockSpec((tm, LANES), lambda i: (i, 0)),
        _const_spec(normf_g.shape),
        pl.BlockSpec(memory_space=pl.ANY),
    ]
    out_shape = [jax.ShapeDtypeStruct((n_prompt_rows, d), F32),
                 jax.ShapeDtypeStruct((n_tok - n_prompt_rows, d), F32)]
    out_specs = [pl.BlockSpec((tm, d), lambda i: (jnp.minimum(i, npt - 1), 0)),
                 pl.BlockSpec((tm, d), lambda i: (jnp.maximum(i - npt, 0), 0))]
    return pl.pallas_call(
        functools.partial(_moe_combine_kernel, n_prompt_tiles=npt), out_shape=out_shape,
        grid=(n_tiles,), in_specs=in_specs, out_specs=out_specs,
        scratch_shapes=[pltpu.VMEM((GATHER_AHEAD + 1, TOP_K * tm, yb.shape[1]), yb.dtype),
                        pltpu.SemaphoreType.DMA((GATHER_AHEAD + 1,))],
        compiler_params=pltpu.CompilerParams(dimension_semantics=("arbitrary",),
                                             vmem_limit_bytes=VMEM_LIMIT),
        name="moe_combine")(*([pos3] * (GATHER_AHEAD + 1)), x1, route, normf_g, yb)


def _dispatch(route, counts, blk, tm):
    n_tok = route.shape[0]
    n_assign = n_tok * TOP_K
    expert = route[:, ROUTE_EXPERT:ROUTE_EXPERT + TOP_K].astype(jnp.int32)
    rank = route[:, ROUTE_RANK:ROUTE_RANK + TOP_K].astype(jnp.int32)
    counts = counts[0, :N_EXPERTS].astype(jnp.int32)
    padded = (counts + blk - 1) // blk * blk
    pad_end = jnp.cumsum(padded)
    pad_start = pad_end - padded
    is_e = expert[:, :, None] == jnp.arange(N_EXPERTS, dtype=jnp.int32)
    dest = jnp.sum(jnp.where(is_e, pad_start, 0), axis=-1) + rank
    n_blocks = -(-n_assign // blk) + N_EXPERTS
    block_start = jnp.arange(n_blocks, dtype=jnp.int32) * blk
    block_expert = jnp.minimum(jnp.sum(block_start[:, None] >= pad_end[None, :], axis=1),
                               N_EXPERTS - 1).astype(jnp.int32)
    n_used = (pad_end[-1] // blk).reshape(1)
    pos_tiles = [dest.reshape(n_tok // t, t, TOP_K).transpose(0, 2, 1).reshape(n_tok // t, 1, TOP_K * t)
                 for t in tm]
    return pos_tiles, block_expert, n_used, n_blocks


def _state_to_kernel(s):
    return s.reshape(s.shape[0], N_HEAD_GROUPS, GROUP_LANES, RWKV_HEAD)


def _state_from_kernel(s):
    return s.reshape(s.shape[0], RWKV_HEADS, RWKV_HEAD, RWKV_HEAD)


def kernel(x_prompt, x_sample, state_conv, state_shift, state_wkv, norm1_g, w_in, conv_w, mu_shift, w0, w_lora_w, a0, w_lora_a, w_lora_g, k_k, k_a, r_k, lnx_g, lnx_b, w_out_a, w_out_b, w_o, norm2_g, w_router_group, b_router_group, w_router_expert, b_router_expert, w_e_gate, w_e_up, w_e_down, normf_g):
    depth = norm1_g.shape[0]
    bp, seq, d = x_prompt.shape
    db, dseq, _ = x_sample.shape
    assert depth == 1 and bp == 1, "single layer, single prompt stream"
    tm = ROW_TILE
    n_p, n_s = bp * seq, db * dseq
    assert n_p % tm == 0 and n_s % tm == 0 and tm % dseq == 0
    n_prompt_tiles, n_sample_tiles = n_p // tm, n_s // tm
    seqs = tm // dseq
    n_tok = n_p + n_s

    x_p = x_prompt.reshape(n_p, d)
    x_s = x_sample.reshape(n_s, d)

    l = 0
    c3 = 3 * CONV_DIM
    head_id = np.arange(GROUP_LANES) // RWKV_HEAD
    ones_bf16 = jnp.asarray(head_id[:, None] == head_id[None, :], BF16)
    zpad = jnp.zeros((LORA_W, RWKV_DIM), F32)
    n_r = N_GROUPS + N_EXPERTS
    p = {
        'norm1_g': norm1_g[l].reshape(1, d),
        'w_in_a': w_in[l][:, :c3].astype(BF16),
        'w_in_b': w_in[l][:, c3:c3 + SHIFT_DIM].astype(BF16),
        'w_in_g': w_in[l][:, c3 + SHIFT_DIM:].astype(BF16),
        'conv_w': conv_w[l],
        'mu_shift': mu_shift[l].reshape(1, SHIFT_DIM),
        'w0': w0[l].reshape(1, RWKV_DIM),
        'w_lora_w': _stack3(jnp.concatenate([w_lora_w[l], zpad], axis=0)),
        'a0': a0[l].reshape(1, RWKV_DIM),
        'w_lora_a': _stack3(jnp.concatenate([zpad, w_lora_a[l]], axis=0)),
        'w_lora_g': _stack3(w_lora_g[l]),
        'k_k': k_k[l].reshape(1, RWKV_DIM),
        'k_a': k_a[l].reshape(1, RWKV_DIM),
        'r_k': r_k[l].reshape(1, RWKV_DIM),
        'w_out_a': w_out_a[l].astype(BF16),
        'ones_bf16': ones_bf16,
        'lnx_g': lnx_g[l].reshape(1, RWKV_DIM),
        'lnx_b': lnx_b[l].reshape(1, RWKV_DIM),
        'w_out_b': w_out_b[l].astype(BF16),
        'w_o': w_o[l].astype(BF16),
        'norm2_g': norm2_g[l].reshape(1, d),
        'w_router': _stack3(jnp.pad(jnp.concatenate([w_router_group[l], w_router_expert[l]], axis=1),
                                    ((0, 0), (0, LANES - n_r)))),
        'b_router': jnp.pad(jnp.concatenate([b_router_group[l], b_router_expert[l]]),
                            (0, LANES - n_r)).reshape(1, LANES),
    }
    st_conv_t = state_conv[l].reshape(n_sample_tiles, seqs * (CONV_WIDTH - 1), CONV_DIM)
    st_shift_t = state_shift[l].reshape(n_sample_tiles, seqs, SHIFT_DIM)

    (yag, sgb, r, w, k, v, a, b, bonus, g, ctail, stail) = _mixer_pre(
        x_p, x_s, st_conv_t, st_shift_t, p, seq_len=dseq, tm=tm)

    s0_prompt = jnp.zeros((1, N_HEAD_GROUPS, GROUP_LANES, RWKV_HEAD), F32)
    s0_sample = _state_to_kernel(state_wkv[l])
    rlkvab = (r, w, k, v, a, b)
    cp = min(SCAN_CHUNK, n_p)
    assert cp & (cp - 1) == 0 and dseq & (dseq - 1) == 0
    y_p, s_p = _wkv_scan(rlkvab, s0_prompt, row0=0, n_rows=n_p, chunk=cp,
                         chunks_per_step=SCAN_CHUNKS_PER_STEP, chained=True)
    y_s, s_s = _wkv_scan(rlkvab, s0_sample, row0=n_p, n_rows=n_s, chunk=dseq,
                         chunks_per_step=SCAN_CHUNKS_PER_STEP, chained=False)

    post_tm = POST_TILES * tm
    assert n_p % post_tm == 0 and n_s % post_tm == 0
    x1, h2, route, counts = _mixer_post(y_p, y_s, bonus, g, yag, sgb, x_p, x_s, p, tm=post_tm)

    blk = MOE_ROWS
    (pos_scatter, pos_combine), block_expert, n_used, n_blocks = _dispatch(route, counts, blk, (post_tm, tm))
    xs = _moe_scatter(h2, pos_scatter, n_blocks * blk)
    yb = _moe_experts(xs, block_expert, n_used, w_e_gate[l], w_e_up[l], w_e_down[l], blk=blk)
    out_p, out_s = _moe_combine(pos_combine, x1, route, normf_g.reshape(1, d), yb, n_prompt_rows=n_p, tm=tm)

    y_prompt = out_p.reshape(bp, seq, d)
    y_sample = out_s.reshape(db, dseq, d)
    conv_p = ctail[n_prompt_tiles - 1, 2 * (seqs - 1):2 * seqs].reshape(1, bp, CONV_WIDTH - 1, CONV_DIM)
    shift_p = stail[n_prompt_tiles - 1, seqs - 1].reshape(1, bp, 1, SHIFT_DIM)
    wkv_p = _state_from_kernel(s_p).reshape(1, bp, RWKV_HEADS, RWKV_HEAD, RWKV_HEAD)
    conv_s = ctail[n_prompt_tiles:].reshape(1, db, CONV_WIDTH - 1, CONV_DIM)
    shift_s = stail[n_prompt_tiles:].reshape(1, db, 1, SHIFT_DIM)
    wkv_s = _state_from_kernel(s_s).reshape(1, db, RWKV_HEADS, RWKV_HEAD, RWKV_HEAD)
    return (y_prompt, y_sample, conv_p, shift_p, wkv_p, conv_s, shift_s, wkv_s)
```
